```python
import math
import jax, jax.numpy as jnp
from jax import lax
import numpy as np

D_MODEL = 2048
BATCH = 8
SEQ = 2048
DEPTH = 1

SSM_WIDTH = D_MODEL // 2
SSM_GROUP = 16
SSM_GROUPS = SSM_WIDTH // SSM_GROUP
SSM_STATE = 64
DT_MIN = 1e-3
DT_MAX = 1e-1
CONV_WIDTH = D_MODEL // 2
CONV_K = 3
D_FF = 5504
EPS = 1e-6
IN_COLS = SSM_WIDTH + 3 * CONV_WIDTH + 2 * D_MODEL

kernel_name = "hybrid_s5_shortconv_macaron_block"


def _rmsnorm(x, g):
    xf = x.astype(jnp.float32)
    y = xf * lax.rsqrt(jnp.mean(xf * xf, axis=-1, keepdims=True) + EPS)
    return (y * g.astype(jnp.float32)).astype(x.dtype)


def _swiglu(x, w_gate, w_up, w_down):
    return (jax.nn.silu(x @ w_gate) * (x @ w_up)) @ w_down


def _s5_discretize(lam_re, lam_im, log_dt, b_re, b_im):
    lam_re = jnp.minimum(lam_re, -1e-4)
    dt = jnp.exp(log_dt)[:, None]
    mag = jnp.exp(lam_re * dt)
    a_re = mag * jnp.cos(lam_im * dt)
    a_im = mag * jnp.sin(lam_im * dt)
    den = lam_re * lam_re + lam_im * lam_im
    p = a_re - 1.0
    f_re = (p * lam_re + a_im * lam_im) / den
    f_im = (a_im * lam_re - p * lam_im) / den
    f_re = f_re[:, :, None]
    f_im = f_im[:, :, None]
    bb_re = f_re * b_re - f_im * b_im
    bb_im = f_re * b_im + f_im * b_re
    return a_re, a_im, bb_re, bb_im


def _ssm_combine(left, right):
    a1r, a1i, b1r, b1i = left
    a2r, a2i, b2r, b2i = right
    ar = a2r * a1r - a2i * a1i
    ai = a2r * a1i + a2i * a1r
    br = a2r * b1r - a2i * b1i + b2r
    bi = a2r * b1i + a2i * b1r + b2i
    return ar, ai, br, bi


def _s5_branch(v, lam_re, lam_im, log_dt, b_re, b_im, c_re, c_im, d_skip, w_glu, b_glu):
    bsz, seq, _ = v.shape
    vf = v.astype(jnp.float32).reshape(bsz, seq, SSM_GROUPS, SSM_GROUP)
    a_re, a_im, bb_re, bb_im = _s5_discretize(lam_re, lam_im, log_dt, b_re, b_im)
    bu_re = jnp.einsum('bsgc,gnc->bsgn', vf, bb_re)
    bu_im = jnp.einsum('bsgc,gnc->bsgn', vf, bb_im)
    shp = (1, seq, SSM_GROUPS, SSM_STATE)
    ar = jnp.broadcast_to(a_re[None, None], shp)
    ai = jnp.broadcast_to(a_im[None, None], shp)
    _, _, s_re, s_im = lax.associative_scan(_ssm_combine, (ar, ai, bu_re, bu_im), axis=1)
    y = (jnp.einsum('bsgn,gcn->bsgc', s_re, c_re)
         - jnp.einsum('bsgn,gcn->bsgc', s_im, c_im))
    y = y.reshape(bsz, seq, SSM_WIDTH) + d_skip * vf.reshape(bsz, seq, SSM_WIDTH)
    y = jax.nn.gelu(y)
    y = y * jax.nn.sigmoid(y @ w_glu + b_glu)
    return y.astype(v.dtype)


def _short_conv_branch(b_gate, c_gate, val, conv_w, conv_b):
    seq = val.shape[1]
    z = c_gate * val
    zp = jnp.pad(z, ((0, 0), (CONV_K - 1, 0), (0, 0)))
    conv = conv_b + sum(conv_w[k] * zp[:, k:k + seq] for k in range(CONV_K))
    return b_gate * conv


def _fwd_setup_inputs(seed: int = 0) -> dict:
    key = jax.random.key(seed)
    ks = jax.random.split(key, 32)
    f32 = jnp.float32
    D, F, W, CW, G, N, C = D_MODEL, D_FF, SSM_WIDTH, CONV_WIDTH, SSM_GROUPS, SSM_STATE, SSM_GROUP

    def nrm(k, shape, scale):
        return jax.random.normal(k, shape, f32) * scale

    def gain(k, n):
        return 1.0 + 0.02 * jax.random.normal(k, (n,), f32)

    return {
        "x": jax.random.normal(ks[0], (BATCH, SEQ, D), f32),
        "ffn1_norm": gain(ks[1], D),
        "ffn1_w_gate": nrm(ks[2], (D, F), D ** -0.5),
        "ffn1_w_up": nrm(ks[3], (D, F), D ** -0.5),
        "ffn1_w_down": nrm(ks[4], (F, D), F ** -0.5),
        "mix_norm": gain(ks[5], D),
        "w_in": nrm(ks[6], (D, IN_COLS), D ** -0.5),
        "ssm_lambda_re": -0.5 + 0.01 * jax.random.normal(ks[7], (G, N), f32),
        "ssm_lambda_im": math.pi * jnp.broadcast_to(jnp.arange(N, dtype=f32), (G, N))
                         + 0.01 * jax.random.normal(ks[8], (G, N), f32),
        "ssm_log_dt": jax.random.uniform(ks[9], (G,), f32, math.log(DT_MIN), math.log(DT_MAX)),
        "ssm_b_re": nrm(ks[10], (G, N, C), (2 * C) ** -0.5),
        "ssm_b_im": nrm(ks[11], (G, N, C), (2 * C) ** -0.5),
        "ssm_c_re": nrm(ks[12], (G, C, N), N ** -0.5),
        "ssm_c_im": nrm(ks[13], (G, C, N), N ** -0.5),
        "ssm_d": nrm(ks[14], (W,), 1.0),
        "ssm_w_glu": nrm(ks[15], (W, W), W ** -0.5),
        "ssm_b_glu": nrm(ks[16], (W,), 0.01),
        "ssm_w_out": nrm(ks[17], (W, D), W ** -0.5),
        "conv_w": nrm(ks[18], (CONV_K, CW), CONV_K ** -0.5),
        "conv_b": nrm(ks[19], (CW,), 0.01),
        "conv_w_out": nrm(ks[20], (CW, D), CW ** -0.5),
        "w_o": nrm(ks[21], (D, D), D ** -0.5),
        "ffn2_norm": gain(ks[22], D),
        "ffn2_w_gate": nrm(ks[23], (D, F), D ** -0.5),
        "ffn2_w_up": nrm(ks[24], (D, F), D ** -0.5),
        "ffn2_w_down": nrm(ks[25], (F, D), F ** -0.5),
        "final_norm": gain(ks[26], D),
    }


def _fwd_reference(x, ffn1_norm, ffn1_w_gate, ffn1_w_up, ffn1_w_down, mix_norm, w_in,
              ssm_lambda_re, ssm_lambda_im, ssm_log_dt, ssm_b_re, ssm_b_im, ssm_c_re, ssm_c_im,
              ssm_d, ssm_w_glu, ssm_b_glu, ssm_w_out, conv_w, conv_b, conv_w_out, w_o,
              ffn2_norm, ffn2_w_gate, ffn2_w_up, ffn2_w_down, final_norm):
    h = x
    for _ in range(DEPTH):
        h = h + 0.5 * _swiglu(_rmsnorm(h, ffn1_norm), ffn1_w_gate, ffn1_w_up, ffn1_w_down)
        u = _rmsnorm(h, mix_norm)
        proj = u @ w_in
        splits = [SSM_WIDTH, SSM_WIDTH + CONV_WIDTH, SSM_WIDTH + 2 * CONV_WIDTH,
                  SSM_WIDTH + 3 * CONV_WIDTH, SSM_WIDTH + 3 * CONV_WIDTH + D_MODEL]
        v_ssm, b_gate, c_gate, val, ga_pre, gb_pre = jnp.split(proj, splits, axis=-1)
        y_a = _s5_branch(v_ssm, ssm_lambda_re, ssm_lambda_im, ssm_log_dt, ssm_b_re, ssm_b_im,
                         ssm_c_re, ssm_c_im, ssm_d, ssm_w_glu, ssm_b_glu)
        y_b = _short_conv_branch(b_gate, c_gate, val, conv_w, conv_b)
        z_a = y_a @ ssm_w_out
        z_b = y_b @ conv_w_out
        merged = jax.nn.sigmoid(ga_pre) * z_a + jax.nn.sigmoid(gb_pre) * z_b
        h = h + merged @ w_o
        h = h + 0.5 * _swiglu(_rmsnorm(h, ffn2_norm), ffn2_w_gate, ffn2_w_up, ffn2_w_down)
    return _rmsnorm(h, final_norm)


import jax as _jax
import jax.numpy as _jnp

TWIN_FORMAT = 'train_step'
FWD_PARAMS = ['x', 'ffn1_norm', 'ffn1_w_gate', 'ffn1_w_up', 'ffn1_w_down', 'mix_norm', 'w_in', 'ssm_lambda_re', 'ssm_lambda_im', 'ssm_log_dt', 'ssm_b_re', 'ssm_b_im', 'ssm_c_re', 'ssm_c_im', 'ssm_d', 'ssm_w_glu', 'ssm_b_glu', 'ssm_w_out', 'conv_w', 'conv_b', 'conv_w_out', 'w_o', 'ffn2_norm', 'ffn2_w_gate', 'ffn2_w_up', 'ffn2_w_down', 'final_norm']
TWIN_WEIGHTS = ['ffn1_norm', 'ffn1_w_gate', 'ffn1_w_up', 'ffn1_w_down', 'mix_norm', 'w_in', 'ssm_lambda_re', 'ssm_lambda_im', 'ssm_log_dt', 'ssm_b_re', 'ssm_b_im', 'ssm_c_re', 'ssm_c_im', 'ssm_d', 'ssm_w_glu', 'ssm_b_glu', 'ssm_w_out', 'conv_w', 'conv_b', 'conv_w_out', 'w_o', 'ffn2_norm', 'ffn2_w_gate', 'ffn2_w_up', 'ffn2_w_down', 'final_norm']
TWIN_DIFF_INPUT = 'x'
TWIN_INPUTS = ['x', 'ffn1_norm', 'ffn1_w_gate', 'ffn1_w_up', 'ffn1_w_down', 'mix_norm', 'w_in', 'ssm_lambda_re', 'ssm_lambda_im', 'ssm_log_dt', 'ssm_b_re', 'ssm_b_im', 'ssm_c_re', 'ssm_c_im', 'ssm_d', 'ssm_w_glu', 'ssm_b_glu', 'ssm_w_out', 'conv_w', 'conv_b', 'conv_w_out', 'w_o', 'ffn2_norm', 'ffn2_w_gate', 'ffn2_w_up', 'ffn2_w_down', 'final_norm', 'loss_target', 'm_ffn1_norm', 'm_ffn1_w_gate', 'm_ffn1_w_up', 'm_ffn1_w_down', 'm_mix_norm', 'm_w_in', 'm_ssm_lambda_re', 'm_ssm_lambda_im', 'm_ssm_log_dt', 'm_ssm_b_re', 'm_ssm_b_im', 'm_ssm_c_re', 'm_ssm_c_im', 'm_ssm_d', 'm_ssm_w_glu', 'm_ssm_b_glu', 'm_ssm_w_out', 'm_conv_w', 'm_conv_b', 'm_conv_w_out', 'm_w_o', 'm_ffn2_norm', 'm_ffn2_w_gate', 'm_ffn2_w_up', 'm_ffn2_w_down', 'm_final_norm', 'v_ffn1_norm', 'v_ffn1_w_gate', 'v_ffn1_w_up', 'v_ffn1_w_down', 'v_mix_norm', 'v_w_in', 'v_ssm_lambda_re', 'v_ssm_lambda_im', 'v_ssm_log_dt', 'v_ssm_b_re', 'v_ssm_b_im', 'v_ssm_c_re', 'v_ssm_c_im', 'v_ssm_d', 'v_ssm_w_glu', 'v_ssm_b_glu', 'v_ssm_w_out', 'v_conv_w', 'v_conv_b', 'v_conv_w_out', 'v_w_o', 'v_ffn2_norm', 'v_ffn2_w_gate', 'v_ffn2_w_up', 'v_ffn2_w_down', 'v_final_norm']
TWIN_OUTPUTS = ['loss', 'grad_x', 'grad_ffn1_norm', 'grad_ffn1_w_gate', 'grad_ffn1_w_up', 'grad_ffn1_w_down', 'grad_mix_norm', 'grad_w_in', 'grad_ssm_lambda_re', 'grad_ssm_lambda_im', 'grad_ssm_log_dt', 'grad_ssm_b_re', 'grad_ssm_b_im', 'grad_ssm_c_re', 'grad_ssm_c_im', 'grad_ssm_d', 'grad_ssm_w_glu', 'grad_ssm_b_glu', 'grad_ssm_w_out', 'grad_conv_w', 'grad_conv_b', 'grad_conv_w_out', 'grad_w_o', 'grad_ffn2_norm', 'grad_ffn2_w_gate', 'grad_ffn2_w_up', 'grad_ffn2_w_down', 'grad_final_norm', 'delta_ffn1_norm', 'delta_ffn1_w_gate', 'delta_ffn1_w_up', 'delta_ffn1_w_down', 'delta_mix_norm', 'delta_w_in', 'delta_ssm_lambda_re', 'delta_ssm_lambda_im', 'delta_ssm_log_dt', 'delta_ssm_b_re', 'delta_ssm_b_im', 'delta_ssm_c_re', 'delta_ssm_c_im', 'delta_ssm_d', 'delta_ssm_w_glu', 'delta_ssm_b_glu', 'delta_ssm_w_out', 'delta_conv_w', 'delta_conv_b', 'delta_conv_w_out', 'delta_w_o', 'delta_ffn2_norm', 'delta_ffn2_w_gate', 'delta_ffn2_w_up', 'delta_ffn2_w_down', 'delta_final_norm', 'new_m_ffn1_norm', 'new_m_ffn1_w_gate', 'new_m_ffn1_w_up', 'new_m_ffn1_w_down', 'new_m_mix_norm', 'new_m_w_in', 'new_m_ssm_lambda_re', 'new_m_ssm_lambda_im', 'new_m_ssm_log_dt', 'new_m_ssm_b_re', 'new_m_ssm_b_im', 'new_m_ssm_c_re', 'new_m_ssm_c_im', 'new_m_ssm_d', 'new_m_ssm_w_glu', 'new_m_ssm_b_glu', 'new_m_ssm_w_out', 'new_m_conv_w', 'new_m_conv_b', 'new_m_conv_w_out', 'new_m_w_o', 'new_m_ffn2_norm', 'new_m_ffn2_w_gate', 'new_m_ffn2_w_up', 'new_m_ffn2_w_down', 'new_m_final_norm', 'new_v_ffn1_norm', 'new_v_ffn1_w_gate', 'new_v_ffn1_w_up', 'new_v_ffn1_w_down', 'new_v_mix_norm', 'new_v_w_in', 'new_v_ssm_lambda_re', 'new_v_ssm_lambda_im', 'new_v_ssm_log_dt', 'new_v_ssm_b_re', 'new_v_ssm_b_im', 'new_v_ssm_c_re', 'new_v_ssm_c_im', 'new_v_ssm_d', 'new_v_ssm_w_glu', 'new_v_ssm_b_glu', 'new_v_ssm_w_out', 'new_v_conv_w', 'new_v_conv_b', 'new_v_conv_w_out', 'new_v_w_o', 'new_v_ffn2_norm', 'new_v_ffn2_w_gate', 'new_v_ffn2_w_up', 'new_v_ffn2_w_down', 'new_v_final_norm']
TWIN_LEAF_KINDS = {'loss': 'loss', 'grad_x': 'grad_x', 'grad_ffn1_norm': 'grad_w', 'grad_ffn1_w_gate': 'grad_w', 'grad_ffn1_w_up': 'grad_w', 'grad_ffn1_w_down': 'grad_w', 'grad_mix_norm': 'grad_w', 'grad_w_in': 'grad_w', 'grad_ssm_lambda_re': 'grad_w', 'grad_ssm_lambda_im': 'grad_w', 'grad_ssm_log_dt': 'grad_w', 'grad_ssm_b_re': 'grad_w', 'grad_ssm_b_im': 'grad_w', 'grad_ssm_c_re': 'grad_w', 'grad_ssm_c_im': 'grad_w', 'grad_ssm_d': 'grad_w', 'grad_ssm_w_glu': 'grad_w', 'grad_ssm_b_glu': 'grad_w', 'grad_ssm_w_out': 'grad_w', 'grad_conv_w': 'grad_w', 'grad_conv_b': 'grad_w', 'grad_conv_w_out': 'grad_w', 'grad_w_o': 'grad_w', 'grad_ffn2_norm': 'grad_w', 'grad_ffn2_w_gate': 'grad_w', 'grad_ffn2_w_up': 'grad_w', 'grad_ffn2_w_down': 'grad_w', 'grad_final_norm': 'grad_w', 'delta_ffn1_norm': 'delta_w', 'delta_ffn1_w_gate': 'delta_w', 'delta_ffn1_w_up': 'delta_w', 'delta_ffn1_w_down': 'delta_w', 'delta_mix_norm': 'delta_w', 'delta_w_in': 'delta_w', 'delta_ssm_lambda_re': 'delta_w', 'delta_ssm_lambda_im': 'delta_w', 'delta_ssm_log_dt': 'delta_w', 'delta_ssm_b_re': 'delta_w', 'delta_ssm_b_im': 'delta_w', 'delta_ssm_c_re': 'delta_w', 'delta_ssm_c_im': 'delta_w', 'delta_ssm_d': 'delta_w', 'delta_ssm_w_glu': 'delta_w', 'delta_ssm_b_glu': 'delta_w', 'delta_ssm_w_out': 'delta_w', 'delta_conv_w': 'delta_w', 'delta_conv_b': 'delta_w', 'delta_conv_w_out': 'delta_w', 'delta_w_o': 'delta_w', 'delta_ffn2_norm': 'delta_w', 'delta_ffn2_w_gate': 'delta_w', 'delta_ffn2_w_up': 'delta_w', 'delta_ffn2_w_down': 'delta_w', 'delta_final_norm': 'delta_w', 'new_m_ffn1_norm': 'new_m', 'new_m_ffn1_w_gate': 'new_m', 'new_m_ffn1_w_up': 'new_m', 'new_m_ffn1_w_down': 'new_m', 'new_m_mix_norm': 'new_m', 'new_m_w_in': 'new_m', 'new_m_ssm_lambda_re': 'new_m', 'new_m_ssm_lambda_im': 'new_m', 'new_m_ssm_log_dt': 'new_m', 'new_m_ssm_b_re': 'new_m', 'new_m_ssm_b_im': 'new_m', 'new_m_ssm_c_re': 'new_m', 'new_m_ssm_c_im': 'new_m', 'new_m_ssm_d': 'new_m', 'new_m_ssm_w_glu': 'new_m', 'new_m_ssm_b_glu': 'new_m', 'new_m_ssm_w_out': 'new_m', 'new_m_conv_w': 'new_m', 'new_m_conv_b': 'new_m', 'new_m_conv_w_out': 'new_m', 'new_m_w_o': 'new_m', 'new_m_ffn2_norm': 'new_m', 'new_m_ffn2_w_gate': 'new_m', 'new_m_ffn2_w_up': 'new_m', 'new_m_ffn2_w_down': 'new_m', 'new_m_final_norm': 'new_m', 'new_v_ffn1_norm': 'new_v', 'new_v_ffn1_w_gate': 'new_v', 'new_v_ffn1_w_up': 'new_v', 'new_v_ffn1_w_down': 'new_v', 'new_v_mix_norm': 'new_v', 'new_v_w_in': 'new_v', 'new_v_ssm_lambda_re': 'new_v', 'new_v_ssm_lambda_im': 'new_v', 'new_v_ssm_log_dt': 'new_v', 'new_v_ssm_b_re': 'new_v', 'new_v_ssm_b_im': 'new_v', 'new_v_ssm_c_re': 'new_v', 'new_v_ssm_c_im': 'new_v', 'new_v_ssm_d': 'new_v', 'new_v_ssm_w_glu': 'new_v', 'new_v_ssm_b_glu': 'new_v', 'new_v_ssm_w_out': 'new_v', 'new_v_conv_w': 'new_v', 'new_v_conv_b': 'new_v', 'new_v_conv_w_out': 'new_v', 'new_v_w_o': 'new_v', 'new_v_ffn2_norm': 'new_v', 'new_v_ffn2_w_gate': 'new_v', 'new_v_ffn2_w_up': 'new_v', 'new_v_ffn2_w_down': 'new_v', 'new_v_final_norm': 'new_v'}


def _forward(args):
    return _fwd_reference(*[args[k] for k in FWD_PARAMS])


def _output_shape():
    out = _jax.eval_shape(lambda: _forward(_fwd_setup_inputs(0)))
    return out.shape, out.dtype

N_MICROBATCH = 1
ADAM_LR = 0.001
ADAM_B1 = 0.9
ADAM_B2 = 0.999
ADAM_EPS = 1e-08
ADAM_WD = 0.01
ADAM_STEP = 10
PER_EXAMPLE_BATCH_AXIS = {'x': 0, 'loss_target': 0}
SHARED_INPUTS = []
_WEIGHT_DTYPES = {'ffn1_norm': _jnp.float32, 'ffn1_w_gate': _jnp.float32, 'ffn1_w_up': _jnp.float32, 'ffn1_w_down': _jnp.float32, 'mix_norm': _jnp.float32, 'w_in': _jnp.float32, 'ssm_lambda_re': _jnp.float32, 'ssm_lambda_im': _jnp.float32, 'ssm_log_dt': _jnp.float32, 'ssm_b_re': _jnp.float32, 'ssm_b_im': _jnp.float32, 'ssm_c_re': _jnp.float32, 'ssm_c_im': _jnp.float32, 'ssm_d': _jnp.float32, 'ssm_w_glu': _jnp.float32, 'ssm_b_glu': _jnp.float32, 'ssm_w_out': _jnp.float32, 'conv_w': _jnp.float32, 'conv_b': _jnp.float32, 'conv_w_out': _jnp.float32, 'w_o': _jnp.float32, 'ffn2_norm': _jnp.float32, 'ffn2_w_gate': _jnp.float32, 'ffn2_w_up': _jnp.float32, 'ffn2_w_down': _jnp.float32, 'final_norm': _jnp.float32}
MOMENT_SCALE = {'ffn1_norm': 3.262164e-02, 'ffn1_w_gate': 1.404700e-02, 'ffn1_w_up': 1.359495e-02, 'ffn1_w_down': 2.227959e-02, 'mix_norm': 5.456372e-02, 'w_in': 2.681345e-02, 'ssm_lambda_re': 1.143037e-03, 'ssm_lambda_im': 1.027599e-03, 'ssm_log_dt': 8.719890e-01, 'ssm_b_re': 6.992861e-04, 'ssm_b_im': 7.046669e-04, 'ssm_c_re': 1.000056e-03, 'ssm_c_im': 9.808077e-04, 'ssm_d': 1.571514e-02, 'ssm_w_glu': 4.243110e-03, 'ssm_b_glu': 6.429761e-03, 'ssm_w_out': 1.030399e-02, 'conv_w': 4.252854e-02, 'conv_b': 4.238101e-02, 'conv_w_out': 2.937059e-02, 'w_o': 3.116665e-02, 'ffn2_norm': 2.196770e-02, 'ffn2_w_gate': 9.585622e-03, 'ffn2_w_up': 9.275360e-03, 'ffn2_w_down': 1.521150e-02, 'final_norm': 8.003152e+00}


def _to_microbatches(a, axis):
    t = _jnp.moveaxis(a, axis, 0)
    t = t.reshape((N_MICROBATCH, t.shape[0] // N_MICROBATCH) + t.shape[1:])
    return _jnp.moveaxis(t, 1, axis + 1)


def setup_inputs(seed: int = 0) -> dict:
    inp = _fwd_setup_inputs(seed)
    key = _jax.random.fold_in(_jax.random.key(seed), 7919)
    shape, _ = _output_shape()
    out = dict(inp)
    out["loss_target"] = _jax.random.normal(_jax.random.fold_in(key, 0), shape, _jnp.float32)
    for i, name in enumerate(TWIN_WEIGHTS):
        w = inp[name].astype(_jnp.float32)
        if MOMENT_SCALE is None:
            s = _jnp.sqrt(_jnp.mean(_jnp.square(w)) + 1e-30)
        else:
            s = MOMENT_SCALE[name]
        km, kv = _jax.random.split(_jax.random.fold_in(key, i + 1))
        out[name] = w
        out["m_" + name] = s * _jax.random.normal(km, w.shape, _jnp.float32)
        out["v_" + name] = (s * s) * _jax.random.uniform(kv, w.shape, _jnp.float32, 0.5, 1.5)
    if N_MICROBATCH > 1:
        for name, axis in PER_EXAMPLE_BATCH_AXIS.items():
            out[name] = _to_microbatches(out[name], axis)
    return {'x': out['x'], 'ffn1_norm': out['ffn1_norm'], 'ffn1_w_gate': out['ffn1_w_gate'], 'ffn1_w_up': out['ffn1_w_up'], 'ffn1_w_down': out['ffn1_w_down'], 'mix_norm': out['mix_norm'], 'w_in': out['w_in'], 'ssm_lambda_re': out['ssm_lambda_re'], 'ssm_lambda_im': out['ssm_lambda_im'], 'ssm_log_dt': out['ssm_log_dt'], 'ssm_b_re': out['ssm_b_re'], 'ssm_b_im': out['ssm_b_im'], 'ssm_c_re': out['ssm_c_re'], 'ssm_c_im': out['ssm_c_im'], 'ssm_d': out['ssm_d'], 'ssm_w_glu': out['ssm_w_glu'], 'ssm_b_glu': out['ssm_b_glu'], 'ssm_w_out': out['ssm_w_out'], 'conv_w': out['conv_w'], 'conv_b': out['conv_b'], 'conv_w_out': out['conv_w_out'], 'w_o': out['w_o'], 'ffn2_norm': out['ffn2_norm'], 'ffn2_w_gate': out['ffn2_w_gate'], 'ffn2_w_up': out['ffn2_w_up'], 'ffn2_w_down': out['ffn2_w_down'], 'final_norm': out['final_norm'], 'loss_target': out['loss_target'], 'm_ffn1_norm': out['m_ffn1_norm'], 'm_ffn1_w_gate': out['m_ffn1_w_gate'], 'm_ffn1_w_up': out['m_ffn1_w_up'], 'm_ffn1_w_down': out['m_ffn1_w_down'], 'm_mix_norm': out['m_mix_norm'], 'm_w_in': out['m_w_in'], 'm_ssm_lambda_re': out['m_ssm_lambda_re'], 'm_ssm_lambda_im': out['m_ssm_lambda_im'], 'm_ssm_log_dt': out['m_ssm_log_dt'], 'm_ssm_b_re': out['m_ssm_b_re'], 'm_ssm_b_im': out['m_ssm_b_im'], 'm_ssm_c_re': out['m_ssm_c_re'], 'm_ssm_c_im': out['m_ssm_c_im'], 'm_ssm_d': out['m_ssm_d'], 'm_ssm_w_glu': out['m_ssm_w_glu'], 'm_ssm_b_glu': out['m_ssm_b_glu'], 'm_ssm_w_out': out['m_ssm_w_out'], 'm_conv_w': out['m_conv_w'], 'm_conv_b': out['m_conv_b'], 'm_conv_w_out': out['m_conv_w_out'], 'm_w_o': out['m_w_o'], 'm_ffn2_norm': out['m_ffn2_norm'], 'm_ffn2_w_gate': out['m_ffn2_w_gate'], 'm_ffn2_w_up': out['m_ffn2_w_up'], 'm_ffn2_w_down': out['m_ffn2_w_down'], 'm_final_norm': out['m_final_norm'], 'v_ffn1_norm': out['v_ffn1_norm'], 'v_ffn1_w_gate': out['v_ffn1_w_gate'], 'v_ffn1_w_up': out['v_ffn1_w_up'], 'v_ffn1_w_down': out['v_ffn1_w_down'], 'v_mix_norm': out['v_mix_norm'], 'v_w_in': out['v_w_in'], 'v_ssm_lambda_re': out['v_ssm_lambda_re'], 'v_ssm_lambda_im': out['v_ssm_lambda_im'], 'v_ssm_log_dt': out['v_ssm_log_dt'], 'v_ssm_b_re': out['v_ssm_b_re'], 'v_ssm_b_im': out['v_ssm_b_im'], 'v_ssm_c_re': out['v_ssm_c_re'], 'v_ssm_c_im': out['v_ssm_c_im'], 'v_ssm_d': out['v_ssm_d'], 'v_ssm_w_glu': out['v_ssm_w_glu'], 'v_ssm_b_glu': out['v_ssm_b_glu'], 'v_ssm_w_out': out['v_ssm_w_out'], 'v_conv_w': out['v_conv_w'], 'v_conv_b': out['v_conv_b'], 'v_conv_w_out': out['v_conv_w_out'], 'v_w_o': out['v_w_o'], 'v_ffn2_norm': out['v_ffn2_norm'], 'v_ffn2_w_gate': out['v_ffn2_w_gate'], 'v_ffn2_w_up': out['v_ffn2_w_up'], 'v_ffn2_w_down': out['v_ffn2_w_down'], 'v_final_norm': out['v_final_norm']}


def _loss(weights, diff, rest, loss_target):
    with _jax.named_scope("forward"):
        args = {**rest, TWIN_DIFF_INPUT: diff, **{k: w.astype(_WEIGHT_DTYPES[k]) for k, w in weights.items()}}
        y = _forward(args)
    with _jax.named_scope("loss_head"):
        err = _jnp.square(y.astype(_jnp.float32) - loss_target)
        return 0.5 * _jnp.sum(_jnp.mean(err, axis=-1)) if err.ndim else 0.5 * err


def _adamw(w, g, m, v):
    m = ADAM_B1 * m + (1.0 - ADAM_B1) * g
    v = ADAM_B2 * v + (1.0 - ADAM_B2) * _jnp.square(g)
    m_hat = m / (1.0 - ADAM_B1 ** ADAM_STEP)
    v_hat = v / (1.0 - ADAM_B2 ** ADAM_STEP)
    delta = -ADAM_LR * (m_hat / (_jnp.sqrt(v_hat) + ADAM_EPS) + ADAM_WD * w)
    return delta, m, v


def reference(x, ffn1_norm, ffn1_w_gate, ffn1_w_up, ffn1_w_down, mix_norm, w_in, ssm_lambda_re, ssm_lambda_im, ssm_log_dt, ssm_b_re, ssm_b_im, ssm_c_re, ssm_c_im, ssm_d, ssm_w_glu, ssm_b_glu, ssm_w_out, conv_w, conv_b, conv_w_out, w_o, ffn2_norm, ffn2_w_gate, ffn2_w_up, ffn2_w_down, final_norm, loss_target, m_ffn1_norm, m_ffn1_w_gate, m_ffn1_w_up, m_ffn1_w_down, m_mix_norm, m_w_in, m_ssm_lambda_re, m_ssm_lambda_im, m_ssm_log_dt, m_ssm_b_re, m_ssm_b_im, m_ssm_c_re, m_ssm_c_im, m_ssm_d, m_ssm_w_glu, m_ssm_b_glu, m_ssm_w_out, m_conv_w, m_conv_b, m_conv_w_out, m_w_o, m_ffn2_norm, m_ffn2_w_gate, m_ffn2_w_up, m_ffn2_w_down, m_final_norm, v_ffn1_norm, v_ffn1_w_gate, v_ffn1_w_up, v_ffn1_w_down, v_mix_norm, v_w_in, v_ssm_lambda_re, v_ssm_lambda_im, v_ssm_log_dt, v_ssm_b_re, v_ssm_b_im, v_ssm_c_re, v_ssm_c_im, v_ssm_d, v_ssm_w_glu, v_ssm_b_glu, v_ssm_w_out, v_conv_w, v_conv_b, v_conv_w_out, v_w_o, v_ffn2_norm, v_ffn2_w_gate, v_ffn2_w_up, v_ffn2_w_down, v_final_norm):
    given = dict(x=x, ffn1_norm=ffn1_norm, ffn1_w_gate=ffn1_w_gate, ffn1_w_up=ffn1_w_up, ffn1_w_down=ffn1_w_down, mix_norm=mix_norm, w_in=w_in, ssm_lambda_re=ssm_lambda_re, ssm_lambda_im=ssm_lambda_im, ssm_log_dt=ssm_log_dt, ssm_b_re=ssm_b_re, ssm_b_im=ssm_b_im, ssm_c_re=ssm_c_re, ssm_c_im=ssm_c_im, ssm_d=ssm_d, ssm_w_glu=ssm_w_glu, ssm_b_glu=ssm_b_glu, ssm_w_out=ssm_w_out, conv_w=conv_w, conv_b=conv_b, conv_w_out=conv_w_out, w_o=w_o, ffn2_norm=ffn2_norm, ffn2_w_gate=ffn2_w_gate, ffn2_w_up=ffn2_w_up, ffn2_w_down=ffn2_w_down, final_norm=final_norm, loss_target=loss_target, m_ffn1_norm=m_ffn1_norm, m_ffn1_w_gate=m_ffn1_w_gate, m_ffn1_w_up=m_ffn1_w_up, m_ffn1_w_down=m_ffn1_w_down, m_mix_norm=m_mix_norm, m_w_in=m_w_in, m_ssm_lambda_re=m_ssm_lambda_re, m_ssm_lambda_im=m_ssm_lambda_im, m_ssm_log_dt=m_ssm_log_dt, m_ssm_b_re=m_ssm_b_re, m_ssm_b_im=m_ssm_b_im, m_ssm_c_re=m_ssm_c_re, m_ssm_c_im=m_ssm_c_im, m_ssm_d=m_ssm_d, m_ssm_w_glu=m_ssm_w_glu, m_ssm_b_glu=m_ssm_b_glu, m_ssm_w_out=m_ssm_w_out, m_conv_w=m_conv_w, m_conv_b=m_conv_b, m_conv_w_out=m_conv_w_out, m_w_o=m_w_o, m_ffn2_norm=m_ffn2_norm, m_ffn2_w_gate=m_ffn2_w_gate, m_ffn2_w_up=m_ffn2_w_up, m_ffn2_w_down=m_ffn2_w_down, m_final_norm=m_final_norm, v_ffn1_norm=v_ffn1_norm, v_ffn1_w_gate=v_ffn1_w_gate, v_ffn1_w_up=v_ffn1_w_up, v_ffn1_w_down=v_ffn1_w_down, v_mix_norm=v_mix_norm, v_w_in=v_w_in, v_ssm_lambda_re=v_ssm_lambda_re, v_ssm_lambda_im=v_ssm_lambda_im, v_ssm_log_dt=v_ssm_log_dt, v_ssm_b_re=v_ssm_b_re, v_ssm_b_im=v_ssm_b_im, v_ssm_c_re=v_ssm_c_re, v_ssm_c_im=v_ssm_c_im, v_ssm_d=v_ssm_d, v_ssm_w_glu=v_ssm_w_glu, v_ssm_b_glu=v_ssm_b_glu, v_ssm_w_out=v_ssm_w_out, v_conv_w=v_conv_w, v_conv_b=v_conv_b, v_conv_w_out=v_conv_w_out, v_w_o=v_w_o, v_ffn2_norm=v_ffn2_norm, v_ffn2_w_gate=v_ffn2_w_gate, v_ffn2_w_up=v_ffn2_w_up, v_ffn2_w_down=v_ffn2_w_down, v_final_norm=v_final_norm)
    weights = {n: given[n] for n in TWIN_WEIGHTS}
    shared = {n: given[n] for n in SHARED_INPUTS}
    per_example = {n: given[n] for n in ['x']}
    grad_fn = _jax.value_and_grad(_loss, argnums=(0, 1))

    def one_microbatch(ex, loss_target):
        ex = dict(ex)
        diff = ex.pop(TWIN_DIFF_INPUT)
        return grad_fn(weights, diff, {**shared, **ex}, loss_target)

    if N_MICROBATCH == 1:
        loss, (grad_w, grad_x) = one_microbatch(per_example, given["loss_target"])
    else:
        def body(carry, xs):
            loss_sum, grad_sum = carry
            l_k, (gw_k, gx_k) = one_microbatch(xs[0], xs[1])
            with _jax.named_scope("update"):
                return (loss_sum + l_k, _jax.tree.map(_jnp.add, grad_sum, gw_k)), gx_k

        init = (_jnp.zeros((), _jnp.float32), _jax.tree.map(_jnp.zeros_like, weights))
        (loss, grad_w), grad_x = _jax.lax.scan(body, init, (per_example, given["loss_target"]))
    with _jax.named_scope("update"):
        delta_w, new_m, new_v = {}, {}, {}
        for n in TWIN_WEIGHTS:
            delta_w[n], new_m[n], new_v[n] = _adamw(weights[n], grad_w[n], given["m_" + n], given["v_" + n])
    return (loss, grad_x, *[grad_w[n] for n in TWIN_WEIGHTS], *[delta_w[n] for n in TWIN_WEIGHTS],
            *[new_m[n] for n in TWIN_WEIGHTS], *[new_v[n] for n in TWIN_WEIGHTS])
```

```python
import functools
import math

import jax
import jax.numpy as jnp
from jax import lax
from jax.experimental import pallas as pl
from jax.experimental.pallas import tpu as pltpu

F32 = jnp.float32
BF = jnp.bfloat16
I32 = jnp.int32
MESH = pl.DeviceIdType.MESH
LANE = 128
SUBLANE = 8
NDEV = 8
EPS = 1e-6
ADAM_LR, ADAM_B1, ADAM_B2, ADAM_EPS, ADAM_WD, ADAM_STEP = 0.001, 0.9, 0.999, 1e-08, 0.01, 10
NN = ((1,), (0,))
NT = ((1,), (1,))
TN = ((0,), (0,))
HBM = pl.BlockSpec(memory_space=pltpu.HBM)


def _pick(n, pref, mult):
    t = min(pref, n)
    t -= t % mult
    while t >= mult:
        if n % t == 0:
            return t
        t -= mult
    return n


def _sigmoid(x):
    return 1.0 / (1.0 + jnp.exp(-x))


_GELU_C = math.sqrt(2.0 / math.pi)


def _gelu(x):
    return 0.5 * x * (1.0 + jnp.tanh(_GELU_C * (x + 0.044715 * x * x * x)))


def _gelu_grad(x):
    t = jnp.tanh(_GELU_C * (x + 0.044715 * x * x * x))
    return 0.5 * (1.0 + t) + 0.5 * x * (1.0 - t * t) * _GELU_C * (1.0 + 3.0 * 0.044715 * x * x)


def _mm(name, grid, ops, pairs, sides, outs, epilogue, acc_shapes):
    nk = grid[-1]
    n_ops, n_sides, n_outs = len(ops), len(sides), len(outs)

    def body(*refs):
        op_refs = refs[:n_ops]
        side_refs = refs[n_ops:n_ops + n_sides]
        out_refs = refs[n_ops + n_sides:n_ops + n_sides + n_outs]
        acc_refs = refs[n_ops + n_sides + n_outs:]

        def partials():
            res = [None] * len(acc_shapes)
            for ia, ib, dims, ai in pairs:
                p = lax.dot_general(op_refs[ia][...], op_refs[ib][...], (dims, ((), ())),
                                    preferred_element_type=F32)
                res[ai] = p if res[ai] is None else res[ai] + p
            return res

        def finish(accs):
            vals = epilogue(accs, [s[...] for s in side_refs])
            for o, v in zip(out_refs, vals):
                o[...] = v.astype(o.dtype)

        if nk == 1:
            finish(partials())
        else:
            k = pl.program_id(len(grid) - 1)

            @pl.when(k == 0)
            def _():
                for a, p in zip(acc_refs, partials()):
                    a[...] = p

            @pl.when(k > 0)
            def _():
                for a, p in zip(acc_refs, partials()):
                    a[...] += p

            @pl.when(k == nk - 1)
            def _():
                finish([a[...] for a in acc_refs])

    return pl.pallas_call(
        body, name=name, grid=grid,
        in_specs=[pl.BlockSpec(b, m) for (_, b, m) in list(ops) + list(sides)],
        out_specs=[pl.BlockSpec(b, m) for (_, _, b, m) in outs],
        out_shape=[jax.ShapeDtypeStruct(s, d) for (s, d, _, _) in outs],
        scratch_shapes=[pltpu.VMEM(s, F32) for s in acc_shapes] if nk > 1 else [],
        compiler_params=pltpu.CompilerParams(
            dimension_semantics=("parallel",) * (len(grid) - 1) + ("arbitrary",)),
    )(*[a for (a, _, _) in list(ops) + list(sides)])


def _ew(name, grid, ins, outs, fn, acc=()):
    n_in = len(ins)

    def body(*refs):
        vals = fn(*[r[...] for r in refs[:n_in]])
        first = pl.program_id(0) == 0
        for idx, (o, v) in enumerate(zip(refs[n_in:], vals)):
            if idx in acc:
                @pl.when(first)
                def _(o=o, v=v):
                    o[...] = v.astype(o.dtype)

                @pl.when(jnp.logical_not(first))
                def _(o=o, v=v):
                    o[...] += v.astype(o.dtype)
            else:
                o[...] = v.astype(o.dtype)

    return pl.pallas_call(
        body, name=name, grid=grid,
        in_specs=[pl.BlockSpec(b, m) for (_, b, m) in ins],
        out_specs=[pl.BlockSpec(b, m) for (_, _, b, m) in outs],
        out_shape=[jax.ShapeDtypeStruct(s, d) for (s, d, _, _) in outs],
        compiler_params=pltpu.CompilerParams(
            dimension_semantics=(("arbitrary",) if acc else ("parallel",)) * len(grid)),
    )(*[a for (a, _, _) in ins])


def _position():
    x, y, c = lax.axis_index("x"), lax.axis_index("y"), lax.axis_index("c")
    chips = [(1 - x, y), (x, 1 - y), (1 - x, 1 - y)]
    return x, y, c, chips


def _all_gather(name, shards):
    n = len(shards)

    def body(*refs):
        xs, outs = refs[:n], refs[n:2 * n]
        send_sems, recv_sems, local_sems = refs[2 * n:]
        x, y, c, chips = _position()
        me, sibling = (x, y, c), (x, y, 1 - c)

        def copy(a, k, block, to, src=None):
            dst = outs[a].at[4 * block[0] + 2 * block[1] + block[2]]
            return pltpu.make_async_remote_copy(
                src_ref=dst if src is None else src, dst_ref=dst,
                send_sem=send_sems.at[a, k], recv_sem=recv_sems.at[a, k],
                device_id=to, device_id_type=MESH)

        mine = [pltpu.make_async_copy(xs[a], outs[a].at[4 * x + 2 * y + c], local_sems.at[a]) for a in range(n)]
        for cp in mine:
            cp.start()
        first = []
        for a in range(n):
            first.append(copy(a, 0, me, sibling, src=xs[a]))
            first += [copy(a, 1 + j, me, (*chip, c), src=xs[a]) for j, chip in enumerate(chips)]
        for cp in first:
            cp.start()
        passed = []
        for a in range(n):
            for j, chip in enumerate(chips):
                copy(a, 1 + j, (*chip, c), me).wait_recv()
                cp = copy(a, 4 + j, (*chip, c), sibling)
                cp.start()
                passed.append(cp)
        for a in range(n):
            copy(a, 0, sibling, me).wait_recv()
            for j, chip in enumerate(chips):
                copy(a, 4 + j, (*chip, 1 - c), me).wait_recv()
        for cp in first + passed:
            cp.wait_send()
        for cp in mine:
            cp.wait()

    return pl.pallas_call(
        body, name=name,
        out_shape=[jax.ShapeDtypeStruct((NDEV,) + s.shape, s.dtype) for s in shards],
        in_specs=[HBM] * n, out_specs=[HBM] * n,
        scratch_shapes=[pltpu.SemaphoreType.DMA((n, 7)), pltpu.SemaphoreType.DMA((n, 7)),
                        pltpu.SemaphoreType.DMA((n,))],
    )(*shards)


def _exchange_sibling(name, grads):
    n = len(grads)

    def body(*refs):
        gs, lands = refs[:n], refs[n:2 * n]
        send_sems, recv_sems = refs[2 * n:]
        x, y, c, _ = _position()
        copies = []
        for a in range(n):
            for q in range(4):
                copies.append(pltpu.make_async_remote_copy(
                    src_ref=gs[a].at[2 * q + 1 - c], dst_ref=lands[a].at[q],
                    send_sem=send_sems.at[a, q], recv_sem=recv_sems.at[a, q],
                    device_id=(x, y, 1 - c), device_id_type=MESH))
        for cp in copies:
            cp.start()
        for cp in copies:
            cp.wait()

    return pl.pallas_call(
        body, name=name,
        out_shape=[jax.ShapeDtypeStruct((4,) + g.shape[1:], g.dtype) for g in grads],
        in_specs=[HBM] * n, out_specs=[HBM] * n,
        scratch_shapes=[pltpu.SemaphoreType.DMA((n, 4)), pltpu.SemaphoreType.DMA((n, 4))],
    )(*grads)


def _exchange_chips(name, sums):
    n = len(sums)

    def body(*refs):
        ps, lands = refs[:n], refs[n:2 * n]
        send_sems, recv_sems = refs[2 * n:]
        x, y, c, chips = _position()
        copies = []
        for a in range(n):
            for j, chip in enumerate(chips):
                copies.append(pltpu.make_async_remote_copy(
                    src_ref=ps[a].at[2 * chip[0] + chip[1]], dst_ref=lands[a].at[j],
                    send_sem=send_sems.at[a, j], recv_sem=recv_sems.at[a, j],
                    device_id=(*chip, c), device_id_type=MESH))
        for cp in copies:
            cp.start()
        for cp in copies:
            cp.wait()

    return pl.pallas_call(
        body, name=name,
        out_shape=[jax.ShapeDtypeStruct((3,) + p.shape[1:], p.dtype) for p in sums],
        in_specs=[HBM] * n, out_specs=[HBM] * n,
        scratch_shapes=[pltpu.SemaphoreType.DMA((n, 3)), pltpu.SemaphoreType.DMA((n, 3))],
    )(*sums)


def _sum_sibling(name, g, land, c_arr):
    _, R, C = g.shape
    tr = _pick(R, 512, SUBLANE)

    def body(c_ref, g_ref, l_ref, o_ref):
        o_ref[...] = (g_ref[...].astype(F32) + l_ref[...].astype(F32)).astype(o_ref.dtype)

    return pl.pallas_call(
        body, name=name,
        grid_spec=pltpu.PrefetchScalarGridSpec(
            num_scalar_prefetch=1, grid=(4, R // tr),
            in_specs=[pl.BlockSpec((None, tr, C), lambda q, i, cr: (2 * q + cr[0], i, 0)),
                      pl.BlockSpec((None, tr, C), lambda q, i, cr: (q, i, 0))],
            out_specs=pl.BlockSpec((None, tr, C), lambda q, i, cr: (q, i, 0))),
        out_shape=jax.ShapeDtypeStruct((4, R, C), g.dtype),
        compiler_params=pltpu.CompilerParams(dimension_semantics=("parallel", "parallel")),
    )(c_arr, g, land)


def _adamw(w, g, m, v):
    m = ADAM_B1 * m + (1.0 - ADAM_B1) * g
    v = ADAM_B2 * v + (1.0 - ADAM_B2) * (g * g)
    m_hat = m / (1.0 - ADAM_B1 ** ADAM_STEP)
    v_hat = v / (1.0 - ADAM_B2 ** ADAM_STEP)
    delta = -ADAM_LR * (m_hat / (jnp.sqrt(v_hat) + ADAM_EPS) + ADAM_WD * w)
    return delta, m, v


def _finish_sharded(name, sums, land, q_arr, w, m, v):
    R, C = w.shape
    tr = _pick(R, 256, SUBLANE)

    def body(q_ref, p_ref, l_ref, w_ref, m_ref, v_ref, g_out, d_out, m_out, v_out):
        g = p_ref[...].astype(F32)
        for j in range(3):
            g = g + l_ref[j].astype(F32)
        d, mn, vn = _adamw(w_ref[...], g, m_ref[...], v_ref[...])
        g_out[...] = g
        d_out[...] = d
        m_out[...] = mn
        v_out[...] = vn

    row = pl.BlockSpec((tr, C), lambda i, qr: (i, 0))
    return pl.pallas_call(
        body, name=name,
        grid_spec=pltpu.PrefetchScalarGridSpec(
            num_scalar_prefetch=1, grid=(R // tr,),
            in_specs=[pl.BlockSpec((None, tr, C), lambda i, qr: (qr[0], i, 0)),
                      pl.BlockSpec((3, tr, C), lambda i, qr: (0, i, 0)), row, row, row],
            out_specs=[row] * 4),
        out_shape=[jax.ShapeDtypeStruct((R, C), F32)] * 4,
        compiler_params=pltpu.CompilerParams(dimension_semantics=("parallel",)),
    )(q_arr, sums, land, w, m, v)


def _finish_replicated(name, gathered, w, m, v):
    _, R, C = gathered.shape
    tr = _pick(R, 256, SUBLANE)

    def fn(gv, wv, mv, vv):
        g = gv[0]
        for d in range(1, NDEV):
            g = g + gv[d]
        dl, mn, vn = _adamw(wv, g, mv, vv)
        return g, dl, mn, vn

    row = ((tr, C), lambda i: (i, 0))
    return _ew(name, (R // tr,),
               [(gathered, (NDEV, tr, C), lambda i: (0, i, 0)), (w,) + row, (m,) + row, (v,) + row],
               [((R, C), F32) + row] * 4, fn)


def _rms_fwd(name, h, g):
    S, D = h.shape
    tr = _pick(S, 256, SUBLANE)

    def fn(hv, gv):
        r = lax.rsqrt(jnp.mean(hv * hv, axis=-1, keepdims=True) + EPS)
        return hv * r * gv, r

    return _ew(name, (S // tr,),
               [(h, (tr, D), lambda i: (i, 0)), (g, (1, D), lambda i: (0, 0))],
               [((S, D), BF, (tr, D), lambda i: (i, 0)), ((S, 1), F32, (tr, 1), lambda i: (i, 0))], fn)


def _rms_bwd(name, du, h, r, g, dres, scale):
    S, D = h.shape
    tr = _pick(S, 256, SUBLANE)

    def fn(duv, hv, rv, gv, drv):
        xn = hv * rv
        dxn = duv * gv
        dh = drv + rv * (dxn - xn * jnp.mean(dxn * xn, axis=-1, keepdims=True))
        return dh, scale * dh, jnp.sum(duv * xn, axis=0, keepdims=True)

    row = ((tr, D), lambda i: (i, 0))
    return _ew(name, (S // tr,),
               [(du,) + row, (h,) + row, (r, (tr, 1), lambda i: (i, 0)), (g, (1, D), lambda i: (0, 0)), (dres,) + row],
               [((S, D), F32) + row, ((S, D), BF) + row, ((1, D), F32, (1, D), lambda i: (0, 0))], fn, acc=(2,))


def _loss_head(name, h, g, target):
    S, D = h.shape
    tr = _pick(S, 256, SUBLANE)

    def fn(hv, gv, tv):
        r = lax.rsqrt(jnp.mean(hv * hv, axis=-1, keepdims=True) + EPS)
        xn = hv * r
        diff = xn * gv - tv
        loss = 0.5 * jnp.sum(jnp.mean(diff * diff, axis=-1, keepdims=True))
        dout = diff / D
        dxn = dout * gv
        dh = r * (dxn - xn * jnp.mean(dxn * xn, axis=-1, keepdims=True))
        return (jnp.zeros((1, LANE), F32) + loss, dh, 0.5 * dh, jnp.sum(dout * xn, axis=0, keepdims=True))

    row = ((tr, D), lambda i: (i, 0))
    return _ew(name, (S // tr,),
               [(h,) + row, (g, (1, D), lambda i: (0, 0)), (target,) + row],
               [((1, LANE), F32, (1, LANE), lambda i: (0, 0)), ((S, D), F32) + row, ((S, D), BF) + row,
                ((1, D), F32, (1, D), lambda i: (0, 0))], fn, acc=(0, 3))


def _ffn_fwd(tag, u, h, wg, wu, wd):
    S, D = u.shape
    Fs = wg.shape[2]
    tm, tk = _pick(S, 512, SUBLANE), _pick(D, 1024, LANE)

    def up_epilogue(accs, sides):
        gt, up = accs
        return gt, up, gt * _sigmoid(gt) * up

    act = ((NDEV, S, Fs), BF, (None, tm, Fs), lambda b, i, j, k: (b, i, 0))
    gt, up, a = _mm(
        tag + "_up", (NDEV, S // tm, 1, D // tk),
        [(u, (tm, tk), lambda b, i, j, k: (i, k)),
         (wg, (None, tk, Fs), lambda b, i, j, k: (b, k, 0)), (wu, (None, tk, Fs), lambda b, i, j, k: (b, k, 0))],
        [(0, 1, NN, 0), (0, 2, NN, 1)], [], [act, act, act], up_epilogue, [(tm, Fs), (tm, Fs)])
    tn = _pick(D, 1024, LANE)
    (hn,) = _mm(
        tag + "_down", (1, S // tm, D // tn, NDEV),
        [(a, (None, tm, Fs), lambda b, i, j, k: (k, i, 0)), (wd, (None, Fs, tn), lambda b, i, j, k: (k, 0, j))],
        [(0, 1, NN, 0)], [(h, (tm, tn), lambda b, i, j, k: (i, j))],
        [((S, D), F32, (tm, tn), lambda b, i, j, k: (i, j))],
        lambda accs, sides: [sides[0] + 0.5 * accs[0]], [(tm, tn)])
    return hn, (gt, up, a)


def _ffn_bwd(tag, dhs, u, saved, wg, wu, wd):
    gt, up, a = saved
    S, D = u.shape
    Fs = wg.shape[2]
    tm, tk = _pick(S, 512, SUBLANE), _pick(D, 1024, LANE)
    act_in = lambda arr: (arr, (None, tm, Fs), lambda b, i, j, k: (b, i, 0))
    act_out = ((NDEV, S, Fs), BF, (None, tm, Fs), lambda b, i, j, k: (b, i, 0))

    def act_epilogue(accs, sides):
        da = accs[0]
        gtv, upv = sides[0].astype(F32), sides[1].astype(F32)
        sg = _sigmoid(gtv)
        return da * upv * sg * (1.0 + gtv * (1.0 - sg)), da * gtv * sg

    dgt, dup = _mm(
        tag + "_dact", (NDEV, S // tm, 1, D // tk),
        [(dhs, (tm, tk), lambda b, i, j, k: (i, k)), (wd, (None, Fs, tk), lambda b, i, j, k: (b, 0, k))],
        [(0, 1, NT, 0)], [act_in(gt), act_in(up)], [act_out, act_out], act_epilogue, [(tm, Fs)])

    ts = _pick(S, 512, SUBLANE)
    tn = _pick(D, 1024, LANE)
    (dwd,) = _mm(
        tag + "_dwd", (NDEV, 1, D // tn, S // ts),
        [(a, (None, ts, Fs), lambda b, i, j, k: (b, k, 0)), (dhs, (ts, tn), lambda b, i, j, k: (k, j))],
        [(0, 1, TN, 0)], [], [((NDEV, Fs, D), BF, (None, Fs, tn), lambda b, i, j, k: (b, 0, j))],
        lambda accs, sides: accs, [(Fs, tn)])
    td = _pick(D, 512, LANE)
    wgrad = ((NDEV, D, Fs), BF, (None, td, Fs), lambda b, i, j, k: (b, i, 0))
    dwg, dwu = _mm(
        tag + "_dwgu", (NDEV, D // td, 1, S // ts),
        [(u, (ts, td), lambda b, i, j, k: (k, i)),
         (dgt, (None, ts, Fs), lambda b, i, j, k: (b, k, 0)), (dup, (None, ts, Fs), lambda b, i, j, k: (b, k, 0))],
        [(0, 1, TN, 0), (0, 2, TN, 1)], [], [wgrad, wgrad], lambda accs, sides: accs, [(td, Fs), (td, Fs)])
    (du,) = _mm(
        tag + "_du", (1, S // tm, D // tn, NDEV),
        [(dgt, (None, tm, Fs), lambda b, i, j, k: (k, i, 0)), (dup, (None, tm, Fs), lambda b, i, j, k: (k, i, 0)),
         (wg, (None, tn, Fs), lambda b, i, j, k: (k, j, 0)), (wu, (None, tn, Fs), lambda b, i, j, k: (k, j, 0))],
        [(0, 2, NT, 0), (1, 3, NT, 0)], [], [((S, D), F32, (tm, tn), lambda b, i, j, k: (i, j))],
        lambda accs, sides: accs, [(tm, tn)])
    return du, dwg, dwu, dwd


def _ssm_params(lam_re, lam_im, log_dt, b_re, b_im, c_re, c_im):
    G, N = lam_re.shape
    C = b_re.shape[2]
    lam_re = jnp.minimum(lam_re, -1e-4)
    dt = jnp.exp(log_dt)[:, None]
    mag = jnp.exp(lam_re * dt)
    a_re = mag * jnp.cos(lam_im * dt)
    a_im = mag * jnp.sin(lam_im * dt)
    den = lam_re * lam_re + lam_im * lam_im
    p = a_re - 1.0
    f_re = ((p * lam_re + a_im * lam_im) / den)[:, :, None]
    f_im = ((a_im * lam_re - p * lam_im) / den)[:, :, None]
    bb_re = f_re * b_re - f_im * b_im
    bb_im = f_re * b_im + f_im * b_re
    gpt = LANE // C
    tiles = G // gpt
    eye = jnp.eye(gpt, dtype=F32)

    def bd(bb):
        return jnp.einsum("bgnc,gh->bgchn", bb.reshape(tiles, gpt, N, C), eye).reshape(tiles, gpt * C, gpt * N)

    def cd(cc):
        return jnp.einsum("bgcn,gh->bgnhc", cc.reshape(tiles, gpt, C, N), eye).reshape(tiles, gpt * N, gpt * C)

    rows = G * N // LANE
    return (a_re.reshape(rows, LANE), a_im.reshape(rows, LANE), bd(bb_re), bd(bb_im), cd(c_re), cd(-c_im))


def _scan_fwd(bu_re, bu_im, a_re, a_im):
    S, R, _ = bu_re.shape
    tc = _pick(S, 256, SUBLANE)

    def body(bre, bim, are, aim, sre, sim, carry):
        @pl.when(pl.program_id(0) == 0)
        def _():
            carry[...] = jnp.zeros_like(carry)

        ar, ai = are[...], aim[...]

        def step(t, c):
            pr, pi = c
            nr = ar * pr - ai * pi + bre[t]
            ni = ar * pi + ai * pr + bim[t]
            sre[t] = nr
            sim[t] = ni
            return nr, ni

        pr, pi = lax.fori_loop(0, tc, step, (carry[0], carry[1]), unroll=8)
        carry[0] = pr
        carry[1] = pi

    blk = pl.BlockSpec((tc, R, LANE), lambda i: (i, 0, 0))
    par = pl.BlockSpec((R, LANE), lambda i: (0, 0))
    return pl.pallas_call(
        body, name="ssm_scan_fwd", grid=(S // tc,),
        in_specs=[blk, blk, par, par], out_specs=[blk, blk],
        out_shape=[jax.ShapeDtypeStruct((S, R, LANE), F32)] * 2,
        scratch_shapes=[pltpu.VMEM((2, R, LANE), F32)],
        compiler_params=pltpu.CompilerParams(dimension_semantics=("arbitrary",)),
    )(bu_re, bu_im, a_re, a_im)


def _scan_bwd(ds_re, ds_im, s_re, s_im, a_re, a_im):
    S, R, _ = ds_re.shape
    tc = _pick(S, 256, SUBLANE)
    nc = S // tc

    def body(dre, dim_, sre, sim, are, aim, lre, lim, dar, dai, carry):
        @pl.when(pl.program_id(0) == 0)
        def _():
            carry[...] = jnp.zeros_like(carry)
            dar[...] = jnp.zeros_like(dar)
            dai[...] = jnp.zeros_like(dai)

        ar, ai = are[...], aim[...]

        def step(tt, c):
            t = tc - 1 - tt
            lr, li, gr, gi = c
            sr, si = sre[t], sim[t]
            gr = gr + lr * sr + li * si
            gi = gi + li * sr - lr * si
            nlr = dre[t] + ar * lr + ai * li
            nli = dim_[t] + ar * li - ai * lr
            lre[t] = nlr
            lim[t] = nli
            return nlr, nli, gr, gi

        lr, li, gr, gi = lax.fori_loop(0, tc, step, (carry[0], carry[1], dar[...], dai[...]), unroll=8)
        carry[0] = lr
        carry[1] = li
        dar[...] = gr
        dai[...] = gi

    blk = pl.BlockSpec((tc, R, LANE), lambda i: (nc - 1 - i, 0, 0))
    par = pl.BlockSpec((R, LANE), lambda i: (0, 0))
    return pl.pallas_call(
        body, name="ssm_scan_bwd", grid=(nc,),
        in_specs=[blk, blk, blk, blk, par, par], out_specs=[blk, blk, par, par],
        out_shape=[jax.ShapeDtypeStruct((S, R, LANE), F32)] * 2 + [jax.ShapeDtypeStruct((R, LANE), F32)] * 2,
        scratch_shapes=[pltpu.VMEM((2, R, LANE), F32)],
        compiler_params=pltpu.CompilerParams(dimension_semantics=("arbitrary",)),
    )(ds_re, ds_im, s_re, s_im, a_re, a_im)


def _shift_down(z, k):
    t = lax.broadcasted_iota(I32, z.shape, 0)
    return jnp.where(t >= k, pltpu.roll(z, k, 0), 0.0)


def _shift_up(z, k):
    n = z.shape[0]
    t = lax.broadcasted_iota(I32, z.shape, 0)
    return jnp.where(t < n - k, pltpu.roll(z, n - k, 0), 0.0)


def _conv_fwd(proj, cw, cb):
    _, S, W = proj.shape
    ct = _pick(W, 256, LANE)

    def fn(bg, cg, val, w, b):
        z = cg * val
        conv = b + w[0:1] * _shift_down(z, 2) + w[1:2] * _shift_down(z, 1) + w[2:3] * z
        return bg * conv, conv

    sl = lambda s: (proj, (None, S, ct), lambda j, s=s: (s, 0, j))
    col = ((S, ct), lambda j: (0, j))
    return _ew("conv_fwd", (W // ct,),
               [sl(1), sl(2), sl(3), (cw, (3, ct), lambda j: (0, j)), (cb, (1, ct), lambda j: (0, j))],
               [((S, W), BF) + col, ((S, W), F32) + col], fn)


def _conv_bwd(dyb, proj, conv, cw):
    _, S, W = proj.shape
    ct = _pick(W, 256, LANE)

    def fn(dy, bg, cg, val, cv, w):
        z = cg * val
        z1, z2 = _shift_down(z, 1), _shift_down(z, 2)
        dconv = dy * bg
        dz = w[2:3] * dconv + w[1:2] * _shift_up(dconv, 1) + w[0:1] * _shift_up(dconv, 2)
        dw = jnp.concatenate([jnp.sum(dconv * z2, axis=0, keepdims=True), jnp.sum(dconv * z1, axis=0, keepdims=True),
                              jnp.sum(dconv * z, axis=0, keepdims=True)], axis=0)
        return dy * cv, dz * val, dz * cg, dw, jnp.sum(dconv, axis=0, keepdims=True)

    sl = lambda s: (proj, (None, S, ct), lambda j, s=s: (s, 0, j))
    col = ((S, ct), lambda j: (0, j))
    return _ew("conv_bwd", (W // ct,),
               [(dyb,) + col, sl(1), sl(2), sl(3), (conv,) + col, (cw, (3, ct), lambda j: (0, j))],
               [((S, W), BF) + col, ((S, W), BF) + col, ((S, W), BF) + col,
                ((3, W), F32, (3, ct), lambda j: (0, j)), ((1, W), F32, (1, ct), lambda j: (0, j))], fn)


def _plain(accs, sides):
    return accs


def kernel(x, ffn1_norm, ffn1_w_gate, ffn1_w_up, ffn1_w_down, mix_norm, w_in, ssm_lambda_re, ssm_lambda_im, ssm_log_dt, ssm_b_re, ssm_b_im, ssm_c_re, ssm_c_im, ssm_d, ssm_w_glu, ssm_b_glu, ssm_w_out, conv_w, conv_b, conv_w_out, w_o, ffn2_norm, ffn2_w_gate, ffn2_w_up, ffn2_w_down, final_norm, loss_target, m_ffn1_norm, m_ffn1_w_gate, m_ffn1_w_up, m_ffn1_w_down, m_mix_norm, m_w_in, m_ssm_lambda_re, m_ssm_lambda_im, m_ssm_log_dt, m_ssm_b_re, m_ssm_b_im, m_ssm_c_re, m_ssm_c_im, m_ssm_d, m_ssm_w_glu, m_ssm_b_glu, m_ssm_w_out, m_conv_w, m_conv_b, m_conv_w_out, m_w_o, m_ffn2_norm, m_ffn2_w_gate, m_ffn2_w_up, m_ffn2_w_down, m_final_norm, v_ffn1_norm, v_ffn1_w_gate, v_ffn1_w_up, v_ffn1_w_down, v_mix_norm, v_w_in, v_ssm_lambda_re, v_ssm_lambda_im, v_ssm_log_dt, v_ssm_b_re, v_ssm_b_im, v_ssm_c_re, v_ssm_c_im, v_ssm_d, v_ssm_w_glu, v_ssm_b_glu, v_ssm_w_out, v_conv_w, v_conv_b, v_conv_w_out, v_w_o, v_ffn2_norm, v_ffn2_w_gate, v_ffn2_w_up, v_ffn2_w_down, v_final_norm):
    P = dict(ffn1_norm=ffn1_norm, ffn1_w_gate=ffn1_w_gate, ffn1_w_up=ffn1_w_up, ffn1_w_down=ffn1_w_down, mix_norm=mix_norm, w_in=w_in, ssm_lambda_re=ssm_lambda_re, ssm_lambda_im=ssm_lambda_im, ssm_log_dt=ssm_log_dt, ssm_b_re=ssm_b_re, ssm_b_im=ssm_b_im, ssm_c_re=ssm_c_re, ssm_c_im=ssm_c_im, ssm_d=ssm_d, ssm_w_glu=ssm_w_glu, ssm_b_glu=ssm_b_glu, ssm_w_out=ssm_w_out, conv_w=conv_w, conv_b=conv_b, conv_w_out=conv_w_out, w_o=w_o, ffn2_norm=ffn2_norm, ffn2_w_gate=ffn2_w_gate, ffn2_w_up=ffn2_w_up, ffn2_w_down=ffn2_w_down, final_norm=final_norm)
    M = dict(ffn1_norm=m_ffn1_norm, ffn1_w_gate=m_ffn1_w_gate, ffn1_w_up=m_ffn1_w_up, ffn1_w_down=m_ffn1_w_down, mix_norm=m_mix_norm, w_in=m_w_in, ssm_lambda_re=m_ssm_lambda_re, ssm_lambda_im=m_ssm_lambda_im, ssm_log_dt=m_ssm_log_dt, ssm_b_re=m_ssm_b_re, ssm_b_im=m_ssm_b_im, ssm_c_re=m_ssm_c_re, ssm_c_im=m_ssm_c_im, ssm_d=m_ssm_d, ssm_w_glu=m_ssm_w_glu, ssm_b_glu=m_ssm_b_glu, ssm_w_out=m_ssm_w_out, conv_w=m_conv_w, conv_b=m_conv_b, conv_w_out=m_conv_w_out, w_o=m_w_o, ffn2_norm=m_ffn2_norm, ffn2_w_gate=m_ffn2_w_gate, ffn2_w_up=m_ffn2_w_up, ffn2_w_down=m_ffn2_w_down, final_norm=m_final_norm)
    V = dict(ffn1_norm=v_ffn1_norm, ffn1_w_gate=v_ffn1_w_gate, ffn1_w_up=v_ffn1_w_up, ffn1_w_down=v_ffn1_w_down, mix_norm=v_mix_norm, w_in=v_w_in, ssm_lambda_re=v_ssm_lambda_re, ssm_lambda_im=v_ssm_lambda_im, ssm_log_dt=v_ssm_log_dt, ssm_b_re=v_ssm_b_re, ssm_b_im=v_ssm_b_im, ssm_c_re=v_ssm_c_re, ssm_c_im=v_ssm_c_im, ssm_d=v_ssm_d, ssm_w_glu=v_ssm_w_glu, ssm_b_glu=v_ssm_b_glu, ssm_w_out=v_ssm_w_out, conv_w=v_conv_w, conv_b=v_conv_b, conv_w_out=v_conv_w_out, w_o=v_w_o, ffn2_norm=v_ffn2_norm, ffn2_w_gate=v_ffn2_w_gate, ffn2_w_up=v_ffn2_w_up, ffn2_w_down=v_ffn2_w_down, final_norm=v_final_norm)
    names = list(P)
    sharded = ["ffn1_w_gate", "ffn1_w_up", "ffn1_w_down", "w_in", "ssm_w_glu", "ssm_w_out", "conv_w_out", "w_o",
               "ffn2_w_gate", "ffn2_w_up", "ffn2_w_down"]
    replicated = [n for n in names if n not in sharded and n != "conv_w"]

    S, D = x.shape[1], x.shape[2]
    W = ssm_d.shape[0]
    Dc = D // NDEV
    G, N = ssm_lambda_re.shape
    GN = G * N
    rows = GN // LANE
    xh = x.reshape(S, D)
    target = loss_target.reshape(S, D)
    xi, yi, ci = lax.axis_index("x"), lax.axis_index("y"), lax.axis_index("c")
    c_arr = jnp.reshape(ci, (1,)).astype(I32)
    q_arr = jnp.reshape(2 * xi + yi, (1,)).astype(I32)
    row = lambda v: v.reshape(1, -1)

    wg1, wu1, wd1 = _all_gather("gather_ffn1", [P[n].astype(BF) for n in sharded[0:3]])
    w_in_f, w_glu_f, w_so, w_co, w_o_f, cw_f = _all_gather(
        "gather_mixer", [P[n].astype(BF) for n in sharded[3:8]] + [conv_w])
    wg2, wu2, wd2 = _all_gather("gather_ffn2", [P[n].astype(BF) for n in sharded[8:11]])
    w_glu_f = w_glu_f.reshape(W, W)
    w_o_f = w_o_f.reshape(D, D)
    cw = jnp.transpose(cw_f, (1, 0, 2)).reshape(3, W)

    tm = _pick(S, 512, SUBLANE)
    tk = _pick(D, 1024, LANE)
    ts = _pick(S, 512, SUBLANE)
    tn = _pick(D, 1024, LANE)
    b4 = lambda f: (lambda b, i, j, k: f(b, i, j, k))

    u1, r1 = _rms_fwd("rms1", xh, row(ffn1_norm))
    h1, ffn1_saved = _ffn_fwd("ffn1", u1, xh, wg1, wu1, wd1)
    u2, r2 = _rms_fwd("rms2", h1, row(mix_norm))
    (proj,) = _mm(
        "in_proj", (NDEV, S // tm, 1, D // tk),
        [(u2, (tm, tk), lambda b, i, j, k: (i, k)), (w_in_f, (None, tk, W), lambda b, i, j, k: (b, k, 0))],
        [(0, 1, NN, 0)], [], [((NDEV, S, W), F32, (None, tm, W), lambda b, i, j, k: (b, i, 0))], _plain, [(tm, W)])

    ssm_in = (ssm_lambda_re, ssm_lambda_im, ssm_log_dt, ssm_b_re, ssm_b_im, ssm_c_re, ssm_c_im)
    (a_re, a_im, bd_re, bd_im, cd_re, cd_imn), ssm_vjp = jax.vjp(_ssm_params, *ssm_in)
    tiles, tch, tst = bd_re.shape
    bd_re_b, bd_im_b, cd_re_b, cd_imn_b = (t.astype(BF) for t in (bd_re, bd_im, cd_re, cd_imn))
    v_bf = proj[0].astype(BF)
    st_blk = lambda arr: (arr, (tm, tst), lambda b, i, j, k: (i, b))
    ch_blk = lambda arr: (arr, (tm, tch), lambda b, i, j, k: (i, b))
    bd_blk = lambda arr: (arr, (None, tch, tst), lambda b, i, j, k: (b, 0, 0))
    cd_blk = lambda arr: (arr, (None, tst, tch), lambda b, i, j, k: (b, 0, 0))
    st_out = lambda dt: ((S, GN), dt, (tm, tst), lambda b, i, j, k: (i, b))
    ch_out = lambda dt: ((S, W), dt, (tm, tch), lambda b, i, j, k: (i, b))
    d_blk = (row(ssm_d), (1, tch), lambda b, i, j, k: (0, b))
    v_blk = (proj, (None, tm, tch), lambda b, i, j, k: (0, i, b))

    bu_re, bu_im = _mm("ssm_bu", (tiles, S // tm, 1, 1), [ch_blk(v_bf), bd_blk(bd_re_b), bd_blk(bd_im_b)],
                       [(0, 1, NN, 0), (0, 2, NN, 1)], [], [st_out(F32), st_out(F32)], _plain, [(tm, tst)] * 2)
    s_re3, s_im3 = _scan_fwd(bu_re.reshape(S, rows, LANE), bu_im.reshape(S, rows, LANE), a_re, a_im)
    s_re_b = s_re3.reshape(S, GN).astype(BF)
    s_im_b = s_im3.reshape(S, GN).astype(BF)

    def y0_epilogue(accs, sides):
        y0 = accs[0] + sides[1] * sides[0]
        return y0, _gelu(y0)

    y0, y1 = _mm("ssm_y0", (tiles, S // tm, 1, 1),
                 [st_blk(s_re_b), st_blk(s_im_b), cd_blk(cd_re_b), cd_blk(cd_imn_b)],
                 [(0, 2, NN, 0), (1, 3, NN, 0)], [v_blk, d_blk], [ch_out(F32), ch_out(BF)], y0_epilogue, [(tm, tch)])

    tw = _pick(W, 512, LANE)

    def glu_epilogue(accs, sides):
        q = accs[0] + sides[1]
        return q, _gelu(sides[0]) * _sigmoid(q)

    q_pre, y2 = _mm("ssm_glu", (1, S // tm, W // tw, 1),
                    [(y1, (tm, W), lambda b, i, j, k: (i, 0)), (w_glu_f, (W, tw), lambda b, i, j, k: (0, j))],
                    [(0, 1, NN, 0)],
                    [(y0, (tm, tw), lambda b, i, j, k: (i, j)), (row(ssm_b_glu), (1, tw), lambda b, i, j, k: (0, j))],
                    [((S, W), F32, (tm, tw), lambda b, i, j, k: (i, j)), ((S, W), BF, (tm, tw), lambda b, i, j, k: (i, j))],
                    glu_epilogue, [(tm, tw)])

    yb, conv = _conv_fwd(proj, cw, row(conv_b))

    per = W // Dc
    ga_blk = (proj, (None, tm, Dc), lambda b, i, j, k: (4 + b // per, i, b % per))
    gb_blk = (proj, (None, tm, Dc), lambda b, i, j, k: (6 + b // per, i, b % per))
    dc_out = ((S, D), BF, (tm, Dc), lambda b, i, j, k: (i, b))

    def merge_epilogue(accs, sides):
        za, zb = accs
        return _sigmoid(sides[0]) * za + _sigmoid(sides[1]) * zb, za, zb

    merged, z_a, z_b = _mm(
        "mix_merge", (NDEV, S // tm, 1, 1),
        [(y2, (tm, W), lambda b, i, j, k: (i, 0)), (yb, (tm, W), lambda b, i, j, k: (i, 0)),
         (w_so, (None, W, Dc), lambda b, i, j, k: (b, 0, 0)), (w_co, (None, W, Dc), lambda b, i, j, k: (b, 0, 0))],
        [(0, 2, NN, 0), (1, 3, NN, 1)], [ga_blk, gb_blk], [dc_out, dc_out, dc_out], merge_epilogue, [(tm, Dc)] * 2)

    (h2,) = _mm("mix_out", (1, S // tm, D // tn, D // tk),
                [(merged, (tm, tk), lambda b, i, j, k: (i, k)), (w_o_f, (tk, tn), lambda b, i, j, k: (k, j))],
                [(0, 1, NN, 0)], [(h1, (tm, tn), lambda b, i, j, k: (i, j))],
                [((S, D), F32, (tm, tn), lambda b, i, j, k: (i, j))],
                lambda accs, sides: [sides[0] + accs[0]], [(tm, tn)])

    u3, r3 = _rms_fwd("rms3", h2, row(ffn2_norm))
    h3, ffn2_saved = _ffn_fwd("ffn2", u3, h2, wg2, wu2, wd2)
    loss_vec, dh3, dh3_half, d_final_norm = _loss_head("loss_head", h3, row(final_norm), target)
    loss = lax.psum(loss_vec[0, 0], ("x", "y", "c"))

    du3, dwg2, dwu2, dwd2 = _ffn_bwd("ffn2", dh3_half, u3, ffn2_saved, wg2, wu2, wd2)
    dh2, dh2_b, d_ffn2_norm = _rms_bwd("rms3_bwd", du3, h2, r3, row(ffn2_norm), dh3, 1.0)

    dg_out = ((2, S, W), BF, (None, tm, Dc), lambda b, i, j, k: (j // per, i, j % per))
    ga_blk2 = (proj, (None, tm, Dc), lambda b, i, j, k: (4 + j // per, i, j % per))
    gb_blk2 = (proj, (None, tm, Dc), lambda b, i, j, k: (6 + j // per, i, j % per))
    dcj = lambda arr: (arr, (tm, Dc), lambda b, i, j, k: (i, j))
    dcj_out = ((S, D), BF, (tm, Dc), lambda b, i, j, k: (i, j))

    def dmerge_epilogue(accs, sides):
        dm = accs[0]
        sa, sb = _sigmoid(sides[0]), _sigmoid(sides[1])
        za, zb = sides[2].astype(F32), sides[3].astype(F32)
        return dm * sa, dm * sb, dm * za * sa * (1.0 - sa), dm * zb * sb * (1.0 - sb)

    dz_a, dz_b, dga, dgb = _mm(
        "mix_out_dx", (1, S // tm, NDEV, D // tk),
        [(dh2_b, (tm, tk), lambda b, i, j, k: (i, k)), (w_o_f, (Dc, tk), lambda b, i, j, k: (j, k))],
        [(0, 1, NT, 0)], [ga_blk2, gb_blk2, dcj(z_a), dcj(z_b)], [dcj_out, dcj_out, dg_out, dg_out],
        dmerge_epilogue, [(tm, Dc)])

    td = _pick(D, 512, LANE)
    (dw_o,) = _mm("mix_out_dw", (1, D // td, D // tn, S // ts),
                  [(merged, (ts, td), lambda b, i, j, k: (k, i)), (dh2_b, (ts, tn), lambda b, i, j, k: (k, j))],
                  [(0, 1, TN, 0)], [], [((D, D), BF, (td, tn), lambda b, i, j, k: (i, j))], _plain, [(td, tn)])

    wout = ((NDEV, W, Dc), BF, (None, W, Dc), lambda b, i, j, k: (b, 0, 0))
    dw_so, dw_co = _mm(
        "mix_merge_dw", (NDEV, 1, 1, S // ts),
        [(y2, (ts, W), lambda b, i, j, k: (k, 0)), (yb, (ts, W), lambda b, i, j, k: (k, 0)),
         (dz_a, (ts, Dc), lambda b, i, j, k: (k, b)), (dz_b, (ts, Dc), lambda b, i, j, k: (k, b))],
        [(0, 2, TN, 0), (1, 3, TN, 1)], [], [wout, wout], _plain, [(W, Dc)] * 2)

    def dglu_epilogue(accs, sides):
        dy2, dyb = accs
        sq = _sigmoid(sides[1])
        return dy2 * _gelu(sides[0]) * sq * (1.0 - sq), dy2 * sq, dyb

    full_w = lambda arr: (arr, (tm, W), lambda b, i, j, k: (i, 0))
    full_w_out = lambda dt: ((S, W), dt, (tm, W), lambda b, i, j, k: (i, 0))
    dq, dy1p, dyb = _mm(
        "mix_merge_dx", (1, S // tm, 1, NDEV),
        [(dz_a, (tm, Dc), lambda b, i, j, k: (i, k)), (dz_b, (tm, Dc), lambda b, i, j, k: (i, k)),
         (w_so, (None, W, Dc), lambda b, i, j, k: (k, 0, 0)), (w_co, (None, W, Dc), lambda b, i, j, k: (k, 0, 0))],
        [(0, 2, NT, 0), (1, 3, NT, 1)], [full_w(y0), full_w(q_pre)], [full_w_out(BF), full_w_out(F32), full_w_out(F32)],
        dglu_epilogue, [(tm, W)] * 2)

    def dy0_epilogue(accs, sides):
        dy0 = (sides[0] + accs[0]) * _gelu_grad(sides[1])
        return dy0, dy0

    wj = lambda arr: (arr, (tm, tw), lambda b, i, j, k: (i, j))
    dy0, dy0_b = _mm("ssm_glu_dx", (1, S // tm, W // tw, 1),
                     [(dq, (tm, W), lambda b, i, j, k: (i, 0)), (w_glu_f, (tw, W), lambda b, i, j, k: (j, 0))],
                     [(0, 1, NT, 0)], [wj(dy1p), wj(y0)],
                     [((S, W), F32, (tm, tw), lambda b, i, j, k: (i, j)), ((S, W), BF, (tm, tw), lambda b, i, j, k: (i, j))],
                     dy0_epilogue, [(tm, tw)])

    (dw_glu,) = _mm("ssm_glu_dw", (1, W // tw, 1, S // ts),
                    [(y1, (ts, tw), lambda b, i, j, k: (k, i)), (dq, (ts, W), lambda b, i, j, k: (k, 0))],
                    [(0, 1, TN, 0)], [], [((W, W), BF, (tw, W), lambda b, i, j, k: (i, 0))], _plain, [(tw, W)])

    tr = _pick(S, 256, SUBLANE)
    rw = ((tr, W), lambda i: (i, 0))
    vec_w = ((1, W), F32, (1, W), lambda i: (0, 0))
    d_b_glu, d_ssm_d = _ew(
        "ssm_colsums", (S // tr,), [(dq,) + rw, (dy0,) + rw, (proj, (None, tr, W), lambda i: (0, i, 0))],
        [vec_w, vec_w],
        lambda dqv, dyv, vv: (jnp.sum(dqv.astype(F32), axis=0, keepdims=True), jnp.sum(dyv * vv, axis=0, keepdims=True)),
        acc=(0, 1))

    ds_re, ds_im = _mm("ssm_ds", (tiles, S // tm, 1, 1), [ch_blk(dy0_b), cd_blk(cd_re_b), cd_blk(cd_imn_b)],
                       [(0, 1, NT, 0), (0, 2, NT, 1)], [], [st_out(F32), st_out(F32)], _plain, [(tm, tst)] * 2)
    lam_re3, lam_im3, da_re, da_im = _scan_bwd(ds_re.reshape(S, rows, LANE), ds_im.reshape(S, rows, LANE),
                                               s_re3, s_im3, a_re, a_im)
    lam_re_b = lam_re3.reshape(S, GN).astype(BF)
    lam_im_b = lam_im3.reshape(S, GN).astype(BF)

    (dv,) = _mm("ssm_dv", (tiles, S // tm, 1, 1),
                [st_blk(lam_re_b), st_blk(lam_im_b), bd_blk(bd_re_b), bd_blk(bd_im_b)],
                [(0, 2, NT, 0), (1, 3, NT, 0)], [ch_blk(dy0), d_blk], [ch_out(BF)],
                lambda accs, sides: [accs[0] + sides[0] * sides[1]], [(tm, tch)])

    tok_ch = lambda arr: (arr, (ts, tch), lambda b, i, j, k: (k, b))
    tok_st = lambda arr: (arr, (ts, tst), lambda b, i, j, k: (k, b))
    bd_out = ((tiles, tch, tst), F32, (None, tch, tst), lambda b, i, j, k: (b, 0, 0))
    cd_out = ((tiles, tst, tch), F32, (None, tst, tch), lambda b, i, j, k: (b, 0, 0))
    dbd_re, dbd_im = _mm("ssm_dbd", (tiles, 1, 1, S // ts), [tok_ch(v_bf), tok_st(lam_re_b), tok_st(lam_im_b)],
                         [(0, 1, TN, 0), (0, 2, TN, 1)], [], [bd_out, bd_out], _plain, [(tch, tst)] * 2)
    dcd_re, dcd_imn = _mm("ssm_dcd", (tiles, 1, 1, S // ts), [tok_st(s_re_b), tok_st(s_im_b), tok_ch(dy0_b)],
                          [(0, 2, TN, 0), (1, 2, TN, 1)], [], [cd_out, cd_out], _plain, [(tst, tch)] * 2)
    d_ssm = ssm_vjp((da_re, da_im, dbd_re, dbd_im, dcd_re, dcd_imn))

    dbg, dcg, dval, d_conv_w_full, d_conv_b = _conv_bwd(dyb, proj, conv, cw)
    dproj = jnp.concatenate([dv[None], dbg[None], dcg[None], dval[None], dga, dgb], axis=0)

    (dw_in,) = _mm("in_proj_dw", (NDEV, D // td, 1, S // ts),
                   [(u2, (ts, td), lambda b, i, j, k: (k, i)), (dproj, (None, ts, W), lambda b, i, j, k: (b, k, 0))],
                   [(0, 1, TN, 0)], [], [((NDEV, D, W), BF, (None, td, W), lambda b, i, j, k: (b, i, 0))],
                   _plain, [(td, W)])
    (du2,) = _mm("in_proj_dx", (1, S // tm, D // tn, NDEV),
                 [(dproj, (None, tm, W), lambda b, i, j, k: (k, i, 0)), (w_in_f, (None, tn, W), lambda b, i, j, k: (k, j, 0))],
                 [(0, 1, NT, 0)], [], [((S, D), F32, (tm, tn), lambda b, i, j, k: (i, j))], _plain, [(tm, tn)])
    dh1, dh1_half, d_mix_norm = _rms_bwd("rms2_bwd", du2, h1, r2, row(mix_norm), dh2, 0.5)

    du1, dwg1, dwu1, dwd1 = _ffn_bwd("ffn1", dh1_half, u1, ffn1_saved, wg1, wu1, wd1)
    dx, _, d_ffn1_norm = _rms_bwd("rms1_bwd", du1, xh, r1, row(ffn1_norm), dh1, 1.0)

    partial = dict(ffn1_w_gate=dwg1, ffn1_w_up=dwu1, ffn1_w_down=dwd1, w_in=dw_in,
                   ssm_w_glu=dw_glu.reshape(NDEV, W // NDEV, W), ssm_w_out=dw_so, conv_w_out=dw_co,
                   w_o=dw_o.reshape(NDEV, Dc, D), ffn2_w_gate=dwg2, ffn2_w_up=dwu2, ffn2_w_down=dwd2)
    grads, deltas, new_m, new_v = {}, {}, {}, {}
    for tag, group in (("ffn2", sharded[8:11]), ("mixer", sharded[3:8]), ("ffn1", sharded[0:3])):
        lands = _exchange_sibling("rs_sibling_" + tag, [partial[n] for n in group])
        sums = [_sum_sibling("rs_sum_" + n, partial[n], land, c_arr) for n, land in zip(group, lands)]
        lands2 = _exchange_chips("rs_chips_" + tag, sums)
        for n, sm, land2 in zip(group, sums, lands2):
            grads[n], deltas[n], new_m[n], new_v[n] = _finish_sharded("adamw_" + n, sm, land2, q_arr, P[n], M[n], V[n])

    small = dict(ffn1_norm=d_ffn1_norm, mix_norm=d_mix_norm, ffn2_norm=d_ffn2_norm, final_norm=d_final_norm,
                 ssm_lambda_re=d_ssm[0], ssm_lambda_im=d_ssm[1], ssm_log_dt=d_ssm[2], ssm_b_re=d_ssm[3],
                 ssm_b_im=d_ssm[4], ssm_c_re=d_ssm[5], ssm_c_im=d_ssm[6], ssm_d=d_ssm_d, ssm_b_glu=d_b_glu,
                 conv_b=d_conv_b)
    sizes = [P[n].size for n in replicated]
    total = sum(sizes) + d_conv_w_full.size
    padded = -(-total // (SUBLANE * LANE)) * (SUBLANE * LANE)

    def pack(parts):
        flat = jnp.concatenate([p.reshape(-1).astype(F32) for p in parts])
        return jnp.pad(flat, (0, padded - flat.shape[0])).reshape(padded // LANE, LANE)

    cw_zero = jnp.zeros_like(d_conv_w_full)
    (gathered,) = _all_gather("gather_small_grads", [pack([small[n] for n in replicated] + [d_conv_w_full])])
    g_pk, d_pk, m_pk, v_pk = _finish_replicated(
        "adamw_replicated", gathered, pack([P[n] for n in replicated] + [cw_zero]),
        pack([M[n] for n in replicated] + [cw_zero]), pack([V[n] for n in replicated] + [cw_zero + 1.0]))
    off = 0
    for n, sz in zip(replicated, sizes):
        for store, pk in ((grads, g_pk), (deltas, d_pk), (new_m, m_pk), (new_v, v_pk)):
            store[n] = pk.reshape(-1)[off:off + sz].reshape(P[n].shape)
        off += sz
    g_cw_full = g_pk.reshape(-1)[off:off + d_conv_w_full.size].reshape(d_conv_w_full.shape)
    cwl = conv_w.shape[1]
    g_cw = lax.dynamic_slice_in_dim(g_cw_full, (4 * xi + 2 * yi + ci) * cwl, cwl, axis=1)
    full3 = ((3, cwl), lambda i: (0, 0))
    grads["conv_w"], deltas["conv_w"], new_m["conv_w"], new_v["conv_w"] = _ew(
        "adamw_conv_w", (1,), [(g_cw,) + full3, (conv_w,) + full3, (m_conv_w,) + full3, (v_conv_w,) + full3],
        [((3, cwl), F32) + full3] * 4, lambda g, w, m, v: (g,) + _adamw(w, g, m, v))

    return (loss, dx.reshape(x.shape), *[grads[n] for n in names], *[deltas[n] for n in names],
            *[new_m[n] for n in names], *[new_v[n] for n in names])
```

```python
import functools
import math

import jax
import jax.numpy as jnp
from jax import lax
from jax.experimental import pallas as pl
from jax.experimental.pallas import tpu as pltpu

F32 = jnp.float32
BF = jnp.bfloat16
I32 = jnp.int32
MESH = pl.DeviceIdType.MESH
LANE = 128
SUBLANE = 8
NDEV = 8
EPS = 1e-6
ADAM_LR, ADAM_B1, ADAM_B2, ADAM_EPS, ADAM_WD, ADAM_STEP = 0.001, 0.9, 0.999, 1e-08, 0.01, 10
NN = ((1,), (0,))
NT = ((1,), (1,))
TN = ((0,), (0,))
HBM = pl.BlockSpec(memory_space=pltpu.HBM)


def _pick(n, pref, mult):
    t = min(pref, n)
    t -= t % mult
    while t >= mult:
        if n % t == 0:
            return t
        t -= mult
    return n


def _sigmoid(x):
    return 1.0 / (1.0 + jnp.exp(-x))


_GELU_C = math.sqrt(2.0 / math.pi)


def _gelu(x):
    return 0.5 * x * (1.0 + jnp.tanh(_GELU_C * (x + 0.044715 * x * x * x)))


def _gelu_grad(x):
    t = jnp.tanh(_GELU_C * (x + 0.044715 * x * x * x))
    return 0.5 * (1.0 + t) + 0.5 * x * (1.0 - t * t) * _GELU_C * (1.0 + 3.0 * 0.044715 * x * x)


def _dep_specs(deps, rank):
    return [(d, d.shape, lambda *_, nd=d.ndim: (0,) * nd) for d in deps]


def _mm(name, grid, ops, pairs, sides, outs, epilogue, acc_shapes, deps=()):
    nk = grid[-1]
    n_ops, n_sides, n_outs = len(ops), len(sides), len(outs)
    dep_specs = _dep_specs(deps, len(grid))
    n_deps = len(dep_specs)

    def body(*refs):
        op_refs = refs[:n_ops]
        side_refs = refs[n_ops:n_ops + n_sides]
        out_refs = refs[n_ops + n_sides + n_deps:n_ops + n_sides + n_deps + n_outs]
        acc_refs = refs[n_ops + n_sides + n_deps + n_outs:]

        def partials():
            res = [None] * len(acc_shapes)
            for ia, ib, dims, ai in pairs:
                p = lax.dot_general(op_refs[ia][...], op_refs[ib][...], (dims, ((), ())),
                                    preferred_element_type=F32)
                res[ai] = p if res[ai] is None else res[ai] + p
            return res

        def finish(accs):
            vals = epilogue(accs, [s[...] for s in side_refs])
            for o, v in zip(out_refs, vals):
                o[...] = v.astype(o.dtype)

        if nk == 1:
            finish(partials())
        else:
            k = pl.program_id(len(grid) - 1)

            @pl.when(k == 0)
            def _():
                for a, p in zip(acc_refs, partials()):
                    a[...] = p

            @pl.when(k > 0)
            def _():
                for a, p in zip(acc_refs, partials()):
                    a[...] += p

            @pl.when(k == nk - 1)
            def _():
                finish([a[...] for a in acc_refs])

    return pl.pallas_call(
        body, name=name, grid=grid,
        in_specs=[pl.BlockSpec(b, m) for (_, b, m) in list(ops) + list(sides) + dep_specs],
        out_specs=[pl.BlockSpec(b, m) for (_, _, b, m) in outs],
        out_shape=[jax.ShapeDtypeStruct(s, d) for (s, d, _, _) in outs],
        scratch_shapes=[pltpu.VMEM(s, F32) for s in acc_shapes] if nk > 1 else [],
        compiler_params=pltpu.CompilerParams(
            dimension_semantics=("parallel",) * (len(grid) - 1) + ("arbitrary",)),
    )(*[a for (a, _, _) in list(ops) + list(sides) + dep_specs])


def _ew(name, grid, ins, outs, fn, acc=(), deps=()):
    n_in = len(ins)
    dep_specs = _dep_specs(deps, len(grid))

    def body(*refs):
        vals = fn(*[r[...] for r in refs[:n_in]])
        first = pl.program_id(0) == 0
        for idx, (o, v) in enumerate(zip(refs[n_in + len(dep_specs):], vals)):
            if idx in acc:
                @pl.when(first)
                def _(o=o, v=v):
                    o[...] = v.astype(o.dtype)

                @pl.when(jnp.logical_not(first))
                def _(o=o, v=v):
                    o[...] += v.astype(o.dtype)
            else:
                o[...] = v.astype(o.dtype)

    return pl.pallas_call(
        body, name=name, grid=grid,
        in_specs=[pl.BlockSpec(b, m) for (_, b, m) in list(ins) + dep_specs],
        out_specs=[pl.BlockSpec(b, m) for (_, _, b, m) in outs],
        out_shape=[jax.ShapeDtypeStruct(s, d) for (s, d, _, _) in outs],
        compiler_params=pltpu.CompilerParams(
            dimension_semantics=(("arbitrary",) if acc else ("parallel",)) * len(grid)),
    )(*[a for (a, _, _) in list(ins) + dep_specs])


def _position():
    x, y, c = lax.axis_index("x"), lax.axis_index("y"), lax.axis_index("c")
    chips = [(1 - x, y), (x, 1 - y), (1 - x, 1 - y)]
    return x, y, c, chips


def _all_gather(name, shards):
    n = len(shards)

    def body(*refs):
        xs, outs = refs[:n], refs[n:2 * n]
        send_sems, recv_sems, local_sems = refs[2 * n:]
        x, y, c, chips = _position()
        me, sibling = (x, y, c), (x, y, 1 - c)

        def copy(a, k, block, to, src=None):
            dst = outs[a].at[4 * block[0] + 2 * block[1] + block[2]]
            return pltpu.make_async_remote_copy(
                src_ref=dst if src is None else src, dst_ref=dst,
                send_sem=send_sems.at[a, k], recv_sem=recv_sems.at[a, k],
                device_id=to, device_id_type=MESH)

        mine = [pltpu.make_async_copy(xs[a], outs[a].at[4 * x + 2 * y + c], local_sems.at[a]) for a in range(n)]
        for cp in mine:
            cp.start()
        first = []
        for a in range(n):
            first.append(copy(a, 0, me, sibling, src=xs[a]))
            first += [copy(a, 1 + j, me, (*chip, c), src=xs[a]) for j, chip in enumerate(chips)]
        for cp in first:
            cp.start()
        passed = []
        for a in range(n):
            for j, chip in enumerate(chips):
                copy(a, 1 + j, (*chip, c), me).wait_recv()
                cp = copy(a, 4 + j, (*chip, c), sibling)
                cp.start()
                passed.append(cp)
        for a in range(n):
            copy(a, 0, sibling, me).wait_recv()
            for j, chip in enumerate(chips):
                copy(a, 4 + j, (*chip, 1 - c), me).wait_recv()
        for cp in first + passed:
            cp.wait_send()
        for cp in mine:
            cp.wait()

    return pl.pallas_call(
        body, name=name,
        out_shape=[jax.ShapeDtypeStruct((NDEV,) + s.shape, s.dtype) for s in shards],
        in_specs=[HBM] * n, out_specs=[HBM] * n,
        scratch_shapes=[pltpu.SemaphoreType.DMA((n, 7)), pltpu.SemaphoreType.DMA((n, 7)),
                        pltpu.SemaphoreType.DMA((n,))],
    )(*shards)


SEM = pl.BlockSpec(memory_space=pltpu.SEMAPHORE)
EFFECT = pltpu.SideEffectType.DATAFLOW_SIDE_EFFECTING


def _in_hbm(v):
    return pltpu.with_memory_space_constraint(v, pltpu.HBM)


def _split_start(name, srcs, lands, make_copies, n_per):
    n = len(srcs)

    def body(*refs):
        send_sems, recv_sems = refs[2 * n], refs[2 * n + 1]
        for cp in make_copies(refs[:n], refs[n:2 * n], send_sems, recv_sems):
            cp.start()
        refs[-1][...] = jnp.zeros_like(refs[-1])

    outs = pl.pallas_call(
        body, name=name,
        out_shape=(pltpu.SemaphoreType.DMA((n * n_per,)), pltpu.SemaphoreType.DMA((n * n_per,)),
                   *[pltpu.HBM(v.shape, v.dtype) for v in list(srcs) + list(lands)],
                   jax.ShapeDtypeStruct((SUBLANE, LANE), F32)),
        in_specs=[HBM] * (2 * n),
        out_specs=(SEM, SEM, *[HBM] * (2 * n), pl.BlockSpec(memory_space=pltpu.VMEM)),
        input_output_aliases={i: 2 + i for i in range(2 * n)},
        compiler_params=pltpu.CompilerParams(has_side_effects=EFFECT),
    )(*[_in_hbm(v) for v in list(srcs) + list(lands)])
    return outs[0], outs[1], list(outs[2:2 + n]), list(outs[2 + n:2 + 2 * n]), outs[-1]


def _split_wait(name, started, make_copies, after):
    send_sems, recv_sems, srcs, lands, _ = started
    n = len(srcs)

    def body(*refs):
        for cp in make_copies(refs[:n], refs[n:2 * n], refs[2 * n], refs[2 * n + 1]):
            cp.wait_send()
            cp.wait_recv()

    order = [] if after is None else [after]
    outs = pl.pallas_call(
        body, name=name,
        out_shape=tuple(pltpu.HBM(v.shape, v.dtype) for v in srcs + lands),
        in_specs=[HBM] * (2 * n) + [SEM, SEM] + [pl.BlockSpec(memory_space=pl.ANY)] * len(order),
        out_specs=tuple([HBM] * (2 * n)),
        input_output_aliases={i: i for i in range(2 * n)},
        compiler_params=pltpu.CompilerParams(has_side_effects=EFFECT),
    )(*srcs, *lands, send_sems, recv_sems, *order)
    return list(outs[:n]), list(outs[n:])


def _gather_copies(xs, lands, send_sems, recv_sems):
    x, y, c, chips = _position()
    copies = []
    for a in range(len(xs)):
        for k, peer in enumerate([(x, y, 1 - c)] + [(*chip, c) for chip in chips]):
            copies.append(pltpu.make_async_remote_copy(
                src_ref=xs[a], dst_ref=lands[a].at[4 * x + 2 * y + c],
                send_sem=send_sems.at[4 * a + k], recv_sem=recv_sems.at[4 * a + k], device_id=peer, device_id_type=MESH))
    return copies


def _chips_copies(ps, lands, send_sems, recv_sems):
    x, y, c, chips = _position()
    copies = []
    for a in range(len(ps)):
        for j, chip in enumerate(chips):
            copies.append(pltpu.make_async_remote_copy(
                src_ref=ps[a].at[2 * chip[0] + chip[1]], dst_ref=lands[a].at[j],
                send_sem=send_sems.at[3 * a + j], recv_sem=recv_sems.at[3 * a + j], device_id=(*chip, c),
                device_id_type=MESH))
    return copies


def _gather_forward(name, shards, lands):
    n = len(shards)

    def body(*refs):
        xs, ins, outs = refs[:n], refs[n:2 * n], refs[2 * n:3 * n]
        send_sems, recv_sems, local_sems = refs[3 * n:]
        x, y, c, chips = _position()
        mine = [pltpu.make_async_copy(xs[a], outs[a].at[4 * x + 2 * y + c], local_sems.at[a]) for a in range(n)]
        passed, arrivals = [], []
        for a in range(n):
            for j, chip in enumerate(chips):
                slot = 4 * chip[0] + 2 * chip[1]
                passed.append(pltpu.make_async_remote_copy(
                    src_ref=ins[a].at[slot + c], dst_ref=outs[a].at[slot + c],
                    send_sem=send_sems.at[a, j], recv_sem=recv_sems.at[a, j],
                    device_id=(x, y, 1 - c), device_id_type=MESH))
                arrivals.append(pltpu.make_async_remote_copy(
                    src_ref=ins[a].at[slot + c], dst_ref=outs[a].at[slot + 1 - c],
                    send_sem=send_sems.at[a, j], recv_sem=recv_sems.at[a, j],
                    device_id=(x, y, 1 - c), device_id_type=MESH))
        for cp in mine + passed:
            cp.start()
        for cp in arrivals:
            cp.wait_recv()
        for cp in passed:
            cp.wait_send()
        for cp in mine:
            cp.wait()

    return pl.pallas_call(
        body, name=name,
        out_shape=[jax.ShapeDtypeStruct(l.shape, l.dtype) for l in lands],
        in_specs=[HBM] * (2 * n), out_specs=[HBM] * n,
        input_output_aliases={n + a: a for a in range(n)},
        scratch_shapes=[pltpu.SemaphoreType.DMA((n, 3)), pltpu.SemaphoreType.DMA((n, 3)),
                        pltpu.SemaphoreType.DMA((n,))],
    )(*shards, *lands)


def _exchange_sibling(name, grads):
    n = len(grads)

    def body(*refs):
        gs, lands = refs[:n], refs[n:2 * n]
        send_sems, recv_sems = refs[2 * n:]
        x, y, c, _ = _position()
        copies = []
        for a in range(n):
            for q in range(4):
                copies.append(pltpu.make_async_remote_copy(
                    src_ref=gs[a].at[2 * q + 1 - c], dst_ref=lands[a].at[q],
                    send_sem=send_sems.at[a, q], recv_sem=recv_sems.at[a, q],
                    device_id=(x, y, 1 - c), device_id_type=MESH))
        for cp in copies:
            cp.start()
        for cp in copies:
            cp.wait()

    return pl.pallas_call(
        body, name=name,
        out_shape=[jax.ShapeDtypeStruct((4,) + g.shape[1:], g.dtype) for g in grads],
        in_specs=[HBM] * n, out_specs=[HBM] * n,
        scratch_shapes=[pltpu.SemaphoreType.DMA((n, 4)), pltpu.SemaphoreType.DMA((n, 4))],
    )(*grads)


def _sum_sibling(name, g, land, c_arr):
    _, R, C = g.shape
    tr = _pick(R, 512, SUBLANE)

    def body(c_ref, g_ref, l_ref, o_ref):
        o_ref[...] = (g_ref[...].astype(F32) + l_ref[...].astype(F32)).astype(o_ref.dtype)

    return pl.pallas_call(
        body, name=name,
        grid_spec=pltpu.PrefetchScalarGridSpec(
            num_scalar_prefetch=1, grid=(4, R // tr),
            in_specs=[pl.BlockSpec((None, tr, C), lambda q, i, cr: (2 * q + cr[0], i, 0)),
                      pl.BlockSpec((None, tr, C), lambda q, i, cr: (q, i, 0))],
            out_specs=pl.BlockSpec((None, tr, C), lambda q, i, cr: (q, i, 0))),
        out_shape=jax.ShapeDtypeStruct((4, R, C), g.dtype),
        compiler_params=pltpu.CompilerParams(dimension_semantics=("parallel", "parallel")),
    )(c_arr, g, land)


def _adamw(w, g, m, v):
    m = ADAM_B1 * m + (1.0 - ADAM_B1) * g
    v = ADAM_B2 * v + (1.0 - ADAM_B2) * (g * g)
    m_hat = m / (1.0 - ADAM_B1 ** ADAM_STEP)
    v_hat = v / (1.0 - ADAM_B2 ** ADAM_STEP)
    delta = -ADAM_LR * (m_hat / (jnp.sqrt(v_hat) + ADAM_EPS) + ADAM_WD * w)
    return delta, m, v


def _finish_sharded(name, sums, land, q_arr, w, m, v):
    R, C = w.shape
    tr = _pick(R, 256, SUBLANE)

    def body(q_ref, p_ref, l_ref, w_ref, m_ref, v_ref, g_out, d_out, m_out, v_out):
        g = p_ref[...].astype(F32)
        for j in range(3):
            g = g + l_ref[j].astype(F32)
        d, mn, vn = _adamw(w_ref[...], g, m_ref[...], v_ref[...])
        g_out[...] = g
        d_out[...] = d
        m_out[...] = mn
        v_out[...] = vn

    row = pl.BlockSpec((tr, C), lambda i, qr: (i, 0))
    return pl.pallas_call(
        body, name=name,
        grid_spec=pltpu.PrefetchScalarGridSpec(
            num_scalar_prefetch=1, grid=(R // tr,),
            in_specs=[pl.BlockSpec((None, tr, C), lambda i, qr: (qr[0], i, 0)),
                      pl.BlockSpec((3, tr, C), lambda i, qr: (0, i, 0)), row, row, row],
            out_specs=[row] * 4),
        out_shape=[jax.ShapeDtypeStruct((R, C), F32)] * 4,
        compiler_params=pltpu.CompilerParams(dimension_semantics=("parallel",)),
    )(q_arr, sums, land, w, m, v)


def _finish_replicated(name, gathered, w, m, v):
    _, R, C = gathered.shape
    tr = _pick(R, 256, SUBLANE)

    def fn(gv, wv, mv, vv):
        g = gv[0]
        for d in range(1, NDEV):
            g = g + gv[d]
        dl, mn, vn = _adamw(wv, g, mv, vv)
        return g, dl, mn, vn

    row = ((tr, C), lambda i: (i, 0))
    return _ew(name, (R // tr,),
               [(gathered, (NDEV, tr, C), lambda i: (0, i, 0)), (w,) + row, (m,) + row, (v,) + row],
               [((R, C), F32) + row] * 4, fn)


def _rms_fwd(name, h, g, deps=()):
    S, D = h.shape
    tr = _pick(S, 256, SUBLANE)

    def fn(hv, gv):
        r = lax.rsqrt(jnp.mean(hv * hv, axis=-1, keepdims=True) + EPS)
        return hv * r * gv, r

    return _ew(name, (S // tr,),
               [(h, (tr, D), lambda i: (i, 0)), (g, (1, D), lambda i: (0, 0))],
               [((S, D), BF, (tr, D), lambda i: (i, 0)), ((S, 1), F32, (tr, 1), lambda i: (i, 0))], fn, deps=deps)


def _rms_bwd(name, du, h, r, g, dres, scale):
    S, D = h.shape
    tr = _pick(S, 256, SUBLANE)

    def fn(duv, hv, rv, gv, drv):
        xn = hv * rv
        dxn = duv * gv
        dh = drv + rv * (dxn - xn * jnp.mean(dxn * xn, axis=-1, keepdims=True))
        return dh, scale * dh, jnp.sum(duv * xn, axis=0, keepdims=True)

    row = ((tr, D), lambda i: (i, 0))
    return _ew(name, (S // tr,),
               [(du,) + row, (h,) + row, (r, (tr, 1), lambda i: (i, 0)), (g, (1, D), lambda i: (0, 0)), (dres,) + row],
               [((S, D), F32) + row, ((S, D), BF) + row, ((1, D), F32, (1, D), lambda i: (0, 0))], fn, acc=(2,))


def _loss_head(name, h, g, target):
    S, D = h.shape
    tr = _pick(S, 256, SUBLANE)

    def fn(hv, gv, tv):
        r = lax.rsqrt(jnp.mean(hv * hv, axis=-1, keepdims=True) + EPS)
        xn = hv * r
        diff = xn * gv - tv
        loss = 0.5 * jnp.sum(jnp.mean(diff * diff, axis=-1, keepdims=True))
        dout = diff / D
        dxn = dout * gv
        dh = r * (dxn - xn * jnp.mean(dxn * xn, axis=-1, keepdims=True))
        return (jnp.zeros((1, LANE), F32) + loss, dh, 0.5 * dh, jnp.sum(dout * xn, axis=0, keepdims=True))

    row = ((tr, D), lambda i: (i, 0))
    return _ew(name, (S // tr,),
               [(h,) + row, (g, (1, D), lambda i: (0, 0)), (target,) + row],
               [((1, LANE), F32, (1, LANE), lambda i: (0, 0)), ((S, D), F32) + row, ((S, D), BF) + row,
                ((1, D), F32, (1, D), lambda i: (0, 0))], fn, acc=(0, 3))


def _ffn_fwd(tag, u, h, wg, wu, wd):
    S, D = u.shape
    Fs = wg.shape[1]
    tm, tk = _pick(S, 512, SUBLANE), _pick(D, 1024, LANE)

    def up_epilogue(accs, sides):
        gt, up = accs
        return gt, up, gt * _sigmoid(gt) * up

    act = ((NDEV, S, Fs), BF, (None, tm, Fs), lambda b, i, j, k: (b, i, 0))
    gt, up, a = _mm(
        tag + "_up", (NDEV, S // tm, 1, D // tk),
        [(u, (tm, tk), lambda b, i, j, k: (i, k)),
         (wg, (None, Fs, tk), lambda b, i, j, k: (b, 0, k)), (wu, (None, Fs, tk), lambda b, i, j, k: (b, 0, k))],
        [(0, 1, NT, 0), (0, 2, NT, 1)], [], [act, act, act], up_epilogue, [(tm, Fs), (tm, Fs)])
    if callable(wd):
        wd = wd(a)
    tn = _pick(D, 1024, LANE)
    (hn,) = _mm(
        tag + "_down", (1, S // tm, D // tn, NDEV),
        [(a, (None, tm, Fs), lambda b, i, j, k: (k, i, 0)), (wd, (None, Fs, tn), lambda b, i, j, k: (k, 0, j))],
        [(0, 1, NN, 0)], [(h, (tm, tn), lambda b, i, j, k: (i, j))],
        [((S, D), F32, (tm, tn), lambda b, i, j, k: (i, j))],
        lambda accs, sides: [sides[0] + 0.5 * accs[0]], [(tm, tn)])
    return hn, (gt, up, a), wd


def _ffn_bwd(tag, dhs, u, saved, wg, wu, wd):
    gt, up, a = saved
    S, D = u.shape
    Fs = wg.shape[1]
    tm, tk = _pick(S, 512, SUBLANE), _pick(D, 1024, LANE)
    act_in = lambda arr: (arr, (None, tm, Fs), lambda b, i, j, k: (b, i, 0))
    act_out = ((NDEV, S, Fs), BF, (None, tm, Fs), lambda b, i, j, k: (b, i, 0))

    def act_epilogue(accs, sides):
        da = accs[0]
        gtv, upv = sides[0].astype(F32), sides[1].astype(F32)
        sg = _sigmoid(gtv)
        return da * upv * sg * (1.0 + gtv * (1.0 - sg)), da * gtv * sg

    dgt, dup = _mm(
        tag + "_dact", (NDEV, S // tm, 1, D // tk),
        [(dhs, (tm, tk), lambda b, i, j, k: (i, k)), (wd, (None, Fs, tk), lambda b, i, j, k: (b, 0, k))],
        [(0, 1, NT, 0)], [act_in(gt), act_in(up)], [act_out, act_out], act_epilogue, [(tm, Fs)])

    ts = _pick(S, 512, SUBLANE)
    tn = _pick(D, 1024, LANE)
    wgrad = ((NDEV, Fs, D), BF, (None, Fs, tn), lambda b, i, j, k: (b, 0, j))
    tok = lambda arr: (arr, (None, ts, Fs), lambda b, i, j, k: (b, k, 0))
    (dwd,) = _mm(
        tag + "_dwd", (NDEV, 1, D // tn, S // ts),
        [tok(a), (dhs, (ts, tn), lambda b, i, j, k: (k, j))],
        [(0, 1, TN, 0)], [], [wgrad], lambda accs, sides: accs, [(Fs, tn)])
    dwg, dwu = _mm(
        tag + "_dwgu", (NDEV, 1, D // tn, S // ts),
        [tok(dgt), tok(dup), (u, (ts, tn), lambda b, i, j, k: (k, j))],
        [(0, 2, TN, 0), (1, 2, TN, 1)], [], [wgrad, wgrad], lambda accs, sides: accs, [(Fs, tn), (Fs, tn)])

    def du(deps=()):
        return _mm(
            tag + "_du", (1, S // tm, D // tn, NDEV),
            [(dgt, (None, tm, Fs), lambda b, i, j, k: (k, i, 0)), (dup, (None, tm, Fs), lambda b, i, j, k: (k, i, 0)),
             (wg, (None, Fs, tn), lambda b, i, j, k: (k, 0, j)), (wu, (None, Fs, tn), lambda b, i, j, k: (k, 0, j))],
            [(0, 2, NN, 0), (1, 3, NN, 0)], [], [((S, D), F32, (tm, tn), lambda b, i, j, k: (i, j))],
            lambda accs, sides: accs, [(tm, tn)], deps=deps)[0]

    return du, dwg, dwu, dwd


def _ssm_params(lam_re, lam_im, log_dt, b_re, b_im, c_re, c_im):
    G, N = lam_re.shape
    C = b_re.shape[2]
    lam_re = jnp.minimum(lam_re, -1e-4)
    dt = jnp.exp(log_dt)[:, None]
    mag = jnp.exp(lam_re * dt)
    a_re = mag * jnp.cos(lam_im * dt)
    a_im = mag * jnp.sin(lam_im * dt)
    den = lam_re * lam_re + lam_im * lam_im
    p = a_re - 1.0
    f_re = ((p * lam_re + a_im * lam_im) / den)[:, :, None]
    f_im = ((a_im * lam_re - p * lam_im) / den)[:, :, None]
    bb_re = f_re * b_re - f_im * b_im
    bb_im = f_re * b_im + f_im * b_re
    gpt = LANE // C
    tiles = G // gpt
    eye = jnp.eye(gpt, dtype=F32)

    def bd(bb):
        return jnp.einsum("bgnc,gh->bgchn", bb.reshape(tiles, gpt, N, C), eye).reshape(tiles, gpt * C, gpt * N)

    def cd(cc):
        return jnp.einsum("bgcn,gh->bgnhc", cc.reshape(tiles, gpt, C, N), eye).reshape(tiles, gpt * N, gpt * C)

    rows = G * N // LANE
    return (a_re.reshape(rows, LANE), a_im.reshape(rows, LANE), bd(bb_re), bd(bb_im), cd(c_re), cd(-c_im))


def _scan_fwd(bu_re, bu_im, a_re, a_im):
    S, R, _ = bu_re.shape
    tc = _pick(S, 256, SUBLANE)

    def body(bre, bim, are, aim, sre, sim, carry):
        @pl.when(pl.program_id(0) == 0)
        def _():
            carry[...] = jnp.zeros_like(carry)

        ar, ai = are[...], aim[...]

        def step(t, c):
            pr, pi = c
            nr = ar * pr - ai * pi + bre[t]
            ni = ar * pi + ai * pr + bim[t]
            sre[t] = nr
            sim[t] = ni
            return nr, ni

        pr, pi = lax.fori_loop(0, tc, step, (carry[0], carry[1]), unroll=8)
        carry[0] = pr
        carry[1] = pi

    blk = pl.BlockSpec((tc, R, LANE), lambda i: (i, 0, 0))
    par = pl.BlockSpec((R, LANE), lambda i: (0, 0))
    return pl.pallas_call(
        body, name="ssm_scan_fwd", grid=(S // tc,),
        in_specs=[blk, blk, par, par], out_specs=[blk, blk],
        out_shape=[jax.ShapeDtypeStruct((S, R, LANE), F32)] * 2,
        scratch_shapes=[pltpu.VMEM((2, R, LANE), F32)],
        compiler_params=pltpu.CompilerParams(dimension_semantics=("arbitrary",)),
    )(bu_re, bu_im, a_re, a_im)


def _scan_bwd(ds_re, ds_im, s_re, s_im, a_re, a_im):
    S, R, _ = ds_re.shape
    tc = _pick(S, 256, SUBLANE)
    nc = S // tc

    def body(dre, dim_, sre, sim, are, aim, lre, lim, dar, dai, carry):
        @pl.when(pl.program_id(0) == 0)
        def _():
            carry[...] = jnp.zeros_like(carry)
            dar[...] = jnp.zeros_like(dar)
            dai[...] = jnp.zeros_like(dai)

        ar, ai = are[...], aim[...]

        def step(tt, c):
            t = tc - 1 - tt
            lr, li, gr, gi = c
            sr, si = sre[t], sim[t]
            gr = gr + lr * sr + li * si
            gi = gi + li * sr - lr * si
            nlr = dre[t] + ar * lr + ai * li
            nli = dim_[t] + ar * li - ai * lr
            lre[t] = nlr
            lim[t] = nli
            return nlr, nli, gr, gi

        lr, li, gr, gi = lax.fori_loop(0, tc, step, (carry[0], carry[1], dar[...], dai[...]), unroll=8)
        carry[0] = lr
        carry[1] = li
        dar[...] = gr
        dai[...] = gi

    blk = pl.BlockSpec((tc, R, LANE), lambda i: (nc - 1 - i, 0, 0))
    par = pl.BlockSpec((R, LANE), lambda i: (0, 0))
    return pl.pallas_call(
        body, name="ssm_scan_bwd", grid=(nc,),
        in_specs=[blk, blk, blk, blk, par, par], out_specs=[blk, blk, par, par],
        out_shape=[jax.ShapeDtypeStruct((S, R, LANE), F32)] * 2 + [jax.ShapeDtypeStruct((R, LANE), F32)] * 2,
        scratch_shapes=[pltpu.VMEM((2, R, LANE), F32)],
        compiler_params=pltpu.CompilerParams(dimension_semantics=("arbitrary",)),
    )(ds_re, ds_im, s_re, s_im, a_re, a_im)


def _shift_down(z, k):
    t = lax.broadcasted_iota(I32, z.shape, 0)
    return jnp.where(t >= k, pltpu.roll(z, k, 0), 0.0)


def _shift_up(z, k):
    n = z.shape[0]
    t = lax.broadcasted_iota(I32, z.shape, 0)
    return jnp.where(t < n - k, pltpu.roll(z, n - k, 0), 0.0)


def _conv_fwd(proj, cw, cb):
    _, S, W = proj.shape
    ct = _pick(W, 256, LANE)

    def fn(bg, cg, val, w, b):
        z = cg * val
        conv = b + w[0:1] * _shift_down(z, 2) + w[1:2] * _shift_down(z, 1) + w[2:3] * z
        return bg * conv, conv

    sl = lambda s: (proj, (None, S, ct), lambda j, s=s: (s, 0, j))
    col = ((S, ct), lambda j: (0, j))
    return _ew("conv_fwd", (W // ct,),
               [sl(1), sl(2), sl(3), (cw, (3, ct), lambda j: (0, j)), (cb, (1, ct), lambda j: (0, j))],
               [((S, W), BF) + col, ((S, W), F32) + col], fn)


def _conv_bwd(dyb, proj, conv, cw):
    _, S, W = proj.shape
    ct = _pick(W, 256, LANE)

    def fn(dy, bg, cg, val, cv, w):
        z = cg * val
        z1, z2 = _shift_down(z, 1), _shift_down(z, 2)
        dconv = dy * bg
        dz = w[2:3] * dconv + w[1:2] * _shift_up(dconv, 1) + w[0:1] * _shift_up(dconv, 2)
        dw = jnp.concatenate([jnp.sum(dconv * z2, axis=0, keepdims=True), jnp.sum(dconv * z1, axis=0, keepdims=True),
                              jnp.sum(dconv * z, axis=0, keepdims=True)], axis=0)
        return dy * cv, dz * val, dz * cg, dw, jnp.sum(dconv, axis=0, keepdims=True)

    sl = lambda s: (proj, (None, S, ct), lambda j, s=s: (s, 0, j))
    col = ((S, ct), lambda j: (0, j))
    return _ew("conv_bwd", (W // ct,),
               [(dyb,) + col, sl(1), sl(2), sl(3), (conv,) + col, (cw, (3, ct), lambda j: (0, j))],
               [((S, W), BF) + col, ((S, W), BF) + col, ((S, W), BF) + col,
                ((3, W), F32, (3, ct), lambda j: (0, j)), ((1, W), F32, (1, ct), lambda j: (0, j))], fn)


def _plain(accs, sides):
    return accs


def kernel(x, ffn1_norm, ffn1_w_gate, ffn1_w_up, ffn1_w_down, mix_norm, w_in, ssm_lambda_re, ssm_lambda_im, ssm_log_dt, ssm_b_re, ssm_b_im, ssm_c_re, ssm_c_im, ssm_d, ssm_w_glu, ssm_b_glu, ssm_w_out, conv_w, conv_b, conv_w_out, w_o, ffn2_norm, ffn2_w_gate, ffn2_w_up, ffn2_w_down, final_norm, loss_target, m_ffn1_norm, m_ffn1_w_gate, m_ffn1_w_up, m_ffn1_w_down, m_mix_norm, m_w_in, m_ssm_lambda_re, m_ssm_lambda_im, m_ssm_log_dt, m_ssm_b_re, m_ssm_b_im, m_ssm_c_re, m_ssm_c_im, m_ssm_d, m_ssm_w_glu, m_ssm_b_glu, m_ssm_w_out, m_conv_w, m_conv_b, m_conv_w_out, m_w_o, m_ffn2_norm, m_ffn2_w_gate, m_ffn2_w_up, m_ffn2_w_down, m_final_norm, v_ffn1_norm, v_ffn1_w_gate, v_ffn1_w_up, v_ffn1_w_down, v_mix_norm, v_w_in, v_ssm_lambda_re, v_ssm_lambda_im, v_ssm_log_dt, v_ssm_b_re, v_ssm_b_im, v_ssm_c_re, v_ssm_c_im, v_ssm_d, v_ssm_w_glu, v_ssm_b_glu, v_ssm_w_out, v_conv_w, v_conv_b, v_conv_w_out, v_w_o, v_ffn2_norm, v_ffn2_w_gate, v_ffn2_w_up, v_ffn2_w_down, v_final_norm):
    P = dict(ffn1_norm=ffn1_norm, ffn1_w_gate=ffn1_w_gate, ffn1_w_up=ffn1_w_up, ffn1_w_down=ffn1_w_down, mix_norm=mix_norm, w_in=w_in, ssm_lambda_re=ssm_lambda_re, ssm_lambda_im=ssm_lambda_im, ssm_log_dt=ssm_log_dt, ssm_b_re=ssm_b_re, ssm_b_im=ssm_b_im, ssm_c_re=ssm_c_re, ssm_c_im=ssm_c_im, ssm_d=ssm_d, ssm_w_glu=ssm_w_glu, ssm_b_glu=ssm_b_glu, ssm_w_out=ssm_w_out, conv_w=conv_w, conv_b=conv_b, conv_w_out=conv_w_out, w_o=w_o, ffn2_norm=ffn2_norm, ffn2_w_gate=ffn2_w_gate, ffn2_w_up=ffn2_w_up, ffn2_w_down=ffn2_w_down, final_norm=final_norm)
    M = dict(ffn1_norm=m_ffn1_norm, ffn1_w_gate=m_ffn1_w_gate, ffn1_w_up=m_ffn1_w_up, ffn1_w_down=m_ffn1_w_down, mix_norm=m_mix_norm, w_in=m_w_in, ssm_lambda_re=m_ssm_lambda_re, ssm_lambda_im=m_ssm_lambda_im, ssm_log_dt=m_ssm_log_dt, ssm_b_re=m_ssm_b_re, ssm_b_im=m_ssm_b_im, ssm_c_re=m_ssm_c_re, ssm_c_im=m_ssm_c_im, ssm_d=m_ssm_d, ssm_w_glu=m_ssm_w_glu, ssm_b_glu=m_ssm_b_glu, ssm_w_out=m_ssm_w_out, conv_w=m_conv_w, conv_b=m_conv_b, conv_w_out=m_conv_w_out, w_o=m_w_o, ffn2_norm=m_ffn2_norm, ffn2_w_gate=m_ffn2_w_gate, ffn2_w_up=m_ffn2_w_up, ffn2_w_down=m_ffn2_w_down, final_norm=m_final_norm)
    V = dict(ffn1_norm=v_ffn1_norm, ffn1_w_gate=v_ffn1_w_gate, ffn1_w_up=v_ffn1_w_up, ffn1_w_down=v_ffn1_w_down, mix_norm=v_mix_norm, w_in=v_w_in, ssm_lambda_re=v_ssm_lambda_re, ssm_lambda_im=v_ssm_lambda_im, ssm_log_dt=v_ssm_log_dt, ssm_b_re=v_ssm_b_re, ssm_b_im=v_ssm_b_im, ssm_c_re=v_ssm_c_re, ssm_c_im=v_ssm_c_im, ssm_d=v_ssm_d, ssm_w_glu=v_ssm_w_glu, ssm_b_glu=v_ssm_b_glu, ssm_w_out=v_ssm_w_out, conv_w=v_conv_w, conv_b=v_conv_b, conv_w_out=v_conv_w_out, w_o=v_w_o, ffn2_norm=v_ffn2_norm, ffn2_w_gate=v_ffn2_w_gate, ffn2_w_up=v_ffn2_w_up, ffn2_w_down=v_ffn2_w_down, final_norm=v_final_norm)
    names = list(P)
    sharded = ["ffn1_w_gate", "ffn1_w_up", "ffn1_w_down", "w_in", "ssm_w_glu", "ssm_w_out", "conv_w_out", "w_o",
               "ffn2_w_gate", "ffn2_w_up", "ffn2_w_down"]
    replicated = [n for n in names if n not in sharded and n != "conv_w"]

    S, D = x.shape[1], x.shape[2]
    W = ssm_d.shape[0]
    Dc = D // NDEV
    G, N = ssm_lambda_re.shape
    GN = G * N
    rows = GN // LANE
    xh = x.reshape(S, D)
    target = loss_target.reshape(S, D)
    xi, yi, ci = lax.axis_index("x"), lax.axis_index("y"), lax.axis_index("c")
    c_arr = jnp.reshape(ci, (1,)).astype(I32)
    q_arr = jnp.reshape(2 * xi + yi, (1,)).astype(I32)
    row = lambda v: v.reshape(1, -1)

    transposed = ("ffn1_w_gate", "ffn1_w_up", "ffn2_w_gate", "ffn2_w_up")
    local = lambda table, n: table[n].T if n in transposed else table[n]

    groups = [sharded[0:2], sharded[2:3], sharded[3:8] + ["conv_w"], sharded[8:11]]
    started = []
    for gi, group in enumerate(groups):
        srcs = [conv_w if n == "conv_w" else local(P, n).astype(BF) for n in group]
        lands = [lax.empty((NDEV,) + s.shape, s.dtype) for s in srcs]
        started.append(_split_start("gather_start_%d" % gi, srcs, lands, _gather_copies, 4))

    def gathered(gi, after):
        srcs, lands = _split_wait("gather_wait_%d" % gi, started[gi], _gather_copies, after)
        return _gather_forward("gather_forward_%d" % gi, srcs, lands)

    tm = _pick(S, 512, SUBLANE)
    tk = _pick(D, 1024, LANE)
    ts = _pick(S, 512, SUBLANE)
    tn = _pick(D, 1024, LANE)

    u1, r1 = _rms_fwd("rms1", xh, row(ffn1_norm), deps=[st[4] for st in started])
    wg1, wu1 = gathered(0, None)
    h1, ffn1_saved, wd1 = _ffn_fwd("ffn1", u1, xh, wg1, wu1, lambda after: gathered(1, after)[0])
    u2, r2 = _rms_fwd("rms2", h1, row(mix_norm))
    w_in_f, w_glu_f, w_so, w_co, w_o_f, cw_f = gathered(2, u2)
    w_glu_f = w_glu_f.reshape(W, W)
    w_o_f = w_o_f.reshape(D, D)
    cw = jnp.transpose(cw_f, (1, 0, 2)).reshape(3, W)
    (proj,) = _mm(
        "in_proj", (NDEV, S // tm, 1, D // tk),
        [(u2, (tm, tk), lambda b, i, j, k: (i, k)), (w_in_f, (None, tk, W), lambda b, i, j, k: (b, k, 0))],
        [(0, 1, NN, 0)], [], [((NDEV, S, W), F32, (None, tm, W), lambda b, i, j, k: (b, i, 0))], _plain, [(tm, W)])

    ssm_in = (ssm_lambda_re, ssm_lambda_im, ssm_log_dt, ssm_b_re, ssm_b_im, ssm_c_re, ssm_c_im)
    (a_re, a_im, bd_re, bd_im, cd_re, cd_imn), ssm_vjp = jax.vjp(_ssm_params, *ssm_in)
    tiles, tch, tst = bd_re.shape
    bd_re_b, bd_im_b, cd_re_b, cd_imn_b = (t.astype(BF) for t in (bd_re, bd_im, cd_re, cd_imn))
    v_bf = proj[0].astype(BF)
    st_blk = lambda arr: (arr, (tm, tst), lambda b, i, j, k: (i, b))
    ch_blk = lambda arr: (arr, (tm, tch), lambda b, i, j, k: (i, b))
    bd_blk = lambda arr: (arr, (None, tch, tst), lambda b, i, j, k: (b, 0, 0))
    cd_blk = lambda arr: (arr, (None, tst, tch), lambda b, i, j, k: (b, 0, 0))
    st_out = lambda dt: ((S, GN), dt, (tm, tst), lambda b, i, j, k: (i, b))
    ch_out = lambda dt: ((S, W), dt, (tm, tch), lambda b, i, j, k: (i, b))
    d_blk = (row(ssm_d), (1, tch), lambda b, i, j, k: (0, b))
    v_blk = (proj, (None, tm, tch), lambda b, i, j, k: (0, i, b))

    bu_re, bu_im = _mm("ssm_bu", (tiles, S // tm, 1, 1), [ch_blk(v_bf), bd_blk(bd_re_b), bd_blk(bd_im_b)],
                       [(0, 1, NN, 0), (0, 2, NN, 1)], [], [st_out(F32), st_out(F32)], _plain, [(tm, tst)] * 2)
    s_re3, s_im3 = _scan_fwd(bu_re.reshape(S, rows, LANE), bu_im.reshape(S, rows, LANE), a_re, a_im)
    s_re_b = s_re3.reshape(S, GN).astype(BF)
    s_im_b = s_im3.reshape(S, GN).astype(BF)

    def y0_epilogue(accs, sides):
        y0 = accs[0] + sides[1] * sides[0]
        return y0, _gelu(y0)

    y0, y1 = _mm("ssm_y0", (tiles, S // tm, 1, 1),
                 [st_blk(s_re_b), st_blk(s_im_b), cd_blk(cd_re_b), cd_blk(cd_imn_b)],
                 [(0, 2, NN, 0), (1, 3, NN, 0)], [v_blk, d_blk], [ch_out(F32), ch_out(BF)], y0_epilogue, [(tm, tch)])

    tw = _pick(W, 512, LANE)

    def glu_epilogue(accs, sides):
        q = accs[0] + sides[1]
        return q, _gelu(sides[0]) * _sigmoid(q)

    q_pre, y2 = _mm("ssm_glu", (1, S // tm, W // tw, 1),
                    [(y1, (tm, W), lambda b, i, j, k: (i, 0)), (w_glu_f, (W, tw), lambda b, i, j, k: (0, j))],
                    [(0, 1, NN, 0)],
                    [(y0, (tm, tw), lambda b, i, j, k: (i, j)), (row(ssm_b_glu), (1, tw), lambda b, i, j, k: (0, j))],
                    [((S, W), F32, (tm, tw), lambda b, i, j, k: (i, j)), ((S, W), BF, (tm, tw), lambda b, i, j, k: (i, j))],
                    glu_epilogue, [(tm, tw)])

    yb, conv = _conv_fwd(proj, cw, row(conv_b))

    per = W // Dc
    ga_blk = (proj, (None, tm, Dc), lambda b, i, j, k: (4 + b // per, i, b % per))
    gb_blk = (proj, (None, tm, Dc), lambda b, i, j, k: (6 + b // per, i, b % per))
    dc_out = ((S, D), BF, (tm, Dc), lambda b, i, j, k: (i, b))

    def merge_epilogue(accs, sides):
        za, zb = accs
        return _sigmoid(sides[0]) * za + _sigmoid(sides[1]) * zb, za, zb

    merged, z_a, z_b = _mm(
        "mix_merge", (NDEV, S // tm, 1, 1),
        [(y2, (tm, W), lambda b, i, j, k: (i, 0)), (yb, (tm, W), lambda b, i, j, k: (i, 0)),
         (w_so, (None, W, Dc), lambda b, i, j, k: (b, 0, 0)), (w_co, (None, W, Dc), lambda b, i, j, k: (b, 0, 0))],
        [(0, 2, NN, 0), (1, 3, NN, 1)], [ga_blk, gb_blk], [dc_out, dc_out, dc_out], merge_epilogue, [(tm, Dc)] * 2)

    (h2,) = _mm("mix_out", (1, S // tm, D // tn, D // tk),
                [(merged, (tm, tk), lambda b, i, j, k: (i, k)), (w_o_f, (tk, tn), lambda b, i, j, k: (k, j))],
                [(0, 1, NN, 0)], [(h1, (tm, tn), lambda b, i, j, k: (i, j))],
                [((S, D), F32, (tm, tn), lambda b, i, j, k: (i, j))],
                lambda accs, sides: [sides[0] + accs[0]], [(tm, tn)])

    u3, r3 = _rms_fwd("rms3", h2, row(ffn2_norm))
    wg2, wu2, wd2 = gathered(3, u3)
    h3, ffn2_saved, _ = _ffn_fwd("ffn2", u3, h2, wg2, wu2, wd2)
    loss_vec, dh3, dh3_half, d_final_norm = _loss_head("loss_head", h3, row(final_norm), target)
    loss = lax.psum(loss_vec[0, 0], ("x", "y", "c"))

    grads, deltas, new_m, new_v = {}, {}, {}, {}

    def rs_begin(tag, parts):
        lands = _exchange_sibling("rs_sibling_" + tag, parts)
        sums = [_sum_sibling("rs_sum_%s_%d" % (tag, a), p, land, c_arr) for a, (p, land) in enumerate(zip(parts, lands))]
        lands2 = [lax.empty((3,) + sm.shape[1:], sm.dtype) for sm in sums]
        return _split_start("rs_chips_start_" + tag, sums, lands2, _chips_copies, 3)

    def rs_end(tag, group, begun, after):
        sums, lands2 = _split_wait("rs_chips_wait_" + tag, begun, _chips_copies, after)
        for n, sm, land2 in zip(group, sums, lands2):
            res = _finish_sharded("adamw_" + n, sm, land2, q_arr, local(P, n), local(M, n), local(V, n))
            grads[n], deltas[n], new_m[n], new_v[n] = [t.T if n in transposed else t for t in res]

    du3_fn, dwg2, dwu2, dwd2 = _ffn_bwd("ffn2", dh3_half, u3, ffn2_saved, wg2, wu2, wd2)
    rs_ffn2 = rs_begin("ffn2", [dwg2, dwu2, dwd2])
    du3 = du3_fn(deps=[rs_ffn2[4]])
    dh2, dh2_b, d_ffn2_norm = _rms_bwd("rms3_bwd", du3, h2, r3, row(ffn2_norm), dh3, 1.0)

    dg_out = ((2, S, W), BF, (None, tm, Dc), lambda b, i, j, k: (j // per, i, j % per))
    ga_blk2 = (proj, (None, tm, Dc), lambda b, i, j, k: (4 + j // per, i, j % per))
    gb_blk2 = (proj, (None, tm, Dc), lambda b, i, j, k: (6 + j // per, i, j % per))
    dcj = lambda arr: (arr, (tm, Dc), lambda b, i, j, k: (i, j))
    dcj_out = ((S, D), BF, (tm, Dc), lambda b, i, j, k: (i, j))

    def dmerge_epilogue(accs, sides):
        dm = accs[0]
        sa, sb = _sigmoid(sides[0]), _sigmoid(sides[1])
        za, zb = sides[2].astype(F32), sides[3].astype(F32)
        return dm * sa, dm * sb, dm * za * sa * (1.0 - sa), dm * zb * sb * (1.0 - sb)

    dz_a, dz_b, dga, dgb = _mm(
        "mix_out_dx", (1, S // tm, NDEV, D // tk),
        [(dh2_b, (tm, tk), lambda b, i, j, k: (i, k)), (w_o_f, (Dc, tk), lambda b, i, j, k: (j, k))],
        [(0, 1, NT, 0)], [ga_blk2, gb_blk2, dcj(z_a), dcj(z_b)], [dcj_out, dcj_out, dg_out, dg_out],
        dmerge_epilogue, [(tm, Dc)])

    td = _pick(D, 512, LANE)
    (dw_o,) = _mm("mix_out_dw", (1, D // td, D // tn, S // ts),
                  [(merged, (ts, td), lambda b, i, j, k: (k, i)), (dh2_b, (ts, tn), lambda b, i, j, k: (k, j))],
                  [(0, 1, TN, 0)], [], [((D, D), BF, (td, tn), lambda b, i, j, k: (i, j))], _plain, [(td, tn)])

    wout = ((NDEV, W, Dc), BF, (None, W, Dc), lambda b, i, j, k: (b, 0, 0))
    dw_so, dw_co = _mm(
        "mix_merge_dw", (NDEV, 1, 1, S // ts),
        [(y2, (ts, W), lambda b, i, j, k: (k, 0)), (yb, (ts, W), lambda b, i, j, k: (k, 0)),
         (dz_a, (ts, Dc), lambda b, i, j, k: (k, b)), (dz_b, (ts, Dc), lambda b, i, j, k: (k, b))],
        [(0, 2, TN, 0), (1, 3, TN, 1)], [], [wout, wout], _plain, [(W, Dc)] * 2)

    def dglu_epilogue(accs, sides):
        dy2, dyb = accs
        sq = _sigmoid(sides[1])
        return dy2 * _gelu(sides[0]) * sq * (1.0 - sq), dy2 * sq, dyb

    full_w = lambda arr: (arr, (tm, W), lambda b, i, j, k: (i, 0))
    full_w_out = lambda dt: ((S, W), dt, (tm, W), lambda b, i, j, k: (i, 0))
    dq, dy1p, dyb = _mm(
        "mix_merge_dx", (1, S // tm, 1, NDEV),
        [(dz_a, (tm, Dc), lambda b, i, j, k: (i, k)), (dz_b, (tm, Dc), lambda b, i, j, k: (i, k)),
         (w_so, (None, W, Dc), lambda b, i, j, k: (k, 0, 0)), (w_co, (None, W, Dc), lambda b, i, j, k: (k, 0, 0))],
        [(0, 2, NT, 0), (1, 3, NT, 1)], [full_w(y0), full_w(q_pre)], [full_w_out(BF), full_w_out(F32), full_w_out(F32)],
        dglu_epilogue, [(tm, W)] * 2)

    def dy0_epilogue(accs, sides):
        dy0 = (sides[0] + accs[0]) * _gelu_grad(sides[1])
        return dy0, dy0

    wj = lambda arr: (arr, (tm, tw), lambda b, i, j, k: (i, j))
    dy0, dy0_b = _mm("ssm_glu_dx", (1, S // tm, W // tw, 1),
                     [(dq, (tm, W), lambda b, i, j, k: (i, 0)), (w_glu_f, (tw, W), lambda b, i, j, k: (j, 0))],
                     [(0, 1, NT, 0)], [wj(dy1p), wj(y0)],
                     [((S, W), F32, (tm, tw), lambda b, i, j, k: (i, j)), ((S, W), BF, (tm, tw), lambda b, i, j, k: (i, j))],
                     dy0_epilogue, [(tm, tw)])

    (dw_glu,) = _mm("ssm_glu_dw", (1, W // tw, 1, S // ts),
                    [(y1, (ts, tw), lambda b, i, j, k: (k, i)), (dq, (ts, W), lambda b, i, j, k: (k, 0))],
                    [(0, 1, TN, 0)], [], [((W, W), BF, (tw, W), lambda b, i, j, k: (i, 0))], _plain, [(tw, W)])

    tr = _pick(S, 256, SUBLANE)
    rw = ((tr, W), lambda i: (i, 0))
    vec_w = ((1, W), F32, (1, W), lambda i: (0, 0))
    d_b_glu, d_ssm_d = _ew(
        "ssm_colsums", (S // tr,), [(dq,) + rw, (dy0,) + rw, (proj, (None, tr, W), lambda i: (0, i, 0))],
        [vec_w, vec_w],
        lambda dqv, dyv, vv: (jnp.sum(dqv.astype(F32), axis=0, keepdims=True), jnp.sum(dyv * vv, axis=0, keepdims=True)),
        acc=(0, 1))

    ds_re, ds_im = _mm("ssm_ds", (tiles, S // tm, 1, 1), [ch_blk(dy0_b), cd_blk(cd_re_b), cd_blk(cd_imn_b)],
                       [(0, 1, NT, 0), (0, 2, NT, 1)], [], [st_out(F32), st_out(F32)], _plain, [(tm, tst)] * 2)
    lam_re3, lam_im3, da_re, da_im = _scan_bwd(ds_re.reshape(S, rows, LANE), ds_im.reshape(S, rows, LANE),
                                               s_re3, s_im3, a_re, a_im)
    lam_re_b = lam_re3.reshape(S, GN).astype(BF)
    lam_im_b = lam_im3.reshape(S, GN).astype(BF)

    (dv,) = _mm("ssm_dv", (tiles, S // tm, 1, 1),
                [st_blk(lam_re_b), st_blk(lam_im_b), bd_blk(bd_re_b), bd_blk(bd_im_b)],
                [(0, 2, NT, 0), (1, 3, NT, 0)], [ch_blk(dy0), d_blk], [ch_out(BF)],
                lambda accs, sides: [accs[0] + sides[0] * sides[1]], [(tm, tch)])

    tok_ch = lambda arr: (arr, (ts, tch), lambda b, i, j, k: (k, b))
    tok_st = lambda arr: (arr, (ts, tst), lambda b, i, j, k: (k, b))
    bd_out = ((tiles, tch, tst), F32, (None, tch, tst), lambda b, i, j, k: (b, 0, 0))
    cd_out = ((tiles, tst, tch), F32, (None, tst, tch), lambda b, i, j, k: (b, 0, 0))
    dbd_re, dbd_im = _mm("ssm_dbd", (tiles, 1, 1, S // ts), [tok_ch(v_bf), tok_st(lam_re_b), tok_st(lam_im_b)],
                         [(0, 1, TN, 0), (0, 2, TN, 1)], [], [bd_out, bd_out], _plain, [(tch, tst)] * 2)
    dcd_re, dcd_imn = _mm("ssm_dcd", (tiles, 1, 1, S // ts), [tok_st(s_re_b), tok_st(s_im_b), tok_ch(dy0_b)],
                          [(0, 2, TN, 0), (1, 2, TN, 1)], [], [cd_out, cd_out], _plain, [(tst, tch)] * 2)
    d_ssm = ssm_vjp((da_re, da_im, dbd_re, dbd_im, dcd_re, dcd_imn))

    dbg, dcg, dval, d_conv_w_full, d_conv_b = _conv_bwd(dyb, proj, conv, cw)
    dproj = jnp.concatenate([dv[None], dbg[None], dcg[None], dval[None], dga, dgb], axis=0)

    (dw_in,) = _mm("in_proj_dw", (NDEV, D // td, 1, S // ts),
                   [(u2, (ts, td), lambda b, i, j, k: (k, i)), (dproj, (None, ts, W), lambda b, i, j, k: (b, k, 0))],
                   [(0, 1, TN, 0)], [], [((NDEV, D, W), BF, (None, td, W), lambda b, i, j, k: (b, i, 0))],
                   _plain, [(td, W)])
    rs_mixer = rs_begin("mixer", [dw_in, dw_glu.reshape(NDEV, W // NDEV, W), dw_so, dw_co, dw_o.reshape(NDEV, Dc, D)])
    (du2,) = _mm("in_proj_dx", (1, S // tm, D // tn, NDEV),
                 [(dproj, (None, tm, W), lambda b, i, j, k: (k, i, 0)), (w_in_f, (None, tn, W), lambda b, i, j, k: (k, j, 0))],
                 [(0, 1, NT, 0)], [], [((S, D), F32, (tm, tn), lambda b, i, j, k: (i, j))], _plain, [(tm, tn)],
                 deps=[rs_mixer[4]])
    dh1, dh1_half, d_mix_norm = _rms_bwd("rms2_bwd", du2, h1, r2, row(mix_norm), dh2, 0.5)
    rs_end("ffn2", sharded[8:11], rs_ffn2, dh1)

    du1_fn, dwg1, dwu1, dwd1 = _ffn_bwd("ffn1", dh1_half, u1, ffn1_saved, wg1, wu1, wd1)
    rs_ffn1 = rs_begin("ffn1", [dwg1, dwu1, dwd1])
    du1 = du1_fn(deps=[rs_ffn1[4]])
    dx, _, d_ffn1_norm = _rms_bwd("rms1_bwd", du1, xh, r1, row(ffn1_norm), dh1, 1.0)
    rs_end("mixer", sharded[3:8], rs_mixer, dx)

    small = dict(ffn1_norm=d_ffn1_norm, mix_norm=d_mix_norm, ffn2_norm=d_ffn2_norm, final_norm=d_final_norm,
                 ssm_lambda_re=d_ssm[0], ssm_lambda_im=d_ssm[1], ssm_log_dt=d_ssm[2], ssm_b_re=d_ssm[3],
                 ssm_b_im=d_ssm[4], ssm_c_re=d_ssm[5], ssm_c_im=d_ssm[6], ssm_d=d_ssm_d, ssm_b_glu=d_b_glu,
                 conv_b=d_conv_b)
    sizes = [P[n].size for n in replicated]
    total = sum(sizes) + d_conv_w_full.size
    padded = -(-total // (SUBLANE * LANE)) * (SUBLANE * LANE)

    def pack(parts):
        flat = jnp.concatenate([p.reshape(-1).astype(F32) for p in parts])
        return jnp.pad(flat, (0, padded - flat.shape[0])).reshape(padded // LANE, LANE)

    cw_zero = jnp.zeros_like(d_conv_w_full)
    (small_all,) = _all_gather("gather_small_grads", [pack([small[n] for n in replicated] + [d_conv_w_full])])
    g_pk, d_pk, m_pk, v_pk = _finish_replicated(
        "adamw_replicated", small_all, pack([P[n] for n in replicated] + [cw_zero]),
        pack([M[n] for n in replicated] + [cw_zero]), pack([V[n] for n in replicated] + [cw_zero + 1.0]))
    off = 0
    for n, sz in zip(replicated, sizes):
        for store, pk in ((grads, g_pk), (deltas, d_pk), (new_m, m_pk), (new_v, v_pk)):
            store[n] = pk.reshape(-1)[off:off + sz].reshape(P[n].shape)
        off += sz
    g_cw_full = g_pk.reshape(-1)[off:off + d_conv_w_full.size].reshape(d_conv_w_full.shape)
    cwl = conv_w.shape[1]
    g_cw = lax.dynamic_slice_in_dim(g_cw_full, (4 * xi + 2 * yi + ci) * cwl, cwl, axis=1)
    full3 = ((3, cwl), lambda i: (0, 0))
    grads["conv_w"], deltas["conv_w"], new_m["conv_w"], new_v["conv_w"] = _ew(
        "adamw_conv_w", (1,), [(g_cw,) + full3, (conv_w,) + full3, (m_conv_w,) + full3, (v_conv_w,) + full3],
        [((3, cwl), F32) + full3] * 4, lambda g, w, m, v: (g,) + _adamw(w, g, m, v))
    rs_end("ffn1", sharded[0:3], rs_ffn1, g_pk)

    return (loss, dx.reshape(x.shape), *[grads[n] for n in names], *[deltas[n] for n in names],
            *[new_m[n] for n in names], *[new_v[n] for n in names])
```

```python
import math

import jax
import jax.numpy as jnp
from jax import lax
from jax.experimental import pallas as pl
from jax.experimental.pallas import tpu as pltpu

F32 = jnp.float32
BF = jnp.bfloat16
I32 = jnp.int32
MESH = pl.DeviceIdType.MESH
LANE = 128
SUBLANE = 8
NDEV = 8
K_TILE = 2048
EPS = 1e-6
ADAM_LR, ADAM_B1, ADAM_B2, ADAM_EPS, ADAM_WD, ADAM_STEP = 0.001, 0.9, 0.999, 1e-08, 0.01, 10
NN = ((1,), (0,))
NT = ((1,), (1,))
TN = ((0,), (0,))
HBM = pl.BlockSpec(memory_space=pltpu.HBM)


def _pick(n, pref, mult):
    t = min(pref, n)
    t -= t % mult
    while t >= mult:
        if n % t == 0:
            return t
        t -= mult
    return n


def _sigmoid(x):
    return 1.0 / (1.0 + jnp.exp(-x))


_GELU_C = math.sqrt(2.0 / math.pi)


def _gelu(x):
    return 0.5 * x * (1.0 + jnp.tanh(_GELU_C * (x + 0.044715 * x * x * x)))


def _gelu_grad(x):
    t = jnp.tanh(_GELU_C * (x + 0.044715 * x * x * x))
    return 0.5 * (1.0 + t) + 0.5 * x * (1.0 - t * t) * _GELU_C * (1.0 + 3.0 * 0.044715 * x * x)


def _dep_specs(deps, rank):
    return [(d, d.shape, lambda *_, nd=d.ndim: (0,) * nd) for d in deps]


def _mm(name, grid, ops, pairs, sides, outs, epilogue, acc_shapes, deps=()):
    nk = grid[-1]
    n_ops, n_sides, n_outs = len(ops), len(sides), len(outs)
    dep_specs = _dep_specs(deps, len(grid))
    n_deps = len(dep_specs)

    def body(*refs):
        op_refs = refs[:n_ops]
        side_refs = refs[n_ops:n_ops + n_sides]
        out_refs = refs[n_ops + n_sides + n_deps:n_ops + n_sides + n_deps + n_outs]
        acc_refs = refs[n_ops + n_sides + n_deps + n_outs:]

        def partials():
            res = [None] * len(acc_shapes)
            for ia, ib, dims, ai in pairs:
                p = lax.dot_general(op_refs[ia][...], op_refs[ib][...], (dims, ((), ())),
                                    preferred_element_type=F32)
                res[ai] = p if res[ai] is None else res[ai] + p
            return res

        def finish(accs):
            vals = epilogue(accs, [s[...] for s in side_refs])
            for o, v in zip(out_refs, vals):
                o[...] = v.astype(o.dtype)

        if nk == 1:
            finish(partials())
        else:
            k = pl.program_id(len(grid) - 1)

            @pl.when(k == 0)
            def _():
                for a, p in zip(acc_refs, partials()):
                    a[...] = p

            @pl.when(k > 0)
            def _():
                for a, p in zip(acc_refs, partials()):
                    a[...] += p

            @pl.when(k == nk - 1)
            def _():
                finish([a[...] for a in acc_refs])

    return pl.pallas_call(
        body, name=name, grid=grid,
        in_specs=[pl.BlockSpec(b, m) for (_, b, m) in list(ops) + list(sides) + dep_specs],
        out_specs=[pl.BlockSpec(b, m) for (_, _, b, m) in outs],
        out_shape=[jax.ShapeDtypeStruct(s, d) for (s, d, _, _) in outs],
        scratch_shapes=[pltpu.VMEM(s, F32) for s in acc_shapes] if nk > 1 else [],
        compiler_params=pltpu.CompilerParams(
            dimension_semantics=("parallel",) * (len(grid) - 1) + ("arbitrary",)),
    )(*[a for (a, _, _) in list(ops) + list(sides) + dep_specs])


def _ew(name, grid, ins, outs, fn, acc=(), deps=()):
    n_in = len(ins)
    dep_specs = _dep_specs(deps, len(grid))

    def body(*refs):
        vals = fn(*[r[...] for r in refs[:n_in]])
        first = pl.program_id(0) == 0
        for idx, (o, v) in enumerate(zip(refs[n_in + len(dep_specs):], vals)):
            if idx in acc:
                @pl.when(first)
                def _(o=o, v=v):
                    o[...] = v.astype(o.dtype)

                @pl.when(jnp.logical_not(first))
                def _(o=o, v=v):
                    o[...] += v.astype(o.dtype)
            else:
                o[...] = v.astype(o.dtype)

    return pl.pallas_call(
        body, name=name, grid=grid,
        in_specs=[pl.BlockSpec(b, m) for (_, b, m) in list(ins) + dep_specs],
        out_specs=[pl.BlockSpec(b, m) for (_, _, b, m) in outs],
        out_shape=[jax.ShapeDtypeStruct(s, d) for (s, d, _, _) in outs],
        compiler_params=pltpu.CompilerParams(
            dimension_semantics=(("arbitrary",) if acc else ("parallel",)) * len(grid)),
    )(*[a for (a, _, _) in list(ins) + dep_specs])


def _position():
    x, y, c = lax.axis_index("x"), lax.axis_index("y"), lax.axis_index("c")
    chips = [(1 - x, y), (x, 1 - y), (1 - x, 1 - y)]
    return x, y, c, chips


def _all_gather(name, shards):
    n = len(shards)

    def body(*refs):
        xs, outs = refs[:n], refs[n:2 * n]
        send_sems, recv_sems, local_sems = refs[2 * n:]
        x, y, c, chips = _position()
        me, sibling = (x, y, c), (x, y, 1 - c)

        def copy(a, k, block, to, src=None):
            dst = outs[a].at[4 * block[0] + 2 * block[1] + block[2]]
            return pltpu.make_async_remote_copy(
                src_ref=dst if src is None else src, dst_ref=dst,
                send_sem=send_sems.at[a, k], recv_sem=recv_sems.at[a, k],
                device_id=to, device_id_type=MESH)

        mine = [pltpu.make_async_copy(xs[a], outs[a].at[4 * x + 2 * y + c], local_sems.at[a]) for a in range(n)]
        for cp in mine:
            cp.start()
        first = []
        for a in range(n):
            first.append(copy(a, 0, me, sibling, src=xs[a]))
            first += [copy(a, 1 + j, me, (*chip, c), src=xs[a]) for j, chip in enumerate(chips)]
        for cp in first:
            cp.start()
        passed = []
        for a in range(n):
            for j, chip in enumerate(chips):
                copy(a, 1 + j, (*chip, c), me).wait_recv()
                cp = copy(a, 4 + j, (*chip, c), sibling)
                cp.start()
                passed.append(cp)
        for a in range(n):
            copy(a, 0, sibling, me).wait_recv()
            for j, chip in enumerate(chips):
                copy(a, 4 + j, (*chip, 1 - c), me).wait_recv()
        for cp in first + passed:
            cp.wait_send()
        for cp in mine:
            cp.wait()

    return pl.pallas_call(
        body, name=name,
        out_shape=[jax.ShapeDtypeStruct((NDEV,) + s.shape, s.dtype) for s in shards],
        in_specs=[HBM] * n, out_specs=[HBM] * n,
        scratch_shapes=[pltpu.SemaphoreType.DMA((n, 7)), pltpu.SemaphoreType.DMA((n, 7)),
                        pltpu.SemaphoreType.DMA((n,))],
    )(*shards)


SEM = pl.BlockSpec(memory_space=pltpu.SEMAPHORE)
EFFECT = pltpu.SideEffectType.DATAFLOW_SIDE_EFFECTING


def _in_hbm(v):
    return pltpu.with_memory_space_constraint(v, pltpu.HBM)


def _split_start(name, srcs, lands, make_copies, n_per, after=()):
    n = len(srcs)
    after = list(after)

    def body(*refs):
        send_sems, recv_sems = refs[2 * n + len(after)], refs[2 * n + len(after) + 1]
        for cp in make_copies(refs[:n], refs[n:2 * n], send_sems, recv_sems):
            cp.start()
        refs[-1][...] = jnp.zeros_like(refs[-1])

    outs = pl.pallas_call(
        body, name=name,
        out_shape=(pltpu.SemaphoreType.DMA((n * n_per,)), pltpu.SemaphoreType.DMA((n * n_per,)),
                   *[pltpu.HBM(v.shape, v.dtype) for v in list(srcs) + list(lands)],
                   jax.ShapeDtypeStruct((SUBLANE, LANE), F32)),
        in_specs=[HBM] * (2 * n) + [pl.BlockSpec(memory_space=pl.ANY)] * len(after),
        out_specs=(SEM, SEM, *[HBM] * (2 * n), pl.BlockSpec(memory_space=pltpu.VMEM)),
        input_output_aliases={i: 2 + i for i in range(2 * n)},
        compiler_params=pltpu.CompilerParams(has_side_effects=EFFECT),
    )(*[_in_hbm(v) for v in list(srcs) + list(lands)], *after)
    return outs[0], outs[1], list(outs[2:2 + n]), list(outs[2 + n:2 + 2 * n]), outs[-1]


def _split_wait(name, started, make_copies, after):
    send_sems, recv_sems, srcs, lands, _ = started
    n = len(srcs)

    def body(*refs):
        for cp in make_copies(refs[:n], refs[n:2 * n], refs[2 * n], refs[2 * n + 1]):
            cp.wait_send()
            cp.wait_recv()

    order = [] if after is None else [after]
    outs = pl.pallas_call(
        body, name=name,
        out_shape=tuple(pltpu.HBM(v.shape, v.dtype) for v in srcs + lands),
        in_specs=[HBM] * (2 * n) + [SEM, SEM] + [pl.BlockSpec(memory_space=pl.ANY)] * len(order),
        out_specs=tuple([HBM] * (2 * n)),
        input_output_aliases={i: i for i in range(2 * n)},
        compiler_params=pltpu.CompilerParams(has_side_effects=EFFECT),
    )(*srcs, *lands, send_sems, recv_sems, *order)
    return list(outs[:n]), list(outs[n:])


def _gather_copies(xs, lands, send_sems, recv_sems):
    x, y, c, chips = _position()
    copies = []
    for a in range(len(xs)):
        for k, peer in enumerate([(x, y, 1 - c)] + [(*chip, c) for chip in chips]):
            copies.append(pltpu.make_async_remote_copy(
                src_ref=xs[a], dst_ref=lands[a].at[4 * x + 2 * y + c],
                send_sem=send_sems.at[4 * a + k], recv_sem=recv_sems.at[4 * a + k], device_id=peer, device_id_type=MESH))
    return copies


def _chips_copies(ps, lands, send_sems, recv_sems):
    x, y, c, chips = _position()
    copies = []
    for a in range(len(ps)):
        for j, chip in enumerate(chips):
            copies.append(pltpu.make_async_remote_copy(
                src_ref=ps[a].at[2 * chip[0] + chip[1]], dst_ref=lands[a].at[j],
                send_sem=send_sems.at[3 * a + j], recv_sem=recv_sems.at[3 * a + j], device_id=(*chip, c),
                device_id_type=MESH))
    return copies


def _gather_forward(name, shards, lands):
    n = len(shards)

    def body(*refs):
        xs, ins, outs = refs[:n], refs[n:2 * n], refs[2 * n:3 * n]
        send_sems, recv_sems, local_sems = refs[3 * n:]
        x, y, c, chips = _position()
        mine = [pltpu.make_async_copy(xs[a], outs[a].at[4 * x + 2 * y + c], local_sems.at[a]) for a in range(n)]
        passed, arrivals = [], []
        for a in range(n):
            for j, chip in enumerate(chips):
                slot = 4 * chip[0] + 2 * chip[1]
                passed.append(pltpu.make_async_remote_copy(
                    src_ref=ins[a].at[slot + c], dst_ref=outs[a].at[slot + c],
                    send_sem=send_sems.at[a, j], recv_sem=recv_sems.at[a, j],
                    device_id=(x, y, 1 - c), device_id_type=MESH))
                arrivals.append(pltpu.make_async_remote_copy(
                    src_ref=ins[a].at[slot + c], dst_ref=outs[a].at[slot + 1 - c],
                    send_sem=send_sems.at[a, j], recv_sem=recv_sems.at[a, j],
                    device_id=(x, y, 1 - c), device_id_type=MESH))
        for cp in mine + passed:
            cp.start()
        for cp in arrivals:
            cp.wait_recv()
        for cp in passed:
            cp.wait_send()
        for cp in mine:
            cp.wait()

    return pl.pallas_call(
        body, name=name,
        out_shape=[jax.ShapeDtypeStruct(l.shape, l.dtype) for l in lands],
        in_specs=[HBM] * (2 * n), out_specs=[HBM] * n,
        input_output_aliases={n + a: a for a in range(n)},
        scratch_shapes=[pltpu.SemaphoreType.DMA((n, 3)), pltpu.SemaphoreType.DMA((n, 3)),
                        pltpu.SemaphoreType.DMA((n,))],
    )(*shards, *lands)


def _exchange_sibling(name, grads):
    n = len(grads)

    def body(*refs):
        gs, lands = refs[:n], refs[n:2 * n]
        send_sems, recv_sems = refs[2 * n:]
        x, y, c, _ = _position()
        copies = []
        for a in range(n):
            for q in range(4):
                copies.append(pltpu.make_async_remote_copy(
                    src_ref=gs[a].at[2 * q + 1 - c], dst_ref=lands[a].at[q],
                    send_sem=send_sems.at[a, q], recv_sem=recv_sems.at[a, q],
                    device_id=(x, y, 1 - c), device_id_type=MESH))
        for cp in copies:
            cp.start()
        for cp in copies:
            cp.wait()

    return pl.pallas_call(
        body, name=name,
        out_shape=[jax.ShapeDtypeStruct((4,) + g.shape[1:], g.dtype) for g in grads],
        in_specs=[HBM] * n, out_specs=[HBM] * n,
        scratch_shapes=[pltpu.SemaphoreType.DMA((n, 4)), pltpu.SemaphoreType.DMA((n, 4))],
    )(*grads)


def _sum_sibling(name, g, land, c_arr):
    _, R, C = g.shape
    tr = _pick(R, 512, SUBLANE)

    def body(c_ref, g_ref, l_ref, o_ref):
        o_ref[...] = (g_ref[...].astype(F32) + l_ref[...].astype(F32)).astype(o_ref.dtype)

    return pl.pallas_call(
        body, name=name,
        grid_spec=pltpu.PrefetchScalarGridSpec(
            num_scalar_prefetch=1, grid=(4, R // tr),
            in_specs=[pl.BlockSpec((None, tr, C), lambda q, i, cr: (2 * q + cr[0], i, 0)),
                      pl.BlockSpec((None, tr, C), lambda q, i, cr: (q, i, 0))],
            out_specs=pl.BlockSpec((None, tr, C), lambda q, i, cr: (q, i, 0))),
        out_shape=jax.ShapeDtypeStruct((4, R, C), g.dtype),
        compiler_params=pltpu.CompilerParams(dimension_semantics=("parallel", "parallel")),
    )(c_arr, g, land)


def _adamw(w, g, m, v):
    m = ADAM_B1 * m + (1.0 - ADAM_B1) * g
    v = ADAM_B2 * v + (1.0 - ADAM_B2) * (g * g)
    m_hat = m / (1.0 - ADAM_B1 ** ADAM_STEP)
    v_hat = v / (1.0 - ADAM_B2 ** ADAM_STEP)
    delta = -ADAM_LR * (m_hat / (jnp.sqrt(v_hat) + ADAM_EPS) + ADAM_WD * w)
    return delta, m, v


def _finish_sharded(name, sums, land, q_arr, w, m, v):
    R, C = w.shape
    tr = _pick(R, 256, SUBLANE)

    def body(q_ref, p_ref, l_ref, w_ref, m_ref, v_ref, g_out, d_out, m_out, v_out):
        g = p_ref[...].astype(F32)
        for j in range(3):
            g = g + l_ref[j].astype(F32)
        d, mn, vn = _adamw(w_ref[...], g, m_ref[...], v_ref[...])
        g_out[...] = g
        d_out[...] = d
        m_out[...] = mn
        v_out[...] = vn

    row = pl.BlockSpec((tr, C), lambda i, qr: (i, 0))
    return pl.pallas_call(
        body, name=name,
        grid_spec=pltpu.PrefetchScalarGridSpec(
            num_scalar_prefetch=1, grid=(R // tr,),
            in_specs=[pl.BlockSpec((None, tr, C), lambda i, qr: (qr[0], i, 0)),
                      pl.BlockSpec((3, tr, C), lambda i, qr: (0, i, 0)), row, row, row],
            out_specs=[row] * 4),
        out_shape=[jax.ShapeDtypeStruct((R, C), F32)] * 4,
        compiler_params=pltpu.CompilerParams(dimension_semantics=("parallel",)),
    )(q_arr, sums, land, w, m, v)


def _finish_replicated(name, gathered, w, m, v):
    _, R, C = gathered.shape
    tr = _pick(R, 256, SUBLANE)

    def fn(gv, wv, mv, vv):
        g = gv[0]
        for d in range(1, NDEV):
            g = g + gv[d]
        dl, mn, vn = _adamw(wv, g, mv, vv)
        return g, dl, mn, vn

    row = ((tr, C), lambda i: (i, 0))
    return _ew(name, (R // tr,),
               [(gathered, (NDEV, tr, C), lambda i: (0, i, 0)), (w,) + row, (m,) + row, (v,) + row],
               [((R, C), F32) + row] * 4, fn)


def _rms_fwd(name, h, g, deps=()):
    S, D = h.shape
    tr = _pick(S, 256, SUBLANE)

    def fn(hv, gv):
        r = lax.rsqrt(jnp.mean(hv * hv, axis=-1, keepdims=True) + EPS)
        return hv * r * gv, r

    return _ew(name, (S // tr,),
               [(h, (tr, D), lambda i: (i, 0)), (g, (1, D), lambda i: (0, 0))],
               [((S, D), BF, (tr, D), lambda i: (i, 0)), ((S, 1), F32, (tr, 1), lambda i: (i, 0))], fn, deps=deps)


def _rms_bwd(name, du, h, r, g, dres, scale):
    S, D = h.shape
    tr = _pick(S, 256, SUBLANE)

    def fn(duv, hv, rv, gv, drv):
        xn = hv * rv
        dxn = duv * gv
        dh = drv + rv * (dxn - xn * jnp.mean(dxn * xn, axis=-1, keepdims=True))
        return dh, scale * dh, jnp.sum(duv * xn, axis=0, keepdims=True)

    row = ((tr, D), lambda i: (i, 0))
    return _ew(name, (S // tr,),
               [(du,) + row, (h,) + row, (r, (tr, 1), lambda i: (i, 0)), (g, (1, D), lambda i: (0, 0)), (dres,) + row],
               [((S, D), F32) + row, ((S, D), BF) + row, ((1, D), F32, (1, D), lambda i: (0, 0))], fn, acc=(2,))


def _loss_head(name, h, g, target):
    S, D = h.shape
    tr = _pick(S, 256, SUBLANE)

    def fn(hv, gv, tv):
        r = lax.rsqrt(jnp.mean(hv * hv, axis=-1, keepdims=True) + EPS)
        xn = hv * r
        diff = xn * gv - tv
        loss = 0.5 * jnp.sum(jnp.mean(diff * diff, axis=-1, keepdims=True))
        dout = diff / D
        dxn = dout * gv
        dh = r * (dxn - xn * jnp.mean(dxn * xn, axis=-1, keepdims=True))
        return (jnp.zeros((1, LANE), F32) + loss, dh, 0.5 * dh, jnp.sum(dout * xn, axis=0, keepdims=True))

    row = ((tr, D), lambda i: (i, 0))
    return _ew(name, (S // tr,),
               [(h,) + row, (g, (1, D), lambda i: (0, 0)), (target,) + row],
               [((1, LANE), F32, (1, LANE), lambda i: (0, 0)), ((S, D), F32) + row, ((S, D), BF) + row,
                ((1, D), F32, (1, D), lambda i: (0, 0))], fn, acc=(0, 3))


def _ffn_fwd(tag, u, h, wg, wu, wd, deps=()):
    S, D = u.shape
    Fs = wg.shape[1]
    tm, tk = _pick(S, 512, SUBLANE), _pick(D, K_TILE, LANE)

    def up_epilogue(accs, sides):
        gt, up = accs
        return gt, up, gt * _sigmoid(gt) * up

    act = ((NDEV, S, Fs), BF, (None, tm, Fs), lambda b, i, j, k: (b, i, 0))
    gt, up, a = _mm(
        tag + "_up", (NDEV, S // tm, 1, D // tk),
        [(u, (tm, tk), lambda b, i, j, k: (i, k)),
         (wg, (None, Fs, tk), lambda b, i, j, k: (b, 0, k)), (wu, (None, Fs, tk), lambda b, i, j, k: (b, 0, k))],
        [(0, 1, NT, 0), (0, 2, NT, 1)], [], [act, act, act], up_epilogue, [(tm, Fs), (tm, Fs)], deps=deps)
    if callable(wd):
        wd = wd(a)
    tn = _pick(D, 1024, LANE)
    (hn,) = _mm(
        tag + "_down", (1, S // tm, D // tn, NDEV),
        [(a, (None, tm, Fs), lambda b, i, j, k: (k, i, 0)), (wd, (None, Fs, tn), lambda b, i, j, k: (k, 0, j))],
        [(0, 1, NN, 0)], [(h, (tm, tn), lambda b, i, j, k: (i, j))],
        [((S, D), F32, (tm, tn), lambda b, i, j, k: (i, j))],
        lambda accs, sides: [sides[0] + 0.5 * accs[0]], [(tm, tn)])
    return hn, (gt, up, a), wd


def _ffn_bwd(tag, dhs, u, saved, wg, wu, wd):
    gt, up, a = saved
    S, D = u.shape
    Fs = wg.shape[1]
    tm, tk = _pick(S, 512, SUBLANE), _pick(D, K_TILE, LANE)
    act_in = lambda arr: (arr, (None, tm, Fs), lambda b, i, j, k: (b, i, 0))
    act_out = ((NDEV, S, Fs), BF, (None, tm, Fs), lambda b, i, j, k: (b, i, 0))

    def act_epilogue(accs, sides):
        da = accs[0]
        gtv, upv = sides[0].astype(F32), sides[1].astype(F32)
        sg = _sigmoid(gtv)
        return da * upv * sg * (1.0 + gtv * (1.0 - sg)), da * gtv * sg

    dgt, dup = _mm(
        tag + "_dact", (NDEV, S // tm, 1, D // tk),
        [(dhs, (tm, tk), lambda b, i, j, k: (i, k)), (wd, (None, Fs, tk), lambda b, i, j, k: (b, 0, k))],
        [(0, 1, NT, 0)], [act_in(gt), act_in(up)], [act_out, act_out], act_epilogue, [(tm, Fs)])

    ts = _pick(S, K_TILE, SUBLANE)
    tn = _pick(D, 1024, LANE)
    wgrad = ((NDEV, Fs, D), BF, (None, Fs, tn), lambda b, i, j, k: (b, 0, j))
    tok = lambda arr: (arr, (None, ts, Fs), lambda b, i, j, k: (b, k, 0))
    (dwd,) = _mm(
        tag + "_dwd", (NDEV, 1, D // tn, S // ts),
        [tok(a), (dhs, (ts, tn), lambda b, i, j, k: (k, j))],
        [(0, 1, TN, 0)], [], [wgrad], lambda accs, sides: accs, [(Fs, tn)])
    dwg, dwu = _mm(
        tag + "_dwgu", (NDEV, 1, D // tn, S // ts),
        [tok(dgt), tok(dup), (u, (ts, tn), lambda b, i, j, k: (k, j))],
        [(0, 2, TN, 0), (1, 2, TN, 1)], [], [wgrad, wgrad], lambda accs, sides: accs, [(Fs, tn), (Fs, tn)])

    def du(deps=()):
        return _mm(
            tag + "_du", (1, S // tm, D // tn, NDEV),
            [(dgt, (None, tm, Fs), lambda b, i, j, k: (k, i, 0)), (dup, (None, tm, Fs), lambda b, i, j, k: (k, i, 0)),
             (wg, (None, Fs, tn), lambda b, i, j, k: (k, 0, j)), (wu, (None, Fs, tn), lambda b, i, j, k: (k, 0, j))],
            [(0, 2, NN, 0), (1, 3, NN, 0)], [], [((S, D), F32, (tm, tn), lambda b, i, j, k: (i, j))],
            lambda accs, sides: accs, [(tm, tn)], deps=deps)[0]

    return du, dwg, dwu, dwd


def _ssm_params(lam_re, lam_im, log_dt, b_re, b_im, c_re, c_im):
    G, N = lam_re.shape
    C = b_re.shape[2]
    lam_re = jnp.minimum(lam_re, -1e-4)
    dt = jnp.exp(log_dt)[:, None]
    mag = jnp.exp(lam_re * dt)
    a_re = mag * jnp.cos(lam_im * dt)
    a_im = mag * jnp.sin(lam_im * dt)
    den = lam_re * lam_re + lam_im * lam_im
    p = a_re - 1.0
    f_re = ((p * lam_re + a_im * lam_im) / den)[:, :, None]
    f_im = ((a_im * lam_re - p * lam_im) / den)[:, :, None]
    bb_re = f_re * b_re - f_im * b_im
    bb_im = f_re * b_im + f_im * b_re
    gpt = LANE // C
    tiles = G // gpt
    eye = jnp.eye(gpt, dtype=F32)

    def bd(bb):
        return jnp.einsum("bgnc,gh->bgchn", bb.reshape(tiles, gpt, N, C), eye).reshape(tiles, gpt * C, gpt * N)

    def cd(cc):
        return jnp.einsum("bgcn,gh->bgnhc", cc.reshape(tiles, gpt, C, N), eye).reshape(tiles, gpt * N, gpt * C)

    rows = G * N // LANE
    return (a_re.reshape(rows, LANE), a_im.reshape(rows, LANE), bd(bb_re), bd(bb_im), cd(c_re), cd(-c_im))


def _scan_fwd(bu_re, bu_im, a_re, a_im):
    S, R, _ = bu_re.shape
    tc = _pick(S, 256, SUBLANE)

    def body(bre, bim, are, aim, sre, sim, carry):
        @pl.when(pl.program_id(0) == 0)
        def _():
            carry[...] = jnp.zeros_like(carry)

        ar, ai = are[...], aim[...]

        def step(t, c):
            pr, pi = c
            nr = ar * pr - ai * pi + bre[t]
            ni = ar * pi + ai * pr + bim[t]
            sre[t] = nr
            sim[t] = ni
            return nr, ni

        pr, pi = lax.fori_loop(0, tc, step, (carry[0], carry[1]), unroll=8)
        carry[0] = pr
        carry[1] = pi

    blk = pl.BlockSpec((tc, R, LANE), lambda i: (i, 0, 0))
    par = pl.BlockSpec((R, LANE), lambda i: (0, 0))
    return pl.pallas_call(
        body, name="ssm_scan_fwd", grid=(S // tc,),
        in_specs=[blk, blk, par, par], out_specs=[blk, blk],
        out_shape=[jax.ShapeDtypeStruct((S, R, LANE), F32)] * 2,
        scratch_shapes=[pltpu.VMEM((2, R, LANE), F32)],
        compiler_params=pltpu.CompilerParams(dimension_semantics=("arbitrary",)),
    )(bu_re, bu_im, a_re, a_im)


def _scan_bwd(ds_re, ds_im, s_re, s_im, a_re, a_im):
    S, R, _ = ds_re.shape
    tc = _pick(S, 256, SUBLANE)
    nc = S // tc

    def body(dre, dim_, sre, sim, are, aim, lre, lim, dar, dai, carry):
        @pl.when(pl.program_id(0) == 0)
        def _():
            carry[...] = jnp.zeros_like(carry)
            dar[...] = jnp.zeros_like(dar)
            dai[...] = jnp.zeros_like(dai)

        ar, ai = are[...], aim[...]

        def step(tt, c):
            t = tc - 1 - tt
            lr, li, gr, gi = c
            sr, si = sre[t], sim[t]
            gr = gr + lr * sr + li * si
            gi = gi + li * sr - lr * si
            nlr = dre[t] + ar * lr + ai * li
            nli = dim_[t] + ar * li - ai * lr
            lre[t] = nlr
            lim[t] = nli
            return nlr, nli, gr, gi

        lr, li, gr, gi = lax.fori_loop(0, tc, step, (carry[0], carry[1], dar[...], dai[...]), unroll=8)
        carry[0] = lr
        carry[1] = li
        dar[...] = gr
        dai[...] = gi

    blk = pl.BlockSpec((tc, R, LANE), lambda i: (nc - 1 - i, 0, 0))
    par = pl.BlockSpec((R, LANE), lambda i: (0, 0))
    return pl.pallas_call(
        body, name="ssm_scan_bwd", grid=(nc,),
        in_specs=[blk, blk, blk, blk, par, par], out_specs=[blk, blk, par, par],
        out_shape=[jax.ShapeDtypeStruct((S, R, LANE), F32)] * 2 + [jax.ShapeDtypeStruct((R, LANE), F32)] * 2,
        scratch_shapes=[pltpu.VMEM((2, R, LANE), F32)],
        compiler_params=pltpu.CompilerParams(dimension_semantics=("arbitrary",)),
    )(ds_re, ds_im, s_re, s_im, a_re, a_im)


def _shift_down(z, k):
    t = lax.broadcasted_iota(I32, z.shape, 0)
    return jnp.where(t >= k, pltpu.roll(z, k, 0), 0.0)


def _shift_up(z, k):
    n = z.shape[0]
    t = lax.broadcasted_iota(I32, z.shape, 0)
    return jnp.where(t < n - k, pltpu.roll(z, n - k, 0), 0.0)


def _conv_fwd(proj, cw, cb):
    _, S, W = proj.shape
    ct = _pick(W, 256, LANE)

    def fn(bg, cg, val, w, b):
        z = cg * val
        conv = b + w[0:1] * _shift_down(z, 2) + w[1:2] * _shift_down(z, 1) + w[2:3] * z
        return bg * conv, conv

    sl = lambda s: (proj, (None, S, ct), lambda j, s=s: (s, 0, j))
    col = ((S, ct), lambda j: (0, j))
    return _ew("conv_fwd", (W // ct,),
               [sl(1), sl(2), sl(3), (cw, (3, ct), lambda j: (0, j)), (cb, (1, ct), lambda j: (0, j))],
               [((S, W), BF) + col, ((S, W), F32) + col], fn)


def _conv_bwd(dyb, proj, conv, cw):
    _, S, W = proj.shape
    ct = _pick(W, 256, LANE)

    def fn(dy, bg, cg, val, cv, w):
        z = cg * val
        z1, z2 = _shift_down(z, 1), _shift_down(z, 2)
        dconv = dy * bg
        dz = w[2:3] * dconv + w[1:2] * _shift_up(dconv, 1) + w[0:1] * _shift_up(dconv, 2)
        dw = jnp.concatenate([jnp.sum(dconv * z2, axis=0, keepdims=True), jnp.sum(dconv * z1, axis=0, keepdims=True),
                              jnp.sum(dconv * z, axis=0, keepdims=True)], axis=0)
        return dy * cv, dz * val, dz * cg, dw, jnp.sum(dconv, axis=0, keepdims=True)

    sl = lambda s: (proj, (None, S, ct), lambda j, s=s: (s, 0, j))
    col = ((S, ct), lambda j: (0, j))
    return _ew("conv_bwd", (W // ct,),
               [(dyb,) + col, sl(1), sl(2), sl(3), (conv,) + col, (cw, (3, ct), lambda j: (0, j))],
               [((S, W), BF) + col, ((S, W), BF) + col, ((S, W), BF) + col,
                ((3, W), F32, (3, ct), lambda j: (0, j)), ((1, W), F32, (1, ct), lambda j: (0, j))], fn)


def _plain(accs, sides):
    return accs


def kernel(x, ffn1_norm, ffn1_w_gate, ffn1_w_up, ffn1_w_down, mix_norm, w_in, ssm_lambda_re, ssm_lambda_im, ssm_log_dt, ssm_b_re, ssm_b_im, ssm_c_re, ssm_c_im, ssm_d, ssm_w_glu, ssm_b_glu, ssm_w_out, conv_w, conv_b, conv_w_out, w_o, ffn2_norm, ffn2_w_gate, ffn2_w_up, ffn2_w_down, final_norm, loss_target, m_ffn1_norm, m_ffn1_w_gate, m_ffn1_w_up, m_ffn1_w_down, m_mix_norm, m_w_in, m_ssm_lambda_re, m_ssm_lambda_im, m_ssm_log_dt, m_ssm_b_re, m_ssm_b_im, m_ssm_c_re, m_ssm_c_im, m_ssm_d, m_ssm_w_glu, m_ssm_b_glu, m_ssm_w_out, m_conv_w, m_conv_b, m_conv_w_out, m_w_o, m_ffn2_norm, m_ffn2_w_gate, m_ffn2_w_up, m_ffn2_w_down, m_final_norm, v_ffn1_norm, v_ffn1_w_gate, v_ffn1_w_up, v_ffn1_w_down, v_mix_norm, v_w_in, v_ssm_lambda_re, v_ssm_lambda_im, v_ssm_log_dt, v_ssm_b_re, v_ssm_b_im, v_ssm_c_re, v_ssm_c_im, v_ssm_d, v_ssm_w_glu, v_ssm_b_glu, v_ssm_w_out, v_conv_w, v_conv_b, v_conv_w_out, v_w_o, v_ffn2_norm, v_ffn2_w_gate, v_ffn2_w_up, v_ffn2_w_down, v_final_norm):
    P = dict(ffn1_norm=ffn1_norm, ffn1_w_gate=ffn1_w_gate, ffn1_w_up=ffn1_w_up, ffn1_w_down=ffn1_w_down, mix_norm=mix_norm, w_in=w_in, ssm_lambda_re=ssm_lambda_re, ssm_lambda_im=ssm_lambda_im, ssm_log_dt=ssm_log_dt, ssm_b_re=ssm_b_re, ssm_b_im=ssm_b_im, ssm_c_re=ssm_c_re, ssm_c_im=ssm_c_im, ssm_d=ssm_d, ssm_w_glu=ssm_w_glu, ssm_b_glu=ssm_b_glu, ssm_w_out=ssm_w_out, conv_w=conv_w, conv_b=conv_b, conv_w_out=conv_w_out, w_o=w_o, ffn2_norm=ffn2_norm, ffn2_w_gate=ffn2_w_gate, ffn2_w_up=ffn2_w_up, ffn2_w_down=ffn2_w_down, final_norm=final_norm)
    M = dict(ffn1_norm=m_ffn1_norm, ffn1_w_gate=m_ffn1_w_gate, ffn1_w_up=m_ffn1_w_up, ffn1_w_down=m_ffn1_w_down, mix_norm=m_mix_norm, w_in=m_w_in, ssm_lambda_re=m_ssm_lambda_re, ssm_lambda_im=m_ssm_lambda_im, ssm_log_dt=m_ssm_log_dt, ssm_b_re=m_ssm_b_re, ssm_b_im=m_ssm_b_im, ssm_c_re=m_ssm_c_re, ssm_c_im=m_ssm_c_im, ssm_d=m_ssm_d, ssm_w_glu=m_ssm_w_glu, ssm_b_glu=m_ssm_b_glu, ssm_w_out=m_ssm_w_out, conv_w=m_conv_w, conv_b=m_conv_b, conv_w_out=m_conv_w_out, w_o=m_w_o, ffn2_norm=m_ffn2_norm, ffn2_w_gate=m_ffn2_w_gate, ffn2_w_up=m_ffn2_w_up, ffn2_w_down=m_ffn2_w_down, final_norm=m_final_norm)
    V = dict(ffn1_norm=v_ffn1_norm, ffn1_w_gate=v_ffn1_w_gate, ffn1_w_up=v_ffn1_w_up, ffn1_w_down=v_ffn1_w_down, mix_norm=v_mix_norm, w_in=v_w_in, ssm_lambda_re=v_ssm_lambda_re, ssm_lambda_im=v_ssm_lambda_im, ssm_log_dt=v_ssm_log_dt, ssm_b_re=v_ssm_b_re, ssm_b_im=v_ssm_b_im, ssm_c_re=v_ssm_c_re, ssm_c_im=v_ssm_c_im, ssm_d=v_ssm_d, ssm_w_glu=v_ssm_w_glu, ssm_b_glu=v_ssm_b_glu, ssm_w_out=v_ssm_w_out, conv_w=v_conv_w, conv_b=v_conv_b, conv_w_out=v_conv_w_out, w_o=v_w_o, ffn2_norm=v_ffn2_norm, ffn2_w_gate=v_ffn2_w_gate, ffn2_w_up=v_ffn2_w_up, ffn2_w_down=v_ffn2_w_down, final_norm=v_final_norm)
    names = list(P)
    sharded = ["ffn1_w_gate", "ffn1_w_up", "ffn1_w_down", "w_in", "ssm_w_glu", "ssm_w_out", "conv_w_out", "w_o",
               "ffn2_w_gate", "ffn2_w_up", "ffn2_w_down"]
    replicated = [n for n in names if n not in sharded and n != "conv_w"]

    S, D = x.shape[1], x.shape[2]
    W = ssm_d.shape[0]
    Dc = D // NDEV
    G, N = ssm_lambda_re.shape
    GN = G * N
    rows = GN // LANE
    xh = x.reshape(S, D)
    target = loss_target.reshape(S, D)
    xi, yi, ci = lax.axis_index("x"), lax.axis_index("y"), lax.axis_index("c")
    c_arr = jnp.reshape(ci, (1,)).astype(I32)
    q_arr = jnp.reshape(2 * xi + yi, (1,)).astype(I32)
    row = lambda v: v.reshape(1, -1)

    transposed = ("ffn1_w_gate", "ffn1_w_up", "ffn2_w_gate", "ffn2_w_up")
    local = lambda table, n: table[n].T if n in transposed else table[n]

    groups = [sharded[0:2], sharded[2:3], sharded[3:8] + ["conv_w"], sharded[8:11]]
    started = {}

    def gather_start(gi, after):
        srcs = [conv_w if n == "conv_w" else local(P, n).astype(BF) for n in groups[gi]]
        lands = [lax.empty((NDEV,) + s.shape, s.dtype) for s in srcs]
        started[gi] = _split_start("gather_start_%d" % gi, srcs, lands, _gather_copies, 4, after)
        return started[gi][4]

    def gathered(gi, after):
        srcs, lands = _split_wait("gather_wait_%d" % gi, started[gi], _gather_copies, after)
        return _gather_forward("gather_forward_%d" % gi, srcs, lands)

    tm = _pick(S, 512, SUBLANE)
    tk = _pick(D, K_TILE, LANE)
    ts = _pick(S, K_TILE, SUBLANE)
    tn = _pick(D, 1024, LANE)

    token0 = gather_start(0, [])
    u1, r1 = _rms_fwd("rms1", xh, row(ffn1_norm), deps=[token0])
    wg1, wu1 = gathered(0, u1)
    tokens = [gather_start(1, [wg1]), gather_start(2, [wg1])]
    h1, ffn1_saved, wd1 = _ffn_fwd("ffn1", u1, xh, wg1, wu1, lambda after: gathered(1, after)[0], deps=tokens)
    u2, r2 = _rms_fwd("rms2", h1, row(mix_norm))
    w_in_f, w_glu_f, w_so, w_co, w_o_f, cw_f = gathered(2, u2)
    token3 = gather_start(3, [w_in_f])
    w_glu_f = w_glu_f.reshape(W, W)
    w_o_f = w_o_f.reshape(D, D)
    cw = jnp.transpose(cw_f, (1, 0, 2)).reshape(3, W)
    (proj,) = _mm(
        "in_proj", (NDEV, S // tm, 1, D // tk),
        [(u2, (tm, tk), lambda b, i, j, k: (i, k)), (w_in_f, (None, tk, W), lambda b, i, j, k: (b, k, 0))],
        [(0, 1, NN, 0)], [], [((NDEV, S, W), F32, (None, tm, W), lambda b, i, j, k: (b, i, 0))], _plain, [(tm, W)],
        deps=[token3])

    ssm_in = (ssm_lambda_re, ssm_lambda_im, ssm_log_dt, ssm_b_re, ssm_b_im, ssm_c_re, ssm_c_im)
    (a_re, a_im, bd_re, bd_im, cd_re, cd_imn), ssm_vjp = jax.vjp(_ssm_params, *ssm_in)
    tiles, tch, tst = bd_re.shape
    bd_re_b, bd_im_b, cd_re_b, cd_imn_b = (t.astype(BF) for t in (bd_re, bd_im, cd_re, cd_imn))
    v_bf = proj[0].astype(BF)
    st_blk = lambda arr: (arr, (tm, tst), lambda b, i, j, k: (i, b))
    ch_blk = lambda arr: (arr, (tm, tch), lambda b, i, j, k: (i, b))
    bd_blk = lambda arr: (arr, (None, tch, tst), lambda b, i, j, k: (b, 0, 0))
    cd_blk = lambda arr: (arr, (None, tst, tch), lambda b, i, j, k: (b, 0, 0))
    st_out = lambda dt: ((S, GN), dt, (tm, tst), lambda b, i, j, k: (i, b))
    ch_out = lambda dt: ((S, W), dt, (tm, tch), lambda b, i, j, k: (i, b))
    d_blk = (row(ssm_d), (1, tch), lambda b, i, j, k: (0, b))
    v_blk = (proj, (None, tm, tch), lambda b, i, j, k: (0, i, b))

    bu_re, bu_im = _mm("ssm_bu", (tiles, S // tm, 1, 1), [ch_blk(v_bf), bd_blk(bd_re_b), bd_blk(bd_im_b)],
                       [(0, 1, NN, 0), (0, 2, NN, 1)], [], [st_out(F32), st_out(F32)], _plain, [(tm, tst)] * 2)
    s_re3, s_im3 = _scan_fwd(bu_re.reshape(S, rows, LANE), bu_im.reshape(S, rows, LANE), a_re, a_im)
    s_re_b = s_re3.reshape(S, GN).astype(BF)
    s_im_b = s_im3.reshape(S, GN).astype(BF)

    def y0_epilogue(accs, sides):
        y0 = accs[0] + sides[1] * sides[0]
        return y0, _gelu(y0)

    y0, y1 = _mm("ssm_y0", (tiles, S // tm, 1, 1),
                 [st_blk(s_re_b), st_blk(s_im_b), cd_blk(cd_re_b), cd_blk(cd_imn_b)],
                 [(0, 2, NN, 0), (1, 3, NN, 0)], [v_blk, d_blk], [ch_out(F32), ch_out(BF)], y0_epilogue, [(tm, tch)])

    tw = _pick(W, 512, LANE)

    def glu_epilogue(accs, sides):
        q = accs[0] + sides[1]
        return q, _gelu(sides[0]) * _sigmoid(q)

    q_pre, y2 = _mm("ssm_glu", (1, S // tm, W // tw, 1),
                    [(y1, (tm, W), lambda b, i, j, k: (i, 0)), (w_glu_f, (W, tw), lambda b, i, j, k: (0, j))],
                    [(0, 1, NN, 0)],
                    [(y0, (tm, tw), lambda b, i, j, k: (i, j)), (row(ssm_b_glu), (1, tw), lambda b, i, j, k: (0, j))],
                    [((S, W), F32, (tm, tw), lambda b, i, j, k: (i, j)), ((S, W), BF, (tm, tw), lambda b, i, j, k: (i, j))],
                    glu_epilogue, [(tm, tw)])

    yb, conv = _conv_fwd(proj, cw, row(conv_b))

    per = W // Dc
    ga_blk = (proj, (None, tm, Dc), lambda b, i, j, k: (4 + b // per, i, b % per))
    gb_blk = (proj, (None, tm, Dc), lambda b, i, j, k: (6 + b // per, i, b % per))
    dc_out = ((S, D), BF, (tm, Dc), lambda b, i, j, k: (i, b))

    def merge_epilogue(accs, sides):
        za, zb = accs
        return _sigmoid(sides[0]) * za + _sigmoid(sides[1]) * zb, za, zb

    merged, z_a, z_b = _mm(
        "mix_merge", (NDEV, S // tm, 1, 1),
        [(y2, (tm, W), lambda b, i, j, k: (i, 0)), (yb, (tm, W), lambda b, i, j, k: (i, 0)),
         (w_so, (None, W, Dc), lambda b, i, j, k: (b, 0, 0)), (w_co, (None, W, Dc), lambda b, i, j, k: (b, 0, 0))],
        [(0, 2, NN, 0), (1, 3, NN, 1)], [ga_blk, gb_blk], [dc_out, dc_out, dc_out], merge_epilogue, [(tm, Dc)] * 2)

    (h2,) = _mm("mix_out", (1, S // tm, D // tn, D // tk),
                [(merged, (tm, tk), lambda b, i, j, k: (i, k)), (w_o_f, (tk, tn), lambda b, i, j, k: (k, j))],
                [(0, 1, NN, 0)], [(h1, (tm, tn), lambda b, i, j, k: (i, j))],
                [((S, D), F32, (tm, tn), lambda b, i, j, k: (i, j))],
                lambda accs, sides: [sides[0] + accs[0]], [(tm, tn)])

    u3, r3 = _rms_fwd("rms3", h2, row(ffn2_norm))
    wg2, wu2, wd2 = gathered(3, u3)
    h3, ffn2_saved, _ = _ffn_fwd("ffn2", u3, h2, wg2, wu2, wd2)
    loss_vec, dh3, dh3_half, d_final_norm = _loss_head("loss_head", h3, row(final_norm), target)
    loss = lax.psum(loss_vec[0, 0], ("x", "y", "c"))

    grads, deltas, new_m, new_v = {}, {}, {}, {}

    def rs_begin(tag, parts):
        lands = _exchange_sibling("rs_sibling_" + tag, parts)
        sums = [_sum_sibling("rs_sum_%s_%d" % (tag, a), p, land, c_arr) for a, (p, land) in enumerate(zip(parts, lands))]
        lands2 = [lax.empty((3,) + sm.shape[1:], sm.dtype) for sm in sums]
        return _split_start("rs_chips_start_" + tag, sums, lands2, _chips_copies, 3)

    def rs_end(tag, group, begun, after):
        sums, lands2 = _split_wait("rs_chips_wait_" + tag, begun, _chips_copies, after)
        for n, sm, land2 in zip(group, sums, lands2):
            res = _finish_sharded("adamw_" + n, sm, land2, q_arr, local(P, n), local(M, n), local(V, n))
            grads[n], deltas[n], new_m[n], new_v[n] = [t.T if n in transposed else t for t in res]

    du3_fn, dwg2, dwu2, dwd2 = _ffn_bwd("ffn2", dh3_half, u3, ffn2_saved, wg2, wu2, wd2)
    rs_ffn2 = rs_begin("ffn2", [dwg2, dwu2, dwd2])
    du3 = du3_fn(deps=[rs_ffn2[4]])
    dh2, dh2_b, d_ffn2_norm = _rms_bwd("rms3_bwd", du3, h2, r3, row(ffn2_norm), dh3, 1.0)

    dg_out = ((2, S, W), BF, (None, tm, Dc), lambda b, i, j, k: (j // per, i, j % per))
    ga_blk2 = (proj, (None, tm, Dc), lambda b, i, j, k: (4 + j // per, i, j % per))
    gb_blk2 = (proj, (None, tm, Dc), lambda b, i, j, k: (6 + j // per, i, j % per))
    dcj = lambda arr: (arr, (tm, Dc), lambda b, i, j, k: (i, j))
    dcj_out = ((S, D), BF, (tm, Dc), lambda b, i, j, k: (i, j))

    def dmerge_epilogue(accs, sides):
        dm = accs[0]
        sa, sb = _sigmoid(sides[0]), _sigmoid(sides[1])
        za, zb = sides[2].astype(F32), sides[3].astype(F32)
        return dm * sa, dm * sb, dm * za * sa * (1.0 - sa), dm * zb * sb * (1.0 - sb)

    dz_a, dz_b, dga, dgb = _mm(
        "mix_out_dx", (1, S // tm, NDEV, D // tk),
        [(dh2_b, (tm, tk), lambda b, i, j, k: (i, k)), (w_o_f, (Dc, tk), lambda b, i, j, k: (j, k))],
        [(0, 1, NT, 0)], [ga_blk2, gb_blk2, dcj(z_a), dcj(z_b)], [dcj_out, dcj_out, dg_out, dg_out],
        dmerge_epilogue, [(tm, Dc)])

    td = _pick(D, 512, LANE)
    (dw_o,) = _mm("mix_out_dw", (1, D // td, D // tn, S // ts),
                  [(merged, (ts, td), lambda b, i, j, k: (k, i)), (dh2_b, (ts, tn), lambda b, i, j, k: (k, j))],
                  [(0, 1, TN, 0)], [], [((D, D), BF, (td, tn), lambda b, i, j, k: (i, j))], _plain, [(td, tn)])

    wout = ((NDEV, W, Dc), BF, (None, W, Dc), lambda b, i, j, k: (b, 0, 0))
    dw_so, dw_co = _mm(
        "mix_merge_dw", (NDEV, 1, 1, S // ts),
        [(y2, (ts, W), lambda b, i, j, k: (k, 0)), (yb, (ts, W), lambda b, i, j, k: (k, 0)),
         (dz_a, (ts, Dc), lambda b, i, j, k: (k, b)), (dz_b, (ts, Dc), lambda b, i, j, k: (k, b))],
        [(0, 2, TN, 0), (1, 3, TN, 1)], [], [wout, wout], _plain, [(W, Dc)] * 2)

    def dglu_epilogue(accs, sides):
        dy2, dyb = accs
        sq = _sigmoid(sides[1])
        return dy2 * _gelu(sides[0]) * sq * (1.0 - sq), dy2 * sq, dyb

    full_w = lambda arr: (arr, (tm, W), lambda b, i, j, k: (i, 0))
    full_w_out = lambda dt: ((S, W), dt, (tm, W), lambda b, i, j, k: (i, 0))
    dq, dy1p, dyb = _mm(
        "mix_merge_dx", (1, S // tm, 1, NDEV),
        [(dz_a, (tm, Dc), lambda b, i, j, k: (i, k)), (dz_b, (tm, Dc), lambda b, i, j, k: (i, k)),
         (w_so, (None, W, Dc), lambda b, i, j, k: (k, 0, 0)), (w_co, (None, W, Dc), lambda b, i, j, k: (k, 0, 0))],
        [(0, 2, NT, 0), (1, 3, NT, 1)], [full_w(y0), full_w(q_pre)], [full_w_out(BF), full_w_out(F32), full_w_out(F32)],
        dglu_epilogue, [(tm, W)] * 2)

    def dy0_epilogue(accs, sides):
        dy0 = (sides[0] + accs[0]) * _gelu_grad(sides[1])
        return dy0, dy0

    wj = lambda arr: (arr, (tm, tw), lambda b, i, j, k: (i, j))
    dy0, dy0_b = _mm("ssm_glu_dx", (1, S // tm, W // tw, 1),
                     [(dq, (tm, W), lambda b, i, j, k: (i, 0)), (w_glu_f, (tw, W), lambda b, i, j, k: (j, 0))],
                     [(0, 1, NT, 0)], [wj(dy1p), wj(y0)],
                     [((S, W), F32, (tm, tw), lambda b, i, j, k: (i, j)), ((S, W), BF, (tm, tw), lambda b, i, j, k: (i, j))],
                     dy0_epilogue, [(tm, tw)])

    (dw_glu,) = _mm("ssm_glu_dw", (1, W // tw, 1, S // ts),
                    [(y1, (ts, tw), lambda b, i, j, k: (k, i)), (dq, (ts, W), lambda b, i, j, k: (k, 0))],
                    [(0, 1, TN, 0)], [], [((W, W), BF, (tw, W), lambda b, i, j, k: (i, 0))], _plain, [(tw, W)])

    tr = _pick(S, 256, SUBLANE)
    rw = ((tr, W), lambda i: (i, 0))
    vec_w = ((1, W), F32, (1, W), lambda i: (0, 0))
    d_b_glu, d_ssm_d = _ew(
        "ssm_colsums", (S // tr,), [(dq,) + rw, (dy0,) + rw, (proj, (None, tr, W), lambda i: (0, i, 0))],
        [vec_w, vec_w],
        lambda dqv, dyv, vv: (jnp.sum(dqv.astype(F32), axis=0, keepdims=True), jnp.sum(dyv * vv, axis=0, keepdims=True)),
        acc=(0, 1))

    ds_re, ds_im = _mm("ssm_ds", (tiles, S // tm, 1, 1), [ch_blk(dy0_b), cd_blk(cd_re_b), cd_blk(cd_imn_b)],
                       [(0, 1, NT, 0), (0, 2, NT, 1)], [], [st_out(F32), st_out(F32)], _plain, [(tm, tst)] * 2)
    lam_re3, lam_im3, da_re, da_im = _scan_bwd(ds_re.reshape(S, rows, LANE), ds_im.reshape(S, rows, LANE),
                                               s_re3, s_im3, a_re, a_im)
    lam_re_b = lam_re3.reshape(S, GN).astype(BF)
    lam_im_b = lam_im3.reshape(S, GN).astype(BF)

    (dv,) = _mm("ssm_dv", (tiles, S // tm, 1, 1),
                [st_blk(lam_re_b), st_blk(lam_im_b), bd_blk(bd_re_b), bd_blk(bd_im_b)],
                [(0, 2, NT, 0), (1, 3, NT, 0)], [ch_blk(dy0), d_blk], [ch_out(BF)],
                lambda accs, sides: [accs[0] + sides[0] * sides[1]], [(tm, tch)])

    tok_ch = lambda arr: (arr, (ts, tch), lambda b, i, j, k: (k, b))
    tok_st = lambda arr: (arr, (ts, tst), lambda b, i, j, k: (k, b))
    bd_out = ((tiles, tch, tst), F32, (None, tch, tst), lambda b, i, j, k: (b, 0, 0))
    cd_out = ((tiles, tst, tch), F32, (None, tst, tch), lambda b, i, j, k: (b, 0, 0))
    dbd_re, dbd_im = _mm("ssm_dbd", (tiles, 1, 1, S // ts), [tok_ch(v_bf), tok_st(lam_re_b), tok_st(lam_im_b)],
                         [(0, 1, TN, 0), (0, 2, TN, 1)], [], [bd_out, bd_out], _plain, [(tch, tst)] * 2)
    dcd_re, dcd_imn = _mm("ssm_dcd", (tiles, 1, 1, S // ts), [tok_st(s_re_b), tok_st(s_im_b), tok_ch(dy0_b)],
                          [(0, 2, TN, 0), (1, 2, TN, 1)], [], [cd_out, cd_out], _plain, [(tst, tch)] * 2)
    d_ssm = ssm_vjp((da_re, da_im, dbd_re, dbd_im, dcd_re, dcd_imn))

    dbg, dcg, dval, d_conv_w_full, d_conv_b = _conv_bwd(dyb, proj, conv, cw)
    dproj = jnp.concatenate([dv[None], dbg[None], dcg[None], dval[None], dga, dgb], axis=0)

    (dw_in,) = _mm("in_proj_dw", (NDEV, D // td, 1, S // ts),
                   [(u2, (ts, td), lambda b, i, j, k: (k, i)), (dproj, (None, ts, W), lambda b, i, j, k: (b, k, 0))],
                   [(0, 1, TN, 0)], [], [((NDEV, D, W), BF, (None, td, W), lambda b, i, j, k: (b, i, 0))],
                   _plain, [(td, W)])
    rs_mixer = rs_begin("mixer", [dw_in, dw_glu.reshape(NDEV, W // NDEV, W), dw_so, dw_co, dw_o.reshape(NDEV, Dc, D)])
    (du2,) = _mm("in_proj_dx", (1, S // tm, D // tn, NDEV),
                 [(dproj, (None, tm, W), lambda b, i, j, k: (k, i, 0)), (w_in_f, (None, tn, W), lambda b, i, j, k: (k, j, 0))],
                 [(0, 1, NT, 0)], [], [((S, D), F32, (tm, tn), lambda b, i, j, k: (i, j))], _plain, [(tm, tn)],
                 deps=[rs_mixer[4]])
    dh1, dh1_half, d_mix_norm = _rms_bwd("rms2_bwd", du2, h1, r2, row(mix_norm), dh2, 0.5)
    rs_end("ffn2", sharded[8:11], rs_ffn2, dh1)

    du1_fn, dwg1, dwu1, dwd1 = _ffn_bwd("ffn1", dh1_half, u1, ffn1_saved, wg1, wu1, wd1)
    rs_ffn1 = rs_begin("ffn1", [dwg1, dwu1, dwd1])
    du1 = du1_fn(deps=[rs_ffn1[4]])
    dx, _, d_ffn1_norm = _rms_bwd("rms1_bwd", du1, xh, r1, row(ffn1_norm), dh1, 1.0)
    rs_end("mixer", sharded[3:8], rs_mixer, dx)

    small = dict(ffn1_norm=d_ffn1_norm, mix_norm=d_mix_norm, ffn2_norm=d_ffn2_norm, final_norm=d_final_norm,
                 ssm_lambda_re=d_ssm[0], ssm_lambda_im=d_ssm[1], ssm_log_dt=d_ssm[2], ssm_b_re=d_ssm[3],
                 ssm_b_im=d_ssm[4], ssm_c_re=d_ssm[5], ssm_c_im=d_ssm[6], ssm_d=d_ssm_d, ssm_b_glu=d_b_glu,
                 conv_b=d_conv_b)
    sizes = [P[n].size for n in replicated]
    total = sum(sizes) + d_conv_w_full.size
    padded = -(-total // (SUBLANE * LANE)) * (SUBLANE * LANE)

    def pack(parts):
        flat = jnp.concatenate([p.reshape(-1).astype(F32) for p in parts])
        return jnp.pad(flat, (0, padded - flat.shape[0])).reshape(padded // LANE, LANE)

    cw_zero = jnp.zeros_like(d_conv_w_full)
    (small_all,) = _all_gather("gather_small_grads", [pack([small[n] for n in replicated] + [d_conv_w_full])])
    g_pk, d_pk, m_pk, v_pk = _finish_replicated(
        "adamw_replicated", small_all, pack([P[n] for n in replicated] + [cw_zero]),
        pack([M[n] for n in replicated] + [cw_zero]), pack([V[n] for n in replicated] + [cw_zero + 1.0]))
    off = 0
    for n, sz in zip(replicated, sizes):
        for store, pk in ((grads, g_pk), (deltas, d_pk), (new_m, m_pk), (new_v, v_pk)):
            store[n] = pk.reshape(-1)[off:off + sz].reshape(P[n].shape)
        off += sz
    g_cw_full = g_pk.reshape(-1)[off:off + d_conv_w_full.size].reshape(d_conv_w_full.shape)
    cwl = conv_w.shape[1]
    g_cw = lax.dynamic_slice_in_dim(g_cw_full, (4 * xi + 2 * yi + ci) * cwl, cwl, axis=1)
    full3 = ((3, cwl), lambda i: (0, 0))
    grads["conv_w"], deltas["conv_w"], new_m["conv_w"], new_v["conv_w"] = _ew(
        "adamw_conv_w", (1,), [(g_cw,) + full3, (conv_w,) + full3, (m_conv_w,) + full3, (v_conv_w,) + full3],
        [((3, cwl), F32) + full3] * 4, lambda g, w, m, v: (g,) + _adamw(w, g, m, v))
    rs_end("ffn1", sharded[0:3], rs_ffn1, g_pk)

    return (loss, dx.reshape(x.shape), *[grads[n] for n in names], *[deltas[n] for n in names],
            *[new_m[n] for n in names], *[new_v[n] for n in names])
```

```python
import math

import jax
import jax.numpy as jnp
from jax import lax
from jax.experimental import pallas as pl
from jax.experimental.pallas import tpu as pltpu

F32 = jnp.float32
BF = jnp.bfloat16
I32 = jnp.int32
MESH = pl.DeviceIdType.MESH
LANE = 128
SUBLANE = 8
NDEV = 8
K_TILE = 2048
DMA_CHUNKS = 4
EPS = 1e-6
ADAM_LR, ADAM_B1, ADAM_B2, ADAM_EPS, ADAM_WD, ADAM_STEP = 0.001, 0.9, 0.999, 1e-08, 0.01, 10
NN = ((1,), (0,))
NT = ((1,), (1,))
TN = ((0,), (0,))
HBM = pl.BlockSpec(memory_space=pltpu.HBM)


def _pick(n, pref, mult):
    t = min(pref, n)
    t -= t % mult
    while t >= mult:
        if n % t == 0:
            return t
        t -= mult
    return n


def _sigmoid(x):
    return 1.0 / (1.0 + jnp.exp(-x))


_GELU_C = math.sqrt(2.0 / math.pi)


def _gelu(x):
    return 0.5 * x * (1.0 + jnp.tanh(_GELU_C * (x + 0.044715 * x * x * x)))


def _gelu_grad(x):
    t = jnp.tanh(_GELU_C * (x + 0.044715 * x * x * x))
    return 0.5 * (1.0 + t) + 0.5 * x * (1.0 - t * t) * _GELU_C * (1.0 + 3.0 * 0.044715 * x * x)


def _dep_specs(deps, rank):
    return [(d, d.shape, lambda *_, nd=d.ndim: (0,) * nd) for d in deps]


def _mm(name, grid, ops, pairs, sides, outs, epilogue, acc_shapes, deps=()):
    nk = grid[-1]
    n_ops, n_sides, n_outs = len(ops), len(sides), len(outs)
    dep_specs = _dep_specs(deps, len(grid))
    n_deps = len(dep_specs)

    def body(*refs):
        op_refs = refs[:n_ops]
        side_refs = refs[n_ops:n_ops + n_sides]
        out_refs = refs[n_ops + n_sides + n_deps:n_ops + n_sides + n_deps + n_outs]
        acc_refs = refs[n_ops + n_sides + n_deps + n_outs:]

        def partials():
            res = [None] * len(acc_shapes)
            for ia, ib, dims, ai in pairs:
                p = lax.dot_general(op_refs[ia][...], op_refs[ib][...], (dims, ((), ())),
                                    preferred_element_type=F32)
                res[ai] = p if res[ai] is None else res[ai] + p
            return res

        def finish(accs):
            vals = epilogue(accs, [s[...] for s in side_refs])
            for o, v in zip(out_refs, vals):
                o[...] = v.astype(o.dtype)

        if nk == 1:
            finish(partials())
        else:
            k = pl.program_id(len(grid) - 1)

            @pl.when(k == 0)
            def _():
                for a, p in zip(acc_refs, partials()):
                    a[...] = p

            @pl.when(k > 0)
            def _():
                for a, p in zip(acc_refs, partials()):
                    a[...] += p

            @pl.when(k == nk - 1)
            def _():
                finish([a[...] for a in acc_refs])

    return pl.pallas_call(
        body, name=name, grid=grid,
        in_specs=[pl.BlockSpec(b, m) for (_, b, m) in list(ops) + list(sides) + dep_specs],
        out_specs=[pl.BlockSpec(b, m) for (_, _, b, m) in outs],
        out_shape=[jax.ShapeDtypeStruct(s, d) for (s, d, _, _) in outs],
        scratch_shapes=[pltpu.VMEM(s, F32) for s in acc_shapes] if nk > 1 else [],
        compiler_params=pltpu.CompilerParams(
            dimension_semantics=("parallel",) * (len(grid) - 1) + ("arbitrary",)),
    )(*[a for (a, _, _) in list(ops) + list(sides) + dep_specs])


def _ew(name, grid, ins, outs, fn, acc=(), deps=()):
    n_in = len(ins)
    dep_specs = _dep_specs(deps, len(grid))

    def body(*refs):
        vals = fn(*[r[...] for r in refs[:n_in]])
        first = pl.program_id(0) == 0
        for idx, (o, v) in enumerate(zip(refs[n_in + len(dep_specs):], vals)):
            if idx in acc:
                @pl.when(first)
                def _(o=o, v=v):
                    o[...] = v.astype(o.dtype)

                @pl.when(jnp.logical_not(first))
                def _(o=o, v=v):
                    o[...] += v.astype(o.dtype)
            else:
                o[...] = v.astype(o.dtype)

    return pl.pallas_call(
        body, name=name, grid=grid,
        in_specs=[pl.BlockSpec(b, m) for (_, b, m) in list(ins) + dep_specs],
        out_specs=[pl.BlockSpec(b, m) for (_, _, b, m) in outs],
        out_shape=[jax.ShapeDtypeStruct(s, d) for (s, d, _, _) in outs],
        compiler_params=pltpu.CompilerParams(
            dimension_semantics=(("arbitrary",) if acc else ("parallel",)) * len(grid)),
    )(*[a for (a, _, _) in list(ins) + dep_specs])


def _position():
    x, y, c = lax.axis_index("x"), lax.axis_index("y"), lax.axis_index("c")
    chips = [(1 - x, y), (x, 1 - y), (1 - x, 1 - y)]
    return x, y, c, chips


def _all_gather(name, shards):
    n = len(shards)

    def body(*refs):
        xs, outs = refs[:n], refs[n:2 * n]
        send_sems, recv_sems, local_sems = refs[2 * n:]
        x, y, c, chips = _position()
        me, sibling = (x, y, c), (x, y, 1 - c)

        def copy(a, k, block, to, src=None):
            dst = outs[a].at[4 * block[0] + 2 * block[1] + block[2]]
            return pltpu.make_async_remote_copy(
                src_ref=dst if src is None else src, dst_ref=dst,
                send_sem=send_sems.at[a, k], recv_sem=recv_sems.at[a, k],
                device_id=to, device_id_type=MESH)

        mine = [pltpu.make_async_copy(xs[a], outs[a].at[4 * x + 2 * y + c], local_sems.at[a]) for a in range(n)]
        for cp in mine:
            cp.start()
        first = []
        for a in range(n):
            first.append(copy(a, 0, me, sibling, src=xs[a]))
            first += [copy(a, 1 + j, me, (*chip, c), src=xs[a]) for j, chip in enumerate(chips)]
        for cp in first:
            cp.start()
        passed = []
        for a in range(n):
            for j, chip in enumerate(chips):
                copy(a, 1 + j, (*chip, c), me).wait_recv()
                cp = copy(a, 4 + j, (*chip, c), sibling)
                cp.start()
                passed.append(cp)
        for a in range(n):
            copy(a, 0, sibling, me).wait_recv()
            for j, chip in enumerate(chips):
                copy(a, 4 + j, (*chip, 1 - c), me).wait_recv()
        for cp in first + passed:
            cp.wait_send()
        for cp in mine:
            cp.wait()

    return pl.pallas_call(
        body, name=name,
        out_shape=[jax.ShapeDtypeStruct((NDEV,) + s.shape, s.dtype) for s in shards],
        in_specs=[HBM] * n, out_specs=[HBM] * n,
        scratch_shapes=[pltpu.SemaphoreType.DMA((n, 7)), pltpu.SemaphoreType.DMA((n, 7)),
                        pltpu.SemaphoreType.DMA((n,))],
    )(*shards)


SEM = pl.BlockSpec(memory_space=pltpu.SEMAPHORE)
EFFECT = pltpu.SideEffectType.DATAFLOW_SIDE_EFFECTING


def _in_hbm(v):
    return pltpu.with_memory_space_constraint(v, pltpu.HBM)


def _split_start(name, srcs, lands, make_copies, n_per, after=()):
    n = len(srcs)
    after = list(after)

    def body(*refs):
        send_sems, recv_sems = refs[2 * n + len(after)], refs[2 * n + len(after) + 1]
        for cp in make_copies(refs[:n], refs[n:2 * n], send_sems, recv_sems):
            cp.start()
        refs[-1][...] = jnp.zeros_like(refs[-1])

    outs = pl.pallas_call(
        body, name=name,
        out_shape=(pltpu.SemaphoreType.DMA((n * n_per,)), pltpu.SemaphoreType.DMA((n * n_per,)),
                   *[pltpu.HBM(v.shape, v.dtype) for v in list(srcs) + list(lands)],
                   jax.ShapeDtypeStruct((SUBLANE, LANE), F32)),
        in_specs=[HBM] * (2 * n) + [pl.BlockSpec(memory_space=pl.ANY)] * len(after),
        out_specs=(SEM, SEM, *[HBM] * (2 * n), pl.BlockSpec(memory_space=pltpu.VMEM)),
        input_output_aliases={i: 2 + i for i in range(2 * n)},
        compiler_params=pltpu.CompilerParams(has_side_effects=EFFECT),
    )(*[_in_hbm(v) for v in list(srcs) + list(lands)], *after)
    return outs[0], outs[1], list(outs[2:2 + n]), list(outs[2 + n:2 + 2 * n]), outs[-1]


def _split_wait(name, started, make_copies, after):
    send_sems, recv_sems, srcs, lands, _ = started
    n = len(srcs)

    def body(*refs):
        for cp in make_copies(refs[:n], refs[n:2 * n], refs[2 * n], refs[2 * n + 1]):
            cp.wait_send()
            cp.wait_recv()

    order = [] if after is None else [after]
    outs = pl.pallas_call(
        body, name=name,
        out_shape=tuple(pltpu.HBM(v.shape, v.dtype) for v in srcs + lands),
        in_specs=[HBM] * (2 * n) + [SEM, SEM] + [pl.BlockSpec(memory_space=pl.ANY)] * len(order),
        out_specs=tuple([HBM] * (2 * n)),
        input_output_aliases={i: i for i in range(2 * n)},
        compiler_params=pltpu.CompilerParams(has_side_effects=EFFECT),
    )(*srcs, *lands, send_sems, recv_sems, *order)
    return list(outs[:n]), list(outs[n:])


def _gather_copies(xs, lands, send_sems, recv_sems):
    x, y, c, chips = _position()
    copies = []
    for a in range(len(xs)):
        for k, peer in enumerate([(x, y, 1 - c)] + [(*chip, c) for chip in chips]):
            copies.append(pltpu.make_async_remote_copy(
                src_ref=xs[a], dst_ref=lands[a].at[4 * x + 2 * y + c],
                send_sem=send_sems.at[4 * a + k], recv_sem=recv_sems.at[4 * a + k], device_id=peer, device_id_type=MESH))
    return copies


def _chips_copies(ps, lands, send_sems, recv_sems):
    x, y, c, chips = _position()
    copies = []
    for a in range(len(ps)):
        for j, chip in enumerate(chips):
            copies.append(pltpu.make_async_remote_copy(
                src_ref=ps[a].at[2 * chip[0] + chip[1]], dst_ref=lands[a].at[j],
                send_sem=send_sems.at[3 * a + j], recv_sem=recv_sems.at[3 * a + j], device_id=(*chip, c),
                device_id_type=MESH))
    return copies


def _everyone_copies(xs, lands, send_sems, recv_sems):
    x, y, c, _ = _position()
    flip = lambda v, bit: 1 - v if bit else v
    copies = []
    for a in range(len(xs)):
        for k in range(1, NDEV):
            copies.append(pltpu.make_async_remote_copy(
                src_ref=xs[a], dst_ref=lands[a].at[4 * x + 2 * y + c],
                send_sem=send_sems.at[7 * a + k - 1], recv_sem=recv_sems.at[7 * a + k - 1],
                device_id=(flip(x, k & 4), flip(y, k & 2), flip(c, k & 1)), device_id_type=MESH))
    return copies


def _row_chunks(rows, dtype):
    unit = SUBLANE * (4 // jnp.dtype(dtype).itemsize)
    units = rows // unit
    if rows % unit or units < 2:
        return [(0, rows)]
    k = min(DMA_CHUNKS, units)
    sizes = [(units // k + (1 if i < units % k else 0)) * unit for i in range(k)]
    return [(sum(sizes[:i]), sz) for i, sz in enumerate(sizes)]


def _gather_forward(name, shards, lands):
    n = len(shards)

    def body(*refs):
        xs, ins, outs = refs[:n], refs[n:2 * n], refs[2 * n:3 * n]
        send_sems, recv_sems, local_sems = refs[3 * n:]
        x, y, c, chips = _position()
        me = 4 * x + 2 * y + c
        whole, chunks = [], []
        for a in range(n):
            rows = _row_chunks(xs[a].shape[0], xs[a].dtype)
            whole.append(pltpu.make_async_copy(xs[a], outs[a].at[me], local_sems.at[a]))
            chunks += [pltpu.make_async_copy(xs[a].at[pl.ds(r0, nr)], outs[a].at[me, pl.ds(r0, nr)], local_sems.at[a])
                       for r0, nr in rows]
            for j, chip in enumerate(chips):
                slot = 4 * chip[0] + 2 * chip[1]

                def to_sibling(src, dst):
                    return pltpu.make_async_remote_copy(
                        src_ref=src, dst_ref=dst, send_sem=send_sems.at[a, j], recv_sem=recv_sems.at[a, j],
                        device_id=(x, y, 1 - c), device_id_type=MESH)

                whole.append(to_sibling(ins[a].at[slot + c], outs[a].at[slot + 1 - c]))
                chunks += [to_sibling(ins[a].at[slot + c, pl.ds(r0, nr)], outs[a].at[slot + c, pl.ds(r0, nr)])
                           for r0, nr in rows]
        for cp in chunks:
            cp.start()
        for cp in whole:
            cp.wait()

    return pl.pallas_call(
        body, name=name,
        out_shape=[jax.ShapeDtypeStruct(l.shape, l.dtype) for l in lands],
        in_specs=[HBM] * (2 * n), out_specs=[HBM] * n,
        input_output_aliases={n + a: a for a in range(n)},
        scratch_shapes=[pltpu.SemaphoreType.DMA((n, 3)), pltpu.SemaphoreType.DMA((n, 3)),
                        pltpu.SemaphoreType.DMA((n,))],
    )(*shards, *lands)


def _exchange_sibling(name, grads):
    n = len(grads)

    def body(*refs):
        gs, lands = refs[:n], refs[n:2 * n]
        send_sems, recv_sems = refs[2 * n:]
        x, y, c, _ = _position()
        copies = []
        for a in range(n):
            for q in range(4):
                copies.append(pltpu.make_async_remote_copy(
                    src_ref=gs[a].at[2 * q + 1 - c], dst_ref=lands[a].at[q],
                    send_sem=send_sems.at[a, q], recv_sem=recv_sems.at[a, q],
                    device_id=(x, y, 1 - c), device_id_type=MESH))
        for cp in copies:
            cp.start()
        for cp in copies:
            cp.wait()

    return pl.pallas_call(
        body, name=name,
        out_shape=[jax.ShapeDtypeStruct((4,) + g.shape[1:], g.dtype) for g in grads],
        in_specs=[HBM] * n, out_specs=[HBM] * n,
        scratch_shapes=[pltpu.SemaphoreType.DMA((n, 4)), pltpu.SemaphoreType.DMA((n, 4))],
    )(*grads)


def _sum_sibling(name, g, land, c_arr):
    _, R, C = g.shape
    tr = _pick(R, 512, SUBLANE)

    def body(c_ref, g_ref, l_ref, o_ref):
        o_ref[...] = (g_ref[...].astype(F32) + l_ref[...].astype(F32)).astype(o_ref.dtype)

    return pl.pallas_call(
        body, name=name,
        grid_spec=pltpu.PrefetchScalarGridSpec(
            num_scalar_prefetch=1, grid=(4, R // tr),
            in_specs=[pl.BlockSpec((None, tr, C), lambda q, i, cr: (2 * q + cr[0], i, 0)),
                      pl.BlockSpec((None, tr, C), lambda q, i, cr: (q, i, 0))],
            out_specs=pl.BlockSpec((None, tr, C), lambda q, i, cr: (q, i, 0))),
        out_shape=jax.ShapeDtypeStruct((4, R, C), g.dtype),
        compiler_params=pltpu.CompilerParams(dimension_semantics=("parallel", "parallel")),
    )(c_arr, g, land)


def _adamw(w, g, m, v):
    m = ADAM_B1 * m + (1.0 - ADAM_B1) * g
    v = ADAM_B2 * v + (1.0 - ADAM_B2) * (g * g)
    m_hat = m / (1.0 - ADAM_B1 ** ADAM_STEP)
    v_hat = v / (1.0 - ADAM_B2 ** ADAM_STEP)
    delta = -ADAM_LR * (m_hat / (jnp.sqrt(v_hat) + ADAM_EPS) + ADAM_WD * w)
    return delta, m, v


def _finish_sharded(name, sums, land, q_arr, w, m, v):
    R, C = w.shape
    tr = _pick(R, 256, SUBLANE)

    def body(q_ref, p_ref, l_ref, w_ref, m_ref, v_ref, g_out, d_out, m_out, v_out):
        g = p_ref[...].astype(F32)
        for j in range(3):
            g = g + l_ref[j].astype(F32)
        d, mn, vn = _adamw(w_ref[...], g, m_ref[...], v_ref[...])
        g_out[...] = g
        d_out[...] = d
        m_out[...] = mn
        v_out[...] = vn

    row = pl.BlockSpec((tr, C), lambda i, qr: (i, 0))
    return pl.pallas_call(
        body, name=name,
        grid_spec=pltpu.PrefetchScalarGridSpec(
            num_scalar_prefetch=1, grid=(R // tr,),
            in_specs=[pl.BlockSpec((None, tr, C), lambda i, qr: (qr[0], i, 0)),
                      pl.BlockSpec((3, tr, C), lambda i, qr: (0, i, 0)), row, row, row],
            out_specs=[row] * 4),
        out_shape=[jax.ShapeDtypeStruct((R, C), F32)] * 4,
        compiler_params=pltpu.CompilerParams(dimension_semantics=("parallel",)),
    )(q_arr, sums, land, w, m, v)


def _finish_replicated(name, gathered, w, m, v):
    _, R, C = gathered.shape
    tr = _pick(R, 256, SUBLANE)

    def fn(gv, wv, mv, vv):
        g = gv[0]
        for d in range(1, NDEV):
            g = g + gv[d]
        dl, mn, vn = _adamw(wv, g, mv, vv)
        return g, dl, mn, vn

    row = ((tr, C), lambda i: (i, 0))
    return _ew(name, (R // tr,),
               [(gathered, (NDEV, tr, C), lambda i: (0, i, 0)), (w,) + row, (m,) + row, (v,) + row],
               [((R, C), F32) + row] * 4, fn)


def _rms_fwd(name, h, g, deps=()):
    S, D = h.shape
    tr = _pick(S, 256, SUBLANE)

    def fn(hv, gv):
        r = lax.rsqrt(jnp.mean(hv * hv, axis=-1, keepdims=True) + EPS)
        return hv * r * gv, r

    return _ew(name, (S // tr,),
               [(h, (tr, D), lambda i: (i, 0)), (g, (1, D), lambda i: (0, 0))],
               [((S, D), BF, (tr, D), lambda i: (i, 0)), ((S, 1), F32, (tr, 1), lambda i: (i, 0))], fn, deps=deps)


def _rms_bwd(name, du, h, r, g, dres, scale):
    S, D = h.shape
    tr = _pick(S, 256, SUBLANE)

    def fn(duv, hv, rv, gv, drv):
        xn = hv * rv
        dxn = duv * gv
        dh = drv + rv * (dxn - xn * jnp.mean(dxn * xn, axis=-1, keepdims=True))
        return dh, scale * dh, jnp.sum(duv * xn, axis=0, keepdims=True)

    row = ((tr, D), lambda i: (i, 0))
    return _ew(name, (S // tr,),
               [(du,) + row, (h,) + row, (r, (tr, 1), lambda i: (i, 0)), (g, (1, D), lambda i: (0, 0)), (dres,) + row],
               [((S, D), F32) + row, ((S, D), BF) + row, ((1, D), F32, (1, D), lambda i: (0, 0))], fn, acc=(2,))


def _loss_head(name, h, g, target):
    S, D = h.shape
    tr = _pick(S, 256, SUBLANE)

    def fn(hv, gv, tv):
        r = lax.rsqrt(jnp.mean(hv * hv, axis=-1, keepdims=True) + EPS)
        xn = hv * r
        diff = xn * gv - tv
        loss = 0.5 * jnp.sum(jnp.mean(diff * diff, axis=-1, keepdims=True))
        dout = diff / D
        dxn = dout * gv
        dh = r * (dxn - xn * jnp.mean(dxn * xn, axis=-1, keepdims=True))
        return (jnp.zeros((1, LANE), F32) + loss, dh, 0.5 * dh, jnp.sum(dout * xn, axis=0, keepdims=True))

    row = ((tr, D), lambda i: (i, 0))
    return _ew(name, (S // tr,),
               [(h,) + row, (g, (1, D), lambda i: (0, 0)), (target,) + row],
               [((1, LANE), F32, (1, LANE), lambda i: (0, 0)), ((S, D), F32) + row, ((S, D), BF) + row,
                ((1, D), F32, (1, D), lambda i: (0, 0))], fn, acc=(0, 3))


def _ffn_fwd(tag, u, h, wg, wu, wd, deps=()):
    S, D = u.shape
    Fs = wg.shape[1]
    tm, tk = _pick(S, 512, SUBLANE), _pick(D, K_TILE, LANE)

    def up_epilogue(accs, sides):
        gt, up = accs
        return gt, up, gt * _sigmoid(gt) * up

    act = ((NDEV, S, Fs), BF, (None, tm, Fs), lambda b, i, j, k: (b, i, 0))
    gt, up, a = _mm(
        tag + "_up", (NDEV, S // tm, 1, D // tk),
        [(u, (tm, tk), lambda b, i, j, k: (i, k)),
         (wg, (None, Fs, tk), lambda b, i, j, k: (b, 0, k)), (wu, (None, Fs, tk), lambda b, i, j, k: (b, 0, k))],
        [(0, 1, NT, 0), (0, 2, NT, 1)], [], [act, act, act], up_epilogue, [(tm, Fs), (tm, Fs)], deps=deps)
    if callable(wd):
        wd = wd(a)
    tn = _pick(D, 1024, LANE)
    (hn,) = _mm(
        tag + "_down", (1, S // tm, D // tn, NDEV),
        [(a, (None, tm, Fs), lambda b, i, j, k: (k, i, 0)), (wd, (None, Fs, tn), lambda b, i, j, k: (k, 0, j))],
        [(0, 1, NN, 0)], [(h, (tm, tn), lambda b, i, j, k: (i, j))],
        [((S, D), F32, (tm, tn), lambda b, i, j, k: (i, j))],
        lambda accs, sides: [sides[0] + 0.5 * accs[0]], [(tm, tn)])
    return hn, (gt, up, a), wd


def _ffn_bwd(tag, dhs, u, saved, wg, wu, wd):
    gt, up, a = saved
    S, D = u.shape
    Fs = wg.shape[1]
    tm, tk = _pick(S, 512, SUBLANE), _pick(D, K_TILE, LANE)
    act_in = lambda arr: (arr, (None, tm, Fs), lambda b, i, j, k: (b, i, 0))
    act_out = ((NDEV, S, Fs), BF, (None, tm, Fs), lambda b, i, j, k: (b, i, 0))

    def act_epilogue(accs, sides):
        da = accs[0]
        gtv, upv = sides[0].astype(F32), sides[1].astype(F32)
        sg = _sigmoid(gtv)
        return da * upv * sg * (1.0 + gtv * (1.0 - sg)), da * gtv * sg

    dgt, dup = _mm(
        tag + "_dact", (NDEV, S // tm, 1, D // tk),
        [(dhs, (tm, tk), lambda b, i, j, k: (i, k)), (wd, (None, Fs, tk), lambda b, i, j, k: (b, 0, k))],
        [(0, 1, NT, 0)], [act_in(gt), act_in(up)], [act_out, act_out], act_epilogue, [(tm, Fs)])

    ts = _pick(S, K_TILE, SUBLANE)
    tn = _pick(D, 1024, LANE)
    wgrad = ((NDEV, Fs, D), BF, (None, Fs, tn), lambda b, i, j, k: (b, 0, j))
    tok = lambda arr: (arr, (None, ts, Fs), lambda b, i, j, k: (b, k, 0))
    (dwd,) = _mm(
        tag + "_dwd", (NDEV, 1, D // tn, S // ts),
        [tok(a), (dhs, (ts, tn), lambda b, i, j, k: (k, j))],
        [(0, 1, TN, 0)], [], [wgrad], lambda accs, sides: accs, [(Fs, tn)])
    dwg, dwu = _mm(
        tag + "_dwgu", (NDEV, 1, D // tn, S // ts),
        [tok(dgt), tok(dup), (u, (ts, tn), lambda b, i, j, k: (k, j))],
        [(0, 2, TN, 0), (1, 2, TN, 1)], [], [wgrad, wgrad], lambda accs, sides: accs, [(Fs, tn), (Fs, tn)])

    def du(deps=()):
        return _mm(
            tag + "_du", (1, S // tm, D // tn, NDEV),
            [(dgt, (None, tm, Fs), lambda b, i, j, k: (k, i, 0)), (dup, (None, tm, Fs), lambda b, i, j, k: (k, i, 0)),
             (wg, (None, Fs, tn), lambda b, i, j, k: (k, 0, j)), (wu, (None, Fs, tn), lambda b, i, j, k: (k, 0, j))],
            [(0, 2, NN, 0), (1, 3, NN, 0)], [], [((S, D), F32, (tm, tn), lambda b, i, j, k: (i, j))],
            lambda accs, sides: accs, [(tm, tn)], deps=deps)[0]

    return du, dwg, dwu, dwd


def _ssm_params(lam_re, lam_im, log_dt, b_re, b_im, c_re, c_im):
    G, N = lam_re.shape
    C = b_re.shape[2]
    lam_re = jnp.minimum(lam_re, -1e-4)
    dt = jnp.exp(log_dt)[:, None]
    mag = jnp.exp(lam_re * dt)
    a_re = mag * jnp.cos(lam_im * dt)
    a_im = mag * jnp.sin(lam_im * dt)
    den = lam_re * lam_re + lam_im * lam_im
    p = a_re - 1.0
    f_re = ((p * lam_re + a_im * lam_im) / den)[:, :, None]
    f_im = ((a_im * lam_re - p * lam_im) / den)[:, :, None]
    bb_re = f_re * b_re - f_im * b_im
    bb_im = f_re * b_im + f_im * b_re
    gpt = LANE // C
    tiles = G // gpt
    eye = jnp.eye(gpt, dtype=F32)

    def bd(bb):
        return jnp.einsum("bgnc,gh->bgchn", bb.reshape(tiles, gpt, N, C), eye).reshape(tiles, gpt * C, gpt * N)

    def cd(cc):
        return jnp.einsum("bgcn,gh->bgnhc", cc.reshape(tiles, gpt, C, N), eye).reshape(tiles, gpt * N, gpt * C)

    rows = G * N // LANE
    return (a_re.reshape(rows, LANE), a_im.reshape(rows, LANE), bd(bb_re), bd(bb_im), cd(c_re), cd(-c_im))


def _scan_fwd(bu_re, bu_im, a_re, a_im):
    S, R, _ = bu_re.shape
    tc = _pick(S, 256, SUBLANE)

    def body(bre, bim, are, aim, sre, sim, carry):
        @pl.when(pl.program_id(0) == 0)
        def _():
            carry[...] = jnp.zeros_like(carry)

        ar, ai = are[...], aim[...]

        def step(t, c):
            pr, pi = c
            nr = ar * pr - ai * pi + bre[t]
            ni = ar * pi + ai * pr + bim[t]
            sre[t] = nr
            sim[t] = ni
            return nr, ni

        pr, pi = lax.fori_loop(0, tc, step, (carry[0], carry[1]), unroll=8)
        carry[0] = pr
        carry[1] = pi

    blk = pl.BlockSpec((tc, R, LANE), lambda i: (i, 0, 0))
    par = pl.BlockSpec((R, LANE), lambda i: (0, 0))
    return pl.pallas_call(
        body, name="ssm_scan_fwd", grid=(S // tc,),
        in_specs=[blk, blk, par, par], out_specs=[blk, blk],
        out_shape=[jax.ShapeDtypeStruct((S, R, LANE), F32)] * 2,
        scratch_shapes=[pltpu.VMEM((2, R, LANE), F32)],
        compiler_params=pltpu.CompilerParams(dimension_semantics=("arbitrary",)),
    )(bu_re, bu_im, a_re, a_im)


def _scan_bwd(ds_re, ds_im, s_re, s_im, a_re, a_im):
    S, R, _ = ds_re.shape
    tc = _pick(S, 256, SUBLANE)
    nc = S // tc

    def body(dre, dim_, sre, sim, are, aim, lre, lim, dar, dai, carry):
        @pl.when(pl.program_id(0) == 0)
        def _():
            carry[...] = jnp.zeros_like(carry)
            dar[...] = jnp.zeros_like(dar)
            dai[...] = jnp.zeros_like(dai)

        ar, ai = are[...], aim[...]

        def step(tt, c):
            t = tc - 1 - tt
            lr, li, gr, gi = c
            sr, si = sre[t], sim[t]
            gr = gr + lr * sr + li * si
            gi = gi + li * sr - lr * si
            nlr = dre[t] + ar * lr + ai * li
            nli = dim_[t] + ar * li - ai * lr
            lre[t] = nlr
            lim[t] = nli
            return nlr, nli, gr, gi

        lr, li, gr, gi = lax.fori_loop(0, tc, step, (carry[0], carry[1], dar[...], dai[...]), unroll=8)
        carry[0] = lr
        carry[1] = li
        dar[...] = gr
        dai[...] = gi

    blk = pl.BlockSpec((tc, R, LANE), lambda i: (nc - 1 - i, 0, 0))
    par = pl.BlockSpec((R, LANE), lambda i: (0, 0))
    return pl.pallas_call(
        body, name="ssm_scan_bwd", grid=(nc,),
        in_specs=[blk, blk, blk, blk, par, par], out_specs=[blk, blk, par, par],
        out_shape=[jax.ShapeDtypeStruct((S, R, LANE), F32)] * 2 + [jax.ShapeDtypeStruct((R, LANE), F32)] * 2,
        scratch_shapes=[pltpu.VMEM((2, R, LANE), F32)],
        compiler_params=pltpu.CompilerParams(dimension_semantics=("arbitrary",)),
    )(ds_re, ds_im, s_re, s_im, a_re, a_im)


def _shift_down(z, k):
    t = lax.broadcasted_iota(I32, z.shape, 0)
    return jnp.where(t >= k, pltpu.roll(z, k, 0), 0.0)


def _shift_up(z, k):
    n = z.shape[0]
    t = lax.broadcasted_iota(I32, z.shape, 0)
    return jnp.where(t < n - k, pltpu.roll(z, n - k, 0), 0.0)


def _conv_fwd(proj, cw, cb):
    _, S, W = proj.shape
    ct = _pick(W, 256, LANE)

    def fn(bg, cg, val, w, b):
        z = cg * val
        conv = b + w[0:1] * _shift_down(z, 2) + w[1:2] * _shift_down(z, 1) + w[2:3] * z
        return bg * conv, conv

    sl = lambda s: (proj, (None, S, ct), lambda j, s=s: (s, 0, j))
    col = ((S, ct), lambda j: (0, j))
    return _ew("conv_fwd", (W // ct,),
               [sl(1), sl(2), sl(3), (cw, (3, ct), lambda j: (0, j)), (cb, (1, ct), lambda j: (0, j))],
               [((S, W), BF) + col, ((S, W), F32) + col], fn)


def _conv_bwd(dyb, proj, conv, cw):
    _, S, W = proj.shape
    ct = _pick(W, 256, LANE)

    def fn(dy, bg, cg, val, cv, w):
        z = cg * val
        z1, z2 = _shift_down(z, 1), _shift_down(z, 2)
        dconv = dy * bg
        dz = w[2:3] * dconv + w[1:2] * _shift_up(dconv, 1) + w[0:1] * _shift_up(dconv, 2)
        dw = jnp.concatenate([jnp.sum(dconv * z2, axis=0, keepdims=True), jnp.sum(dconv * z1, axis=0, keepdims=True),
                              jnp.sum(dconv * z, axis=0, keepdims=True)], axis=0)
        return dy * cv, dz * val, dz * cg, dw, jnp.sum(dconv, axis=0, keepdims=True)

    sl = lambda s: (proj, (None, S, ct), lambda j, s=s: (s, 0, j))
    col = ((S, ct), lambda j: (0, j))
    return _ew("conv_bwd", (W // ct,),
               [(dyb,) + col, sl(1), sl(2), sl(3), (conv,) + col, (cw, (3, ct), lambda j: (0, j))],
               [((S, W), BF) + col, ((S, W), BF) + col, ((S, W), BF) + col,
                ((3, W), F32, (3, ct), lambda j: (0, j)), ((1, W), F32, (1, ct), lambda j: (0, j))], fn)


def _plain(accs, sides):
    return accs


def kernel(x, ffn1_norm, ffn1_w_gate, ffn1_w_up, ffn1_w_down, mix_norm, w_in, ssm_lambda_re, ssm_lambda_im, ssm_log_dt, ssm_b_re, ssm_b_im, ssm_c_re, ssm_c_im, ssm_d, ssm_w_glu, ssm_b_glu, ssm_w_out, conv_w, conv_b, conv_w_out, w_o, ffn2_norm, ffn2_w_gate, ffn2_w_up, ffn2_w_down, final_norm, loss_target, m_ffn1_norm, m_ffn1_w_gate, m_ffn1_w_up, m_ffn1_w_down, m_mix_norm, m_w_in, m_ssm_lambda_re, m_ssm_lambda_im, m_ssm_log_dt, m_ssm_b_re, m_ssm_b_im, m_ssm_c_re, m_ssm_c_im, m_ssm_d, m_ssm_w_glu, m_ssm_b_glu, m_ssm_w_out, m_conv_w, m_conv_b, m_conv_w_out, m_w_o, m_ffn2_norm, m_ffn2_w_gate, m_ffn2_w_up, m_ffn2_w_down, m_final_norm, v_ffn1_norm, v_ffn1_w_gate, v_ffn1_w_up, v_ffn1_w_down, v_mix_norm, v_w_in, v_ssm_lambda_re, v_ssm_lambda_im, v_ssm_log_dt, v_ssm_b_re, v_ssm_b_im, v_ssm_c_re, v_ssm_c_im, v_ssm_d, v_ssm_w_glu, v_ssm_b_glu, v_ssm_w_out, v_conv_w, v_conv_b, v_conv_w_out, v_w_o, v_ffn2_norm, v_ffn2_w_gate, v_ffn2_w_up, v_ffn2_w_down, v_final_norm):
    P = dict(ffn1_norm=ffn1_norm, ffn1_w_gate=ffn1_w_gate, ffn1_w_up=ffn1_w_up, ffn1_w_down=ffn1_w_down, mix_norm=mix_norm, w_in=w_in, ssm_lambda_re=ssm_lambda_re, ssm_lambda_im=ssm_lambda_im, ssm_log_dt=ssm_log_dt, ssm_b_re=ssm_b_re, ssm_b_im=ssm_b_im, ssm_c_re=ssm_c_re, ssm_c_im=ssm_c_im, ssm_d=ssm_d, ssm_w_glu=ssm_w_glu, ssm_b_glu=ssm_b_glu, ssm_w_out=ssm_w_out, conv_w=conv_w, conv_b=conv_b, conv_w_out=conv_w_out, w_o=w_o, ffn2_norm=ffn2_norm, ffn2_w_gate=ffn2_w_gate, ffn2_w_up=ffn2_w_up, ffn2_w_down=ffn2_w_down, final_norm=final_norm)
    M = dict(ffn1_norm=m_ffn1_norm, ffn1_w_gate=m_ffn1_w_gate, ffn1_w_up=m_ffn1_w_up, ffn1_w_down=m_ffn1_w_down, mix_norm=m_mix_norm, w_in=m_w_in, ssm_lambda_re=m_ssm_lambda_re, ssm_lambda_im=m_ssm_lambda_im, ssm_log_dt=m_ssm_log_dt, ssm_b_re=m_ssm_b_re, ssm_b_im=m_ssm_b_im, ssm_c_re=m_ssm_c_re, ssm_c_im=m_ssm_c_im, ssm_d=m_ssm_d, ssm_w_glu=m_ssm_w_glu, ssm_b_glu=m_ssm_b_glu, ssm_w_out=m_ssm_w_out, conv_w=m_conv_w, conv_b=m_conv_b, conv_w_out=m_conv_w_out, w_o=m_w_o, ffn2_norm=m_ffn2_norm, ffn2_w_gate=m_ffn2_w_gate, ffn2_w_up=m_ffn2_w_up, ffn2_w_down=m_ffn2_w_down, final_norm=m_final_norm)
    V = dict(ffn1_norm=v_ffn1_norm, ffn1_w_gate=v_ffn1_w_gate, ffn1_w_up=v_ffn1_w_up, ffn1_w_down=v_ffn1_w_down, mix_norm=v_mix_norm, w_in=v_w_in, ssm_lambda_re=v_ssm_lambda_re, ssm_lambda_im=v_ssm_lambda_im, ssm_log_dt=v_ssm_log_dt, ssm_b_re=v_ssm_b_re, ssm_b_im=v_ssm_b_im, ssm_c_re=v_ssm_c_re, ssm_c_im=v_ssm_c_im, ssm_d=v_ssm_d, ssm_w_glu=v_ssm_w_glu, ssm_b_glu=v_ssm_b_glu, ssm_w_out=v_ssm_w_out, conv_w=v_conv_w, conv_b=v_conv_b, conv_w_out=v_conv_w_out, w_o=v_w_o, ffn2_norm=v_ffn2_norm, ffn2_w_gate=v_ffn2_w_gate, ffn2_w_up=v_ffn2_w_up, ffn2_w_down=v_ffn2_w_down, final_norm=v_final_norm)
    names = list(P)
    sharded = ["ffn1_w_gate", "ffn1_w_up", "ffn1_w_down", "w_in", "ssm_w_glu", "ssm_w_out", "conv_w_out", "w_o",
               "ffn2_w_gate", "ffn2_w_up", "ffn2_w_down"]
    replicated = [n for n in names if n not in sharded and n != "conv_w"]

    S, D = x.shape[1], x.shape[2]
    W = ssm_d.shape[0]
    Dc = D // NDEV
    G, N = ssm_lambda_re.shape
    GN = G * N
    rows = GN // LANE
    xh = x.reshape(S, D)
    target = loss_target.reshape(S, D)
    xi, yi, ci = lax.axis_index("x"), lax.axis_index("y"), lax.axis_index("c")
    c_arr = jnp.reshape(ci, (1,)).astype(I32)
    q_arr = jnp.reshape(2 * xi + yi, (1,)).astype(I32)
    row = lambda v: v.reshape(1, -1)

    transposed = ("ffn1_w_gate", "ffn1_w_up", "ffn2_w_gate", "ffn2_w_up")
    local = lambda table, n: table[n].T if n in transposed else table[n]

    groups = [sharded[0:2], sharded[2:3], sharded[3:8] + ["conv_w"], sharded[8:11]]
    started = {}

    def gather_start(gi, after):
        srcs = [conv_w if n == "conv_w" else local(P, n).astype(BF) for n in groups[gi]]
        lands = [lax.empty((NDEV,) + s.shape, s.dtype) for s in srcs]
        started[gi] = _split_start("gather_start_%d" % gi, srcs, lands, _gather_copies, 4, after)
        return started[gi][4]

    def gathered(gi, after):
        srcs, lands = _split_wait("gather_wait_%d" % gi, started[gi], _gather_copies, after)
        return _gather_forward("gather_forward_%d" % gi, srcs, lands)

    tm = _pick(S, 512, SUBLANE)
    tk = _pick(D, K_TILE, LANE)
    ts = _pick(S, K_TILE, SUBLANE)
    tn = _pick(D, 1024, LANE)

    token0 = gather_start(0, [])
    u1, r1 = _rms_fwd("rms1", xh, row(ffn1_norm), deps=[token0])
    wg1, wu1 = gathered(0, u1)
    tokens = [gather_start(1, [wg1]), gather_start(2, [wg1])]
    h1, ffn1_saved, wd1 = _ffn_fwd("ffn1", u1, xh, wg1, wu1, lambda after: gathered(1, after)[0], deps=tokens)
    u2, r2 = _rms_fwd("rms2", h1, row(mix_norm))
    w_in_f, w_glu_f, w_so, w_co, w_o_f, cw_f = gathered(2, u2)
    token3 = gather_start(3, [w_in_f])
    w_glu_f = w_glu_f.reshape(W, W)
    w_o_f = w_o_f.reshape(D, D)
    cw = jnp.transpose(cw_f, (1, 0, 2)).reshape(3, W)
    (proj,) = _mm(
        "in_proj", (NDEV, S // tm, 1, D // tk),
        [(u2, (tm, tk), lambda b, i, j, k: (i, k)), (w_in_f, (None, tk, W), lambda b, i, j, k: (b, k, 0))],
        [(0, 1, NN, 0)], [], [((NDEV, S, W), F32, (None, tm, W), lambda b, i, j, k: (b, i, 0))], _plain, [(tm, W)],
        deps=[token3])

    ssm_in = (ssm_lambda_re, ssm_lambda_im, ssm_log_dt, ssm_b_re, ssm_b_im, ssm_c_re, ssm_c_im)
    (a_re, a_im, bd_re, bd_im, cd_re, cd_imn), ssm_vjp = jax.vjp(_ssm_params, *ssm_in)
    tiles, tch, tst = bd_re.shape
    bd_re_b, bd_im_b, cd_re_b, cd_imn_b = (t.astype(BF) for t in (bd_re, bd_im, cd_re, cd_imn))
    v_bf = proj[0].astype(BF)
    st_blk = lambda arr: (arr, (tm, tst), lambda b, i, j, k: (i, b))
    ch_blk = lambda arr: (arr, (tm, tch), lambda b, i, j, k: (i, b))
    bd_blk = lambda arr: (arr, (None, tch, tst), lambda b, i, j, k: (b, 0, 0))
    cd_blk = lambda arr: (arr, (None, tst, tch), lambda b, i, j, k: (b, 0, 0))
    st_out = lambda dt: ((S, GN), dt, (tm, tst), lambda b, i, j, k: (i, b))
    ch_out = lambda dt: ((S, W), dt, (tm, tch), lambda b, i, j, k: (i, b))
    d_blk = (row(ssm_d), (1, tch), lambda b, i, j, k: (0, b))
    v_blk = (proj, (None, tm, tch), lambda b, i, j, k: (0, i, b))

    bu_re, bu_im = _mm("ssm_bu", (tiles, S // tm, 1, 1), [ch_blk(v_bf), bd_blk(bd_re_b), bd_blk(bd_im_b)],
                       [(0, 1, NN, 0), (0, 2, NN, 1)], [], [st_out(F32), st_out(F32)], _plain, [(tm, tst)] * 2)
    s_re3, s_im3 = _scan_fwd(bu_re.reshape(S, rows, LANE), bu_im.reshape(S, rows, LANE), a_re, a_im)
    s_re_b = s_re3.reshape(S, GN).astype(BF)
    s_im_b = s_im3.reshape(S, GN).astype(BF)

    def y0_epilogue(accs, sides):
        y0 = accs[0] + sides[1] * sides[0]
        return y0, _gelu(y0)

    y0, y1 = _mm("ssm_y0", (tiles, S // tm, 1, 1),
                 [st_blk(s_re_b), st_blk(s_im_b), cd_blk(cd_re_b), cd_blk(cd_imn_b)],
                 [(0, 2, NN, 0), (1, 3, NN, 0)], [v_blk, d_blk], [ch_out(F32), ch_out(BF)], y0_epilogue, [(tm, tch)])

    tw = _pick(W, 512, LANE)

    def glu_epilogue(accs, sides):
        q = accs[0] + sides[1]
        return q, _gelu(sides[0]) * _sigmoid(q)

    q_pre, y2 = _mm("ssm_glu", (1, S // tm, W // tw, 1),
                    [(y1, (tm, W), lambda b, i, j, k: (i, 0)), (w_glu_f, (W, tw), lambda b, i, j, k: (0, j))],
                    [(0, 1, NN, 0)],
                    [(y0, (tm, tw), lambda b, i, j, k: (i, j)), (row(ssm_b_glu), (1, tw), lambda b, i, j, k: (0, j))],
                    [((S, W), F32, (tm, tw), lambda b, i, j, k: (i, j)), ((S, W), BF, (tm, tw), lambda b, i, j, k: (i, j))],
                    glu_epilogue, [(tm, tw)])

    yb, conv = _conv_fwd(proj, cw, row(conv_b))

    per = W // Dc
    ga_blk = (proj, (None, tm, Dc), lambda b, i, j, k: (4 + b // per, i, b % per))
    gb_blk = (proj, (None, tm, Dc), lambda b, i, j, k: (6 + b // per, i, b % per))
    dc_out = ((S, D), BF, (tm, Dc), lambda b, i, j, k: (i, b))

    def merge_epilogue(accs, sides):
        za, zb = accs
        return _sigmoid(sides[0]) * za + _sigmoid(sides[1]) * zb, za, zb

    merged, z_a, z_b = _mm(
        "mix_merge", (NDEV, S // tm, 1, 1),
        [(y2, (tm, W), lambda b, i, j, k: (i, 0)), (yb, (tm, W), lambda b, i, j, k: (i, 0)),
         (w_so, (None, W, Dc), lambda b, i, j, k: (b, 0, 0)), (w_co, (None, W, Dc), lambda b, i, j, k: (b, 0, 0))],
        [(0, 2, NN, 0), (1, 3, NN, 1)], [ga_blk, gb_blk], [dc_out, dc_out, dc_out], merge_epilogue, [(tm, Dc)] * 2)

    (h2,) = _mm("mix_out", (1, S // tm, D // tn, D // tk),
                [(merged, (tm, tk), lambda b, i, j, k: (i, k)), (w_o_f, (tk, tn), lambda b, i, j, k: (k, j))],
                [(0, 1, NN, 0)], [(h1, (tm, tn), lambda b, i, j, k: (i, j))],
                [((S, D), F32, (tm, tn), lambda b, i, j, k: (i, j))],
                lambda accs, sides: [sides[0] + accs[0]], [(tm, tn)])

    u3, r3 = _rms_fwd("rms3", h2, row(ffn2_norm))
    wg2, wu2, wd2 = gathered(3, u3)
    h3, ffn2_saved, _ = _ffn_fwd("ffn2", u3, h2, wg2, wu2, wd2)
    loss_vec, dh3, dh3_half, d_final_norm = _loss_head("loss_head", h3, row(final_norm), target)
    loss = lax.psum(loss_vec[0, 0], ("x", "y", "c"))

    grads, deltas, new_m, new_v = {}, {}, {}, {}

    def rs_begin(tag, parts):
        lands = _exchange_sibling("rs_sibling_" + tag, parts)
        sums = [_sum_sibling("rs_sum_%s_%d" % (tag, a), p, land, c_arr) for a, (p, land) in enumerate(zip(parts, lands))]
        lands2 = [lax.empty((3,) + sm.shape[1:], sm.dtype) for sm in sums]
        return _split_start("rs_chips_start_" + tag, sums, lands2, _chips_copies, 3)

    def rs_end(tag, group, begun, after):
        sums, lands2 = _split_wait("rs_chips_wait_" + tag, begun, _chips_copies, after)
        for n, sm, land2 in zip(group, sums, lands2):
            res = _finish_sharded("adamw_" + n, sm, land2, q_arr, local(P, n), local(M, n), local(V, n))
            grads[n], deltas[n], new_m[n], new_v[n] = [t.T if n in transposed else t for t in res]

    du3_fn, dwg2, dwu2, dwd2 = _ffn_bwd("ffn2", dh3_half, u3, ffn2_saved, wg2, wu2, wd2)
    rs_ffn2 = rs_begin("ffn2", [dwg2, dwu2, dwd2])
    du3 = du3_fn(deps=[rs_ffn2[4]])
    dh2, dh2_b, d_ffn2_norm = _rms_bwd("rms3_bwd", du3, h2, r3, row(ffn2_norm), dh3, 1.0)

    dg_out = ((2, S, W), BF, (None, tm, Dc), lambda b, i, j, k: (j // per, i, j % per))
    ga_blk2 = (proj, (None, tm, Dc), lambda b, i, j, k: (4 + j // per, i, j % per))
    gb_blk2 = (proj, (None, tm, Dc), lambda b, i, j, k: (6 + j // per, i, j % per))
    dcj = lambda arr: (arr, (tm, Dc), lambda b, i, j, k: (i, j))
    dcj_out = ((S, D), BF, (tm, Dc), lambda b, i, j, k: (i, j))

    def dmerge_epilogue(accs, sides):
        dm = accs[0]
        sa, sb = _sigmoid(sides[0]), _sigmoid(sides[1])
        za, zb = sides[2].astype(F32), sides[3].astype(F32)
        return dm * sa, dm * sb, dm * za * sa * (1.0 - sa), dm * zb * sb * (1.0 - sb)

    dz_a, dz_b, dga, dgb = _mm(
        "mix_out_dx", (1, S // tm, NDEV, D // tk),
        [(dh2_b, (tm, tk), lambda b, i, j, k: (i, k)), (w_o_f, (Dc, tk), lambda b, i, j, k: (j, k))],
        [(0, 1, NT, 0)], [ga_blk2, gb_blk2, dcj(z_a), dcj(z_b)], [dcj_out, dcj_out, dg_out, dg_out],
        dmerge_epilogue, [(tm, Dc)])

    td = _pick(D, 512, LANE)
    (dw_o,) = _mm("mix_out_dw", (1, D // td, D // tn, S // ts),
                  [(merged, (ts, td), lambda b, i, j, k: (k, i)), (dh2_b, (ts, tn), lambda b, i, j, k: (k, j))],
                  [(0, 1, TN, 0)], [], [((D, D), BF, (td, tn), lambda b, i, j, k: (i, j))], _plain, [(td, tn)])

    wout = ((NDEV, W, Dc), BF, (None, W, Dc), lambda b, i, j, k: (b, 0, 0))
    dw_so, dw_co = _mm(
        "mix_merge_dw", (NDEV, 1, 1, S // ts),
        [(y2, (ts, W), lambda b, i, j, k: (k, 0)), (yb, (ts, W), lambda b, i, j, k: (k, 0)),
         (dz_a, (ts, Dc), lambda b, i, j, k: (k, b)), (dz_b, (ts, Dc), lambda b, i, j, k: (k, b))],
        [(0, 2, TN, 0), (1, 3, TN, 1)], [], [wout, wout], _plain, [(W, Dc)] * 2)

    def dglu_epilogue(accs, sides):
        dy2, dyb = accs
        sq = _sigmoid(sides[1])
        return dy2 * _gelu(sides[0]) * sq * (1.0 - sq), dy2 * sq, dyb

    full_w = lambda arr: (arr, (tm, W), lambda b, i, j, k: (i, 0))
    full_w_out = lambda dt: ((S, W), dt, (tm, W), lambda b, i, j, k: (i, 0))
    dq, dy1p, dyb = _mm(
        "mix_merge_dx", (1, S // tm, 1, NDEV),
        [(dz_a, (tm, Dc), lambda b, i, j, k: (i, k)), (dz_b, (tm, Dc), lambda b, i, j, k: (i, k)),
         (w_so, (None, W, Dc), lambda b, i, j, k: (k, 0, 0)), (w_co, (None, W, Dc), lambda b, i, j, k: (k, 0, 0))],
        [(0, 2, NT, 0), (1, 3, NT, 1)], [full_w(y0), full_w(q_pre)], [full_w_out(BF), full_w_out(F32), full_w_out(F32)],
        dglu_epilogue, [(tm, W)] * 2)

    def dy0_epilogue(accs, sides):
        dy0 = (sides[0] + accs[0]) * _gelu_grad(sides[1])
        return dy0, dy0

    wj = lambda arr: (arr, (tm, tw), lambda b, i, j, k: (i, j))
    dy0, dy0_b = _mm("ssm_glu_dx", (1, S // tm, W // tw, 1),
                     [(dq, (tm, W), lambda b, i, j, k: (i, 0)), (w_glu_f, (tw, W), lambda b, i, j, k: (j, 0))],
                     [(0, 1, NT, 0)], [wj(dy1p), wj(y0)],
                     [((S, W), F32, (tm, tw), lambda b, i, j, k: (i, j)), ((S, W), BF, (tm, tw), lambda b, i, j, k: (i, j))],
                     dy0_epilogue, [(tm, tw)])

    (dw_glu,) = _mm("ssm_glu_dw", (1, W // tw, 1, S // ts),
                    [(y1, (ts, tw), lambda b, i, j, k: (k, i)), (dq, (ts, W), lambda b, i, j, k: (k, 0))],
                    [(0, 1, TN, 0)], [], [((W, W), BF, (tw, W), lambda b, i, j, k: (i, 0))], _plain, [(tw, W)])

    tr = _pick(S, 256, SUBLANE)
    rw = ((tr, W), lambda i: (i, 0))
    vec_w = ((1, W), F32, (1, W), lambda i: (0, 0))
    d_b_glu, d_ssm_d = _ew(
        "ssm_colsums", (S // tr,), [(dq,) + rw, (dy0,) + rw, (proj, (None, tr, W), lambda i: (0, i, 0))],
        [vec_w, vec_w],
        lambda dqv, dyv, vv: (jnp.sum(dqv.astype(F32), axis=0, keepdims=True), jnp.sum(dyv * vv, axis=0, keepdims=True)),
        acc=(0, 1))

    ds_re, ds_im = _mm("ssm_ds", (tiles, S // tm, 1, 1), [ch_blk(dy0_b), cd_blk(cd_re_b), cd_blk(cd_imn_b)],
                       [(0, 1, NT, 0), (0, 2, NT, 1)], [], [st_out(F32), st_out(F32)], _plain, [(tm, tst)] * 2)
    lam_re3, lam_im3, da_re, da_im = _scan_bwd(ds_re.reshape(S, rows, LANE), ds_im.reshape(S, rows, LANE),
                                               s_re3, s_im3, a_re, a_im)
    lam_re_b = lam_re3.reshape(S, GN).astype(BF)
    lam_im_b = lam_im3.reshape(S, GN).astype(BF)

    (dv,) = _mm("ssm_dv", (tiles, S // tm, 1, 1),
                [st_blk(lam_re_b), st_blk(lam_im_b), bd_blk(bd_re_b), bd_blk(bd_im_b)],
                [(0, 2, NT, 0), (1, 3, NT, 0)], [ch_blk(dy0), d_blk], [ch_out(BF)],
                lambda accs, sides: [accs[0] + sides[0] * sides[1]], [(tm, tch)])

    tok_ch = lambda arr: (arr, (ts, tch), lambda b, i, j, k: (k, b))
    tok_st = lambda arr: (arr, (ts, tst), lambda b, i, j, k: (k, b))
    bd_out = ((tiles, tch, tst), F32, (None, tch, tst), lambda b, i, j, k: (b, 0, 0))
    cd_out = ((tiles, tst, tch), F32, (None, tst, tch), lambda b, i, j, k: (b, 0, 0))
    dbd_re, dbd_im = _mm("ssm_dbd", (tiles, 1, 1, S // ts), [tok_ch(v_bf), tok_st(lam_re_b), tok_st(lam_im_b)],
                         [(0, 1, TN, 0), (0, 2, TN, 1)], [], [bd_out, bd_out], _plain, [(tch, tst)] * 2)
    dcd_re, dcd_imn = _mm("ssm_dcd", (tiles, 1, 1, S // ts), [tok_st(s_re_b), tok_st(s_im_b), tok_ch(dy0_b)],
                          [(0, 2, TN, 0), (1, 2, TN, 1)], [], [cd_out, cd_out], _plain, [(tst, tch)] * 2)
    d_ssm = ssm_vjp((da_re, da_im, dbd_re, dbd_im, dcd_re, dcd_imn))

    dbg, dcg, dval, d_conv_w_full, d_conv_b = _conv_bwd(dyb, proj, conv, cw)
    dproj = jnp.concatenate([dv[None], dbg[None], dcg[None], dval[None], dga, dgb], axis=0)

    (dw_in,) = _mm("in_proj_dw", (NDEV, D // td, 1, S // ts),
                   [(u2, (ts, td), lambda b, i, j, k: (k, i)), (dproj, (None, ts, W), lambda b, i, j, k: (b, k, 0))],
                   [(0, 1, TN, 0)], [], [((NDEV, D, W), BF, (None, td, W), lambda b, i, j, k: (b, i, 0))],
                   _plain, [(td, W)])
    rs_mixer = rs_begin("mixer", [dw_in, dw_glu.reshape(NDEV, W // NDEV, W), dw_so, dw_co, dw_o.reshape(NDEV, Dc, D)])
    (du2,) = _mm("in_proj_dx", (1, S // tm, D // tn, NDEV),
                 [(dproj, (None, tm, W), lambda b, i, j, k: (k, i, 0)), (w_in_f, (None, tn, W), lambda b, i, j, k: (k, j, 0))],
                 [(0, 1, NT, 0)], [], [((S, D), F32, (tm, tn), lambda b, i, j, k: (i, j))], _plain, [(tm, tn)],
                 deps=[rs_mixer[4]])
    dh1, dh1_half, d_mix_norm = _rms_bwd("rms2_bwd", du2, h1, r2, row(mix_norm), dh2, 0.5)

    small = dict(mix_norm=d_mix_norm, ffn2_norm=d_ffn2_norm, final_norm=d_final_norm,
                 ssm_lambda_re=d_ssm[0], ssm_lambda_im=d_ssm[1], ssm_log_dt=d_ssm[2], ssm_b_re=d_ssm[3],
                 ssm_b_im=d_ssm[4], ssm_c_re=d_ssm[5], ssm_c_im=d_ssm[6], ssm_d=d_ssm_d, ssm_b_glu=d_b_glu,
                 conv_b=d_conv_b)
    replicated = [n for n in replicated if n != "ffn1_norm"] + ["ffn1_norm"]
    sizes = [P[n].size for n in replicated]
    early = sum(sizes[:-1]) + d_conv_w_full.size
    padded = -(-early // (SUBLANE * LANE)) * (SUBLANE * LANE)

    def pack(parts, last):
        flat = jnp.concatenate([p.reshape(-1).astype(F32) for p in parts])
        flat = jnp.pad(flat, (0, padded - flat.shape[0]))
        if last is not None:
            flat = jnp.concatenate([flat, last.reshape(-1)])
        return flat.reshape(-1, LANE)

    cw_zero = jnp.zeros_like(d_conv_w_full)
    me = 4 * xi + 2 * yi + ci
    early_pk = pack([small[n] for n in replicated[:-1]] + [d_conv_w_full], None)
    early_land = lax.dynamic_update_slice(jnp.zeros((NDEV,) + early_pk.shape, F32), early_pk[None], (me, 0, 0))
    small_begun = _split_start("gather_small_start", [early_pk], [early_land], _everyone_copies, NDEV - 1)
    rs_end("ffn2", sharded[8:11], rs_ffn2, dh1)

    du1_fn, dwg1, dwu1, dwd1 = _ffn_bwd("ffn1", dh1_half, u1, ffn1_saved, wg1, wu1, wd1)
    rs_ffn1 = rs_begin("ffn1", [dwg1, dwu1, dwd1])
    du1 = du1_fn(deps=[rs_ffn1[4]])
    dx, _, d_ffn1_norm = _rms_bwd("rms1_bwd", du1, xh, r1, row(ffn1_norm), dh1, 1.0)
    (late_all,) = _all_gather("gather_ffn1_norm_grad", [d_ffn1_norm.reshape(-1, LANE)])
    rs_end("mixer", sharded[3:8], rs_mixer, dx)
    _, (early_all,) = _split_wait("gather_small_wait", small_begun, _everyone_copies, late_all)
    small_all = jnp.concatenate([early_all, late_all], axis=1)
    g_pk, d_pk, m_pk, v_pk = _finish_replicated(
        "adamw_replicated", small_all, pack([P[n] for n in replicated[:-1]] + [cw_zero], P["ffn1_norm"]),
        pack([M[n] for n in replicated[:-1]] + [cw_zero], M["ffn1_norm"]),
        pack([V[n] for n in replicated[:-1]] + [cw_zero + 1.0], V["ffn1_norm"]))
    off = 0
    for n, sz in zip(replicated, sizes):
        at = padded if n == "ffn1_norm" else off
        for store, pk in ((grads, g_pk), (deltas, d_pk), (new_m, m_pk), (new_v, v_pk)):
            store[n] = pk.reshape(-1)[at:at + sz].reshape(P[n].shape)
        off += sz
    off -= sizes[-1]
    g_cw_full = g_pk.reshape(-1)[off:off + d_conv_w_full.size].reshape(d_conv_w_full.shape)
    cwl = conv_w.shape[1]
    g_cw = lax.dynamic_slice_in_dim(g_cw_full, me * cwl, cwl, axis=1)
    full3 = ((3, cwl), lambda i: (0, 0))
    grads["conv_w"], deltas["conv_w"], new_m["conv_w"], new_v["conv_w"] = _ew(
        "adamw_conv_w", (1,), [(g_cw,) + full3, (conv_w,) + full3, (m_conv_w,) + full3, (v_conv_w,) + full3],
        [((3, cwl), F32) + full3] * 4, lambda g, w, m, v: (g,) + _adamw(w, g, m, v))
    rs_end("ffn1", sharded[0:3], rs_ffn1, g_pk)

    return (loss, dx.reshape(x.shape), *[grads[n] for n in names], *[deltas[n] for n in names],
            *[new_m[n] for n in names], *[new_v[n] for n in names])
```

```python
import math

import jax
import jax.numpy as jnp
from jax import lax
from jax.experimental import pallas as pl
from jax.experimental.pallas import tpu as pltpu

F32 = jnp.float32
BF = jnp.bfloat16
I32 = jnp.int32
MESH = pl.DeviceIdType.MESH
LANE = 128
SUBLANE = 8
NDEV = 8
K_TILE = 2048
DMA_CHUNKS = 4
EPS = 1e-6
ADAM_LR, ADAM_B1, ADAM_B2, ADAM_EPS, ADAM_WD, ADAM_STEP = 0.001, 0.9, 0.999, 1e-08, 0.01, 10
NN = ((1,), (0,))
NT = ((1,), (1,))
TN = ((0,), (0,))
HBM = pl.BlockSpec(memory_space=pltpu.HBM)


def _pick(n, pref, mult):
    t = min(pref, n)
    t -= t % mult
    while t >= mult:
        if n % t == 0:
            return t
        t -= mult
    return n


def _sigmoid(x):
    return 1.0 / (1.0 + jnp.exp(-x))


_GELU_C = math.sqrt(2.0 / math.pi)


def _gelu(x):
    return 0.5 * x * (1.0 + jnp.tanh(_GELU_C * (x + 0.044715 * x * x * x)))


def _gelu_grad(x):
    t = jnp.tanh(_GELU_C * (x + 0.044715 * x * x * x))
    return 0.5 * (1.0 + t) + 0.5 * x * (1.0 - t * t) * _GELU_C * (1.0 + 3.0 * 0.044715 * x * x)


def _dep_specs(deps, rank):
    return [(d, d.shape, lambda *_, nd=d.ndim: (0,) * nd) for d in deps]


def _mm(name, grid, ops, pairs, sides, outs, epilogue, acc_shapes, deps=()):
    nk = grid[-1]
    n_ops, n_sides, n_outs = len(ops), len(sides), len(outs)
    dep_specs = _dep_specs(deps, len(grid))
    n_deps = len(dep_specs)

    def body(*refs):
        op_refs = refs[:n_ops]
        side_refs = refs[n_ops:n_ops + n_sides]
        out_refs = refs[n_ops + n_sides + n_deps:n_ops + n_sides + n_deps + n_outs]
        acc_refs = refs[n_ops + n_sides + n_deps + n_outs:]

        def partials():
            res = [None] * len(acc_shapes)
            for ia, ib, dims, ai in pairs:
                p = lax.dot_general(op_refs[ia][...], op_refs[ib][...], (dims, ((), ())),
                                    preferred_element_type=F32)
                res[ai] = p if res[ai] is None else res[ai] + p
            return res

        def finish(accs):
            vals = epilogue(accs, [s[...] for s in side_refs])
            for o, v in zip(out_refs, vals):
                o[...] = v.astype(o.dtype)

        if nk == 1:
            finish(partials())
        else:
            k = pl.program_id(len(grid) - 1)

            @pl.when(k == 0)
            def _():
                for a, p in zip(acc_refs, partials()):
                    a[...] = p

            @pl.when(k > 0)
            def _():
                for a, p in zip(acc_refs, partials()):
                    a[...] += p

            @pl.when(k == nk - 1)
            def _():
                finish([a[...] for a in acc_refs])

    return pl.pallas_call(
        body, name=name, grid=grid,
        in_specs=[pl.BlockSpec(b, m) for (_, b, m) in list(ops) + list(sides) + dep_specs],
        out_specs=[pl.BlockSpec(b, m) for (_, _, b, m) in outs],
        out_shape=[jax.ShapeDtypeStruct(s, d) for (s, d, _, _) in outs],
        scratch_shapes=[pltpu.VMEM(s, F32) for s in acc_shapes] if nk > 1 else [],
        compiler_params=pltpu.CompilerParams(
            dimension_semantics=("parallel",) * (len(grid) - 1) + ("arbitrary",)),
    )(*[a for (a, _, _) in list(ops) + list(sides) + dep_specs])


def _ew(name, grid, ins, outs, fn, acc=(), deps=()):
    n_in = len(ins)
    dep_specs = _dep_specs(deps, len(grid))

    def body(*refs):
        vals = fn(*[r[...] for r in refs[:n_in]])
        first = pl.program_id(0) == 0
        for idx, (o, v) in enumerate(zip(refs[n_in + len(dep_specs):], vals)):
            if idx in acc:
                @pl.when(first)
                def _(o=o, v=v):
                    o[...] = v.astype(o.dtype)

                @pl.when(jnp.logical_not(first))
                def _(o=o, v=v):
                    o[...] += v.astype(o.dtype)
            else:
                o[...] = v.astype(o.dtype)

    return pl.pallas_call(
        body, name=name, grid=grid,
        in_specs=[pl.BlockSpec(b, m) for (_, b, m) in list(ins) + dep_specs],
        out_specs=[pl.BlockSpec(b, m) for (_, _, b, m) in outs],
        out_shape=[jax.ShapeDtypeStruct(s, d) for (s, d, _, _) in outs],
        compiler_params=pltpu.CompilerParams(
            dimension_semantics=(("arbitrary",) if acc else ("parallel",)) * len(grid)),
    )(*[a for (a, _, _) in list(ins) + dep_specs])


def _position():
    x, y, c = lax.axis_index("x"), lax.axis_index("y"), lax.axis_index("c")
    chips = [(1 - x, y), (x, 1 - y), (1 - x, 1 - y)]
    return x, y, c, chips


def _all_gather(name, shards):
    n = len(shards)

    def body(*refs):
        xs, outs = refs[:n], refs[n:2 * n]
        send_sems, recv_sems, local_sems = refs[2 * n:]
        x, y, c, chips = _position()
        me, sibling = (x, y, c), (x, y, 1 - c)

        def copy(a, k, block, to, src=None):
            dst = outs[a].at[4 * block[0] + 2 * block[1] + block[2]]
            return pltpu.make_async_remote_copy(
                src_ref=dst if src is None else src, dst_ref=dst,
                send_sem=send_sems.at[a, k], recv_sem=recv_sems.at[a, k],
                device_id=to, device_id_type=MESH)

        mine = [pltpu.make_async_copy(xs[a], outs[a].at[4 * x + 2 * y + c], local_sems.at[a]) for a in range(n)]
        for cp in mine:
            cp.start()
        first = []
        for a in range(n):
            first.append(copy(a, 0, me, sibling, src=xs[a]))
            first += [copy(a, 1 + j, me, (*chip, c), src=xs[a]) for j, chip in enumerate(chips)]
        for cp in first:
            cp.start()
        passed = []
        for a in range(n):
            for j, chip in enumerate(chips):
                copy(a, 1 + j, (*chip, c), me).wait_recv()
                cp = copy(a, 4 + j, (*chip, c), sibling)
                cp.start()
                passed.append(cp)
        for a in range(n):
            copy(a, 0, sibling, me).wait_recv()
            for j, chip in enumerate(chips):
                copy(a, 4 + j, (*chip, 1 - c), me).wait_recv()
        for cp in first + passed:
            cp.wait_send()
        for cp in mine:
            cp.wait()

    return pl.pallas_call(
        body, name=name,
        out_shape=[jax.ShapeDtypeStruct((NDEV,) + s.shape, s.dtype) for s in shards],
        in_specs=[HBM] * n, out_specs=[HBM] * n,
        scratch_shapes=[pltpu.SemaphoreType.DMA((n, 7)), pltpu.SemaphoreType.DMA((n, 7)),
                        pltpu.SemaphoreType.DMA((n,))],
    )(*shards)


SEM = pl.BlockSpec(memory_space=pltpu.SEMAPHORE)
EFFECT = pltpu.SideEffectType.DATAFLOW_SIDE_EFFECTING


def _in_hbm(v):
    return pltpu.with_memory_space_constraint(v, pltpu.HBM)


def _split_start(name, srcs, lands, make_copies, n_per, after=()):
    n = len(srcs)
    after = list(after)

    def body(*refs):
        send_sems, recv_sems = refs[2 * n + len(after)], refs[2 * n + len(after) + 1]
        for cp in make_copies(refs[:n], refs[n:2 * n], send_sems, recv_sems):
            cp.start()
        refs[-1][...] = jnp.zeros_like(refs[-1])

    outs = pl.pallas_call(
        body, name=name,
        out_shape=(pltpu.SemaphoreType.DMA((n * n_per,)), pltpu.SemaphoreType.DMA((n * n_per,)),
                   *[pltpu.HBM(v.shape, v.dtype) for v in list(srcs) + list(lands)],
                   jax.ShapeDtypeStruct((SUBLANE, LANE), F32)),
        in_specs=[HBM] * (2 * n) + [pl.BlockSpec(memory_space=pl.ANY)] * len(after),
        out_specs=(SEM, SEM, *[HBM] * (2 * n), pl.BlockSpec(memory_space=pltpu.VMEM)),
        input_output_aliases={i: 2 + i for i in range(2 * n)},
        compiler_params=pltpu.CompilerParams(has_side_effects=EFFECT),
    )(*[_in_hbm(v) for v in list(srcs) + list(lands)], *after)
    return outs[0], outs[1], list(outs[2:2 + n]), list(outs[2 + n:2 + 2 * n]), outs[-1]


def _split_wait(name, started, make_copies, after):
    send_sems, recv_sems, srcs, lands, _ = started
    n = len(srcs)

    def body(*refs):
        for cp in make_copies(refs[:n], refs[n:2 * n], refs[2 * n], refs[2 * n + 1]):
            cp.wait_send()
            cp.wait_recv()

    order = [] if after is None else [after]
    outs = pl.pallas_call(
        body, name=name,
        out_shape=tuple(pltpu.HBM(v.shape, v.dtype) for v in srcs + lands),
        in_specs=[HBM] * (2 * n) + [SEM, SEM] + [pl.BlockSpec(memory_space=pl.ANY)] * len(order),
        out_specs=tuple([HBM] * (2 * n)),
        input_output_aliases={i: i for i in range(2 * n)},
        compiler_params=pltpu.CompilerParams(has_side_effects=EFFECT),
    )(*srcs, *lands, send_sems, recv_sems, *order)
    return list(outs[:n]), list(outs[n:])


def _gather_copies(xs, lands, send_sems, recv_sems):
    x, y, c, chips = _position()
    copies = []
    for a in range(len(xs)):
        for k, peer in enumerate([(x, y, 1 - c)] + [(*chip, c) for chip in chips]):
            copies.append(pltpu.make_async_remote_copy(
                src_ref=xs[a], dst_ref=lands[a].at[4 * x + 2 * y + c],
                send_sem=send_sems.at[4 * a + k], recv_sem=recv_sems.at[4 * a + k], device_id=peer, device_id_type=MESH))
    return copies


def _chips_copies(ps, lands, send_sems, recv_sems):
    x, y, c, chips = _position()
    copies = []
    for a in range(len(ps)):
        for j, chip in enumerate(chips):
            copies.append(pltpu.make_async_remote_copy(
                src_ref=ps[a].at[2 * chip[0] + chip[1]], dst_ref=lands[a].at[j],
                send_sem=send_sems.at[3 * a + j], recv_sem=recv_sems.at[3 * a + j], device_id=(*chip, c),
                device_id_type=MESH))
    return copies


def _everyone_copies(xs, lands, send_sems, recv_sems):
    x, y, c, _ = _position()
    flip = lambda v, bit: 1 - v if bit else v
    copies = []
    for a in range(len(xs)):
        for k in range(1, NDEV):
            copies.append(pltpu.make_async_remote_copy(
                src_ref=xs[a], dst_ref=lands[a].at[4 * x + 2 * y + c],
                send_sem=send_sems.at[7 * a + k - 1], recv_sem=recv_sems.at[7 * a + k - 1],
                device_id=(flip(x, k & 4), flip(y, k & 2), flip(c, k & 1)), device_id_type=MESH))
    return copies


def _row_chunks(rows, dtype):
    unit = SUBLANE * (4 // jnp.dtype(dtype).itemsize)
    units = rows // unit
    if rows % unit or units < 2:
        return [(0, rows)]
    k = min(DMA_CHUNKS, units)
    sizes = [(units // k + (1 if i < units % k else 0)) * unit for i in range(k)]
    return [(sum(sizes[:i]), sz) for i, sz in enumerate(sizes)]


def _gather_forward(name, lands):
    n = len(lands)

    def body(*refs):
        ins, outs = refs[:n], refs[n:2 * n]
        send_sems, recv_sems = refs[2 * n:]
        x, y, c, chips = _position()
        whole, chunks = [], []
        for a in range(n):
            rows = _row_chunks(ins[a].shape[1], ins[a].dtype)
            for j, chip in enumerate(chips):
                slot = 4 * chip[0] + 2 * chip[1]

                def to_sibling(src, dst):
                    return pltpu.make_async_remote_copy(
                        src_ref=src, dst_ref=dst, send_sem=send_sems.at[a, j], recv_sem=recv_sems.at[a, j],
                        device_id=(x, y, 1 - c), device_id_type=MESH)

                whole.append(to_sibling(ins[a].at[slot + c], outs[a].at[slot + 1 - c]))
                chunks += [to_sibling(ins[a].at[slot + c, pl.ds(r0, nr)], outs[a].at[slot + c, pl.ds(r0, nr)])
                           for r0, nr in rows]
        for cp in chunks:
            cp.start()
        for cp in whole:
            cp.wait()

    return pl.pallas_call(
        body, name=name,
        out_shape=[jax.ShapeDtypeStruct(l.shape, l.dtype) for l in lands],
        in_specs=[HBM] * n, out_specs=[HBM] * n,
        input_output_aliases={a: a for a in range(n)},
        scratch_shapes=[pltpu.SemaphoreType.DMA((n, 3)), pltpu.SemaphoreType.DMA((n, 3))],
    )(*lands)


def _exchange_sibling(name, grads):
    n = len(grads)

    def body(*refs):
        gs, lands = refs[:n], refs[n:2 * n]
        send_sems, recv_sems = refs[2 * n:]
        x, y, c, _ = _position()
        copies = []
        for a in range(n):
            for q in range(4):
                copies.append(pltpu.make_async_remote_copy(
                    src_ref=gs[a].at[2 * q + 1 - c], dst_ref=lands[a].at[q],
                    send_sem=send_sems.at[a, q], recv_sem=recv_sems.at[a, q],
                    device_id=(x, y, 1 - c), device_id_type=MESH))
        for cp in copies:
            cp.start()
        for cp in copies:
            cp.wait()

    return pl.pallas_call(
        body, name=name,
        out_shape=[jax.ShapeDtypeStruct((4,) + g.shape[1:], g.dtype) for g in grads],
        in_specs=[HBM] * n, out_specs=[HBM] * n,
        scratch_shapes=[pltpu.SemaphoreType.DMA((n, 4)), pltpu.SemaphoreType.DMA((n, 4))],
    )(*grads)


def _sum_sibling(name, g, land, c_arr):
    _, R, C = g.shape
    tr = _pick(R, 512, SUBLANE)

    def body(c_ref, g_ref, l_ref, o_ref):
        o_ref[...] = (g_ref[...].astype(F32) + l_ref[...].astype(F32)).astype(o_ref.dtype)

    return pl.pallas_call(
        body, name=name,
        grid_spec=pltpu.PrefetchScalarGridSpec(
            num_scalar_prefetch=1, grid=(4, R // tr),
            in_specs=[pl.BlockSpec((None, tr, C), lambda q, i, cr: (2 * q + cr[0], i, 0)),
                      pl.BlockSpec((None, tr, C), lambda q, i, cr: (q, i, 0))],
            out_specs=pl.BlockSpec((None, tr, C), lambda q, i, cr: (q, i, 0))),
        out_shape=jax.ShapeDtypeStruct((4, R, C), g.dtype),
        compiler_params=pltpu.CompilerParams(dimension_semantics=("parallel", "parallel")),
    )(c_arr, g, land)


def _adamw(w, g, m, v):
    m = ADAM_B1 * m + (1.0 - ADAM_B1) * g
    v = ADAM_B2 * v + (1.0 - ADAM_B2) * (g * g)
    m_hat = m / (1.0 - ADAM_B1 ** ADAM_STEP)
    v_hat = v / (1.0 - ADAM_B2 ** ADAM_STEP)
    delta = -ADAM_LR * (m_hat / (jnp.sqrt(v_hat) + ADAM_EPS) + ADAM_WD * w)
    return delta, m, v


def _finish_sharded(name, sums, land, q_arr, w, m, v):
    R, C = w.shape
    tr = _pick(R, 256, SUBLANE)

    def body(q_ref, p_ref, l_ref, w_ref, m_ref, v_ref, g_out, d_out, m_out, v_out):
        g = p_ref[...].astype(F32)
        for j in range(3):
            g = g + l_ref[j].astype(F32)
        d, mn, vn = _adamw(w_ref[...], g, m_ref[...], v_ref[...])
        g_out[...] = g
        d_out[...] = d
        m_out[...] = mn
        v_out[...] = vn

    row = pl.BlockSpec((tr, C), lambda i, qr: (i, 0))
    return pl.pallas_call(
        body, name=name,
        grid_spec=pltpu.PrefetchScalarGridSpec(
            num_scalar_prefetch=1, grid=(R // tr,),
            in_specs=[pl.BlockSpec((None, tr, C), lambda i, qr: (qr[0], i, 0)),
                      pl.BlockSpec((3, tr, C), lambda i, qr: (0, i, 0)), row, row, row],
            out_specs=[row] * 4),
        out_shape=[jax.ShapeDtypeStruct((R, C), F32)] * 4,
        compiler_params=pltpu.CompilerParams(dimension_semantics=("parallel",)),
    )(q_arr, sums, land, w, m, v)


def _finish_replicated(name, gathered, w, m, v):
    _, R, C = gathered.shape
    tr = _pick(R, 256, SUBLANE)

    def fn(gv, wv, mv, vv):
        g = gv[0]
        for d in range(1, NDEV):
            g = g + gv[d]
        dl, mn, vn = _adamw(wv, g, mv, vv)
        return g, dl, mn, vn

    row = ((tr, C), lambda i: (i, 0))
    return _ew(name, (R // tr,),
               [(gathered, (NDEV, tr, C), lambda i: (0, i, 0)), (w,) + row, (m,) + row, (v,) + row],
               [((R, C), F32) + row] * 4, fn)


def _rms_fwd(name, h, g, deps=()):
    S, D = h.shape
    tr = _pick(S, 256, SUBLANE)

    def fn(hv, gv):
        r = lax.rsqrt(jnp.mean(hv * hv, axis=-1, keepdims=True) + EPS)
        return hv * r * gv, r

    return _ew(name, (S // tr,),
               [(h, (tr, D), lambda i: (i, 0)), (g, (1, D), lambda i: (0, 0))],
               [((S, D), BF, (tr, D), lambda i: (i, 0)), ((S, 1), F32, (tr, 1), lambda i: (i, 0))], fn, deps=deps)


def _rms_bwd(name, du, h, r, g, dres, scale):
    S, D = h.shape
    tr = _pick(S, 256, SUBLANE)

    def fn(duv, hv, rv, gv, drv):
        xn = hv * rv
        dxn = duv * gv
        dh = drv + rv * (dxn - xn * jnp.mean(dxn * xn, axis=-1, keepdims=True))
        return dh, scale * dh, jnp.sum(duv * xn, axis=0, keepdims=True)

    row = ((tr, D), lambda i: (i, 0))
    return _ew(name, (S // tr,),
               [(du,) + row, (h,) + row, (r, (tr, 1), lambda i: (i, 0)), (g, (1, D), lambda i: (0, 0)), (dres,) + row],
               [((S, D), F32) + row, ((S, D), BF) + row, ((1, D), F32, (1, D), lambda i: (0, 0))], fn, acc=(2,))


def _loss_head(name, h, g, target):
    S, D = h.shape
    tr = _pick(S, 256, SUBLANE)

    def fn(hv, gv, tv):
        r = lax.rsqrt(jnp.mean(hv * hv, axis=-1, keepdims=True) + EPS)
        xn = hv * r
        diff = xn * gv - tv
        loss = 0.5 * jnp.sum(jnp.mean(diff * diff, axis=-1, keepdims=True))
        dout = diff / D
        dxn = dout * gv
        dh = r * (dxn - xn * jnp.mean(dxn * xn, axis=-1, keepdims=True))
        return (jnp.zeros((1, LANE), F32) + loss, dh, 0.5 * dh, jnp.sum(dout * xn, axis=0, keepdims=True))

    row = ((tr, D), lambda i: (i, 0))
    return _ew(name, (S // tr,),
               [(h,) + row, (g, (1, D), lambda i: (0, 0)), (target,) + row],
               [((1, LANE), F32, (1, LANE), lambda i: (0, 0)), ((S, D), F32) + row, ((S, D), BF) + row,
                ((1, D), F32, (1, D), lambda i: (0, 0))], fn, acc=(0, 3))


def _ffn_fwd(tag, u, h, wg, wu, wd, deps=()):
    S, D = u.shape
    Fs = wg.shape[1]
    tm, tk = _pick(S, 512, SUBLANE), _pick(D, K_TILE, LANE)

    def up_epilogue(accs, sides):
        gt, up = accs
        return gt, up, gt * _sigmoid(gt) * up

    act = ((NDEV, S, Fs), BF, (None, tm, Fs), lambda b, i, j, k: (b, i, 0))
    gt, up, a = _mm(
        tag + "_up", (NDEV, S // tm, 1, D // tk),
        [(u, (tm, tk), lambda b, i, j, k: (i, k)),
         (wg, (None, Fs, tk), lambda b, i, j, k: (b, 0, k)), (wu, (None, Fs, tk), lambda b, i, j, k: (b, 0, k))],
        [(0, 1, NT, 0), (0, 2, NT, 1)], [], [act, act, act], up_epilogue, [(tm, Fs), (tm, Fs)], deps=deps)
    if callable(wd):
        wd = wd(a)
    tn = _pick(D, 1024, LANE)
    (hn,) = _mm(
        tag + "_down", (1, S // tm, D // tn, NDEV),
        [(a, (None, tm, Fs), lambda b, i, j, k: (k, i, 0)), (wd, (None, Fs, tn), lambda b, i, j, k: (k, 0, j))],
        [(0, 1, NN, 0)], [(h, (tm, tn), lambda b, i, j, k: (i, j))],
        [((S, D), F32, (tm, tn), lambda b, i, j, k: (i, j))],
        lambda accs, sides: [sides[0] + 0.5 * accs[0]], [(tm, tn)])
    return hn, (gt, up, a), wd


def _ffn_bwd(tag, dhs, u, saved, wg, wu, wd):
    gt, up, a = saved
    S, D = u.shape
    Fs = wg.shape[1]
    tm, tk = _pick(S, 512, SUBLANE), _pick(D, K_TILE, LANE)
    act_in = lambda arr: (arr, (None, tm, Fs), lambda b, i, j, k: (b, i, 0))
    act_out = ((NDEV, S, Fs), BF, (None, tm, Fs), lambda b, i, j, k: (b, i, 0))

    def act_epilogue(accs, sides):
        da = accs[0]
        gtv, upv = sides[0].astype(F32), sides[1].astype(F32)
        sg = _sigmoid(gtv)
        return da * upv * sg * (1.0 + gtv * (1.0 - sg)), da * gtv * sg

    dgt, dup = _mm(
        tag + "_dact", (NDEV, S // tm, 1, D // tk),
        [(dhs, (tm, tk), lambda b, i, j, k: (i, k)), (wd, (None, Fs, tk), lambda b, i, j, k: (b, 0, k))],
        [(0, 1, NT, 0)], [act_in(gt), act_in(up)], [act_out, act_out], act_epilogue, [(tm, Fs)])

    ts = _pick(S, K_TILE, SUBLANE)
    tn = _pick(D, 1024, LANE)
    wgrad = ((NDEV, Fs, D), BF, (None, Fs, tn), lambda b, i, j, k: (b, 0, j))
    tok = lambda arr: (arr, (None, ts, Fs), lambda b, i, j, k: (b, k, 0))

    def weight_grads(deps=()):
        (dwd,) = _mm(
            tag + "_dwd", (NDEV, 1, D // tn, S // ts),
            [tok(a), (dhs, (ts, tn), lambda b, i, j, k: (k, j))],
            [(0, 1, TN, 0)], [], [wgrad], lambda accs, sides: accs, [(Fs, tn)], deps=deps)
        dwg, dwu = _mm(
            tag + "_dwgu", (NDEV, 1, D // tn, S // ts),
            [tok(dgt), tok(dup), (u, (ts, tn), lambda b, i, j, k: (k, j))],
            [(0, 2, TN, 0), (1, 2, TN, 1)], [], [wgrad, wgrad], lambda accs, sides: accs, [(Fs, tn), (Fs, tn)])
        return dwg, dwu, dwd

    def du(deps=()):
        return _mm(
            tag + "_du", (1, S // tm, D // tn, NDEV),
            [(dgt, (None, tm, Fs), lambda b, i, j, k: (k, i, 0)), (dup, (None, tm, Fs), lambda b, i, j, k: (k, i, 0)),
             (wg, (None, Fs, tn), lambda b, i, j, k: (k, 0, j)), (wu, (None, Fs, tn), lambda b, i, j, k: (k, 0, j))],
            [(0, 2, NN, 0), (1, 3, NN, 0)], [], [((S, D), F32, (tm, tn), lambda b, i, j, k: (i, j))],
            lambda accs, sides: accs, [(tm, tn)], deps=deps)[0]

    return weight_grads, du


def _ssm_params(lam_re, lam_im, log_dt, b_re, b_im, c_re, c_im):
    G, N = lam_re.shape
    C = b_re.shape[2]
    lam_re = jnp.minimum(lam_re, -1e-4)
    dt = jnp.exp(log_dt)[:, None]
    mag = jnp.exp(lam_re * dt)
    a_re = mag * jnp.cos(lam_im * dt)
    a_im = mag * jnp.sin(lam_im * dt)
    den = lam_re * lam_re + lam_im * lam_im
    p = a_re - 1.0
    f_re = ((p * lam_re + a_im * lam_im) / den)[:, :, None]
    f_im = ((a_im * lam_re - p * lam_im) / den)[:, :, None]
    bb_re = f_re * b_re - f_im * b_im
    bb_im = f_re * b_im + f_im * b_re
    gpt = LANE // C
    tiles = G // gpt
    eye = jnp.eye(gpt, dtype=F32)

    def bd(bb):
        return jnp.einsum("bgnc,gh->bgchn", bb.reshape(tiles, gpt, N, C), eye).reshape(tiles, gpt * C, gpt * N)

    def cd(cc):
        return jnp.einsum("bgcn,gh->bgnhc", cc.reshape(tiles, gpt, C, N), eye).reshape(tiles, gpt * N, gpt * C)

    rows = G * N // LANE
    return (a_re.reshape(rows, LANE), a_im.reshape(rows, LANE), bd(bb_re), bd(bb_im), cd(c_re), cd(-c_im))


def _scan_fwd(bu_re, bu_im, a_re, a_im):
    S, R, _ = bu_re.shape
    tc = _pick(S, 256, SUBLANE)

    def body(bre, bim, are, aim, sre, sim, carry):
        @pl.when(pl.program_id(0) == 0)
        def _():
            carry[...] = jnp.zeros_like(carry)

        ar, ai = are[...], aim[...]

        def step(t, c):
            pr, pi = c
            nr = ar * pr - ai * pi + bre[t]
            ni = ar * pi + ai * pr + bim[t]
            sre[t] = nr
            sim[t] = ni
            return nr, ni

        pr, pi = lax.fori_loop(0, tc, step, (carry[0], carry[1]), unroll=8)
        carry[0] = pr
        carry[1] = pi

    blk = pl.BlockSpec((tc, R, LANE), lambda i: (i, 0, 0))
    par = pl.BlockSpec((R, LANE), lambda i: (0, 0))
    return pl.pallas_call(
        body, name="ssm_scan_fwd", grid=(S // tc,),
        in_specs=[blk, blk, par, par], out_specs=[blk, blk],
        out_shape=[jax.ShapeDtypeStruct((S, R, LANE), F32)] * 2,
        scratch_shapes=[pltpu.VMEM((2, R, LANE), F32)],
        compiler_params=pltpu.CompilerParams(dimension_semantics=("arbitrary",)),
    )(bu_re, bu_im, a_re, a_im)


def _scan_bwd(ds_re, ds_im, s_re, s_im, a_re, a_im):
    S, R, _ = ds_re.shape
    tc = _pick(S, 256, SUBLANE)
    nc = S // tc

    def body(dre, dim_, sre, sim, are, aim, lre, lim, dar, dai, carry):
        @pl.when(pl.program_id(0) == 0)
        def _():
            carry[...] = jnp.zeros_like(carry)
            dar[...] = jnp.zeros_like(dar)
            dai[...] = jnp.zeros_like(dai)

        ar, ai = are[...], aim[...]

        def step(tt, c):
            t = tc - 1 - tt
            lr, li, gr, gi = c
            sr, si = sre[t], sim[t]
            gr = gr + lr * sr + li * si
            gi = gi + li * sr - lr * si
            nlr = dre[t] + ar * lr + ai * li
            nli = dim_[t] + ar * li - ai * lr
            lre[t] = nlr
            lim[t] = nli
            return nlr, nli, gr, gi

        lr, li, gr, gi = lax.fori_loop(0, tc, step, (carry[0], carry[1], dar[...], dai[...]), unroll=8)
        carry[0] = lr
        carry[1] = li
        dar[...] = gr
        dai[...] = gi

    blk = pl.BlockSpec((tc, R, LANE), lambda i: (nc - 1 - i, 0, 0))
    par = pl.BlockSpec((R, LANE), lambda i: (0, 0))
    return pl.pallas_call(
        body, name="ssm_scan_bwd", grid=(nc,),
        in_specs=[blk, blk, blk, blk, par, par], out_specs=[blk, blk, par, par],
        out_shape=[jax.ShapeDtypeStruct((S, R, LANE), F32)] * 2 + [jax.ShapeDtypeStruct((R, LANE), F32)] * 2,
        scratch_shapes=[pltpu.VMEM((2, R, LANE), F32)],
        compiler_params=pltpu.CompilerParams(dimension_semantics=("arbitrary",)),
    )(ds_re, ds_im, s_re, s_im, a_re, a_im)


def _shift_down(z, k):
    t = lax.broadcasted_iota(I32, z.shape, 0)
    return jnp.where(t >= k, pltpu.roll(z, k, 0), 0.0)


def _shift_up(z, k):
    n = z.shape[0]
    t = lax.broadcasted_iota(I32, z.shape, 0)
    return jnp.where(t < n - k, pltpu.roll(z, n - k, 0), 0.0)


def _conv_fwd(proj, cw, cb):
    _, S, W = proj.shape
    ct = _pick(W, 256, LANE)

    def fn(bg, cg, val, w, b):
        z = cg * val
        conv = b + w[0:1] * _shift_down(z, 2) + w[1:2] * _shift_down(z, 1) + w[2:3] * z
        return bg * conv, conv

    sl = lambda s: (proj, (None, S, ct), lambda j, s=s: (s, 0, j))
    col = ((S, ct), lambda j: (0, j))
    return _ew("conv_fwd", (W // ct,),
               [sl(1), sl(2), sl(3), (cw, (3, ct), lambda j: (0, j)), (cb, (1, ct), lambda j: (0, j))],
               [((S, W), BF) + col, ((S, W), F32) + col], fn)


def _conv_bwd(dyb, proj, conv, cw):
    _, S, W = proj.shape
    ct = _pick(W, 256, LANE)

    def fn(dy, bg, cg, val, cv, w):
        z = cg * val
        z1, z2 = _shift_down(z, 1), _shift_down(z, 2)
        dconv = dy * bg
        dz = w[2:3] * dconv + w[1:2] * _shift_up(dconv, 1) + w[0:1] * _shift_up(dconv, 2)
        dw = jnp.concatenate([jnp.sum(dconv * z2, axis=0, keepdims=True), jnp.sum(dconv * z1, axis=0, keepdims=True),
                              jnp.sum(dconv * z, axis=0, keepdims=True)], axis=0)
        return dy * cv, dz * val, dz * cg, dw, jnp.sum(dconv, axis=0, keepdims=True)

    sl = lambda s: (proj, (None, S, ct), lambda j, s=s: (s, 0, j))
    col = ((S, ct), lambda j: (0, j))
    return _ew("conv_bwd", (W // ct,),
               [(dyb,) + col, sl(1), sl(2), sl(3), (conv,) + col, (cw, (3, ct), lambda j: (0, j))],
               [((S, W), BF) + col, ((S, W), BF) + col, ((S, W), BF) + col,
                ((3, W), F32, (3, ct), lambda j: (0, j)), ((1, W), F32, (1, ct), lambda j: (0, j))], fn)


def _plain(accs, sides):
    return accs


def kernel(x, ffn1_norm, ffn1_w_gate, ffn1_w_up, ffn1_w_down, mix_norm, w_in, ssm_lambda_re, ssm_lambda_im, ssm_log_dt, ssm_b_re, ssm_b_im, ssm_c_re, ssm_c_im, ssm_d, ssm_w_glu, ssm_b_glu, ssm_w_out, conv_w, conv_b, conv_w_out, w_o, ffn2_norm, ffn2_w_gate, ffn2_w_up, ffn2_w_down, final_norm, loss_target, m_ffn1_norm, m_ffn1_w_gate, m_ffn1_w_up, m_ffn1_w_down, m_mix_norm, m_w_in, m_ssm_lambda_re, m_ssm_lambda_im, m_ssm_log_dt, m_ssm_b_re, m_ssm_b_im, m_ssm_c_re, m_ssm_c_im, m_ssm_d, m_ssm_w_glu, m_ssm_b_glu, m_ssm_w_out, m_conv_w, m_conv_b, m_conv_w_out, m_w_o, m_ffn2_norm, m_ffn2_w_gate, m_ffn2_w_up, m_ffn2_w_down, m_final_norm, v_ffn1_norm, v_ffn1_w_gate, v_ffn1_w_up, v_ffn1_w_down, v_mix_norm, v_w_in, v_ssm_lambda_re, v_ssm_lambda_im, v_ssm_log_dt, v_ssm_b_re, v_ssm_b_im, v_ssm_c_re, v_ssm_c_im, v_ssm_d, v_ssm_w_glu, v_ssm_b_glu, v_ssm_w_out, v_conv_w, v_conv_b, v_conv_w_out, v_w_o, v_ffn2_norm, v_ffn2_w_gate, v_ffn2_w_up, v_ffn2_w_down, v_final_norm):
    P = dict(ffn1_norm=ffn1_norm, ffn1_w_gate=ffn1_w_gate, ffn1_w_up=ffn1_w_up, ffn1_w_down=ffn1_w_down, mix_norm=mix_norm, w_in=w_in, ssm_lambda_re=ssm_lambda_re, ssm_lambda_im=ssm_lambda_im, ssm_log_dt=ssm_log_dt, ssm_b_re=ssm_b_re, ssm_b_im=ssm_b_im, ssm_c_re=ssm_c_re, ssm_c_im=ssm_c_im, ssm_d=ssm_d, ssm_w_glu=ssm_w_glu, ssm_b_glu=ssm_b_glu, ssm_w_out=ssm_w_out, conv_w=conv_w, conv_b=conv_b, conv_w_out=conv_w_out, w_o=w_o, ffn2_norm=ffn2_norm, ffn2_w_gate=ffn2_w_gate, ffn2_w_up=ffn2_w_up, ffn2_w_down=ffn2_w_down, final_norm=final_norm)
    M = dict(ffn1_norm=m_ffn1_norm, ffn1_w_gate=m_ffn1_w_gate, ffn1_w_up=m_ffn1_w_up, ffn1_w_down=m_ffn1_w_down, mix_norm=m_mix_norm, w_in=m_w_in, ssm_lambda_re=m_ssm_lambda_re, ssm_lambda_im=m_ssm_lambda_im, ssm_log_dt=m_ssm_log_dt, ssm_b_re=m_ssm_b_re, ssm_b_im=m_ssm_b_im, ssm_c_re=m_ssm_c_re, ssm_c_im=m_ssm_c_im, ssm_d=m_ssm_d, ssm_w_glu=m_ssm_w_glu, ssm_b_glu=m_ssm_b_glu, ssm_w_out=m_ssm_w_out, conv_w=m_conv_w, conv_b=m_conv_b, conv_w_out=m_conv_w_out, w_o=m_w_o, ffn2_norm=m_ffn2_norm, ffn2_w_gate=m_ffn2_w_gate, ffn2_w_up=m_ffn2_w_up, ffn2_w_down=m_ffn2_w_down, final_norm=m_final_norm)
    V = dict(ffn1_norm=v_ffn1_norm, ffn1_w_gate=v_ffn1_w_gate, ffn1_w_up=v_ffn1_w_up, ffn1_w_down=v_ffn1_w_down, mix_norm=v_mix_norm, w_in=v_w_in, ssm_lambda_re=v_ssm_lambda_re, ssm_lambda_im=v_ssm_lambda_im, ssm_log_dt=v_ssm_log_dt, ssm_b_re=v_ssm_b_re, ssm_b_im=v_ssm_b_im, ssm_c_re=v_ssm_c_re, ssm_c_im=v_ssm_c_im, ssm_d=v_ssm_d, ssm_w_glu=v_ssm_w_glu, ssm_b_glu=v_ssm_b_glu, ssm_w_out=v_ssm_w_out, conv_w=v_conv_w, conv_b=v_conv_b, conv_w_out=v_conv_w_out, w_o=v_w_o, ffn2_norm=v_ffn2_norm, ffn2_w_gate=v_ffn2_w_gate, ffn2_w_up=v_ffn2_w_up, ffn2_w_down=v_ffn2_w_down, final_norm=v_final_norm)
    names = list(P)
    sharded = ["ffn1_w_gate", "ffn1_w_up", "ffn1_w_down", "w_in", "ssm_w_glu", "ssm_w_out", "conv_w_out", "w_o",
               "ffn2_w_gate", "ffn2_w_up", "ffn2_w_down"]
    replicated = [n for n in names if n not in sharded and n != "conv_w"]

    S, D = x.shape[1], x.shape[2]
    W = ssm_d.shape[0]
    Dc = D // NDEV
    G, N = ssm_lambda_re.shape
    GN = G * N
    rows = GN // LANE
    xh = x.reshape(S, D)
    target = loss_target.reshape(S, D)
    xi, yi, ci = lax.axis_index("x"), lax.axis_index("y"), lax.axis_index("c")
    c_arr = jnp.reshape(ci, (1,)).astype(I32)
    q_arr = jnp.reshape(2 * xi + yi, (1,)).astype(I32)
    row = lambda v: v.reshape(1, -1)

    transposed = ("ffn1_w_gate", "ffn1_w_up", "ffn2_w_gate", "ffn2_w_up")
    local = lambda table, n: table[n].T if n in transposed else table[n]

    groups = [sharded[0:2], sharded[2:3], sharded[3:8] + ["conv_w"], sharded[8:11]]
    started = {}

    me = 4 * xi + 2 * yi + ci

    def gather_start(gi, after):
        srcs = [conv_w if n == "conv_w" else local(P, n).astype(BF) for n in groups[gi]]
        lands = [lax.dynamic_update_slice(lax.empty((NDEV,) + s.shape, s.dtype), s[None], (me,) + (0,) * s.ndim)
                 for s in srcs]
        started[gi] = _split_start("gather_start_%d" % gi, srcs, lands, _gather_copies, 4, after)
        return started[gi][4]

    def gathered(gi, after):
        _, lands = _split_wait("gather_wait_%d" % gi, started[gi], _gather_copies, after)
        return _gather_forward("gather_forward_%d" % gi, lands)

    tm = _pick(S, 512, SUBLANE)
    tk = _pick(D, K_TILE, LANE)
    ts = _pick(S, K_TILE, SUBLANE)
    tn = _pick(D, 1024, LANE)

    token0 = gather_start(0, [])
    u1, r1 = _rms_fwd("rms1", xh, row(ffn1_norm), deps=[token0])
    wg1, wu1 = gathered(0, u1)
    tokens = [gather_start(1, [wg1]), gather_start(2, [wg1])]
    h1, ffn1_saved, wd1 = _ffn_fwd("ffn1", u1, xh, wg1, wu1, lambda after: gathered(1, after)[0], deps=tokens)
    u2, r2 = _rms_fwd("rms2", h1, row(mix_norm))
    w_in_f, w_glu_f, w_so, w_co, w_o_f, cw_f = gathered(2, u2)
    token3 = gather_start(3, [w_in_f])
    w_glu_f = w_glu_f.reshape(W, W)
    w_o_f = w_o_f.reshape(D, D)
    cw = jnp.transpose(cw_f, (1, 0, 2)).reshape(3, W)
    (proj,) = _mm(
        "in_proj", (NDEV, S // tm, 1, D // tk),
        [(u2, (tm, tk), lambda b, i, j, k: (i, k)), (w_in_f, (None, tk, W), lambda b, i, j, k: (b, k, 0))],
        [(0, 1, NN, 0)], [], [((NDEV, S, W), F32, (None, tm, W), lambda b, i, j, k: (b, i, 0))], _plain, [(tm, W)],
        deps=[token3])

    ssm_in = (ssm_lambda_re, ssm_lambda_im, ssm_log_dt, ssm_b_re, ssm_b_im, ssm_c_re, ssm_c_im)
    (a_re, a_im, bd_re, bd_im, cd_re, cd_imn), ssm_vjp = jax.vjp(_ssm_params, *ssm_in)
    tiles, tch, tst = bd_re.shape
    bd_re_b, bd_im_b, cd_re_b, cd_imn_b = (t.astype(BF) for t in (bd_re, bd_im, cd_re, cd_imn))
    v_bf = proj[0].astype(BF)
    st_blk = lambda arr: (arr, (tm, tst), lambda b, i, j, k: (i, b))
    ch_blk = lambda arr: (arr, (tm, tch), lambda b, i, j, k: (i, b))
    bd_blk = lambda arr: (arr, (None, tch, tst), lambda b, i, j, k: (b, 0, 0))
    cd_blk = lambda arr: (arr, (None, tst, tch), lambda b, i, j, k: (b, 0, 0))
    st_out = lambda dt: ((S, GN), dt, (tm, tst), lambda b, i, j, k: (i, b))
    ch_out = lambda dt: ((S, W), dt, (tm, tch), lambda b, i, j, k: (i, b))
    d_blk = (row(ssm_d), (1, tch), lambda b, i, j, k: (0, b))
    v_blk = (proj, (None, tm, tch), lambda b, i, j, k: (0, i, b))

    bu_re, bu_im = _mm("ssm_bu", (tiles, S // tm, 1, 1), [ch_blk(v_bf), bd_blk(bd_re_b), bd_blk(bd_im_b)],
                       [(0, 1, NN, 0), (0, 2, NN, 1)], [], [st_out(F32), st_out(F32)], _plain, [(tm, tst)] * 2)
    s_re3, s_im3 = _scan_fwd(bu_re.reshape(S, rows, LANE), bu_im.reshape(S, rows, LANE), a_re, a_im)
    s_re_b = s_re3.reshape(S, GN).astype(BF)
    s_im_b = s_im3.reshape(S, GN).astype(BF)

    def y0_epilogue(accs, sides):
        y0 = accs[0] + sides[1] * sides[0]
        return y0, _gelu(y0)

    y0, y1 = _mm("ssm_y0", (tiles, S // tm, 1, 1),
                 [st_blk(s_re_b), st_blk(s_im_b), cd_blk(cd_re_b), cd_blk(cd_imn_b)],
                 [(0, 2, NN, 0), (1, 3, NN, 0)], [v_blk, d_blk], [ch_out(F32), ch_out(BF)], y0_epilogue, [(tm, tch)])

    tw = _pick(W, 512, LANE)

    def glu_epilogue(accs, sides):
        q = accs[0] + sides[1]
        return q, _gelu(sides[0]) * _sigmoid(q)

    q_pre, y2 = _mm("ssm_glu", (1, S // tm, W // tw, 1),
                    [(y1, (tm, W), lambda b, i, j, k: (i, 0)), (w_glu_f, (W, tw), lambda b, i, j, k: (0, j))],
                    [(0, 1, NN, 0)],
                    [(y0, (tm, tw), lambda b, i, j, k: (i, j)), (row(ssm_b_glu), (1, tw), lambda b, i, j, k: (0, j))],
                    [((S, W), F32, (tm, tw), lambda b, i, j, k: (i, j)), ((S, W), BF, (tm, tw), lambda b, i, j, k: (i, j))],
                    glu_epilogue, [(tm, tw)])

    yb, conv = _conv_fwd(proj, cw, row(conv_b))

    per = W // Dc
    ga_blk = (proj, (None, tm, Dc), lambda b, i, j, k: (4 + b // per, i, b % per))
    gb_blk = (proj, (None, tm, Dc), lambda b, i, j, k: (6 + b // per, i, b % per))
    dc_out = ((S, D), BF, (tm, Dc), lambda b, i, j, k: (i, b))

    def merge_epilogue(accs, sides):
        za, zb = accs
        return _sigmoid(sides[0]) * za + _sigmoid(sides[1]) * zb, za, zb

    merged, z_a, z_b = _mm(
        "mix_merge", (NDEV, S // tm, 1, 1),
        [(y2, (tm, W), lambda b, i, j, k: (i, 0)), (yb, (tm, W), lambda b, i, j, k: (i, 0)),
         (w_so, (None, W, Dc), lambda b, i, j, k: (b, 0, 0)), (w_co, (None, W, Dc), lambda b, i, j, k: (b, 0, 0))],
        [(0, 2, NN, 0), (1, 3, NN, 1)], [ga_blk, gb_blk], [dc_out, dc_out, dc_out], merge_epilogue, [(tm, Dc)] * 2)

    (h2,) = _mm("mix_out", (1, S // tm, D // tn, D // tk),
                [(merged, (tm, tk), lambda b, i, j, k: (i, k)), (w_o_f, (tk, tn), lambda b, i, j, k: (k, j))],
                [(0, 1, NN, 0)], [(h1, (tm, tn), lambda b, i, j, k: (i, j))],
                [((S, D), F32, (tm, tn), lambda b, i, j, k: (i, j))],
                lambda accs, sides: [sides[0] + accs[0]], [(tm, tn)])

    u3, r3 = _rms_fwd("rms3", h2, row(ffn2_norm))
    wg2, wu2, wd2 = gathered(3, u3)
    h3, ffn2_saved, _ = _ffn_fwd("ffn2", u3, h2, wg2, wu2, wd2)
    loss_vec, dh3, dh3_half, d_final_norm = _loss_head("loss_head", h3, row(final_norm), target)
    loss = lax.psum(loss_vec[0, 0], ("x", "y", "c"))

    grads, deltas, new_m, new_v = {}, {}, {}, {}

    def rs_begin(tag, parts):
        lands = _exchange_sibling("rs_sibling_" + tag, parts)
        sums = [_sum_sibling("rs_sum_%s_%d" % (tag, a), p, land, c_arr) for a, (p, land) in enumerate(zip(parts, lands))]
        lands2 = [lax.empty((3,) + sm.shape[1:], sm.dtype) for sm in sums]
        return _split_start("rs_chips_start_" + tag, sums, lands2, _chips_copies, 3)

    def rs_end(tag, group, begun, after):
        sums, lands2 = _split_wait("rs_chips_wait_" + tag, begun, _chips_copies, after)
        for n, sm, land2 in zip(group, sums, lands2):
            res = _finish_sharded("adamw_" + n, sm, land2, q_arr, local(P, n), local(M, n), local(V, n))
            grads[n], deltas[n], new_m[n], new_v[n] = [t.T if n in transposed else t for t in res]

    ffn2_dw, ffn2_du = _ffn_bwd("ffn2", dh3_half, u3, ffn2_saved, wg2, wu2, wd2)
    rs_ffn2 = rs_begin("ffn2", list(ffn2_dw()))
    du3 = ffn2_du(deps=[rs_ffn2[4]])
    dh2, dh2_b, d_ffn2_norm = _rms_bwd("rms3_bwd", du3, h2, r3, row(ffn2_norm), dh3, 1.0)

    dg_out = ((2, S, W), BF, (None, tm, Dc), lambda b, i, j, k: (j // per, i, j % per))
    ga_blk2 = (proj, (None, tm, Dc), lambda b, i, j, k: (4 + j // per, i, j % per))
    gb_blk2 = (proj, (None, tm, Dc), lambda b, i, j, k: (6 + j // per, i, j % per))
    dcj = lambda arr: (arr, (tm, Dc), lambda b, i, j, k: (i, j))
    dcj_out = ((S, D), BF, (tm, Dc), lambda b, i, j, k: (i, j))

    def dmerge_epilogue(accs, sides):
        dm = accs[0]
        sa, sb = _sigmoid(sides[0]), _sigmoid(sides[1])
        za, zb = sides[2].astype(F32), sides[3].astype(F32)
        return dm * sa, dm * sb, dm * za * sa * (1.0 - sa), dm * zb * sb * (1.0 - sb)

    dz_a, dz_b, dga, dgb = _mm(
        "mix_out_dx", (1, S // tm, NDEV, D // tk),
        [(dh2_b, (tm, tk), lambda b, i, j, k: (i, k)), (w_o_f, (Dc, tk), lambda b, i, j, k: (j, k))],
        [(0, 1, NT, 0)], [ga_blk2, gb_blk2, dcj(z_a), dcj(z_b)], [dcj_out, dcj_out, dg_out, dg_out],
        dmerge_epilogue, [(tm, Dc)])

    td = _pick(D, 512, LANE)
    (dw_o,) = _mm("mix_out_dw", (1, D // td, D // tn, S // ts),
                  [(merged, (ts, td), lambda b, i, j, k: (k, i)), (dh2_b, (ts, tn), lambda b, i, j, k: (k, j))],
                  [(0, 1, TN, 0)], [], [((D, D), BF, (td, tn), lambda b, i, j, k: (i, j))], _plain, [(td, tn)])

    wout = ((NDEV, W, Dc), BF, (None, W, Dc), lambda b, i, j, k: (b, 0, 0))
    dw_so, dw_co = _mm(
        "mix_merge_dw", (NDEV, 1, 1, S // ts),
        [(y2, (ts, W), lambda b, i, j, k: (k, 0)), (yb, (ts, W), lambda b, i, j, k: (k, 0)),
         (dz_a, (ts, Dc), lambda b, i, j, k: (k, b)), (dz_b, (ts, Dc), lambda b, i, j, k: (k, b))],
        [(0, 2, TN, 0), (1, 3, TN, 1)], [], [wout, wout], _plain, [(W, Dc)] * 2)

    def dglu_epilogue(accs, sides):
        dy2, dyb = accs
        sq = _sigmoid(sides[1])
        return dy2 * _gelu(sides[0]) * sq * (1.0 - sq), dy2 * sq, dyb

    full_w = lambda arr: (arr, (tm, W), lambda b, i, j, k: (i, 0))
    full_w_out = lambda dt: ((S, W), dt, (tm, W), lambda b, i, j, k: (i, 0))
    dq, dy1p, dyb = _mm(
        "mix_merge_dx", (1, S // tm, 1, NDEV),
        [(dz_a, (tm, Dc), lambda b, i, j, k: (i, k)), (dz_b, (tm, Dc), lambda b, i, j, k: (i, k)),
         (w_so, (None, W, Dc), lambda b, i, j, k: (k, 0, 0)), (w_co, (None, W, Dc), lambda b, i, j, k: (k, 0, 0))],
        [(0, 2, NT, 0), (1, 3, NT, 1)], [full_w(y0), full_w(q_pre)], [full_w_out(BF), full_w_out(F32), full_w_out(F32)],
        dglu_epilogue, [(tm, W)] * 2)

    def dy0_epilogue(accs, sides):
        dy0 = (sides[0] + accs[0]) * _gelu_grad(sides[1])
        return dy0, dy0

    wj = lambda arr: (arr, (tm, tw), lambda b, i, j, k: (i, j))
    dy0, dy0_b = _mm("ssm_glu_dx", (1, S // tm, W // tw, 1),
                     [(dq, (tm, W), lambda b, i, j, k: (i, 0)), (w_glu_f, (tw, W), lambda b, i, j, k: (j, 0))],
                     [(0, 1, NT, 0)], [wj(dy1p), wj(y0)],
                     [((S, W), F32, (tm, tw), lambda b, i, j, k: (i, j)), ((S, W), BF, (tm, tw), lambda b, i, j, k: (i, j))],
                     dy0_epilogue, [(tm, tw)])

    (dw_glu,) = _mm("ssm_glu_dw", (1, W // tw, 1, S // ts),
                    [(y1, (ts, tw), lambda b, i, j, k: (k, i)), (dq, (ts, W), lambda b, i, j, k: (k, 0))],
                    [(0, 1, TN, 0)], [], [((W, W), BF, (tw, W), lambda b, i, j, k: (i, 0))], _plain, [(tw, W)])

    tr = _pick(S, 256, SUBLANE)
    rw = ((tr, W), lambda i: (i, 0))
    vec_w = ((1, W), F32, (1, W), lambda i: (0, 0))
    d_b_glu, d_ssm_d = _ew(
        "ssm_colsums", (S // tr,), [(dq,) + rw, (dy0,) + rw, (proj, (None, tr, W), lambda i: (0, i, 0))],
        [vec_w, vec_w],
        lambda dqv, dyv, vv: (jnp.sum(dqv.astype(F32), axis=0, keepdims=True), jnp.sum(dyv * vv, axis=0, keepdims=True)),
        acc=(0, 1))

    ds_re, ds_im = _mm("ssm_ds", (tiles, S // tm, 1, 1), [ch_blk(dy0_b), cd_blk(cd_re_b), cd_blk(cd_imn_b)],
                       [(0, 1, NT, 0), (0, 2, NT, 1)], [], [st_out(F32), st_out(F32)], _plain, [(tm, tst)] * 2)
    lam_re3, lam_im3, da_re, da_im = _scan_bwd(ds_re.reshape(S, rows, LANE), ds_im.reshape(S, rows, LANE),
                                               s_re3, s_im3, a_re, a_im)
    lam_re_b = lam_re3.reshape(S, GN).astype(BF)
    lam_im_b = lam_im3.reshape(S, GN).astype(BF)

    (dv,) = _mm("ssm_dv", (tiles, S // tm, 1, 1),
                [st_blk(lam_re_b), st_blk(lam_im_b), bd_blk(bd_re_b), bd_blk(bd_im_b)],
                [(0, 2, NT, 0), (1, 3, NT, 0)], [ch_blk(dy0), d_blk], [ch_out(BF)],
                lambda accs, sides: [accs[0] + sides[0] * sides[1]], [(tm, tch)])

    tok_ch = lambda arr: (arr, (ts, tch), lambda b, i, j, k: (k, b))
    tok_st = lambda arr: (arr, (ts, tst), lambda b, i, j, k: (k, b))
    bd_out = ((tiles, tch, tst), F32, (None, tch, tst), lambda b, i, j, k: (b, 0, 0))
    cd_out = ((tiles, tst, tch), F32, (None, tst, tch), lambda b, i, j, k: (b, 0, 0))
    dbd_re, dbd_im = _mm("ssm_dbd", (tiles, 1, 1, S // ts), [tok_ch(v_bf), tok_st(lam_re_b), tok_st(lam_im_b)],
                         [(0, 1, TN, 0), (0, 2, TN, 1)], [], [bd_out, bd_out], _plain, [(tch, tst)] * 2)
    dcd_re, dcd_imn = _mm("ssm_dcd", (tiles, 1, 1, S // ts), [tok_st(s_re_b), tok_st(s_im_b), tok_ch(dy0_b)],
                          [(0, 2, TN, 0), (1, 2, TN, 1)], [], [cd_out, cd_out], _plain, [(tst, tch)] * 2)
    d_ssm = ssm_vjp((da_re, da_im, dbd_re, dbd_im, dcd_re, dcd_imn))

    dbg, dcg, dval, d_conv_w_full, d_conv_b = _conv_bwd(dyb, proj, conv, cw)
    dproj = jnp.concatenate([dv[None], dbg[None], dcg[None], dval[None], dga, dgb], axis=0)

    (dw_in,) = _mm("in_proj_dw", (NDEV, D // td, 1, S // ts),
                   [(u2, (ts, td), lambda b, i, j, k: (k, i)), (dproj, (None, ts, W), lambda b, i, j, k: (b, k, 0))],
                   [(0, 1, TN, 0)], [], [((NDEV, D, W), BF, (None, td, W), lambda b, i, j, k: (b, i, 0))],
                   _plain, [(td, W)])
    rs_mixer = rs_begin("mixer", [dw_in, dw_glu.reshape(NDEV, W // NDEV, W), dw_so, dw_co, dw_o.reshape(NDEV, Dc, D)])
    (du2,) = _mm("in_proj_dx", (1, S // tm, D // tn, NDEV),
                 [(dproj, (None, tm, W), lambda b, i, j, k: (k, i, 0)), (w_in_f, (None, tn, W), lambda b, i, j, k: (k, j, 0))],
                 [(0, 1, NT, 0)], [], [((S, D), F32, (tm, tn), lambda b, i, j, k: (i, j))], _plain, [(tm, tn)],
                 deps=[rs_mixer[4]])
    dh1, dh1_half, d_mix_norm = _rms_bwd("rms2_bwd", du2, h1, r2, row(mix_norm), dh2, 0.5)

    small = dict(mix_norm=d_mix_norm, ffn2_norm=d_ffn2_norm, final_norm=d_final_norm,
                 ssm_lambda_re=d_ssm[0], ssm_lambda_im=d_ssm[1], ssm_log_dt=d_ssm[2], ssm_b_re=d_ssm[3],
                 ssm_b_im=d_ssm[4], ssm_c_re=d_ssm[5], ssm_c_im=d_ssm[6], ssm_d=d_ssm_d, ssm_b_glu=d_b_glu,
                 conv_b=d_conv_b)
    replicated = [n for n in replicated if n != "ffn1_norm"] + ["ffn1_norm"]
    sizes = [P[n].size for n in replicated]
    early = sum(sizes[:-1]) + d_conv_w_full.size
    padded = -(-early // (SUBLANE * LANE)) * (SUBLANE * LANE)

    def pack(parts, last):
        flat = jnp.concatenate([p.reshape(-1).astype(F32) for p in parts])
        flat = jnp.pad(flat, (0, padded - flat.shape[0]))
        if last is not None:
            flat = jnp.concatenate([flat, last.reshape(-1)])
        return flat.reshape(-1, LANE)

    cw_zero = jnp.zeros_like(d_conv_w_full)
    early_pk = pack([small[n] for n in replicated[:-1]] + [d_conv_w_full], None)
    early_land = lax.dynamic_update_slice(jnp.zeros((NDEV,) + early_pk.shape, F32), early_pk[None], (me, 0, 0))
    small_begun = _split_start("gather_small_start", [early_pk], [early_land], _everyone_copies, NDEV - 1)

    ffn1_dw, ffn1_du = _ffn_bwd("ffn1", dh1_half, u1, ffn1_saved, wg1, wu1, wd1)
    du1 = ffn1_du(deps=[small_begun[4]])
    dx, _, d_ffn1_norm = _rms_bwd("rms1_bwd", du1, xh, r1, row(ffn1_norm), dh1, 1.0)
    (late_all,) = _all_gather("gather_ffn1_norm_grad", [d_ffn1_norm.reshape(-1, LANE)])
    rs_ffn1 = rs_begin("ffn1", list(ffn1_dw(deps=[late_all])))
    rs_end("ffn2", sharded[8:11], rs_ffn2, rs_ffn1[4])
    rs_end("mixer", sharded[3:8], rs_mixer, grads["ffn2_w_down"])
    _, (early_all,) = _split_wait("gather_small_wait", small_begun, _everyone_copies, grads["w_o"])
    small_all = jnp.concatenate([early_all, late_all], axis=1)
    g_pk, d_pk, m_pk, v_pk = _finish_replicated(
        "adamw_replicated", small_all, pack([P[n] for n in replicated[:-1]] + [cw_zero], P["ffn1_norm"]),
        pack([M[n] for n in replicated[:-1]] + [cw_zero], M["ffn1_norm"]),
        pack([V[n] for n in replicated[:-1]] + [cw_zero + 1.0], V["ffn1_norm"]))
    off = 0
    for n, sz in zip(replicated, sizes):
        at = padded if n == "ffn1_norm" else off
        for store, pk in ((grads, g_pk), (deltas, d_pk), (new_m, m_pk), (new_v, v_pk)):
            store[n] = pk.reshape(-1)[at:at + sz].reshape(P[n].shape)
        off += sz
    off -= sizes[-1]
    g_cw_full = g_pk.reshape(-1)[off:off + d_conv_w_full.size].reshape(d_conv_w_full.shape)
    cwl = conv_w.shape[1]
    g_cw = lax.dynamic_slice_in_dim(g_cw_full, me * cwl, cwl, axis=1)
    full3 = ((3, cwl), lambda i: (0, 0))
    grads["conv_w"], deltas["conv_w"], new_m["conv_w"], new_v["conv_w"] = _ew(
        "adamw_conv_w", (1,), [(g_cw,) + full3, (conv_w,) + full3, (m_conv_w,) + full3, (v_conv_w,) + full3],
        [((3, cwl), F32) + full3] * 4, lambda g, w, m, v: (g,) + _adamw(w, g, m, v))
    rs_end("ffn1", sharded[0:3], rs_ffn1, g_pk)

    return (loss, dx.reshape(x.shape), *[grads[n] for n in names], *[deltas[n] for n in names],
            *[new_m[n] for n in names], *[new_v[n] for n in names])
```

```python
import math

import jax
import jax.numpy as jnp
from jax import lax
from jax.experimental import pallas as pl
from jax.experimental.pallas import tpu as pltpu

F32 = jnp.float32
BF = jnp.bfloat16
I32 = jnp.int32
MESH = pl.DeviceIdType.MESH
LANE = 128
SUBLANE = 8
NDEV = 8
M_TILE = 1024
K_TILE = 2048
DMA_CHUNKS = 4
EPS = 1e-6
ADAM_LR, ADAM_B1, ADAM_B2, ADAM_EPS, ADAM_WD, ADAM_STEP = 0.001, 0.9, 0.999, 1e-08, 0.01, 10
NN = ((1,), (0,))
NT = ((1,), (1,))
TN = ((0,), (0,))
HBM = pl.BlockSpec(memory_space=pltpu.HBM)


def _pick(n, pref, mult):
    t = min(pref, n)
    t -= t % mult
    while t >= mult:
        if n % t == 0:
            return t
        t -= mult
    return n


def _sigmoid(x):
    return 1.0 / (1.0 + jnp.exp(-x))


_GELU_C = math.sqrt(2.0 / math.pi)


def _gelu(x):
    return 0.5 * x * (1.0 + jnp.tanh(_GELU_C * (x + 0.044715 * x * x * x)))


def _gelu_grad(x):
    t = jnp.tanh(_GELU_C * (x + 0.044715 * x * x * x))
    return 0.5 * (1.0 + t) + 0.5 * x * (1.0 - t * t) * _GELU_C * (1.0 + 3.0 * 0.044715 * x * x)


def _dep_specs(deps, rank):
    return [(d, d.shape, lambda *_, nd=d.ndim: (0,) * nd) for d in deps]


def _mm(name, grid, ops, pairs, sides, outs, epilogue, acc_shapes, deps=()):
    nk = grid[-1]
    n_ops, n_sides, n_outs = len(ops), len(sides), len(outs)
    dep_specs = _dep_specs(deps, len(grid))
    n_deps = len(dep_specs)

    def body(*refs):
        op_refs = refs[:n_ops]
        side_refs = refs[n_ops:n_ops + n_sides]
        out_refs = refs[n_ops + n_sides + n_deps:n_ops + n_sides + n_deps + n_outs]
        acc_refs = refs[n_ops + n_sides + n_deps + n_outs:]

        def partials():
            res = [None] * len(acc_shapes)
            for ia, ib, dims, ai in pairs:
                p = lax.dot_general(op_refs[ia][...], op_refs[ib][...], (dims, ((), ())),
                                    preferred_element_type=F32)
                res[ai] = p if res[ai] is None else res[ai] + p
            return res

        def finish(accs):
            vals = epilogue(accs, [s[...] for s in side_refs])
            for o, v in zip(out_refs, vals):
                o[...] = v.astype(o.dtype)

        if nk == 1:
            finish(partials())
        else:
            k = pl.program_id(len(grid) - 1)

            @pl.when(k == 0)
            def _():
                for a, p in zip(acc_refs, partials()):
                    a[...] = p

            @pl.when(k > 0)
            def _():
                for a, p in zip(acc_refs, partials()):
                    a[...] += p

            @pl.when(k == nk - 1)
            def _():
                finish([a[...] for a in acc_refs])

    return pl.pallas_call(
        body, name=name, grid=grid,
        in_specs=[pl.BlockSpec(b, m) for (_, b, m) in list(ops) + list(sides) + dep_specs],
        out_specs=[pl.BlockSpec(b, m) for (_, _, b, m) in outs],
        out_shape=[jax.ShapeDtypeStruct(s, d) for (s, d, _, _) in outs],
        scratch_shapes=[pltpu.VMEM(s, F32) for s in acc_shapes] if nk > 1 else [],
        compiler_params=pltpu.CompilerParams(
            dimension_semantics=("parallel",) * (len(grid) - 1) + ("arbitrary",)),
    )(*[a for (a, _, _) in list(ops) + list(sides) + dep_specs])


def _ew(name, grid, ins, outs, fn, acc=(), deps=()):
    n_in = len(ins)
    dep_specs = _dep_specs(deps, len(grid))

    def body(*refs):
        vals = fn(*[r[...] for r in refs[:n_in]])
        first = pl.program_id(0) == 0
        for idx, (o, v) in enumerate(zip(refs[n_in + len(dep_specs):], vals)):
            if idx in acc:
                @pl.when(first)
                def _(o=o, v=v):
                    o[...] = v.astype(o.dtype)

                @pl.when(jnp.logical_not(first))
                def _(o=o, v=v):
                    o[...] += v.astype(o.dtype)
            else:
                o[...] = v.astype(o.dtype)

    return pl.pallas_call(
        body, name=name, grid=grid,
        in_specs=[pl.BlockSpec(b, m) for (_, b, m) in list(ins) + dep_specs],
        out_specs=[pl.BlockSpec(b, m) for (_, _, b, m) in outs],
        out_shape=[jax.ShapeDtypeStruct(s, d) for (s, d, _, _) in outs],
        compiler_params=pltpu.CompilerParams(
            dimension_semantics=(("arbitrary",) if acc else ("parallel",)) * len(grid)),
    )(*[a for (a, _, _) in list(ins) + dep_specs])


def _position():
    x, y, c = lax.axis_index("x"), lax.axis_index("y"), lax.axis_index("c")
    chips = [(1 - x, y), (x, 1 - y), (1 - x, 1 - y)]
    return x, y, c, chips


def _all_gather(name, shards):
    n = len(shards)

    def body(*refs):
        xs, outs = refs[:n], refs[n:2 * n]
        send_sems, recv_sems, local_sems = refs[2 * n:]
        x, y, c, chips = _position()
        me, sibling = (x, y, c), (x, y, 1 - c)

        def copy(a, k, block, to, src=None):
            dst = outs[a].at[4 * block[0] + 2 * block[1] + block[2]]
            return pltpu.make_async_remote_copy(
                src_ref=dst if src is None else src, dst_ref=dst,
                send_sem=send_sems.at[a, k], recv_sem=recv_sems.at[a, k],
                device_id=to, device_id_type=MESH)

        mine = [pltpu.make_async_copy(xs[a], outs[a].at[4 * x + 2 * y + c], local_sems.at[a]) for a in range(n)]
        for cp in mine:
            cp.start()
        first = []
        for a in range(n):
            first.append(copy(a, 0, me, sibling, src=xs[a]))
            first += [copy(a, 1 + j, me, (*chip, c), src=xs[a]) for j, chip in enumerate(chips)]
        for cp in first:
            cp.start()
        passed = []
        for a in range(n):
            for j, chip in enumerate(chips):
                copy(a, 1 + j, (*chip, c), me).wait_recv()
                cp = copy(a, 4 + j, (*chip, c), sibling)
                cp.start()
                passed.append(cp)
        for a in range(n):
            copy(a, 0, sibling, me).wait_recv()
            for j, chip in enumerate(chips):
                copy(a, 4 + j, (*chip, 1 - c), me).wait_recv()
        for cp in first + passed:
            cp.wait_send()
        for cp in mine:
            cp.wait()

    return pl.pallas_call(
        body, name=name,
        out_shape=[jax.ShapeDtypeStruct((NDEV,) + s.shape, s.dtype) for s in shards],
        in_specs=[HBM] * n, out_specs=[HBM] * n,
        scratch_shapes=[pltpu.SemaphoreType.DMA((n, 7)), pltpu.SemaphoreType.DMA((n, 7)),
                        pltpu.SemaphoreType.DMA((n,))],
    )(*shards)


SEM = pl.BlockSpec(memory_space=pltpu.SEMAPHORE)
EFFECT = pltpu.SideEffectType.DATAFLOW_SIDE_EFFECTING


def _in_hbm(v):
    return pltpu.with_memory_space_constraint(v, pltpu.HBM)


def _split_start(name, srcs, lands, make_copies, n_per, after=()):
    n = len(srcs)
    after = list(after)

    def body(*refs):
        send_sems, recv_sems = refs[2 * n + len(after)], refs[2 * n + len(after) + 1]
        for cp in make_copies(refs[:n], refs[n:2 * n], send_sems, recv_sems):
            cp.start()
        refs[-1][...] = jnp.zeros_like(refs[-1])

    outs = pl.pallas_call(
        body, name=name,
        out_shape=(pltpu.SemaphoreType.DMA((n * n_per,)), pltpu.SemaphoreType.DMA((n * n_per,)),
                   *[pltpu.HBM(v.shape, v.dtype) for v in list(srcs) + list(lands)],
                   jax.ShapeDtypeStruct((SUBLANE, LANE), F32)),
        in_specs=[HBM] * (2 * n) + [pl.BlockSpec(memory_space=pl.ANY)] * len(after),
        out_specs=(SEM, SEM, *[HBM] * (2 * n), pl.BlockSpec(memory_space=pltpu.VMEM)),
        input_output_aliases={i: 2 + i for i in range(2 * n)},
        compiler_params=pltpu.CompilerParams(has_side_effects=EFFECT),
    )(*[_in_hbm(v) for v in list(srcs) + list(lands)], *after)
    return outs[0], outs[1], list(outs[2:2 + n]), list(outs[2 + n:2 + 2 * n]), outs[-1]


def _split_wait(name, started, make_copies, after):
    send_sems, recv_sems, srcs, lands, _ = started
    n = len(srcs)

    def body(*refs):
        for cp in make_copies(refs[:n], refs[n:2 * n], refs[2 * n], refs[2 * n + 1]):
            cp.wait_send()
            cp.wait_recv()

    order = [] if after is None else list(after) if isinstance(after, (list, tuple)) else [after]
    outs = pl.pallas_call(
        body, name=name,
        out_shape=tuple(pltpu.HBM(v.shape, v.dtype) for v in srcs + lands),
        in_specs=[HBM] * (2 * n) + [SEM, SEM] + [pl.BlockSpec(memory_space=pl.ANY)] * len(order),
        out_specs=tuple([HBM] * (2 * n)),
        input_output_aliases={i: i for i in range(2 * n)},
        compiler_params=pltpu.CompilerParams(has_side_effects=EFFECT),
    )(*srcs, *lands, send_sems, recv_sems, *order)
    return list(outs[:n]), list(outs[n:])


def _gather_copies(xs, lands, send_sems, recv_sems):
    x, y, c, chips = _position()
    copies = []
    for a in range(len(xs)):
        for k, peer in enumerate([(x, y, 1 - c)] + [(*chip, c) for chip in chips]):
            copies.append(pltpu.make_async_remote_copy(
                src_ref=xs[a], dst_ref=lands[a].at[4 * x + 2 * y + c],
                send_sem=send_sems.at[4 * a + k], recv_sem=recv_sems.at[4 * a + k], device_id=peer, device_id_type=MESH))
    return copies


def _chips_copies(ps, lands, send_sems, recv_sems):
    x, y, c, chips = _position()
    copies = []
    for a in range(len(ps)):
        for j, chip in enumerate(chips):
            copies.append(pltpu.make_async_remote_copy(
                src_ref=ps[a].at[2 * chip[0] + chip[1]], dst_ref=lands[a].at[j],
                send_sem=send_sems.at[3 * a + j], recv_sem=recv_sems.at[3 * a + j], device_id=(*chip, c),
                device_id_type=MESH))
    return copies


def _sibling_copies(gs, lands, send_sems, recv_sems):
    x, y, c, _ = _position()
    copies = []
    for a in range(len(gs)):
        for q in range(4):
            copies.append(pltpu.make_async_remote_copy(
                src_ref=gs[a].at[2 * q + 1 - c], dst_ref=lands[a].at[q],
                send_sem=send_sems.at[4 * a + q], recv_sem=recv_sems.at[4 * a + q],
                device_id=(x, y, 1 - c), device_id_type=MESH))
    return copies


def _everyone_copies(xs, lands, send_sems, recv_sems):
    x, y, c, _ = _position()
    flip = lambda v, bit: 1 - v if bit else v
    copies = []
    for a in range(len(xs)):
        for k in range(1, NDEV):
            copies.append(pltpu.make_async_remote_copy(
                src_ref=xs[a], dst_ref=lands[a].at[4 * x + 2 * y + c],
                send_sem=send_sems.at[7 * a + k - 1], recv_sem=recv_sems.at[7 * a + k - 1],
                device_id=(flip(x, k & 4), flip(y, k & 2), flip(c, k & 1)), device_id_type=MESH))
    return copies


def _row_chunks(rows, dtype):
    unit = SUBLANE * (4 // jnp.dtype(dtype).itemsize)
    units = rows // unit
    if rows % unit or units < 2:
        return [(0, rows)]
    k = min(DMA_CHUNKS, units)
    sizes = [(units // k + (1 if i < units % k else 0)) * unit for i in range(k)]
    return [(sum(sizes[:i]), sz) for i, sz in enumerate(sizes)]


def _gather_forward(name, lands):
    n = len(lands)

    def body(*refs):
        ins, outs = refs[:n], refs[n:2 * n]
        send_sems, recv_sems = refs[2 * n:]
        x, y, c, chips = _position()
        whole, chunks = [], []
        for a in range(n):
            rows = _row_chunks(ins[a].shape[1], ins[a].dtype)
            for j, chip in enumerate(chips):
                slot = 4 * chip[0] + 2 * chip[1]

                def to_sibling(src, dst):
                    return pltpu.make_async_remote_copy(
                        src_ref=src, dst_ref=dst, send_sem=send_sems.at[a, j], recv_sem=recv_sems.at[a, j],
                        device_id=(x, y, 1 - c), device_id_type=MESH)

                whole.append(to_sibling(ins[a].at[slot + c], outs[a].at[slot + 1 - c]))
                chunks += [to_sibling(ins[a].at[slot + c, pl.ds(r0, nr)], outs[a].at[slot + c, pl.ds(r0, nr)])
                           for r0, nr in rows]
        for cp in chunks:
            cp.start()
        for cp in whole:
            cp.wait()

    return pl.pallas_call(
        body, name=name,
        out_shape=[jax.ShapeDtypeStruct(l.shape, l.dtype) for l in lands],
        in_specs=[HBM] * n, out_specs=[HBM] * n,
        input_output_aliases={a: a for a in range(n)},
        scratch_shapes=[pltpu.SemaphoreType.DMA((n, 3)), pltpu.SemaphoreType.DMA((n, 3))],
    )(*lands)


def _exchange_sibling(name, grads):
    n = len(grads)

    def body(*refs):
        gs, lands = refs[:n], refs[n:2 * n]
        send_sems, recv_sems = refs[2 * n:]
        x, y, c, _ = _position()
        copies = []
        for a in range(n):
            for q in range(4):
                copies.append(pltpu.make_async_remote_copy(
                    src_ref=gs[a].at[2 * q + 1 - c], dst_ref=lands[a].at[q],
                    send_sem=send_sems.at[a, q], recv_sem=recv_sems.at[a, q],
                    device_id=(x, y, 1 - c), device_id_type=MESH))
        for cp in copies:
            cp.start()
        for cp in copies:
            cp.wait()

    return pl.pallas_call(
        body, name=name,
        out_shape=[jax.ShapeDtypeStruct((4,) + g.shape[1:], g.dtype) for g in grads],
        in_specs=[HBM] * n, out_specs=[HBM] * n,
        scratch_shapes=[pltpu.SemaphoreType.DMA((n, 4)), pltpu.SemaphoreType.DMA((n, 4))],
    )(*grads)


def _sum_sibling(name, g, land, c_arr):
    _, R, C = g.shape
    tr = _pick(R, 512, SUBLANE)

    def body(c_ref, g_ref, l_ref, o_ref):
        o_ref[...] = (g_ref[...].astype(F32) + l_ref[...].astype(F32)).astype(o_ref.dtype)

    return pl.pallas_call(
        body, name=name,
        grid_spec=pltpu.PrefetchScalarGridSpec(
            num_scalar_prefetch=1, grid=(4, R // tr),
            in_specs=[pl.BlockSpec((None, tr, C), lambda q, i, cr: (2 * q + cr[0], i, 0)),
                      pl.BlockSpec((None, tr, C), lambda q, i, cr: (q, i, 0))],
            out_specs=pl.BlockSpec((None, tr, C), lambda q, i, cr: (q, i, 0))),
        out_shape=jax.ShapeDtypeStruct((4, R, C), g.dtype),
        compiler_params=pltpu.CompilerParams(dimension_semantics=("parallel", "parallel")),
    )(c_arr, g, land)


def _adamw(w, g, m, v):
    m = ADAM_B1 * m + (1.0 - ADAM_B1) * g
    v = ADAM_B2 * v + (1.0 - ADAM_B2) * (g * g)
    m_hat = m / (1.0 - ADAM_B1 ** ADAM_STEP)
    v_hat = v / (1.0 - ADAM_B2 ** ADAM_STEP)
    delta = -ADAM_LR * (m_hat / (jnp.sqrt(v_hat) + ADAM_EPS) + ADAM_WD * w)
    return delta, m, v


def _finish_sharded(name, sums, land, q_arr, w, m, v):
    R, C = w.shape
    tr = _pick(R, 256, SUBLANE)

    def body(q_ref, p_ref, l_ref, w_ref, m_ref, v_ref, g_out, d_out, m_out, v_out):
        g = p_ref[...].astype(F32)
        for j in range(3):
            g = g + l_ref[j].astype(F32)
        d, mn, vn = _adamw(w_ref[...], g, m_ref[...], v_ref[...])
        g_out[...] = g
        d_out[...] = d
        m_out[...] = mn
        v_out[...] = vn

    row = pl.BlockSpec((tr, C), lambda i, qr: (i, 0))
    return pl.pallas_call(
        body, name=name,
        grid_spec=pltpu.PrefetchScalarGridSpec(
            num_scalar_prefetch=1, grid=(R // tr,),
            in_specs=[pl.BlockSpec((None, tr, C), lambda i, qr: (qr[0], i, 0)),
                      pl.BlockSpec((3, tr, C), lambda i, qr: (0, i, 0)), row, row, row],
            out_specs=[row] * 4),
        out_shape=[jax.ShapeDtypeStruct((R, C), F32)] * 4,
        compiler_params=pltpu.CompilerParams(dimension_semantics=("parallel",)),
    )(q_arr, sums, land, w, m, v)


def _finish_replicated(name, gathered, w, m, v):
    _, R, C = gathered.shape
    tr = _pick(R, 256, SUBLANE)

    def fn(gv, wv, mv, vv):
        g = gv[0]
        for d in range(1, NDEV):
            g = g + gv[d]
        dl, mn, vn = _adamw(wv, g, mv, vv)
        return g, dl, mn, vn

    row = ((tr, C), lambda i: (i, 0))
    return _ew(name, (R // tr,),
               [(gathered, (NDEV, tr, C), lambda i: (0, i, 0)), (w,) + row, (m,) + row, (v,) + row],
               [((R, C), F32) + row] * 4, fn)


def _rms_fwd(name, h, g, deps=()):
    S, D = h.shape
    tr = _pick(S, 256, SUBLANE)

    def fn(hv, gv):
        r = lax.rsqrt(jnp.mean(hv * hv, axis=-1, keepdims=True) + EPS)
        return hv * r * gv, r

    return _ew(name, (S // tr,),
               [(h, (tr, D), lambda i: (i, 0)), (g, (1, D), lambda i: (0, 0))],
               [((S, D), BF, (tr, D), lambda i: (i, 0)), ((S, 1), F32, (tr, 1), lambda i: (i, 0))], fn, deps=deps)


def _rms_bwd(name, du, h, r, g, dres, scale):
    S, D = h.shape
    tr = _pick(S, 256, SUBLANE)

    def fn(duv, hv, rv, gv, drv):
        xn = hv * rv
        dxn = duv * gv
        dh = drv + rv * (dxn - xn * jnp.mean(dxn * xn, axis=-1, keepdims=True))
        return dh, scale * dh, jnp.sum(duv * xn, axis=0, keepdims=True)

    row = ((tr, D), lambda i: (i, 0))
    return _ew(name, (S // tr,),
               [(du,) + row, (h,) + row, (r, (tr, 1), lambda i: (i, 0)), (g, (1, D), lambda i: (0, 0)), (dres,) + row],
               [((S, D), F32) + row, ((S, D), BF) + row, ((1, D), F32, (1, D), lambda i: (0, 0))], fn, acc=(2,))


def _loss_head(name, h, g, target):
    S, D = h.shape
    tr = _pick(S, 256, SUBLANE)

    def fn(hv, gv, tv):
        r = lax.rsqrt(jnp.mean(hv * hv, axis=-1, keepdims=True) + EPS)
        xn = hv * r
        diff = xn * gv - tv
        loss = 0.5 * jnp.sum(jnp.mean(diff * diff, axis=-1, keepdims=True))
        dout = diff / D
        dxn = dout * gv
        dh = r * (dxn - xn * jnp.mean(dxn * xn, axis=-1, keepdims=True))
        return (jnp.zeros((1, LANE), F32) + loss, dh, 0.5 * dh, jnp.sum(dout * xn, axis=0, keepdims=True))

    row = ((tr, D), lambda i: (i, 0))
    return _ew(name, (S // tr,),
               [(h,) + row, (g, (1, D), lambda i: (0, 0)), (target,) + row],
               [((1, LANE), F32, (1, LANE), lambda i: (0, 0)), ((S, D), F32) + row, ((S, D), BF) + row,
                ((1, D), F32, (1, D), lambda i: (0, 0))], fn, acc=(0, 3))


def _ffn_fwd(tag, u, h, wg, wu, wd, deps=()):
    S, D = u.shape
    Fs = wg.shape[1]
    tm, tk = _pick(S, M_TILE, SUBLANE), _pick(D, K_TILE, LANE)

    def up_epilogue(accs, sides):
        gt, up = accs
        return gt, up, gt * _sigmoid(gt) * up

    act = ((NDEV, S, Fs), BF, (None, tm, Fs), lambda b, i, j, k: (b, i, 0))
    gt, up, a = _mm(
        tag + "_up", (NDEV, S // tm, 1, D // tk),
        [(u, (tm, tk), lambda b, i, j, k: (i, k)),
         (wg, (None, Fs, tk), lambda b, i, j, k: (b, 0, k)), (wu, (None, Fs, tk), lambda b, i, j, k: (b, 0, k))],
        [(0, 1, NT, 0), (0, 2, NT, 1)], [], [act, act, act], up_epilogue, [(tm, Fs), (tm, Fs)], deps=deps)
    if callable(wd):
        wd = wd(a)
    tn = _pick(D, 1024, LANE)
    (hn,) = _mm(
        tag + "_down", (1, S // tm, D // tn, NDEV),
        [(a, (None, tm, Fs), lambda b, i, j, k: (k, i, 0)), (wd, (None, Fs, tn), lambda b, i, j, k: (k, 0, j))],
        [(0, 1, NN, 0)], [(h, (tm, tn), lambda b, i, j, k: (i, j))],
        [((S, D), F32, (tm, tn), lambda b, i, j, k: (i, j))],
        lambda accs, sides: [sides[0] + 0.5 * accs[0]], [(tm, tn)])
    return hn, (gt, up, a), wd


def _ffn_bwd(tag, dhs, u, saved, wg, wu, wd, deps=()):
    gt, up, a = saved
    S, D = u.shape
    Fs = wg.shape[1]
    tm, tk = _pick(S, M_TILE, SUBLANE), _pick(D, K_TILE, LANE)
    act_in = lambda arr: (arr, (None, tm, Fs), lambda b, i, j, k: (b, i, 0))
    act_out = ((NDEV, S, Fs), BF, (None, tm, Fs), lambda b, i, j, k: (b, i, 0))

    def act_epilogue(accs, sides):
        da = accs[0]
        gtv, upv = sides[0].astype(F32), sides[1].astype(F32)
        sg = _sigmoid(gtv)
        return da * upv * sg * (1.0 + gtv * (1.0 - sg)), da * gtv * sg

    dgt, dup = _mm(
        tag + "_dact", (NDEV, S // tm, 1, D // tk),
        [(dhs, (tm, tk), lambda b, i, j, k: (i, k)), (wd, (None, Fs, tk), lambda b, i, j, k: (b, 0, k))],
        [(0, 1, NT, 0)], [act_in(gt), act_in(up)], [act_out, act_out], act_epilogue, [(tm, Fs)], deps=deps)

    ts = _pick(S, K_TILE, SUBLANE)
    tn = _pick(D, 1024, LANE)
    wgrad = ((NDEV, Fs, D), BF, (None, Fs, tn), lambda b, i, j, k: (b, 0, j))
    tok = lambda arr: (arr, (None, ts, Fs), lambda b, i, j, k: (b, k, 0))

    def grad_down(deps=()):
        return _mm(
            tag + "_dwd", (NDEV, 1, D // tn, S // ts),
            [tok(a), (dhs, (ts, tn), lambda b, i, j, k: (k, j))],
            [(0, 1, TN, 0)], [], [wgrad], lambda accs, sides: accs, [(Fs, tn)], deps=deps)[0]

    def grad_gate_up(deps=()):
        return _mm(
            tag + "_dwgu", (NDEV, 1, D // tn, S // ts),
            [tok(dgt), tok(dup), (u, (ts, tn), lambda b, i, j, k: (k, j))],
            [(0, 2, TN, 0), (1, 2, TN, 1)], [], [wgrad, wgrad], lambda accs, sides: accs, [(Fs, tn), (Fs, tn)],
            deps=deps)

    def du(deps=()):
        return _mm(
            tag + "_du", (1, S // tm, D // tn, NDEV),
            [(dgt, (None, tm, Fs), lambda b, i, j, k: (k, i, 0)), (dup, (None, tm, Fs), lambda b, i, j, k: (k, i, 0)),
             (wg, (None, Fs, tn), lambda b, i, j, k: (k, 0, j)), (wu, (None, Fs, tn), lambda b, i, j, k: (k, 0, j))],
            [(0, 2, NN, 0), (1, 3, NN, 0)], [], [((S, D), F32, (tm, tn), lambda b, i, j, k: (i, j))],
            lambda accs, sides: accs, [(tm, tn)], deps=deps)[0]

    return grad_down, grad_gate_up, du


def _ssm_params(lam_re, lam_im, log_dt, b_re, b_im, c_re, c_im):
    G, N = lam_re.shape
    C = b_re.shape[2]
    lam_re = jnp.minimum(lam_re, -1e-4)
    dt = jnp.exp(log_dt)[:, None]
    mag = jnp.exp(lam_re * dt)
    a_re = mag * jnp.cos(lam_im * dt)
    a_im = mag * jnp.sin(lam_im * dt)
    den = lam_re * lam_re + lam_im * lam_im
    p = a_re - 1.0
    f_re = ((p * lam_re + a_im * lam_im) / den)[:, :, None]
    f_im = ((a_im * lam_re - p * lam_im) / den)[:, :, None]
    bb_re = f_re * b_re - f_im * b_im
    bb_im = f_re * b_im + f_im * b_re
    gpt = LANE // C
    tiles = G // gpt
    eye = jnp.eye(gpt, dtype=F32)

    def bd(bb):
        return jnp.einsum("bgnc,gh->bgchn", bb.reshape(tiles, gpt, N, C), eye).reshape(tiles, gpt * C, gpt * N)

    def cd(cc):
        return jnp.einsum("bgcn,gh->bgnhc", cc.reshape(tiles, gpt, C, N), eye).reshape(tiles, gpt * N, gpt * C)

    rows = G * N // LANE
    return (a_re.reshape(rows, LANE), a_im.reshape(rows, LANE), bd(bb_re), bd(bb_im), cd(c_re), cd(-c_im))


def _scan_fwd(bu_re, bu_im, a_re, a_im):
    S, R, _ = bu_re.shape
    tc = _pick(S, 256, SUBLANE)

    def body(bre, bim, are, aim, sre, sim, carry):
        @pl.when(pl.program_id(0) == 0)
        def _():
            carry[...] = jnp.zeros_like(carry)

        ar, ai = are[...], aim[...]

        def step(t, c):
            pr, pi = c
            nr = ar * pr - ai * pi + bre[t]
            ni = ar * pi + ai * pr + bim[t]
            sre[t] = nr
            sim[t] = ni
            return nr, ni

        pr, pi = lax.fori_loop(0, tc, step, (carry[0], carry[1]), unroll=8)
        carry[0] = pr
        carry[1] = pi

    blk = pl.BlockSpec((tc, R, LANE), lambda i: (i, 0, 0))
    par = pl.BlockSpec((R, LANE), lambda i: (0, 0))
    return pl.pallas_call(
        body, name="ssm_scan_fwd", grid=(S // tc,),
        in_specs=[blk, blk, par, par], out_specs=[blk, blk],
        out_shape=[jax.ShapeDtypeStruct((S, R, LANE), F32)] * 2,
        scratch_shapes=[pltpu.VMEM((2, R, LANE), F32)],
        compiler_params=pltpu.CompilerParams(dimension_semantics=("arbitrary",)),
    )(bu_re, bu_im, a_re, a_im)


def _scan_bwd(ds_re, ds_im, s_re, s_im, a_re, a_im):
    S, R, _ = ds_re.shape
    tc = _pick(S, 256, SUBLANE)
    nc = S // tc

    def body(dre, dim_, sre, sim, are, aim, lre, lim, dar, dai, carry):
        @pl.when(pl.program_id(0) == 0)
        def _():
            carry[...] = jnp.zeros_like(carry)
            dar[...] = jnp.zeros_like(dar)
            dai[...] = jnp.zeros_like(dai)

        ar, ai = are[...], aim[...]

        def step(tt, c):
            t = tc - 1 - tt
            lr, li, gr, gi = c
            sr, si = sre[t], sim[t]
            gr = gr + lr * sr + li * si
            gi = gi + li * sr - lr * si
            nlr = dre[t] + ar * lr + ai * li
            nli = dim_[t] + ar * li - ai * lr
            lre[t] = nlr
            lim[t] = nli
            return nlr, nli, gr, gi

        lr, li, gr, gi = lax.fori_loop(0, tc, step, (carry[0], carry[1], dar[...], dai[...]), unroll=8)
        carry[0] = lr
        carry[1] = li
        dar[...] = gr
        dai[...] = gi

    blk = pl.BlockSpec((tc, R, LANE), lambda i: (nc - 1 - i, 0, 0))
    par = pl.BlockSpec((R, LANE), lambda i: (0, 0))
    return pl.pallas_call(
        body, name="ssm_scan_bwd", grid=(nc,),
        in_specs=[blk, blk, blk, blk, par, par], out_specs=[blk, blk, par, par],
        out_shape=[jax.ShapeDtypeStruct((S, R, LANE), F32)] * 2 + [jax.ShapeDtypeStruct((R, LANE), F32)] * 2,
        scratch_shapes=[pltpu.VMEM((2, R, LANE), F32)],
        compiler_params=pltpu.CompilerParams(dimension_semantics=("arbitrary",)),
    )(ds_re, ds_im, s_re, s_im, a_re, a_im)


def _shift_down(z, k):
    t = lax.broadcasted_iota(I32, z.shape, 0)
    return jnp.where(t >= k, pltpu.roll(z, k, 0), 0.0)


def _shift_up(z, k):
    n = z.shape[0]
    t = lax.broadcasted_iota(I32, z.shape, 0)
    return jnp.where(t < n - k, pltpu.roll(z, n - k, 0), 0.0)


def _conv_fwd(proj, cw, cb):
    _, S, W = proj.shape
    ct = _pick(W, 256, LANE)

    def fn(bg, cg, val, w, b):
        z = cg * val
        conv = b + w[0:1] * _shift_down(z, 2) + w[1:2] * _shift_down(z, 1) + w[2:3] * z
        return bg * conv, conv

    sl = lambda s: (proj, (None, S, ct), lambda j, s=s: (s, 0, j))
    col = ((S, ct), lambda j: (0, j))
    return _ew("conv_fwd", (W // ct,),
               [sl(1), sl(2), sl(3), (cw, (3, ct), lambda j: (0, j)), (cb, (1, ct), lambda j: (0, j))],
               [((S, W), BF) + col, ((S, W), F32) + col], fn)


def _conv_bwd(dyb, proj, conv, cw):
    _, S, W = proj.shape
    ct = _pick(W, 256, LANE)

    def fn(dy, bg, cg, val, cv, w):
        z = cg * val
        z1, z2 = _shift_down(z, 1), _shift_down(z, 2)
        dconv = dy * bg
        dz = w[2:3] * dconv + w[1:2] * _shift_up(dconv, 1) + w[0:1] * _shift_up(dconv, 2)
        dw = jnp.concatenate([jnp.sum(dconv * z2, axis=0, keepdims=True), jnp.sum(dconv * z1, axis=0, keepdims=True),
                              jnp.sum(dconv * z, axis=0, keepdims=True)], axis=0)
        return dy * cv, dz * val, dz * cg, dw, jnp.sum(dconv, axis=0, keepdims=True)

    sl = lambda s: (proj, (None, S, ct), lambda j, s=s: (s, 0, j))
    col = ((S, ct), lambda j: (0, j))
    return _ew("conv_bwd", (W // ct,),
               [(dyb,) + col, sl(1), sl(2), sl(3), (conv,) + col, (cw, (3, ct), lambda j: (0, j))],
               [((S, W), BF) + col, ((S, W), BF) + col, ((S, W), BF) + col,
                ((3, W), F32, (3, ct), lambda j: (0, j)), ((1, W), F32, (1, ct), lambda j: (0, j))], fn)


def _plain(accs, sides):
    return accs


def kernel(x, ffn1_norm, ffn1_w_gate, ffn1_w_up, ffn1_w_down, mix_norm, w_in, ssm_lambda_re, ssm_lambda_im, ssm_log_dt, ssm_b_re, ssm_b_im, ssm_c_re, ssm_c_im, ssm_d, ssm_w_glu, ssm_b_glu, ssm_w_out, conv_w, conv_b, conv_w_out, w_o, ffn2_norm, ffn2_w_gate, ffn2_w_up, ffn2_w_down, final_norm, loss_target, m_ffn1_norm, m_ffn1_w_gate, m_ffn1_w_up, m_ffn1_w_down, m_mix_norm, m_w_in, m_ssm_lambda_re, m_ssm_lambda_im, m_ssm_log_dt, m_ssm_b_re, m_ssm_b_im, m_ssm_c_re, m_ssm_c_im, m_ssm_d, m_ssm_w_glu, m_ssm_b_glu, m_ssm_w_out, m_conv_w, m_conv_b, m_conv_w_out, m_w_o, m_ffn2_norm, m_ffn2_w_gate, m_ffn2_w_up, m_ffn2_w_down, m_final_norm, v_ffn1_norm, v_ffn1_w_gate, v_ffn1_w_up, v_ffn1_w_down, v_mix_norm, v_w_in, v_ssm_lambda_re, v_ssm_lambda_im, v_ssm_log_dt, v_ssm_b_re, v_ssm_b_im, v_ssm_c_re, v_ssm_c_im, v_ssm_d, v_ssm_w_glu, v_ssm_b_glu, v_ssm_w_out, v_conv_w, v_conv_b, v_conv_w_out, v_w_o, v_ffn2_norm, v_ffn2_w_gate, v_ffn2_w_up, v_ffn2_w_down, v_final_norm):
    P = dict(ffn1_norm=ffn1_norm, ffn1_w_gate=ffn1_w_gate, ffn1_w_up=ffn1_w_up, ffn1_w_down=ffn1_w_down, mix_norm=mix_norm, w_in=w_in, ssm_lambda_re=ssm_lambda_re, ssm_lambda_im=ssm_lambda_im, ssm_log_dt=ssm_log_dt, ssm_b_re=ssm_b_re, ssm_b_im=ssm_b_im, ssm_c_re=ssm_c_re, ssm_c_im=ssm_c_im, ssm_d=ssm_d, ssm_w_glu=ssm_w_glu, ssm_b_glu=ssm_b_glu, ssm_w_out=ssm_w_out, conv_w=conv_w, conv_b=conv_b, conv_w_out=conv_w_out, w_o=w_o, ffn2_norm=ffn2_norm, ffn2_w_gate=ffn2_w_gate, ffn2_w_up=ffn2_w_up, ffn2_w_down=ffn2_w_down, final_norm=final_norm)
    M = dict(ffn1_norm=m_ffn1_norm, ffn1_w_gate=m_ffn1_w_gate, ffn1_w_up=m_ffn1_w_up, ffn1_w_down=m_ffn1_w_down, mix_norm=m_mix_norm, w_in=m_w_in, ssm_lambda_re=m_ssm_lambda_re, ssm_lambda_im=m_ssm_lambda_im, ssm_log_dt=m_ssm_log_dt, ssm_b_re=m_ssm_b_re, ssm_b_im=m_ssm_b_im, ssm_c_re=m_ssm_c_re, ssm_c_im=m_ssm_c_im, ssm_d=m_ssm_d, ssm_w_glu=m_ssm_w_glu, ssm_b_glu=m_ssm_b_glu, ssm_w_out=m_ssm_w_out, conv_w=m_conv_w, conv_b=m_conv_b, conv_w_out=m_conv_w_out, w_o=m_w_o, ffn2_norm=m_ffn2_norm, ffn2_w_gate=m_ffn2_w_gate, ffn2_w_up=m_ffn2_w_up, ffn2_w_down=m_ffn2_w_down, final_norm=m_final_norm)
    V = dict(ffn1_norm=v_ffn1_norm, ffn1_w_gate=v_ffn1_w_gate, ffn1_w_up=v_ffn1_w_up, ffn1_w_down=v_ffn1_w_down, mix_norm=v_mix_norm, w_in=v_w_in, ssm_lambda_re=v_ssm_lambda_re, ssm_lambda_im=v_ssm_lambda_im, ssm_log_dt=v_ssm_log_dt, ssm_b_re=v_ssm_b_re, ssm_b_im=v_ssm_b_im, ssm_c_re=v_ssm_c_re, ssm_c_im=v_ssm_c_im, ssm_d=v_ssm_d, ssm_w_glu=v_ssm_w_glu, ssm_b_glu=v_ssm_b_glu, ssm_w_out=v_ssm_w_out, conv_w=v_conv_w, conv_b=v_conv_b, conv_w_out=v_conv_w_out, w_o=v_w_o, ffn2_norm=v_ffn2_norm, ffn2_w_gate=v_ffn2_w_gate, ffn2_w_up=v_ffn2_w_up, ffn2_w_down=v_ffn2_w_down, final_norm=v_final_norm)
    names = list(P)
    sharded = ["ffn1_w_gate", "ffn1_w_up", "ffn1_w_down", "w_in", "ssm_w_glu", "ssm_w_out", "conv_w_out", "w_o",
               "ffn2_w_gate", "ffn2_w_up", "ffn2_w_down"]
    replicated = [n for n in names if n not in sharded and n != "conv_w"]

    S, D = x.shape[1], x.shape[2]
    W = ssm_d.shape[0]
    Dc = D // NDEV
    G, N = ssm_lambda_re.shape
    GN = G * N
    rows = GN // LANE
    xh = x.reshape(S, D)
    target = loss_target.reshape(S, D)
    xi, yi, ci = lax.axis_index("x"), lax.axis_index("y"), lax.axis_index("c")
    c_arr = jnp.reshape(ci, (1,)).astype(I32)
    q_arr = jnp.reshape(2 * xi + yi, (1,)).astype(I32)
    row = lambda v: v.reshape(1, -1)

    transposed = ("ffn1_w_gate", "ffn1_w_up", "ffn2_w_gate", "ffn2_w_up")
    local = lambda table, n: table[n].T if n in transposed else table[n]

    groups = [sharded[0:2], sharded[2:3], sharded[3:8] + ["conv_w"], sharded[8:11]]
    started = {}
    me = 4 * xi + 2 * yi + ci

    def gather_start(gi):
        srcs = [conv_w if n == "conv_w" else local(P, n).astype(BF) for n in groups[gi]]
        lands = [lax.dynamic_update_slice(lax.empty((NDEV,) + s.shape, s.dtype), s[None], (me,) + (0,) * s.ndim)
                 for s in srcs]
        started[gi] = _split_start("gather_start_%d" % gi, srcs, lands, _gather_copies, 4)
        return started[gi][4]

    def gathered(gi, after):
        _, lands = _split_wait("gather_wait_%d" % gi, started[gi], _gather_copies, after)
        return _gather_forward("gather_forward_%d" % gi, lands)

    tm = _pick(S, M_TILE, SUBLANE)
    th = _pick(S, M_TILE // 2, SUBLANE)
    tq = _pick(S, 2 * M_TILE, SUBLANE)
    tk = _pick(D, K_TILE, LANE)
    ts = _pick(S, K_TILE, SUBLANE)
    tn = _pick(D, 1024, LANE)

    tokens = [gather_start(gi) for gi in range(len(groups))]
    u1, r1 = _rms_fwd("rms1", xh, row(ffn1_norm), deps=tokens)
    wg1, wu1 = gathered(0, u1)
    h1, ffn1_saved, wd1 = _ffn_fwd("ffn1", u1, xh, wg1, wu1, lambda after: gathered(1, after)[0])
    u2, r2 = _rms_fwd("rms2", h1, row(mix_norm))
    w_in_f, w_glu_f, w_so, w_co, w_o_f, cw_f = gathered(2, u2)
    w_glu_f = w_glu_f.reshape(W, W)
    w_o_f = w_o_f.reshape(D, D)
    cw = jnp.transpose(cw_f, (1, 0, 2)).reshape(3, W)
    (proj,) = _mm(
        "in_proj", (NDEV, S // tm, 1, D // tk),
        [(u2, (tm, tk), lambda b, i, j, k: (i, k)), (w_in_f, (None, tk, W), lambda b, i, j, k: (b, k, 0))],
        [(0, 1, NN, 0)], [], [((NDEV, S, W), F32, (None, tm, W), lambda b, i, j, k: (b, i, 0))], _plain, [(tm, W)])

    ssm_in = (ssm_lambda_re, ssm_lambda_im, ssm_log_dt, ssm_b_re, ssm_b_im, ssm_c_re, ssm_c_im)
    (a_re, a_im, bd_re, bd_im, cd_re, cd_imn), ssm_vjp = jax.vjp(_ssm_params, *ssm_in)
    tiles, tch, tst = bd_re.shape
    bd_re_b, bd_im_b, cd_re_b, cd_imn_b = (t.astype(BF) for t in (bd_re, bd_im, cd_re, cd_imn))
    v_bf = proj[0].astype(BF)
    st_blk = lambda arr: (arr, (tq, tst), lambda b, i, j, k: (i, b))
    ch_blk = lambda arr: (arr, (tq, tch), lambda b, i, j, k: (i, b))
    bd_blk = lambda arr: (arr, (None, tch, tst), lambda b, i, j, k: (b, 0, 0))
    cd_blk = lambda arr: (arr, (None, tst, tch), lambda b, i, j, k: (b, 0, 0))
    st_out = lambda dt: ((S, GN), dt, (tq, tst), lambda b, i, j, k: (i, b))
    ch_out = lambda dt: ((S, W), dt, (tq, tch), lambda b, i, j, k: (i, b))
    d_blk = (row(ssm_d), (1, tch), lambda b, i, j, k: (0, b))
    v_blk = (proj, (None, tq, tch), lambda b, i, j, k: (0, i, b))

    bu_re, bu_im = _mm("ssm_bu", (tiles, S // tq, 1, 1), [ch_blk(v_bf), bd_blk(bd_re_b), bd_blk(bd_im_b)],
                       [(0, 1, NN, 0), (0, 2, NN, 1)], [], [st_out(F32), st_out(F32)], _plain, [(tq, tst)] * 2)
    s_re3, s_im3 = _scan_fwd(bu_re.reshape(S, rows, LANE), bu_im.reshape(S, rows, LANE), a_re, a_im)
    s_re_b = s_re3.reshape(S, GN).astype(BF)
    s_im_b = s_im3.reshape(S, GN).astype(BF)

    def y0_epilogue(accs, sides):
        y0 = accs[0] + sides[1] * sides[0]
        return y0, _gelu(y0)

    y0, y1 = _mm("ssm_y0", (tiles, S // tq, 1, 1),
                 [st_blk(s_re_b), st_blk(s_im_b), cd_blk(cd_re_b), cd_blk(cd_imn_b)],
                 [(0, 2, NN, 0), (1, 3, NN, 0)], [v_blk, d_blk], [ch_out(F32), ch_out(BF)], y0_epilogue, [(tq, tch)])

    tw = _pick(W, 512, LANE)

    def glu_epilogue(accs, sides):
        q = accs[0] + sides[1]
        return q, _gelu(sides[0]) * _sigmoid(q)

    q_pre, y2 = _mm("ssm_glu", (1, S // tm, W // tw, 1),
                    [(y1, (tm, W), lambda b, i, j, k: (i, 0)), (w_glu_f, (W, tw), lambda b, i, j, k: (0, j))],
                    [(0, 1, NN, 0)],
                    [(y0, (tm, tw), lambda b, i, j, k: (i, j)), (row(ssm_b_glu), (1, tw), lambda b, i, j, k: (0, j))],
                    [((S, W), F32, (tm, tw), lambda b, i, j, k: (i, j)), ((S, W), BF, (tm, tw), lambda b, i, j, k: (i, j))],
                    glu_epilogue, [(tm, tw)])

    yb, conv = _conv_fwd(proj, cw, row(conv_b))

    per = W // Dc
    ga_blk = (proj, (None, tm, Dc), lambda b, i, j, k: (4 + b // per, i, b % per))
    gb_blk = (proj, (None, tm, Dc), lambda b, i, j, k: (6 + b // per, i, b % per))
    dc_out = ((S, D), BF, (tm, Dc), lambda b, i, j, k: (i, b))

    def merge_epilogue(accs, sides):
        za, zb = accs
        return _sigmoid(sides[0]) * za + _sigmoid(sides[1]) * zb, za, zb

    merged, z_a, z_b = _mm(
        "mix_merge", (NDEV, S // tm, 1, 1),
        [(y2, (tm, W), lambda b, i, j, k: (i, 0)), (yb, (tm, W), lambda b, i, j, k: (i, 0)),
         (w_so, (None, W, Dc), lambda b, i, j, k: (b, 0, 0)), (w_co, (None, W, Dc), lambda b, i, j, k: (b, 0, 0))],
        [(0, 2, NN, 0), (1, 3, NN, 1)], [ga_blk, gb_blk], [dc_out, dc_out, dc_out], merge_epilogue, [(tm, Dc)] * 2)

    (h2,) = _mm("mix_out", (1, S // tm, D // tn, D // tk),
                [(merged, (tm, tk), lambda b, i, j, k: (i, k)), (w_o_f, (tk, tn), lambda b, i, j, k: (k, j))],
                [(0, 1, NN, 0)], [(h1, (tm, tn), lambda b, i, j, k: (i, j))],
                [((S, D), F32, (tm, tn), lambda b, i, j, k: (i, j))],
                lambda accs, sides: [sides[0] + accs[0]], [(tm, tn)])

    u3, r3 = _rms_fwd("rms3", h2, row(ffn2_norm))
    wg2, wu2, wd2 = gathered(3, u3)
    h3, ffn2_saved, _ = _ffn_fwd("ffn2", u3, h2, wg2, wu2, wd2)
    loss_vec, dh3, dh3_half, d_final_norm = _loss_head("loss_head", h3, row(final_norm), target)
    loss = lax.psum(loss_vec[0, 0], ("x", "y", "c"))

    grads, deltas, new_m, new_v = {}, {}, {}, {}

    def rs_sibling_start(tag, parts):
        lands = [lax.empty((4,) + p.shape[1:], p.dtype) for p in parts]
        return _split_start("rs_sibling_start_" + tag, parts, lands, _sibling_copies, 4)

    def rs_chips_start(tag, sibling_begun, after):
        parts, lands = _split_wait("rs_sibling_wait_" + tag, sibling_begun, _sibling_copies, after)
        sums = [_sum_sibling("rs_sum_%s_%d" % (tag, a), p, land, c_arr) for a, (p, land) in enumerate(zip(parts, lands))]
        lands2 = [lax.empty((3,) + sm.shape[1:], sm.dtype) for sm in sums]
        return _split_start("rs_chips_start_" + tag, sums, lands2, _chips_copies, 3)

    def rs_end(tag, group, begun, after):
        sums, lands2 = _split_wait("rs_chips_wait_" + tag, begun, _chips_copies, after)
        for n, sm, land2 in zip(group, sums, lands2):
            res = _finish_sharded("adamw_" + n, sm, land2, q_arr, local(P, n), local(M, n), local(V, n))
            grads[n], deltas[n], new_m[n], new_v[n] = [t.T if n in transposed else t for t in res]

    f2_dwd, f2_dwgu, f2_du = _ffn_bwd("ffn2", dh3_half, u3, ffn2_saved, wg2, wu2, wd2)
    dwd2 = f2_dwd()
    dwg2, dwu2 = f2_dwgu()
    sib_ffn2 = rs_sibling_start("ffn2", [dwg2, dwu2, dwd2])
    du3 = f2_du(deps=[sib_ffn2[4]])
    dh2, dh2_b, d_ffn2_norm = _rms_bwd("rms3_bwd", du3, h2, r3, row(ffn2_norm), dh3, 1.0)
    rs_ffn2 = rs_chips_start("ffn2", sib_ffn2, dh2)

    dg_out = ((2, S, W), BF, (None, tm, Dc), lambda b, i, j, k: (j // per, i, j % per))
    ga_blk2 = (proj, (None, tm, Dc), lambda b, i, j, k: (4 + j // per, i, j % per))
    gb_blk2 = (proj, (None, tm, Dc), lambda b, i, j, k: (6 + j // per, i, j % per))
    dcj = lambda arr: (arr, (tm, Dc), lambda b, i, j, k: (i, j))
    dcj_out = ((S, D), BF, (tm, Dc), lambda b, i, j, k: (i, j))

    def dmerge_epilogue(accs, sides):
        dm = accs[0]
        sa, sb = _sigmoid(sides[0]), _sigmoid(sides[1])
        za, zb = sides[2].astype(F32), sides[3].astype(F32)
        return dm * sa, dm * sb, dm * za * sa * (1.0 - sa), dm * zb * sb * (1.0 - sb)

    dz_a, dz_b, dga, dgb = _mm(
        "mix_out_dx", (1, S // tm, NDEV, D // tk),
        [(dh2_b, (tm, tk), lambda b, i, j, k: (i, k)), (w_o_f, (Dc, tk), lambda b, i, j, k: (j, k))],
        [(0, 1, NT, 0)], [ga_blk2, gb_blk2, dcj(z_a), dcj(z_b)], [dcj_out, dcj_out, dg_out, dg_out],
        dmerge_epilogue, [(tm, Dc)], deps=[rs_ffn2[4]])

    td = _pick(D, M_TILE, LANE)
    (dw_o,) = _mm("mix_out_dw", (1, D // td, D // tn, S // ts),
                  [(merged, (ts, td), lambda b, i, j, k: (k, i)), (dh2_b, (ts, tn), lambda b, i, j, k: (k, j))],
                  [(0, 1, TN, 0)], [], [((D, D), BF, (td, tn), lambda b, i, j, k: (i, j))], _plain, [(td, tn)])

    wout = ((NDEV, W, Dc), BF, (None, W, Dc), lambda b, i, j, k: (b, 0, 0))
    dw_so, dw_co = _mm(
        "mix_merge_dw", (NDEV, 1, 1, S // ts),
        [(y2, (ts, W), lambda b, i, j, k: (k, 0)), (yb, (ts, W), lambda b, i, j, k: (k, 0)),
         (dz_a, (ts, Dc), lambda b, i, j, k: (k, b)), (dz_b, (ts, Dc), lambda b, i, j, k: (k, b))],
        [(0, 2, TN, 0), (1, 3, TN, 1)], [], [wout, wout], _plain, [(W, Dc)] * 2)

    def dglu_epilogue(accs, sides):
        dy2, dyb = accs
        sq = _sigmoid(sides[1])
        return dy2 * _gelu(sides[0]) * sq * (1.0 - sq), dy2 * sq, dyb

    full_w = lambda arr: (arr, (th, W), lambda b, i, j, k: (i, 0))
    full_w_out = lambda dt: ((S, W), dt, (th, W), lambda b, i, j, k: (i, 0))
    dq, dy1p, dyb = _mm(
        "mix_merge_dx", (1, S // th, 1, NDEV),
        [(dz_a, (th, Dc), lambda b, i, j, k: (i, k)), (dz_b, (th, Dc), lambda b, i, j, k: (i, k)),
         (w_so, (None, W, Dc), lambda b, i, j, k: (k, 0, 0)), (w_co, (None, W, Dc), lambda b, i, j, k: (k, 0, 0))],
        [(0, 2, NT, 0), (1, 3, NT, 1)], [full_w(y0), full_w(q_pre)], [full_w_out(BF), full_w_out(F32), full_w_out(F32)],
        dglu_epilogue, [(th, W)] * 2)

    def dy0_epilogue(accs, sides):
        dy0 = (sides[0] + accs[0]) * _gelu_grad(sides[1])
        return dy0, dy0

    wj = lambda arr: (arr, (tm, tw), lambda b, i, j, k: (i, j))
    dy0, dy0_b = _mm("ssm_glu_dx", (1, S // tm, W // tw, 1),
                     [(dq, (tm, W), lambda b, i, j, k: (i, 0)), (w_glu_f, (tw, W), lambda b, i, j, k: (j, 0))],
                     [(0, 1, NT, 0)], [wj(dy1p), wj(y0)],
                     [((S, W), F32, (tm, tw), lambda b, i, j, k: (i, j)), ((S, W), BF, (tm, tw), lambda b, i, j, k: (i, j))],
                     dy0_epilogue, [(tm, tw)])

    (dw_glu,) = _mm("ssm_glu_dw", (1, W // tw, 1, S // ts),
                    [(y1, (ts, tw), lambda b, i, j, k: (k, i)), (dq, (ts, W), lambda b, i, j, k: (k, 0))],
                    [(0, 1, TN, 0)], [], [((W, W), BF, (tw, W), lambda b, i, j, k: (i, 0))], _plain, [(tw, W)])

    tr = _pick(S, 256, SUBLANE)
    rw = ((tr, W), lambda i: (i, 0))
    vec_w = ((1, W), F32, (1, W), lambda i: (0, 0))
    d_b_glu, d_ssm_d = _ew(
        "ssm_colsums", (S // tr,), [(dq,) + rw, (dy0,) + rw, (proj, (None, tr, W), lambda i: (0, i, 0))],
        [vec_w, vec_w],
        lambda dqv, dyv, vv: (jnp.sum(dqv.astype(F32), axis=0, keepdims=True), jnp.sum(dyv * vv, axis=0, keepdims=True)),
        acc=(0, 1))

    ds_re, ds_im = _mm("ssm_ds", (tiles, S // tq, 1, 1), [ch_blk(dy0_b), cd_blk(cd_re_b), cd_blk(cd_imn_b)],
                       [(0, 1, NT, 0), (0, 2, NT, 1)], [], [st_out(F32), st_out(F32)], _plain, [(tq, tst)] * 2)
    lam_re3, lam_im3, da_re, da_im = _scan_bwd(ds_re.reshape(S, rows, LANE), ds_im.reshape(S, rows, LANE),
                                               s_re3, s_im3, a_re, a_im)
    lam_re_b = lam_re3.reshape(S, GN).astype(BF)
    lam_im_b = lam_im3.reshape(S, GN).astype(BF)

    (dv,) = _mm("ssm_dv", (tiles, S // tq, 1, 1),
                [st_blk(lam_re_b), st_blk(lam_im_b), bd_blk(bd_re_b), bd_blk(bd_im_b)],
                [(0, 2, NT, 0), (1, 3, NT, 0)], [ch_blk(dy0), d_blk], [ch_out(BF)],
                lambda accs, sides: [accs[0] + sides[0] * sides[1]], [(tq, tch)])

    tok_ch = lambda arr: (arr, (ts, tch), lambda b, i, j, k: (k, b))
    tok_st = lambda arr: (arr, (ts, tst), lambda b, i, j, k: (k, b))
    bd_out = ((tiles, tch, tst), F32, (None, tch, tst), lambda b, i, j, k: (b, 0, 0))
    cd_out = ((tiles, tst, tch), F32, (None, tst, tch), lambda b, i, j, k: (b, 0, 0))
    dbd_re, dbd_im = _mm("ssm_dbd", (tiles, 1, 1, S // ts), [tok_ch(v_bf), tok_st(lam_re_b), tok_st(lam_im_b)],
                         [(0, 1, TN, 0), (0, 2, TN, 1)], [], [bd_out, bd_out], _plain, [(tch, tst)] * 2)
    dcd_re, dcd_imn = _mm("ssm_dcd", (tiles, 1, 1, S // ts), [tok_st(s_re_b), tok_st(s_im_b), tok_ch(dy0_b)],
                          [(0, 2, TN, 0), (1, 2, TN, 1)], [], [cd_out, cd_out], _plain, [(tst, tch)] * 2)
    d_ssm = ssm_vjp((da_re, da_im, dbd_re, dbd_im, dcd_re, dcd_imn))

    dbg, dcg, dval, d_conv_w_full, d_conv_b = _conv_bwd(dyb, proj, conv, cw)
    dproj = jnp.concatenate([dv[None], dbg[None], dcg[None], dval[None], dga, dgb], axis=0)

    (dw_in,) = _mm("in_proj_dw", (NDEV, D // td, 1, S // ts),
                   [(u2, (ts, td), lambda b, i, j, k: (k, i)), (dproj, (None, ts, W), lambda b, i, j, k: (b, k, 0))],
                   [(0, 1, TN, 0)], [], [((NDEV, D, W), BF, (None, td, W), lambda b, i, j, k: (b, i, 0))],
                   _plain, [(td, W)])
    sib_mixer = rs_sibling_start(
        "mixer", [dw_in, dw_glu.reshape(NDEV, W // NDEV, W), dw_so, dw_co, dw_o.reshape(NDEV, Dc, D)])
    (du2,) = _mm("in_proj_dx", (1, S // tm, D // tn, NDEV),
                 [(dproj, (None, tm, W), lambda b, i, j, k: (k, i, 0)), (w_in_f, (None, tn, W), lambda b, i, j, k: (k, j, 0))],
                 [(0, 1, NT, 0)], [], [((S, D), F32, (tm, tn), lambda b, i, j, k: (i, j))], _plain, [(tm, tn)],
                 deps=[sib_mixer[4]])
    dh1, dh1_half, d_mix_norm = _rms_bwd("rms2_bwd", du2, h1, r2, row(mix_norm), dh2, 0.5)
    rs_mixer = rs_chips_start("mixer", sib_mixer, dh1)

    small = dict(mix_norm=d_mix_norm, ffn2_norm=d_ffn2_norm, final_norm=d_final_norm,
                 ssm_lambda_re=d_ssm[0], ssm_lambda_im=d_ssm[1], ssm_log_dt=d_ssm[2], ssm_b_re=d_ssm[3],
                 ssm_b_im=d_ssm[4], ssm_c_re=d_ssm[5], ssm_c_im=d_ssm[6], ssm_d=d_ssm_d, ssm_b_glu=d_b_glu,
                 conv_b=d_conv_b)
    replicated = [n for n in replicated if n != "ffn1_norm"] + ["ffn1_norm"]
    sizes = [P[n].size for n in replicated]
    early = sum(sizes[:-1]) + d_conv_w_full.size
    padded = -(-early // (SUBLANE * LANE)) * (SUBLANE * LANE)

    def pack(parts, last):
        flat = jnp.concatenate([p.reshape(-1).astype(F32) for p in parts])
        flat = jnp.pad(flat, (0, padded - flat.shape[0]))
        if last is not None:
            flat = jnp.concatenate([flat, last.reshape(-1)])
        return flat.reshape(-1, LANE)

    cw_zero = jnp.zeros_like(d_conv_w_full)
    early_pk = pack([small[n] for n in replicated[:-1]] + [d_conv_w_full], None)
    early_land = lax.dynamic_update_slice(jnp.zeros((NDEV,) + early_pk.shape, F32), early_pk[None], (me, 0, 0))
    small_begun = _split_start("gather_small_start", [early_pk], [early_land], _everyone_copies, NDEV - 1)

    f1_dwd, f1_dwgu, f1_du = _ffn_bwd("ffn1", dh1_half, u1, ffn1_saved, wg1, wu1, wd1,
                                      deps=[rs_mixer[4], small_begun[4]])
    du1 = f1_du()
    dx, _, d_ffn1_norm = _rms_bwd("rms1_bwd", du1, xh, r1, row(ffn1_norm), dh1, 1.0)
    (late_all,) = _all_gather("gather_ffn1_norm_grad", [d_ffn1_norm.reshape(-1, LANE)])
    dwd1 = f1_dwd(deps=[late_all])
    rs_ffn1_down = rs_chips_start("ffn1_down", rs_sibling_start("ffn1_down", [dwd1]), None)
    dwg1, dwu1 = f1_dwgu(deps=[rs_ffn1_down[4]])
    rs_ffn1_gate_up = rs_chips_start("ffn1_gate_up", rs_sibling_start("ffn1_gate_up", [dwg1, dwu1]), None)
    rs_end("ffn2", sharded[8:11], rs_ffn2, rs_ffn1_gate_up[4])
    rs_end("mixer", sharded[3:8], rs_mixer, [grads[n] for n in sharded[8:11]])
    _, (early_all,) = _split_wait("gather_small_wait", small_begun, _everyone_copies, [grads[n] for n in sharded[3:8]])
    small_all = jnp.concatenate([early_all, late_all], axis=1)
    g_pk, d_pk, m_pk, v_pk = _finish_replicated(
        "adamw_replicated", small_all, pack([P[n] for n in replicated[:-1]] + [cw_zero], P["ffn1_norm"]),
        pack([M[n] for n in replicated[:-1]] + [cw_zero], M["ffn1_norm"]),
        pack([V[n] for n in replicated[:-1]] + [cw_zero + 1.0], V["ffn1_norm"]))
    off = 0
    for n, sz in zip(replicated, sizes):
        at = padded if n == "ffn1_norm" else off
        for store, pk in ((grads, g_pk), (deltas, d_pk), (new_m, m_pk), (new_v, v_pk)):
            store[n] = pk.reshape(-1)[at:at + sz].reshape(P[n].shape)
        off += sz
    off -= sizes[-1]
    g_cw_full = g_pk.reshape(-1)[off:off + d_conv_w_full.size].reshape(d_conv_w_full.shape)
    cwl = conv_w.shape[1]
    g_cw = lax.dynamic_slice_in_dim(g_cw_full, me * cwl, cwl, axis=1)
    full3 = ((3, cwl), lambda i: (0, 0))
    grads["conv_w"], deltas["conv_w"], new_m["conv_w"], new_v["conv_w"] = _ew(
        "adamw_conv_w", (1,), [(g_cw,) + full3, (conv_w,) + full3, (m_conv_w,) + full3, (v_conv_w,) + full3],
        [((3, cwl), F32) + full3] * 4, lambda g, w, m, v: (g,) + _adamw(w, g, m, v))
    rs_end("ffn1_down", sharded[2:3], rs_ffn1_down, [g_pk, grads["conv_w"]])
    rs_end("ffn1_gate_up", sharded[0:2], rs_ffn1_gate_up, grads["ffn1_w_down"])

    return (loss, dx.reshape(x.shape), *[grads[n] for n in names], *[deltas[n] for n in names],
            *[new_m[n] for n in names], *[new_v[n] for n in names])
```

```python
import math

import jax
import jax.numpy as jnp
from jax import lax
from jax.experimental import pallas as pl
from jax.experimental.pallas import tpu as pltpu

F32 = jnp.float32
BF = jnp.bfloat16
I32 = jnp.int32
MESH = pl.DeviceIdType.MESH
LANE = 128
SUBLANE = 8
NDEV = 8
EW_BLOCK = 256 * 1024
M_TILE = 1024
K_TILE = 2048
DMA_CHUNKS = 4
EPS = 1e-6
ADAM_LR, ADAM_B1, ADAM_B2, ADAM_EPS, ADAM_WD, ADAM_STEP = 0.001, 0.9, 0.999, 1e-08, 0.01, 10
NN = ((1,), (0,))
NT = ((1,), (1,))
TN = ((0,), (0,))
HBM = pl.BlockSpec(memory_space=pltpu.HBM)


def _pick(n, pref, mult):
    t = min(pref, n)
    t -= t % mult
    while t >= mult:
        if n % t == 0:
            return t
        t -= mult
    return n


def _sigmoid(x):
    return 1.0 / (1.0 + jnp.exp(-x))


_GELU_C = math.sqrt(2.0 / math.pi)


def _gelu(x):
    return 0.5 * x * (1.0 + jnp.tanh(_GELU_C * (x + 0.044715 * x * x * x)))


def _gelu_grad(x):
    t = jnp.tanh(_GELU_C * (x + 0.044715 * x * x * x))
    return 0.5 * (1.0 + t) + 0.5 * x * (1.0 - t * t) * _GELU_C * (1.0 + 3.0 * 0.044715 * x * x)


def _dep_specs(deps, rank):
    return [(d, d.shape, lambda *_, nd=d.ndim: (0,) * nd) for d in deps]


def _mm(name, grid, ops, pairs, sides, outs, epilogue, acc_shapes, deps=()):
    nk = grid[-1]
    n_ops, n_sides, n_outs = len(ops), len(sides), len(outs)
    dep_specs = _dep_specs(deps, len(grid))
    n_deps = len(dep_specs)

    def body(*refs):
        op_refs = refs[:n_ops]
        side_refs = refs[n_ops:n_ops + n_sides]
        out_refs = refs[n_ops + n_sides + n_deps:n_ops + n_sides + n_deps + n_outs]
        acc_refs = refs[n_ops + n_sides + n_deps + n_outs:]

        def partials():
            res = [None] * len(acc_shapes)
            for ia, ib, dims, ai in pairs:
                p = lax.dot_general(op_refs[ia][...], op_refs[ib][...], (dims, ((), ())),
                                    preferred_element_type=F32)
                res[ai] = p if res[ai] is None else res[ai] + p
            return res

        def finish(accs):
            vals = epilogue(accs, [s[...] for s in side_refs])
            for o, v in zip(out_refs, vals):
                o[...] = v.astype(o.dtype)

        if nk == 1:
            finish(partials())
        else:
            k = pl.program_id(len(grid) - 1)

            @pl.when(k == 0)
            def _():
                for a, p in zip(acc_refs, partials()):
                    a[...] = p

            @pl.when(k > 0)
            def _():
                for a, p in zip(acc_refs, partials()):
                    a[...] += p

            @pl.when(k == nk - 1)
            def _():
                finish([a[...] for a in acc_refs])

    return pl.pallas_call(
        body, name=name, grid=grid,
        in_specs=[pl.BlockSpec(b, m) for (_, b, m) in list(ops) + list(sides) + dep_specs],
        out_specs=[pl.BlockSpec(b, m) for (_, _, b, m) in outs],
        out_shape=[jax.ShapeDtypeStruct(s, d) for (s, d, _, _) in outs],
        scratch_shapes=[pltpu.VMEM(s, F32) for s in acc_shapes] if nk > 1 else [],
        compiler_params=pltpu.CompilerParams(
            dimension_semantics=("parallel",) * (len(grid) - 1) + ("arbitrary",)),
    )(*[a for (a, _, _) in list(ops) + list(sides) + dep_specs])


def _ew(name, grid, ins, outs, fn, acc=(), deps=()):
    n_in = len(ins)
    dep_specs = _dep_specs(deps, len(grid))

    def body(*refs):
        vals = fn(*[r[...] for r in refs[:n_in]])
        first = pl.program_id(0) == 0
        for idx, (o, v) in enumerate(zip(refs[n_in + len(dep_specs):], vals)):
            if idx in acc:
                @pl.when(first)
                def _(o=o, v=v):
                    o[...] = v.astype(o.dtype)

                @pl.when(jnp.logical_not(first))
                def _(o=o, v=v):
                    o[...] += v.astype(o.dtype)
            else:
                o[...] = v.astype(o.dtype)

    return pl.pallas_call(
        body, name=name, grid=grid,
        in_specs=[pl.BlockSpec(b, m) for (_, b, m) in list(ins) + dep_specs],
        out_specs=[pl.BlockSpec(b, m) for (_, _, b, m) in outs],
        out_shape=[jax.ShapeDtypeStruct(s, d) for (s, d, _, _) in outs],
        compiler_params=pltpu.CompilerParams(
            dimension_semantics=(("arbitrary",) if acc else ("parallel",)) * len(grid)),
    )(*[a for (a, _, _) in list(ins) + dep_specs])


def _position():
    x, y, c = lax.axis_index("x"), lax.axis_index("y"), lax.axis_index("c")
    chips = [(1 - x, y), (x, 1 - y), (1 - x, 1 - y)]
    return x, y, c, chips


def _all_gather(name, shards):
    n = len(shards)

    def body(*refs):
        xs, outs = refs[:n], refs[n:2 * n]
        send_sems, recv_sems, local_sems = refs[2 * n:]
        x, y, c, chips = _position()
        me, sibling = (x, y, c), (x, y, 1 - c)

        def copy(a, k, block, to, src=None):
            dst = outs[a].at[4 * block[0] + 2 * block[1] + block[2]]
            return pltpu.make_async_remote_copy(
                src_ref=dst if src is None else src, dst_ref=dst,
                send_sem=send_sems.at[a, k], recv_sem=recv_sems.at[a, k],
                device_id=to, device_id_type=MESH)

        mine = [pltpu.make_async_copy(xs[a], outs[a].at[4 * x + 2 * y + c], local_sems.at[a]) for a in range(n)]
        for cp in mine:
            cp.start()
        first = []
        for a in range(n):
            first.append(copy(a, 0, me, sibling, src=xs[a]))
            first += [copy(a, 1 + j, me, (*chip, c), src=xs[a]) for j, chip in enumerate(chips)]
        for cp in first:
            cp.start()
        passed = []
        for a in range(n):
            for j, chip in enumerate(chips):
                copy(a, 1 + j, (*chip, c), me).wait_recv()
                cp = copy(a, 4 + j, (*chip, c), sibling)
                cp.start()
                passed.append(cp)
        for a in range(n):
            copy(a, 0, sibling, me).wait_recv()
            for j, chip in enumerate(chips):
                copy(a, 4 + j, (*chip, 1 - c), me).wait_recv()
        for cp in first + passed:
            cp.wait_send()
        for cp in mine:
            cp.wait()

    return pl.pallas_call(
        body, name=name,
        out_shape=[jax.ShapeDtypeStruct((NDEV,) + s.shape, s.dtype) for s in shards],
        in_specs=[HBM] * n, out_specs=[HBM] * n,
        scratch_shapes=[pltpu.SemaphoreType.DMA((n, 7)), pltpu.SemaphoreType.DMA((n, 7)),
                        pltpu.SemaphoreType.DMA((n,))],
    )(*shards)


SEM = pl.BlockSpec(memory_space=pltpu.SEMAPHORE)
EFFECT = pltpu.SideEffectType.DATAFLOW_SIDE_EFFECTING


def _in_hbm(v):
    return pltpu.with_memory_space_constraint(v, pltpu.HBM)


def _split_start(name, srcs, lands, make_copies, n_per, after=()):
    n = len(srcs)
    after = list(after)

    def body(*refs):
        send_sems, recv_sems = refs[2 * n + len(after)], refs[2 * n + len(after) + 1]
        for cp in make_copies(refs[:n], refs[n:2 * n], send_sems, recv_sems):
            cp.start()
        refs[-1][...] = jnp.zeros_like(refs[-1])

    outs = pl.pallas_call(
        body, name=name,
        out_shape=(pltpu.SemaphoreType.DMA((n * n_per,)), pltpu.SemaphoreType.DMA((n * n_per,)),
                   *[pltpu.HBM(v.shape, v.dtype) for v in list(srcs) + list(lands)],
                   jax.ShapeDtypeStruct((SUBLANE, LANE), F32)),
        in_specs=[HBM] * (2 * n) + [pl.BlockSpec(memory_space=pl.ANY)] * len(after),
        out_specs=(SEM, SEM, *[HBM] * (2 * n), pl.BlockSpec(memory_space=pltpu.VMEM)),
        input_output_aliases={i: 2 + i for i in range(2 * n)},
        compiler_params=pltpu.CompilerParams(has_side_effects=EFFECT),
    )(*[_in_hbm(v) for v in list(srcs) + list(lands)], *after)
    return outs[0], outs[1], list(outs[2:2 + n]), list(outs[2 + n:2 + 2 * n]), outs[-1]


def _split_wait(name, started, make_copies, after):
    send_sems, recv_sems, srcs, lands, _ = started
    n = len(srcs)

    def body(*refs):
        for cp in make_copies(refs[:n], refs[n:2 * n], refs[2 * n], refs[2 * n + 1]):
            cp.wait_send()
            cp.wait_recv()

    order = [] if after is None else list(after) if isinstance(after, (list, tuple)) else [after]
    outs = pl.pallas_call(
        body, name=name,
        out_shape=tuple(pltpu.HBM(v.shape, v.dtype) for v in srcs + lands),
        in_specs=[HBM] * (2 * n) + [SEM, SEM] + [pl.BlockSpec(memory_space=pl.ANY)] * len(order),
        out_specs=tuple([HBM] * (2 * n)),
        input_output_aliases={i: i for i in range(2 * n)},
        compiler_params=pltpu.CompilerParams(has_side_effects=EFFECT),
    )(*srcs, *lands, send_sems, recv_sems, *order)
    return list(outs[:n]), list(outs[n:])


def _gather_copies(xs, lands, send_sems, recv_sems):
    x, y, c, chips = _position()
    copies = []
    for a in range(len(xs)):
        for k, peer in enumerate([(x, y, 1 - c)] + [(*chip, c) for chip in chips]):
            copies.append(pltpu.make_async_remote_copy(
                src_ref=xs[a], dst_ref=lands[a].at[4 * x + 2 * y + c],
                send_sem=send_sems.at[4 * a + k], recv_sem=recv_sems.at[4 * a + k], device_id=peer, device_id_type=MESH))
    return copies


def _chips_copies(ps, lands, send_sems, recv_sems):
    x, y, c, chips = _position()
    copies = []
    for a in range(len(ps)):
        for j, chip in enumerate(chips):
            copies.append(pltpu.make_async_remote_copy(
                src_ref=ps[a].at[2 * chip[0] + chip[1]], dst_ref=lands[a].at[j],
                send_sem=send_sems.at[3 * a + j], recv_sem=recv_sems.at[3 * a + j], device_id=(*chip, c),
                device_id_type=MESH))
    return copies


def _sibling_copies(gs, lands, send_sems, recv_sems):
    x, y, c, _ = _position()
    copies = []
    for a in range(len(gs)):
        for q in range(4):
            copies.append(pltpu.make_async_remote_copy(
                src_ref=gs[a].at[2 * q + 1 - c], dst_ref=lands[a].at[q],
                send_sem=send_sems.at[4 * a + q], recv_sem=recv_sems.at[4 * a + q],
                device_id=(x, y, 1 - c), device_id_type=MESH))
    return copies


def _everyone_copies(xs, lands, send_sems, recv_sems):
    x, y, c, _ = _position()
    flip = lambda v, bit: 1 - v if bit else v
    copies = []
    for a in range(len(xs)):
        for k in range(1, NDEV):
            copies.append(pltpu.make_async_remote_copy(
                src_ref=xs[a], dst_ref=lands[a].at[4 * x + 2 * y + c],
                send_sem=send_sems.at[7 * a + k - 1], recv_sem=recv_sems.at[7 * a + k - 1],
                device_id=(flip(x, k & 4), flip(y, k & 2), flip(c, k & 1)), device_id_type=MESH))
    return copies


def _row_chunks(rows, dtype):
    unit = SUBLANE * (4 // jnp.dtype(dtype).itemsize)
    units = rows // unit
    if rows % unit or units < 2:
        return [(0, rows)]
    k = min(DMA_CHUNKS, units)
    sizes = [(units // k + (1 if i < units % k else 0)) * unit for i in range(k)]
    return [(sum(sizes[:i]), sz) for i, sz in enumerate(sizes)]


def _gather_forward(name, lands):
    n = len(lands)

    def body(*refs):
        ins, outs = refs[:n], refs[n:2 * n]
        send_sems, recv_sems = refs[2 * n:]
        x, y, c, chips = _position()
        whole, chunks = [], []
        for a in range(n):
            rows = _row_chunks(ins[a].shape[1], ins[a].dtype)
            for j, chip in enumerate(chips):
                slot = 4 * chip[0] + 2 * chip[1]

                def to_sibling(src, dst):
                    return pltpu.make_async_remote_copy(
                        src_ref=src, dst_ref=dst, send_sem=send_sems.at[a, j], recv_sem=recv_sems.at[a, j],
                        device_id=(x, y, 1 - c), device_id_type=MESH)

                whole.append(to_sibling(ins[a].at[slot + c], outs[a].at[slot + 1 - c]))
                chunks += [to_sibling(ins[a].at[slot + c, pl.ds(r0, nr)], outs[a].at[slot + c, pl.ds(r0, nr)])
                           for r0, nr in rows]
        for cp in chunks:
            cp.start()
        for cp in whole:
            cp.wait()

    return pl.pallas_call(
        body, name=name,
        out_shape=[jax.ShapeDtypeStruct(l.shape, l.dtype) for l in lands],
        in_specs=[HBM] * n, out_specs=[HBM] * n,
        input_output_aliases={a: a for a in range(n)},
        scratch_shapes=[pltpu.SemaphoreType.DMA((n, 3)), pltpu.SemaphoreType.DMA((n, 3))],
    )(*lands)


def _sum_sibling(name, g, land, c_arr):
    _, R, C = g.shape
    tr = _pick(R, 512, SUBLANE)

    def body(c_ref, g_ref, l_ref, o_ref):
        o_ref[...] = (g_ref[...].astype(F32) + l_ref[...].astype(F32)).astype(o_ref.dtype)

    return pl.pallas_call(
        body, name=name,
        grid_spec=pltpu.PrefetchScalarGridSpec(
            num_scalar_prefetch=1, grid=(4, R // tr),
            in_specs=[pl.BlockSpec((None, tr, C), lambda q, i, cr: (2 * q + cr[0], i, 0)),
                      pl.BlockSpec((None, tr, C), lambda q, i, cr: (q, i, 0))],
            out_specs=pl.BlockSpec((None, tr, C), lambda q, i, cr: (q, i, 0))),
        out_shape=jax.ShapeDtypeStruct((4, R, C), g.dtype),
        compiler_params=pltpu.CompilerParams(dimension_semantics=("parallel", "parallel")),
    )(c_arr, g, land)


def _adamw(w, g, m, v):
    m = ADAM_B1 * m + (1.0 - ADAM_B1) * g
    v = ADAM_B2 * v + (1.0 - ADAM_B2) * (g * g)
    m_hat = m / (1.0 - ADAM_B1 ** ADAM_STEP)
    v_hat = v / (1.0 - ADAM_B2 ** ADAM_STEP)
    delta = -ADAM_LR * (m_hat / (jnp.sqrt(v_hat) + ADAM_EPS) + ADAM_WD * w)
    return delta, m, v


def _finish_sharded(name, sums, land, q_arr, w, m, v):
    R, C = w.shape
    tr = _pick(R, 512, SUBLANE)
    tc = _pick(C, max(LANE, EW_BLOCK // tr), LANE)

    def body(q_ref, p_ref, l_ref, w_ref, m_ref, v_ref, g_out, d_out, m_out, v_out):
        g = p_ref[...].astype(F32)
        for j in range(3):
            g = g + l_ref[j].astype(F32)
        d, mn, vn = _adamw(w_ref[...], g, m_ref[...], v_ref[...])
        g_out[...] = g
        d_out[...] = d
        m_out[...] = mn
        v_out[...] = vn

    blk = pl.BlockSpec((tr, tc), lambda i, j, qr: (i, j))
    return pl.pallas_call(
        body, name=name,
        grid_spec=pltpu.PrefetchScalarGridSpec(
            num_scalar_prefetch=1, grid=(R // tr, C // tc),
            in_specs=[pl.BlockSpec((None, tr, tc), lambda i, j, qr: (qr[0], i, j)),
                      pl.BlockSpec((3, tr, tc), lambda i, j, qr: (0, i, j)), blk, blk, blk],
            out_specs=[blk] * 4),
        out_shape=[jax.ShapeDtypeStruct((R, C), F32)] * 4,
        compiler_params=pltpu.CompilerParams(dimension_semantics=("parallel", "parallel")),
    )(q_arr, sums, land, w, m, v)


def _finish_replicated(name, gathered, w, m, v):
    _, R, C = gathered.shape
    tr = _pick(R, 256, SUBLANE)

    def fn(gv, wv, mv, vv):
        g = gv[0]
        for d in range(1, NDEV):
            g = g + gv[d]
        dl, mn, vn = _adamw(wv, g, mv, vv)
        return g, dl, mn, vn

    row = ((tr, C), lambda i: (i, 0))
    return _ew(name, (R // tr,),
               [(gathered, (NDEV, tr, C), lambda i: (0, i, 0)), (w,) + row, (m,) + row, (v,) + row],
               [((R, C), F32) + row] * 4, fn)


def _rms_fwd(name, h, g, deps=()):
    S, D = h.shape
    tr = _pick(S, 256, SUBLANE)

    def fn(hv, gv):
        r = lax.rsqrt(jnp.mean(hv * hv, axis=-1, keepdims=True) + EPS)
        return hv * r * gv, r

    return _ew(name, (S // tr,),
               [(h, (tr, D), lambda i: (i, 0)), (g, (1, D), lambda i: (0, 0))],
               [((S, D), BF, (tr, D), lambda i: (i, 0)), ((S, 1), F32, (tr, 1), lambda i: (i, 0))], fn, deps=deps)


def _rms_bwd(name, du, h, r, g, dres, scale):
    S, D = h.shape
    tr = _pick(S, 256, SUBLANE)

    def fn(duv, hv, rv, gv, drv):
        xn = hv * rv
        dxn = duv * gv
        dh = drv + rv * (dxn - xn * jnp.mean(dxn * xn, axis=-1, keepdims=True))
        return dh, scale * dh, jnp.sum(duv * xn, axis=0, keepdims=True)

    row = ((tr, D), lambda i: (i, 0))
    return _ew(name, (S // tr,),
               [(du,) + row, (h,) + row, (r, (tr, 1), lambda i: (i, 0)), (g, (1, D), lambda i: (0, 0)), (dres,) + row],
               [((S, D), F32) + row, ((S, D), BF) + row, ((1, D), F32, (1, D), lambda i: (0, 0))], fn, acc=(2,))


def _loss_head(name, h, g, target):
    S, D = h.shape
    tr = _pick(S, 256, SUBLANE)

    def fn(hv, gv, tv):
        r = lax.rsqrt(jnp.mean(hv * hv, axis=-1, keepdims=True) + EPS)
        xn = hv * r
        diff = xn * gv - tv
        loss = 0.5 * jnp.sum(jnp.mean(diff * diff, axis=-1, keepdims=True))
        dout = diff / D
        dxn = dout * gv
        dh = r * (dxn - xn * jnp.mean(dxn * xn, axis=-1, keepdims=True))
        return (jnp.zeros((1, LANE), F32) + loss, dh, 0.5 * dh, jnp.sum(dout * xn, axis=0, keepdims=True))

    row = ((tr, D), lambda i: (i, 0))
    return _ew(name, (S // tr,),
               [(h,) + row, (g, (1, D), lambda i: (0, 0)), (target,) + row],
               [((1, LANE), F32, (1, LANE), lambda i: (0, 0)), ((S, D), F32) + row, ((S, D), BF) + row,
                ((1, D), F32, (1, D), lambda i: (0, 0))], fn, acc=(0, 3))


def _ffn_fwd(tag, u, h, wg, wu, wd, deps=()):
    S, D = u.shape
    Fs = wg.shape[1]
    tm, tk = _pick(S, M_TILE, SUBLANE), _pick(D, K_TILE, LANE)

    def up_epilogue(accs, sides):
        gt, up = accs
        return gt, up, gt * _sigmoid(gt) * up

    act = ((NDEV, S, Fs), BF, (None, tm, Fs), lambda b, i, j, k: (b, i, 0))
    gt, up, a = _mm(
        tag + "_up", (NDEV, S // tm, 1, D // tk),
        [(u, (tm, tk), lambda b, i, j, k: (i, k)),
         (wg, (None, Fs, tk), lambda b, i, j, k: (b, 0, k)), (wu, (None, Fs, tk), lambda b, i, j, k: (b, 0, k))],
        [(0, 1, NT, 0), (0, 2, NT, 1)], [], [act, act, act], up_epilogue, [(tm, Fs), (tm, Fs)], deps=deps)
    if callable(wd):
        wd = wd(a)
    tn = _pick(D, 1024, LANE)
    (hn,) = _mm(
        tag + "_down", (1, S // tm, D // tn, NDEV),
        [(a, (None, tm, Fs), lambda b, i, j, k: (k, i, 0)), (wd, (None, Fs, tn), lambda b, i, j, k: (k, 0, j))],
        [(0, 1, NN, 0)], [(h, (tm, tn), lambda b, i, j, k: (i, j))],
        [((S, D), F32, (tm, tn), lambda b, i, j, k: (i, j))],
        lambda accs, sides: [sides[0] + 0.5 * accs[0]], [(tm, tn)])
    return hn, (gt, up, a), wd


def _ffn_bwd(tag, dhs, u, saved, wg, wu, wd, deps=()):
    gt, up, a = saved
    S, D = u.shape
    Fs = wg.shape[1]
    tm, tk = _pick(S, M_TILE, SUBLANE), _pick(D, K_TILE, LANE)
    act_in = lambda arr: (arr, (None, tm, Fs), lambda b, i, j, k: (b, i, 0))
    act_out = ((NDEV, S, Fs), BF, (None, tm, Fs), lambda b, i, j, k: (b, i, 0))

    def act_epilogue(accs, sides):
        da = accs[0]
        gtv, upv = sides[0].astype(F32), sides[1].astype(F32)
        sg = _sigmoid(gtv)
        return da * upv * sg * (1.0 + gtv * (1.0 - sg)), da * gtv * sg

    dgt, dup = _mm(
        tag + "_dact", (NDEV, S // tm, 1, D // tk),
        [(dhs, (tm, tk), lambda b, i, j, k: (i, k)), (wd, (None, Fs, tk), lambda b, i, j, k: (b, 0, k))],
        [(0, 1, NT, 0)], [act_in(gt), act_in(up)], [act_out, act_out], act_epilogue, [(tm, Fs)], deps=deps)

    ts = _pick(S, K_TILE, SUBLANE)
    tn = _pick(D, 1024, LANE)
    wgrad = ((NDEV, Fs, D), BF, (None, Fs, tn), lambda b, i, j, k: (b, 0, j))
    tok = lambda arr: (arr, (None, ts, Fs), lambda b, i, j, k: (b, k, 0))

    def grad_down(deps=()):
        return _mm(
            tag + "_dwd", (NDEV, 1, D // tn, S // ts),
            [tok(a), (dhs, (ts, tn), lambda b, i, j, k: (k, j))],
            [(0, 1, TN, 0)], [], [wgrad], lambda accs, sides: accs, [(Fs, tn)], deps=deps)[0]

    def grad_gate_up(deps=()):
        return _mm(
            tag + "_dwgu", (NDEV, 1, D // tn, S // ts),
            [tok(dgt), tok(dup), (u, (ts, tn), lambda b, i, j, k: (k, j))],
            [(0, 2, TN, 0), (1, 2, TN, 1)], [], [wgrad, wgrad], lambda accs, sides: accs, [(Fs, tn), (Fs, tn)],
            deps=deps)

    def du(deps=()):
        return _mm(
            tag + "_du", (1, S // tm, D // tn, NDEV),
            [(dgt, (None, tm, Fs), lambda b, i, j, k: (k, i, 0)), (dup, (None, tm, Fs), lambda b, i, j, k: (k, i, 0)),
             (wg, (None, Fs, tn), lambda b, i, j, k: (k, 0, j)), (wu, (None, Fs, tn), lambda b, i, j, k: (k, 0, j))],
            [(0, 2, NN, 0), (1, 3, NN, 0)], [], [((S, D), F32, (tm, tn), lambda b, i, j, k: (i, j))],
            lambda accs, sides: accs, [(tm, tn)], deps=deps)[0]

    return grad_down, grad_gate_up, du


def _ssm_params(lam_re, lam_im, log_dt, b_re, b_im, c_re, c_im):
    G, N = lam_re.shape
    C = b_re.shape[2]
    lam_re = jnp.minimum(lam_re, -1e-4)
    dt = jnp.exp(log_dt)[:, None]
    mag = jnp.exp(lam_re * dt)
    a_re = mag * jnp.cos(lam_im * dt)
    a_im = mag * jnp.sin(lam_im * dt)
    den = lam_re * lam_re + lam_im * lam_im
    p = a_re - 1.0
    f_re = ((p * lam_re + a_im * lam_im) / den)[:, :, None]
    f_im = ((a_im * lam_re - p * lam_im) / den)[:, :, None]
    bb_re = f_re * b_re - f_im * b_im
    bb_im = f_re * b_im + f_im * b_re
    gpt = LANE // C
    tiles = G // gpt
    eye = jnp.eye(gpt, dtype=F32)

    def bd(bb):
        return jnp.einsum("bgnc,gh->bgchn", bb.reshape(tiles, gpt, N, C), eye).reshape(tiles, gpt * C, gpt * N)

    def cd(cc):
        return jnp.einsum("bgcn,gh->bgnhc", cc.reshape(tiles, gpt, C, N), eye).reshape(tiles, gpt * N, gpt * C)

    rows = G * N // LANE
    return (a_re.reshape(rows, LANE), a_im.reshape(rows, LANE), bd(bb_re), bd(bb_im), cd(c_re), cd(-c_im))


def _scan_fwd(bu_re, bu_im, a_re, a_im):
    S, R, _ = bu_re.shape
    tc = _pick(S, 256, SUBLANE)

    def body(bre, bim, are, aim, sre, sim, carry):
        @pl.when(pl.program_id(0) == 0)
        def _():
            carry[...] = jnp.zeros_like(carry)

        ar, ai = are[...], aim[...]

        def step(t, c):
            pr, pi = c
            nr = ar * pr - ai * pi + bre[t]
            ni = ar * pi + ai * pr + bim[t]
            sre[t] = nr
            sim[t] = ni
            return nr, ni

        pr, pi = lax.fori_loop(0, tc, step, (carry[0], carry[1]), unroll=8)
        carry[0] = pr
        carry[1] = pi

    blk = pl.BlockSpec((tc, R, LANE), lambda i: (i, 0, 0))
    par = pl.BlockSpec((R, LANE), lambda i: (0, 0))
    return pl.pallas_call(
        body, name="ssm_scan_fwd", grid=(S // tc,),
        in_specs=[blk, blk, par, par], out_specs=[blk, blk],
        out_shape=[jax.ShapeDtypeStruct((S, R, LANE), F32)] * 2,
        scratch_shapes=[pltpu.VMEM((2, R, LANE), F32)],
        compiler_params=pltpu.CompilerParams(dimension_semantics=("arbitrary",)),
    )(bu_re, bu_im, a_re, a_im)


def _scan_bwd(ds_re, ds_im, s_re, s_im, a_re, a_im):
    S, R, _ = ds_re.shape
    tc = _pick(S, 256, SUBLANE)
    nc = S // tc

    def body(dre, dim_, sre, sim, are, aim, lre, lim, dar, dai, carry):
        @pl.when(pl.program_id(0) == 0)
        def _():
            carry[...] = jnp.zeros_like(carry)
            dar[...] = jnp.zeros_like(dar)
            dai[...] = jnp.zeros_like(dai)

        ar, ai = are[...], aim[...]

        def step(tt, c):
            t = tc - 1 - tt
            lr, li, gr, gi = c
            sr, si = sre[t], sim[t]
            gr = gr + lr * sr + li * si
            gi = gi + li * sr - lr * si
            nlr = dre[t] + ar * lr + ai * li
            nli = dim_[t] + ar * li - ai * lr
            lre[t] = nlr
            lim[t] = nli
            return nlr, nli, gr, gi

        lr, li, gr, gi = lax.fori_loop(0, tc, step, (carry[0], carry[1], dar[...], dai[...]), unroll=8)
        carry[0] = lr
        carry[1] = li
        dar[...] = gr
        dai[...] = gi

    blk = pl.BlockSpec((tc, R, LANE), lambda i: (nc - 1 - i, 0, 0))
    par = pl.BlockSpec((R, LANE), lambda i: (0, 0))
    return pl.pallas_call(
        body, name="ssm_scan_bwd", grid=(nc,),
        in_specs=[blk, blk, blk, blk, par, par], out_specs=[blk, blk, par, par],
        out_shape=[jax.ShapeDtypeStruct((S, R, LANE), F32)] * 2 + [jax.ShapeDtypeStruct((R, LANE), F32)] * 2,
        scratch_shapes=[pltpu.VMEM((2, R, LANE), F32)],
        compiler_params=pltpu.CompilerParams(dimension_semantics=("arbitrary",)),
    )(ds_re, ds_im, s_re, s_im, a_re, a_im)


def _shift_down(z, k):
    t = lax.broadcasted_iota(I32, z.shape, 0)
    return jnp.where(t >= k, pltpu.roll(z, k, 0), 0.0)


def _shift_up(z, k):
    n = z.shape[0]
    t = lax.broadcasted_iota(I32, z.shape, 0)
    return jnp.where(t < n - k, pltpu.roll(z, n - k, 0), 0.0)


def _conv_fwd(proj, cw, cb):
    _, S, W = proj.shape
    ct = _pick(W, 256, LANE)

    def fn(bg, cg, val, w, b):
        z = cg * val
        conv = b + w[0:1] * _shift_down(z, 2) + w[1:2] * _shift_down(z, 1) + w[2:3] * z
        return bg * conv, conv

    sl = lambda s: (proj, (None, S, ct), lambda j, s=s: (s, 0, j))
    col = ((S, ct), lambda j: (0, j))
    return _ew("conv_fwd", (W // ct,),
               [sl(1), sl(2), sl(3), (cw, (3, ct), lambda j: (0, j)), (cb, (1, ct), lambda j: (0, j))],
               [((S, W), BF) + col, ((S, W), F32) + col], fn)


def _conv_bwd(dyb, proj, conv, cw):
    _, S, W = proj.shape
    ct = _pick(W, 256, LANE)

    def fn(dy, bg, cg, val, cv, w):
        z = cg * val
        z1, z2 = _shift_down(z, 1), _shift_down(z, 2)
        dconv = dy * bg
        dz = w[2:3] * dconv + w[1:2] * _shift_up(dconv, 1) + w[0:1] * _shift_up(dconv, 2)
        dw = jnp.concatenate([jnp.sum(dconv * z2, axis=0, keepdims=True), jnp.sum(dconv * z1, axis=0, keepdims=True),
                              jnp.sum(dconv * z, axis=0, keepdims=True)], axis=0)
        return dy * cv, dz * val, dz * cg, dw, jnp.sum(dconv, axis=0, keepdims=True)

    sl = lambda s: (proj, (None, S, ct), lambda j, s=s: (s, 0, j))
    col = ((S, ct), lambda j: (0, j))
    return _ew("conv_bwd", (W // ct,),
               [(dyb,) + col, sl(1), sl(2), sl(3), (conv,) + col, (cw, (3, ct), lambda j: (0, j))],
               [((S, W), BF) + col, ((S, W), BF) + col, ((S, W), BF) + col,
                ((3, W), F32, (3, ct), lambda j: (0, j)), ((1, W), F32, (1, ct), lambda j: (0, j))], fn)


def _plain(accs, sides):
    return accs


def kernel(x, ffn1_norm, ffn1_w_gate, ffn1_w_up, ffn1_w_down, mix_norm, w_in, ssm_lambda_re, ssm_lambda_im, ssm_log_dt, ssm_b_re, ssm_b_im, ssm_c_re, ssm_c_im, ssm_d, ssm_w_glu, ssm_b_glu, ssm_w_out, conv_w, conv_b, conv_w_out, w_o, ffn2_norm, ffn2_w_gate, ffn2_w_up, ffn2_w_down, final_norm, loss_target, m_ffn1_norm, m_ffn1_w_gate, m_ffn1_w_up, m_ffn1_w_down, m_mix_norm, m_w_in, m_ssm_lambda_re, m_ssm_lambda_im, m_ssm_log_dt, m_ssm_b_re, m_ssm_b_im, m_ssm_c_re, m_ssm_c_im, m_ssm_d, m_ssm_w_glu, m_ssm_b_glu, m_ssm_w_out, m_conv_w, m_conv_b, m_conv_w_out, m_w_o, m_ffn2_norm, m_ffn2_w_gate, m_ffn2_w_up, m_ffn2_w_down, m_final_norm, v_ffn1_norm, v_ffn1_w_gate, v_ffn1_w_up, v_ffn1_w_down, v_mix_norm, v_w_in, v_ssm_lambda_re, v_ssm_lambda_im, v_ssm_log_dt, v_ssm_b_re, v_ssm_b_im, v_ssm_c_re, v_ssm_c_im, v_ssm_d, v_ssm_w_glu, v_ssm_b_glu, v_ssm_w_out, v_conv_w, v_conv_b, v_conv_w_out, v_w_o, v_ffn2_norm, v_ffn2_w_gate, v_ffn2_w_up, v_ffn2_w_down, v_final_norm):
    P = dict(ffn1_norm=ffn1_norm, ffn1_w_gate=ffn1_w_gate, ffn1_w_up=ffn1_w_up, ffn1_w_down=ffn1_w_down, mix_norm=mix_norm, w_in=w_in, ssm_lambda_re=ssm_lambda_re, ssm_lambda_im=ssm_lambda_im, ssm_log_dt=ssm_log_dt, ssm_b_re=ssm_b_re, ssm_b_im=ssm_b_im, ssm_c_re=ssm_c_re, ssm_c_im=ssm_c_im, ssm_d=ssm_d, ssm_w_glu=ssm_w_glu, ssm_b_glu=ssm_b_glu, ssm_w_out=ssm_w_out, conv_w=conv_w, conv_b=conv_b, conv_w_out=conv_w_out, w_o=w_o, ffn2_norm=ffn2_norm, ffn2_w_gate=ffn2_w_gate, ffn2_w_up=ffn2_w_up, ffn2_w_down=ffn2_w_down, final_norm=final_norm)
    M = dict(ffn1_norm=m_ffn1_norm, ffn1_w_gate=m_ffn1_w_gate, ffn1_w_up=m_ffn1_w_up, ffn1_w_down=m_ffn1_w_down, mix_norm=m_mix_norm, w_in=m_w_in, ssm_lambda_re=m_ssm_lambda_re, ssm_lambda_im=m_ssm_lambda_im, ssm_log_dt=m_ssm_log_dt, ssm_b_re=m_ssm_b_re, ssm_b_im=m_ssm_b_im, ssm_c_re=m_ssm_c_re, ssm_c_im=m_ssm_c_im, ssm_d=m_ssm_d, ssm_w_glu=m_ssm_w_glu, ssm_b_glu=m_ssm_b_glu, ssm_w_out=m_ssm_w_out, conv_w=m_conv_w, conv_b=m_conv_b, conv_w_out=m_conv_w_out, w_o=m_w_o, ffn2_norm=m_ffn2_norm, ffn2_w_gate=m_ffn2_w_gate, ffn2_w_up=m_ffn2_w_up, ffn2_w_down=m_ffn2_w_down, final_norm=m_final_norm)
    V = dict(ffn1_norm=v_ffn1_norm, ffn1_w_gate=v_ffn1_w_gate, ffn1_w_up=v_ffn1_w_up, ffn1_w_down=v_ffn1_w_down, mix_norm=v_mix_norm, w_in=v_w_in, ssm_lambda_re=v_ssm_lambda_re, ssm_lambda_im=v_ssm_lambda_im, ssm_log_dt=v_ssm_log_dt, ssm_b_re=v_ssm_b_re, ssm_b_im=v_ssm_b_im, ssm_c_re=v_ssm_c_re, ssm_c_im=v_ssm_c_im, ssm_d=v_ssm_d, ssm_w_glu=v_ssm_w_glu, ssm_b_glu=v_ssm_b_glu, ssm_w_out=v_ssm_w_out, conv_w=v_conv_w, conv_b=v_conv_b, conv_w_out=v_conv_w_out, w_o=v_w_o, ffn2_norm=v_ffn2_norm, ffn2_w_gate=v_ffn2_w_gate, ffn2_w_up=v_ffn2_w_up, ffn2_w_down=v_ffn2_w_down, final_norm=v_final_norm)
    names = list(P)
    sharded = ["ffn1_w_gate", "ffn1_w_up", "ffn1_w_down", "w_in", "ssm_w_glu", "ssm_w_out", "conv_w_out", "w_o",
               "ffn2_w_gate", "ffn2_w_up", "ffn2_w_down"]
    replicated = [n for n in names if n not in sharded and n != "conv_w"]

    S, D = x.shape[1], x.shape[2]
    W = ssm_d.shape[0]
    Dc = D // NDEV
    G, N = ssm_lambda_re.shape
    GN = G * N
    rows = GN // LANE
    xh = x.reshape(S, D)
    target = loss_target.reshape(S, D)
    xi, yi, ci = lax.axis_index("x"), lax.axis_index("y"), lax.axis_index("c")
    c_arr = jnp.reshape(ci, (1,)).astype(I32)
    q_arr = jnp.reshape(2 * xi + yi, (1,)).astype(I32)
    row = lambda v: v.reshape(1, -1)

    transposed = ("ffn1_w_gate", "ffn1_w_up", "ffn2_w_gate", "ffn2_w_up")
    local = lambda table, n: table[n].T if n in transposed else table[n]

    groups = [sharded[0:2], sharded[2:3], sharded[3:8] + ["conv_w"], sharded[8:11]]
    started = {}
    me = 4 * xi + 2 * yi + ci

    def gather_start(gi):
        srcs = [conv_w if n == "conv_w" else local(P, n).astype(BF) for n in groups[gi]]
        lands = [lax.dynamic_update_slice(lax.empty((NDEV,) + s.shape, s.dtype), s[None], (me,) + (0,) * s.ndim)
                 for s in srcs]
        after = [started[gi - 1][4]] if gi else []
        started[gi] = _split_start("gather_start_%d" % gi, srcs, lands, _gather_copies, 4, after)
        return started[gi][4]

    def gathered(gi, after):
        _, lands = _split_wait("gather_wait_%d" % gi, started[gi], _gather_copies, after)
        return _gather_forward("gather_forward_%d" % gi, lands)

    tm = _pick(S, M_TILE, SUBLANE)
    th = _pick(S, M_TILE // 2, SUBLANE)
    tq = _pick(S, 2 * M_TILE, SUBLANE)
    tk = _pick(D, K_TILE, LANE)
    ts = _pick(S, K_TILE, SUBLANE)
    tn = _pick(D, 1024, LANE)

    tokens = [gather_start(gi) for gi in range(len(groups))]
    u1, r1 = _rms_fwd("rms1", xh, row(ffn1_norm), deps=tokens)
    wg1, wu1 = gathered(0, u1)
    h1, ffn1_saved, wd1 = _ffn_fwd("ffn1", u1, xh, wg1, wu1, lambda after: gathered(1, after)[0])
    u2, r2 = _rms_fwd("rms2", h1, row(mix_norm))
    w_in_f, w_glu_f, w_so, w_co, w_o_f, cw_f = gathered(2, u2)
    w_glu_f = w_glu_f.reshape(W, W)
    w_o_f = w_o_f.reshape(D, D)
    cw = jnp.transpose(cw_f, (1, 0, 2)).reshape(3, W)
    (proj,) = _mm(
        "in_proj", (NDEV, S // tm, 1, D // tk),
        [(u2, (tm, tk), lambda b, i, j, k: (i, k)), (w_in_f, (None, tk, W), lambda b, i, j, k: (b, k, 0))],
        [(0, 1, NN, 0)], [], [((NDEV, S, W), F32, (None, tm, W), lambda b, i, j, k: (b, i, 0))], _plain, [(tm, W)])

    ssm_in = (ssm_lambda_re, ssm_lambda_im, ssm_log_dt, ssm_b_re, ssm_b_im, ssm_c_re, ssm_c_im)
    (a_re, a_im, bd_re, bd_im, cd_re, cd_imn), ssm_vjp = jax.vjp(_ssm_params, *ssm_in)
    tiles, tch, tst = bd_re.shape
    bd_re_b, bd_im_b, cd_re_b, cd_imn_b = (t.astype(BF) for t in (bd_re, bd_im, cd_re, cd_imn))
    v_bf = proj[0].astype(BF)
    st_blk = lambda arr: (arr, (tq, tst), lambda b, i, j, k: (i, b))
    ch_blk = lambda arr: (arr, (tq, tch), lambda b, i, j, k: (i, b))
    bd_blk = lambda arr: (arr, (None, tch, tst), lambda b, i, j, k: (b, 0, 0))
    cd_blk = lambda arr: (arr, (None, tst, tch), lambda b, i, j, k: (b, 0, 0))
    st_out = lambda dt: ((S, GN), dt, (tq, tst), lambda b, i, j, k: (i, b))
    ch_out = lambda dt: ((S, W), dt, (tq, tch), lambda b, i, j, k: (i, b))
    d_blk = (row(ssm_d), (1, tch), lambda b, i, j, k: (0, b))
    v_blk = (proj, (None, tq, tch), lambda b, i, j, k: (0, i, b))

    bu_re, bu_im = _mm("ssm_bu", (tiles, S // tq, 1, 1), [ch_blk(v_bf), bd_blk(bd_re_b), bd_blk(bd_im_b)],
                       [(0, 1, NN, 0), (0, 2, NN, 1)], [], [st_out(F32), st_out(F32)], _plain, [(tq, tst)] * 2)
    s_re3, s_im3 = _scan_fwd(bu_re.reshape(S, rows, LANE), bu_im.reshape(S, rows, LANE), a_re, a_im)
    s_re_b = s_re3.reshape(S, GN).astype(BF)
    s_im_b = s_im3.reshape(S, GN).astype(BF)

    def y0_epilogue(accs, sides):
        y0 = accs[0] + sides[1] * sides[0]
        return y0, _gelu(y0)

    y0, y1 = _mm("ssm_y0", (tiles, S // tq, 1, 1),
                 [st_blk(s_re_b), st_blk(s_im_b), cd_blk(cd_re_b), cd_blk(cd_imn_b)],
                 [(0, 2, NN, 0), (1, 3, NN, 0)], [v_blk, d_blk], [ch_out(F32), ch_out(BF)], y0_epilogue, [(tq, tch)])

    tw = _pick(W, 512, LANE)

    def glu_epilogue(accs, sides):
        q = accs[0] + sides[1]
        return q, _gelu(sides[0]) * _sigmoid(q)

    q_pre, y2 = _mm("ssm_glu", (1, S // tm, W // tw, 1),
                    [(y1, (tm, W), lambda b, i, j, k: (i, 0)), (w_glu_f, (W, tw), lambda b, i, j, k: (0, j))],
                    [(0, 1, NN, 0)],
                    [(y0, (tm, tw), lambda b, i, j, k: (i, j)), (row(ssm_b_glu), (1, tw), lambda b, i, j, k: (0, j))],
                    [((S, W), F32, (tm, tw), lambda b, i, j, k: (i, j)), ((S, W), BF, (tm, tw), lambda b, i, j, k: (i, j))],
                    glu_epilogue, [(tm, tw)])

    yb, conv = _conv_fwd(proj, cw, row(conv_b))

    per = W // Dc
    ga_blk = (proj, (None, tm, Dc), lambda b, i, j, k: (4 + b // per, i, b % per))
    gb_blk = (proj, (None, tm, Dc), lambda b, i, j, k: (6 + b // per, i, b % per))
    dc_out = ((S, D), BF, (tm, Dc), lambda b, i, j, k: (i, b))

    def merge_epilogue(accs, sides):
        za, zb = accs
        return _sigmoid(sides[0]) * za + _sigmoid(sides[1]) * zb, za, zb

    merged, z_a, z_b = _mm(
        "mix_merge", (NDEV, S // tm, 1, 1),
        [(y2, (tm, W), lambda b, i, j, k: (i, 0)), (yb, (tm, W), lambda b, i, j, k: (i, 0)),
         (w_so, (None, W, Dc), lambda b, i, j, k: (b, 0, 0)), (w_co, (None, W, Dc), lambda b, i, j, k: (b, 0, 0))],
        [(0, 2, NN, 0), (1, 3, NN, 1)], [ga_blk, gb_blk], [dc_out, dc_out, dc_out], merge_epilogue, [(tm, Dc)] * 2)

    (h2,) = _mm("mix_out", (1, S // tm, D // tn, D // tk),
                [(merged, (tm, tk), lambda b, i, j, k: (i, k)), (w_o_f, (tk, tn), lambda b, i, j, k: (k, j))],
                [(0, 1, NN, 0)], [(h1, (tm, tn), lambda b, i, j, k: (i, j))],
                [((S, D), F32, (tm, tn), lambda b, i, j, k: (i, j))],
                lambda accs, sides: [sides[0] + accs[0]], [(tm, tn)])

    u3, r3 = _rms_fwd("rms3", h2, row(ffn2_norm))
    wg2, wu2, wd2 = gathered(3, u3)
    h3, ffn2_saved, _ = _ffn_fwd("ffn2", u3, h2, wg2, wu2, wd2)
    loss_vec, dh3, dh3_half, d_final_norm = _loss_head("loss_head", h3, row(final_norm), target)
    loss = lax.psum(loss_vec[0, 0], ("x", "y", "c"))

    grads, deltas, new_m, new_v = {}, {}, {}, {}

    def rs_sibling_start(tag, parts):
        lands = [lax.empty((4,) + p.shape[1:], p.dtype) for p in parts]
        return _split_start("rs_sibling_start_" + tag, parts, lands, _sibling_copies, 4)

    def rs_chips_start(tag, sibling_begun, after):
        parts, lands = _split_wait("rs_sibling_wait_" + tag, sibling_begun, _sibling_copies, after)
        sums = [_sum_sibling("rs_sum_%s_%d" % (tag, a), p, land, c_arr) for a, (p, land) in enumerate(zip(parts, lands))]
        lands2 = [lax.empty((3,) + sm.shape[1:], sm.dtype) for sm in sums]
        return _split_start("rs_chips_start_" + tag, sums, lands2, _chips_copies, 3)

    def rs_end(tag, group, begun, after):
        sums, lands2 = _split_wait("rs_chips_wait_" + tag, begun, _chips_copies, after)
        for n, sm, land2 in zip(group, sums, lands2):
            res = _finish_sharded("adamw_" + n, sm, land2, q_arr, local(P, n), local(M, n), local(V, n))
            grads[n], deltas[n], new_m[n], new_v[n] = [t.T if n in transposed else t for t in res]

    f2_dwd, f2_dwgu, f2_du = _ffn_bwd("ffn2", dh3_half, u3, ffn2_saved, wg2, wu2, wd2)
    dwd2 = f2_dwd()
    dwg2, dwu2 = f2_dwgu()
    sib_ffn2 = rs_sibling_start("ffn2", [dwg2, dwu2, dwd2])
    du3 = f2_du(deps=[sib_ffn2[4]])
    dh2, dh2_b, d_ffn2_norm = _rms_bwd("rms3_bwd", du3, h2, r3, row(ffn2_norm), dh3, 1.0)
    rs_ffn2 = rs_chips_start("ffn2", sib_ffn2, dh2)

    dg_out = ((2, S, W), BF, (None, tm, Dc), lambda b, i, j, k: (j // per, i, j % per))
    ga_blk2 = (proj, (None, tm, Dc), lambda b, i, j, k: (4 + j // per, i, j % per))
    gb_blk2 = (proj, (None, tm, Dc), lambda b, i, j, k: (6 + j // per, i, j % per))
    dcj = lambda arr: (arr, (tm, Dc), lambda b, i, j, k: (i, j))
    dcj_out = ((S, D), BF, (tm, Dc), lambda b, i, j, k: (i, j))

    def dmerge_epilogue(accs, sides):
        dm = accs[0]
        sa, sb = _sigmoid(sides[0]), _sigmoid(sides[1])
        za, zb = sides[2].astype(F32), sides[3].astype(F32)
        return dm * sa, dm * sb, dm * za * sa * (1.0 - sa), dm * zb * sb * (1.0 - sb)

    dz_a, dz_b, dga, dgb = _mm(
        "mix_out_dx", (1, S // tm, NDEV, D // tk),
        [(dh2_b, (tm, tk), lambda b, i, j, k: (i, k)), (w_o_f, (Dc, tk), lambda b, i, j, k: (j, k))],
        [(0, 1, NT, 0)], [ga_blk2, gb_blk2, dcj(z_a), dcj(z_b)], [dcj_out, dcj_out, dg_out, dg_out],
        dmerge_epilogue, [(tm, Dc)], deps=[rs_ffn2[4]])

    td = _pick(D, M_TILE, LANE)
    (dw_o,) = _mm("mix_out_dw", (1, D // td, D // tn, S // ts),
                  [(merged, (ts, td), lambda b, i, j, k: (k, i)), (dh2_b, (ts, tn), lambda b, i, j, k: (k, j))],
                  [(0, 1, TN, 0)], [], [((D, D), BF, (td, tn), lambda b, i, j, k: (i, j))], _plain, [(td, tn)])

    wout = ((NDEV, W, Dc), BF, (None, W, Dc), lambda b, i, j, k: (b, 0, 0))
    dw_so, dw_co = _mm(
        "mix_merge_dw", (NDEV, 1, 1, S // ts),
        [(y2, (ts, W), lambda b, i, j, k: (k, 0)), (yb, (ts, W), lambda b, i, j, k: (k, 0)),
         (dz_a, (ts, Dc), lambda b, i, j, k: (k, b)), (dz_b, (ts, Dc), lambda b, i, j, k: (k, b))],
        [(0, 2, TN, 0), (1, 3, TN, 1)], [], [wout, wout], _plain, [(W, Dc)] * 2)

    def dglu_epilogue(accs, sides):
        dy2, dyb = accs
        sq = _sigmoid(sides[1])
        return dy2 * _gelu(sides[0]) * sq * (1.0 - sq), dy2 * sq, dyb

    full_w = lambda arr: (arr, (th, W), lambda b, i, j, k: (i, 0))
    full_w_out = lambda dt: ((S, W), dt, (th, W), lambda b, i, j, k: (i, 0))
    dq, dy1p, dyb = _mm(
        "mix_merge_dx", (1, S // th, 1, NDEV),
        [(dz_a, (th, Dc), lambda b, i, j, k: (i, k)), (dz_b, (th, Dc), lambda b, i, j, k: (i, k)),
         (w_so, (None, W, Dc), lambda b, i, j, k: (k, 0, 0)), (w_co, (None, W, Dc), lambda b, i, j, k: (k, 0, 0))],
        [(0, 2, NT, 0), (1, 3, NT, 1)], [full_w(y0), full_w(q_pre)], [full_w_out(BF), full_w_out(F32), full_w_out(F32)],
        dglu_epilogue, [(th, W)] * 2)

    def dy0_epilogue(accs, sides):
        dy0 = (sides[0] + accs[0]) * _gelu_grad(sides[1])
        return dy0, dy0

    wj = lambda arr: (arr, (tm, tw), lambda b, i, j, k: (i, j))
    dy0, dy0_b = _mm("ssm_glu_dx", (1, S // tm, W // tw, 1),
                     [(dq, (tm, W), lambda b, i, j, k: (i, 0)), (w_glu_f, (tw, W), lambda b, i, j, k: (j, 0))],
                     [(0, 1, NT, 0)], [wj(dy1p), wj(y0)],
                     [((S, W), F32, (tm, tw), lambda b, i, j, k: (i, j)), ((S, W), BF, (tm, tw), lambda b, i, j, k: (i, j))],
                     dy0_epilogue, [(tm, tw)])

    (dw_glu,) = _mm("ssm_glu_dw", (1, W // tw, 1, S // ts),
                    [(y1, (ts, tw), lambda b, i, j, k: (k, i)), (dq, (ts, W), lambda b, i, j, k: (k, 0))],
                    [(0, 1, TN, 0)], [], [((W, W), BF, (tw, W), lambda b, i, j, k: (i, 0))], _plain, [(tw, W)])

    tr = _pick(S, 256, SUBLANE)
    rw = ((tr, W), lambda i: (i, 0))
    vec_w = ((1, W), F32, (1, W), lambda i: (0, 0))
    d_b_glu, d_ssm_d = _ew(
        "ssm_colsums", (S // tr,), [(dq,) + rw, (dy0,) + rw, (proj, (None, tr, W), lambda i: (0, i, 0))],
        [vec_w, vec_w],
        lambda dqv, dyv, vv: (jnp.sum(dqv.astype(F32), axis=0, keepdims=True), jnp.sum(dyv * vv, axis=0, keepdims=True)),
        acc=(0, 1))

    ds_re, ds_im = _mm("ssm_ds", (tiles, S // tq, 1, 1), [ch_blk(dy0_b), cd_blk(cd_re_b), cd_blk(cd_imn_b)],
                       [(0, 1, NT, 0), (0, 2, NT, 1)], [], [st_out(F32), st_out(F32)], _plain, [(tq, tst)] * 2)
    lam_re3, lam_im3, da_re, da_im = _scan_bwd(ds_re.reshape(S, rows, LANE), ds_im.reshape(S, rows, LANE),
                                               s_re3, s_im3, a_re, a_im)
    lam_re_b = lam_re3.reshape(S, GN).astype(BF)
    lam_im_b = lam_im3.reshape(S, GN).astype(BF)

    (dv,) = _mm("ssm_dv", (tiles, S // tq, 1, 1),
                [st_blk(lam_re_b), st_blk(lam_im_b), bd_blk(bd_re_b), bd_blk(bd_im_b)],
                [(0, 2, NT, 0), (1, 3, NT, 0)], [ch_blk(dy0), d_blk], [ch_out(BF)],
                lambda accs, sides: [accs[0] + sides[0] * sides[1]], [(tq, tch)])

    tok_ch = lambda arr: (arr, (ts, tch), lambda b, i, j, k: (k, b))
    tok_st = lambda arr: (arr, (ts, tst), lambda b, i, j, k: (k, b))
    bd_out = ((tiles, tch, tst), F32, (None, tch, tst), lambda b, i, j, k: (b, 0, 0))
    cd_out = ((tiles, tst, tch), F32, (None, tst, tch), lambda b, i, j, k: (b, 0, 0))
    dbd_re, dbd_im = _mm("ssm_dbd", (tiles, 1, 1, S // ts), [tok_ch(v_bf), tok_st(lam_re_b), tok_st(lam_im_b)],
                         [(0, 1, TN, 0), (0, 2, TN, 1)], [], [bd_out, bd_out], _plain, [(tch, tst)] * 2)
    dcd_re, dcd_imn = _mm("ssm_dcd", (tiles, 1, 1, S // ts), [tok_st(s_re_b), tok_st(s_im_b), tok_ch(dy0_b)],
                          [(0, 2, TN, 0), (1, 2, TN, 1)], [], [cd_out, cd_out], _plain, [(tst, tch)] * 2)
    d_ssm = ssm_vjp((da_re, da_im, dbd_re, dbd_im, dcd_re, dcd_imn))

    dbg, dcg, dval, d_conv_w_full, d_conv_b = _conv_bwd(dyb, proj, conv, cw)
    dproj = jnp.concatenate([dv[None], dbg[None], dcg[None], dval[None], dga, dgb], axis=0)

    (dw_in,) = _mm("in_proj_dw", (NDEV, D // td, 1, S // ts),
                   [(u2, (ts, td), lambda b, i, j, k: (k, i)), (dproj, (None, ts, W), lambda b, i, j, k: (b, k, 0))],
                   [(0, 1, TN, 0)], [], [((NDEV, D, W), BF, (None, td, W), lambda b, i, j, k: (b, i, 0))],
                   _plain, [(td, W)])
    sib_mixer = rs_sibling_start(
        "mixer", [dw_in, dw_glu.reshape(NDEV, W // NDEV, W), dw_so, dw_co, dw_o.reshape(NDEV, Dc, D)])
    (du2,) = _mm("in_proj_dx", (1, S // tm, D // tn, NDEV),
                 [(dproj, (None, tm, W), lambda b, i, j, k: (k, i, 0)), (w_in_f, (None, tn, W), lambda b, i, j, k: (k, j, 0))],
                 [(0, 1, NT, 0)], [], [((S, D), F32, (tm, tn), lambda b, i, j, k: (i, j))], _plain, [(tm, tn)],
                 deps=[sib_mixer[4]])
    dh1, dh1_half, d_mix_norm = _rms_bwd("rms2_bwd", du2, h1, r2, row(mix_norm), dh2, 0.5)
    rs_mixer = rs_chips_start("mixer", sib_mixer, dh1)

    small = dict(mix_norm=d_mix_norm, ffn2_norm=d_ffn2_norm, final_norm=d_final_norm,
                 ssm_lambda_re=d_ssm[0], ssm_lambda_im=d_ssm[1], ssm_log_dt=d_ssm[2], ssm_b_re=d_ssm[3],
                 ssm_b_im=d_ssm[4], ssm_c_re=d_ssm[5], ssm_c_im=d_ssm[6], ssm_d=d_ssm_d, ssm_b_glu=d_b_glu,
                 conv_b=d_conv_b)
    replicated = [n for n in replicated if n != "ffn1_norm"] + ["ffn1_norm"]
    tile = SUBLANE * LANE

    def as_rows(p):
        flat = p.reshape(-1).astype(F32)
        return jnp.pad(flat, (0, -flat.shape[0] % tile)).reshape(-1, LANE)

    def pack(parts):
        return jnp.concatenate([as_rows(p) for p in parts], axis=0)

    cw_zero = jnp.zeros_like(d_conv_w_full)
    early_pk = pack([small[n] for n in replicated[:-1]] + [d_conv_w_full])
    early_land = lax.dynamic_update_slice(jnp.zeros((NDEV,) + early_pk.shape, F32), early_pk[None], (me, 0, 0))
    small_begun = _split_start("gather_small_start", [early_pk], [early_land], _everyone_copies, NDEV - 1)

    f1_dwd, f1_dwgu, f1_du = _ffn_bwd("ffn1", dh1_half, u1, ffn1_saved, wg1, wu1, wd1,
                                      deps=[rs_mixer[4], small_begun[4]])
    du1 = f1_du()
    dx, _, d_ffn1_norm = _rms_bwd("rms1_bwd", du1, xh, r1, row(ffn1_norm), dh1, 1.0)
    dwd1 = f1_dwd(deps=[d_ffn1_norm])
    late = d_ffn1_norm + 0.0 * dwd1[0, :1, :1].astype(F32)
    (late_all,) = _all_gather("gather_ffn1_norm_grad", [late.reshape(-1, LANE)])
    rs_ffn1_down = rs_chips_start("ffn1_down", rs_sibling_start("ffn1_down", [dwd1]), late_all)
    dwg1, dwu1 = f1_dwgu(deps=[rs_ffn1_down[4]])
    rs_ffn1_gate_up = rs_chips_start("ffn1_gate_up", rs_sibling_start("ffn1_gate_up", [dwg1, dwu1]), None)
    rs_end("ffn2", sharded[8:11], rs_ffn2, rs_ffn1_gate_up[4])
    rs_end("mixer", sharded[3:8], rs_mixer, [grads[n] for n in sharded[8:11]])
    _, (early_all,) = _split_wait("gather_small_wait", small_begun, _everyone_copies, [grads[n] for n in sharded[3:8]])
    small_all = jnp.concatenate([early_all, late_all], axis=1)
    g_pk, d_pk, m_pk, v_pk = _finish_replicated(
        "adamw_replicated", small_all, pack([P[n] for n in replicated[:-1]] + [cw_zero, P["ffn1_norm"]]),
        pack([M[n] for n in replicated[:-1]] + [cw_zero, M["ffn1_norm"]]),
        pack([V[n] for n in replicated[:-1]] + [cw_zero + 1.0, V["ffn1_norm"]]))

    def unpack(pk, r0, like):
        nr = -(-like.size // tile) * SUBLANE
        return pk[r0:r0 + nr].reshape(-1)[:like.size].reshape(like.shape), r0 + nr

    r0 = 0
    for n in replicated[:-1] + ["conv_w", "ffn1_norm"]:
        like = d_conv_w_full if n == "conv_w" else P[n]
        for store, pk in ((grads, g_pk), (deltas, d_pk), (new_m, m_pk), (new_v, v_pk)):
            store[n], r1 = unpack(pk, r0, like)
        r0 = r1
    g_cw_full = grads["conv_w"]
    cwl = conv_w.shape[1]
    g_cw = lax.dynamic_slice_in_dim(g_cw_full, me * cwl, cwl, axis=1)
    full3 = ((3, cwl), lambda i: (0, 0))
    grads["conv_w"], deltas["conv_w"], new_m["conv_w"], new_v["conv_w"] = _ew(
        "adamw_conv_w", (1,), [(g_cw,) + full3, (conv_w,) + full3, (m_conv_w,) + full3, (v_conv_w,) + full3],
        [((3, cwl), F32) + full3] * 4, lambda g, w, m, v: (g,) + _adamw(w, g, m, v))
    rs_end("ffn1_down", sharded[2:3], rs_ffn1_down, [g_pk, grads["conv_w"]])
    rs_end("ffn1_gate_up", sharded[0:2], rs_ffn1_gate_up, grads["ffn1_w_down"])

    return (loss, dx.reshape(x.shape), *[grads[n] for n in names], *[deltas[n] for n in names],
            *[new_m[n] for n in names], *[new_v[n] for n in names])
```

```python
import math

import jax
import jax.numpy as jnp
from jax import lax
from jax.experimental import pallas as pl
from jax.experimental.pallas import tpu as pltpu

F32 = jnp.float32
BF = jnp.bfloat16
I32 = jnp.int32
MESH = pl.DeviceIdType.MESH
LANE = 128
SUBLANE = 8
NDEV = 8
EW_BLOCK = 256 * 1024
M_TILE = 1024
K_TILE = 2048
DMA_CHUNKS = 4
EPS = 1e-6
ADAM_LR, ADAM_B1, ADAM_B2, ADAM_EPS, ADAM_WD, ADAM_STEP = 0.001, 0.9, 0.999, 1e-08, 0.01, 10
NN = ((1,), (0,))
NT = ((1,), (1,))
TN = ((0,), (0,))
HBM = pl.BlockSpec(memory_space=pltpu.HBM)


def _pick(n, pref, mult):
    t = min(pref, n)
    t -= t % mult
    while t >= mult:
        if n % t == 0:
            return t
        t -= mult
    return n


def _sigmoid(x):
    return 1.0 / (1.0 + jnp.exp(-x))


_GELU_C = math.sqrt(2.0 / math.pi)


def _gelu(x):
    return 0.5 * x * (1.0 + jnp.tanh(_GELU_C * (x + 0.044715 * x * x * x)))


def _gelu_grad(x):
    t = jnp.tanh(_GELU_C * (x + 0.044715 * x * x * x))
    return 0.5 * (1.0 + t) + 0.5 * x * (1.0 - t * t) * _GELU_C * (1.0 + 3.0 * 0.044715 * x * x)


def _dep_specs(deps, rank):
    return [(d, d.shape, lambda *_, nd=d.ndim: (0,) * nd) for d in deps]


def _mm(name, grid, ops, pairs, sides, outs, epilogue, acc_shapes, deps=()):
    nk = grid[-1]
    n_ops, n_sides, n_outs = len(ops), len(sides), len(outs)
    dep_specs = _dep_specs(deps, len(grid))
    n_deps = len(dep_specs)

    def body(*refs):
        op_refs = refs[:n_ops]
        side_refs = refs[n_ops:n_ops + n_sides]
        out_refs = refs[n_ops + n_sides + n_deps:n_ops + n_sides + n_deps + n_outs]
        acc_refs = refs[n_ops + n_sides + n_deps + n_outs:]

        def partials():
            res = [None] * len(acc_shapes)
            for ia, ib, dims, ai in pairs:
                p = lax.dot_general(op_refs[ia][...], op_refs[ib][...], (dims, ((), ())),
                                    preferred_element_type=F32)
                res[ai] = p if res[ai] is None else res[ai] + p
            return res

        def finish(accs):
            vals = epilogue(accs, [s[...] for s in side_refs])
            for o, v in zip(out_refs, vals):
                o[...] = v.astype(o.dtype)

        if nk == 1:
            finish(partials())
        else:
            k = pl.program_id(len(grid) - 1)

            @pl.when(k == 0)
            def _():
                for a, p in zip(acc_refs, partials()):
                    a[...] = p

            @pl.when(k > 0)
            def _():
                for a, p in zip(acc_refs, partials()):
                    a[...] += p

            @pl.when(k == nk - 1)
            def _():
                finish([a[...] for a in acc_refs])

    return pl.pallas_call(
        body, name=name, grid=grid,
        in_specs=[pl.BlockSpec(b, m) for (_, b, m) in list(ops) + list(sides) + dep_specs],
        out_specs=[pl.BlockSpec(b, m) for (_, _, b, m) in outs],
        out_shape=[jax.ShapeDtypeStruct(s, d) for (s, d, _, _) in outs],
        scratch_shapes=[pltpu.VMEM(s, F32) for s in acc_shapes] if nk > 1 else [],
        compiler_params=pltpu.CompilerParams(
            dimension_semantics=("parallel",) * (len(grid) - 1) + ("arbitrary",)),
    )(*[a for (a, _, _) in list(ops) + list(sides) + dep_specs])


def _ew(name, grid, ins, outs, fn, acc=(), deps=()):
    n_in = len(ins)
    dep_specs = _dep_specs(deps, len(grid))

    def body(*refs):
        vals = fn(*[r[...] for r in refs[:n_in]])
        first = pl.program_id(0) == 0
        for idx, (o, v) in enumerate(zip(refs[n_in + len(dep_specs):], vals)):
            if idx in acc:
                @pl.when(first)
                def _(o=o, v=v):
                    o[...] = v.astype(o.dtype)

                @pl.when(jnp.logical_not(first))
                def _(o=o, v=v):
                    o[...] += v.astype(o.dtype)
            else:
                o[...] = v.astype(o.dtype)

    return pl.pallas_call(
        body, name=name, grid=grid,
        in_specs=[pl.BlockSpec(b, m) for (_, b, m) in list(ins) + dep_specs],
        out_specs=[pl.BlockSpec(b, m) for (_, _, b, m) in outs],
        out_shape=[jax.ShapeDtypeStruct(s, d) for (s, d, _, _) in outs],
        compiler_params=pltpu.CompilerParams(
            dimension_semantics=(("arbitrary",) if acc else ("parallel",)) * len(grid)),
    )(*[a for (a, _, _) in list(ins) + dep_specs])


def _position():
    x, y, c = lax.axis_index("x"), lax.axis_index("y"), lax.axis_index("c")
    chips = [(1 - x, y), (x, 1 - y), (1 - x, 1 - y)]
    return x, y, c, chips


def _all_gather(name, shards):
    n = len(shards)

    def body(*refs):
        xs, outs = refs[:n], refs[n:2 * n]
        send_sems, recv_sems, local_sems = refs[2 * n:]
        x, y, c, chips = _position()
        me, sibling = (x, y, c), (x, y, 1 - c)

        def copy(a, k, block, to, src=None):
            dst = outs[a].at[4 * block[0] + 2 * block[1] + block[2]]
            return pltpu.make_async_remote_copy(
                src_ref=dst if src is None else src, dst_ref=dst,
                send_sem=send_sems.at[a, k], recv_sem=recv_sems.at[a, k],
                device_id=to, device_id_type=MESH)

        mine = [pltpu.make_async_copy(xs[a], outs[a].at[4 * x + 2 * y + c], local_sems.at[a]) for a in range(n)]
        for cp in mine:
            cp.start()
        first = []
        for a in range(n):
            first.append(copy(a, 0, me, sibling, src=xs[a]))
            first += [copy(a, 1 + j, me, (*chip, c), src=xs[a]) for j, chip in enumerate(chips)]
        for cp in first:
            cp.start()
        passed = []
        for a in range(n):
            for j, chip in enumerate(chips):
                copy(a, 1 + j, (*chip, c), me).wait_recv()
                cp = copy(a, 4 + j, (*chip, c), sibling)
                cp.start()
                passed.append(cp)
        for a in range(n):
            copy(a, 0, sibling, me).wait_recv()
            for j, chip in enumerate(chips):
                copy(a, 4 + j, (*chip, 1 - c), me).wait_recv()
        for cp in first + passed:
            cp.wait_send()
        for cp in mine:
            cp.wait()

    return pl.pallas_call(
        body, name=name,
        out_shape=[jax.ShapeDtypeStruct((NDEV,) + s.shape, s.dtype) for s in shards],
        in_specs=[HBM] * n, out_specs=[HBM] * n,
        scratch_shapes=[pltpu.SemaphoreType.DMA((n, 7)), pltpu.SemaphoreType.DMA((n, 7)),
                        pltpu.SemaphoreType.DMA((n,))],
    )(*shards)


SEM = pl.BlockSpec(memory_space=pltpu.SEMAPHORE)
EFFECT = pltpu.SideEffectType.DATAFLOW_SIDE_EFFECTING


def _in_hbm(v):
    return pltpu.with_memory_space_constraint(v, pltpu.HBM)


def _split_start(name, srcs, lands, make_copies, n_per, after=()):
    n = len(srcs)
    after = list(after)

    def body(*refs):
        send_sems, recv_sems = refs[2 * n + len(after)], refs[2 * n + len(after) + 1]
        for cp in make_copies(refs[:n], refs[n:2 * n], send_sems, recv_sems):
            cp.start()
        refs[-1][...] = jnp.zeros_like(refs[-1])

    outs = pl.pallas_call(
        body, name=name,
        out_shape=(pltpu.SemaphoreType.DMA((n * n_per,)), pltpu.SemaphoreType.DMA((n * n_per,)),
                   *[pltpu.HBM(v.shape, v.dtype) for v in list(srcs) + list(lands)],
                   jax.ShapeDtypeStruct((SUBLANE, LANE), F32)),
        in_specs=[HBM] * (2 * n) + [pl.BlockSpec(memory_space=pl.ANY)] * len(after),
        out_specs=(SEM, SEM, *[HBM] * (2 * n), pl.BlockSpec(memory_space=pltpu.VMEM)),
        input_output_aliases={i: 2 + i for i in range(2 * n)},
        compiler_params=pltpu.CompilerParams(has_side_effects=EFFECT),
    )(*[_in_hbm(v) for v in list(srcs) + list(lands)], *after)
    return outs[0], outs[1], list(outs[2:2 + n]), list(outs[2 + n:2 + 2 * n]), outs[-1]


def _split_wait(name, started, make_copies, after):
    send_sems, recv_sems, srcs, lands, _ = started
    n = len(srcs)

    def body(*refs):
        for cp in make_copies(refs[:n], refs[n:2 * n], refs[2 * n], refs[2 * n + 1]):
            cp.wait_send()
            cp.wait_recv()

    order = [] if after is None else list(after) if isinstance(after, (list, tuple)) else [after]
    outs = pl.pallas_call(
        body, name=name,
        out_shape=tuple(pltpu.HBM(v.shape, v.dtype) for v in srcs + lands),
        in_specs=[HBM] * (2 * n) + [SEM, SEM] + [pl.BlockSpec(memory_space=pl.ANY)] * len(order),
        out_specs=tuple([HBM] * (2 * n)),
        input_output_aliases={i: i for i in range(2 * n)},
        compiler_params=pltpu.CompilerParams(has_side_effects=EFFECT),
    )(*srcs, *lands, send_sems, recv_sems, *order)
    return list(outs[:n]), list(outs[n:])


def _gather_copies(xs, lands, send_sems, recv_sems):
    x, y, c, chips = _position()
    copies = []
    for a in range(len(xs)):
        for k, peer in enumerate([(x, y, 1 - c)] + [(*chip, c) for chip in chips]):
            copies.append(pltpu.make_async_remote_copy(
                src_ref=xs[a], dst_ref=lands[a].at[4 * x + 2 * y + c],
                send_sem=send_sems.at[4 * a + k], recv_sem=recv_sems.at[4 * a + k], device_id=peer, device_id_type=MESH))
    return copies


def _chips_copies(ps, lands, send_sems, recv_sems):
    x, y, c, chips = _position()
    copies = []
    for a in range(len(ps)):
        for j, chip in enumerate(chips):
            copies.append(pltpu.make_async_remote_copy(
                src_ref=ps[a].at[2 * chip[0] + chip[1]], dst_ref=lands[a].at[j],
                send_sem=send_sems.at[3 * a + j], recv_sem=recv_sems.at[3 * a + j], device_id=(*chip, c),
                device_id_type=MESH))
    return copies


def _sibling_copies(gs, lands, send_sems, recv_sems):
    x, y, c, _ = _position()
    copies = []
    for a in range(len(gs)):
        for q in range(4):
            copies.append(pltpu.make_async_remote_copy(
                src_ref=gs[a].at[2 * q + 1 - c], dst_ref=lands[a].at[q],
                send_sem=send_sems.at[4 * a + q], recv_sem=recv_sems.at[4 * a + q],
                device_id=(x, y, 1 - c), device_id_type=MESH))
    return copies


def _everyone_copies(xs, lands, send_sems, recv_sems):
    x, y, c, _ = _position()
    flip = lambda v, bit: 1 - v if bit else v
    copies = []
    for a in range(len(xs)):
        for k in range(1, NDEV):
            copies.append(pltpu.make_async_remote_copy(
                src_ref=xs[a], dst_ref=lands[a].at[4 * x + 2 * y + c],
                send_sem=send_sems.at[7 * a + k - 1], recv_sem=recv_sems.at[7 * a + k - 1],
                device_id=(flip(x, k & 4), flip(y, k & 2), flip(c, k & 1)), device_id_type=MESH))
    return copies


def _row_chunks(rows, dtype):
    unit = SUBLANE * (4 // jnp.dtype(dtype).itemsize)
    units = rows // unit
    if rows % unit or units < 2:
        return [(0, rows)]
    k = min(DMA_CHUNKS, units)
    sizes = [(units // k + (1 if i < units % k else 0)) * unit for i in range(k)]
    return [(sum(sizes[:i]), sz) for i, sz in enumerate(sizes)]


def _gather_forward(name, lands):
    n = len(lands)

    def body(*refs):
        ins, outs = refs[:n], refs[n:2 * n]
        send_sems, recv_sems = refs[2 * n:]
        x, y, c, chips = _position()
        whole, chunks = [], []
        for a in range(n):
            rows = _row_chunks(ins[a].shape[1], ins[a].dtype)
            for j, chip in enumerate(chips):
                slot = 4 * chip[0] + 2 * chip[1]

                def to_sibling(src, dst):
                    return pltpu.make_async_remote_copy(
                        src_ref=src, dst_ref=dst, send_sem=send_sems.at[a, j], recv_sem=recv_sems.at[a, j],
                        device_id=(x, y, 1 - c), device_id_type=MESH)

                whole.append(to_sibling(ins[a].at[slot + c], outs[a].at[slot + 1 - c]))
                chunks += [to_sibling(ins[a].at[slot + c, pl.ds(r0, nr)], outs[a].at[slot + c, pl.ds(r0, nr)])
                           for r0, nr in rows]
        for cp in chunks:
            cp.start()
        for cp in whole:
            cp.wait()

    return pl.pallas_call(
        body, name=name,
        out_shape=[jax.ShapeDtypeStruct(l.shape, l.dtype) for l in lands],
        in_specs=[HBM] * n, out_specs=[HBM] * n,
        input_output_aliases={a: a for a in range(n)},
        scratch_shapes=[pltpu.SemaphoreType.DMA((n, 3)), pltpu.SemaphoreType.DMA((n, 3))],
    )(*lands)


def _sum_sibling(name, g, land, c_arr):
    _, R, C = g.shape
    tr = _pick(R, 512, SUBLANE)

    def body(c_ref, g_ref, l_ref, o_ref):
        o_ref[...] = (g_ref[...].astype(F32) + l_ref[...].astype(F32)).astype(o_ref.dtype)

    return pl.pallas_call(
        body, name=name,
        grid_spec=pltpu.PrefetchScalarGridSpec(
            num_scalar_prefetch=1, grid=(4, R // tr),
            in_specs=[pl.BlockSpec((None, tr, C), lambda q, i, cr: (2 * q + cr[0], i, 0)),
                      pl.BlockSpec((None, tr, C), lambda q, i, cr: (q, i, 0))],
            out_specs=pl.BlockSpec((None, tr, C), lambda q, i, cr: (q, i, 0))),
        out_shape=jax.ShapeDtypeStruct((4, R, C), g.dtype),
        compiler_params=pltpu.CompilerParams(dimension_semantics=("parallel", "parallel")),
    )(c_arr, g, land)


def _adamw(w, g, m, v):
    m = ADAM_B1 * m + (1.0 - ADAM_B1) * g
    v = ADAM_B2 * v + (1.0 - ADAM_B2) * (g * g)
    m_hat = m / (1.0 - ADAM_B1 ** ADAM_STEP)
    v_hat = v / (1.0 - ADAM_B2 ** ADAM_STEP)
    delta = -ADAM_LR * (m_hat / (jnp.sqrt(v_hat) + ADAM_EPS) + ADAM_WD * w)
    return delta, m, v


def _finish_sharded(name, sums, land, q_arr, w, m, v):
    R, C = w.shape
    tr = _pick(R, 512, SUBLANE)
    tc = _pick(C, max(LANE, EW_BLOCK // tr), LANE)

    def body(q_ref, p_ref, l_ref, w_ref, m_ref, v_ref, g_out, d_out, m_out, v_out):
        g = p_ref[...].astype(F32)
        for j in range(3):
            g = g + l_ref[j].astype(F32)
        d, mn, vn = _adamw(w_ref[...], g, m_ref[...], v_ref[...])
        g_out[...] = g
        d_out[...] = d
        m_out[...] = mn
        v_out[...] = vn

    blk = pl.BlockSpec((tr, tc), lambda i, j, qr: (i, j))
    return pl.pallas_call(
        body, name=name,
        grid_spec=pltpu.PrefetchScalarGridSpec(
            num_scalar_prefetch=1, grid=(R // tr, C // tc),
            in_specs=[pl.BlockSpec((None, tr, tc), lambda i, j, qr: (qr[0], i, j)),
                      pl.BlockSpec((3, tr, tc), lambda i, j, qr: (0, i, j)), blk, blk, blk],
            out_specs=[blk] * 4),
        out_shape=[jax.ShapeDtypeStruct((R, C), F32)] * 4,
        compiler_params=pltpu.CompilerParams(dimension_semantics=("parallel", "parallel")),
    )(q_arr, sums, land, w, m, v)


def _finish_replicated(name, gathered, w, m, v):
    _, R, C = gathered.shape
    tr = _pick(R, 256, SUBLANE)

    def fn(gv, wv, mv, vv):
        g = gv[0]
        for d in range(1, NDEV):
            g = g + gv[d]
        dl, mn, vn = _adamw(wv, g, mv, vv)
        return g, dl, mn, vn

    row = ((tr, C), lambda i: (i, 0))
    return _ew(name, (R // tr,),
               [(gathered, (NDEV, tr, C), lambda i: (0, i, 0)), (w,) + row, (m,) + row, (v,) + row],
               [((R, C), F32) + row] * 4, fn)


def _rms_fwd(name, h, g, deps=()):
    S, D = h.shape
    tr = _pick(S, 256, SUBLANE)

    def fn(hv, gv):
        r = lax.rsqrt(jnp.mean(hv * hv, axis=-1, keepdims=True) + EPS)
        return hv * r * gv, r

    return _ew(name, (S // tr,),
               [(h, (tr, D), lambda i: (i, 0)), (g, (1, D), lambda i: (0, 0))],
               [((S, D), BF, (tr, D), lambda i: (i, 0)), ((S, 1), F32, (tr, 1), lambda i: (i, 0))], fn, deps=deps)


def _rms_bwd(name, du, h, r, g, dres, scale):
    S, D = h.shape
    tr = _pick(S, 256, SUBLANE)

    def fn(duv, hv, rv, gv, drv):
        xn = hv * rv
        dxn = duv * gv
        dh = drv + rv * (dxn - xn * jnp.mean(dxn * xn, axis=-1, keepdims=True))
        return dh, scale * dh, jnp.sum(duv * xn, axis=0, keepdims=True)

    row = ((tr, D), lambda i: (i, 0))
    return _ew(name, (S // tr,),
               [(du,) + row, (h,) + row, (r, (tr, 1), lambda i: (i, 0)), (g, (1, D), lambda i: (0, 0)), (dres,) + row],
               [((S, D), F32) + row, ((S, D), BF) + row, ((1, D), F32, (1, D), lambda i: (0, 0))], fn, acc=(2,))


def _loss_head(name, h, g, target):
    S, D = h.shape
    tr = _pick(S, 256, SUBLANE)

    def fn(hv, gv, tv):
        r = lax.rsqrt(jnp.mean(hv * hv, axis=-1, keepdims=True) + EPS)
        xn = hv * r
        diff = xn * gv - tv
        loss = 0.5 * jnp.sum(jnp.mean(diff * diff, axis=-1, keepdims=True))
        dout = diff / D
        dxn = dout * gv
        dh = r * (dxn - xn * jnp.mean(dxn * xn, axis=-1, keepdims=True))
        return (jnp.zeros((1, LANE), F32) + loss, dh, 0.5 * dh, jnp.sum(dout * xn, axis=0, keepdims=True))

    row = ((tr, D), lambda i: (i, 0))
    return _ew(name, (S // tr,),
               [(h,) + row, (g, (1, D), lambda i: (0, 0)), (target,) + row],
               [((1, LANE), F32, (1, LANE), lambda i: (0, 0)), ((S, D), F32) + row, ((S, D), BF) + row,
                ((1, D), F32, (1, D), lambda i: (0, 0))], fn, acc=(0, 3))


def _ffn_fwd(tag, u, h, wg, wu, wd, deps=()):
    S, D = u.shape
    Fs = wg.shape[1]
    tm, tk = _pick(S, M_TILE, SUBLANE), _pick(D, K_TILE, LANE)

    def up_epilogue(accs, sides):
        gt, up = accs
        return gt, up, gt * _sigmoid(gt) * up

    act = ((NDEV, S, Fs), BF, (None, tm, Fs), lambda b, i, j, k: (b, i, 0))
    gt, up, a = _mm(
        tag + "_up", (NDEV, S // tm, 1, D // tk),
        [(u, (tm, tk), lambda b, i, j, k: (i, k)),
         (wg, (None, Fs, tk), lambda b, i, j, k: (b, 0, k)), (wu, (None, Fs, tk), lambda b, i, j, k: (b, 0, k))],
        [(0, 1, NT, 0), (0, 2, NT, 1)], [], [act, act, act], up_epilogue, [(tm, Fs), (tm, Fs)], deps=deps)
    if callable(wd):
        wd = wd(a)
    tn = _pick(D, 1024, LANE)
    (hn,) = _mm(
        tag + "_down", (1, S // tm, D // tn, NDEV),
        [(a, (None, tm, Fs), lambda b, i, j, k: (k, i, 0)), (wd, (None, Fs, tn), lambda b, i, j, k: (k, 0, j))],
        [(0, 1, NN, 0)], [(h, (tm, tn), lambda b, i, j, k: (i, j))],
        [((S, D), F32, (tm, tn), lambda b, i, j, k: (i, j))],
        lambda accs, sides: [sides[0] + 0.5 * accs[0]], [(tm, tn)])
    return hn, (gt, up, a), wd


def _ffn_bwd(tag, dhs, u, saved, wg, wu, wd, deps=()):
    gt, up, a = saved
    S, D = u.shape
    Fs = wg.shape[1]
    tm, tk = _pick(S, M_TILE, SUBLANE), _pick(D, K_TILE, LANE)
    act_in = lambda arr: (arr, (None, tm, Fs), lambda b, i, j, k: (b, i, 0))
    act_out = ((NDEV, S, Fs), BF, (None, tm, Fs), lambda b, i, j, k: (b, i, 0))

    def act_epilogue(accs, sides):
        da = accs[0]
        gtv, upv = sides[0].astype(F32), sides[1].astype(F32)
        sg = _sigmoid(gtv)
        return da * upv * sg * (1.0 + gtv * (1.0 - sg)), da * gtv * sg

    dgt, dup = _mm(
        tag + "_dact", (NDEV, S // tm, 1, D // tk),
        [(dhs, (tm, tk), lambda b, i, j, k: (i, k)), (wd, (None, Fs, tk), lambda b, i, j, k: (b, 0, k))],
        [(0, 1, NT, 0)], [act_in(gt), act_in(up)], [act_out, act_out], act_epilogue, [(tm, Fs)], deps=deps)

    ts = _pick(S, K_TILE, SUBLANE)
    tn = _pick(D, 1024, LANE)
    wgrad = ((NDEV, Fs, D), BF, (None, Fs, tn), lambda b, i, j, k: (b, 0, j))
    tok = lambda arr: (arr, (None, ts, Fs), lambda b, i, j, k: (b, k, 0))

    def grad_down(deps=()):
        return _mm(
            tag + "_dwd", (NDEV, 1, D // tn, S // ts),
            [tok(a), (dhs, (ts, tn), lambda b, i, j, k: (k, j))],
            [(0, 1, TN, 0)], [], [wgrad], lambda accs, sides: accs, [(Fs, tn)], deps=deps)[0]

    def grad_gate_up(deps=()):
        return _mm(
            tag + "_dwgu", (NDEV, 1, D // tn, S // ts),
            [tok(dgt), tok(dup), (u, (ts, tn), lambda b, i, j, k: (k, j))],
            [(0, 2, TN, 0), (1, 2, TN, 1)], [], [wgrad, wgrad], lambda accs, sides: accs, [(Fs, tn), (Fs, tn)],
            deps=deps)

    def du(deps=()):
        return _mm(
            tag + "_du", (1, S // tm, D // tn, NDEV),
            [(dgt, (None, tm, Fs), lambda b, i, j, k: (k, i, 0)), (dup, (None, tm, Fs), lambda b, i, j, k: (k, i, 0)),
             (wg, (None, Fs, tn), lambda b, i, j, k: (k, 0, j)), (wu, (None, Fs, tn), lambda b, i, j, k: (k, 0, j))],
            [(0, 2, NN, 0), (1, 3, NN, 0)], [], [((S, D), F32, (tm, tn), lambda b, i, j, k: (i, j))],
            lambda accs, sides: accs, [(tm, tn)], deps=deps)[0]

    return grad_down, grad_gate_up, du


def _ssm_params(lam_re, lam_im, log_dt, b_re, b_im, c_re, c_im):
    G, N = lam_re.shape
    C = b_re.shape[2]
    lam_re = jnp.minimum(lam_re, -1e-4)
    dt = jnp.exp(log_dt)[:, None]
    mag = jnp.exp(lam_re * dt)
    a_re = mag * jnp.cos(lam_im * dt)
    a_im = mag * jnp.sin(lam_im * dt)
    den = lam_re * lam_re + lam_im * lam_im
    p = a_re - 1.0
    f_re = ((p * lam_re + a_im * lam_im) / den)[:, :, None]
    f_im = ((a_im * lam_re - p * lam_im) / den)[:, :, None]
    bb_re = f_re * b_re - f_im * b_im
    bb_im = f_re * b_im + f_im * b_re
    gpt = LANE // C
    tiles = G // gpt
    eye = jnp.eye(gpt, dtype=F32)

    def bd(bb):
        return jnp.einsum("bgnc,gh->bgchn", bb.reshape(tiles, gpt, N, C), eye).reshape(tiles, gpt * C, gpt * N)

    def cd(cc):
        return jnp.einsum("bgcn,gh->bgnhc", cc.reshape(tiles, gpt, C, N), eye).reshape(tiles, gpt * N, gpt * C)

    rows = G * N // LANE
    return (a_re.reshape(rows, LANE), a_im.reshape(rows, LANE), bd(bb_re), bd(bb_im), cd(c_re), cd(-c_im))


def _scan_fwd(bu_re, bu_im, a_re, a_im):
    S, R, _ = bu_re.shape
    tc = _pick(S, 256, SUBLANE)

    def body(bre, bim, are, aim, sre, sim, carry):
        @pl.when(pl.program_id(0) == 0)
        def _():
            carry[...] = jnp.zeros_like(carry)

        ar, ai = are[...], aim[...]

        def step(t, c):
            pr, pi = c
            nr = ar * pr - ai * pi + bre[t]
            ni = ar * pi + ai * pr + bim[t]
            sre[t] = nr
            sim[t] = ni
            return nr, ni

        pr, pi = lax.fori_loop(0, tc, step, (carry[0], carry[1]), unroll=8)
        carry[0] = pr
        carry[1] = pi

    blk = pl.BlockSpec((tc, R, LANE), lambda i: (i, 0, 0))
    par = pl.BlockSpec((R, LANE), lambda i: (0, 0))
    return pl.pallas_call(
        body, name="ssm_scan_fwd", grid=(S // tc,),
        in_specs=[blk, blk, par, par], out_specs=[blk, blk],
        out_shape=[jax.ShapeDtypeStruct((S, R, LANE), F32)] * 2,
        scratch_shapes=[pltpu.VMEM((2, R, LANE), F32)],
        compiler_params=pltpu.CompilerParams(dimension_semantics=("arbitrary",)),
    )(bu_re, bu_im, a_re, a_im)


def _scan_bwd(ds_re, ds_im, s_re, s_im, a_re, a_im):
    S, R, _ = ds_re.shape
    tc = _pick(S, 256, SUBLANE)
    nc = S // tc

    def body(dre, dim_, sre, sim, are, aim, lre, lim, dar, dai, carry):
        @pl.when(pl.program_id(0) == 0)
        def _():
            carry[...] = jnp.zeros_like(carry)
            dar[...] = jnp.zeros_like(dar)
            dai[...] = jnp.zeros_like(dai)

        ar, ai = are[...], aim[...]

        def step(tt, c):
            t = tc - 1 - tt
            lr, li, gr, gi = c
            sr, si = sre[t], sim[t]
            gr = gr + lr * sr + li * si
            gi = gi + li * sr - lr * si
            nlr = dre[t] + ar * lr + ai * li
            nli = dim_[t] + ar * li - ai * lr
            lre[t] = nlr
            lim[t] = nli
            return nlr, nli, gr, gi

        lr, li, gr, gi = lax.fori_loop(0, tc, step, (carry[0], carry[1], dar[...], dai[...]), unroll=8)
        carry[0] = lr
        carry[1] = li
        dar[...] = gr
        dai[...] = gi

    blk = pl.BlockSpec((tc, R, LANE), lambda i: (nc - 1 - i, 0, 0))
    par = pl.BlockSpec((R, LANE), lambda i: (0, 0))
    return pl.pallas_call(
        body, name="ssm_scan_bwd", grid=(nc,),
        in_specs=[blk, blk, blk, blk, par, par], out_specs=[blk, blk, par, par],
        out_shape=[jax.ShapeDtypeStruct((S, R, LANE), F32)] * 2 + [jax.ShapeDtypeStruct((R, LANE), F32)] * 2,
        scratch_shapes=[pltpu.VMEM((2, R, LANE), F32)],
        compiler_params=pltpu.CompilerParams(dimension_semantics=("arbitrary",)),
    )(ds_re, ds_im, s_re, s_im, a_re, a_im)


def _shift_down(z, k):
    t = lax.broadcasted_iota(I32, z.shape, 0)
    return jnp.where(t >= k, pltpu.roll(z, k, 0), 0.0)


def _shift_up(z, k):
    n = z.shape[0]
    t = lax.broadcasted_iota(I32, z.shape, 0)
    return jnp.where(t < n - k, pltpu.roll(z, n - k, 0), 0.0)


def _conv_fwd(proj, cw, cb):
    _, S, W = proj.shape
    ct = _pick(W, 256, LANE)

    def fn(bg, cg, val, w, b):
        z = cg * val
        conv = b + w[0:1] * _shift_down(z, 2) + w[1:2] * _shift_down(z, 1) + w[2:3] * z
        return bg * conv, conv

    sl = lambda s: (proj, (None, S, ct), lambda j, s=s: (s, 0, j))
    col = ((S, ct), lambda j: (0, j))
    return _ew("conv_fwd", (W // ct,),
               [sl(1), sl(2), sl(3), (cw, (3, ct), lambda j: (0, j)), (cb, (1, ct), lambda j: (0, j))],
               [((S, W), BF) + col, ((S, W), F32) + col], fn)


def _conv_bwd(dyb, proj, conv, cw):
    _, S, W = proj.shape
    ct = _pick(W, 256, LANE)

    def fn(dy, bg, cg, val, cv, w):
        z = cg * val
        z1, z2 = _shift_down(z, 1), _shift_down(z, 2)
        dconv = dy * bg
        dz = w[2:3] * dconv + w[1:2] * _shift_up(dconv, 1) + w[0:1] * _shift_up(dconv, 2)
        dw = jnp.concatenate([jnp.sum(dconv * z2, axis=0, keepdims=True), jnp.sum(dconv * z1, axis=0, keepdims=True),
                              jnp.sum(dconv * z, axis=0, keepdims=True)], axis=0)
        return dy * cv, dz * val, dz * cg, dw, jnp.sum(dconv, axis=0, keepdims=True)

    sl = lambda s: (proj, (None, S, ct), lambda j, s=s: (s, 0, j))
    col = ((S, ct), lambda j: (0, j))
    return _ew("conv_bwd", (W // ct,),
               [(dyb,) + col, sl(1), sl(2), sl(3), (conv,) + col, (cw, (3, ct), lambda j: (0, j))],
               [((S, W), BF) + col, ((S, W), BF) + col, ((S, W), BF) + col,
                ((3, W), F32, (3, ct), lambda j: (0, j)), ((1, W), F32, (1, ct), lambda j: (0, j))], fn)


def _plain(accs, sides):
    return accs


def kernel(x, ffn1_norm, ffn1_w_gate, ffn1_w_up, ffn1_w_down, mix_norm, w_in, ssm_lambda_re, ssm_lambda_im, ssm_log_dt, ssm_b_re, ssm_b_im, ssm_c_re, ssm_c_im, ssm_d, ssm_w_glu, ssm_b_glu, ssm_w_out, conv_w, conv_b, conv_w_out, w_o, ffn2_norm, ffn2_w_gate, ffn2_w_up, ffn2_w_down, final_norm, loss_target, m_ffn1_norm, m_ffn1_w_gate, m_ffn1_w_up, m_ffn1_w_down, m_mix_norm, m_w_in, m_ssm_lambda_re, m_ssm_lambda_im, m_ssm_log_dt, m_ssm_b_re, m_ssm_b_im, m_ssm_c_re, m_ssm_c_im, m_ssm_d, m_ssm_w_glu, m_ssm_b_glu, m_ssm_w_out, m_conv_w, m_conv_b, m_conv_w_out, m_w_o, m_ffn2_norm, m_ffn2_w_gate, m_ffn2_w_up, m_ffn2_w_down, m_final_norm, v_ffn1_norm, v_ffn1_w_gate, v_ffn1_w_up, v_ffn1_w_down, v_mix_norm, v_w_in, v_ssm_lambda_re, v_ssm_lambda_im, v_ssm_log_dt, v_ssm_b_re, v_ssm_b_im, v_ssm_c_re, v_ssm_c_im, v_ssm_d, v_ssm_w_glu, v_ssm_b_glu, v_ssm_w_out, v_conv_w, v_conv_b, v_conv_w_out, v_w_o, v_ffn2_norm, v_ffn2_w_gate, v_ffn2_w_up, v_ffn2_w_down, v_final_norm):
    P = dict(ffn1_norm=ffn1_norm, ffn1_w_gate=ffn1_w_gate, ffn1_w_up=ffn1_w_up, ffn1_w_down=ffn1_w_down, mix_norm=mix_norm, w_in=w_in, ssm_lambda_re=ssm_lambda_re, ssm_lambda_im=ssm_lambda_im, ssm_log_dt=ssm_log_dt, ssm_b_re=ssm_b_re, ssm_b_im=ssm_b_im, ssm_c_re=ssm_c_re, ssm_c_im=ssm_c_im, ssm_d=ssm_d, ssm_w_glu=ssm_w_glu, ssm_b_glu=ssm_b_glu, ssm_w_out=ssm_w_out, conv_w=conv_w, conv_b=conv_b, conv_w_out=conv_w_out, w_o=w_o, ffn2_norm=ffn2_norm, ffn2_w_gate=ffn2_w_gate, ffn2_w_up=ffn2_w_up, ffn2_w_down=ffn2_w_down, final_norm=final_norm)
    M = dict(ffn1_norm=m_ffn1_norm, ffn1_w_gate=m_ffn1_w_gate, ffn1_w_up=m_ffn1_w_up, ffn1_w_down=m_ffn1_w_down, mix_norm=m_mix_norm, w_in=m_w_in, ssm_lambda_re=m_ssm_lambda_re, ssm_lambda_im=m_ssm_lambda_im, ssm_log_dt=m_ssm_log_dt, ssm_b_re=m_ssm_b_re, ssm_b_im=m_ssm_b_im, ssm_c_re=m_ssm_c_re, ssm_c_im=m_ssm_c_im, ssm_d=m_ssm_d, ssm_w_glu=m_ssm_w_glu, ssm_b_glu=m_ssm_b_glu, ssm_w_out=m_ssm_w_out, conv_w=m_conv_w, conv_b=m_conv_b, conv_w_out=m_conv_w_out, w_o=m_w_o, ffn2_norm=m_ffn2_norm, ffn2_w_gate=m_ffn2_w_gate, ffn2_w_up=m_ffn2_w_up, ffn2_w_down=m_ffn2_w_down, final_norm=m_final_norm)
    V = dict(ffn1_norm=v_ffn1_norm, ffn1_w_gate=v_ffn1_w_gate, ffn1_w_up=v_ffn1_w_up, ffn1_w_down=v_ffn1_w_down, mix_norm=v_mix_norm, w_in=v_w_in, ssm_lambda_re=v_ssm_lambda_re, ssm_lambda_im=v_ssm_lambda_im, ssm_log_dt=v_ssm_log_dt, ssm_b_re=v_ssm_b_re, ssm_b_im=v_ssm_b_im, ssm_c_re=v_ssm_c_re, ssm_c_im=v_ssm_c_im, ssm_d=v_ssm_d, ssm_w_glu=v_ssm_w_glu, ssm_b_glu=v_ssm_b_glu, ssm_w_out=v_ssm_w_out, conv_w=v_conv_w, conv_b=v_conv_b, conv_w_out=v_conv_w_out, w_o=v_w_o, ffn2_norm=v_ffn2_norm, ffn2_w_gate=v_ffn2_w_gate, ffn2_w_up=v_ffn2_w_up, ffn2_w_down=v_ffn2_w_down, final_norm=v_final_norm)
    names = list(P)
    sharded = ["ffn1_w_gate", "ffn1_w_up", "ffn1_w_down", "w_in", "ssm_w_glu", "ssm_w_out", "conv_w_out", "w_o",
               "ffn2_w_gate", "ffn2_w_up", "ffn2_w_down"]
    replicated = [n for n in names if n not in sharded and n != "conv_w"]

    S, D = x.shape[1], x.shape[2]
    W = ssm_d.shape[0]
    Dc = D // NDEV
    G, N = ssm_lambda_re.shape
    GN = G * N
    rows = GN // LANE
    xh = x.reshape(S, D)
    target = loss_target.reshape(S, D)
    xi, yi, ci = lax.axis_index("x"), lax.axis_index("y"), lax.axis_index("c")
    c_arr = jnp.reshape(ci, (1,)).astype(I32)
    q_arr = jnp.reshape(2 * xi + yi, (1,)).astype(I32)
    row = lambda v: v.reshape(1, -1)

    transposed = ("ffn1_w_gate", "ffn1_w_up", "ffn2_w_gate", "ffn2_w_up")
    local = lambda table, n: table[n].T if n in transposed else table[n]

    groups = [sharded[0:2], sharded[2:3], ["w_in", "conv_w"], sharded[4:8], sharded[8:11]]
    started = {}
    me = 4 * xi + 2 * yi + ci

    def gather_start(gi):
        hold = 0.0 * started[0][4][0, 0] if gi else 0.0
        srcs = [conv_w + hold if n == "conv_w" else (local(P, n) + hold).astype(BF) for n in groups[gi]]
        lands = [lax.dynamic_update_slice(lax.empty((NDEV,) + s.shape, s.dtype), s[None], (me,) + (0,) * s.ndim)
                 for s in srcs]
        after = [started[gi - 1][4]] if gi else []
        started[gi] = _split_start("gather_start_%d" % gi, srcs, lands, _gather_copies, 4, after)
        return started[gi][4]

    def gathered(gi, after):
        _, lands = _split_wait("gather_wait_%d" % gi, started[gi], _gather_copies, after)
        return _gather_forward("gather_forward_%d" % gi, lands)

    tm = _pick(S, M_TILE, SUBLANE)
    th = _pick(S, M_TILE // 2, SUBLANE)
    tq = _pick(S, 2 * M_TILE, SUBLANE)
    tk = _pick(D, K_TILE, LANE)
    ts = _pick(S, K_TILE, SUBLANE)
    tn = _pick(D, 1024, LANE)

    tokens = [gather_start(gi) for gi in range(len(groups))]
    u1, r1 = _rms_fwd("rms1", xh, row(ffn1_norm), deps=tokens)
    wg1, wu1 = gathered(0, u1)
    h1, ffn1_saved, wd1 = _ffn_fwd("ffn1", u1, xh, wg1, wu1, lambda after: gathered(1, after)[0])
    u2, r2 = _rms_fwd("rms2", h1, row(mix_norm))
    w_in_f, cw_f = gathered(2, u2)
    cw = jnp.transpose(cw_f, (1, 0, 2)).reshape(3, W)
    (proj,) = _mm(
        "in_proj", (NDEV, S // tm, 1, D // tk),
        [(u2, (tm, tk), lambda b, i, j, k: (i, k)), (w_in_f, (None, tk, W), lambda b, i, j, k: (b, k, 0))],
        [(0, 1, NN, 0)], [], [((NDEV, S, W), F32, (None, tm, W), lambda b, i, j, k: (b, i, 0))], _plain, [(tm, W)])

    ssm_in = (ssm_lambda_re, ssm_lambda_im, ssm_log_dt, ssm_b_re, ssm_b_im, ssm_c_re, ssm_c_im)
    (a_re, a_im, bd_re, bd_im, cd_re, cd_imn), ssm_vjp = jax.vjp(_ssm_params, *ssm_in)
    tiles, tch, tst = bd_re.shape
    bd_re_b, bd_im_b, cd_re_b, cd_imn_b = (t.astype(BF) for t in (bd_re, bd_im, cd_re, cd_imn))
    v_bf = proj[0].astype(BF)
    st_blk = lambda arr: (arr, (tq, tst), lambda b, i, j, k: (i, b))
    ch_blk = lambda arr: (arr, (tq, tch), lambda b, i, j, k: (i, b))
    bd_blk = lambda arr: (arr, (None, tch, tst), lambda b, i, j, k: (b, 0, 0))
    cd_blk = lambda arr: (arr, (None, tst, tch), lambda b, i, j, k: (b, 0, 0))
    st_out = lambda dt: ((S, GN), dt, (tq, tst), lambda b, i, j, k: (i, b))
    ch_out = lambda dt: ((S, W), dt, (tq, tch), lambda b, i, j, k: (i, b))
    d_blk = (row(ssm_d), (1, tch), lambda b, i, j, k: (0, b))
    v_blk = (proj, (None, tq, tch), lambda b, i, j, k: (0, i, b))

    bu_re, bu_im = _mm("ssm_bu", (tiles, S // tq, 1, 1), [ch_blk(v_bf), bd_blk(bd_re_b), bd_blk(bd_im_b)],
                       [(0, 1, NN, 0), (0, 2, NN, 1)], [], [st_out(F32), st_out(F32)], _plain, [(tq, tst)] * 2)
    s_re3, s_im3 = _scan_fwd(bu_re.reshape(S, rows, LANE), bu_im.reshape(S, rows, LANE), a_re, a_im)
    s_re_b = s_re3.reshape(S, GN).astype(BF)
    s_im_b = s_im3.reshape(S, GN).astype(BF)

    def y0_epilogue(accs, sides):
        y0 = accs[0] + sides[1] * sides[0]
        return y0, _gelu(y0)

    y0, y1 = _mm("ssm_y0", (tiles, S // tq, 1, 1),
                 [st_blk(s_re_b), st_blk(s_im_b), cd_blk(cd_re_b), cd_blk(cd_imn_b)],
                 [(0, 2, NN, 0), (1, 3, NN, 0)], [v_blk, d_blk], [ch_out(F32), ch_out(BF)], y0_epilogue, [(tq, tch)])

    tw = _pick(W, 512, LANE)
    w_glu_f, w_so, w_co, w_o_f = gathered(3, y1)
    w_glu_f = w_glu_f.reshape(W, W)
    w_o_f = w_o_f.reshape(D, D)

    def glu_epilogue(accs, sides):
        q = accs[0] + sides[1]
        return q, _gelu(sides[0]) * _sigmoid(q)

    q_pre, y2 = _mm("ssm_glu", (1, S // tm, W // tw, 1),
                    [(y1, (tm, W), lambda b, i, j, k: (i, 0)), (w_glu_f, (W, tw), lambda b, i, j, k: (0, j))],
                    [(0, 1, NN, 0)],
                    [(y0, (tm, tw), lambda b, i, j, k: (i, j)), (row(ssm_b_glu), (1, tw), lambda b, i, j, k: (0, j))],
                    [((S, W), F32, (tm, tw), lambda b, i, j, k: (i, j)), ((S, W), BF, (tm, tw), lambda b, i, j, k: (i, j))],
                    glu_epilogue, [(tm, tw)])

    yb, conv = _conv_fwd(proj, cw, row(conv_b))

    per = W // Dc
    ga_blk = (proj, (None, tm, Dc), lambda b, i, j, k: (4 + b // per, i, b % per))
    gb_blk = (proj, (None, tm, Dc), lambda b, i, j, k: (6 + b // per, i, b % per))
    dc_out = ((S, D), BF, (tm, Dc), lambda b, i, j, k: (i, b))

    def merge_epilogue(accs, sides):
        za, zb = accs
        return _sigmoid(sides[0]) * za + _sigmoid(sides[1]) * zb, za, zb

    merged, z_a, z_b = _mm(
        "mix_merge", (NDEV, S // tm, 1, 1),
        [(y2, (tm, W), lambda b, i, j, k: (i, 0)), (yb, (tm, W), lambda b, i, j, k: (i, 0)),
         (w_so, (None, W, Dc), lambda b, i, j, k: (b, 0, 0)), (w_co, (None, W, Dc), lambda b, i, j, k: (b, 0, 0))],
        [(0, 2, NN, 0), (1, 3, NN, 1)], [ga_blk, gb_blk], [dc_out, dc_out, dc_out], merge_epilogue, [(tm, Dc)] * 2)

    (h2,) = _mm("mix_out", (1, S // tm, D // tn, D // tk),
                [(merged, (tm, tk), lambda b, i, j, k: (i, k)), (w_o_f, (tk, tn), lambda b, i, j, k: (k, j))],
                [(0, 1, NN, 0)], [(h1, (tm, tn), lambda b, i, j, k: (i, j))],
                [((S, D), F32, (tm, tn), lambda b, i, j, k: (i, j))],
                lambda accs, sides: [sides[0] + accs[0]], [(tm, tn)])

    u3, r3 = _rms_fwd("rms3", h2, row(ffn2_norm))
    wg2, wu2, wd2 = gathered(4, u3)
    h3, ffn2_saved, _ = _ffn_fwd("ffn2", u3, h2, wg2, wu2, wd2)
    loss_vec, dh3, dh3_half, d_final_norm = _loss_head("loss_head", h3, row(final_norm), target)
    loss = lax.psum(loss_vec[0, 0], ("x", "y", "c"))
    loss_done = jnp.zeros((SUBLANE, LANE), F32) + loss

    grads, deltas, new_m, new_v = {}, {}, {}, {}

    def rs_sibling_start(tag, parts):
        lands = [lax.empty((4,) + p.shape[1:], p.dtype) for p in parts]
        return _split_start("rs_sibling_start_" + tag, parts, lands, _sibling_copies, 4)

    def rs_chips_start(tag, sibling_begun, after):
        parts, lands = _split_wait("rs_sibling_wait_" + tag, sibling_begun, _sibling_copies, after)
        sums = [_sum_sibling("rs_sum_%s_%d" % (tag, a), p, land, c_arr) for a, (p, land) in enumerate(zip(parts, lands))]
        lands2 = [lax.empty((3,) + sm.shape[1:], sm.dtype) for sm in sums]
        return _split_start("rs_chips_start_" + tag, sums, lands2, _chips_copies, 3)

    def rs_end(tag, group, begun, after):
        sums, lands2 = _split_wait("rs_chips_wait_" + tag, begun, _chips_copies, after)
        for n, sm, land2 in zip(group, sums, lands2):
            res = _finish_sharded("adamw_" + n, sm, land2, q_arr, local(P, n), local(M, n), local(V, n))
            grads[n], deltas[n], new_m[n], new_v[n] = [t.T if n in transposed else t for t in res]

    f2_dwd, f2_dwgu, f2_du = _ffn_bwd("ffn2", dh3_half, u3, ffn2_saved, wg2, wu2, wd2, deps=[loss_done])
    dwd2 = f2_dwd()
    dwg2, dwu2 = f2_dwgu()
    sib_ffn2 = rs_sibling_start("ffn2", [dwg2, dwu2, dwd2])
    du3 = f2_du(deps=[sib_ffn2[4]])
    dh2, dh2_b, d_ffn2_norm = _rms_bwd("rms3_bwd", du3, h2, r3, row(ffn2_norm), dh3, 1.0)
    rs_ffn2 = rs_chips_start("ffn2", sib_ffn2, dh2)

    dg_out = ((2, S, W), BF, (None, tm, Dc), lambda b, i, j, k: (j // per, i, j % per))
    ga_blk2 = (proj, (None, tm, Dc), lambda b, i, j, k: (4 + j // per, i, j % per))
    gb_blk2 = (proj, (None, tm, Dc), lambda b, i, j, k: (6 + j // per, i, j % per))
    dcj = lambda arr: (arr, (tm, Dc), lambda b, i, j, k: (i, j))
    dcj_out = ((S, D), BF, (tm, Dc), lambda b, i, j, k: (i, j))

    def dmerge_epilogue(accs, sides):
        dm = accs[0]
        sa, sb = _sigmoid(sides[0]), _sigmoid(sides[1])
        za, zb = sides[2].astype(F32), sides[3].astype(F32)
        return dm * sa, dm * sb, dm * za * sa * (1.0 - sa), dm * zb * sb * (1.0 - sb)

    dz_a, dz_b, dga, dgb = _mm(
        "mix_out_dx", (1, S // tm, NDEV, D // tk),
        [(dh2_b, (tm, tk), lambda b, i, j, k: (i, k)), (w_o_f, (Dc, tk), lambda b, i, j, k: (j, k))],
        [(0, 1, NT, 0)], [ga_blk2, gb_blk2, dcj(z_a), dcj(z_b)], [dcj_out, dcj_out, dg_out, dg_out],
        dmerge_epilogue, [(tm, Dc)], deps=[rs_ffn2[4]])

    td = _pick(D, M_TILE, LANE)
    (dw_o,) = _mm("mix_out_dw", (1, D // td, D // tn, S // ts),
                  [(merged, (ts, td), lambda b, i, j, k: (k, i)), (dh2_b, (ts, tn), lambda b, i, j, k: (k, j))],
                  [(0, 1, TN, 0)], [], [((D, D), BF, (td, tn), lambda b, i, j, k: (i, j))], _plain, [(td, tn)])

    wout = ((NDEV, W, Dc), BF, (None, W, Dc), lambda b, i, j, k: (b, 0, 0))
    dw_so, dw_co = _mm(
        "mix_merge_dw", (NDEV, 1, 1, S // ts),
        [(y2, (ts, W), lambda b, i, j, k: (k, 0)), (yb, (ts, W), lambda b, i, j, k: (k, 0)),
         (dz_a, (ts, Dc), lambda b, i, j, k: (k, b)), (dz_b, (ts, Dc), lambda b, i, j, k: (k, b))],
        [(0, 2, TN, 0), (1, 3, TN, 1)], [], [wout, wout], _plain, [(W, Dc)] * 2)

    def dglu_epilogue(accs, sides):
        dy2, dyb = accs
        sq = _sigmoid(sides[1])
        return dy2 * _gelu(sides[0]) * sq * (1.0 - sq), dy2 * sq, dyb

    full_w = lambda arr: (arr, (th, W), lambda b, i, j, k: (i, 0))
    full_w_out = lambda dt: ((S, W), dt, (th, W), lambda b, i, j, k: (i, 0))
    dq, dy1p, dyb = _mm(
        "mix_merge_dx", (1, S // th, 1, NDEV),
        [(dz_a, (th, Dc), lambda b, i, j, k: (i, k)), (dz_b, (th, Dc), lambda b, i, j, k: (i, k)),
         (w_so, (None, W, Dc), lambda b, i, j, k: (k, 0, 0)), (w_co, (None, W, Dc), lambda b, i, j, k: (k, 0, 0))],
        [(0, 2, NT, 0), (1, 3, NT, 1)], [full_w(y0), full_w(q_pre)], [full_w_out(BF), full_w_out(F32), full_w_out(F32)],
        dglu_epilogue, [(th, W)] * 2)

    def dy0_epilogue(accs, sides):
        dy0 = (sides[0] + accs[0]) * _gelu_grad(sides[1])
        return dy0, dy0

    wj = lambda arr: (arr, (tm, tw), lambda b, i, j, k: (i, j))
    dy0, dy0_b = _mm("ssm_glu_dx", (1, S // tm, W // tw, 1),
                     [(dq, (tm, W), lambda b, i, j, k: (i, 0)), (w_glu_f, (tw, W), lambda b, i, j, k: (j, 0))],
                     [(0, 1, NT, 0)], [wj(dy1p), wj(y0)],
                     [((S, W), F32, (tm, tw), lambda b, i, j, k: (i, j)), ((S, W), BF, (tm, tw), lambda b, i, j, k: (i, j))],
                     dy0_epilogue, [(tm, tw)])

    (dw_glu,) = _mm("ssm_glu_dw", (1, W // tw, 1, S // ts),
                    [(y1, (ts, tw), lambda b, i, j, k: (k, i)), (dq, (ts, W), lambda b, i, j, k: (k, 0))],
                    [(0, 1, TN, 0)], [], [((W, W), BF, (tw, W), lambda b, i, j, k: (i, 0))], _plain, [(tw, W)])

    tr = _pick(S, 256, SUBLANE)
    rw = ((tr, W), lambda i: (i, 0))
    vec_w = ((1, W), F32, (1, W), lambda i: (0, 0))
    d_b_glu, d_ssm_d = _ew(
        "ssm_colsums", (S // tr,), [(dq,) + rw, (dy0,) + rw, (proj, (None, tr, W), lambda i: (0, i, 0))],
        [vec_w, vec_w],
        lambda dqv, dyv, vv: (jnp.sum(dqv.astype(F32), axis=0, keepdims=True), jnp.sum(dyv * vv, axis=0, keepdims=True)),
        acc=(0, 1))

    ds_re, ds_im = _mm("ssm_ds", (tiles, S // tq, 1, 1), [ch_blk(dy0_b), cd_blk(cd_re_b), cd_blk(cd_imn_b)],
                       [(0, 1, NT, 0), (0, 2, NT, 1)], [], [st_out(F32), st_out(F32)], _plain, [(tq, tst)] * 2)
    lam_re3, lam_im3, da_re, da_im = _scan_bwd(ds_re.reshape(S, rows, LANE), ds_im.reshape(S, rows, LANE),
                                               s_re3, s_im3, a_re, a_im)
    lam_re_b = lam_re3.reshape(S, GN).astype(BF)
    lam_im_b = lam_im3.reshape(S, GN).astype(BF)

    (dv,) = _mm("ssm_dv", (tiles, S // tq, 1, 1),
                [st_blk(lam_re_b), st_blk(lam_im_b), bd_blk(bd_re_b), bd_blk(bd_im_b)],
                [(0, 2, NT, 0), (1, 3, NT, 0)], [ch_blk(dy0), d_blk], [ch_out(BF)],
                lambda accs, sides: [accs[0] + sides[0] * sides[1]], [(tq, tch)])

    tok_ch = lambda arr: (arr, (ts, tch), lambda b, i, j, k: (k, b))
    tok_st = lambda arr: (arr, (ts, tst), lambda b, i, j, k: (k, b))
    bd_out = ((tiles, tch, tst), F32, (None, tch, tst), lambda b, i, j, k: (b, 0, 0))
    cd_out = ((tiles, tst, tch), F32, (None, tst, tch), lambda b, i, j, k: (b, 0, 0))
    dbd_re, dbd_im = _mm("ssm_dbd", (tiles, 1, 1, S // ts), [tok_ch(v_bf), tok_st(lam_re_b), tok_st(lam_im_b)],
                         [(0, 1, TN, 0), (0, 2, TN, 1)], [], [bd_out, bd_out], _plain, [(tch, tst)] * 2)
    dcd_re, dcd_imn = _mm("ssm_dcd", (tiles, 1, 1, S // ts), [tok_st(s_re_b), tok_st(s_im_b), tok_ch(dy0_b)],
                          [(0, 2, TN, 0), (1, 2, TN, 1)], [], [cd_out, cd_out], _plain, [(tst, tch)] * 2)
    d_ssm = ssm_vjp((da_re, da_im, dbd_re, dbd_im, dcd_re, dcd_imn))

    dbg, dcg, dval, d_conv_w_full, d_conv_b = _conv_bwd(dyb, proj, conv, cw)
    dproj = jnp.concatenate([dv[None], dbg[None], dcg[None], dval[None], dga, dgb], axis=0)

    (dw_in,) = _mm("in_proj_dw", (NDEV, D // td, 1, S // ts),
                   [(u2, (ts, td), lambda b, i, j, k: (k, i)), (dproj, (None, ts, W), lambda b, i, j, k: (b, k, 0))],
                   [(0, 1, TN, 0)], [], [((NDEV, D, W), BF, (None, td, W), lambda b, i, j, k: (b, i, 0))],
                   _plain, [(td, W)])
    sib_mixer = rs_sibling_start(
        "mixer", [dw_in, dw_glu.reshape(NDEV, W // NDEV, W), dw_so, dw_co, dw_o.reshape(NDEV, Dc, D)])
    (du2,) = _mm("in_proj_dx", (1, S // tm, D // tn, NDEV),
                 [(dproj, (None, tm, W), lambda b, i, j, k: (k, i, 0)), (w_in_f, (None, tn, W), lambda b, i, j, k: (k, j, 0))],
                 [(0, 1, NT, 0)], [], [((S, D), F32, (tm, tn), lambda b, i, j, k: (i, j))], _plain, [(tm, tn)],
                 deps=[sib_mixer[4]])
    dh1, dh1_half, d_mix_norm = _rms_bwd("rms2_bwd", du2, h1, r2, row(mix_norm), dh2, 0.5)
    rs_mixer = rs_chips_start("mixer", sib_mixer, dh1)

    small = dict(mix_norm=d_mix_norm, ffn2_norm=d_ffn2_norm, final_norm=d_final_norm,
                 ssm_lambda_re=d_ssm[0], ssm_lambda_im=d_ssm[1], ssm_log_dt=d_ssm[2], ssm_b_re=d_ssm[3],
                 ssm_b_im=d_ssm[4], ssm_c_re=d_ssm[5], ssm_c_im=d_ssm[6], ssm_d=d_ssm_d, ssm_b_glu=d_b_glu,
                 conv_b=d_conv_b)
    replicated = [n for n in replicated if n != "ffn1_norm"] + ["ffn1_norm"]
    tile = SUBLANE * LANE

    def as_rows(p):
        flat = p.reshape(-1).astype(F32)
        return jnp.pad(flat, (0, -flat.shape[0] % tile)).reshape(-1, LANE)

    def pack(parts):
        return jnp.concatenate([as_rows(p) for p in parts], axis=0)

    cw_zero = jnp.zeros_like(d_conv_w_full)
    early_pk = pack([small[n] for n in replicated[:-1]] + [d_conv_w_full])
    early_land = lax.dynamic_update_slice(jnp.zeros((NDEV,) + early_pk.shape, F32), early_pk[None], (me, 0, 0))
    small_begun = _split_start("gather_small_start", [early_pk], [early_land], _everyone_copies, NDEV - 1)

    f1_dwd, f1_dwgu, f1_du = _ffn_bwd("ffn1", dh1_half, u1, ffn1_saved, wg1, wu1, wd1,
                                      deps=[rs_mixer[4], small_begun[4]])
    du1 = f1_du()
    dx, _, d_ffn1_norm = _rms_bwd("rms1_bwd", du1, xh, r1, row(ffn1_norm), dh1, 1.0)
    dwd1 = f1_dwd(deps=[d_ffn1_norm])
    late = d_ffn1_norm + 0.0 * dwd1[0, :1, :1].astype(F32)
    (late_all,) = _all_gather("gather_ffn1_norm_grad", [late.reshape(-1, LANE)])
    rs_ffn1_down = rs_chips_start("ffn1_down", rs_sibling_start("ffn1_down", [dwd1]), late_all)
    dwg1, dwu1 = f1_dwgu(deps=[rs_ffn1_down[4]])
    rs_ffn1_gate_up = rs_chips_start("ffn1_gate_up", rs_sibling_start("ffn1_gate_up", [dwg1, dwu1]), None)
    rs_end("ffn2", sharded[8:11], rs_ffn2, rs_ffn1_gate_up[4])
    rs_end("mixer", sharded[3:8], rs_mixer, [grads[n] for n in sharded[8:11]])
    _, (early_all,) = _split_wait("gather_small_wait", small_begun, _everyone_copies, [grads[n] for n in sharded[3:8]])
    small_all = jnp.concatenate([early_all, late_all], axis=1)
    g_pk, d_pk, m_pk, v_pk = _finish_replicated(
        "adamw_replicated", small_all, pack([P[n] for n in replicated[:-1]] + [cw_zero, P["ffn1_norm"]]),
        pack([M[n] for n in replicated[:-1]] + [cw_zero, M["ffn1_norm"]]),
        pack([V[n] for n in replicated[:-1]] + [cw_zero + 1.0, V["ffn1_norm"]]))

    def unpack(pk, r0, like):
        nr = -(-like.size // tile) * SUBLANE
        return pk[r0:r0 + nr].reshape(-1)[:like.size].reshape(like.shape), r0 + nr

    r0 = 0
    for n in replicated[:-1] + ["conv_w", "ffn1_norm"]:
        like = d_conv_w_full if n == "conv_w" else P[n]
        for store, pk in ((grads, g_pk), (deltas, d_pk), (new_m, m_pk), (new_v, v_pk)):
            store[n], r1 = unpack(pk, r0, like)
        r0 = r1
    g_cw_full = grads["conv_w"]
    cwl = conv_w.shape[1]
    g_cw = lax.dynamic_slice_in_dim(g_cw_full, me * cwl, cwl, axis=1)
    full3 = ((3, cwl), lambda i: (0, 0))
    grads["conv_w"], deltas["conv_w"], new_m["conv_w"], new_v["conv_w"] = _ew(
        "adamw_conv_w", (1,), [(g_cw,) + full3, (conv_w,) + full3, (m_conv_w,) + full3, (v_conv_w,) + full3],
        [((3, cwl), F32) + full3] * 4, lambda g, w, m, v: (g,) + _adamw(w, g, m, v))
    rs_end("ffn1_down", sharded[2:3], rs_ffn1_down, [g_pk, grads["conv_w"]])
    rs_end("ffn1_gate_up", sharded[0:2], rs_ffn1_gate_up, grads["ffn1_w_down"])

    return (loss, dx.reshape(x.shape), *[grads[n] for n in names], *[deltas[n] for n in names],
            *[new_m[n] for n in names], *[new_v[n] for n in names])
```

```python
import math

import jax
import jax.numpy as jnp
from jax import lax
from jax.experimental import pallas as pl
from jax.experimental.pallas import tpu as pltpu

F32 = jnp.float32
BF = jnp.bfloat16
I32 = jnp.int32
MESH = pl.DeviceIdType.MESH
LANE = 128
SUBLANE = 8
NDEV = 8
EW_BLOCK = 256 * 1024
M_TILE = 1024
K_TILE = 2048
DMA_CHUNKS = 4
EPS = 1e-6
ADAM_LR, ADAM_B1, ADAM_B2, ADAM_EPS, ADAM_WD, ADAM_STEP = 0.001, 0.9, 0.999, 1e-08, 0.01, 10
NN = ((1,), (0,))
NT = ((1,), (1,))
TN = ((0,), (0,))
HBM = pl.BlockSpec(memory_space=pltpu.HBM)


def _pick(n, pref, mult):
    t = min(pref, n)
    t -= t % mult
    while t >= mult:
        if n % t == 0:
            return t
        t -= mult
    return n


def _sigmoid(x):
    return 1.0 / (1.0 + jnp.exp(-x))


_GELU_C = math.sqrt(2.0 / math.pi)


def _gelu(x):
    return 0.5 * x * (1.0 + jnp.tanh(_GELU_C * (x + 0.044715 * x * x * x)))


def _gelu_grad(x):
    t = jnp.tanh(_GELU_C * (x + 0.044715 * x * x * x))
    return 0.5 * (1.0 + t) + 0.5 * x * (1.0 - t * t) * _GELU_C * (1.0 + 3.0 * 0.044715 * x * x)


def _dep_specs(deps, rank):
    return [(d, d.shape, lambda *_, nd=d.ndim: (0,) * nd) for d in deps]


def _mm(name, grid, ops, pairs, sides, outs, epilogue, acc_shapes, deps=()):
    nk = grid[-1]
    n_ops, n_sides, n_outs = len(ops), len(sides), len(outs)
    dep_specs = _dep_specs(deps, len(grid))
    n_deps = len(dep_specs)

    def body(*refs):
        op_refs = refs[:n_ops]
        side_refs = refs[n_ops:n_ops + n_sides]
        out_refs = refs[n_ops + n_sides + n_deps:n_ops + n_sides + n_deps + n_outs]
        acc_refs = refs[n_ops + n_sides + n_deps + n_outs:]

        def partials():
            res = [None] * len(acc_shapes)
            for ia, ib, dims, ai in pairs:
                p = lax.dot_general(op_refs[ia][...], op_refs[ib][...], (dims, ((), ())),
                                    preferred_element_type=F32)
                res[ai] = p if res[ai] is None else res[ai] + p
            return res

        def finish(accs):
            vals = epilogue(accs, [s[...] for s in side_refs])
            for o, v in zip(out_refs, vals):
                o[...] = v.astype(o.dtype)

        if nk == 1:
            finish(partials())
        else:
            k = pl.program_id(len(grid) - 1)

            @pl.when(k == 0)
            def _():
                for a, p in zip(acc_refs, partials()):
                    a[...] = p

            @pl.when(k > 0)
            def _():
                for a, p in zip(acc_refs, partials()):
                    a[...] += p

            @pl.when(k == nk - 1)
            def _():
                finish([a[...] for a in acc_refs])

    return pl.pallas_call(
        body, name=name, grid=grid,
        in_specs=[pl.BlockSpec(b, m) for (_, b, m) in list(ops) + list(sides) + dep_specs],
        out_specs=[pl.BlockSpec(b, m) for (_, _, b, m) in outs],
        out_shape=[jax.ShapeDtypeStruct(s, d) for (s, d, _, _) in outs],
        scratch_shapes=[pltpu.VMEM(s, F32) for s in acc_shapes] if nk > 1 else [],
        compiler_params=pltpu.CompilerParams(
            dimension_semantics=("parallel",) * (len(grid) - 1) + ("arbitrary",)),
    )(*[a for (a, _, _) in list(ops) + list(sides) + dep_specs])


def _ew(name, grid, ins, outs, fn, acc=(), deps=()):
    n_in = len(ins)
    dep_specs = _dep_specs(deps, len(grid))

    def body(*refs):
        vals = fn(*[r[...] for r in refs[:n_in]])
        first = pl.program_id(0) == 0
        for idx, (o, v) in enumerate(zip(refs[n_in + len(dep_specs):], vals)):
            if idx in acc:
                @pl.when(first)
                def _(o=o, v=v):
                    o[...] = v.astype(o.dtype)

                @pl.when(jnp.logical_not(first))
                def _(o=o, v=v):
                    o[...] += v.astype(o.dtype)
            else:
                o[...] = v.astype(o.dtype)

    return pl.pallas_call(
        body, name=name, grid=grid,
        in_specs=[pl.BlockSpec(b, m) for (_, b, m) in list(ins) + dep_specs],
        out_specs=[pl.BlockSpec(b, m) for (_, _, b, m) in outs],
        out_shape=[jax.ShapeDtypeStruct(s, d) for (s, d, _, _) in outs],
        compiler_params=pltpu.CompilerParams(
            dimension_semantics=(("arbitrary",) if acc else ("parallel",)) * len(grid)),
    )(*[a for (a, _, _) in list(ins) + dep_specs])


def _position():
    x, y, c = lax.axis_index("x"), lax.axis_index("y"), lax.axis_index("c")
    chips = [(1 - x, y), (x, 1 - y), (1 - x, 1 - y)]
    return x, y, c, chips


def _all_gather(name, shards):
    n = len(shards)

    def body(*refs):
        xs, outs = refs[:n], refs[n:2 * n]
        send_sems, recv_sems, local_sems = refs[2 * n:]
        x, y, c, chips = _position()
        me, sibling = (x, y, c), (x, y, 1 - c)

        def copy(a, k, block, to, src=None):
            dst = outs[a].at[4 * block[0] + 2 * block[1] + block[2]]
            return pltpu.make_async_remote_copy(
                src_ref=dst if src is None else src, dst_ref=dst,
                send_sem=send_sems.at[a, k], recv_sem=recv_sems.at[a, k],
                device_id=to, device_id_type=MESH)

        mine = [pltpu.make_async_copy(xs[a], outs[a].at[4 * x + 2 * y + c], local_sems.at[a]) for a in range(n)]
        for cp in mine:
            cp.start()
        first = []
        for a in range(n):
            first.append(copy(a, 0, me, sibling, src=xs[a]))
            first += [copy(a, 1 + j, me, (*chip, c), src=xs[a]) for j, chip in enumerate(chips)]
        for cp in first:
            cp.start()
        passed = []
        for a in range(n):
            for j, chip in enumerate(chips):
                copy(a, 1 + j, (*chip, c), me).wait_recv()
                cp = copy(a, 4 + j, (*chip, c), sibling)
                cp.start()
                passed.append(cp)
        for a in range(n):
            copy(a, 0, sibling, me).wait_recv()
            for j, chip in enumerate(chips):
                copy(a, 4 + j, (*chip, 1 - c), me).wait_recv()
        for cp in first + passed:
            cp.wait_send()
        for cp in mine:
            cp.wait()

    return pl.pallas_call(
        body, name=name,
        out_shape=[jax.ShapeDtypeStruct((NDEV,) + s.shape, s.dtype) for s in shards],
        in_specs=[HBM] * n, out_specs=[HBM] * n,
        scratch_shapes=[pltpu.SemaphoreType.DMA((n, 7)), pltpu.SemaphoreType.DMA((n, 7)),
                        pltpu.SemaphoreType.DMA((n,))],
    )(*shards)


SEM = pl.BlockSpec(memory_space=pltpu.SEMAPHORE)
EFFECT = pltpu.SideEffectType.DATAFLOW_SIDE_EFFECTING


def _in_hbm(v):
    return pltpu.with_memory_space_constraint(v, pltpu.HBM)


def _split_start(name, srcs, lands, make_copies, n_per, after=()):
    n = len(srcs)
    after = list(after)

    def body(*refs):
        send_sems, recv_sems = refs[2 * n + len(after)], refs[2 * n + len(after) + 1]
        for cp in make_copies(refs[:n], refs[n:2 * n], send_sems, recv_sems):
            cp.start()
        refs[-1][...] = jnp.zeros_like(refs[-1])

    outs = pl.pallas_call(
        body, name=name,
        out_shape=(pltpu.SemaphoreType.DMA((n * n_per,)), pltpu.SemaphoreType.DMA((n * n_per,)),
                   *[pltpu.HBM(v.shape, v.dtype) for v in list(srcs) + list(lands)],
                   jax.ShapeDtypeStruct((SUBLANE, LANE), F32)),
        in_specs=[HBM] * (2 * n) + [pl.BlockSpec(memory_space=pl.ANY)] * len(after),
        out_specs=(SEM, SEM, *[HBM] * (2 * n), pl.BlockSpec(memory_space=pltpu.VMEM)),
        input_output_aliases={i: 2 + i for i in range(2 * n)},
        compiler_params=pltpu.CompilerParams(has_side_effects=EFFECT),
    )(*[_in_hbm(v) for v in list(srcs) + list(lands)], *after)
    return outs[0], outs[1], list(outs[2:2 + n]), list(outs[2 + n:2 + 2 * n]), outs[-1]


def _split_wait(name, started, make_copies, after):
    send_sems, recv_sems, srcs, lands, _ = started
    n = len(srcs)

    def body(*refs):
        for cp in make_copies(refs[:n], refs[n:2 * n], refs[2 * n], refs[2 * n + 1]):
            cp.wait_send()
            cp.wait_recv()

    order = [] if after is None else list(after) if isinstance(after, (list, tuple)) else [after]
    outs = pl.pallas_call(
        body, name=name,
        out_shape=tuple(pltpu.HBM(v.shape, v.dtype) for v in srcs + lands),
        in_specs=[HBM] * (2 * n) + [SEM, SEM] + [pl.BlockSpec(memory_space=pl.ANY)] * len(order),
        out_specs=tuple([HBM] * (2 * n)),
        input_output_aliases={i: i for i in range(2 * n)},
        compiler_params=pltpu.CompilerParams(has_side_effects=EFFECT),
    )(*srcs, *lands, send_sems, recv_sems, *order)
    return list(outs[:n]), list(outs[n:])


def _gather_copies(xs, lands, send_sems, recv_sems):
    x, y, c, chips = _position()
    copies = []
    for a in range(len(xs)):
        for k, peer in enumerate([(x, y, 1 - c)] + [(*chip, c) for chip in chips]):
            copies.append(pltpu.make_async_remote_copy(
                src_ref=xs[a], dst_ref=lands[a].at[4 * x + 2 * y + c],
                send_sem=send_sems.at[4 * a + k], recv_sem=recv_sems.at[4 * a + k], device_id=peer, device_id_type=MESH))
    return copies


def _chips_copies(ps, lands, send_sems, recv_sems):
    x, y, c, chips = _position()
    copies = []
    for a in range(len(ps)):
        for j, chip in enumerate(chips):
            copies.append(pltpu.make_async_remote_copy(
                src_ref=ps[a].at[2 * chip[0] + chip[1]], dst_ref=lands[a].at[j],
                send_sem=send_sems.at[3 * a + j], recv_sem=recv_sems.at[3 * a + j], device_id=(*chip, c),
                device_id_type=MESH))
    return copies


def _sibling_copies(gs, lands, send_sems, recv_sems):
    x, y, c, _ = _position()
    copies = []
    for a in range(len(gs)):
        for q in range(4):
            copies.append(pltpu.make_async_remote_copy(
                src_ref=gs[a].at[2 * q + 1 - c], dst_ref=lands[a].at[q],
                send_sem=send_sems.at[4 * a + q], recv_sem=recv_sems.at[4 * a + q],
                device_id=(x, y, 1 - c), device_id_type=MESH))
    return copies


def _everyone_copies(xs, lands, send_sems, recv_sems):
    x, y, c, _ = _position()
    flip = lambda v, bit: 1 - v if bit else v
    copies = []
    for a in range(len(xs)):
        for k in range(1, NDEV):
            copies.append(pltpu.make_async_remote_copy(
                src_ref=xs[a], dst_ref=lands[a].at[4 * x + 2 * y + c],
                send_sem=send_sems.at[7 * a + k - 1], recv_sem=recv_sems.at[7 * a + k - 1],
                device_id=(flip(x, k & 4), flip(y, k & 2), flip(c, k & 1)), device_id_type=MESH))
    return copies


def _row_chunks(rows, dtype):
    unit = SUBLANE * (4 // jnp.dtype(dtype).itemsize)
    units = rows // unit
    if rows % unit or units < 2:
        return [(0, rows)]
    k = min(DMA_CHUNKS, units)
    sizes = [(units // k + (1 if i < units % k else 0)) * unit for i in range(k)]
    return [(sum(sizes[:i]), sz) for i, sz in enumerate(sizes)]


def _gather_forward(name, lands):
    n = len(lands)

    def body(*refs):
        ins, outs = refs[:n], refs[n:2 * n]
        send_sems, recv_sems = refs[2 * n:]
        x, y, c, chips = _position()
        whole, chunks = [], []
        for a in range(n):
            rows = _row_chunks(ins[a].shape[1], ins[a].dtype)
            for j, chip in enumerate(chips):
                slot = 4 * chip[0] + 2 * chip[1]

                def to_sibling(src, dst):
                    return pltpu.make_async_remote_copy(
                        src_ref=src, dst_ref=dst, send_sem=send_sems.at[a, j], recv_sem=recv_sems.at[a, j],
                        device_id=(x, y, 1 - c), device_id_type=MESH)

                whole.append(to_sibling(ins[a].at[slot + c], outs[a].at[slot + 1 - c]))
                chunks += [to_sibling(ins[a].at[slot + c, pl.ds(r0, nr)], outs[a].at[slot + c, pl.ds(r0, nr)])
                           for r0, nr in rows]
        for cp in chunks:
            cp.start()
        for cp in whole:
            cp.wait()

    return pl.pallas_call(
        body, name=name,
        out_shape=[jax.ShapeDtypeStruct(l.shape, l.dtype) for l in lands],
        in_specs=[HBM] * n, out_specs=[HBM] * n,
        input_output_aliases={a: a for a in range(n)},
        scratch_shapes=[pltpu.SemaphoreType.DMA((n, 3)), pltpu.SemaphoreType.DMA((n, 3))],
    )(*lands)


def _sum_sibling(name, g, land, c_arr):
    _, R, C = g.shape
    tr = _pick(R, 512, SUBLANE)

    def body(c_ref, g_ref, l_ref, o_ref):
        o_ref[...] = (g_ref[...].astype(F32) + l_ref[...].astype(F32)).astype(o_ref.dtype)

    return pl.pallas_call(
        body, name=name,
        grid_spec=pltpu.PrefetchScalarGridSpec(
            num_scalar_prefetch=1, grid=(4, R // tr),
            in_specs=[pl.BlockSpec((None, tr, C), lambda q, i, cr: (2 * q + cr[0], i, 0)),
                      pl.BlockSpec((None, tr, C), lambda q, i, cr: (q, i, 0))],
            out_specs=pl.BlockSpec((None, tr, C), lambda q, i, cr: (q, i, 0))),
        out_shape=jax.ShapeDtypeStruct((4, R, C), g.dtype),
        compiler_params=pltpu.CompilerParams(dimension_semantics=("parallel", "parallel")),
    )(c_arr, g, land)


def _adamw(w, g, m, v):
    m = ADAM_B1 * m + (1.0 - ADAM_B1) * g
    v = ADAM_B2 * v + (1.0 - ADAM_B2) * (g * g)
    m_hat = m / (1.0 - ADAM_B1 ** ADAM_STEP)
    v_hat = v / (1.0 - ADAM_B2 ** ADAM_STEP)
    delta = -ADAM_LR * (m_hat / (jnp.sqrt(v_hat) + ADAM_EPS) + ADAM_WD * w)
    return delta, m, v


def _finish_sharded(name, sums, land, q_arr, w, m, v):
    R, C = w.shape
    tr = _pick(R, 512, SUBLANE)
    tc = _pick(C, max(LANE, EW_BLOCK // tr), LANE)

    def body(q_ref, p_ref, l_ref, w_ref, m_ref, v_ref, g_out, d_out, m_out, v_out):
        g = p_ref[...].astype(F32)
        for j in range(3):
            g = g + l_ref[j].astype(F32)
        d, mn, vn = _adamw(w_ref[...], g, m_ref[...], v_ref[...])
        g_out[...] = g
        d_out[...] = d
        m_out[...] = mn
        v_out[...] = vn

    blk = pl.BlockSpec((tr, tc), lambda i, j, qr: (i, j))
    return pl.pallas_call(
        body, name=name,
        grid_spec=pltpu.PrefetchScalarGridSpec(
            num_scalar_prefetch=1, grid=(R // tr, C // tc),
            in_specs=[pl.BlockSpec((None, tr, tc), lambda i, j, qr: (qr[0], i, j)),
                      pl.BlockSpec((3, tr, tc), lambda i, j, qr: (0, i, j)), blk, blk, blk],
            out_specs=[blk] * 4),
        out_shape=[jax.ShapeDtypeStruct((R, C), F32)] * 4,
        compiler_params=pltpu.CompilerParams(dimension_semantics=("parallel", "parallel")),
    )(q_arr, sums, land, w, m, v)


def _finish_replicated(name, gathered, w, m, v):
    _, R, C = gathered.shape
    tr = _pick(R, 256, SUBLANE)

    def fn(gv, wv, mv, vv):
        g = gv[0]
        for d in range(1, NDEV):
            g = g + gv[d]
        dl, mn, vn = _adamw(wv, g, mv, vv)
        return g, dl, mn, vn

    row = ((tr, C), lambda i: (i, 0))
    return _ew(name, (R // tr,),
               [(gathered, (NDEV, tr, C), lambda i: (0, i, 0)), (w,) + row, (m,) + row, (v,) + row],
               [((R, C), F32) + row] * 4, fn)


def _rms_fwd(name, h, g, deps=()):
    S, D = h.shape
    tr = _pick(S, 256, SUBLANE)

    def fn(hv, gv):
        r = lax.rsqrt(jnp.mean(hv * hv, axis=-1, keepdims=True) + EPS)
        return hv * r * gv, r

    return _ew(name, (S // tr,),
               [(h, (tr, D), lambda i: (i, 0)), (g, (1, D), lambda i: (0, 0))],
               [((S, D), BF, (tr, D), lambda i: (i, 0)), ((S, 1), F32, (tr, 1), lambda i: (i, 0))], fn, deps=deps)


def _rms_bwd(name, du, h, r, g, dres, scale):
    S, D = h.shape
    tr = _pick(S, 256, SUBLANE)

    def fn(duv, hv, rv, gv, drv):
        xn = hv * rv
        dxn = duv * gv
        dh = drv + rv * (dxn - xn * jnp.mean(dxn * xn, axis=-1, keepdims=True))
        return dh, scale * dh, jnp.sum(duv * xn, axis=0, keepdims=True)

    row = ((tr, D), lambda i: (i, 0))
    return _ew(name, (S // tr,),
               [(du,) + row, (h,) + row, (r, (tr, 1), lambda i: (i, 0)), (g, (1, D), lambda i: (0, 0)), (dres,) + row],
               [((S, D), F32) + row, ((S, D), BF) + row, ((1, D), F32, (1, D), lambda i: (0, 0))], fn, acc=(2,))


def _loss_head(name, h, g, target):
    S, D = h.shape
    tr = _pick(S, 256, SUBLANE)

    def fn(hv, gv, tv):
        r = lax.rsqrt(jnp.mean(hv * hv, axis=-1, keepdims=True) + EPS)
        xn = hv * r
        diff = xn * gv - tv
        loss = 0.5 * jnp.sum(jnp.mean(diff * diff, axis=-1, keepdims=True))
        dout = diff / D
        dxn = dout * gv
        dh = r * (dxn - xn * jnp.mean(dxn * xn, axis=-1, keepdims=True))
        return (jnp.zeros((1, LANE), F32) + loss, dh, 0.5 * dh, jnp.sum(dout * xn, axis=0, keepdims=True))

    row = ((tr, D), lambda i: (i, 0))
    return _ew(name, (S // tr,),
               [(h,) + row, (g, (1, D), lambda i: (0, 0)), (target,) + row],
               [((1, LANE), F32, (1, LANE), lambda i: (0, 0)), ((S, D), F32) + row, ((S, D), BF) + row,
                ((1, D), F32, (1, D), lambda i: (0, 0))], fn, acc=(0, 3))


def _ffn_fwd(tag, u, h, wg, wu, wd, deps=()):
    S, D = u.shape
    Fs = wg.shape[1]
    tm, tk = _pick(S, M_TILE, SUBLANE), _pick(D, K_TILE, LANE)

    def up_epilogue(accs, sides):
        gt, up = accs
        return gt, up, gt * _sigmoid(gt) * up

    act = ((NDEV, S, Fs), BF, (None, tm, Fs), lambda b, i, j, k: (b, i, 0))
    gt, up, a = _mm(
        tag + "_up", (NDEV, S // tm, 1, D // tk),
        [(u, (tm, tk), lambda b, i, j, k: (i, k)),
         (wg, (None, Fs, tk), lambda b, i, j, k: (b, 0, k)), (wu, (None, Fs, tk), lambda b, i, j, k: (b, 0, k))],
        [(0, 1, NT, 0), (0, 2, NT, 1)], [], [act, act, act], up_epilogue, [(tm, Fs), (tm, Fs)], deps=deps)
    if callable(wd):
        wd = wd(a)
    tn = _pick(D, 1024, LANE)
    (hn,) = _mm(
        tag + "_down", (1, S // tm, D // tn, NDEV),
        [(a, (None, tm, Fs), lambda b, i, j, k: (k, i, 0)), (wd, (None, Fs, tn), lambda b, i, j, k: (k, 0, j))],
        [(0, 1, NN, 0)], [(h, (tm, tn), lambda b, i, j, k: (i, j))],
        [((S, D), F32, (tm, tn), lambda b, i, j, k: (i, j))],
        lambda accs, sides: [sides[0] + 0.5 * accs[0]], [(tm, tn)])
    return hn, (gt, up, a), wd


def _ffn_bwd(tag, dhs, u, saved, wg, wu, wd, deps=()):
    gt, up, a = saved
    S, D = u.shape
    Fs = wg.shape[1]
    tm, tk = _pick(S, M_TILE, SUBLANE), _pick(D, K_TILE, LANE)
    act_in = lambda arr: (arr, (None, tm, Fs), lambda b, i, j, k: (b, i, 0))
    act_out = ((NDEV, S, Fs), BF, (None, tm, Fs), lambda b, i, j, k: (b, i, 0))

    def act_epilogue(accs, sides):
        da = accs[0]
        gtv, upv = sides[0].astype(F32), sides[1].astype(F32)
        sg = _sigmoid(gtv)
        return da * upv * sg * (1.0 + gtv * (1.0 - sg)), da * gtv * sg

    dgt, dup = _mm(
        tag + "_dact", (NDEV, S // tm, 1, D // tk),
        [(dhs, (tm, tk), lambda b, i, j, k: (i, k)), (wd, (None, Fs, tk), lambda b, i, j, k: (b, 0, k))],
        [(0, 1, NT, 0)], [act_in(gt), act_in(up)], [act_out, act_out], act_epilogue, [(tm, Fs)], deps=deps)

    ts = _pick(S, K_TILE, SUBLANE)
    tn = _pick(D, 1024, LANE)
    wgrad = ((NDEV, Fs, D), BF, (None, Fs, tn), lambda b, i, j, k: (b, 0, j))
    tok = lambda arr: (arr, (None, ts, Fs), lambda b, i, j, k: (b, k, 0))

    def grad_down(deps=()):
        return _mm(
            tag + "_dwd", (NDEV, 1, D // tn, S // ts),
            [tok(a), (dhs, (ts, tn), lambda b, i, j, k: (k, j))],
            [(0, 1, TN, 0)], [], [wgrad], lambda accs, sides: accs, [(Fs, tn)], deps=deps)[0]

    def grad_gate_up(deps=()):
        return _mm(
            tag + "_dwgu", (NDEV, 1, D // tn, S // ts),
            [tok(dgt), tok(dup), (u, (ts, tn), lambda b, i, j, k: (k, j))],
            [(0, 2, TN, 0), (1, 2, TN, 1)], [], [wgrad, wgrad], lambda accs, sides: accs, [(Fs, tn), (Fs, tn)],
            deps=deps)

    def du(deps=()):
        return _mm(
            tag + "_du", (1, S // tm, D // tn, NDEV),
            [(dgt, (None, tm, Fs), lambda b, i, j, k: (k, i, 0)), (dup, (None, tm, Fs), lambda b, i, j, k: (k, i, 0)),
             (wg, (None, Fs, tn), lambda b, i, j, k: (k, 0, j)), (wu, (None, Fs, tn), lambda b, i, j, k: (k, 0, j))],
            [(0, 2, NN, 0), (1, 3, NN, 0)], [], [((S, D), F32, (tm, tn), lambda b, i, j, k: (i, j))],
            lambda accs, sides: accs, [(tm, tn)], deps=deps)[0]

    return grad_down, grad_gate_up, du


def _ssm_params(lam_re, lam_im, log_dt, b_re, b_im, c_re, c_im):
    G, N = lam_re.shape
    C = b_re.shape[2]
    lam_re = jnp.minimum(lam_re, -1e-4)
    dt = jnp.exp(log_dt)[:, None]
    mag = jnp.exp(lam_re * dt)
    a_re = mag * jnp.cos(lam_im * dt)
    a_im = mag * jnp.sin(lam_im * dt)
    den = lam_re * lam_re + lam_im * lam_im
    p = a_re - 1.0
    f_re = ((p * lam_re + a_im * lam_im) / den)[:, :, None]
    f_im = ((a_im * lam_re - p * lam_im) / den)[:, :, None]
    bb_re = f_re * b_re - f_im * b_im
    bb_im = f_re * b_im + f_im * b_re
    gpt = LANE // C
    tiles = G // gpt
    eye = jnp.eye(gpt, dtype=F32)

    def bd(bb):
        return jnp.einsum("bgnc,gh->bgchn", bb.reshape(tiles, gpt, N, C), eye).reshape(tiles, gpt * C, gpt * N)

    def cd(cc):
        return jnp.einsum("bgcn,gh->bgnhc", cc.reshape(tiles, gpt, C, N), eye).reshape(tiles, gpt * N, gpt * C)

    rows = G * N // LANE
    return (a_re.reshape(rows, LANE), a_im.reshape(rows, LANE), bd(bb_re), bd(bb_im), cd(c_re), cd(-c_im))


def _tile_states(ref3, b, per):
    return jnp.concatenate([ref3[:, per * b + r, :] for r in range(per)], axis=1).astype(BF)


def _ssm_spread(name, x, m_re, m_im, dims, rows):
    S, W = x.shape
    tiles = m_re.shape[0]
    tch, per = W // tiles, rows // tiles
    tq = _pick(S, 256, SUBLANE)

    def body(x_ref, mre_ref, mim_ref, ore_ref, oim_ref):
        for b in range(tiles):
            xb = x_ref[:, b * tch:(b + 1) * tch]
            for m_ref, o_ref in ((mre_ref, ore_ref), (mim_ref, oim_ref)):
                val = lax.dot_general(xb, m_ref[b], (dims, ((), ())), preferred_element_type=F32)
                for r in range(per):
                    o_ref[:, per * b + r, :] = val[:, r * LANE:(r + 1) * LANE]

    whole = lambda m: pl.BlockSpec(m.shape, lambda i: (0, 0, 0))
    st = pl.BlockSpec((tq, rows, LANE), lambda i: (i, 0, 0))
    return pl.pallas_call(
        body, name=name, grid=(S // tq,),
        in_specs=[pl.BlockSpec((tq, W), lambda i: (i, 0)), whole(m_re), whole(m_im)], out_specs=[st, st],
        out_shape=[jax.ShapeDtypeStruct((S, rows, LANE), F32)] * 2,
        compiler_params=pltpu.CompilerParams(dimension_semantics=("parallel",)),
    )(x, m_re, m_im)


def _ssm_collect(name, z_re3, z_im3, m_re, m_im, dims, side, gain, epilogue, out_dtypes):
    S, rows, _ = z_re3.shape
    tiles = m_re.shape[0]
    W = side.shape[1]
    tch, per = W // tiles, rows // tiles
    tq = _pick(S, 256, SUBLANE)
    n_out = len(out_dtypes)

    def body(zre_ref, zim_ref, mre_ref, mim_ref, side_ref, gain_ref, *out_refs):
        for b in range(tiles):
            cols = slice(b * tch, (b + 1) * tch)
            acc = lax.dot_general(_tile_states(zre_ref, b, per), mre_ref[b], (dims, ((), ())),
                                  preferred_element_type=F32)
            acc = acc + lax.dot_general(_tile_states(zim_ref, b, per), mim_ref[b], (dims, ((), ())),
                                        preferred_element_type=F32)
            for o, v in zip(out_refs, epilogue(acc, side_ref[:, cols], gain_ref[:, cols])):
                o[:, cols] = v.astype(o.dtype)

    whole = lambda m: pl.BlockSpec(m.shape, lambda i: (0, 0, 0))
    st = pl.BlockSpec((tq, rows, LANE), lambda i: (i, 0, 0))
    ch = pl.BlockSpec((tq, W), lambda i: (i, 0))
    return pl.pallas_call(
        body, name=name, grid=(S // tq,),
        in_specs=[st, st, whole(m_re), whole(m_im), ch, pl.BlockSpec((1, W), lambda i: (0, 0))],
        out_specs=[ch] * n_out, out_shape=[jax.ShapeDtypeStruct((S, W), dt) for dt in out_dtypes],
        compiler_params=pltpu.CompilerParams(dimension_semantics=("parallel",)),
    )(z_re3, z_im3, m_re, m_im, side, gain)


def _ssm_outer(name, x, z_re3, z_im3, x_first):
    S, W = x.shape
    rows = z_re3.shape[1]
    tiles = W // LANE
    per = rows // tiles
    tc = _pick(S, 256, SUBLANE)
    shape = (tiles, LANE, per * LANE) if x_first else (tiles, per * LANE, LANE)

    def body(x_ref, zre_ref, zim_ref, ore_ref, oim_ref):
        first = pl.program_id(0) == 0
        for b in range(tiles):
            xb = x_ref[:, b * LANE:(b + 1) * LANE]
            for z_ref, o_ref in ((zre_ref, ore_ref), (zim_ref, oim_ref)):
                zb = _tile_states(z_ref, b, per)
                val = lax.dot_general(xb if x_first else zb, zb if x_first else xb, (TN, ((), ())),
                                      preferred_element_type=F32)

                @pl.when(first)
                def _(o_ref=o_ref, val=val, b=b):
                    o_ref[b] = val

                @pl.when(jnp.logical_not(first))
                def _(o_ref=o_ref, val=val, b=b):
                    o_ref[b] += val

    st = pl.BlockSpec((tc, rows, LANE), lambda i: (i, 0, 0))
    out = pl.BlockSpec(shape, lambda i: (0, 0, 0))
    return pl.pallas_call(
        body, name=name, grid=(S // tc,),
        in_specs=[pl.BlockSpec((tc, W), lambda i: (i, 0)), st, st], out_specs=[out, out],
        out_shape=[jax.ShapeDtypeStruct(shape, F32)] * 2,
        compiler_params=pltpu.CompilerParams(dimension_semantics=("arbitrary",)),
    )(x, z_re3, z_im3)


def _scan_fwd(bu_re, bu_im, a_re, a_im):
    S, R, _ = bu_re.shape
    tc = _pick(S, 256, SUBLANE)

    def body(bre, bim, are, aim, sre, sim, carry):
        @pl.when(pl.program_id(0) == 0)
        def _():
            carry[...] = jnp.zeros_like(carry)

        ar, ai = are[...], aim[...]

        def step(t, c):
            pr, pi = c
            nr = ar * pr - ai * pi + bre[t]
            ni = ar * pi + ai * pr + bim[t]
            sre[t] = nr
            sim[t] = ni
            return nr, ni

        pr, pi = lax.fori_loop(0, tc, step, (carry[0], carry[1]), unroll=8)
        carry[0] = pr
        carry[1] = pi

    blk = pl.BlockSpec((tc, R, LANE), lambda i: (i, 0, 0))
    par = pl.BlockSpec((R, LANE), lambda i: (0, 0))
    return pl.pallas_call(
        body, name="ssm_scan_fwd", grid=(S // tc,),
        in_specs=[blk, blk, par, par], out_specs=[blk, blk],
        out_shape=[jax.ShapeDtypeStruct((S, R, LANE), F32)] * 2,
        scratch_shapes=[pltpu.VMEM((2, R, LANE), F32)],
        compiler_params=pltpu.CompilerParams(dimension_semantics=("arbitrary",)),
    )(bu_re, bu_im, a_re, a_im)


def _scan_bwd(ds_re, ds_im, s_re, s_im, a_re, a_im):
    S, R, _ = ds_re.shape
    tc = _pick(S, 256, SUBLANE)
    nc = S // tc

    def body(dre, dim_, sre, sim, are, aim, lre, lim, dar, dai, carry):
        @pl.when(pl.program_id(0) == 0)
        def _():
            carry[...] = jnp.zeros_like(carry)
            dar[...] = jnp.zeros_like(dar)
            dai[...] = jnp.zeros_like(dai)

        ar, ai = are[...], aim[...]

        def step(tt, c):
            t = tc - 1 - tt
            lr, li, gr, gi = c
            sr, si = sre[t], sim[t]
            gr = gr + lr * sr + li * si
            gi = gi + li * sr - lr * si
            nlr = dre[t] + ar * lr + ai * li
            nli = dim_[t] + ar * li - ai * lr
            lre[t] = nlr
            lim[t] = nli
            return nlr, nli, gr, gi

        lr, li, gr, gi = lax.fori_loop(0, tc, step, (carry[0], carry[1], dar[...], dai[...]), unroll=8)
        carry[0] = lr
        carry[1] = li
        dar[...] = gr
        dai[...] = gi

    blk = pl.BlockSpec((tc, R, LANE), lambda i: (nc - 1 - i, 0, 0))
    par = pl.BlockSpec((R, LANE), lambda i: (0, 0))
    return pl.pallas_call(
        body, name="ssm_scan_bwd", grid=(nc,),
        in_specs=[blk, blk, blk, blk, par, par], out_specs=[blk, blk, par, par],
        out_shape=[jax.ShapeDtypeStruct((S, R, LANE), F32)] * 2 + [jax.ShapeDtypeStruct((R, LANE), F32)] * 2,
        scratch_shapes=[pltpu.VMEM((2, R, LANE), F32)],
        compiler_params=pltpu.CompilerParams(dimension_semantics=("arbitrary",)),
    )(ds_re, ds_im, s_re, s_im, a_re, a_im)


def _shift_down(z, k):
    t = lax.broadcasted_iota(I32, z.shape, 0)
    return jnp.where(t >= k, pltpu.roll(z, k, 0), 0.0)


def _shift_up(z, k):
    n = z.shape[0]
    t = lax.broadcasted_iota(I32, z.shape, 0)
    return jnp.where(t < n - k, pltpu.roll(z, n - k, 0), 0.0)


def _conv_fwd(proj, cw, cb):
    _, S, W = proj.shape
    ct = _pick(W, 256, LANE)

    def fn(bg, cg, val, w, b):
        z = cg * val
        conv = b + w[0:1] * _shift_down(z, 2) + w[1:2] * _shift_down(z, 1) + w[2:3] * z
        return bg * conv, conv

    sl = lambda s: (proj, (None, S, ct), lambda j, s=s: (s, 0, j))
    col = ((S, ct), lambda j: (0, j))
    return _ew("conv_fwd", (W // ct,),
               [sl(1), sl(2), sl(3), (cw, (3, ct), lambda j: (0, j)), (cb, (1, ct), lambda j: (0, j))],
               [((S, W), BF) + col, ((S, W), F32) + col], fn)


def _conv_bwd(dyb, proj, conv, cw):
    _, S, W = proj.shape
    ct = _pick(W, 256, LANE)

    def fn(dy, bg, cg, val, cv, w):
        z = cg * val
        z1, z2 = _shift_down(z, 1), _shift_down(z, 2)
        dconv = dy * bg
        dz = w[2:3] * dconv + w[1:2] * _shift_up(dconv, 1) + w[0:1] * _shift_up(dconv, 2)
        dw = jnp.concatenate([jnp.sum(dconv * z2, axis=0, keepdims=True), jnp.sum(dconv * z1, axis=0, keepdims=True),
                              jnp.sum(dconv * z, axis=0, keepdims=True)], axis=0)
        return dy * cv, dz * val, dz * cg, dw, jnp.sum(dconv, axis=0, keepdims=True)

    sl = lambda s: (proj, (None, S, ct), lambda j, s=s: (s, 0, j))
    col = ((S, ct), lambda j: (0, j))
    return _ew("conv_bwd", (W // ct,),
               [(dyb,) + col, sl(1), sl(2), sl(3), (conv,) + col, (cw, (3, ct), lambda j: (0, j))],
               [((S, W), BF) + col, ((S, W), BF) + col, ((S, W), BF) + col,
                ((3, W), F32, (3, ct), lambda j: (0, j)), ((1, W), F32, (1, ct), lambda j: (0, j))], fn)


def _plain(accs, sides):
    return accs


def kernel(x, ffn1_norm, ffn1_w_gate, ffn1_w_up, ffn1_w_down, mix_norm, w_in, ssm_lambda_re, ssm_lambda_im, ssm_log_dt, ssm_b_re, ssm_b_im, ssm_c_re, ssm_c_im, ssm_d, ssm_w_glu, ssm_b_glu, ssm_w_out, conv_w, conv_b, conv_w_out, w_o, ffn2_norm, ffn2_w_gate, ffn2_w_up, ffn2_w_down, final_norm, loss_target, m_ffn1_norm, m_ffn1_w_gate, m_ffn1_w_up, m_ffn1_w_down, m_mix_norm, m_w_in, m_ssm_lambda_re, m_ssm_lambda_im, m_ssm_log_dt, m_ssm_b_re, m_ssm_b_im, m_ssm_c_re, m_ssm_c_im, m_ssm_d, m_ssm_w_glu, m_ssm_b_glu, m_ssm_w_out, m_conv_w, m_conv_b, m_conv_w_out, m_w_o, m_ffn2_norm, m_ffn2_w_gate, m_ffn2_w_up, m_ffn2_w_down, m_final_norm, v_ffn1_norm, v_ffn1_w_gate, v_ffn1_w_up, v_ffn1_w_down, v_mix_norm, v_w_in, v_ssm_lambda_re, v_ssm_lambda_im, v_ssm_log_dt, v_ssm_b_re, v_ssm_b_im, v_ssm_c_re, v_ssm_c_im, v_ssm_d, v_ssm_w_glu, v_ssm_b_glu, v_ssm_w_out, v_conv_w, v_conv_b, v_conv_w_out, v_w_o, v_ffn2_norm, v_ffn2_w_gate, v_ffn2_w_up, v_ffn2_w_down, v_final_norm):
    P = dict(ffn1_norm=ffn1_norm, ffn1_w_gate=ffn1_w_gate, ffn1_w_up=ffn1_w_up, ffn1_w_down=ffn1_w_down, mix_norm=mix_norm, w_in=w_in, ssm_lambda_re=ssm_lambda_re, ssm_lambda_im=ssm_lambda_im, ssm_log_dt=ssm_log_dt, ssm_b_re=ssm_b_re, ssm_b_im=ssm_b_im, ssm_c_re=ssm_c_re, ssm_c_im=ssm_c_im, ssm_d=ssm_d, ssm_w_glu=ssm_w_glu, ssm_b_glu=ssm_b_glu, ssm_w_out=ssm_w_out, conv_w=conv_w, conv_b=conv_b, conv_w_out=conv_w_out, w_o=w_o, ffn2_norm=ffn2_norm, ffn2_w_gate=ffn2_w_gate, ffn2_w_up=ffn2_w_up, ffn2_w_down=ffn2_w_down, final_norm=final_norm)
    M = dict(ffn1_norm=m_ffn1_norm, ffn1_w_gate=m_ffn1_w_gate, ffn1_w_up=m_ffn1_w_up, ffn1_w_down=m_ffn1_w_down, mix_norm=m_mix_norm, w_in=m_w_in, ssm_lambda_re=m_ssm_lambda_re, ssm_lambda_im=m_ssm_lambda_im, ssm_log_dt=m_ssm_log_dt, ssm_b_re=m_ssm_b_re, ssm_b_im=m_ssm_b_im, ssm_c_re=m_ssm_c_re, ssm_c_im=m_ssm_c_im, ssm_d=m_ssm_d, ssm_w_glu=m_ssm_w_glu, ssm_b_glu=m_ssm_b_glu, ssm_w_out=m_ssm_w_out, conv_w=m_conv_w, conv_b=m_conv_b, conv_w_out=m_conv_w_out, w_o=m_w_o, ffn2_norm=m_ffn2_norm, ffn2_w_gate=m_ffn2_w_gate, ffn2_w_up=m_ffn2_w_up, ffn2_w_down=m_ffn2_w_down, final_norm=m_final_norm)
    V = dict(ffn1_norm=v_ffn1_norm, ffn1_w_gate=v_ffn1_w_gate, ffn1_w_up=v_ffn1_w_up, ffn1_w_down=v_ffn1_w_down, mix_norm=v_mix_norm, w_in=v_w_in, ssm_lambda_re=v_ssm_lambda_re, ssm_lambda_im=v_ssm_lambda_im, ssm_log_dt=v_ssm_log_dt, ssm_b_re=v_ssm_b_re, ssm_b_im=v_ssm_b_im, ssm_c_re=v_ssm_c_re, ssm_c_im=v_ssm_c_im, ssm_d=v_ssm_d, ssm_w_glu=v_ssm_w_glu, ssm_b_glu=v_ssm_b_glu, ssm_w_out=v_ssm_w_out, conv_w=v_conv_w, conv_b=v_conv_b, conv_w_out=v_conv_w_out, w_o=v_w_o, ffn2_norm=v_ffn2_norm, ffn2_w_gate=v_ffn2_w_gate, ffn2_w_up=v_ffn2_w_up, ffn2_w_down=v_ffn2_w_down, final_norm=v_final_norm)
    names = list(P)
    sharded = ["ffn1_w_gate", "ffn1_w_up", "ffn1_w_down", "w_in", "ssm_w_glu", "ssm_w_out", "conv_w_out", "w_o",
               "ffn2_w_gate", "ffn2_w_up", "ffn2_w_down"]
    replicated = [n for n in names if n not in sharded and n != "conv_w"]

    S, D = x.shape[1], x.shape[2]
    W = ssm_d.shape[0]
    Dc = D // NDEV
    G, N = ssm_lambda_re.shape
    rows = G * N // LANE
    xh = x.reshape(S, D)
    target = loss_target.reshape(S, D)
    xi, yi, ci = lax.axis_index("x"), lax.axis_index("y"), lax.axis_index("c")
    c_arr = jnp.reshape(ci, (1,)).astype(I32)
    q_arr = jnp.reshape(2 * xi + yi, (1,)).astype(I32)
    row = lambda v: v.reshape(1, -1)

    transposed = ("ffn1_w_gate", "ffn1_w_up", "ffn2_w_gate", "ffn2_w_up")
    local = lambda table, n: table[n].T if n in transposed else table[n]

    groups = [sharded[0:2], sharded[2:3], ["w_in", "conv_w"], sharded[4:8], sharded[8:11]]
    started = {}
    me = 4 * xi + 2 * yi + ci

    def gather_start(gi):
        hold = 0.0 * started[0][4][0, 0] if gi else 0.0
        srcs = [conv_w + hold if n == "conv_w" else (local(P, n) + hold).astype(BF) for n in groups[gi]]
        lands = [lax.dynamic_update_slice(lax.empty((NDEV,) + s.shape, s.dtype), s[None], (me,) + (0,) * s.ndim)
                 for s in srcs]
        after = [started[gi - 1][4]] if gi else []
        started[gi] = _split_start("gather_start_%d" % gi, srcs, lands, _gather_copies, 4, after)
        return started[gi][4]

    def gathered(gi, after):
        _, lands = _split_wait("gather_wait_%d" % gi, started[gi], _gather_copies, after)
        return _gather_forward("gather_forward_%d" % gi, lands)

    tm = _pick(S, M_TILE, SUBLANE)
    th = _pick(S, M_TILE // 2, SUBLANE)
    tk = _pick(D, K_TILE, LANE)
    ts = _pick(S, K_TILE, SUBLANE)
    tn = _pick(D, 1024, LANE)

    tokens = [gather_start(gi) for gi in range(len(groups))]
    u1, r1 = _rms_fwd("rms1", xh, row(ffn1_norm), deps=tokens)
    wg1, wu1 = gathered(0, u1)
    h1, ffn1_saved, wd1 = _ffn_fwd("ffn1", u1, xh, wg1, wu1, lambda after: gathered(1, after)[0])
    u2, r2 = _rms_fwd("rms2", h1, row(mix_norm))
    w_in_f, cw_f = gathered(2, u2)
    cw = jnp.transpose(cw_f, (1, 0, 2)).reshape(3, W)
    (proj,) = _mm(
        "in_proj", (NDEV, S // tm, 1, D // tk),
        [(u2, (tm, tk), lambda b, i, j, k: (i, k)), (w_in_f, (None, tk, W), lambda b, i, j, k: (b, k, 0))],
        [(0, 1, NN, 0)], [], [((NDEV, S, W), F32, (None, tm, W), lambda b, i, j, k: (b, i, 0))], _plain, [(tm, W)])

    ssm_in = (ssm_lambda_re, ssm_lambda_im, ssm_log_dt, ssm_b_re, ssm_b_im, ssm_c_re, ssm_c_im)
    (a_re, a_im, bd_re, bd_im, cd_re, cd_imn), ssm_vjp = jax.vjp(_ssm_params, *ssm_in)
    bd_re_b, bd_im_b, cd_re_b, cd_imn_b = (t.astype(BF) for t in (bd_re, bd_im, cd_re, cd_imn))
    v_f = proj[0]
    v_bf = v_f.astype(BF)
    bu_re3, bu_im3 = _ssm_spread("ssm_bu", v_bf, bd_re_b, bd_im_b, NN, rows)
    s_re3, s_im3 = _scan_fwd(bu_re3, bu_im3, a_re, a_im)

    def y0_epilogue(acc, v_tile, d_tile):
        y0 = acc + d_tile * v_tile
        return y0, _gelu(y0)

    y0, y1 = _ssm_collect("ssm_y0", s_re3, s_im3, cd_re_b, cd_imn_b, NN, v_f, row(ssm_d), y0_epilogue, [F32, BF])

    tw = _pick(W, 512, LANE)
    w_glu_f, w_so, w_co, w_o_f = gathered(3, y1)
    w_glu_f = w_glu_f.reshape(W, W)
    w_o_f = w_o_f.reshape(D, D)

    def glu_epilogue(accs, sides):
        q = accs[0] + sides[1]
        return q, _gelu(sides[0]) * _sigmoid(q)

    q_pre, y2 = _mm("ssm_glu", (1, S // tm, W // tw, 1),
                    [(y1, (tm, W), lambda b, i, j, k: (i, 0)), (w_glu_f, (W, tw), lambda b, i, j, k: (0, j))],
                    [(0, 1, NN, 0)],
                    [(y0, (tm, tw), lambda b, i, j, k: (i, j)), (row(ssm_b_glu), (1, tw), lambda b, i, j, k: (0, j))],
                    [((S, W), F32, (tm, tw), lambda b, i, j, k: (i, j)), ((S, W), BF, (tm, tw), lambda b, i, j, k: (i, j))],
                    glu_epilogue, [(tm, tw)])

    yb, conv = _conv_fwd(proj, cw, row(conv_b))

    per = W // Dc
    ga_blk = (proj, (None, tm, Dc), lambda b, i, j, k: (4 + b // per, i, b % per))
    gb_blk = (proj, (None, tm, Dc), lambda b, i, j, k: (6 + b // per, i, b % per))
    dc_out = ((S, D), BF, (tm, Dc), lambda b, i, j, k: (i, b))

    def merge_epilogue(accs, sides):
        za, zb = accs
        return _sigmoid(sides[0]) * za + _sigmoid(sides[1]) * zb, za, zb

    merged, z_a, z_b = _mm(
        "mix_merge", (NDEV, S // tm, 1, 1),
        [(y2, (tm, W), lambda b, i, j, k: (i, 0)), (yb, (tm, W), lambda b, i, j, k: (i, 0)),
         (w_so, (None, W, Dc), lambda b, i, j, k: (b, 0, 0)), (w_co, (None, W, Dc), lambda b, i, j, k: (b, 0, 0))],
        [(0, 2, NN, 0), (1, 3, NN, 1)], [ga_blk, gb_blk], [dc_out, dc_out, dc_out], merge_epilogue, [(tm, Dc)] * 2)

    (h2,) = _mm("mix_out", (1, S // tm, D // tn, D // tk),
                [(merged, (tm, tk), lambda b, i, j, k: (i, k)), (w_o_f, (tk, tn), lambda b, i, j, k: (k, j))],
                [(0, 1, NN, 0)], [(h1, (tm, tn), lambda b, i, j, k: (i, j))],
                [((S, D), F32, (tm, tn), lambda b, i, j, k: (i, j))],
                lambda accs, sides: [sides[0] + accs[0]], [(tm, tn)])

    u3, r3 = _rms_fwd("rms3", h2, row(ffn2_norm))
    wg2, wu2, wd2 = gathered(4, u3)
    h3, ffn2_saved, _ = _ffn_fwd("ffn2", u3, h2, wg2, wu2, wd2)
    loss_vec, dh3, dh3_half, d_final_norm = _loss_head("loss_head", h3, row(final_norm), target)
    loss = lax.psum(loss_vec[0, 0], ("x", "y", "c"))
    loss_done = jnp.zeros((SUBLANE, LANE), F32) + loss

    grads, deltas, new_m, new_v = {}, {}, {}, {}

    def rs_sibling_start(tag, parts):
        lands = [lax.empty((4,) + p.shape[1:], p.dtype) for p in parts]
        return _split_start("rs_sibling_start_" + tag, parts, lands, _sibling_copies, 4)

    def rs_chips_start(tag, sibling_begun, after):
        parts, lands = _split_wait("rs_sibling_wait_" + tag, sibling_begun, _sibling_copies, after)
        sums = [_sum_sibling("rs_sum_%s_%d" % (tag, a), p, land, c_arr) for a, (p, land) in enumerate(zip(parts, lands))]
        lands2 = [lax.empty((3,) + sm.shape[1:], sm.dtype) for sm in sums]
        return _split_start("rs_chips_start_" + tag, sums, lands2, _chips_copies, 3)

    def rs_end(tag, group, begun, after):
        sums, lands2 = _split_wait("rs_chips_wait_" + tag, begun, _chips_copies, after)
        for n, sm, land2 in zip(group, sums, lands2):
            res = _finish_sharded("adamw_" + n, sm, land2, q_arr, local(P, n), local(M, n), local(V, n))
            grads[n], deltas[n], new_m[n], new_v[n] = [t.T if n in transposed else t for t in res]

    f2_dwd, f2_dwgu, f2_du = _ffn_bwd("ffn2", dh3_half, u3, ffn2_saved, wg2, wu2, wd2, deps=[loss_done])
    dwd2 = f2_dwd()
    dwg2, dwu2 = f2_dwgu()
    sib_ffn2 = rs_sibling_start("ffn2", [dwg2, dwu2, dwd2])
    du3 = f2_du(deps=[sib_ffn2[4]])
    dh2, dh2_b, d_ffn2_norm = _rms_bwd("rms3_bwd", du3, h2, r3, row(ffn2_norm), dh3, 1.0)
    rs_ffn2 = rs_chips_start("ffn2", sib_ffn2, dh2)

    dg_out = ((2, S, W), BF, (None, tm, Dc), lambda b, i, j, k: (j // per, i, j % per))
    ga_blk2 = (proj, (None, tm, Dc), lambda b, i, j, k: (4 + j // per, i, j % per))
    gb_blk2 = (proj, (None, tm, Dc), lambda b, i, j, k: (6 + j // per, i, j % per))
    dcj = lambda arr: (arr, (tm, Dc), lambda b, i, j, k: (i, j))
    dcj_out = ((S, D), BF, (tm, Dc), lambda b, i, j, k: (i, j))

    def dmerge_epilogue(accs, sides):
        dm = accs[0]
        sa, sb = _sigmoid(sides[0]), _sigmoid(sides[1])
        za, zb = sides[2].astype(F32), sides[3].astype(F32)
        return dm * sa, dm * sb, dm * za * sa * (1.0 - sa), dm * zb * sb * (1.0 - sb)

    dz_a, dz_b, dga, dgb = _mm(
        "mix_out_dx", (1, S // tm, NDEV, D // tk),
        [(dh2_b, (tm, tk), lambda b, i, j, k: (i, k)), (w_o_f, (Dc, tk), lambda b, i, j, k: (j, k))],
        [(0, 1, NT, 0)], [ga_blk2, gb_blk2, dcj(z_a), dcj(z_b)], [dcj_out, dcj_out, dg_out, dg_out],
        dmerge_epilogue, [(tm, Dc)], deps=[rs_ffn2[4]])

    td = _pick(D, M_TILE, LANE)
    (dw_o,) = _mm("mix_out_dw", (1, D // td, D // tn, S // ts),
                  [(merged, (ts, td), lambda b, i, j, k: (k, i)), (dh2_b, (ts, tn), lambda b, i, j, k: (k, j))],
                  [(0, 1, TN, 0)], [], [((D, D), BF, (td, tn), lambda b, i, j, k: (i, j))], _plain, [(td, tn)])

    wout = ((NDEV, W, Dc), BF, (None, W, Dc), lambda b, i, j, k: (b, 0, 0))
    dw_so, dw_co = _mm(
        "mix_merge_dw", (NDEV, 1, 1, S // ts),
        [(y2, (ts, W), lambda b, i, j, k: (k, 0)), (yb, (ts, W), lambda b, i, j, k: (k, 0)),
         (dz_a, (ts, Dc), lambda b, i, j, k: (k, b)), (dz_b, (ts, Dc), lambda b, i, j, k: (k, b))],
        [(0, 2, TN, 0), (1, 3, TN, 1)], [], [wout, wout], _plain, [(W, Dc)] * 2)

    def dglu_epilogue(accs, sides):
        dy2, dyb = accs
        sq = _sigmoid(sides[1])
        return dy2 * _gelu(sides[0]) * sq * (1.0 - sq), dy2 * sq, dyb

    full_w = lambda arr: (arr, (th, W), lambda b, i, j, k: (i, 0))
    full_w_out = lambda dt: ((S, W), dt, (th, W), lambda b, i, j, k: (i, 0))
    dq, dy1p, dyb = _mm(
        "mix_merge_dx", (1, S // th, 1, NDEV),
        [(dz_a, (th, Dc), lambda b, i, j, k: (i, k)), (dz_b, (th, Dc), lambda b, i, j, k: (i, k)),
         (w_so, (None, W, Dc), lambda b, i, j, k: (k, 0, 0)), (w_co, (None, W, Dc), lambda b, i, j, k: (k, 0, 0))],
        [(0, 2, NT, 0), (1, 3, NT, 1)], [full_w(y0), full_w(q_pre)], [full_w_out(BF), full_w_out(F32), full_w_out(F32)],
        dglu_epilogue, [(th, W)] * 2)

    def dy0_epilogue(accs, sides):
        dy0 = (sides[0] + accs[0]) * _gelu_grad(sides[1])
        return dy0, dy0

    wj = lambda arr: (arr, (tm, tw), lambda b, i, j, k: (i, j))
    dy0, dy0_b = _mm("ssm_glu_dx", (1, S // tm, W // tw, 1),
                     [(dq, (tm, W), lambda b, i, j, k: (i, 0)), (w_glu_f, (tw, W), lambda b, i, j, k: (j, 0))],
                     [(0, 1, NT, 0)], [wj(dy1p), wj(y0)],
                     [((S, W), F32, (tm, tw), lambda b, i, j, k: (i, j)), ((S, W), BF, (tm, tw), lambda b, i, j, k: (i, j))],
                     dy0_epilogue, [(tm, tw)])

    (dw_glu,) = _mm("ssm_glu_dw", (1, W // tw, 1, S // ts),
                    [(y1, (ts, tw), lambda b, i, j, k: (k, i)), (dq, (ts, W), lambda b, i, j, k: (k, 0))],
                    [(0, 1, TN, 0)], [], [((W, W), BF, (tw, W), lambda b, i, j, k: (i, 0))], _plain, [(tw, W)])

    tr = _pick(S, 256, SUBLANE)
    rw = ((tr, W), lambda i: (i, 0))
    vec_w = ((1, W), F32, (1, W), lambda i: (0, 0))
    d_b_glu, d_ssm_d = _ew(
        "ssm_colsums", (S // tr,), [(dq,) + rw, (dy0,) + rw, (proj, (None, tr, W), lambda i: (0, i, 0))],
        [vec_w, vec_w],
        lambda dqv, dyv, vv: (jnp.sum(dqv.astype(F32), axis=0, keepdims=True), jnp.sum(dyv * vv, axis=0, keepdims=True)),
        acc=(0, 1))

    ds_re3, ds_im3 = _ssm_spread("ssm_ds", dy0_b, cd_re_b, cd_imn_b, NT, rows)
    lam_re3, lam_im3, da_re, da_im = _scan_bwd(ds_re3, ds_im3, s_re3, s_im3, a_re, a_im)
    (dv,) = _ssm_collect("ssm_dv", lam_re3, lam_im3, bd_re_b, bd_im_b, NT, dy0, row(ssm_d),
                         lambda acc, dy_tile, d_tile: [acc + dy_tile * d_tile], [BF])
    dbd_re, dbd_im = _ssm_outer("ssm_dbd", v_bf, lam_re3, lam_im3, True)
    dcd_re, dcd_imn = _ssm_outer("ssm_dcd", dy0_b, s_re3, s_im3, False)
    d_ssm = ssm_vjp((da_re, da_im, dbd_re, dbd_im, dcd_re, dcd_imn))

    dbg, dcg, dval, d_conv_w_full, d_conv_b = _conv_bwd(dyb, proj, conv, cw)
    dproj = jnp.concatenate([dv[None], dbg[None], dcg[None], dval[None], dga, dgb], axis=0)

    (dw_in,) = _mm("in_proj_dw", (NDEV, D // td, 1, S // ts),
                   [(u2, (ts, td), lambda b, i, j, k: (k, i)), (dproj, (None, ts, W), lambda b, i, j, k: (b, k, 0))],
                   [(0, 1, TN, 0)], [], [((NDEV, D, W), BF, (None, td, W), lambda b, i, j, k: (b, i, 0))],
                   _plain, [(td, W)])
    sib_mixer = rs_sibling_start(
        "mixer", [dw_in, dw_glu.reshape(NDEV, W // NDEV, W), dw_so, dw_co, dw_o.reshape(NDEV, Dc, D)])
    (du2,) = _mm("in_proj_dx", (1, S // tm, D // tn, NDEV),
                 [(dproj, (None, tm, W), lambda b, i, j, k: (k, i, 0)), (w_in_f, (None, tn, W), lambda b, i, j, k: (k, j, 0))],
                 [(0, 1, NT, 0)], [], [((S, D), F32, (tm, tn), lambda b, i, j, k: (i, j))], _plain, [(tm, tn)],
                 deps=[sib_mixer[4]])
    dh1, dh1_half, d_mix_norm = _rms_bwd("rms2_bwd", du2, h1, r2, row(mix_norm), dh2, 0.5)
    rs_mixer = rs_chips_start("mixer", sib_mixer, dh1)

    small = dict(mix_norm=d_mix_norm, ffn2_norm=d_ffn2_norm, final_norm=d_final_norm,
                 ssm_lambda_re=d_ssm[0], ssm_lambda_im=d_ssm[1], ssm_log_dt=d_ssm[2], ssm_b_re=d_ssm[3],
                 ssm_b_im=d_ssm[4], ssm_c_re=d_ssm[5], ssm_c_im=d_ssm[6], ssm_d=d_ssm_d, ssm_b_glu=d_b_glu,
                 conv_b=d_conv_b)
    replicated = [n for n in replicated if n != "ffn1_norm"] + ["ffn1_norm"]
    tile = SUBLANE * LANE

    def as_rows(p):
        flat = p.reshape(-1).astype(F32)
        return jnp.pad(flat, (0, -flat.shape[0] % tile)).reshape(-1, LANE)

    def pack(parts):
        return jnp.concatenate([as_rows(p) for p in parts], axis=0)

    cw_zero = jnp.zeros_like(d_conv_w_full)
    early_pk = pack([small[n] for n in replicated[:-1]] + [d_conv_w_full])
    early_land = lax.dynamic_update_slice(jnp.zeros((NDEV,) + early_pk.shape, F32), early_pk[None], (me, 0, 0))
    small_begun = _split_start("gather_small_start", [early_pk], [early_land], _everyone_copies, NDEV - 1)

    f1_dwd, f1_dwgu, f1_du = _ffn_bwd("ffn1", dh1_half, u1, ffn1_saved, wg1, wu1, wd1,
                                      deps=[rs_mixer[4], small_begun[4]])
    du1 = f1_du()
    dx, _, d_ffn1_norm = _rms_bwd("rms1_bwd", du1, xh, r1, row(ffn1_norm), dh1, 1.0)
    dwd1 = f1_dwd(deps=[d_ffn1_norm])
    late = d_ffn1_norm + 0.0 * dwd1[0, :1, :1].astype(F32)
    (late_all,) = _all_gather("gather_ffn1_norm_grad", [late.reshape(-1, LANE)])
    rs_ffn1_down = rs_chips_start("ffn1_down", rs_sibling_start("ffn1_down", [dwd1]), late_all)
    dwg1, dwu1 = f1_dwgu(deps=[rs_ffn1_down[4]])
    rs_ffn1_gate_up = rs_chips_start("ffn1_gate_up", rs_sibling_start("ffn1_gate_up", [dwg1, dwu1]), None)
    rs_end("ffn2", sharded[8:11], rs_ffn2, rs_ffn1_gate_up[4])
    rs_end("mixer", sharded[3:8], rs_mixer, [grads[n] for n in sharded[8:11]])
    _, (early_all,) = _split_wait("gather_small_wait", small_begun, _everyone_copies, [grads[n] for n in sharded[3:8]])
    small_all = jnp.concatenate([early_all, late_all], axis=1)
    g_pk, d_pk, m_pk, v_pk = _finish_replicated(
        "adamw_replicated", small_all, pack([P[n] for n in replicated[:-1]] + [cw_zero, P["ffn1_norm"]]),
        pack([M[n] for n in replicated[:-1]] + [cw_zero, M["ffn1_norm"]]),
        pack([V[n] for n in replicated[:-1]] + [cw_zero + 1.0, V["ffn1_norm"]]))

    def unpack(pk, r0, like):
        nr = -(-like.size // tile) * SUBLANE
        return pk[r0:r0 + nr].reshape(-1)[:like.size].reshape(like.shape), r0 + nr

    r0 = 0
    for n in replicated[:-1] + ["conv_w", "ffn1_norm"]:
        like = d_conv_w_full if n == "conv_w" else P[n]
        for store, pk in ((grads, g_pk), (deltas, d_pk), (new_m, m_pk), (new_v, v_pk)):
            store[n], r1 = unpack(pk, r0, like)
        r0 = r1
    g_cw_full = grads["conv_w"]
    cwl = conv_w.shape[1]
    g_cw = lax.dynamic_slice_in_dim(g_cw_full, me * cwl, cwl, axis=1)
    full3 = ((3, cwl), lambda i: (0, 0))
    grads["conv_w"], deltas["conv_w"], new_m["conv_w"], new_v["conv_w"] = _ew(
        "adamw_conv_w", (1,), [(g_cw,) + full3, (conv_w,) + full3, (m_conv_w,) + full3, (v_conv_w,) + full3],
        [((3, cwl), F32) + full3] * 4, lambda g, w, m, v: (g,) + _adamw(w, g, m, v))
    rs_end("ffn1_down", sharded[2:3], rs_ffn1_down, [g_pk, grads["conv_w"]])
    rs_end("ffn1_gate_up", sharded[0:2], rs_ffn1_gate_up, grads["ffn1_w_down"])

    return (loss, dx.reshape(x.shape), *[grads[n] for n in names], *[deltas[n] for n in names],
            *[new_m[n] for n in names], *[new_v[n] for n in names])
```

```python
import math

import jax
import jax.numpy as jnp
from jax import lax
from jax.experimental import pallas as pl
from jax.experimental.pallas import tpu as pltpu

F32 = jnp.float32
BF = jnp.bfloat16
I32 = jnp.int32
MESH = pl.DeviceIdType.MESH
LANE = 128
SUBLANE = 8
NDEV = 8
EW_BLOCK = 256 * 1024
M_TILE = 1024
K_TILE = 2048
DMA_CHUNKS = 4
EPS = 1e-6
ADAM_LR, ADAM_B1, ADAM_B2, ADAM_EPS, ADAM_WD, ADAM_STEP = 0.001, 0.9, 0.999, 1e-08, 0.01, 10
NN = ((1,), (0,))
NT = ((1,), (1,))
TN = ((0,), (0,))
HBM = pl.BlockSpec(memory_space=pltpu.HBM)


def _pick(n, pref, mult):
    t = min(pref, n)
    t -= t % mult
    while t >= mult:
        if n % t == 0:
            return t
        t -= mult
    return n


def _sigmoid(x):
    return 1.0 / (1.0 + jnp.exp(-x))


_GELU_C = math.sqrt(2.0 / math.pi)


def _gelu(x):
    return 0.5 * x * (1.0 + jnp.tanh(_GELU_C * (x + 0.044715 * x * x * x)))


def _gelu_grad(x):
    t = jnp.tanh(_GELU_C * (x + 0.044715 * x * x * x))
    return 0.5 * (1.0 + t) + 0.5 * x * (1.0 - t * t) * _GELU_C * (1.0 + 3.0 * 0.044715 * x * x)


def _dep_specs(deps, rank):
    return [(d, d.shape, lambda *_, nd=d.ndim: (0,) * nd) for d in deps]


def _mm(name, grid, ops, pairs, sides, outs, epilogue, acc_shapes, deps=()):
    nk = grid[-1]
    n_ops, n_sides, n_outs = len(ops), len(sides), len(outs)
    dep_specs = _dep_specs(deps, len(grid))
    n_deps = len(dep_specs)

    def body(*refs):
        op_refs = refs[:n_ops]
        side_refs = refs[n_ops:n_ops + n_sides]
        out_refs = refs[n_ops + n_sides + n_deps:n_ops + n_sides + n_deps + n_outs]
        acc_refs = refs[n_ops + n_sides + n_deps + n_outs:]

        def partials():
            res = [None] * len(acc_shapes)
            for ia, ib, dims, ai in pairs:
                p = lax.dot_general(op_refs[ia][...], op_refs[ib][...], (dims, ((), ())),
                                    preferred_element_type=F32)
                res[ai] = p if res[ai] is None else res[ai] + p
            return res

        def finish(accs):
            vals = epilogue(accs, [s[...] for s in side_refs])
            for o, v in zip(out_refs, vals):
                o[...] = v.astype(o.dtype)

        if nk == 1:
            finish(partials())
        else:
            k = pl.program_id(len(grid) - 1)

            @pl.when(k == 0)
            def _():
                for a, p in zip(acc_refs, partials()):
                    a[...] = p

            @pl.when(k > 0)
            def _():
                for a, p in zip(acc_refs, partials()):
                    a[...] += p

            @pl.when(k == nk - 1)
            def _():
                finish([a[...] for a in acc_refs])

    return pl.pallas_call(
        body, name=name, grid=grid,
        in_specs=[pl.BlockSpec(b, m) for (_, b, m) in list(ops) + list(sides) + dep_specs],
        out_specs=[pl.BlockSpec(b, m) for (_, _, b, m) in outs],
        out_shape=[jax.ShapeDtypeStruct(s, d) for (s, d, _, _) in outs],
        scratch_shapes=[pltpu.VMEM(s, F32) for s in acc_shapes] if nk > 1 else [],
        compiler_params=pltpu.CompilerParams(
            dimension_semantics=("parallel",) * (len(grid) - 1) + ("arbitrary",)),
    )(*[a for (a, _, _) in list(ops) + list(sides) + dep_specs])


def _ew(name, grid, ins, outs, fn, acc=(), deps=()):
    n_in = len(ins)
    dep_specs = _dep_specs(deps, len(grid))

    def body(*refs):
        vals = fn(*[r[...] for r in refs[:n_in]])
        first = pl.program_id(0) == 0
        for idx, (o, v) in enumerate(zip(refs[n_in + len(dep_specs):], vals)):
            if idx in acc:
                @pl.when(first)
                def _(o=o, v=v):
                    o[...] = v.astype(o.dtype)

                @pl.when(jnp.logical_not(first))
                def _(o=o, v=v):
                    o[...] += v.astype(o.dtype)
            else:
                o[...] = v.astype(o.dtype)

    return pl.pallas_call(
        body, name=name, grid=grid,
        in_specs=[pl.BlockSpec(b, m) for (_, b, m) in list(ins) + dep_specs],
        out_specs=[pl.BlockSpec(b, m) for (_, _, b, m) in outs],
        out_shape=[jax.ShapeDtypeStruct(s, d) for (s, d, _, _) in outs],
        compiler_params=pltpu.CompilerParams(
            dimension_semantics=(("arbitrary",) if acc else ("parallel",)) * len(grid)),
    )(*[a for (a, _, _) in list(ins) + dep_specs])


def _position():
    x, y, c = lax.axis_index("x"), lax.axis_index("y"), lax.axis_index("c")
    chips = [(1 - x, y), (x, 1 - y), (1 - x, 1 - y)]
    return x, y, c, chips


def _all_gather(name, shards):
    n = len(shards)

    def body(*refs):
        xs, outs = refs[:n], refs[n:2 * n]
        send_sems, recv_sems, local_sems = refs[2 * n:]
        x, y, c, chips = _position()
        me, sibling = (x, y, c), (x, y, 1 - c)

        def copy(a, k, block, to, src=None):
            dst = outs[a].at[4 * block[0] + 2 * block[1] + block[2]]
            return pltpu.make_async_remote_copy(
                src_ref=dst if src is None else src, dst_ref=dst,
                send_sem=send_sems.at[a, k], recv_sem=recv_sems.at[a, k],
                device_id=to, device_id_type=MESH)

        mine = [pltpu.make_async_copy(xs[a], outs[a].at[4 * x + 2 * y + c], local_sems.at[a]) for a in range(n)]
        for cp in mine:
            cp.start()
        first = []
        for a in range(n):
            first.append(copy(a, 0, me, sibling, src=xs[a]))
            first += [copy(a, 1 + j, me, (*chip, c), src=xs[a]) for j, chip in enumerate(chips)]
        for cp in first:
            cp.start()
        passed = []
        for a in range(n):
            for j, chip in enumerate(chips):
                copy(a, 1 + j, (*chip, c), me).wait_recv()
                cp = copy(a, 4 + j, (*chip, c), sibling)
                cp.start()
                passed.append(cp)
        for a in range(n):
            copy(a, 0, sibling, me).wait_recv()
            for j, chip in enumerate(chips):
                copy(a, 4 + j, (*chip, 1 - c), me).wait_recv()
        for cp in first + passed:
            cp.wait_send()
        for cp in mine:
            cp.wait()

    return pl.pallas_call(
        body, name=name,
        out_shape=[jax.ShapeDtypeStruct((NDEV,) + s.shape, s.dtype) for s in shards],
        in_specs=[HBM] * n, out_specs=[HBM] * n,
        scratch_shapes=[pltpu.SemaphoreType.DMA((n, 7)), pltpu.SemaphoreType.DMA((n, 7)),
                        pltpu.SemaphoreType.DMA((n,))],
    )(*shards)


SEM = pl.BlockSpec(memory_space=pltpu.SEMAPHORE)
EFFECT = pltpu.SideEffectType.DATAFLOW_SIDE_EFFECTING


def _in_hbm(v):
    return pltpu.with_memory_space_constraint(v, pltpu.HBM)


def _split_start(name, srcs, lands, make_copies, n_per, after=()):
    n = len(srcs)
    after = list(after)

    def body(*refs):
        send_sems, recv_sems = refs[2 * n + len(after)], refs[2 * n + len(after) + 1]
        for cp in make_copies(refs[:n], refs[n:2 * n], send_sems, recv_sems):
            cp.start()
        refs[-1][...] = jnp.zeros_like(refs[-1])

    outs = pl.pallas_call(
        body, name=name,
        out_shape=(pltpu.SemaphoreType.DMA((n * n_per,)), pltpu.SemaphoreType.DMA((n * n_per,)),
                   *[pltpu.HBM(v.shape, v.dtype) for v in list(srcs) + list(lands)],
                   jax.ShapeDtypeStruct((SUBLANE, LANE), F32)),
        in_specs=[HBM] * (2 * n) + [pl.BlockSpec(memory_space=pl.ANY)] * len(after),
        out_specs=(SEM, SEM, *[HBM] * (2 * n), pl.BlockSpec(memory_space=pltpu.VMEM)),
        input_output_aliases={i: 2 + i for i in range(2 * n)},
        compiler_params=pltpu.CompilerParams(has_side_effects=EFFECT),
    )(*[_in_hbm(v) for v in list(srcs) + list(lands)], *after)
    return outs[0], outs[1], list(outs[2:2 + n]), list(outs[2 + n:2 + 2 * n]), outs[-1]


def _split_wait(name, started, make_copies, after):
    send_sems, recv_sems, srcs, lands, _ = started
    n = len(srcs)

    def body(*refs):
        for cp in make_copies(refs[:n], refs[n:2 * n], refs[2 * n], refs[2 * n + 1]):
            cp.wait_send()
            cp.wait_recv()

    order = [] if after is None else list(after) if isinstance(after, (list, tuple)) else [after]
    outs = pl.pallas_call(
        body, name=name,
        out_shape=tuple(pltpu.HBM(v.shape, v.dtype) for v in srcs + lands),
        in_specs=[HBM] * (2 * n) + [SEM, SEM] + [pl.BlockSpec(memory_space=pl.ANY)] * len(order),
        out_specs=tuple([HBM] * (2 * n)),
        input_output_aliases={i: i for i in range(2 * n)},
        compiler_params=pltpu.CompilerParams(has_side_effects=EFFECT),
    )(*srcs, *lands, send_sems, recv_sems, *order)
    return list(outs[:n]), list(outs[n:])


def _gather_copies(xs, lands, send_sems, recv_sems):
    x, y, c, chips = _position()
    copies = []
    for a in range(len(xs)):
        for k, peer in enumerate([(x, y, 1 - c)] + [(*chip, c) for chip in chips]):
            copies.append(pltpu.make_async_remote_copy(
                src_ref=xs[a], dst_ref=lands[a].at[4 * x + 2 * y + c],
                send_sem=send_sems.at[4 * a + k], recv_sem=recv_sems.at[4 * a + k], device_id=peer, device_id_type=MESH))
    return copies


def _chips_copies(ps, lands, send_sems, recv_sems):
    x, y, c, chips = _position()
    copies = []
    for a in range(len(ps)):
        for j, chip in enumerate(chips):
            copies.append(pltpu.make_async_remote_copy(
                src_ref=ps[a].at[2 * chip[0] + chip[1]], dst_ref=lands[a].at[j],
                send_sem=send_sems.at[3 * a + j], recv_sem=recv_sems.at[3 * a + j], device_id=(*chip, c),
                device_id_type=MESH))
    return copies


def _sibling_copies(gs, lands, send_sems, recv_sems):
    x, y, c, _ = _position()
    copies = []
    for a in range(len(gs)):
        for q in range(4):
            copies.append(pltpu.make_async_remote_copy(
                src_ref=gs[a].at[2 * q + 1 - c], dst_ref=lands[a].at[q],
                send_sem=send_sems.at[4 * a + q], recv_sem=recv_sems.at[4 * a + q],
                device_id=(x, y, 1 - c), device_id_type=MESH))
    return copies


def _everyone_copies(xs, lands, send_sems, recv_sems):
    x, y, c, _ = _position()
    flip = lambda v, bit: 1 - v if bit else v
    copies = []
    for a in range(len(xs)):
        for k in range(1, NDEV):
            copies.append(pltpu.make_async_remote_copy(
                src_ref=xs[a], dst_ref=lands[a].at[4 * x + 2 * y + c],
                send_sem=send_sems.at[7 * a + k - 1], recv_sem=recv_sems.at[7 * a + k - 1],
                device_id=(flip(x, k & 4), flip(y, k & 2), flip(c, k & 1)), device_id_type=MESH))
    return copies


def _row_chunks(rows, dtype):
    unit = SUBLANE * (4 // jnp.dtype(dtype).itemsize)
    units = rows // unit
    if rows % unit or units < 2:
        return [(0, rows)]
    k = min(DMA_CHUNKS, units)
    sizes = [(units // k + (1 if i < units % k else 0)) * unit for i in range(k)]
    return [(sum(sizes[:i]), sz) for i, sz in enumerate(sizes)]


def _gather_forward(name, lands):
    n = len(lands)

    def body(*refs):
        ins, outs = refs[:n], refs[n:2 * n]
        send_sems, recv_sems = refs[2 * n:]
        x, y, c, chips = _position()
        whole, chunks = [], []
        for a in range(n):
            rows = _row_chunks(ins[a].shape[1], ins[a].dtype)
            for j, chip in enumerate(chips):
                slot = 4 * chip[0] + 2 * chip[1]

                def to_sibling(src, dst):
                    return pltpu.make_async_remote_copy(
                        src_ref=src, dst_ref=dst, send_sem=send_sems.at[a, j], recv_sem=recv_sems.at[a, j],
                        device_id=(x, y, 1 - c), device_id_type=MESH)

                whole.append(to_sibling(ins[a].at[slot + c], outs[a].at[slot + 1 - c]))
                chunks += [to_sibling(ins[a].at[slot + c, pl.ds(r0, nr)], outs[a].at[slot + c, pl.ds(r0, nr)])
                           for r0, nr in rows]
        for cp in chunks:
            cp.start()
        for cp in whole:
            cp.wait()

    return pl.pallas_call(
        body, name=name,
        out_shape=[jax.ShapeDtypeStruct(l.shape, l.dtype) for l in lands],
        in_specs=[HBM] * n, out_specs=[HBM] * n,
        input_output_aliases={a: a for a in range(n)},
        scratch_shapes=[pltpu.SemaphoreType.DMA((n, 3)), pltpu.SemaphoreType.DMA((n, 3))],
    )(*lands)


def _sum_sibling(name, g, land, c_arr):
    _, R, C = g.shape
    tr = _pick(R, 512, SUBLANE)

    def body(c_ref, g_ref, l_ref, o_ref):
        o_ref[...] = (g_ref[...].astype(F32) + l_ref[...].astype(F32)).astype(o_ref.dtype)

    return pl.pallas_call(
        body, name=name,
        grid_spec=pltpu.PrefetchScalarGridSpec(
            num_scalar_prefetch=1, grid=(4, R // tr),
            in_specs=[pl.BlockSpec((None, tr, C), lambda q, i, cr: (2 * q + cr[0], i, 0)),
                      pl.BlockSpec((None, tr, C), lambda q, i, cr: (q, i, 0))],
            out_specs=pl.BlockSpec((None, tr, C), lambda q, i, cr: (q, i, 0))),
        out_shape=jax.ShapeDtypeStruct((4, R, C), g.dtype),
        compiler_params=pltpu.CompilerParams(dimension_semantics=("parallel", "parallel")),
    )(c_arr, g, land)


def _adamw(w, g, m, v):
    m = ADAM_B1 * m + (1.0 - ADAM_B1) * g
    v = ADAM_B2 * v + (1.0 - ADAM_B2) * (g * g)
    m_hat = m / (1.0 - ADAM_B1 ** ADAM_STEP)
    v_hat = v / (1.0 - ADAM_B2 ** ADAM_STEP)
    delta = -ADAM_LR * (m_hat / (jnp.sqrt(v_hat) + ADAM_EPS) + ADAM_WD * w)
    return delta, m, v


def _finish_sharded(name, sums, land, q_arr, w, m, v):
    R, C = w.shape
    tr = _pick(R, 512, SUBLANE)
    tc = _pick(C, max(LANE, EW_BLOCK // tr), LANE)

    def body(q_ref, p_ref, l_ref, w_ref, m_ref, v_ref, g_out, d_out, m_out, v_out):
        g = p_ref[...].astype(F32)
        for j in range(3):
            g = g + l_ref[j].astype(F32)
        d, mn, vn = _adamw(w_ref[...], g, m_ref[...], v_ref[...])
        g_out[...] = g
        d_out[...] = d
        m_out[...] = mn
        v_out[...] = vn

    blk = pl.BlockSpec((tr, tc), lambda i, j, qr: (i, j))
    return pl.pallas_call(
        body, name=name,
        grid_spec=pltpu.PrefetchScalarGridSpec(
            num_scalar_prefetch=1, grid=(R // tr, C // tc),
            in_specs=[pl.BlockSpec((None, tr, tc), lambda i, j, qr: (qr[0], i, j)),
                      pl.BlockSpec((3, tr, tc), lambda i, j, qr: (0, i, j)), blk, blk, blk],
            out_specs=[blk] * 4),
        out_shape=[jax.ShapeDtypeStruct((R, C), F32)] * 4,
        compiler_params=pltpu.CompilerParams(dimension_semantics=("parallel", "parallel")),
    )(q_arr, sums, land, w, m, v)


def _finish_replicated(name, gathered, w, m, v):
    _, R, C = gathered.shape
    tr = _pick(R, 256, SUBLANE)

    def fn(gv, wv, mv, vv):
        g = gv[0]
        for d in range(1, NDEV):
            g = g + gv[d]
        dl, mn, vn = _adamw(wv, g, mv, vv)
        return g, dl, mn, vn

    row = ((tr, C), lambda i: (i, 0))
    return _ew(name, (R // tr,),
               [(gathered, (NDEV, tr, C), lambda i: (0, i, 0)), (w,) + row, (m,) + row, (v,) + row],
               [((R, C), F32) + row] * 4, fn)


def _rms_fwd(name, h, g, deps=()):
    S, D = h.shape
    tr = _pick(S, 256, SUBLANE)

    def fn(hv, gv):
        r = lax.rsqrt(jnp.mean(hv * hv, axis=-1, keepdims=True) + EPS)
        return hv * r * gv, r

    return _ew(name, (S // tr,),
               [(h, (tr, D), lambda i: (i, 0)), (g, (1, D), lambda i: (0, 0))],
               [((S, D), BF, (tr, D), lambda i: (i, 0)), ((S, 1), F32, (tr, 1), lambda i: (i, 0))], fn, deps=deps)


def _rms_bwd(name, du, h, r, g, dres, scale):
    S, D = h.shape
    tr = _pick(S, 256, SUBLANE)

    def fn(duv, hv, rv, gv, drv):
        xn = hv * rv
        dxn = duv * gv
        dh = drv + rv * (dxn - xn * jnp.mean(dxn * xn, axis=-1, keepdims=True))
        return dh, scale * dh, jnp.sum(duv * xn, axis=0, keepdims=True)

    row = ((tr, D), lambda i: (i, 0))
    return _ew(name, (S // tr,),
               [(du,) + row, (h,) + row, (r, (tr, 1), lambda i: (i, 0)), (g, (1, D), lambda i: (0, 0)), (dres,) + row],
               [((S, D), F32) + row, ((S, D), BF) + row, ((1, D), F32, (1, D), lambda i: (0, 0))], fn, acc=(2,))


def _loss_head(name, h, g, target):
    S, D = h.shape
    tr = _pick(S, 256, SUBLANE)

    def fn(hv, gv, tv):
        r = lax.rsqrt(jnp.mean(hv * hv, axis=-1, keepdims=True) + EPS)
        xn = hv * r
        diff = xn * gv - tv
        loss = 0.5 * jnp.sum(jnp.mean(diff * diff, axis=-1, keepdims=True))
        dout = diff / D
        dxn = dout * gv
        dh = r * (dxn - xn * jnp.mean(dxn * xn, axis=-1, keepdims=True))
        return (jnp.zeros((1, LANE), F32) + loss, dh, 0.5 * dh, jnp.sum(dout * xn, axis=0, keepdims=True))

    row = ((tr, D), lambda i: (i, 0))
    return _ew(name, (S // tr,),
               [(h,) + row, (g, (1, D), lambda i: (0, 0)), (target,) + row],
               [((1, LANE), F32, (1, LANE), lambda i: (0, 0)), ((S, D), F32) + row, ((S, D), BF) + row,
                ((1, D), F32, (1, D), lambda i: (0, 0))], fn, acc=(0, 3))


def _ffn_fwd(tag, u, h, wg, wu, wd, deps=()):
    S, D = u.shape
    Fs = wg.shape[1]
    tm, tk = _pick(S, M_TILE, SUBLANE), _pick(D, K_TILE, LANE)

    def up_epilogue(accs, sides):
        gt, up = accs
        return gt, up, gt * _sigmoid(gt) * up

    act = ((NDEV, S, Fs), BF, (None, tm, Fs), lambda b, i, j, k: (b, i, 0))
    gt, up, a = _mm(
        tag + "_up", (NDEV, S // tm, 1, D // tk),
        [(u, (tm, tk), lambda b, i, j, k: (i, k)),
         (wg, (None, Fs, tk), lambda b, i, j, k: (b, 0, k)), (wu, (None, Fs, tk), lambda b, i, j, k: (b, 0, k))],
        [(0, 1, NT, 0), (0, 2, NT, 1)], [], [act, act, act], up_epilogue, [(tm, Fs), (tm, Fs)], deps=deps)
    if callable(wd):
        wd = wd(a)
    tn = _pick(D, 1024, LANE)
    (hn,) = _mm(
        tag + "_down", (1, S // tm, D // tn, NDEV),
        [(a, (None, tm, Fs), lambda b, i, j, k: (k, i, 0)), (wd, (None, Fs, tn), lambda b, i, j, k: (k, 0, j))],
        [(0, 1, NN, 0)], [(h, (tm, tn), lambda b, i, j, k: (i, j))],
        [((S, D), F32, (tm, tn), lambda b, i, j, k: (i, j))],
        lambda accs, sides: [sides[0] + 0.5 * accs[0]], [(tm, tn)])
    return hn, (gt, up, a), wd


def _ffn_bwd(tag, dhs, u, saved, wg, wu, wd, deps=()):
    gt, up, a = saved
    S, D = u.shape
    Fs = wg.shape[1]
    tm, tk = _pick(S, M_TILE, SUBLANE), _pick(D, K_TILE, LANE)
    act_in = lambda arr: (arr, (None, tm, Fs), lambda b, i, j, k: (b, i, 0))
    act_out = ((NDEV, S, Fs), BF, (None, tm, Fs), lambda b, i, j, k: (b, i, 0))

    def act_epilogue(accs, sides):
        da = accs[0]
        gtv, upv = sides[0].astype(F32), sides[1].astype(F32)
        sg = _sigmoid(gtv)
        return da * upv * sg * (1.0 + gtv * (1.0 - sg)), da * gtv * sg

    dgt, dup = _mm(
        tag + "_dact", (NDEV, S // tm, 1, D // tk),
        [(dhs, (tm, tk), lambda b, i, j, k: (i, k)), (wd, (None, Fs, tk), lambda b, i, j, k: (b, 0, k))],
        [(0, 1, NT, 0)], [act_in(gt), act_in(up)], [act_out, act_out], act_epilogue, [(tm, Fs)], deps=deps)

    ts = _pick(S, K_TILE, SUBLANE)
    tn = _pick(D, 1024, LANE)
    wgrad = ((NDEV, Fs, D), BF, (None, Fs, tn), lambda b, i, j, k: (b, 0, j))
    tok = lambda arr: (arr, (None, ts, Fs), lambda b, i, j, k: (b, k, 0))

    def grad_down(deps=()):
        return _mm(
            tag + "_dwd", (NDEV, 1, D // tn, S // ts),
            [tok(a), (dhs, (ts, tn), lambda b, i, j, k: (k, j))],
            [(0, 1, TN, 0)], [], [wgrad], lambda accs, sides: accs, [(Fs, tn)], deps=deps)[0]

    def grad_gate_up(deps=()):
        return _mm(
            tag + "_dwgu", (NDEV, 1, D // tn, S // ts),
            [tok(dgt), tok(dup), (u, (ts, tn), lambda b, i, j, k: (k, j))],
            [(0, 2, TN, 0), (1, 2, TN, 1)], [], [wgrad, wgrad], lambda accs, sides: accs, [(Fs, tn), (Fs, tn)],
            deps=deps)

    def du(deps=()):
        return _mm(
            tag + "_du", (1, S // tm, D // tn, NDEV),
            [(dgt, (None, tm, Fs), lambda b, i, j, k: (k, i, 0)), (dup, (None, tm, Fs), lambda b, i, j, k: (k, i, 0)),
             (wg, (None, Fs, tn), lambda b, i, j, k: (k, 0, j)), (wu, (None, Fs, tn), lambda b, i, j, k: (k, 0, j))],
            [(0, 2, NN, 0), (1, 3, NN, 0)], [], [((S, D), F32, (tm, tn), lambda b, i, j, k: (i, j))],
            lambda accs, sides: accs, [(tm, tn)], deps=deps)[0]

    return grad_down, grad_gate_up, du


def _ssm_params(lam_re, lam_im, log_dt, b_re, b_im, c_re, c_im):
    G, N = lam_re.shape
    C = b_re.shape[2]
    lam_re = jnp.minimum(lam_re, -1e-4)
    dt = jnp.exp(log_dt)[:, None]
    mag = jnp.exp(lam_re * dt)
    a_re = mag * jnp.cos(lam_im * dt)
    a_im = mag * jnp.sin(lam_im * dt)
    den = lam_re * lam_re + lam_im * lam_im
    p = a_re - 1.0
    f_re = ((p * lam_re + a_im * lam_im) / den)[:, :, None]
    f_im = ((a_im * lam_re - p * lam_im) / den)[:, :, None]
    bb_re = f_re * b_re - f_im * b_im
    bb_im = f_re * b_im + f_im * b_re
    gpt = LANE // C
    tiles = G // gpt
    eye = jnp.eye(gpt, dtype=F32)

    def bd(bb):
        return jnp.einsum("bgnc,gh->bgchn", bb.reshape(tiles, gpt, N, C), eye).reshape(tiles, gpt * C, gpt * N)

    def cd(cc):
        return jnp.einsum("bgcn,gh->bgnhc", cc.reshape(tiles, gpt, C, N), eye).reshape(tiles, gpt * N, gpt * C)

    rows = G * N // LANE
    return (a_re.reshape(rows, LANE), a_im.reshape(rows, LANE), bd(bb_re), bd(bb_im), cd(c_re), cd(-c_im))


def _tile_states(ref3, b, per):
    return jnp.concatenate([ref3[:, per * b + r, :] for r in range(per)], axis=1).astype(BF)


def _ssm_spread(name, x, m_re, m_im, dims, rows):
    S, W = x.shape
    tiles = m_re.shape[0]
    tch, per = W // tiles, rows // tiles
    tq = _pick(S, 256, SUBLANE)

    def body(x_ref, mre_ref, mim_ref, ore_ref, oim_ref):
        for b in range(tiles):
            xb = x_ref[:, b * tch:(b + 1) * tch]
            for m_ref, o_ref in ((mre_ref, ore_ref), (mim_ref, oim_ref)):
                val = lax.dot_general(xb, m_ref[b], (dims, ((), ())), preferred_element_type=F32)
                for r in range(per):
                    o_ref[:, per * b + r, :] = val[:, r * LANE:(r + 1) * LANE]

    whole = lambda m: pl.BlockSpec(m.shape, lambda i: (0, 0, 0))
    st = pl.BlockSpec((tq, rows, LANE), lambda i: (i, 0, 0))
    return pl.pallas_call(
        body, name=name, grid=(S // tq,),
        in_specs=[pl.BlockSpec((tq, W), lambda i: (i, 0)), whole(m_re), whole(m_im)], out_specs=[st, st],
        out_shape=[jax.ShapeDtypeStruct((S, rows, LANE), F32)] * 2,
        compiler_params=pltpu.CompilerParams(dimension_semantics=("parallel",)),
    )(x, m_re, m_im)


def _ssm_collect(name, z_re3, z_im3, m_re, m_im, dims, side, gain, epilogue, out_dtypes):
    S, rows, _ = z_re3.shape
    tiles = m_re.shape[0]
    W = side.shape[1]
    tch, per = W // tiles, rows // tiles
    tq = _pick(S, 256, SUBLANE)
    n_out = len(out_dtypes)

    def body(zre_ref, zim_ref, mre_ref, mim_ref, side_ref, gain_ref, *out_refs):
        for b in range(tiles):
            cols = slice(b * tch, (b + 1) * tch)
            zre, zim = _tile_states(zre_ref, b, per), _tile_states(zim_ref, b, per)
            acc = lax.dot_general(zre, mre_ref[b], (dims, ((), ())), preferred_element_type=F32)
            acc = acc + lax.dot_general(zim, mim_ref[b], (dims, ((), ())), preferred_element_type=F32)
            for o, v in zip(out_refs[:n_out], epilogue(acc, side_ref[:, cols], gain_ref[:, cols])):
                o[:, cols] = v.astype(o.dtype)
            out_refs[n_out][:, b * per * LANE:(b + 1) * per * LANE] = zre
            out_refs[n_out + 1][:, b * per * LANE:(b + 1) * per * LANE] = zim

    whole = lambda m: pl.BlockSpec(m.shape, lambda i: (0, 0, 0))
    st = pl.BlockSpec((tq, rows, LANE), lambda i: (i, 0, 0))
    ch = pl.BlockSpec((tq, W), lambda i: (i, 0))
    flat = pl.BlockSpec((tq, rows * LANE), lambda i: (i, 0))
    return pl.pallas_call(
        body, name=name, grid=(S // tq,),
        in_specs=[st, st, whole(m_re), whole(m_im), ch, pl.BlockSpec((1, W), lambda i: (0, 0))],
        out_specs=[ch] * n_out + [flat, flat],
        out_shape=[jax.ShapeDtypeStruct((S, W), dt) for dt in out_dtypes]
        + [jax.ShapeDtypeStruct((S, rows * LANE), BF)] * 2,
        compiler_params=pltpu.CompilerParams(dimension_semantics=("parallel",)),
    )(z_re3, z_im3, m_re, m_im, side, gain)


def _scan_fwd(bu_re, bu_im, a_re, a_im):
    S, R, _ = bu_re.shape
    tc = _pick(S, 256, SUBLANE)

    def body(bre, bim, are, aim, sre, sim, carry):
        @pl.when(pl.program_id(0) == 0)
        def _():
            carry[...] = jnp.zeros_like(carry)

        ar, ai = are[...], aim[...]

        def step(t, c):
            pr, pi = c
            nr = ar * pr - ai * pi + bre[t]
            ni = ar * pi + ai * pr + bim[t]
            sre[t] = nr
            sim[t] = ni
            return nr, ni

        pr, pi = lax.fori_loop(0, tc, step, (carry[0], carry[1]), unroll=8)
        carry[0] = pr
        carry[1] = pi

    blk = pl.BlockSpec((tc, R, LANE), lambda i: (i, 0, 0))
    par = pl.BlockSpec((R, LANE), lambda i: (0, 0))
    return pl.pallas_call(
        body, name="ssm_scan_fwd", grid=(S // tc,),
        in_specs=[blk, blk, par, par], out_specs=[blk, blk],
        out_shape=[jax.ShapeDtypeStruct((S, R, LANE), F32)] * 2,
        scratch_shapes=[pltpu.VMEM((2, R, LANE), F32)],
        compiler_params=pltpu.CompilerParams(dimension_semantics=("arbitrary",)),
    )(bu_re, bu_im, a_re, a_im)


def _scan_bwd(ds_re, ds_im, s_re, s_im, a_re, a_im):
    S, R, _ = ds_re.shape
    tc = _pick(S, 256, SUBLANE)
    nc = S // tc

    def body(dre, dim_, sre, sim, are, aim, lre, lim, dar, dai, carry):
        @pl.when(pl.program_id(0) == 0)
        def _():
            carry[...] = jnp.zeros_like(carry)
            dar[...] = jnp.zeros_like(dar)
            dai[...] = jnp.zeros_like(dai)

        ar, ai = are[...], aim[...]

        def step(tt, c):
            t = tc - 1 - tt
            lr, li, gr, gi = c
            sr, si = sre[t], sim[t]
            gr = gr + lr * sr + li * si
            gi = gi + li * sr - lr * si
            nlr = dre[t] + ar * lr + ai * li
            nli = dim_[t] + ar * li - ai * lr
            lre[t] = nlr
            lim[t] = nli
            return nlr, nli, gr, gi

        lr, li, gr, gi = lax.fori_loop(0, tc, step, (carry[0], carry[1], dar[...], dai[...]), unroll=8)
        carry[0] = lr
        carry[1] = li
        dar[...] = gr
        dai[...] = gi

    blk = pl.BlockSpec((tc, R, LANE), lambda i: (nc - 1 - i, 0, 0))
    par = pl.BlockSpec((R, LANE), lambda i: (0, 0))
    return pl.pallas_call(
        body, name="ssm_scan_bwd", grid=(nc,),
        in_specs=[blk, blk, blk, blk, par, par], out_specs=[blk, blk, par, par],
        out_shape=[jax.ShapeDtypeStruct((S, R, LANE), F32)] * 2 + [jax.ShapeDtypeStruct((R, LANE), F32)] * 2,
        scratch_shapes=[pltpu.VMEM((2, R, LANE), F32)],
        compiler_params=pltpu.CompilerParams(dimension_semantics=("arbitrary",)),
    )(ds_re, ds_im, s_re, s_im, a_re, a_im)


def _shift_down(z, k):
    t = lax.broadcasted_iota(I32, z.shape, 0)
    return jnp.where(t >= k, pltpu.roll(z, k, 0), 0.0)


def _shift_up(z, k):
    n = z.shape[0]
    t = lax.broadcasted_iota(I32, z.shape, 0)
    return jnp.where(t < n - k, pltpu.roll(z, n - k, 0), 0.0)


def _conv_fwd(proj, cw, cb):
    _, S, W = proj.shape
    ct = _pick(W, 256, LANE)

    def fn(bg, cg, val, w, b):
        z = cg * val
        conv = b + w[0:1] * _shift_down(z, 2) + w[1:2] * _shift_down(z, 1) + w[2:3] * z
        return bg * conv, conv

    sl = lambda s: (proj, (None, S, ct), lambda j, s=s: (s, 0, j))
    col = ((S, ct), lambda j: (0, j))
    return _ew("conv_fwd", (W // ct,),
               [sl(1), sl(2), sl(3), (cw, (3, ct), lambda j: (0, j)), (cb, (1, ct), lambda j: (0, j))],
               [((S, W), BF) + col, ((S, W), F32) + col], fn)


def _conv_bwd(dyb, proj, conv, cw):
    _, S, W = proj.shape
    ct = _pick(W, 256, LANE)

    def fn(dy, bg, cg, val, cv, w):
        z = cg * val
        z1, z2 = _shift_down(z, 1), _shift_down(z, 2)
        dconv = dy * bg
        dz = w[2:3] * dconv + w[1:2] * _shift_up(dconv, 1) + w[0:1] * _shift_up(dconv, 2)
        dw = jnp.concatenate([jnp.sum(dconv * z2, axis=0, keepdims=True), jnp.sum(dconv * z1, axis=0, keepdims=True),
                              jnp.sum(dconv * z, axis=0, keepdims=True)], axis=0)
        return dy * cv, dz * val, dz * cg, dw, jnp.sum(dconv, axis=0, keepdims=True)

    sl = lambda s: (proj, (None, S, ct), lambda j, s=s: (s, 0, j))
    col = ((S, ct), lambda j: (0, j))
    return _ew("conv_bwd", (W // ct,),
               [(dyb,) + col, sl(1), sl(2), sl(3), (conv,) + col, (cw, (3, ct), lambda j: (0, j))],
               [((S, W), BF) + col, ((S, W), BF) + col, ((S, W), BF) + col,
                ((3, W), F32, (3, ct), lambda j: (0, j)), ((1, W), F32, (1, ct), lambda j: (0, j))], fn)


def _plain(accs, sides):
    return accs


def kernel(x, ffn1_norm, ffn1_w_gate, ffn1_w_up, ffn1_w_down, mix_norm, w_in, ssm_lambda_re, ssm_lambda_im, ssm_log_dt, ssm_b_re, ssm_b_im, ssm_c_re, ssm_c_im, ssm_d, ssm_w_glu, ssm_b_glu, ssm_w_out, conv_w, conv_b, conv_w_out, w_o, ffn2_norm, ffn2_w_gate, ffn2_w_up, ffn2_w_down, final_norm, loss_target, m_ffn1_norm, m_ffn1_w_gate, m_ffn1_w_up, m_ffn1_w_down, m_mix_norm, m_w_in, m_ssm_lambda_re, m_ssm_lambda_im, m_ssm_log_dt, m_ssm_b_re, m_ssm_b_im, m_ssm_c_re, m_ssm_c_im, m_ssm_d, m_ssm_w_glu, m_ssm_b_glu, m_ssm_w_out, m_conv_w, m_conv_b, m_conv_w_out, m_w_o, m_ffn2_norm, m_ffn2_w_gate, m_ffn2_w_up, m_ffn2_w_down, m_final_norm, v_ffn1_norm, v_ffn1_w_gate, v_ffn1_w_up, v_ffn1_w_down, v_mix_norm, v_w_in, v_ssm_lambda_re, v_ssm_lambda_im, v_ssm_log_dt, v_ssm_b_re, v_ssm_b_im, v_ssm_c_re, v_ssm_c_im, v_ssm_d, v_ssm_w_glu, v_ssm_b_glu, v_ssm_w_out, v_conv_w, v_conv_b, v_conv_w_out, v_w_o, v_ffn2_norm, v_ffn2_w_gate, v_ffn2_w_up, v_ffn2_w_down, v_final_norm):
    P = dict(ffn1_norm=ffn1_norm, ffn1_w_gate=ffn1_w_gate, ffn1_w_up=ffn1_w_up, ffn1_w_down=ffn1_w_down, mix_norm=mix_norm, w_in=w_in, ssm_lambda_re=ssm_lambda_re, ssm_lambda_im=ssm_lambda_im, ssm_log_dt=ssm_log_dt, ssm_b_re=ssm_b_re, ssm_b_im=ssm_b_im, ssm_c_re=ssm_c_re, ssm_c_im=ssm_c_im, ssm_d=ssm_d, ssm_w_glu=ssm_w_glu, ssm_b_glu=ssm_b_glu, ssm_w_out=ssm_w_out, conv_w=conv_w, conv_b=conv_b, conv_w_out=conv_w_out, w_o=w_o, ffn2_norm=ffn2_norm, ffn2_w_gate=ffn2_w_gate, ffn2_w_up=ffn2_w_up, ffn2_w_down=ffn2_w_down, final_norm=final_norm)
    M = dict(ffn1_norm=m_ffn1_norm, ffn1_w_gate=m_ffn1_w_gate, ffn1_w_up=m_ffn1_w_up, ffn1_w_down=m_ffn1_w_down, mix_norm=m_mix_norm, w_in=m_w_in, ssm_lambda_re=m_ssm_lambda_re, ssm_lambda_im=m_ssm_lambda_im, ssm_log_dt=m_ssm_log_dt, ssm_b_re=m_ssm_b_re, ssm_b_im=m_ssm_b_im, ssm_c_re=m_ssm_c_re, ssm_c_im=m_ssm_c_im, ssm_d=m_ssm_d, ssm_w_glu=m_ssm_w_glu, ssm_b_glu=m_ssm_b_glu, ssm_w_out=m_ssm_w_out, conv_w=m_conv_w, conv_b=m_conv_b, conv_w_out=m_conv_w_out, w_o=m_w_o, ffn2_norm=m_ffn2_norm, ffn2_w_gate=m_ffn2_w_gate, ffn2_w_up=m_ffn2_w_up, ffn2_w_down=m_ffn2_w_down, final_norm=m_final_norm)
    V = dict(ffn1_norm=v_ffn1_norm, ffn1_w_gate=v_ffn1_w_gate, ffn1_w_up=v_ffn1_w_up, ffn1_w_down=v_ffn1_w_down, mix_norm=v_mix_norm, w_in=v_w_in, ssm_lambda_re=v_ssm_lambda_re, ssm_lambda_im=v_ssm_lambda_im, ssm_log_dt=v_ssm_log_dt, ssm_b_re=v_ssm_b_re, ssm_b_im=v_ssm_b_im, ssm_c_re=v_ssm_c_re, ssm_c_im=v_ssm_c_im, ssm_d=v_ssm_d, ssm_w_glu=v_ssm_w_glu, ssm_b_glu=v_ssm_b_glu, ssm_w_out=v_ssm_w_out, conv_w=v_conv_w, conv_b=v_conv_b, conv_w_out=v_conv_w_out, w_o=v_w_o, ffn2_norm=v_ffn2_norm, ffn2_w_gate=v_ffn2_w_gate, ffn2_w_up=v_ffn2_w_up, ffn2_w_down=v_ffn2_w_down, final_norm=v_final_norm)
    names = list(P)
    sharded = ["ffn1_w_gate", "ffn1_w_up", "ffn1_w_down", "w_in", "ssm_w_glu", "ssm_w_out", "conv_w_out", "w_o",
               "ffn2_w_gate", "ffn2_w_up", "ffn2_w_down"]
    replicated = [n for n in names if n not in sharded and n != "conv_w"]

    S, D = x.shape[1], x.shape[2]
    W = ssm_d.shape[0]
    Dc = D // NDEV
    G, N = ssm_lambda_re.shape
    rows = G * N // LANE
    xh = x.reshape(S, D)
    target = loss_target.reshape(S, D)
    xi, yi, ci = lax.axis_index("x"), lax.axis_index("y"), lax.axis_index("c")
    c_arr = jnp.reshape(ci, (1,)).astype(I32)
    q_arr = jnp.reshape(2 * xi + yi, (1,)).astype(I32)
    row = lambda v: v.reshape(1, -1)

    transposed = ("ffn1_w_gate", "ffn1_w_up", "ffn2_w_gate", "ffn2_w_up")
    local = lambda table, n: table[n].T if n in transposed else table[n]

    groups = [sharded[0:2], sharded[2:3], ["w_in", "conv_w"], sharded[4:8], sharded[8:11]]
    started = {}
    me = 4 * xi + 2 * yi + ci

    def gather_start(gi):
        hold = 0.0 * started[0][4][0, 0] if gi else 0.0
        srcs = [conv_w + hold if n == "conv_w" else (local(P, n) + hold).astype(BF) for n in groups[gi]]
        lands = [lax.dynamic_update_slice(lax.empty((NDEV,) + s.shape, s.dtype), s[None], (me,) + (0,) * s.ndim)
                 for s in srcs]
        after = [started[gi - 1][4]] if gi else []
        started[gi] = _split_start("gather_start_%d" % gi, srcs, lands, _gather_copies, 4, after)
        return started[gi][4]

    def gathered(gi, after):
        _, lands = _split_wait("gather_wait_%d" % gi, started[gi], _gather_copies, after)
        return _gather_forward("gather_forward_%d" % gi, lands)

    tm = _pick(S, M_TILE, SUBLANE)
    th = _pick(S, M_TILE // 2, SUBLANE)
    tk = _pick(D, K_TILE, LANE)
    ts = _pick(S, K_TILE, SUBLANE)
    tn = _pick(D, 1024, LANE)

    tokens = [gather_start(gi) for gi in range(len(groups))]
    u1, r1 = _rms_fwd("rms1", xh, row(ffn1_norm), deps=tokens)
    wg1, wu1 = gathered(0, u1)
    h1, ffn1_saved, wd1 = _ffn_fwd("ffn1", u1, xh, wg1, wu1, lambda after: gathered(1, after)[0])
    u2, r2 = _rms_fwd("rms2", h1, row(mix_norm))
    w_in_f, cw_f = gathered(2, u2)
    cw = jnp.transpose(cw_f, (1, 0, 2)).reshape(3, W)
    (proj,) = _mm(
        "in_proj", (NDEV, S // tm, 1, D // tk),
        [(u2, (tm, tk), lambda b, i, j, k: (i, k)), (w_in_f, (None, tk, W), lambda b, i, j, k: (b, k, 0))],
        [(0, 1, NN, 0)], [], [((NDEV, S, W), F32, (None, tm, W), lambda b, i, j, k: (b, i, 0))], _plain, [(tm, W)])

    ssm_in = (ssm_lambda_re, ssm_lambda_im, ssm_log_dt, ssm_b_re, ssm_b_im, ssm_c_re, ssm_c_im)
    (a_re, a_im, bd_re, bd_im, cd_re, cd_imn), ssm_vjp = jax.vjp(_ssm_params, *ssm_in)
    bd_re_b, bd_im_b, cd_re_b, cd_imn_b = (t.astype(BF) for t in (bd_re, bd_im, cd_re, cd_imn))
    v_f = proj[0]
    v_bf = v_f.astype(BF)
    bu_re3, bu_im3 = _ssm_spread("ssm_bu", v_bf, bd_re_b, bd_im_b, NN, rows)
    s_re3, s_im3 = _scan_fwd(bu_re3, bu_im3, a_re, a_im)

    def y0_epilogue(acc, v_tile, d_tile):
        y0 = acc + d_tile * v_tile
        return y0, _gelu(y0)

    y0, y1, s_re_b, s_im_b = _ssm_collect("ssm_y0", s_re3, s_im3, cd_re_b, cd_imn_b, NN, v_f, row(ssm_d),
                                          y0_epilogue, [F32, BF])

    tw = _pick(W, 512, LANE)
    w_glu_f, w_so, w_co, w_o_f = gathered(3, y1)
    w_glu_f = w_glu_f.reshape(W, W)
    w_o_f = w_o_f.reshape(D, D)

    def glu_epilogue(accs, sides):
        q = accs[0] + sides[1]
        return q, _gelu(sides[0]) * _sigmoid(q)

    q_pre, y2 = _mm("ssm_glu", (1, S // tm, W // tw, 1),
                    [(y1, (tm, W), lambda b, i, j, k: (i, 0)), (w_glu_f, (W, tw), lambda b, i, j, k: (0, j))],
                    [(0, 1, NN, 0)],
                    [(y0, (tm, tw), lambda b, i, j, k: (i, j)), (row(ssm_b_glu), (1, tw), lambda b, i, j, k: (0, j))],
                    [((S, W), F32, (tm, tw), lambda b, i, j, k: (i, j)), ((S, W), BF, (tm, tw), lambda b, i, j, k: (i, j))],
                    glu_epilogue, [(tm, tw)])

    yb, conv = _conv_fwd(proj, cw, row(conv_b))

    per = W // Dc
    ga_blk = (proj, (None, tm, Dc), lambda b, i, j, k: (4 + b // per, i, b % per))
    gb_blk = (proj, (None, tm, Dc), lambda b, i, j, k: (6 + b // per, i, b % per))
    dc_out = ((S, D), BF, (tm, Dc), lambda b, i, j, k: (i, b))

    def merge_epilogue(accs, sides):
        za, zb = accs
        return _sigmoid(sides[0]) * za + _sigmoid(sides[1]) * zb, za, zb

    merged, z_a, z_b = _mm(
        "mix_merge", (NDEV, S // tm, 1, 1),
        [(y2, (tm, W), lambda b, i, j, k: (i, 0)), (yb, (tm, W), lambda b, i, j, k: (i, 0)),
         (w_so, (None, W, Dc), lambda b, i, j, k: (b, 0, 0)), (w_co, (None, W, Dc), lambda b, i, j, k: (b, 0, 0))],
        [(0, 2, NN, 0), (1, 3, NN, 1)], [ga_blk, gb_blk], [dc_out, dc_out, dc_out], merge_epilogue, [(tm, Dc)] * 2)

    (h2,) = _mm("mix_out", (1, S // tm, D // tn, D // tk),
                [(merged, (tm, tk), lambda b, i, j, k: (i, k)), (w_o_f, (tk, tn), lambda b, i, j, k: (k, j))],
                [(0, 1, NN, 0)], [(h1, (tm, tn), lambda b, i, j, k: (i, j))],
                [((S, D), F32, (tm, tn), lambda b, i, j, k: (i, j))],
                lambda accs, sides: [sides[0] + accs[0]], [(tm, tn)])

    u3, r3 = _rms_fwd("rms3", h2, row(ffn2_norm))
    wg2, wu2, wd2 = gathered(4, u3)
    h3, ffn2_saved, _ = _ffn_fwd("ffn2", u3, h2, wg2, wu2, wd2)
    loss_vec, dh3, dh3_half, d_final_norm = _loss_head("loss_head", h3, row(final_norm), target)
    loss = lax.psum(loss_vec[0, 0], ("x", "y", "c"))
    loss_done = jnp.zeros((SUBLANE, LANE), F32) + loss

    grads, deltas, new_m, new_v = {}, {}, {}, {}

    def rs_sibling_start(tag, parts):
        lands = [lax.empty((4,) + p.shape[1:], p.dtype) for p in parts]
        return _split_start("rs_sibling_start_" + tag, parts, lands, _sibling_copies, 4)

    def rs_chips_start(tag, sibling_begun, after):
        parts, lands = _split_wait("rs_sibling_wait_" + tag, sibling_begun, _sibling_copies, after)
        sums = [_sum_sibling("rs_sum_%s_%d" % (tag, a), p, land, c_arr) for a, (p, land) in enumerate(zip(parts, lands))]
        lands2 = [lax.empty((3,) + sm.shape[1:], sm.dtype) for sm in sums]
        return _split_start("rs_chips_start_" + tag, sums, lands2, _chips_copies, 3)

    def rs_end(tag, group, begun, after):
        sums, lands2 = _split_wait("rs_chips_wait_" + tag, begun, _chips_copies, after)
        for n, sm, land2 in zip(group, sums, lands2):
            res = _finish_sharded("adamw_" + n, sm, land2, q_arr, local(P, n), local(M, n), local(V, n))
            grads[n], deltas[n], new_m[n], new_v[n] = [t.T if n in transposed else t for t in res]

    f2_dwd, f2_dwgu, f2_du = _ffn_bwd("ffn2", dh3_half, u3, ffn2_saved, wg2, wu2, wd2, deps=[loss_done])
    dwd2 = f2_dwd()
    dwg2, dwu2 = f2_dwgu()
    sib_ffn2 = rs_sibling_start("ffn2", [dwg2, dwu2, dwd2])
    du3 = f2_du(deps=[sib_ffn2[4]])
    dh2, dh2_b, d_ffn2_norm = _rms_bwd("rms3_bwd", du3, h2, r3, row(ffn2_norm), dh3, 1.0)
    rs_ffn2 = rs_chips_start("ffn2", sib_ffn2, dh2)

    dg_out = ((2, S, W), BF, (None, tm, Dc), lambda b, i, j, k: (j // per, i, j % per))
    ga_blk2 = (proj, (None, tm, Dc), lambda b, i, j, k: (4 + j // per, i, j % per))
    gb_blk2 = (proj, (None, tm, Dc), lambda b, i, j, k: (6 + j // per, i, j % per))
    dcj = lambda arr: (arr, (tm, Dc), lambda b, i, j, k: (i, j))
    dcj_out = ((S, D), BF, (tm, Dc), lambda b, i, j, k: (i, j))

    def dmerge_epilogue(accs, sides):
        dm = accs[0]
        sa, sb = _sigmoid(sides[0]), _sigmoid(sides[1])
        za, zb = sides[2].astype(F32), sides[3].astype(F32)
        return dm * sa, dm * sb, dm * za * sa * (1.0 - sa), dm * zb * sb * (1.0 - sb)

    dz_a, dz_b, dga, dgb = _mm(
        "mix_out_dx", (1, S // tm, NDEV, D // tk),
        [(dh2_b, (tm, tk), lambda b, i, j, k: (i, k)), (w_o_f, (Dc, tk), lambda b, i, j, k: (j, k))],
        [(0, 1, NT, 0)], [ga_blk2, gb_blk2, dcj(z_a), dcj(z_b)], [dcj_out, dcj_out, dg_out, dg_out],
        dmerge_epilogue, [(tm, Dc)], deps=[rs_ffn2[4]])

    td = _pick(D, M_TILE, LANE)
    (dw_o,) = _mm("mix_out_dw", (1, D // td, D // tn, S // ts),
                  [(merged, (ts, td), lambda b, i, j, k: (k, i)), (dh2_b, (ts, tn), lambda b, i, j, k: (k, j))],
                  [(0, 1, TN, 0)], [], [((D, D), BF, (td, tn), lambda b, i, j, k: (i, j))], _plain, [(td, tn)])

    wout = ((NDEV, W, Dc), BF, (None, W, Dc), lambda b, i, j, k: (b, 0, 0))
    dw_so, dw_co = _mm(
        "mix_merge_dw", (NDEV, 1, 1, S // ts),
        [(y2, (ts, W), lambda b, i, j, k: (k, 0)), (yb, (ts, W), lambda b, i, j, k: (k, 0)),
         (dz_a, (ts, Dc), lambda b, i, j, k: (k, b)), (dz_b, (ts, Dc), lambda b, i, j, k: (k, b))],
        [(0, 2, TN, 0), (1, 3, TN, 1)], [], [wout, wout], _plain, [(W, Dc)] * 2)

    def dglu_epilogue(accs, sides):
        dy2, dyb = accs
        sq = _sigmoid(sides[1])
        return dy2 * _gelu(sides[0]) * sq * (1.0 - sq), dy2 * sq, dyb

    full_w = lambda arr: (arr, (th, W), lambda b, i, j, k: (i, 0))
    full_w_out = lambda dt: ((S, W), dt, (th, W), lambda b, i, j, k: (i, 0))
    dq, dy1p, dyb = _mm(
        "mix_merge_dx", (1, S // th, 1, NDEV),
        [(dz_a, (th, Dc), lambda b, i, j, k: (i, k)), (dz_b, (th, Dc), lambda b, i, j, k: (i, k)),
         (w_so, (None, W, Dc), lambda b, i, j, k: (k, 0, 0)), (w_co, (None, W, Dc), lambda b, i, j, k: (k, 0, 0))],
        [(0, 2, NT, 0), (1, 3, NT, 1)], [full_w(y0), full_w(q_pre)], [full_w_out(BF), full_w_out(F32), full_w_out(F32)],
        dglu_epilogue, [(th, W)] * 2)

    def dy0_epilogue(accs, sides):
        dy0 = (sides[0] + accs[0]) * _gelu_grad(sides[1])
        return dy0, dy0

    wj = lambda arr: (arr, (tm, tw), lambda b, i, j, k: (i, j))
    dy0, dy0_b = _mm("ssm_glu_dx", (1, S // tm, W // tw, 1),
                     [(dq, (tm, W), lambda b, i, j, k: (i, 0)), (w_glu_f, (tw, W), lambda b, i, j, k: (j, 0))],
                     [(0, 1, NT, 0)], [wj(dy1p), wj(y0)],
                     [((S, W), F32, (tm, tw), lambda b, i, j, k: (i, j)), ((S, W), BF, (tm, tw), lambda b, i, j, k: (i, j))],
                     dy0_epilogue, [(tm, tw)])

    (dw_glu,) = _mm("ssm_glu_dw", (1, W // tw, 1, S // ts),
                    [(y1, (ts, tw), lambda b, i, j, k: (k, i)), (dq, (ts, W), lambda b, i, j, k: (k, 0))],
                    [(0, 1, TN, 0)], [], [((W, W), BF, (tw, W), lambda b, i, j, k: (i, 0))], _plain, [(tw, W)])

    tr = _pick(S, 256, SUBLANE)
    rw = ((tr, W), lambda i: (i, 0))
    vec_w = ((1, W), F32, (1, W), lambda i: (0, 0))
    d_b_glu, d_ssm_d = _ew(
        "ssm_colsums", (S // tr,), [(dq,) + rw, (dy0,) + rw, (proj, (None, tr, W), lambda i: (0, i, 0))],
        [vec_w, vec_w],
        lambda dqv, dyv, vv: (jnp.sum(dqv.astype(F32), axis=0, keepdims=True), jnp.sum(dyv * vv, axis=0, keepdims=True)),
        acc=(0, 1))

    ds_re3, ds_im3 = _ssm_spread("ssm_ds", dy0_b, cd_re_b, cd_imn_b, NT, rows)
    lam_re3, lam_im3, da_re, da_im = _scan_bwd(ds_re3, ds_im3, s_re3, s_im3, a_re, a_im)
    dv, lam_re_b, lam_im_b = _ssm_collect("ssm_dv", lam_re3, lam_im3, bd_re_b, bd_im_b, NT, dy0, row(ssm_d),
                                          lambda acc, dy_tile, d_tile: [acc + dy_tile * d_tile], [BF])
    tiles, tch, tst = bd_re.shape
    tok_ch = lambda arr: (arr, (ts, tch), lambda b, i, j, k: (k, b))
    tok_st = lambda arr: (arr, (ts, tst), lambda b, i, j, k: (k, b))
    bd_out = ((tiles, tch, tst), F32, (None, tch, tst), lambda b, i, j, k: (b, 0, 0))
    cd_out = ((tiles, tst, tch), F32, (None, tst, tch), lambda b, i, j, k: (b, 0, 0))
    dbd_re, dbd_im = _mm("ssm_dbd", (tiles, 1, 1, S // ts), [tok_ch(v_bf), tok_st(lam_re_b), tok_st(lam_im_b)],
                         [(0, 1, TN, 0), (0, 2, TN, 1)], [], [bd_out, bd_out], _plain, [(tch, tst)] * 2)
    dcd_re, dcd_imn = _mm("ssm_dcd", (tiles, 1, 1, S // ts), [tok_st(s_re_b), tok_st(s_im_b), tok_ch(dy0_b)],
                          [(0, 2, TN, 0), (1, 2, TN, 1)], [], [cd_out, cd_out], _plain, [(tst, tch)] * 2)
    d_ssm = ssm_vjp((da_re, da_im, dbd_re, dbd_im, dcd_re, dcd_imn))

    dbg, dcg, dval, d_conv_w_full, d_conv_b = _conv_bwd(dyb, proj, conv, cw)
    dproj = jnp.concatenate([dv[None], dbg[None], dcg[None], dval[None], dga, dgb], axis=0)

    (dw_in,) = _mm("in_proj_dw", (NDEV, D // td, 1, S // ts),
                   [(u2, (ts, td), lambda b, i, j, k: (k, i)), (dproj, (None, ts, W), lambda b, i, j, k: (b, k, 0))],
                   [(0, 1, TN, 0)], [], [((NDEV, D, W), BF, (None, td, W), lambda b, i, j, k: (b, i, 0))],
                   _plain, [(td, W)])
    sib_mixer = rs_sibling_start(
        "mixer", [dw_in, dw_glu.reshape(NDEV, W // NDEV, W), dw_so, dw_co, dw_o.reshape(NDEV, Dc, D)])
    (du2,) = _mm("in_proj_dx", (1, S // tm, D // tn, NDEV),
                 [(dproj, (None, tm, W), lambda b, i, j, k: (k, i, 0)), (w_in_f, (None, tn, W), lambda b, i, j, k: (k, j, 0))],
                 [(0, 1, NT, 0)], [], [((S, D), F32, (tm, tn), lambda b, i, j, k: (i, j))], _plain, [(tm, tn)],
                 deps=[sib_mixer[4]])
    dh1, dh1_half, d_mix_norm = _rms_bwd("rms2_bwd", du2, h1, r2, row(mix_norm), dh2, 0.5)
    rs_mixer = rs_chips_start("mixer", sib_mixer, dh1)

    small = dict(mix_norm=d_mix_norm, ffn2_norm=d_ffn2_norm, final_norm=d_final_norm,
                 ssm_lambda_re=d_ssm[0], ssm_lambda_im=d_ssm[1], ssm_log_dt=d_ssm[2], ssm_b_re=d_ssm[3],
                 ssm_b_im=d_ssm[4], ssm_c_re=d_ssm[5], ssm_c_im=d_ssm[6], ssm_d=d_ssm_d, ssm_b_glu=d_b_glu,
                 conv_b=d_conv_b)
    replicated = [n for n in replicated if n != "ffn1_norm"] + ["ffn1_norm"]
    tile = SUBLANE * LANE

    def as_rows(p):
        flat = p.reshape(-1).astype(F32)
        return jnp.pad(flat, (0, -flat.shape[0] % tile)).reshape(-1, LANE)

    def pack(parts):
        return jnp.concatenate([as_rows(p) for p in parts], axis=0)

    cw_zero = jnp.zeros_like(d_conv_w_full)
    early_pk = pack([small[n] for n in replicated[:-1]] + [d_conv_w_full])
    early_land = lax.dynamic_update_slice(jnp.zeros((NDEV,) + early_pk.shape, F32), early_pk[None], (me, 0, 0))
    small_begun = _split_start("gather_small_start", [early_pk], [early_land], _everyone_copies, NDEV - 1)

    f1_dwd, f1_dwgu, f1_du = _ffn_bwd("ffn1", dh1_half, u1, ffn1_saved, wg1, wu1, wd1,
                                      deps=[rs_mixer[4], small_begun[4]])
    du1 = f1_du()
    dx, _, d_ffn1_norm = _rms_bwd("rms1_bwd", du1, xh, r1, row(ffn1_norm), dh1, 1.0)
    dwd1 = f1_dwd(deps=[d_ffn1_norm])
    late = d_ffn1_norm + 0.0 * dwd1[0, :1, :1].astype(F32)
    (late_all,) = _all_gather("gather_ffn1_norm_grad", [late.reshape(-1, LANE)])
    rs_ffn1_down = rs_chips_start("ffn1_down", rs_sibling_start("ffn1_down", [dwd1]), late_all)
    dwg1, dwu1 = f1_dwgu(deps=[rs_ffn1_down[4]])
    rs_ffn1_gate_up = rs_chips_start("ffn1_gate_up", rs_sibling_start("ffn1_gate_up", [dwg1, dwu1]), None)
    rs_end("ffn2", sharded[8:11], rs_ffn2, rs_ffn1_gate_up[4])
    rs_end("mixer", sharded[3:8], rs_mixer, [grads[n] for n in sharded[8:11]])
    _, (early_all,) = _split_wait("gather_small_wait", small_begun, _everyone_copies, [grads[n] for n in sharded[3:8]])
    small_all = jnp.concatenate([early_all, late_all], axis=1)
    g_pk, d_pk, m_pk, v_pk = _finish_replicated(
        "adamw_replicated", small_all, pack([P[n] for n in replicated[:-1]] + [cw_zero, P["ffn1_norm"]]),
        pack([M[n] for n in replicated[:-1]] + [cw_zero, M["ffn1_norm"]]),
        pack([V[n] for n in replicated[:-1]] + [cw_zero + 1.0, V["ffn1_norm"]]))

    def unpack(pk, r0, like):
        nr = -(-like.size // tile) * SUBLANE
        return pk[r0:r0 + nr].reshape(-1)[:like.size].reshape(like.shape), r0 + nr

    r0 = 0
    for n in replicated[:-1] + ["conv_w", "ffn1_norm"]:
        like = d_conv_w_full if n == "conv_w" else P[n]
        for store, pk in ((grads, g_pk), (deltas, d_pk), (new_m, m_pk), (new_v, v_pk)):
            store[n], r1 = unpack(pk, r0, like)
        r0 = r1
    g_cw_full = grads["conv_w"]
    cwl = conv_w.shape[1]
    g_cw = lax.dynamic_slice_in_dim(g_cw_full, me * cwl, cwl, axis=1)
    full3 = ((3, cwl), lambda i: (0, 0))
    grads["conv_w"], deltas["conv_w"], new_m["conv_w"], new_v["conv_w"] = _ew(
        "adamw_conv_w", (1,), [(g_cw,) + full3, (conv_w,) + full3, (m_conv_w,) + full3, (v_conv_w,) + full3],
        [((3, cwl), F32) + full3] * 4, lambda g, w, m, v: (g,) + _adamw(w, g, m, v))
    rs_end("ffn1_down", sharded[2:3], rs_ffn1_down, [g_pk, grads["conv_w"]])
    rs_end("ffn1_gate_up", sharded[0:2], rs_ffn1_gate_up, grads["ffn1_w_down"])

    return (loss, dx.reshape(x.shape), *[grads[n] for n in names], *[deltas[n] for n in names],
            *[new_m[n] for n in names], *[new_v[n] for n in names])
```

```python
import math

import jax
import jax.numpy as jnp
from jax import lax
from jax.experimental import pallas as pl
from jax.experimental.pallas import tpu as pltpu

F32 = jnp.float32
BF = jnp.bfloat16
I32 = jnp.int32
MESH = pl.DeviceIdType.MESH
LANE = 128
SUBLANE = 8
NDEV = 8
EW_BLOCK = 256 * 1024
M_TILE = 1024
K_TILE = 2048
K_SHARDS = 2
DMA_CHUNKS = 4
EPS = 1e-6
ADAM_LR, ADAM_B1, ADAM_B2, ADAM_EPS, ADAM_WD, ADAM_STEP = 0.001, 0.9, 0.999, 1e-08, 0.01, 10
NN = ((1,), (0,))
NT = ((1,), (1,))
TN = ((0,), (0,))
HBM = pl.BlockSpec(memory_space=pltpu.HBM)


def _pick(n, pref, mult):
    t = min(pref, n)
    t -= t % mult
    while t >= mult:
        if n % t == 0:
            return t
        t -= mult
    return n


def _sigmoid(x):
    return 1.0 / (1.0 + jnp.exp(-x))


_GELU_C = math.sqrt(2.0 / math.pi)


def _gelu(x):
    return 0.5 * x * (1.0 + jnp.tanh(_GELU_C * (x + 0.044715 * x * x * x)))


def _gelu_grad(x):
    t = jnp.tanh(_GELU_C * (x + 0.044715 * x * x * x))
    return 0.5 * (1.0 + t) + 0.5 * x * (1.0 - t * t) * _GELU_C * (1.0 + 3.0 * 0.044715 * x * x)


def _dep_specs(deps, rank):
    return [(d, d.shape, lambda *_, nd=d.ndim: (0,) * nd) for d in deps]


def _mm(name, grid, ops, pairs, sides, outs, epilogue, acc_shapes, deps=()):
    nk = grid[-1]
    n_ops, n_sides, n_outs = len(ops), len(sides), len(outs)
    dep_specs = _dep_specs(deps, len(grid))
    n_deps = len(dep_specs)

    def body(*refs):
        op_refs = refs[:n_ops]
        side_refs = refs[n_ops:n_ops + n_sides]
        out_refs = refs[n_ops + n_sides + n_deps:n_ops + n_sides + n_deps + n_outs]
        acc_refs = refs[n_ops + n_sides + n_deps + n_outs:]

        def partials():
            res = [None] * len(acc_shapes)
            for ia, ib, dims, ai in pairs:
                a_ref, b_ref = op_refs[ia], op_refs[ib]
                for s in range(a_ref.shape[0] if len(a_ref.shape) == 3 else 1):
                    a, b = (a_ref[s], b_ref[s]) if len(a_ref.shape) == 3 else (a_ref[...], b_ref[...])
                    p = lax.dot_general(a, b, (dims, ((), ())), preferred_element_type=F32)
                    res[ai] = p if res[ai] is None else res[ai] + p
            return res

        def finish(accs):
            vals = epilogue(accs, [s[...] for s in side_refs])
            for o, v in zip(out_refs, vals):
                o[...] = v.astype(o.dtype)

        if nk == 1:
            finish(partials())
        else:
            k = pl.program_id(len(grid) - 1)

            @pl.when(k == 0)
            def _():
                for a, p in zip(acc_refs, partials()):
                    a[...] = p

            @pl.when(k > 0)
            def _():
                for a, p in zip(acc_refs, partials()):
                    a[...] += p

            @pl.when(k == nk - 1)
            def _():
                finish([a[...] for a in acc_refs])

    return pl.pallas_call(
        body, name=name, grid=grid,
        in_specs=[pl.BlockSpec(b, m) for (_, b, m) in list(ops) + list(sides) + dep_specs],
        out_specs=[pl.BlockSpec(b, m) for (_, _, b, m) in outs],
        out_shape=[jax.ShapeDtypeStruct(s, d) for (s, d, _, _) in outs],
        scratch_shapes=[pltpu.VMEM(s, F32) for s in acc_shapes] if nk > 1 else [],
        compiler_params=pltpu.CompilerParams(
            dimension_semantics=("parallel",) * (len(grid) - 1) + ("arbitrary",)),
    )(*[a for (a, _, _) in list(ops) + list(sides) + dep_specs])


def _ew(name, grid, ins, outs, fn, acc=(), deps=()):
    n_in = len(ins)
    dep_specs = _dep_specs(deps, len(grid))

    def body(*refs):
        vals = fn(*[r[...] for r in refs[:n_in]])
        first = pl.program_id(0) == 0
        for idx, (o, v) in enumerate(zip(refs[n_in + len(dep_specs):], vals)):
            if idx in acc:
                @pl.when(first)
                def _(o=o, v=v):
                    o[...] = v.astype(o.dtype)

                @pl.when(jnp.logical_not(first))
                def _(o=o, v=v):
                    o[...] += v.astype(o.dtype)
            else:
                o[...] = v.astype(o.dtype)

    return pl.pallas_call(
        body, name=name, grid=grid,
        in_specs=[pl.BlockSpec(b, m) for (_, b, m) in list(ins) + dep_specs],
        out_specs=[pl.BlockSpec(b, m) for (_, _, b, m) in outs],
        out_shape=[jax.ShapeDtypeStruct(s, d) for (s, d, _, _) in outs],
        compiler_params=pltpu.CompilerParams(
            dimension_semantics=(("arbitrary",) if acc else ("parallel",)) * len(grid)),
    )(*[a for (a, _, _) in list(ins) + dep_specs])


def _position():
    x, y, c = lax.axis_index("x"), lax.axis_index("y"), lax.axis_index("c")
    chips = [(1 - x, y), (x, 1 - y), (1 - x, 1 - y)]
    return x, y, c, chips


def _all_gather(name, shards):
    n = len(shards)

    def body(*refs):
        xs, outs = refs[:n], refs[n:2 * n]
        send_sems, recv_sems, local_sems = refs[2 * n:]
        x, y, c, chips = _position()
        me, sibling = (x, y, c), (x, y, 1 - c)

        def copy(a, k, block, to, src=None):
            dst = outs[a].at[4 * block[0] + 2 * block[1] + block[2]]
            return pltpu.make_async_remote_copy(
                src_ref=dst if src is None else src, dst_ref=dst,
                send_sem=send_sems.at[a, k], recv_sem=recv_sems.at[a, k],
                device_id=to, device_id_type=MESH)

        mine = [pltpu.make_async_copy(xs[a], outs[a].at[4 * x + 2 * y + c], local_sems.at[a]) for a in range(n)]
        for cp in mine:
            cp.start()
        first = []
        for a in range(n):
            first.append(copy(a, 0, me, sibling, src=xs[a]))
            first += [copy(a, 1 + j, me, (*chip, c), src=xs[a]) for j, chip in enumerate(chips)]
        for cp in first:
            cp.start()
        passed = []
        for a in range(n):
            for j, chip in enumerate(chips):
                copy(a, 1 + j, (*chip, c), me).wait_recv()
                cp = copy(a, 4 + j, (*chip, c), sibling)
                cp.start()
                passed.append(cp)
        for a in range(n):
            copy(a, 0, sibling, me).wait_recv()
            for j, chip in enumerate(chips):
                copy(a, 4 + j, (*chip, 1 - c), me).wait_recv()
        for cp in first + passed:
            cp.wait_send()
        for cp in mine:
            cp.wait()

    return pl.pallas_call(
        body, name=name,
        out_shape=[jax.ShapeDtypeStruct((NDEV,) + s.shape, s.dtype) for s in shards],
        in_specs=[HBM] * n, out_specs=[HBM] * n,
        scratch_shapes=[pltpu.SemaphoreType.DMA((n, 7)), pltpu.SemaphoreType.DMA((n, 7)),
                        pltpu.SemaphoreType.DMA((n,))],
    )(*shards)


SEM = pl.BlockSpec(memory_space=pltpu.SEMAPHORE)
EFFECT = pltpu.SideEffectType.DATAFLOW_SIDE_EFFECTING


def _in_hbm(v):
    return pltpu.with_memory_space_constraint(v, pltpu.HBM)


def _split_start(name, srcs, lands, make_copies, n_per, after=()):
    n = len(srcs)
    after = list(after)

    def body(*refs):
        send_sems, recv_sems = refs[2 * n + len(after)], refs[2 * n + len(after) + 1]
        for cp in make_copies(refs[:n], refs[n:2 * n], send_sems, recv_sems):
            cp.start()
        refs[-1][...] = jnp.zeros_like(refs[-1])

    outs = pl.pallas_call(
        body, name=name,
        out_shape=(pltpu.SemaphoreType.DMA((n * n_per,)), pltpu.SemaphoreType.DMA((n * n_per,)),
                   *[pltpu.HBM(v.shape, v.dtype) for v in list(srcs) + list(lands)],
                   jax.ShapeDtypeStruct((SUBLANE, LANE), F32)),
        in_specs=[HBM] * (2 * n) + [pl.BlockSpec(memory_space=pl.ANY)] * len(after),
        out_specs=(SEM, SEM, *[HBM] * (2 * n), pl.BlockSpec(memory_space=pltpu.VMEM)),
        input_output_aliases={i: 2 + i for i in range(2 * n)},
        compiler_params=pltpu.CompilerParams(has_side_effects=EFFECT),
    )(*[_in_hbm(v) for v in list(srcs) + list(lands)], *after)
    return outs[0], outs[1], list(outs[2:2 + n]), list(outs[2 + n:2 + 2 * n]), outs[-1]


def _split_wait(name, started, make_copies, after):
    send_sems, recv_sems, srcs, lands, _ = started
    n = len(srcs)

    def body(*refs):
        for cp in make_copies(refs[:n], refs[n:2 * n], refs[2 * n], refs[2 * n + 1]):
            cp.wait_send()
            cp.wait_recv()

    order = [] if after is None else list(after) if isinstance(after, (list, tuple)) else [after]
    outs = pl.pallas_call(
        body, name=name,
        out_shape=tuple(pltpu.HBM(v.shape, v.dtype) for v in srcs + lands),
        in_specs=[HBM] * (2 * n) + [SEM, SEM] + [pl.BlockSpec(memory_space=pl.ANY)] * len(order),
        out_specs=tuple([HBM] * (2 * n)),
        input_output_aliases={i: i for i in range(2 * n)},
        compiler_params=pltpu.CompilerParams(has_side_effects=EFFECT),
    )(*srcs, *lands, send_sems, recv_sems, *order)
    return list(outs[:n]), list(outs[n:])


def _gather_copies(xs, lands, send_sems, recv_sems):
    x, y, c, chips = _position()
    copies = []
    for a in range(len(xs)):
        for k, peer in enumerate([(x, y, 1 - c)] + [(*chip, c) for chip in chips]):
            copies.append(pltpu.make_async_remote_copy(
                src_ref=xs[a], dst_ref=lands[a].at[4 * x + 2 * y + c],
                send_sem=send_sems.at[4 * a + k], recv_sem=recv_sems.at[4 * a + k], device_id=peer, device_id_type=MESH))
    return copies


def _chips_copies(ps, lands, send_sems, recv_sems):
    x, y, c, chips = _position()
    copies = []
    for a in range(len(ps)):
        for j, chip in enumerate(chips):
            copies.append(pltpu.make_async_remote_copy(
                src_ref=ps[a].at[2 * chip[0] + chip[1]], dst_ref=lands[a].at[j],
                send_sem=send_sems.at[3 * a + j], recv_sem=recv_sems.at[3 * a + j], device_id=(*chip, c),
                device_id_type=MESH))
    return copies


def _sibling_copies(gs, lands, send_sems, recv_sems):
    x, y, c, _ = _position()
    copies = []
    for a in range(len(gs)):
        for q in range(4):
            copies.append(pltpu.make_async_remote_copy(
                src_ref=gs[a].at[2 * q + 1 - c], dst_ref=lands[a].at[q],
                send_sem=send_sems.at[4 * a + q], recv_sem=recv_sems.at[4 * a + q],
                device_id=(x, y, 1 - c), device_id_type=MESH))
    return copies


def _everyone_copies(xs, lands, send_sems, recv_sems):
    x, y, c, _ = _position()
    flip = lambda v, bit: 1 - v if bit else v
    copies = []
    for a in range(len(xs)):
        for k in range(1, NDEV):
            copies.append(pltpu.make_async_remote_copy(
                src_ref=xs[a], dst_ref=lands[a].at[4 * x + 2 * y + c],
                send_sem=send_sems.at[7 * a + k - 1], recv_sem=recv_sems.at[7 * a + k - 1],
                device_id=(flip(x, k & 4), flip(y, k & 2), flip(c, k & 1)), device_id_type=MESH))
    return copies


def _row_chunks(rows, dtype):
    unit = SUBLANE * (4 // jnp.dtype(dtype).itemsize)
    units = rows // unit
    if rows % unit or units < 2:
        return [(0, rows)]
    k = min(DMA_CHUNKS, units)
    sizes = [(units // k + (1 if i < units % k else 0)) * unit for i in range(k)]
    return [(sum(sizes[:i]), sz) for i, sz in enumerate(sizes)]


def _gather_forward(name, lands):
    n = len(lands)

    def body(*refs):
        ins, outs = refs[:n], refs[n:2 * n]
        send_sems, recv_sems = refs[2 * n:]
        x, y, c, chips = _position()
        whole, chunks = [], []
        for a in range(n):
            rows = _row_chunks(ins[a].shape[1], ins[a].dtype)
            for j, chip in enumerate(chips):
                slot = 4 * chip[0] + 2 * chip[1]

                def to_sibling(src, dst):
                    return pltpu.make_async_remote_copy(
                        src_ref=src, dst_ref=dst, send_sem=send_sems.at[a, j], recv_sem=recv_sems.at[a, j],
                        device_id=(x, y, 1 - c), device_id_type=MESH)

                whole.append(to_sibling(ins[a].at[slot + c], outs[a].at[slot + 1 - c]))
                chunks += [to_sibling(ins[a].at[slot + c, pl.ds(r0, nr)], outs[a].at[slot + c, pl.ds(r0, nr)])
                           for r0, nr in rows]
        for cp in chunks:
            cp.start()
        for cp in whole:
            cp.wait()

    return pl.pallas_call(
        body, name=name,
        out_shape=[jax.ShapeDtypeStruct(l.shape, l.dtype) for l in lands],
        in_specs=[HBM] * n, out_specs=[HBM] * n,
        input_output_aliases={a: a for a in range(n)},
        scratch_shapes=[pltpu.SemaphoreType.DMA((n, 3)), pltpu.SemaphoreType.DMA((n, 3))],
    )(*lands)


def _sum_sibling(name, g, land, c_arr):
    _, R, C = g.shape
    tr = _pick(R, 512, SUBLANE)

    def body(c_ref, g_ref, l_ref, o_ref):
        o_ref[...] = (g_ref[...].astype(F32) + l_ref[...].astype(F32)).astype(o_ref.dtype)

    return pl.pallas_call(
        body, name=name,
        grid_spec=pltpu.PrefetchScalarGridSpec(
            num_scalar_prefetch=1, grid=(4, R // tr),
            in_specs=[pl.BlockSpec((None, tr, C), lambda q, i, cr: (2 * q + cr[0], i, 0)),
                      pl.BlockSpec((None, tr, C), lambda q, i, cr: (q, i, 0))],
            out_specs=pl.BlockSpec((None, tr, C), lambda q, i, cr: (q, i, 0))),
        out_shape=jax.ShapeDtypeStruct((4, R, C), g.dtype),
        compiler_params=pltpu.CompilerParams(dimension_semantics=("parallel", "parallel")),
    )(c_arr, g, land)


def _adamw(w, g, m, v):
    m = ADAM_B1 * m + (1.0 - ADAM_B1) * g
    v = ADAM_B2 * v + (1.0 - ADAM_B2) * (g * g)
    m_hat = m / (1.0 - ADAM_B1 ** ADAM_STEP)
    v_hat = v / (1.0 - ADAM_B2 ** ADAM_STEP)
    delta = -ADAM_LR * (m_hat / (jnp.sqrt(v_hat) + ADAM_EPS) + ADAM_WD * w)
    return delta, m, v


def _finish_sharded(name, sums, land, q_arr, w, m, v):
    R, C = w.shape
    tr = _pick(R, 512, SUBLANE)
    tc = _pick(C, max(LANE, EW_BLOCK // tr), LANE)

    def body(q_ref, p_ref, l_ref, w_ref, m_ref, v_ref, g_out, d_out, m_out, v_out):
        g = p_ref[...].astype(F32)
        for j in range(3):
            g = g + l_ref[j].astype(F32)
        d, mn, vn = _adamw(w_ref[...], g, m_ref[...], v_ref[...])
        g_out[...] = g
        d_out[...] = d
        m_out[...] = mn
        v_out[...] = vn

    blk = pl.BlockSpec((tr, tc), lambda i, j, qr: (i, j))
    return pl.pallas_call(
        body, name=name,
        grid_spec=pltpu.PrefetchScalarGridSpec(
            num_scalar_prefetch=1, grid=(R // tr, C // tc),
            in_specs=[pl.BlockSpec((None, tr, tc), lambda i, j, qr: (qr[0], i, j)),
                      pl.BlockSpec((3, tr, tc), lambda i, j, qr: (0, i, j)), blk, blk, blk],
            out_specs=[blk] * 4),
        out_shape=[jax.ShapeDtypeStruct((R, C), F32)] * 4,
        compiler_params=pltpu.CompilerParams(dimension_semantics=("parallel", "parallel")),
    )(q_arr, sums, land, w, m, v)


def _finish_replicated(name, gathered, w, m, v):
    _, R, C = gathered.shape
    tr = _pick(R, 256, SUBLANE)

    def fn(gv, wv, mv, vv):
        g = gv[0]
        for d in range(1, NDEV):
            g = g + gv[d]
        dl, mn, vn = _adamw(wv, g, mv, vv)
        return g, dl, mn, vn

    row = ((tr, C), lambda i: (i, 0))
    return _ew(name, (R // tr,),
               [(gathered, (NDEV, tr, C), lambda i: (0, i, 0)), (w,) + row, (m,) + row, (v,) + row],
               [((R, C), F32) + row] * 4, fn)


def _rms_fwd(name, h, g, deps=()):
    S, D = h.shape
    tr = _pick(S, 256, SUBLANE)

    def fn(hv, gv):
        r = lax.rsqrt(jnp.mean(hv * hv, axis=-1, keepdims=True) + EPS)
        return hv * r * gv, r

    return _ew(name, (S // tr,),
               [(h, (tr, D), lambda i: (i, 0)), (g, (1, D), lambda i: (0, 0))],
               [((S, D), BF, (tr, D), lambda i: (i, 0)), ((S, 1), F32, (tr, 1), lambda i: (i, 0))], fn, deps=deps)


def _rms_bwd(name, du, h, r, g, dres, scale):
    S, D = h.shape
    tr = _pick(S, 256, SUBLANE)

    def fn(duv, hv, rv, gv, drv):
        xn = hv * rv
        dxn = duv * gv
        dh = drv + rv * (dxn - xn * jnp.mean(dxn * xn, axis=-1, keepdims=True))
        return dh, scale * dh, jnp.sum(duv * xn, axis=0, keepdims=True)

    row = ((tr, D), lambda i: (i, 0))
    return _ew(name, (S // tr,),
               [(du,) + row, (h,) + row, (r, (tr, 1), lambda i: (i, 0)), (g, (1, D), lambda i: (0, 0)), (dres,) + row],
               [((S, D), F32) + row, ((S, D), BF) + row, ((1, D), F32, (1, D), lambda i: (0, 0))], fn, acc=(2,))


def _loss_head(name, h, g, target):
    S, D = h.shape
    tr = _pick(S, 256, SUBLANE)

    def fn(hv, gv, tv):
        r = lax.rsqrt(jnp.mean(hv * hv, axis=-1, keepdims=True) + EPS)
        xn = hv * r
        diff = xn * gv - tv
        loss = 0.5 * jnp.sum(jnp.mean(diff * diff, axis=-1, keepdims=True))
        dout = diff / D
        dxn = dout * gv
        dh = r * (dxn - xn * jnp.mean(dxn * xn, axis=-1, keepdims=True))
        return (jnp.zeros((1, LANE), F32) + loss, dh, 0.5 * dh, jnp.sum(dout * xn, axis=0, keepdims=True))

    row = ((tr, D), lambda i: (i, 0))
    return _ew(name, (S // tr,),
               [(h,) + row, (g, (1, D), lambda i: (0, 0)), (target,) + row],
               [((1, LANE), F32, (1, LANE), lambda i: (0, 0)), ((S, D), F32) + row, ((S, D), BF) + row,
                ((1, D), F32, (1, D), lambda i: (0, 0))], fn, acc=(0, 3))


def _ffn_fwd(tag, u, h, wg, fetch_wu, fetch_wd):
    S, D = u.shape
    Fs = wg.shape[1]
    tm, tk = _pick(S, M_TILE, SUBLANE), _pick(D, K_TILE, LANE)
    act = ((NDEV, S, Fs), BF, (None, tm, Fs), lambda b, i, j, k: (b, i, 0))
    lhs = (u, (tm, tk), lambda b, i, j, k: (i, k))
    rhs = lambda w: (w, (None, Fs, tk), lambda b, i, j, k: (b, 0, k))
    (gt,) = _mm(tag + "_gate", (NDEV, S // tm, 1, D // tk), [lhs, rhs(wg)], [(0, 1, NT, 0)], [], [act],
                lambda accs, sides: accs, [(tm, Fs)])
    wu = fetch_wu(gt)

    def up_epilogue(accs, sides):
        g = sides[0].astype(F32)
        return accs[0], g * _sigmoid(g) * accs[0]

    up, a = _mm(tag + "_up", (NDEV, S // tm, 1, D // tk), [lhs, rhs(wu)], [(0, 1, NT, 0)],
                [(gt, (None, tm, Fs), lambda b, i, j, k: (b, i, 0))], [act, act], up_epilogue, [(tm, Fs)])
    wd = fetch_wd(a)
    tn = _pick(D, 1024, LANE)
    (hn,) = _mm(
        tag + "_down", (1, S // tm, D // tn, NDEV // K_SHARDS),
        [(a, (K_SHARDS, tm, Fs), lambda b, i, j, k: (k, i, 0)), (wd, (K_SHARDS, Fs, tn), lambda b, i, j, k: (k, 0, j))],
        [(0, 1, NN, 0)], [(h, (tm, tn), lambda b, i, j, k: (i, j))],
        [((S, D), F32, (tm, tn), lambda b, i, j, k: (i, j))],
        lambda accs, sides: [sides[0] + 0.5 * accs[0]], [(tm, tn)])
    return hn, (gt, up, a), wu, wd


def _ffn_bwd(tag, dhs, u, saved, wg, wu, wd, deps=()):
    gt, up, a = saved
    S, D = u.shape
    Fs = wg.shape[1]
    tm, tk = _pick(S, M_TILE, SUBLANE), _pick(D, K_TILE, LANE)
    act_in = lambda arr: (arr, (None, tm, Fs), lambda b, i, j, k: (b, i, 0))
    act_out = ((NDEV, S, Fs), BF, (None, tm, Fs), lambda b, i, j, k: (b, i, 0))

    def act_epilogue(accs, sides):
        da = accs[0]
        gtv, upv = sides[0].astype(F32), sides[1].astype(F32)
        sg = _sigmoid(gtv)
        return da * upv * sg * (1.0 + gtv * (1.0 - sg)), da * gtv * sg

    dgt, dup = _mm(
        tag + "_dact", (NDEV, S // tm, 1, D // tk),
        [(dhs, (tm, tk), lambda b, i, j, k: (i, k)), (wd, (None, Fs, tk), lambda b, i, j, k: (b, 0, k))],
        [(0, 1, NT, 0)], [act_in(gt), act_in(up)], [act_out, act_out], act_epilogue, [(tm, Fs)], deps=deps)

    ts = _pick(S, K_TILE, SUBLANE)
    tn = _pick(D, 1024, LANE)
    wgrad = ((NDEV, Fs, D), BF, (None, Fs, tn), lambda b, i, j, k: (b, 0, j))
    tok = lambda arr: (arr, (None, ts, Fs), lambda b, i, j, k: (b, k, 0))

    def grad_down(deps=()):
        return _mm(
            tag + "_dwd", (NDEV, 1, D // tn, S // ts),
            [tok(a), (dhs, (ts, tn), lambda b, i, j, k: (k, j))],
            [(0, 1, TN, 0)], [], [wgrad], lambda accs, sides: accs, [(Fs, tn)], deps=deps)[0]

    def grad_gate_up(deps=()):
        return _mm(
            tag + "_dwgu", (NDEV, 1, D // tn, S // ts),
            [tok(dgt), tok(dup), (u, (ts, tn), lambda b, i, j, k: (k, j))],
            [(0, 2, TN, 0), (1, 2, TN, 1)], [], [wgrad, wgrad], lambda accs, sides: accs, [(Fs, tn), (Fs, tn)],
            deps=deps)

    def du(deps=()):
        return _mm(
            tag + "_du", (1, S // tm, D // tn, NDEV // K_SHARDS),
            [(dgt, (K_SHARDS, tm, Fs), lambda b, i, j, k: (k, i, 0)), (dup, (K_SHARDS, tm, Fs), lambda b, i, j, k: (k, i, 0)),
             (wg, (K_SHARDS, Fs, tn), lambda b, i, j, k: (k, 0, j)), (wu, (K_SHARDS, Fs, tn), lambda b, i, j, k: (k, 0, j))],
            [(0, 2, NN, 0), (1, 3, NN, 0)], [], [((S, D), F32, (tm, tn), lambda b, i, j, k: (i, j))],
            lambda accs, sides: accs, [(tm, tn)], deps=deps)[0]

    return grad_down, grad_gate_up, du


def _ssm_params(lam_re, lam_im, log_dt, b_re, b_im, c_re, c_im):
    G, N = lam_re.shape
    C = b_re.shape[2]
    lam_re = jnp.minimum(lam_re, -1e-4)
    dt = jnp.exp(log_dt)[:, None]
    mag = jnp.exp(lam_re * dt)
    a_re = mag * jnp.cos(lam_im * dt)
    a_im = mag * jnp.sin(lam_im * dt)
    den = lam_re * lam_re + lam_im * lam_im
    p = a_re - 1.0
    f_re = ((p * lam_re + a_im * lam_im) / den)[:, :, None]
    f_im = ((a_im * lam_re - p * lam_im) / den)[:, :, None]
    bb_re = f_re * b_re - f_im * b_im
    bb_im = f_re * b_im + f_im * b_re
    gpt = LANE // C
    tiles = G // gpt
    eye = jnp.eye(gpt, dtype=F32)

    def bd(bb):
        return jnp.einsum("bgnc,gh->bgchn", bb.reshape(tiles, gpt, N, C), eye).reshape(tiles, gpt * C, gpt * N)

    def cd(cc):
        return jnp.einsum("bgcn,gh->bgnhc", cc.reshape(tiles, gpt, C, N), eye).reshape(tiles, gpt * N, gpt * C)

    rows = G * N // LANE
    return (a_re.reshape(rows, LANE), a_im.reshape(rows, LANE), bd(bb_re), bd(bb_im), cd(c_re), cd(-c_im))


def _tile_states(ref3, b, per):
    return jnp.concatenate([ref3[:, per * b + r, :] for r in range(per)], axis=1).astype(BF)


def _ssm_spread(name, x, m_re, m_im, dims, rows):
    S, W = x.shape
    tiles = m_re.shape[0]
    tch, per = W // tiles, rows // tiles
    tq = _pick(S, 256, SUBLANE)

    def body(x_ref, mre_ref, mim_ref, ore_ref, oim_ref):
        for b in range(tiles):
            xb = x_ref[:, b * tch:(b + 1) * tch]
            for m_ref, o_ref in ((mre_ref, ore_ref), (mim_ref, oim_ref)):
                val = lax.dot_general(xb, m_ref[b], (dims, ((), ())), preferred_element_type=F32)
                for r in range(per):
                    o_ref[:, per * b + r, :] = val[:, r * LANE:(r + 1) * LANE]

    whole = lambda m: pl.BlockSpec(m.shape, lambda i: (0, 0, 0))
    st = pl.BlockSpec((tq, rows, LANE), lambda i: (i, 0, 0))
    return pl.pallas_call(
        body, name=name, grid=(S // tq,),
        in_specs=[pl.BlockSpec((tq, W), lambda i: (i, 0)), whole(m_re), whole(m_im)], out_specs=[st, st],
        out_shape=[jax.ShapeDtypeStruct((S, rows, LANE), F32)] * 2,
        compiler_params=pltpu.CompilerParams(dimension_semantics=("parallel",)),
    )(x, m_re, m_im)


def _ssm_collect(name, z_re3, z_im3, m_re, m_im, dims, side, gain, epilogue, out_dtypes):
    S, rows, _ = z_re3.shape
    tiles = m_re.shape[0]
    W = side.shape[1]
    tch, per = W // tiles, rows // tiles
    tq = _pick(S, 256, SUBLANE)
    n_out = len(out_dtypes)

    def body(zre_ref, zim_ref, mre_ref, mim_ref, side_ref, gain_ref, *out_refs):
        for b in range(tiles):
            cols = slice(b * tch, (b + 1) * tch)
            zre, zim = _tile_states(zre_ref, b, per), _tile_states(zim_ref, b, per)
            acc = lax.dot_general(zre, mre_ref[b], (dims, ((), ())), preferred_element_type=F32)
            acc = acc + lax.dot_general(zim, mim_ref[b], (dims, ((), ())), preferred_element_type=F32)
            for o, v in zip(out_refs[:n_out], epilogue(acc, side_ref[:, cols], gain_ref[:, cols])):
                o[:, cols] = v.astype(o.dtype)
            out_refs[n_out][:, b * per * LANE:(b + 1) * per * LANE] = zre
            out_refs[n_out + 1][:, b * per * LANE:(b + 1) * per * LANE] = zim

    whole = lambda m: pl.BlockSpec(m.shape, lambda i: (0, 0, 0))
    st = pl.BlockSpec((tq, rows, LANE), lambda i: (i, 0, 0))
    ch = pl.BlockSpec((tq, W), lambda i: (i, 0))
    flat = pl.BlockSpec((tq, rows * LANE), lambda i: (i, 0))
    return pl.pallas_call(
        body, name=name, grid=(S // tq,),
        in_specs=[st, st, whole(m_re), whole(m_im), ch, pl.BlockSpec((1, W), lambda i: (0, 0))],
        out_specs=[ch] * n_out + [flat, flat],
        out_shape=[jax.ShapeDtypeStruct((S, W), dt) for dt in out_dtypes]
        + [jax.ShapeDtypeStruct((S, rows * LANE), BF)] * 2,
        compiler_params=pltpu.CompilerParams(dimension_semantics=("parallel",)),
    )(z_re3, z_im3, m_re, m_im, side, gain)


def _scan_fwd(bu_re, bu_im, a_re, a_im):
    S, R, _ = bu_re.shape
    tc = _pick(S, 256, SUBLANE)

    def body(bre, bim, are, aim, sre, sim, carry):
        @pl.when(pl.program_id(0) == 0)
        def _():
            carry[...] = jnp.zeros_like(carry)

        ar, ai = are[...], aim[...]

        def step(t, c):
            pr, pi = c
            nr = ar * pr - ai * pi + bre[t]
            ni = ar * pi + ai * pr + bim[t]
            sre[t] = nr
            sim[t] = ni
            return nr, ni

        pr, pi = lax.fori_loop(0, tc, step, (carry[0], carry[1]), unroll=8)
        carry[0] = pr
        carry[1] = pi

    blk = pl.BlockSpec((tc, R, LANE), lambda i: (i, 0, 0))
    par = pl.BlockSpec((R, LANE), lambda i: (0, 0))
    return pl.pallas_call(
        body, name="ssm_scan_fwd", grid=(S // tc,),
        in_specs=[blk, blk, par, par], out_specs=[blk, blk],
        out_shape=[jax.ShapeDtypeStruct((S, R, LANE), F32)] * 2,
        scratch_shapes=[pltpu.VMEM((2, R, LANE), F32)],
        compiler_params=pltpu.CompilerParams(dimension_semantics=("arbitrary",)),
    )(bu_re, bu_im, a_re, a_im)


def _scan_bwd(ds_re, ds_im, s_re, s_im, a_re, a_im):
    S, R, _ = ds_re.shape
    tc = _pick(S, 256, SUBLANE)
    nc = S // tc

    def body(dre, dim_, sre, sim, are, aim, lre, lim, dar, dai, carry):
        @pl.when(pl.program_id(0) == 0)
        def _():
            carry[...] = jnp.zeros_like(carry)
            dar[...] = jnp.zeros_like(dar)
            dai[...] = jnp.zeros_like(dai)

        ar, ai = are[...], aim[...]

        def step(tt, c):
            t = tc - 1 - tt
            lr, li, gr, gi = c
            sr, si = sre[t], sim[t]
            gr = gr + lr * sr + li * si
            gi = gi + li * sr - lr * si
            nlr = dre[t] + ar * lr + ai * li
            nli = dim_[t] + ar * li - ai * lr
            lre[t] = nlr
            lim[t] = nli
            return nlr, nli, gr, gi

        lr, li, gr, gi = lax.fori_loop(0, tc, step, (carry[0], carry[1], dar[...], dai[...]), unroll=8)
        carry[0] = lr
        carry[1] = li
        dar[...] = gr
        dai[...] = gi

    blk = pl.BlockSpec((tc, R, LANE), lambda i: (nc - 1 - i, 0, 0))
    par = pl.BlockSpec((R, LANE), lambda i: (0, 0))
    return pl.pallas_call(
        body, name="ssm_scan_bwd", grid=(nc,),
        in_specs=[blk, blk, blk, blk, par, par], out_specs=[blk, blk, par, par],
        out_shape=[jax.ShapeDtypeStruct((S, R, LANE), F32)] * 2 + [jax.ShapeDtypeStruct((R, LANE), F32)] * 2,
        scratch_shapes=[pltpu.VMEM((2, R, LANE), F32)],
        compiler_params=pltpu.CompilerParams(dimension_semantics=("arbitrary",)),
    )(ds_re, ds_im, s_re, s_im, a_re, a_im)


def _shift_down(z, k):
    t = lax.broadcasted_iota(I32, z.shape, 0)
    return jnp.where(t >= k, pltpu.roll(z, k, 0), 0.0)


def _shift_up(z, k):
    n = z.shape[0]
    t = lax.broadcasted_iota(I32, z.shape, 0)
    return jnp.where(t < n - k, pltpu.roll(z, n - k, 0), 0.0)


def _conv_fwd(proj, cw, cb):
    _, S, W = proj.shape
    ct = _pick(W, 256, LANE)

    def fn(bg, cg, val, w, b):
        z = cg * val
        conv = b + w[0:1] * _shift_down(z, 2) + w[1:2] * _shift_down(z, 1) + w[2:3] * z
        return bg * conv, conv

    sl = lambda s: (proj, (None, S, ct), lambda j, s=s: (s, 0, j))
    col = ((S, ct), lambda j: (0, j))
    return _ew("conv_fwd", (W // ct,),
               [sl(1), sl(2), sl(3), (cw, (3, ct), lambda j: (0, j)), (cb, (1, ct), lambda j: (0, j))],
               [((S, W), BF) + col, ((S, W), F32) + col], fn)


def _conv_bwd(dyb, proj, conv, cw):
    _, S, W = proj.shape
    ct = _pick(W, 256, LANE)

    def fn(dy, bg, cg, val, cv, w):
        z = cg * val
        z1, z2 = _shift_down(z, 1), _shift_down(z, 2)
        dconv = dy * bg
        dz = w[2:3] * dconv + w[1:2] * _shift_up(dconv, 1) + w[0:1] * _shift_up(dconv, 2)
        dw = jnp.concatenate([jnp.sum(dconv * z2, axis=0, keepdims=True), jnp.sum(dconv * z1, axis=0, keepdims=True),
                              jnp.sum(dconv * z, axis=0, keepdims=True)], axis=0)
        return dy * cv, dz * val, dz * cg, dw, jnp.sum(dconv, axis=0, keepdims=True)

    sl = lambda s: (proj, (None, S, ct), lambda j, s=s: (s, 0, j))
    col = ((S, ct), lambda j: (0, j))
    return _ew("conv_bwd", (W // ct,),
               [(dyb,) + col, sl(1), sl(2), sl(3), (conv,) + col, (cw, (3, ct), lambda j: (0, j))],
               [((S, W), BF) + col, ((S, W), BF) + col, ((S, W), BF) + col,
                ((3, W), F32, (3, ct), lambda j: (0, j)), ((1, W), F32, (1, ct), lambda j: (0, j))], fn)


def _plain(accs, sides):
    return accs


def kernel(x, ffn1_norm, ffn1_w_gate, ffn1_w_up, ffn1_w_down, mix_norm, w_in, ssm_lambda_re, ssm_lambda_im, ssm_log_dt, ssm_b_re, ssm_b_im, ssm_c_re, ssm_c_im, ssm_d, ssm_w_glu, ssm_b_glu, ssm_w_out, conv_w, conv_b, conv_w_out, w_o, ffn2_norm, ffn2_w_gate, ffn2_w_up, ffn2_w_down, final_norm, loss_target, m_ffn1_norm, m_ffn1_w_gate, m_ffn1_w_up, m_ffn1_w_down, m_mix_norm, m_w_in, m_ssm_lambda_re, m_ssm_lambda_im, m_ssm_log_dt, m_ssm_b_re, m_ssm_b_im, m_ssm_c_re, m_ssm_c_im, m_ssm_d, m_ssm_w_glu, m_ssm_b_glu, m_ssm_w_out, m_conv_w, m_conv_b, m_conv_w_out, m_w_o, m_ffn2_norm, m_ffn2_w_gate, m_ffn2_w_up, m_ffn2_w_down, m_final_norm, v_ffn1_norm, v_ffn1_w_gate, v_ffn1_w_up, v_ffn1_w_down, v_mix_norm, v_w_in, v_ssm_lambda_re, v_ssm_lambda_im, v_ssm_log_dt, v_ssm_b_re, v_ssm_b_im, v_ssm_c_re, v_ssm_c_im, v_ssm_d, v_ssm_w_glu, v_ssm_b_glu, v_ssm_w_out, v_conv_w, v_conv_b, v_conv_w_out, v_w_o, v_ffn2_norm, v_ffn2_w_gate, v_ffn2_w_up, v_ffn2_w_down, v_final_norm):
    P = dict(ffn1_norm=ffn1_norm, ffn1_w_gate=ffn1_w_gate, ffn1_w_up=ffn1_w_up, ffn1_w_down=ffn1_w_down, mix_norm=mix_norm, w_in=w_in, ssm_lambda_re=ssm_lambda_re, ssm_lambda_im=ssm_lambda_im, ssm_log_dt=ssm_log_dt, ssm_b_re=ssm_b_re, ssm_b_im=ssm_b_im, ssm_c_re=ssm_c_re, ssm_c_im=ssm_c_im, ssm_d=ssm_d, ssm_w_glu=ssm_w_glu, ssm_b_glu=ssm_b_glu, ssm_w_out=ssm_w_out, conv_w=conv_w, conv_b=conv_b, conv_w_out=conv_w_out, w_o=w_o, ffn2_norm=ffn2_norm, ffn2_w_gate=ffn2_w_gate, ffn2_w_up=ffn2_w_up, ffn2_w_down=ffn2_w_down, final_norm=final_norm)
    M = dict(ffn1_norm=m_ffn1_norm, ffn1_w_gate=m_ffn1_w_gate, ffn1_w_up=m_ffn1_w_up, ffn1_w_down=m_ffn1_w_down, mix_norm=m_mix_norm, w_in=m_w_in, ssm_lambda_re=m_ssm_lambda_re, ssm_lambda_im=m_ssm_lambda_im, ssm_log_dt=m_ssm_log_dt, ssm_b_re=m_ssm_b_re, ssm_b_im=m_ssm_b_im, ssm_c_re=m_ssm_c_re, ssm_c_im=m_ssm_c_im, ssm_d=m_ssm_d, ssm_w_glu=m_ssm_w_glu, ssm_b_glu=m_ssm_b_glu, ssm_w_out=m_ssm_w_out, conv_w=m_conv_w, conv_b=m_conv_b, conv_w_out=m_conv_w_out, w_o=m_w_o, ffn2_norm=m_ffn2_norm, ffn2_w_gate=m_ffn2_w_gate, ffn2_w_up=m_ffn2_w_up, ffn2_w_down=m_ffn2_w_down, final_norm=m_final_norm)
    V = dict(ffn1_norm=v_ffn1_norm, ffn1_w_gate=v_ffn1_w_gate, ffn1_w_up=v_ffn1_w_up, ffn1_w_down=v_ffn1_w_down, mix_norm=v_mix_norm, w_in=v_w_in, ssm_lambda_re=v_ssm_lambda_re, ssm_lambda_im=v_ssm_lambda_im, ssm_log_dt=v_ssm_log_dt, ssm_b_re=v_ssm_b_re, ssm_b_im=v_ssm_b_im, ssm_c_re=v_ssm_c_re, ssm_c_im=v_ssm_c_im, ssm_d=v_ssm_d, ssm_w_glu=v_ssm_w_glu, ssm_b_glu=v_ssm_b_glu, ssm_w_out=v_ssm_w_out, conv_w=v_conv_w, conv_b=v_conv_b, conv_w_out=v_conv_w_out, w_o=v_w_o, ffn2_norm=v_ffn2_norm, ffn2_w_gate=v_ffn2_w_gate, ffn2_w_up=v_ffn2_w_up, ffn2_w_down=v_ffn2_w_down, final_norm=v_final_norm)
    names = list(P)
    sharded = ["ffn1_w_gate", "ffn1_w_up", "ffn1_w_down", "w_in", "ssm_w_glu", "ssm_w_out", "conv_w_out", "w_o",
               "ffn2_w_gate", "ffn2_w_up", "ffn2_w_down"]
    replicated = [n for n in names if n not in sharded and n != "conv_w"]

    S, D = x.shape[1], x.shape[2]
    W = ssm_d.shape[0]
    Dc = D // NDEV
    G, N = ssm_lambda_re.shape
    rows = G * N // LANE
    xh = x.reshape(S, D)
    target = loss_target.reshape(S, D)
    xi, yi, ci = lax.axis_index("x"), lax.axis_index("y"), lax.axis_index("c")
    c_arr = jnp.reshape(ci, (1,)).astype(I32)
    q_arr = jnp.reshape(2 * xi + yi, (1,)).astype(I32)
    row = lambda v: v.reshape(1, -1)

    transposed = ("ffn1_w_gate", "ffn1_w_up", "ffn2_w_gate", "ffn2_w_up")
    local = lambda table, n: table[n].T if n in transposed else table[n]

    groups = [sharded[0:1], sharded[1:2], sharded[2:3], ["w_in", "conv_w"], sharded[4:8],
              sharded[8:9], sharded[9:10], sharded[10:11]]
    started = {}
    me = 4 * xi + 2 * yi + ci

    def gather_start(gi):
        hold = 0.0 * started[0][4][0, 0] if gi else 0.0
        srcs = [conv_w + hold if n == "conv_w" else (local(P, n) + hold).astype(BF) for n in groups[gi]]
        lands = [lax.dynamic_update_slice(lax.empty((NDEV,) + s.shape, s.dtype), s[None], (me,) + (0,) * s.ndim)
                 for s in srcs]
        after = [started[gi - 1][4]] if gi else []
        started[gi] = _split_start("gather_start_%d" % gi, srcs, lands, _gather_copies, 4, after)
        return started[gi][4]

    def gathered(gi, after):
        _, lands = _split_wait("gather_wait_%d" % gi, started[gi], _gather_copies, after)
        return _gather_forward("gather_forward_%d" % gi, lands)

    tm = _pick(S, M_TILE, SUBLANE)
    th = _pick(S, M_TILE // 2, SUBLANE)
    tk = _pick(D, K_TILE, LANE)
    ts = _pick(S, K_TILE, SUBLANE)
    tn = _pick(D, 1024, LANE)

    tokens = [gather_start(gi) for gi in range(len(groups))]
    u1, r1 = _rms_fwd("rms1", xh, row(ffn1_norm), deps=tokens)
    (wg1,) = gathered(0, u1)
    h1, ffn1_saved, wu1, wd1 = _ffn_fwd("ffn1", u1, xh, wg1, lambda after: gathered(1, after)[0],
                                        lambda after: gathered(2, after)[0])
    u2, r2 = _rms_fwd("rms2", h1, row(mix_norm))
    w_in_f, cw_f = gathered(3, u2)
    cw = jnp.transpose(cw_f, (1, 0, 2)).reshape(3, W)
    (proj,) = _mm(
        "in_proj", (NDEV, S // tm, 1, D // tk),
        [(u2, (tm, tk), lambda b, i, j, k: (i, k)), (w_in_f, (None, tk, W), lambda b, i, j, k: (b, k, 0))],
        [(0, 1, NN, 0)], [], [((NDEV, S, W), F32, (None, tm, W), lambda b, i, j, k: (b, i, 0))], _plain, [(tm, W)])

    ssm_in = (ssm_lambda_re, ssm_lambda_im, ssm_log_dt, ssm_b_re, ssm_b_im, ssm_c_re, ssm_c_im)
    (a_re, a_im, bd_re, bd_im, cd_re, cd_imn), ssm_vjp = jax.vjp(_ssm_params, *ssm_in)
    bd_re_b, bd_im_b, cd_re_b, cd_imn_b = (t.astype(BF) for t in (bd_re, bd_im, cd_re, cd_imn))
    v_f = proj[0]
    v_bf = v_f.astype(BF)
    bu_re3, bu_im3 = _ssm_spread("ssm_bu", v_bf, bd_re_b, bd_im_b, NN, rows)
    s_re3, s_im3 = _scan_fwd(bu_re3, bu_im3, a_re, a_im)

    def y0_epilogue(acc, v_tile, d_tile):
        y0 = acc + d_tile * v_tile
        return y0, _gelu(y0)

    y0, y1, s_re_b, s_im_b = _ssm_collect("ssm_y0", s_re3, s_im3, cd_re_b, cd_imn_b, NN, v_f, row(ssm_d),
                                          y0_epilogue, [F32, BF])

    tw = _pick(W, 512, LANE)
    w_glu_f, w_so, w_co, w_o_f = gathered(4, y1)
    w_glu_f = w_glu_f.reshape(W, W)
    w_o_f = w_o_f.reshape(D, D)

    def glu_epilogue(accs, sides):
        q = accs[0] + sides[1]
        return q, _gelu(sides[0]) * _sigmoid(q)

    q_pre, y2 = _mm("ssm_glu", (1, S // tm, W // tw, 1),
                    [(y1, (tm, W), lambda b, i, j, k: (i, 0)), (w_glu_f, (W, tw), lambda b, i, j, k: (0, j))],
                    [(0, 1, NN, 0)],
                    [(y0, (tm, tw), lambda b, i, j, k: (i, j)), (row(ssm_b_glu), (1, tw), lambda b, i, j, k: (0, j))],
                    [((S, W), F32, (tm, tw), lambda b, i, j, k: (i, j)), ((S, W), BF, (tm, tw), lambda b, i, j, k: (i, j))],
                    glu_epilogue, [(tm, tw)])

    yb, conv = _conv_fwd(proj, cw, row(conv_b))

    per = W // Dc
    ga_blk = (proj, (None, tm, Dc), lambda b, i, j, k: (4 + b // per, i, b % per))
    gb_blk = (proj, (None, tm, Dc), lambda b, i, j, k: (6 + b // per, i, b % per))
    dc_out = ((S, D), BF, (tm, Dc), lambda b, i, j, k: (i, b))

    def merge_epilogue(accs, sides):
        za, zb = accs
        return _sigmoid(sides[0]) * za + _sigmoid(sides[1]) * zb, za, zb

    merged, z_a, z_b = _mm(
        "mix_merge", (NDEV, S // tm, 1, 1),
        [(y2, (tm, W), lambda b, i, j, k: (i, 0)), (yb, (tm, W), lambda b, i, j, k: (i, 0)),
         (w_so, (None, W, Dc), lambda b, i, j, k: (b, 0, 0)), (w_co, (None, W, Dc), lambda b, i, j, k: (b, 0, 0))],
        [(0, 2, NN, 0), (1, 3, NN, 1)], [ga_blk, gb_blk], [dc_out, dc_out, dc_out], merge_epilogue, [(tm, Dc)] * 2)

    (h2,) = _mm("mix_out", (1, S // tm, D // tn, D // tk),
                [(merged, (tm, tk), lambda b, i, j, k: (i, k)), (w_o_f, (tk, tn), lambda b, i, j, k: (k, j))],
                [(0, 1, NN, 0)], [(h1, (tm, tn), lambda b, i, j, k: (i, j))],
                [((S, D), F32, (tm, tn), lambda b, i, j, k: (i, j))],
                lambda accs, sides: [sides[0] + accs[0]], [(tm, tn)])

    u3, r3 = _rms_fwd("rms3", h2, row(ffn2_norm))
    (wg2,) = gathered(5, u3)
    h3, ffn2_saved, wu2, wd2 = _ffn_fwd("ffn2", u3, h2, wg2, lambda after: gathered(6, after)[0],
                                        lambda after: gathered(7, after)[0])
    loss_vec, dh3, dh3_half, d_final_norm = _loss_head("loss_head", h3, row(final_norm), target)
    loss = lax.psum(loss_vec[0, 0], ("x", "y", "c"))
    loss_done = jnp.zeros((SUBLANE, LANE), F32) + loss

    grads, deltas, new_m, new_v = {}, {}, {}, {}

    def rs_sibling_start(tag, parts):
        lands = [lax.empty((4,) + p.shape[1:], p.dtype) for p in parts]
        return _split_start("rs_sibling_start_" + tag, parts, lands, _sibling_copies, 4)

    def rs_chips_start(tag, sibling_begun, after):
        parts, lands = _split_wait("rs_sibling_wait_" + tag, sibling_begun, _sibling_copies, after)
        sums = [_sum_sibling("rs_sum_%s_%d" % (tag, a), p, land, c_arr) for a, (p, land) in enumerate(zip(parts, lands))]
        lands2 = [lax.empty((3,) + sm.shape[1:], sm.dtype) for sm in sums]
        return _split_start("rs_chips_start_" + tag, sums, lands2, _chips_copies, 3)

    def rs_end(tag, group, begun, after):
        sums, lands2 = _split_wait("rs_chips_wait_" + tag, begun, _chips_copies, after)
        for n, sm, land2 in zip(group, sums, lands2):
            res = _finish_sharded("adamw_" + n, sm, land2, q_arr, local(P, n), local(M, n), local(V, n))
            grads[n], deltas[n], new_m[n], new_v[n] = [t.T if n in transposed else t for t in res]

    f2_dwd, f2_dwgu, f2_du = _ffn_bwd("ffn2", dh3_half, u3, ffn2_saved, wg2, wu2, wd2, deps=[loss_done])
    dwd2 = f2_dwd()
    dwg2, dwu2 = f2_dwgu()
    sib_ffn2 = rs_sibling_start("ffn2", [dwg2, dwu2, dwd2])
    du3 = f2_du(deps=[sib_ffn2[4]])
    dh2, dh2_b, d_ffn2_norm = _rms_bwd("rms3_bwd", du3, h2, r3, row(ffn2_norm), dh3, 1.0)
    rs_ffn2 = rs_chips_start("ffn2", sib_ffn2, dh2)

    dg_out = ((2, S, W), BF, (None, tm, Dc), lambda b, i, j, k: (j // per, i, j % per))
    ga_blk2 = (proj, (None, tm, Dc), lambda b, i, j, k: (4 + j // per, i, j % per))
    gb_blk2 = (proj, (None, tm, Dc), lambda b, i, j, k: (6 + j // per, i, j % per))
    dcj = lambda arr: (arr, (tm, Dc), lambda b, i, j, k: (i, j))
    dcj_out = ((S, D), BF, (tm, Dc), lambda b, i, j, k: (i, j))

    def dmerge_epilogue(accs, sides):
        dm = accs[0]
        sa, sb = _sigmoid(sides[0]), _sigmoid(sides[1])
        za, zb = sides[2].astype(F32), sides[3].astype(F32)
        return dm * sa, dm * sb, dm * za * sa * (1.0 - sa), dm * zb * sb * (1.0 - sb)

    dz_a, dz_b, dga, dgb = _mm(
        "mix_out_dx", (1, S // tm, NDEV, D // tk),
        [(dh2_b, (tm, tk), lambda b, i, j, k: (i, k)), (w_o_f, (Dc, tk), lambda b, i, j, k: (j, k))],
        [(0, 1, NT, 0)], [ga_blk2, gb_blk2, dcj(z_a), dcj(z_b)], [dcj_out, dcj_out, dg_out, dg_out],
        dmerge_epilogue, [(tm, Dc)], deps=[rs_ffn2[4]])

    td = _pick(D, M_TILE, LANE)
    (dw_o,) = _mm("mix_out_dw", (1, D // td, D // tn, S // ts),
                  [(merged, (ts, td), lambda b, i, j, k: (k, i)), (dh2_b, (ts, tn), lambda b, i, j, k: (k, j))],
                  [(0, 1, TN, 0)], [], [((D, D), BF, (td, tn), lambda b, i, j, k: (i, j))], _plain, [(td, tn)])

    wout = ((NDEV, W, Dc), BF, (None, W, Dc), lambda b, i, j, k: (b, 0, 0))
    dw_so, dw_co = _mm(
        "mix_merge_dw", (NDEV, 1, 1, S // ts),
        [(y2, (ts, W), lambda b, i, j, k: (k, 0)), (yb, (ts, W), lambda b, i, j, k: (k, 0)),
         (dz_a, (ts, Dc), lambda b, i, j, k: (k, b)), (dz_b, (ts, Dc), lambda b, i, j, k: (k, b))],
        [(0, 2, TN, 0), (1, 3, TN, 1)], [], [wout, wout], _plain, [(W, Dc)] * 2)

    def dglu_epilogue(accs, sides):
        dy2, dyb = accs
        sq = _sigmoid(sides[1])
        return dy2 * _gelu(sides[0]) * sq * (1.0 - sq), dy2 * sq, dyb

    full_w = lambda arr: (arr, (th, W), lambda b, i, j, k: (i, 0))
    full_w_out = lambda dt: ((S, W), dt, (th, W), lambda b, i, j, k: (i, 0))
    dq, dy1p, dyb = _mm(
        "mix_merge_dx", (1, S // th, 1, NDEV),
        [(dz_a, (th, Dc), lambda b, i, j, k: (i, k)), (dz_b, (th, Dc), lambda b, i, j, k: (i, k)),
         (w_so, (None, W, Dc), lambda b, i, j, k: (k, 0, 0)), (w_co, (None, W, Dc), lambda b, i, j, k: (k, 0, 0))],
        [(0, 2, NT, 0), (1, 3, NT, 1)], [full_w(y0), full_w(q_pre)], [full_w_out(BF), full_w_out(F32), full_w_out(F32)],
        dglu_epilogue, [(th, W)] * 2)

    def dy0_epilogue(accs, sides):
        dy0 = (sides[0] + accs[0]) * _gelu_grad(sides[1])
        return dy0, dy0

    wj = lambda arr: (arr, (tm, tw), lambda b, i, j, k: (i, j))
    dy0, dy0_b = _mm("ssm_glu_dx", (1, S // tm, W // tw, 1),
                     [(dq, (tm, W), lambda b, i, j, k: (i, 0)), (w_glu_f, (tw, W), lambda b, i, j, k: (j, 0))],
                     [(0, 1, NT, 0)], [wj(dy1p), wj(y0)],
                     [((S, W), F32, (tm, tw), lambda b, i, j, k: (i, j)), ((S, W), BF, (tm, tw), lambda b, i, j, k: (i, j))],
                     dy0_epilogue, [(tm, tw)])

    (dw_glu,) = _mm("ssm_glu_dw", (1, W // tw, 1, S // ts),
                    [(y1, (ts, tw), lambda b, i, j, k: (k, i)), (dq, (ts, W), lambda b, i, j, k: (k, 0))],
                    [(0, 1, TN, 0)], [], [((W, W), BF, (tw, W), lambda b, i, j, k: (i, 0))], _plain, [(tw, W)])

    tr = _pick(S, 256, SUBLANE)
    rw = ((tr, W), lambda i: (i, 0))
    vec_w = ((1, W), F32, (1, W), lambda i: (0, 0))
    d_b_glu, d_ssm_d = _ew(
        "ssm_colsums", (S // tr,), [(dq,) + rw, (dy0,) + rw, (proj, (None, tr, W), lambda i: (0, i, 0))],
        [vec_w, vec_w],
        lambda dqv, dyv, vv: (jnp.sum(dqv.astype(F32), axis=0, keepdims=True), jnp.sum(dyv * vv, axis=0, keepdims=True)),
        acc=(0, 1))

    ds_re3, ds_im3 = _ssm_spread("ssm_ds", dy0_b, cd_re_b, cd_imn_b, NT, rows)
    lam_re3, lam_im3, da_re, da_im = _scan_bwd(ds_re3, ds_im3, s_re3, s_im3, a_re, a_im)
    dv, lam_re_b, lam_im_b = _ssm_collect("ssm_dv", lam_re3, lam_im3, bd_re_b, bd_im_b, NT, dy0, row(ssm_d),
                                          lambda acc, dy_tile, d_tile: [acc + dy_tile * d_tile], [BF])
    tiles, tch, tst = bd_re.shape
    tok_ch = lambda arr: (arr, (ts, tch), lambda b, i, j, k: (k, b))
    tok_st = lambda arr: (arr, (ts, tst), lambda b, i, j, k: (k, b))
    bd_out = ((tiles, tch, tst), F32, (None, tch, tst), lambda b, i, j, k: (b, 0, 0))
    cd_out = ((tiles, tst, tch), F32, (None, tst, tch), lambda b, i, j, k: (b, 0, 0))
    dbd_re, dbd_im = _mm("ssm_dbd", (tiles, 1, 1, S // ts), [tok_ch(v_bf), tok_st(lam_re_b), tok_st(lam_im_b)],
                         [(0, 1, TN, 0), (0, 2, TN, 1)], [], [bd_out, bd_out], _plain, [(tch, tst)] * 2)
    dcd_re, dcd_imn = _mm("ssm_dcd", (tiles, 1, 1, S // ts), [tok_st(s_re_b), tok_st(s_im_b), tok_ch(dy0_b)],
                          [(0, 2, TN, 0), (1, 2, TN, 1)], [], [cd_out, cd_out], _plain, [(tst, tch)] * 2)
    d_ssm = ssm_vjp((da_re, da_im, dbd_re, dbd_im, dcd_re, dcd_imn))

    dbg, dcg, dval, d_conv_w_full, d_conv_b = _conv_bwd(dyb, proj, conv, cw)
    dproj = jnp.concatenate([dv[None], dbg[None], dcg[None], dval[None], dga, dgb], axis=0)

    (dw_in,) = _mm("in_proj_dw", (NDEV, D // td, 1, S // ts),
                   [(u2, (ts, td), lambda b, i, j, k: (k, i)), (dproj, (None, ts, W), lambda b, i, j, k: (b, k, 0))],
                   [(0, 1, TN, 0)], [], [((NDEV, D, W), BF, (None, td, W), lambda b, i, j, k: (b, i, 0))],
                   _plain, [(td, W)])
    sib_mixer = rs_sibling_start(
        "mixer", [dw_in, dw_glu.reshape(NDEV, W // NDEV, W), dw_so, dw_co, dw_o.reshape(NDEV, Dc, D)])
    (du2,) = _mm("in_proj_dx", (1, S // tm, D // tn, NDEV // K_SHARDS),
                 [(dproj, (K_SHARDS, tm, W), lambda b, i, j, k: (k, i, 0)),
                  (w_in_f, (K_SHARDS, tn, W), lambda b, i, j, k: (k, j, 0))],
                 [(0, 1, NT, 0)], [], [((S, D), F32, (tm, tn), lambda b, i, j, k: (i, j))], _plain, [(tm, tn)],
                 deps=[sib_mixer[4]])
    dh1, dh1_half, d_mix_norm = _rms_bwd("rms2_bwd", du2, h1, r2, row(mix_norm), dh2, 0.5)
    rs_mixer = rs_chips_start("mixer", sib_mixer, dh1)

    small = dict(mix_norm=d_mix_norm, ffn2_norm=d_ffn2_norm, final_norm=d_final_norm,
                 ssm_lambda_re=d_ssm[0], ssm_lambda_im=d_ssm[1], ssm_log_dt=d_ssm[2], ssm_b_re=d_ssm[3],
                 ssm_b_im=d_ssm[4], ssm_c_re=d_ssm[5], ssm_c_im=d_ssm[6], ssm_d=d_ssm_d, ssm_b_glu=d_b_glu,
                 conv_b=d_conv_b)
    replicated = [n for n in replicated if n != "ffn1_norm"] + ["ffn1_norm"]
    tile = SUBLANE * LANE

    def as_rows(p):
        flat = p.reshape(-1).astype(F32)
        return jnp.pad(flat, (0, -flat.shape[0] % tile)).reshape(-1, LANE)

    def pack(parts):
        return jnp.concatenate([as_rows(p) for p in parts], axis=0)

    cw_zero = jnp.zeros_like(d_conv_w_full)
    early_pk = pack([small[n] for n in replicated[:-1]] + [d_conv_w_full])
    early_land = lax.dynamic_update_slice(jnp.zeros((NDEV,) + early_pk.shape, F32), early_pk[None], (me, 0, 0))
    small_begun = _split_start("gather_small_start", [early_pk], [early_land], _everyone_copies, NDEV - 1)

    f1_dwd, f1_dwgu, f1_du = _ffn_bwd("ffn1", dh1_half, u1, ffn1_saved, wg1, wu1, wd1,
                                      deps=[rs_mixer[4], small_begun[4]])
    du1 = f1_du()
    dx, _, d_ffn1_norm = _rms_bwd("rms1_bwd", du1, xh, r1, row(ffn1_norm), dh1, 1.0)
    dwd1 = f1_dwd(deps=[d_ffn1_norm])
    late = d_ffn1_norm + 0.0 * dwd1[0, :1, :1].astype(F32)
    (late_all,) = _all_gather("gather_ffn1_norm_grad", [late.reshape(-1, LANE)])
    rs_ffn1_down = rs_chips_start("ffn1_down", rs_sibling_start("ffn1_down", [dwd1]), late_all)
    dwg1, dwu1 = f1_dwgu(deps=[rs_ffn1_down[4]])
    rs_ffn1_gate_up = rs_chips_start("ffn1_gate_up", rs_sibling_start("ffn1_gate_up", [dwg1, dwu1]), None)
    rs_end("ffn2", sharded[8:11], rs_ffn2, rs_ffn1_gate_up[4])
    rs_end("mixer", sharded[3:8], rs_mixer, [grads[n] for n in sharded[8:11]])
    _, (early_all,) = _split_wait("gather_small_wait", small_begun, _everyone_copies, [grads[n] for n in sharded[3:8]])
    small_all = jnp.concatenate([early_all, late_all], axis=1)
    g_pk, d_pk, m_pk, v_pk = _finish_replicated(
        "adamw_replicated", small_all, pack([P[n] for n in replicated[:-1]] + [cw_zero, P["ffn1_norm"]]),
        pack([M[n] for n in replicated[:-1]] + [cw_zero, M["ffn1_norm"]]),
        pack([V[n] for n in replicated[:-1]] + [cw_zero + 1.0, V["ffn1_norm"]]))

    def unpack(pk, r0, like):
        nr = -(-like.size // tile) * SUBLANE
        return pk[r0:r0 + nr].reshape(-1)[:like.size].reshape(like.shape), r0 + nr

    r0 = 0
    for n in replicated[:-1] + ["conv_w", "ffn1_norm"]:
        like = d_conv_w_full if n == "conv_w" else P[n]
        for store, pk in ((grads, g_pk), (deltas, d_pk), (new_m, m_pk), (new_v, v_pk)):
            store[n], r1 = unpack(pk, r0, like)
        r0 = r1
    g_cw_full = grads["conv_w"]
    cwl = conv_w.shape[1]
    g_cw = lax.dynamic_slice_in_dim(g_cw_full, me * cwl, cwl, axis=1)
    full3 = ((3, cwl), lambda i: (0, 0))
    grads["conv_w"], deltas["conv_w"], new_m["conv_w"], new_v["conv_w"] = _ew(
        "adamw_conv_w", (1,), [(g_cw,) + full3, (conv_w,) + full3, (m_conv_w,) + full3, (v_conv_w,) + full3],
        [((3, cwl), F32) + full3] * 4, lambda g, w, m, v: (g,) + _adamw(w, g, m, v))
    rs_end("ffn1_down", sharded[2:3], rs_ffn1_down, [g_pk, grads["conv_w"]])
    rs_end("ffn1_gate_up", sharded[0:2], rs_ffn1_gate_up, grads["ffn1_w_down"])

    return (loss, dx.reshape(x.shape), *[grads[n] for n in names], *[deltas[n] for n in names],
            *[new_m[n] for n in names], *[new_v[n] for n in names])
```

```python
import math

import jax
import jax.numpy as jnp
from jax import lax
from jax.experimental import pallas as pl
from jax.experimental.pallas import tpu as pltpu

F32 = jnp.float32
BF = jnp.bfloat16
I32 = jnp.int32
MESH = pl.DeviceIdType.MESH
LANE = 128
SUBLANE = 8
NDEV = 8
EW_BLOCK = 256 * 1024
M_TILE = 1024
K_TILE = 2048
K_SHARDS = 2
DMA_CHUNKS = 4
EPS = 1e-6
ADAM_LR, ADAM_B1, ADAM_B2, ADAM_EPS, ADAM_WD, ADAM_STEP = 0.001, 0.9, 0.999, 1e-08, 0.01, 10
NN = ((1,), (0,))
NT = ((1,), (1,))
TN = ((0,), (0,))
HBM = pl.BlockSpec(memory_space=pltpu.HBM)


def _pick(n, pref, mult):
    t = min(pref, n)
    t -= t % mult
    while t >= mult:
        if n % t == 0:
            return t
        t -= mult
    return n


def _sigmoid(x):
    return 1.0 / (1.0 + jnp.exp(-x))


_GELU_C = math.sqrt(2.0 / math.pi)


def _gelu(x):
    return 0.5 * x * (1.0 + jnp.tanh(_GELU_C * (x + 0.044715 * x * x * x)))


def _gelu_grad(x):
    t = jnp.tanh(_GELU_C * (x + 0.044715 * x * x * x))
    return 0.5 * (1.0 + t) + 0.5 * x * (1.0 - t * t) * _GELU_C * (1.0 + 3.0 * 0.044715 * x * x)


def _dep_specs(deps, rank):
    return [(d, d.shape, lambda *_, nd=d.ndim: (0,) * nd) for d in deps]


def _mm(name, grid, ops, pairs, sides, outs, epilogue, acc_shapes, deps=()):
    nk = grid[-1]
    n_ops, n_sides, n_outs = len(ops), len(sides), len(outs)
    dep_specs = _dep_specs(deps, len(grid))
    n_deps = len(dep_specs)

    def body(*refs):
        op_refs = refs[:n_ops]
        side_refs = refs[n_ops:n_ops + n_sides]
        out_refs = refs[n_ops + n_sides + n_deps:n_ops + n_sides + n_deps + n_outs]
        acc_refs = refs[n_ops + n_sides + n_deps + n_outs:]

        def partials():
            res = [None] * len(acc_shapes)
            for ia, ib, dims, ai in pairs:
                a_ref, b_ref = op_refs[ia], op_refs[ib]
                for s in range(a_ref.shape[0] if len(a_ref.shape) == 3 else 1):
                    a, b = (a_ref[s], b_ref[s]) if len(a_ref.shape) == 3 else (a_ref[...], b_ref[...])
                    p = lax.dot_general(a, b, (dims, ((), ())), preferred_element_type=F32)
                    res[ai] = p if res[ai] is None else res[ai] + p
            return res

        def finish(accs):
            vals = epilogue(accs, [s[...] for s in side_refs])
            for o, v in zip(out_refs, vals):
                o[...] = v.astype(o.dtype)

        if nk == 1:
            finish(partials())
        else:
            k = pl.program_id(len(grid) - 1)

            @pl.when(k == 0)
            def _():
                for a, p in zip(acc_refs, partials()):
                    a[...] = p

            @pl.when(k > 0)
            def _():
                for a, p in zip(acc_refs, partials()):
                    a[...] += p

            @pl.when(k == nk - 1)
            def _():
                finish([a[...] for a in acc_refs])

    return pl.pallas_call(
        body, name=name, grid=grid,
        in_specs=[pl.BlockSpec(b, m) for (_, b, m) in list(ops) + list(sides) + dep_specs],
        out_specs=[pl.BlockSpec(b, m) for (_, _, b, m) in outs],
        out_shape=[jax.ShapeDtypeStruct(s, d) for (s, d, _, _) in outs],
        scratch_shapes=[pltpu.VMEM(s, F32) for s in acc_shapes] if nk > 1 else [],
        compiler_params=pltpu.CompilerParams(
            dimension_semantics=("parallel",) * (len(grid) - 1) + ("arbitrary",)),
    )(*[a for (a, _, _) in list(ops) + list(sides) + dep_specs])


def _ew(name, grid, ins, outs, fn, acc=(), deps=()):
    n_in = len(ins)
    dep_specs = _dep_specs(deps, len(grid))

    def body(*refs):
        vals = fn(*[r[...] for r in refs[:n_in]])
        first = pl.program_id(0) == 0
        for idx, (o, v) in enumerate(zip(refs[n_in + len(dep_specs):], vals)):
            if idx in acc:
                @pl.when(first)
                def _(o=o, v=v):
                    o[...] = v.astype(o.dtype)

                @pl.when(jnp.logical_not(first))
                def _(o=o, v=v):
                    o[...] += v.astype(o.dtype)
            else:
                o[...] = v.astype(o.dtype)

    return pl.pallas_call(
        body, name=name, grid=grid,
        in_specs=[pl.BlockSpec(b, m) for (_, b, m) in list(ins) + dep_specs],
        out_specs=[pl.BlockSpec(b, m) for (_, _, b, m) in outs],
        out_shape=[jax.ShapeDtypeStruct(s, d) for (s, d, _, _) in outs],
        compiler_params=pltpu.CompilerParams(
            dimension_semantics=(("arbitrary",) if acc else ("parallel",)) * len(grid)),
    )(*[a for (a, _, _) in list(ins) + dep_specs])


def _position():
    x, y, c = lax.axis_index("x"), lax.axis_index("y"), lax.axis_index("c")
    chips = [(1 - x, y), (x, 1 - y), (1 - x, 1 - y)]
    return x, y, c, chips


def _all_gather(name, shards):
    n = len(shards)

    def body(*refs):
        xs, outs = refs[:n], refs[n:2 * n]
        send_sems, recv_sems, local_sems = refs[2 * n:]
        x, y, c, chips = _position()
        me, sibling = (x, y, c), (x, y, 1 - c)

        def copy(a, k, block, to, src=None):
            dst = outs[a].at[4 * block[0] + 2 * block[1] + block[2]]
            return pltpu.make_async_remote_copy(
                src_ref=dst if src is None else src, dst_ref=dst,
                send_sem=send_sems.at[a, k], recv_sem=recv_sems.at[a, k],
                device_id=to, device_id_type=MESH)

        mine = [pltpu.make_async_copy(xs[a], outs[a].at[4 * x + 2 * y + c], local_sems.at[a]) for a in range(n)]
        for cp in mine:
            cp.start()
        first = []
        for a in range(n):
            first.append(copy(a, 0, me, sibling, src=xs[a]))
            first += [copy(a, 1 + j, me, (*chip, c), src=xs[a]) for j, chip in enumerate(chips)]
        for cp in first:
            cp.start()
        passed = []
        for a in range(n):
            for j, chip in enumerate(chips):
                copy(a, 1 + j, (*chip, c), me).wait_recv()
                cp = copy(a, 4 + j, (*chip, c), sibling)
                cp.start()
                passed.append(cp)
        for a in range(n):
            copy(a, 0, sibling, me).wait_recv()
            for j, chip in enumerate(chips):
                copy(a, 4 + j, (*chip, 1 - c), me).wait_recv()
        for cp in first + passed:
            cp.wait_send()
        for cp in mine:
            cp.wait()

    return pl.pallas_call(
        body, name=name,
        out_shape=[jax.ShapeDtypeStruct((NDEV,) + s.shape, s.dtype) for s in shards],
        in_specs=[HBM] * n, out_specs=[HBM] * n,
        scratch_shapes=[pltpu.SemaphoreType.DMA((n, 7)), pltpu.SemaphoreType.DMA((n, 7)),
                        pltpu.SemaphoreType.DMA((n,))],
    )(*shards)


SEM = pl.BlockSpec(memory_space=pltpu.SEMAPHORE)
EFFECT = pltpu.SideEffectType.DATAFLOW_SIDE_EFFECTING


def _in_hbm(v):
    return pltpu.with_memory_space_constraint(v, pltpu.HBM)


def _split_start(name, srcs, lands, make_copies, n_per, after=()):
    n, nb = len(srcs), len(srcs) + len(lands)
    n_sems = len(lands) * n_per
    after = list(after)

    def body(*refs):
        send_sems, recv_sems = refs[nb + len(after)], refs[nb + len(after) + 1]
        for cp in make_copies(refs[:n], refs[n:nb], send_sems, recv_sems):
            cp.start()
        refs[-1][...] = jnp.zeros_like(refs[-1])

    outs = pl.pallas_call(
        body, name=name,
        out_shape=(pltpu.SemaphoreType.DMA((n_sems,)), pltpu.SemaphoreType.DMA((n_sems,)),
                   *[pltpu.HBM(v.shape, v.dtype) for v in list(srcs) + list(lands)],
                   jax.ShapeDtypeStruct((SUBLANE, LANE), F32)),
        in_specs=[HBM] * nb + [pl.BlockSpec(memory_space=pl.ANY)] * len(after),
        out_specs=(SEM, SEM, *[HBM] * nb, pl.BlockSpec(memory_space=pltpu.VMEM)),
        input_output_aliases={i: 2 + i for i in range(nb)},
        compiler_params=pltpu.CompilerParams(has_side_effects=EFFECT),
    )(*[_in_hbm(v) for v in list(srcs) + list(lands)], *after)
    return outs[0], outs[1], list(outs[2:2 + n]), list(outs[2 + n:2 + nb]), outs[-1]


def _split_wait(name, started, make_copies, after):
    send_sems, recv_sems, srcs, lands, _ = started
    n, nb = len(srcs), len(srcs) + len(lands)

    def body(*refs):
        for cp in make_copies(refs[:n], refs[n:nb], refs[nb], refs[nb + 1]):
            cp.wait_send()
            cp.wait_recv()

    order = [] if after is None else list(after) if isinstance(after, (list, tuple)) else [after]
    outs = pl.pallas_call(
        body, name=name,
        out_shape=tuple(pltpu.HBM(v.shape, v.dtype) for v in srcs + lands),
        in_specs=[HBM] * nb + [SEM, SEM] + [pl.BlockSpec(memory_space=pl.ANY)] * len(order),
        out_specs=tuple([HBM] * nb),
        input_output_aliases={i: i for i in range(nb)},
        compiler_params=pltpu.CompilerParams(has_side_effects=EFFECT),
    )(*srcs, *lands, send_sems, recv_sems, *order)
    return list(outs[:n]), list(outs[n:])


def _gather_first_copies(xs, lands, send_sems, recv_sems):
    x, y, c, _ = _position()
    copies = []
    for a in range(len(xs)):
        for k, peer in enumerate([(x, y, 1 - c), (1 - x, y, c), (x, 1 - y, c)]):
            copies.append(pltpu.make_async_remote_copy(
                src_ref=xs[a], dst_ref=lands[a].at[4 * x + 2 * y + c],
                send_sem=send_sems.at[3 * a + k], recv_sem=recv_sems.at[3 * a + k], device_id=peer, device_id_type=MESH))
    return copies


def _gather_second_copies(xs, lands, send_sems, recv_sems):
    x, y, c, _ = _position()
    copies = []
    for a in range(len(lands)):
        rows = lands[a].shape[1]
        unit = SUBLANE * (4 // jnp.dtype(lands[a].dtype).itemsize)
        half = rows // 2 // unit * unit or rows
        parts = [((1 - x, y), (x, 1 - y), pl.ds(0, half))]
        if half < rows:
            parts.append(((x, 1 - y), (1 - x, y), pl.ds(half, rows - half)))
        for k, (block, to, rs) in enumerate(parts):
            ref = lands[a].at[4 * block[0] + 2 * block[1] + c, rs]
            copies.append(pltpu.make_async_remote_copy(
                src_ref=ref, dst_ref=ref, send_sem=send_sems.at[2 * a + k], recv_sem=recv_sems.at[2 * a + k],
                device_id=(*to, c), device_id_type=MESH))
    return copies


def _chips_copies(ps, lands, send_sems, recv_sems):
    x, y, c, chips = _position()
    copies = []
    for a in range(len(ps)):
        for j, chip in enumerate(chips):
            copies.append(pltpu.make_async_remote_copy(
                src_ref=ps[a].at[2 * chip[0] + chip[1]], dst_ref=lands[a].at[j],
                send_sem=send_sems.at[3 * a + j], recv_sem=recv_sems.at[3 * a + j], device_id=(*chip, c),
                device_id_type=MESH))
    return copies


def _sibling_copies(gs, lands, send_sems, recv_sems):
    x, y, c, _ = _position()
    copies = []
    for a in range(len(gs)):
        for q in range(4):
            copies.append(pltpu.make_async_remote_copy(
                src_ref=gs[a].at[2 * q + 1 - c], dst_ref=lands[a].at[q],
                send_sem=send_sems.at[4 * a + q], recv_sem=recv_sems.at[4 * a + q],
                device_id=(x, y, 1 - c), device_id_type=MESH))
    return copies


def _everyone_copies(xs, lands, send_sems, recv_sems):
    x, y, c, _ = _position()
    flip = lambda v, bit: 1 - v if bit else v
    copies = []
    for a in range(len(xs)):
        for k in range(1, NDEV):
            copies.append(pltpu.make_async_remote_copy(
                src_ref=xs[a], dst_ref=lands[a].at[4 * x + 2 * y + c],
                send_sem=send_sems.at[7 * a + k - 1], recv_sem=recv_sems.at[7 * a + k - 1],
                device_id=(flip(x, k & 4), flip(y, k & 2), flip(c, k & 1)), device_id_type=MESH))
    return copies


def _row_chunks(rows, dtype):
    unit = SUBLANE * (4 // jnp.dtype(dtype).itemsize)
    units = rows // unit
    if rows % unit or units < 2:
        return [(0, rows)]
    k = min(DMA_CHUNKS, units)
    sizes = [(units // k + (1 if i < units % k else 0)) * unit for i in range(k)]
    return [(sum(sizes[:i]), sz) for i, sz in enumerate(sizes)]


def _gather_forward(name, lands):
    n = len(lands)

    def body(*refs):
        ins, outs = refs[:n], refs[n:2 * n]
        send_sems, recv_sems = refs[2 * n:]
        x, y, c, chips = _position()
        whole, chunks = [], []
        for a in range(n):
            rows = _row_chunks(ins[a].shape[1], ins[a].dtype)
            for j, chip in enumerate(chips):
                slot = 4 * chip[0] + 2 * chip[1]

                def to_sibling(src, dst):
                    return pltpu.make_async_remote_copy(
                        src_ref=src, dst_ref=dst, send_sem=send_sems.at[a, j], recv_sem=recv_sems.at[a, j],
                        device_id=(x, y, 1 - c), device_id_type=MESH)

                whole.append(to_sibling(ins[a].at[slot + c], outs[a].at[slot + 1 - c]))
                chunks += [to_sibling(ins[a].at[slot + c, pl.ds(r0, nr)], outs[a].at[slot + c, pl.ds(r0, nr)])
                           for r0, nr in rows]
        for cp in chunks:
            cp.start()
        for cp in whole:
            cp.wait()

    return pl.pallas_call(
        body, name=name,
        out_shape=[jax.ShapeDtypeStruct(l.shape, l.dtype) for l in lands],
        in_specs=[HBM] * n, out_specs=[HBM] * n,
        input_output_aliases={a: a for a in range(n)},
        scratch_shapes=[pltpu.SemaphoreType.DMA((n, 3)), pltpu.SemaphoreType.DMA((n, 3))],
    )(*lands)


def _sum_sibling(name, g, land, c_arr):
    _, R, C = g.shape
    tr = _pick(R, 512, SUBLANE)

    def body(c_ref, g_ref, l_ref, o_ref):
        o_ref[...] = (g_ref[...].astype(F32) + l_ref[...].astype(F32)).astype(o_ref.dtype)

    return pl.pallas_call(
        body, name=name,
        grid_spec=pltpu.PrefetchScalarGridSpec(
            num_scalar_prefetch=1, grid=(4, R // tr),
            in_specs=[pl.BlockSpec((None, tr, C), lambda q, i, cr: (2 * q + cr[0], i, 0)),
                      pl.BlockSpec((None, tr, C), lambda q, i, cr: (q, i, 0))],
            out_specs=pl.BlockSpec((None, tr, C), lambda q, i, cr: (q, i, 0))),
        out_shape=jax.ShapeDtypeStruct((4, R, C), g.dtype),
        compiler_params=pltpu.CompilerParams(dimension_semantics=("parallel", "parallel")),
    )(c_arr, g, land)


def _adamw(w, g, m, v):
    m = ADAM_B1 * m + (1.0 - ADAM_B1) * g
    v = ADAM_B2 * v + (1.0 - ADAM_B2) * (g * g)
    m_hat = m / (1.0 - ADAM_B1 ** ADAM_STEP)
    v_hat = v / (1.0 - ADAM_B2 ** ADAM_STEP)
    delta = -ADAM_LR * (m_hat / (jnp.sqrt(v_hat) + ADAM_EPS) + ADAM_WD * w)
    return delta, m, v


def _finish_sharded(name, sums, land, q_arr, w, m, v):
    R, C = w.shape
    tr = _pick(R, 512, SUBLANE)
    tc = _pick(C, max(LANE, EW_BLOCK // tr), LANE)

    def body(q_ref, p_ref, l_ref, w_ref, m_ref, v_ref, g_out, d_out, m_out, v_out):
        g = p_ref[...].astype(F32)
        for j in range(3):
            g = g + l_ref[j].astype(F32)
        d, mn, vn = _adamw(w_ref[...], g, m_ref[...], v_ref[...])
        g_out[...] = g
        d_out[...] = d
        m_out[...] = mn
        v_out[...] = vn

    blk = pl.BlockSpec((tr, tc), lambda i, j, qr: (i, j))
    return pl.pallas_call(
        body, name=name,
        grid_spec=pltpu.PrefetchScalarGridSpec(
            num_scalar_prefetch=1, grid=(R // tr, C // tc),
            in_specs=[pl.BlockSpec((None, tr, tc), lambda i, j, qr: (qr[0], i, j)),
                      pl.BlockSpec((3, tr, tc), lambda i, j, qr: (0, i, j)), blk, blk, blk],
            out_specs=[blk] * 4),
        out_shape=[jax.ShapeDtypeStruct((R, C), F32)] * 4,
        compiler_params=pltpu.CompilerParams(dimension_semantics=("parallel", "parallel")),
    )(q_arr, sums, land, w, m, v)


def _finish_replicated(name, gathered, w, m, v):
    _, R, C = gathered.shape
    tr = _pick(R, 256, SUBLANE)

    def fn(gv, wv, mv, vv):
        g = gv[0]
        for d in range(1, NDEV):
            g = g + gv[d]
        dl, mn, vn = _adamw(wv, g, mv, vv)
        return g, dl, mn, vn

    row = ((tr, C), lambda i: (i, 0))
    return _ew(name, (R // tr,),
               [(gathered, (NDEV, tr, C), lambda i: (0, i, 0)), (w,) + row, (m,) + row, (v,) + row],
               [((R, C), F32) + row] * 4, fn)


def _rms_fwd(name, h, g, deps=()):
    S, D = h.shape
    tr = _pick(S, 256, SUBLANE)

    def fn(hv, gv):
        r = lax.rsqrt(jnp.mean(hv * hv, axis=-1, keepdims=True) + EPS)
        return hv * r * gv, r

    return _ew(name, (S // tr,),
               [(h, (tr, D), lambda i: (i, 0)), (g, (1, D), lambda i: (0, 0))],
               [((S, D), BF, (tr, D), lambda i: (i, 0)), ((S, 1), F32, (tr, 1), lambda i: (i, 0))], fn, deps=deps)


def _rms_bwd(name, du, h, r, g, dres, scale):
    S, D = h.shape
    tr = _pick(S, 256, SUBLANE)

    def fn(duv, hv, rv, gv, drv):
        xn = hv * rv
        dxn = duv * gv
        dh = drv + rv * (dxn - xn * jnp.mean(dxn * xn, axis=-1, keepdims=True))
        return dh, scale * dh, jnp.sum(duv * xn, axis=0, keepdims=True)

    row = ((tr, D), lambda i: (i, 0))
    return _ew(name, (S // tr,),
               [(du,) + row, (h,) + row, (r, (tr, 1), lambda i: (i, 0)), (g, (1, D), lambda i: (0, 0)), (dres,) + row],
               [((S, D), F32) + row, ((S, D), BF) + row, ((1, D), F32, (1, D), lambda i: (0, 0))], fn, acc=(2,))


def _loss_head(name, h, g, target):
    S, D = h.shape
    tr = _pick(S, 256, SUBLANE)

    def fn(hv, gv, tv):
        r = lax.rsqrt(jnp.mean(hv * hv, axis=-1, keepdims=True) + EPS)
        xn = hv * r
        diff = xn * gv - tv
        loss = 0.5 * jnp.sum(jnp.mean(diff * diff, axis=-1, keepdims=True))
        dout = diff / D
        dxn = dout * gv
        dh = r * (dxn - xn * jnp.mean(dxn * xn, axis=-1, keepdims=True))
        return (jnp.zeros((1, LANE), F32) + loss, dh, 0.5 * dh, jnp.sum(dout * xn, axis=0, keepdims=True))

    row = ((tr, D), lambda i: (i, 0))
    return _ew(name, (S // tr,),
               [(h,) + row, (g, (1, D), lambda i: (0, 0)), (target,) + row],
               [((1, LANE), F32, (1, LANE), lambda i: (0, 0)), ((S, D), F32) + row, ((S, D), BF) + row,
                ((1, D), F32, (1, D), lambda i: (0, 0))], fn, acc=(0, 3))


def _ffn_fwd(tag, u, h, wg, fetch_wu, fetch_wd):
    S, D = u.shape
    Fs = wg.shape[1]
    tm, tk = _pick(S, M_TILE, SUBLANE), _pick(D, K_TILE, LANE)
    act = ((NDEV, S, Fs), BF, (None, tm, Fs), lambda b, i, j, k: (b, i, 0))
    lhs = (u, (tm, tk), lambda b, i, j, k: (i, k))
    rhs = lambda w: (w, (None, Fs, tk), lambda b, i, j, k: (b, 0, k))
    (gt,) = _mm(tag + "_gate", (NDEV, S // tm, 1, D // tk), [lhs, rhs(wg)], [(0, 1, NT, 0)], [], [act],
                lambda accs, sides: accs, [(tm, Fs)])
    wu = fetch_wu(gt)

    def up_epilogue(accs, sides):
        g = sides[0].astype(F32)
        return accs[0], g * _sigmoid(g) * accs[0]

    up, a = _mm(tag + "_up", (NDEV, S // tm, 1, D // tk), [lhs, rhs(wu)], [(0, 1, NT, 0)],
                [(gt, (None, tm, Fs), lambda b, i, j, k: (b, i, 0))], [act, act], up_epilogue, [(tm, Fs)])
    wd = fetch_wd(a)
    tn = _pick(D, 1024, LANE)
    (hn,) = _mm(
        tag + "_down", (1, S // tm, D // tn, NDEV // K_SHARDS),
        [(a, (K_SHARDS, tm, Fs), lambda b, i, j, k: (k, i, 0)), (wd, (K_SHARDS, Fs, tn), lambda b, i, j, k: (k, 0, j))],
        [(0, 1, NN, 0)], [(h, (tm, tn), lambda b, i, j, k: (i, j))],
        [((S, D), F32, (tm, tn), lambda b, i, j, k: (i, j))],
        lambda accs, sides: [sides[0] + 0.5 * accs[0]], [(tm, tn)])
    return hn, (gt, up, a), wu, wd


def _ffn_bwd(tag, dhs, u, saved, wg, wu, wd, deps=()):
    gt, up, a = saved
    S, D = u.shape
    Fs = wg.shape[1]
    tm, tk = _pick(S, M_TILE, SUBLANE), _pick(D, K_TILE, LANE)
    act_in = lambda arr: (arr, (None, tm, Fs), lambda b, i, j, k: (b, i, 0))
    act_out = ((NDEV, S, Fs), BF, (None, tm, Fs), lambda b, i, j, k: (b, i, 0))

    def act_epilogue(accs, sides):
        da = accs[0]
        gtv, upv = sides[0].astype(F32), sides[1].astype(F32)
        sg = _sigmoid(gtv)
        return da * upv * sg * (1.0 + gtv * (1.0 - sg)), da * gtv * sg

    dgt, dup = _mm(
        tag + "_dact", (NDEV, S // tm, 1, D // tk),
        [(dhs, (tm, tk), lambda b, i, j, k: (i, k)), (wd, (None, Fs, tk), lambda b, i, j, k: (b, 0, k))],
        [(0, 1, NT, 0)], [act_in(gt), act_in(up)], [act_out, act_out], act_epilogue, [(tm, Fs)], deps=deps)

    ts = _pick(S, K_TILE, SUBLANE)
    tn = _pick(D, 1024, LANE)
    wgrad = ((NDEV, Fs, D), BF, (None, Fs, tn), lambda b, i, j, k: (b, 0, j))
    tok = lambda arr: (arr, (None, ts, Fs), lambda b, i, j, k: (b, k, 0))

    def grad_down(deps=()):
        return _mm(
            tag + "_dwd", (NDEV, 1, D // tn, S // ts),
            [tok(a), (dhs, (ts, tn), lambda b, i, j, k: (k, j))],
            [(0, 1, TN, 0)], [], [wgrad], lambda accs, sides: accs, [(Fs, tn)], deps=deps)[0]

    def grad_gate_up(deps=()):
        return _mm(
            tag + "_dwgu", (NDEV, 1, D // tn, S // ts),
            [tok(dgt), tok(dup), (u, (ts, tn), lambda b, i, j, k: (k, j))],
            [(0, 2, TN, 0), (1, 2, TN, 1)], [], [wgrad, wgrad], lambda accs, sides: accs, [(Fs, tn), (Fs, tn)],
            deps=deps)

    def du(deps=()):
        return _mm(
            tag + "_du", (1, S // tm, D // tn, NDEV // K_SHARDS),
            [(dgt, (K_SHARDS, tm, Fs), lambda b, i, j, k: (k, i, 0)), (dup, (K_SHARDS, tm, Fs), lambda b, i, j, k: (k, i, 0)),
             (wg, (K_SHARDS, Fs, tn), lambda b, i, j, k: (k, 0, j)), (wu, (K_SHARDS, Fs, tn), lambda b, i, j, k: (k, 0, j))],
            [(0, 2, NN, 0), (1, 3, NN, 0)], [], [((S, D), F32, (tm, tn), lambda b, i, j, k: (i, j))],
            lambda accs, sides: accs, [(tm, tn)], deps=deps)[0]

    return grad_down, grad_gate_up, du


def _ssm_params(lam_re, lam_im, log_dt, b_re, b_im, c_re, c_im):
    G, N = lam_re.shape
    C = b_re.shape[2]
    lam_re = jnp.minimum(lam_re, -1e-4)
    dt = jnp.exp(log_dt)[:, None]
    mag = jnp.exp(lam_re * dt)
    a_re = mag * jnp.cos(lam_im * dt)
    a_im = mag * jnp.sin(lam_im * dt)
    den = lam_re * lam_re + lam_im * lam_im
    p = a_re - 1.0
    f_re = ((p * lam_re + a_im * lam_im) / den)[:, :, None]
    f_im = ((a_im * lam_re - p * lam_im) / den)[:, :, None]
    bb_re = f_re * b_re - f_im * b_im
    bb_im = f_re * b_im + f_im * b_re
    gpt = LANE // C
    tiles = G // gpt
    eye = jnp.eye(gpt, dtype=F32)

    def bd(bb):
        return jnp.einsum("bgnc,gh->bgchn", bb.reshape(tiles, gpt, N, C), eye).reshape(tiles, gpt * C, gpt * N)

    def cd(cc):
        return jnp.einsum("bgcn,gh->bgnhc", cc.reshape(tiles, gpt, C, N), eye).reshape(tiles, gpt * N, gpt * C)

    rows = G * N // LANE
    return (a_re.reshape(rows, LANE), a_im.reshape(rows, LANE), bd(bb_re), bd(bb_im), cd(c_re), cd(-c_im))


def _tile_states(ref3, b, per):
    return jnp.concatenate([ref3[:, per * b + r, :] for r in range(per)], axis=1).astype(BF)


def _ssm_spread(name, x, m_re, m_im, dims, rows):
    S, W = x.shape
    tiles = m_re.shape[0]
    tch, per = W // tiles, rows // tiles
    tq = _pick(S, 256, SUBLANE)

    def body(x_ref, mre_ref, mim_ref, ore_ref, oim_ref):
        for b in range(tiles):
            xb = x_ref[:, b * tch:(b + 1) * tch]
            for m_ref, o_ref in ((mre_ref, ore_ref), (mim_ref, oim_ref)):
                val = lax.dot_general(xb, m_ref[b], (dims, ((), ())), preferred_element_type=F32)
                for r in range(per):
                    o_ref[:, per * b + r, :] = val[:, r * LANE:(r + 1) * LANE]

    whole = lambda m: pl.BlockSpec(m.shape, lambda i: (0, 0, 0))
    st = pl.BlockSpec((tq, rows, LANE), lambda i: (i, 0, 0))
    return pl.pallas_call(
        body, name=name, grid=(S // tq,),
        in_specs=[pl.BlockSpec((tq, W), lambda i: (i, 0)), whole(m_re), whole(m_im)], out_specs=[st, st],
        out_shape=[jax.ShapeDtypeStruct((S, rows, LANE), F32)] * 2,
        compiler_params=pltpu.CompilerParams(dimension_semantics=("parallel",)),
    )(x, m_re, m_im)


def _ssm_collect(name, z_re3, z_im3, m_re, m_im, dims, side, gain, epilogue, out_dtypes):
    S, rows, _ = z_re3.shape
    tiles = m_re.shape[0]
    W = side.shape[1]
    tch, per = W // tiles, rows // tiles
    tq = _pick(S, 256, SUBLANE)
    n_out = len(out_dtypes)

    def body(zre_ref, zim_ref, mre_ref, mim_ref, side_ref, gain_ref, *out_refs):
        for b in range(tiles):
            cols = slice(b * tch, (b + 1) * tch)
            zre, zim = _tile_states(zre_ref, b, per), _tile_states(zim_ref, b, per)
            acc = lax.dot_general(zre, mre_ref[b], (dims, ((), ())), preferred_element_type=F32)
            acc = acc + lax.dot_general(zim, mim_ref[b], (dims, ((), ())), preferred_element_type=F32)
            for o, v in zip(out_refs[:n_out], epilogue(acc, side_ref[:, cols], gain_ref[:, cols])):
                o[:, cols] = v.astype(o.dtype)
            out_refs[n_out][:, b * per * LANE:(b + 1) * per * LANE] = zre
            out_refs[n_out + 1][:, b * per * LANE:(b + 1) * per * LANE] = zim

    whole = lambda m: pl.BlockSpec(m.shape, lambda i: (0, 0, 0))
    st = pl.BlockSpec((tq, rows, LANE), lambda i: (i, 0, 0))
    ch = pl.BlockSpec((tq, W), lambda i: (i, 0))
    flat = pl.BlockSpec((tq, rows * LANE), lambda i: (i, 0))
    return pl.pallas_call(
        body, name=name, grid=(S // tq,),
        in_specs=[st, st, whole(m_re), whole(m_im), ch, pl.BlockSpec((1, W), lambda i: (0, 0))],
        out_specs=[ch] * n_out + [flat, flat],
        out_shape=[jax.ShapeDtypeStruct((S, W), dt) for dt in out_dtypes]
        + [jax.ShapeDtypeStruct((S, rows * LANE), BF)] * 2,
        compiler_params=pltpu.CompilerParams(dimension_semantics=("parallel",)),
    )(z_re3, z_im3, m_re, m_im, side, gain)


def _scan_fwd(bu_re, bu_im, a_re, a_im):
    S, R, _ = bu_re.shape
    tc = _pick(S, 256, SUBLANE)

    def body(bre, bim, are, aim, sre, sim, carry):
        @pl.when(pl.program_id(0) == 0)
        def _():
            carry[...] = jnp.zeros_like(carry)

        ar, ai = are[...], aim[...]

        def step(t, c):
            pr, pi = c
            nr = ar * pr - ai * pi + bre[t]
            ni = ar * pi + ai * pr + bim[t]
            sre[t] = nr
            sim[t] = ni
            return nr, ni

        pr, pi = lax.fori_loop(0, tc, step, (carry[0], carry[1]), unroll=8)
        carry[0] = pr
        carry[1] = pi

    blk = pl.BlockSpec((tc, R, LANE), lambda i: (i, 0, 0))
    par = pl.BlockSpec((R, LANE), lambda i: (0, 0))
    return pl.pallas_call(
        body, name="ssm_scan_fwd", grid=(S // tc,),
        in_specs=[blk, blk, par, par], out_specs=[blk, blk],
        out_shape=[jax.ShapeDtypeStruct((S, R, LANE), F32)] * 2,
        scratch_shapes=[pltpu.VMEM((2, R, LANE), F32)],
        compiler_params=pltpu.CompilerParams(dimension_semantics=("arbitrary",)),
    )(bu_re, bu_im, a_re, a_im)


def _scan_bwd(ds_re, ds_im, s_re, s_im, a_re, a_im):
    S, R, _ = ds_re.shape
    tc = _pick(S, 256, SUBLANE)
    nc = S // tc

    def body(dre, dim_, sre, sim, are, aim, lre, lim, dar, dai, carry):
        @pl.when(pl.program_id(0) == 0)
        def _():
            carry[...] = jnp.zeros_like(carry)
            dar[...] = jnp.zeros_like(dar)
            dai[...] = jnp.zeros_like(dai)

        ar, ai = are[...], aim[...]

        def step(tt, c):
            t = tc - 1 - tt
            lr, li, gr, gi = c
            sr, si = sre[t], sim[t]
            gr = gr + lr * sr + li * si
            gi = gi + li * sr - lr * si
            nlr = dre[t] + ar * lr + ai * li
            nli = dim_[t] + ar * li - ai * lr
            lre[t] = nlr
            lim[t] = nli
            return nlr, nli, gr, gi

        lr, li, gr, gi = lax.fori_loop(0, tc, step, (carry[0], carry[1], dar[...], dai[...]), unroll=8)
        carry[0] = lr
        carry[1] = li
        dar[...] = gr
        dai[...] = gi

    blk = pl.BlockSpec((tc, R, LANE), lambda i: (nc - 1 - i, 0, 0))
    par = pl.BlockSpec((R, LANE), lambda i: (0, 0))
    return pl.pallas_call(
        body, name="ssm_scan_bwd", grid=(nc,),
        in_specs=[blk, blk, blk, blk, par, par], out_specs=[blk, blk, par, par],
        out_shape=[jax.ShapeDtypeStruct((S, R, LANE), F32)] * 2 + [jax.ShapeDtypeStruct((R, LANE), F32)] * 2,
        scratch_shapes=[pltpu.VMEM((2, R, LANE), F32)],
        compiler_params=pltpu.CompilerParams(dimension_semantics=("arbitrary",)),
    )(ds_re, ds_im, s_re, s_im, a_re, a_im)


def _shift_down(z, k):
    t = lax.broadcasted_iota(I32, z.shape, 0)
    return jnp.where(t >= k, pltpu.roll(z, k, 0), 0.0)


def _shift_up(z, k):
    n = z.shape[0]
    t = lax.broadcasted_iota(I32, z.shape, 0)
    return jnp.where(t < n - k, pltpu.roll(z, n - k, 0), 0.0)


def _conv_fwd(proj, cw, cb):
    _, S, W = proj.shape
    ct = _pick(W, 256, LANE)

    def fn(bg, cg, val, w, b):
        z = cg * val
        conv = b + w[0:1] * _shift_down(z, 2) + w[1:2] * _shift_down(z, 1) + w[2:3] * z
        return bg * conv, conv

    sl = lambda s: (proj, (None, S, ct), lambda j, s=s: (s, 0, j))
    col = ((S, ct), lambda j: (0, j))
    return _ew("conv_fwd", (W // ct,),
               [sl(1), sl(2), sl(3), (cw, (3, ct), lambda j: (0, j)), (cb, (1, ct), lambda j: (0, j))],
               [((S, W), BF) + col, ((S, W), F32) + col], fn)


def _conv_bwd(dyb, proj, conv, cw):
    _, S, W = proj.shape
    ct = _pick(W, 256, LANE)

    def fn(dy, bg, cg, val, cv, w):
        z = cg * val
        z1, z2 = _shift_down(z, 1), _shift_down(z, 2)
        dconv = dy * bg
        dz = w[2:3] * dconv + w[1:2] * _shift_up(dconv, 1) + w[0:1] * _shift_up(dconv, 2)
        dw = jnp.concatenate([jnp.sum(dconv * z2, axis=0, keepdims=True), jnp.sum(dconv * z1, axis=0, keepdims=True),
                              jnp.sum(dconv * z, axis=0, keepdims=True)], axis=0)
        return dy * cv, dz * val, dz * cg, dw, jnp.sum(dconv, axis=0, keepdims=True)

    sl = lambda s: (proj, (None, S, ct), lambda j, s=s: (s, 0, j))
    col = ((S, ct), lambda j: (0, j))
    return _ew("conv_bwd", (W // ct,),
               [(dyb,) + col, sl(1), sl(2), sl(3), (conv,) + col, (cw, (3, ct), lambda j: (0, j))],
               [((S, W), BF) + col, ((S, W), BF) + col, ((S, W), BF) + col,
                ((3, W), F32, (3, ct), lambda j: (0, j)), ((1, W), F32, (1, ct), lambda j: (0, j))], fn)


def _plain(accs, sides):
    return accs


def kernel(x, ffn1_norm, ffn1_w_gate, ffn1_w_up, ffn1_w_down, mix_norm, w_in, ssm_lambda_re, ssm_lambda_im, ssm_log_dt, ssm_b_re, ssm_b_im, ssm_c_re, ssm_c_im, ssm_d, ssm_w_glu, ssm_b_glu, ssm_w_out, conv_w, conv_b, conv_w_out, w_o, ffn2_norm, ffn2_w_gate, ffn2_w_up, ffn2_w_down, final_norm, loss_target, m_ffn1_norm, m_ffn1_w_gate, m_ffn1_w_up, m_ffn1_w_down, m_mix_norm, m_w_in, m_ssm_lambda_re, m_ssm_lambda_im, m_ssm_log_dt, m_ssm_b_re, m_ssm_b_im, m_ssm_c_re, m_ssm_c_im, m_ssm_d, m_ssm_w_glu, m_ssm_b_glu, m_ssm_w_out, m_conv_w, m_conv_b, m_conv_w_out, m_w_o, m_ffn2_norm, m_ffn2_w_gate, m_ffn2_w_up, m_ffn2_w_down, m_final_norm, v_ffn1_norm, v_ffn1_w_gate, v_ffn1_w_up, v_ffn1_w_down, v_mix_norm, v_w_in, v_ssm_lambda_re, v_ssm_lambda_im, v_ssm_log_dt, v_ssm_b_re, v_ssm_b_im, v_ssm_c_re, v_ssm_c_im, v_ssm_d, v_ssm_w_glu, v_ssm_b_glu, v_ssm_w_out, v_conv_w, v_conv_b, v_conv_w_out, v_w_o, v_ffn2_norm, v_ffn2_w_gate, v_ffn2_w_up, v_ffn2_w_down, v_final_norm):
    P = dict(ffn1_norm=ffn1_norm, ffn1_w_gate=ffn1_w_gate, ffn1_w_up=ffn1_w_up, ffn1_w_down=ffn1_w_down, mix_norm=mix_norm, w_in=w_in, ssm_lambda_re=ssm_lambda_re, ssm_lambda_im=ssm_lambda_im, ssm_log_dt=ssm_log_dt, ssm_b_re=ssm_b_re, ssm_b_im=ssm_b_im, ssm_c_re=ssm_c_re, ssm_c_im=ssm_c_im, ssm_d=ssm_d, ssm_w_glu=ssm_w_glu, ssm_b_glu=ssm_b_glu, ssm_w_out=ssm_w_out, conv_w=conv_w, conv_b=conv_b, conv_w_out=conv_w_out, w_o=w_o, ffn2_norm=ffn2_norm, ffn2_w_gate=ffn2_w_gate, ffn2_w_up=ffn2_w_up, ffn2_w_down=ffn2_w_down, final_norm=final_norm)
    M = dict(ffn1_norm=m_ffn1_norm, ffn1_w_gate=m_ffn1_w_gate, ffn1_w_up=m_ffn1_w_up, ffn1_w_down=m_ffn1_w_down, mix_norm=m_mix_norm, w_in=m_w_in, ssm_lambda_re=m_ssm_lambda_re, ssm_lambda_im=m_ssm_lambda_im, ssm_log_dt=m_ssm_log_dt, ssm_b_re=m_ssm_b_re, ssm_b_im=m_ssm_b_im, ssm_c_re=m_ssm_c_re, ssm_c_im=m_ssm_c_im, ssm_d=m_ssm_d, ssm_w_glu=m_ssm_w_glu, ssm_b_glu=m_ssm_b_glu, ssm_w_out=m_ssm_w_out, conv_w=m_conv_w, conv_b=m_conv_b, conv_w_out=m_conv_w_out, w_o=m_w_o, ffn2_norm=m_ffn2_norm, ffn2_w_gate=m_ffn2_w_gate, ffn2_w_up=m_ffn2_w_up, ffn2_w_down=m_ffn2_w_down, final_norm=m_final_norm)
    V = dict(ffn1_norm=v_ffn1_norm, ffn1_w_gate=v_ffn1_w_gate, ffn1_w_up=v_ffn1_w_up, ffn1_w_down=v_ffn1_w_down, mix_norm=v_mix_norm, w_in=v_w_in, ssm_lambda_re=v_ssm_lambda_re, ssm_lambda_im=v_ssm_lambda_im, ssm_log_dt=v_ssm_log_dt, ssm_b_re=v_ssm_b_re, ssm_b_im=v_ssm_b_im, ssm_c_re=v_ssm_c_re, ssm_c_im=v_ssm_c_im, ssm_d=v_ssm_d, ssm_w_glu=v_ssm_w_glu, ssm_b_glu=v_ssm_b_glu, ssm_w_out=v_ssm_w_out, conv_w=v_conv_w, conv_b=v_conv_b, conv_w_out=v_conv_w_out, w_o=v_w_o, ffn2_norm=v_ffn2_norm, ffn2_w_gate=v_ffn2_w_gate, ffn2_w_up=v_ffn2_w_up, ffn2_w_down=v_ffn2_w_down, final_norm=v_final_norm)
    names = list(P)
    sharded = ["ffn1_w_gate", "ffn1_w_up", "ffn1_w_down", "w_in", "ssm_w_glu", "ssm_w_out", "conv_w_out", "w_o",
               "ffn2_w_gate", "ffn2_w_up", "ffn2_w_down"]
    replicated = [n for n in names if n not in sharded and n != "conv_w"]

    S, D = x.shape[1], x.shape[2]
    W = ssm_d.shape[0]
    Dc = D // NDEV
    G, N = ssm_lambda_re.shape
    rows = G * N // LANE
    xh = x.reshape(S, D)
    target = loss_target.reshape(S, D)
    xi, yi, ci = lax.axis_index("x"), lax.axis_index("y"), lax.axis_index("c")
    c_arr = jnp.reshape(ci, (1,)).astype(I32)
    q_arr = jnp.reshape(2 * xi + yi, (1,)).astype(I32)
    row = lambda v: v.reshape(1, -1)

    transposed = ("ffn1_w_gate", "ffn1_w_up", "ffn2_w_gate", "ffn2_w_up")
    local = lambda table, n: table[n].T if n in transposed else table[n]

    groups = [sharded[0:1], sharded[1:2], sharded[2:3], ["w_in", "conv_w"], sharded[4:8],
              sharded[8:9], sharded[9:10], sharded[10:11]]
    first_begun, second_begun, chain = {}, {}, []
    me = 4 * xi + 2 * yi + ci

    def gather_first(gi):
        hold = 0.0 * chain[0][0, 0] if chain else 0.0
        srcs = [conv_w + hold if n == "conv_w" else (local(P, n) + hold).astype(BF) for n in groups[gi]]
        lands = [lax.dynamic_update_slice(lax.empty((NDEV,) + s.shape, s.dtype), s[None], (me,) + (0,) * s.ndim)
                 for s in srcs]
        first_begun[gi] = _split_start("gather_first_start_%d" % gi, srcs, lands, _gather_first_copies, 3, chain[-1:])
        chain.append(first_begun[gi][4])

    def gather_second(gi, after):
        _, lands = _split_wait("gather_first_wait_%d" % gi, first_begun[gi], _gather_first_copies, after)
        second_begun[gi] = _split_start("gather_second_start_%d" % gi, [], lands, _gather_second_copies, 2, chain[-1:])
        chain.append(second_begun[gi][4])

    def gathered(gi, after):
        _, lands = _split_wait("gather_second_wait_%d" % gi, second_begun[gi], _gather_second_copies, after)
        return _gather_forward("gather_forward_%d" % gi, lands)

    def fetch(gi, seconds, firsts):
        def get(after):
            for g in seconds:
                gather_second(g, after)
            for g in firsts:
                gather_first(g)
            return gathered(gi, after)
        return get

    tm = _pick(S, M_TILE, SUBLANE)
    th = _pick(S, M_TILE // 2, SUBLANE)
    tk = _pick(D, K_TILE, LANE)
    ts = _pick(S, K_TILE, SUBLANE)
    tn = _pick(D, 1024, LANE)

    gather_first(0)
    gather_first(1)
    u1, r1 = _rms_fwd("rms1", xh, row(ffn1_norm), deps=list(chain))
    gather_second(0, u1)
    gather_first(2)
    gather_second(1, u1)
    gather_first(3)
    (wg1,) = gathered(0, u1)
    h1, ffn1_saved, wu1, wd1 = _ffn_fwd("ffn1", u1, xh, wg1, lambda after: fetch(1, [2], [4])(after)[0],
                                        lambda after: fetch(2, [3], [5])(after)[0])
    u2, r2 = _rms_fwd("rms2", h1, row(mix_norm))
    w_in_f, cw_f = fetch(3, [4], [6])(u2)
    cw = jnp.transpose(cw_f, (1, 0, 2)).reshape(3, W)
    (proj,) = _mm(
        "in_proj", (NDEV, S // tm, 1, D // tk),
        [(u2, (tm, tk), lambda b, i, j, k: (i, k)), (w_in_f, (None, tk, W), lambda b, i, j, k: (b, k, 0))],
        [(0, 1, NN, 0)], [], [((NDEV, S, W), F32, (None, tm, W), lambda b, i, j, k: (b, i, 0))], _plain, [(tm, W)])

    ssm_in = (ssm_lambda_re, ssm_lambda_im, ssm_log_dt, ssm_b_re, ssm_b_im, ssm_c_re, ssm_c_im)
    (a_re, a_im, bd_re, bd_im, cd_re, cd_imn), ssm_vjp = jax.vjp(_ssm_params, *ssm_in)
    bd_re_b, bd_im_b, cd_re_b, cd_imn_b = (t.astype(BF) for t in (bd_re, bd_im, cd_re, cd_imn))
    v_f = proj[0]
    v_bf = v_f.astype(BF)
    bu_re3, bu_im3 = _ssm_spread("ssm_bu", v_bf, bd_re_b, bd_im_b, NN, rows)
    s_re3, s_im3 = _scan_fwd(bu_re3, bu_im3, a_re, a_im)

    def y0_epilogue(acc, v_tile, d_tile):
        y0 = acc + d_tile * v_tile
        return y0, _gelu(y0)

    y0, y1, s_re_b, s_im_b = _ssm_collect("ssm_y0", s_re3, s_im3, cd_re_b, cd_imn_b, NN, v_f, row(ssm_d),
                                          y0_epilogue, [F32, BF])

    tw = _pick(W, 512, LANE)
    w_glu_f, w_so, w_co, w_o_f = fetch(4, [5], [7])(y1)
    w_glu_f = w_glu_f.reshape(W, W)
    w_o_f = w_o_f.reshape(D, D)

    def glu_epilogue(accs, sides):
        q = accs[0] + sides[1]
        return q, _gelu(sides[0]) * _sigmoid(q)

    q_pre, y2 = _mm("ssm_glu", (1, S // tm, W // tw, 1),
                    [(y1, (tm, W), lambda b, i, j, k: (i, 0)), (w_glu_f, (W, tw), lambda b, i, j, k: (0, j))],
                    [(0, 1, NN, 0)],
                    [(y0, (tm, tw), lambda b, i, j, k: (i, j)), (row(ssm_b_glu), (1, tw), lambda b, i, j, k: (0, j))],
                    [((S, W), F32, (tm, tw), lambda b, i, j, k: (i, j)), ((S, W), BF, (tm, tw), lambda b, i, j, k: (i, j))],
                    glu_epilogue, [(tm, tw)])

    yb, conv = _conv_fwd(proj, cw, row(conv_b))

    per = W // Dc
    ga_blk = (proj, (None, tm, Dc), lambda b, i, j, k: (4 + b // per, i, b % per))
    gb_blk = (proj, (None, tm, Dc), lambda b, i, j, k: (6 + b // per, i, b % per))
    dc_out = ((S, D), BF, (tm, Dc), lambda b, i, j, k: (i, b))

    def merge_epilogue(accs, sides):
        za, zb = accs
        return _sigmoid(sides[0]) * za + _sigmoid(sides[1]) * zb, za, zb

    merged, z_a, z_b = _mm(
        "mix_merge", (NDEV, S // tm, 1, 1),
        [(y2, (tm, W), lambda b, i, j, k: (i, 0)), (yb, (tm, W), lambda b, i, j, k: (i, 0)),
         (w_so, (None, W, Dc), lambda b, i, j, k: (b, 0, 0)), (w_co, (None, W, Dc), lambda b, i, j, k: (b, 0, 0))],
        [(0, 2, NN, 0), (1, 3, NN, 1)], [ga_blk, gb_blk], [dc_out, dc_out, dc_out], merge_epilogue, [(tm, Dc)] * 2)

    (h2,) = _mm("mix_out", (1, S // tm, D // tn, D // tk),
                [(merged, (tm, tk), lambda b, i, j, k: (i, k)), (w_o_f, (tk, tn), lambda b, i, j, k: (k, j))],
                [(0, 1, NN, 0)], [(h1, (tm, tn), lambda b, i, j, k: (i, j))],
                [((S, D), F32, (tm, tn), lambda b, i, j, k: (i, j))],
                lambda accs, sides: [sides[0] + accs[0]], [(tm, tn)])

    u3, r3 = _rms_fwd("rms3", h2, row(ffn2_norm))
    (wg2,) = fetch(5, [6, 7], [])(u3)
    h3, ffn2_saved, wu2, wd2 = _ffn_fwd("ffn2", u3, h2, wg2, lambda after: gathered(6, after)[0],
                                        lambda after: gathered(7, after)[0])
    loss_vec, dh3, dh3_half, d_final_norm = _loss_head("loss_head", h3, row(final_norm), target)
    loss = lax.psum(loss_vec[0, 0], ("x", "y", "c"))
    loss_done = jnp.zeros((SUBLANE, LANE), F32) + loss

    grads, deltas, new_m, new_v = {}, {}, {}, {}

    def rs_sibling_start(tag, parts):
        lands = [lax.empty((4,) + p.shape[1:], p.dtype) for p in parts]
        return _split_start("rs_sibling_start_" + tag, parts, lands, _sibling_copies, 4)

    def rs_chips_start(tag, sibling_begun, after):
        parts, lands = _split_wait("rs_sibling_wait_" + tag, sibling_begun, _sibling_copies, after)
        sums = [_sum_sibling("rs_sum_%s_%d" % (tag, a), p, land, c_arr) for a, (p, land) in enumerate(zip(parts, lands))]
        lands2 = [lax.empty((3,) + sm.shape[1:], sm.dtype) for sm in sums]
        return _split_start("rs_chips_start_" + tag, sums, lands2, _chips_copies, 3)

    def rs_end(tag, group, begun, after):
        sums, lands2 = _split_wait("rs_chips_wait_" + tag, begun, _chips_copies, after)
        for n, sm, land2 in zip(group, sums, lands2):
            res = _finish_sharded("adamw_" + n, sm, land2, q_arr, local(P, n), local(M, n), local(V, n))
            grads[n], deltas[n], new_m[n], new_v[n] = [t.T if n in transposed else t for t in res]

    f2_dwd, f2_dwgu, f2_du = _ffn_bwd("ffn2", dh3_half, u3, ffn2_saved, wg2, wu2, wd2, deps=[loss_done])
    dwd2 = f2_dwd()
    dwg2, dwu2 = f2_dwgu()
    sib_ffn2 = rs_sibling_start("ffn2", [dwg2, dwu2, dwd2])
    du3 = f2_du(deps=[sib_ffn2[4]])
    dh2, dh2_b, d_ffn2_norm = _rms_bwd("rms3_bwd", du3, h2, r3, row(ffn2_norm), dh3, 1.0)
    rs_ffn2 = rs_chips_start("ffn2", sib_ffn2, dh2)

    dg_out = ((2, S, W), BF, (None, tm, Dc), lambda b, i, j, k: (j // per, i, j % per))
    ga_blk2 = (proj, (None, tm, Dc), lambda b, i, j, k: (4 + j // per, i, j % per))
    gb_blk2 = (proj, (None, tm, Dc), lambda b, i, j, k: (6 + j // per, i, j % per))
    dcj = lambda arr: (arr, (tm, Dc), lambda b, i, j, k: (i, j))
    dcj_out = ((S, D), BF, (tm, Dc), lambda b, i, j, k: (i, j))

    def dmerge_epilogue(accs, sides):
        dm = accs[0]
        sa, sb = _sigmoid(sides[0]), _sigmoid(sides[1])
        za, zb = sides[2].astype(F32), sides[3].astype(F32)
        return dm * sa, dm * sb, dm * za * sa * (1.0 - sa), dm * zb * sb * (1.0 - sb)

    dz_a, dz_b, dga, dgb = _mm(
        "mix_out_dx", (1, S // tm, NDEV, D // tk),
        [(dh2_b, (tm, tk), lambda b, i, j, k: (i, k)), (w_o_f, (Dc, tk), lambda b, i, j, k: (j, k))],
        [(0, 1, NT, 0)], [ga_blk2, gb_blk2, dcj(z_a), dcj(z_b)], [dcj_out, dcj_out, dg_out, dg_out],
        dmerge_epilogue, [(tm, Dc)], deps=[rs_ffn2[4]])

    td = _pick(D, M_TILE, LANE)
    (dw_o,) = _mm("mix_out_dw", (1, D // td, D // tn, S // ts),
                  [(merged, (ts, td), lambda b, i, j, k: (k, i)), (dh2_b, (ts, tn), lambda b, i, j, k: (k, j))],
                  [(0, 1, TN, 0)], [], [((D, D), BF, (td, tn), lambda b, i, j, k: (i, j))], _plain, [(td, tn)])

    wout = ((NDEV, W, Dc), BF, (None, W, Dc), lambda b, i, j, k: (b, 0, 0))
    dw_so, dw_co = _mm(
        "mix_merge_dw", (NDEV, 1, 1, S // ts),
        [(y2, (ts, W), lambda b, i, j, k: (k, 0)), (yb, (ts, W), lambda b, i, j, k: (k, 0)),
         (dz_a, (ts, Dc), lambda b, i, j, k: (k, b)), (dz_b, (ts, Dc), lambda b, i, j, k: (k, b))],
        [(0, 2, TN, 0), (1, 3, TN, 1)], [], [wout, wout], _plain, [(W, Dc)] * 2)

    def dglu_epilogue(accs, sides):
        dy2, dyb = accs
        sq = _sigmoid(sides[1])
        return dy2 * _gelu(sides[0]) * sq * (1.0 - sq), dy2 * sq, dyb

    full_w = lambda arr: (arr, (th, W), lambda b, i, j, k: (i, 0))
    full_w_out = lambda dt: ((S, W), dt, (th, W), lambda b, i, j, k: (i, 0))
    dq, dy1p, dyb = _mm(
        "mix_merge_dx", (1, S // th, 1, NDEV),
        [(dz_a, (th, Dc), lambda b, i, j, k: (i, k)), (dz_b, (th, Dc), lambda b, i, j, k: (i, k)),
         (w_so, (None, W, Dc), lambda b, i, j, k: (k, 0, 0)), (w_co, (None, W, Dc), lambda b, i, j, k: (k, 0, 0))],
        [(0, 2, NT, 0), (1, 3, NT, 1)], [full_w(y0), full_w(q_pre)], [full_w_out(BF), full_w_out(F32), full_w_out(F32)],
        dglu_epilogue, [(th, W)] * 2)

    def dy0_epilogue(accs, sides):
        dy0 = (sides[0] + accs[0]) * _gelu_grad(sides[1])
        return dy0, dy0

    wj = lambda arr: (arr, (tm, tw), lambda b, i, j, k: (i, j))
    dy0, dy0_b = _mm("ssm_glu_dx", (1, S // tm, W // tw, 1),
                     [(dq, (tm, W), lambda b, i, j, k: (i, 0)), (w_glu_f, (tw, W), lambda b, i, j, k: (j, 0))],
                     [(0, 1, NT, 0)], [wj(dy1p), wj(y0)],
                     [((S, W), F32, (tm, tw), lambda b, i, j, k: (i, j)), ((S, W), BF, (tm, tw), lambda b, i, j, k: (i, j))],
                     dy0_epilogue, [(tm, tw)])

    (dw_glu,) = _mm("ssm_glu_dw", (1, W // tw, 1, S // ts),
                    [(y1, (ts, tw), lambda b, i, j, k: (k, i)), (dq, (ts, W), lambda b, i, j, k: (k, 0))],
                    [(0, 1, TN, 0)], [], [((W, W), BF, (tw, W), lambda b, i, j, k: (i, 0))], _plain, [(tw, W)])

    tr = _pick(S, 256, SUBLANE)
    rw = ((tr, W), lambda i: (i, 0))
    vec_w = ((1, W), F32, (1, W), lambda i: (0, 0))
    d_b_glu, d_ssm_d = _ew(
        "ssm_colsums", (S // tr,), [(dq,) + rw, (dy0,) + rw, (proj, (None, tr, W), lambda i: (0, i, 0))],
        [vec_w, vec_w],
        lambda dqv, dyv, vv: (jnp.sum(dqv.astype(F32), axis=0, keepdims=True), jnp.sum(dyv * vv, axis=0, keepdims=True)),
        acc=(0, 1))

    ds_re3, ds_im3 = _ssm_spread("ssm_ds", dy0_b, cd_re_b, cd_imn_b, NT, rows)
    lam_re3, lam_im3, da_re, da_im = _scan_bwd(ds_re3, ds_im3, s_re3, s_im3, a_re, a_im)
    dv, lam_re_b, lam_im_b = _ssm_collect("ssm_dv", lam_re3, lam_im3, bd_re_b, bd_im_b, NT, dy0, row(ssm_d),
                                          lambda acc, dy_tile, d_tile: [acc + dy_tile * d_tile], [BF])
    tiles, tch, tst = bd_re.shape
    tok_ch = lambda arr: (arr, (ts, tch), lambda b, i, j, k: (k, b))
    tok_st = lambda arr: (arr, (ts, tst), lambda b, i, j, k: (k, b))
    bd_out = ((tiles, tch, tst), F32, (None, tch, tst), lambda b, i, j, k: (b, 0, 0))
    cd_out = ((tiles, tst, tch), F32, (None, tst, tch), lambda b, i, j, k: (b, 0, 0))
    dbd_re, dbd_im = _mm("ssm_dbd", (tiles, 1, 1, S // ts), [tok_ch(v_bf), tok_st(lam_re_b), tok_st(lam_im_b)],
                         [(0, 1, TN, 0), (0, 2, TN, 1)], [], [bd_out, bd_out], _plain, [(tch, tst)] * 2)
    dcd_re, dcd_imn = _mm("ssm_dcd", (tiles, 1, 1, S // ts), [tok_st(s_re_b), tok_st(s_im_b), tok_ch(dy0_b)],
                          [(0, 2, TN, 0), (1, 2, TN, 1)], [], [cd_out, cd_out], _plain, [(tst, tch)] * 2)
    d_ssm = ssm_vjp((da_re, da_im, dbd_re, dbd_im, dcd_re, dcd_imn))

    dbg, dcg, dval, d_conv_w_full, d_conv_b = _conv_bwd(dyb, proj, conv, cw)
    dproj = jnp.concatenate([dv[None], dbg[None], dcg[None], dval[None], dga, dgb], axis=0)

    (dw_in,) = _mm("in_proj_dw", (NDEV, D // td, 1, S // ts),
                   [(u2, (ts, td), lambda b, i, j, k: (k, i)), (dproj, (None, ts, W), lambda b, i, j, k: (b, k, 0))],
                   [(0, 1, TN, 0)], [], [((NDEV, D, W), BF, (None, td, W), lambda b, i, j, k: (b, i, 0))],
                   _plain, [(td, W)])
    sib_mixer = rs_sibling_start(
        "mixer", [dw_in, dw_glu.reshape(NDEV, W // NDEV, W), dw_so, dw_co, dw_o.reshape(NDEV, Dc, D)])
    (du2,) = _mm("in_proj_dx", (1, S // tm, D // tn, NDEV // K_SHARDS),
                 [(dproj, (K_SHARDS, tm, W), lambda b, i, j, k: (k, i, 0)),
                  (w_in_f, (K_SHARDS, tn, W), lambda b, i, j, k: (k, j, 0))],
                 [(0, 1, NT, 0)], [], [((S, D), F32, (tm, tn), lambda b, i, j, k: (i, j))], _plain, [(tm, tn)],
                 deps=[sib_mixer[4]])
    dh1, dh1_half, d_mix_norm = _rms_bwd("rms2_bwd", du2, h1, r2, row(mix_norm), dh2, 0.5)
    rs_mixer = rs_chips_start("mixer", sib_mixer, dh1)

    small = dict(mix_norm=d_mix_norm, ffn2_norm=d_ffn2_norm, final_norm=d_final_norm,
                 ssm_lambda_re=d_ssm[0], ssm_lambda_im=d_ssm[1], ssm_log_dt=d_ssm[2], ssm_b_re=d_ssm[3],
                 ssm_b_im=d_ssm[4], ssm_c_re=d_ssm[5], ssm_c_im=d_ssm[6], ssm_d=d_ssm_d, ssm_b_glu=d_b_glu,
                 conv_b=d_conv_b)
    replicated = [n for n in replicated if n != "ffn1_norm"] + ["ffn1_norm"]
    tile = SUBLANE * LANE

    def as_rows(p):
        flat = p.reshape(-1).astype(F32)
        return jnp.pad(flat, (0, -flat.shape[0] % tile)).reshape(-1, LANE)

    def pack(parts):
        return jnp.concatenate([as_rows(p) for p in parts], axis=0)

    cw_zero = jnp.zeros_like(d_conv_w_full)
    early_pk = pack([small[n] for n in replicated[:-1]] + [d_conv_w_full])
    early_land = lax.dynamic_update_slice(jnp.zeros((NDEV,) + early_pk.shape, F32), early_pk[None], (me, 0, 0))
    small_begun = _split_start("gather_small_start", [early_pk], [early_land], _everyone_copies, NDEV - 1)

    f1_dwd, f1_dwgu, f1_du = _ffn_bwd("ffn1", dh1_half, u1, ffn1_saved, wg1, wu1, wd1,
                                      deps=[rs_mixer[4], small_begun[4]])
    du1 = f1_du()
    dx, _, d_ffn1_norm = _rms_bwd("rms1_bwd", du1, xh, r1, row(ffn1_norm), dh1, 1.0)
    dwd1 = f1_dwd(deps=[d_ffn1_norm])
    late = d_ffn1_norm + 0.0 * dwd1[0, :1, :1].astype(F32)
    (late_all,) = _all_gather("gather_ffn1_norm_grad", [late.reshape(-1, LANE)])
    rs_ffn1_down = rs_chips_start("ffn1_down", rs_sibling_start("ffn1_down", [dwd1]), late_all)
    dwg1, dwu1 = f1_dwgu(deps=[rs_ffn1_down[4]])
    rs_ffn1_gate_up = rs_chips_start("ffn1_gate_up", rs_sibling_start("ffn1_gate_up", [dwg1, dwu1]), None)
    rs_end("ffn2", sharded[8:11], rs_ffn2, rs_ffn1_gate_up[4])
    rs_end("mixer", sharded[3:8], rs_mixer, [grads[n] for n in sharded[8:11]])
    _, (early_all,) = _split_wait("gather_small_wait", small_begun, _everyone_copies, [grads[n] for n in sharded[3:8]])
    small_all = jnp.concatenate([early_all, late_all], axis=1)
    g_pk, d_pk, m_pk, v_pk = _finish_replicated(
        "adamw_replicated", small_all, pack([P[n] for n in replicated[:-1]] + [cw_zero, P["ffn1_norm"]]),
        pack([M[n] for n in replicated[:-1]] + [cw_zero, M["ffn1_norm"]]),
        pack([V[n] for n in replicated[:-1]] + [cw_zero + 1.0, V["ffn1_norm"]]))

    def unpack(pk, r0, like):
        nr = -(-like.size // tile) * SUBLANE
        return pk[r0:r0 + nr].reshape(-1)[:like.size].reshape(like.shape), r0 + nr

    r0 = 0
    for n in replicated[:-1] + ["conv_w", "ffn1_norm"]:
        like = d_conv_w_full if n == "conv_w" else P[n]
        for store, pk in ((grads, g_pk), (deltas, d_pk), (new_m, m_pk), (new_v, v_pk)):
            store[n], r1 = unpack(pk, r0, like)
        r0 = r1
    g_cw_full = grads["conv_w"]
    cwl = conv_w.shape[1]
    g_cw = lax.dynamic_slice_in_dim(g_cw_full, me * cwl, cwl, axis=1)
    full3 = ((3, cwl), lambda i: (0, 0))
    grads["conv_w"], deltas["conv_w"], new_m["conv_w"], new_v["conv_w"] = _ew(
        "adamw_conv_w", (1,), [(g_cw,) + full3, (conv_w,) + full3, (m_conv_w,) + full3, (v_conv_w,) + full3],
        [((3, cwl), F32) + full3] * 4, lambda g, w, m, v: (g,) + _adamw(w, g, m, v))
    rs_end("ffn1_down", sharded[2:3], rs_ffn1_down, [g_pk, grads["conv_w"]])
    rs_end("ffn1_gate_up", sharded[0:2], rs_ffn1_gate_up, grads["ffn1_w_down"])

    return (loss, dx.reshape(x.shape), *[grads[n] for n in names], *[deltas[n] for n in names],
            *[new_m[n] for n in names], *[new_v[n] for n in names])
```

```python
import math

import jax
import jax.numpy as jnp
from jax import lax
from jax.experimental import pallas as pl
from jax.experimental.pallas import tpu as pltpu

F32 = jnp.float32
BF = jnp.bfloat16
I32 = jnp.int32
MESH = pl.DeviceIdType.MESH
LANE = 128
SUBLANE = 8
NDEV = 8
EW_BLOCK = 256 * 1024
M_TILE = 1024
K_TILE = 2048
K_SHARDS = 2
DMA_CHUNKS = 4
EPS = 1e-6
ADAM_LR, ADAM_B1, ADAM_B2, ADAM_EPS, ADAM_WD, ADAM_STEP = 0.001, 0.9, 0.999, 1e-08, 0.01, 10
NN = ((1,), (0,))
NT = ((1,), (1,))
TN = ((0,), (0,))
HBM = pl.BlockSpec(memory_space=pltpu.HBM)


def _pick(n, pref, mult):
    t = min(pref, n)
    t -= t % mult
    while t >= mult:
        if n % t == 0:
            return t
        t -= mult
    return n


def _sigmoid(x):
    return 1.0 / (1.0 + jnp.exp(-x))


_GELU_C = math.sqrt(2.0 / math.pi)


def _gelu(x):
    return 0.5 * x * (1.0 + jnp.tanh(_GELU_C * (x + 0.044715 * x * x * x)))


def _gelu_grad(x):
    t = jnp.tanh(_GELU_C * (x + 0.044715 * x * x * x))
    return 0.5 * (1.0 + t) + 0.5 * x * (1.0 - t * t) * _GELU_C * (1.0 + 3.0 * 0.044715 * x * x)


def _dep_specs(deps, rank):
    return [(d, d.shape, lambda *_, nd=d.ndim: (0,) * nd) for d in deps]


def _mm(name, grid, ops, pairs, sides, outs, epilogue, acc_shapes, deps=()):
    nk = grid[-1]
    n_ops, n_sides, n_outs = len(ops), len(sides), len(outs)
    dep_specs = _dep_specs(deps, len(grid))
    n_deps = len(dep_specs)

    def body(*refs):
        op_refs = refs[:n_ops]
        side_refs = refs[n_ops:n_ops + n_sides]
        out_refs = refs[n_ops + n_sides + n_deps:n_ops + n_sides + n_deps + n_outs]
        acc_refs = refs[n_ops + n_sides + n_deps + n_outs:]

        def partials():
            res = [None] * len(acc_shapes)
            for ia, ib, dims, ai in pairs:
                a_ref, b_ref = op_refs[ia], op_refs[ib]
                for s in range(a_ref.shape[0] if len(a_ref.shape) == 3 else 1):
                    a, b = (a_ref[s], b_ref[s]) if len(a_ref.shape) == 3 else (a_ref[...], b_ref[...])
                    p = lax.dot_general(a, b, (dims, ((), ())), preferred_element_type=F32)
                    res[ai] = p if res[ai] is None else res[ai] + p
            return res

        def finish(accs):
            vals = epilogue(accs, [s[...] for s in side_refs])
            for o, v in zip(out_refs, vals):
                o[...] = v.astype(o.dtype)

        if nk == 1:
            finish(partials())
        else:
            k = pl.program_id(len(grid) - 1)

            @pl.when(k == 0)
            def _():
                for a, p in zip(acc_refs, partials()):
                    a[...] = p

            @pl.when(k > 0)
            def _():
                for a, p in zip(acc_refs, partials()):
                    a[...] += p

            @pl.when(k == nk - 1)
            def _():
                finish([a[...] for a in acc_refs])

    return pl.pallas_call(
        body, name=name, grid=grid,
        in_specs=[pl.BlockSpec(b, m) for (_, b, m) in list(ops) + list(sides) + dep_specs],
        out_specs=[pl.BlockSpec(b, m) for (_, _, b, m) in outs],
        out_shape=[jax.ShapeDtypeStruct(s, d) for (s, d, _, _) in outs],
        scratch_shapes=[pltpu.VMEM(s, F32) for s in acc_shapes] if nk > 1 else [],
        compiler_params=pltpu.CompilerParams(
            dimension_semantics=("parallel",) * (len(grid) - 1) + ("arbitrary",)),
    )(*[a for (a, _, _) in list(ops) + list(sides) + dep_specs])


def _ew(name, grid, ins, outs, fn, acc=(), deps=()):
    n_in = len(ins)
    dep_specs = _dep_specs(deps, len(grid))

    def body(*refs):
        vals = fn(*[r[...] for r in refs[:n_in]])
        first = pl.program_id(0) == 0
        for idx, (o, v) in enumerate(zip(refs[n_in + len(dep_specs):], vals)):
            if idx in acc:
                @pl.when(first)
                def _(o=o, v=v):
                    o[...] = v.astype(o.dtype)

                @pl.when(jnp.logical_not(first))
                def _(o=o, v=v):
                    o[...] += v.astype(o.dtype)
            else:
                o[...] = v.astype(o.dtype)

    return pl.pallas_call(
        body, name=name, grid=grid,
        in_specs=[pl.BlockSpec(b, m) for (_, b, m) in list(ins) + dep_specs],
        out_specs=[pl.BlockSpec(b, m) for (_, _, b, m) in outs],
        out_shape=[jax.ShapeDtypeStruct(s, d) for (s, d, _, _) in outs],
        compiler_params=pltpu.CompilerParams(
            dimension_semantics=(("arbitrary",) if acc else ("parallel",)) * len(grid)),
    )(*[a for (a, _, _) in list(ins) + dep_specs])


def _position():
    x, y, c = lax.axis_index("x"), lax.axis_index("y"), lax.axis_index("c")
    chips = [(1 - x, y), (x, 1 - y), (1 - x, 1 - y)]
    return x, y, c, chips


def _all_gather(name, shards):
    n = len(shards)

    def body(*refs):
        xs, outs = refs[:n], refs[n:2 * n]
        send_sems, recv_sems, local_sems = refs[2 * n:]
        x, y, c, chips = _position()
        me, sibling = (x, y, c), (x, y, 1 - c)

        def copy(a, k, block, to, src=None):
            dst = outs[a].at[4 * block[0] + 2 * block[1] + block[2]]
            return pltpu.make_async_remote_copy(
                src_ref=dst if src is None else src, dst_ref=dst,
                send_sem=send_sems.at[a, k], recv_sem=recv_sems.at[a, k],
                device_id=to, device_id_type=MESH)

        mine = [pltpu.make_async_copy(xs[a], outs[a].at[4 * x + 2 * y + c], local_sems.at[a]) for a in range(n)]
        for cp in mine:
            cp.start()
        first = []
        for a in range(n):
            first.append(copy(a, 0, me, sibling, src=xs[a]))
            first += [copy(a, 1 + j, me, (*chip, c), src=xs[a]) for j, chip in enumerate(chips)]
        for cp in first:
            cp.start()
        passed = []
        for a in range(n):
            for j, chip in enumerate(chips):
                copy(a, 1 + j, (*chip, c), me).wait_recv()
                cp = copy(a, 4 + j, (*chip, c), sibling)
                cp.start()
                passed.append(cp)
        for a in range(n):
            copy(a, 0, sibling, me).wait_recv()
            for j, chip in enumerate(chips):
                copy(a, 4 + j, (*chip, 1 - c), me).wait_recv()
        for cp in first + passed:
            cp.wait_send()
        for cp in mine:
            cp.wait()

    return pl.pallas_call(
        body, name=name,
        out_shape=[jax.ShapeDtypeStruct((NDEV,) + s.shape, s.dtype) for s in shards],
        in_specs=[HBM] * n, out_specs=[HBM] * n,
        scratch_shapes=[pltpu.SemaphoreType.DMA((n, 7)), pltpu.SemaphoreType.DMA((n, 7)),
                        pltpu.SemaphoreType.DMA((n,))],
    )(*shards)


SEM = pl.BlockSpec(memory_space=pltpu.SEMAPHORE)
EFFECT = pltpu.SideEffectType.DATAFLOW_SIDE_EFFECTING


def _in_hbm(v):
    return pltpu.with_memory_space_constraint(v, pltpu.HBM)


def _split_start(name, srcs, lands, make_copies, n_per, after=()):
    n, nb = len(srcs), len(srcs) + len(lands)
    n_sems = len(lands) * n_per
    after = list(after)

    def body(*refs):
        send_sems, recv_sems = refs[nb + len(after)], refs[nb + len(after) + 1]
        for cp in make_copies(refs[:n], refs[n:nb], send_sems, recv_sems):
            cp.start()
        refs[-1][...] = jnp.zeros_like(refs[-1])

    outs = pl.pallas_call(
        body, name=name,
        out_shape=(pltpu.SemaphoreType.DMA((n_sems,)), pltpu.SemaphoreType.DMA((n_sems,)),
                   *[pltpu.HBM(v.shape, v.dtype) for v in list(srcs) + list(lands)],
                   jax.ShapeDtypeStruct((SUBLANE, LANE), F32)),
        in_specs=[HBM] * nb + [pl.BlockSpec(memory_space=pl.ANY)] * len(after),
        out_specs=(SEM, SEM, *[HBM] * nb, pl.BlockSpec(memory_space=pltpu.VMEM)),
        input_output_aliases={i: 2 + i for i in range(nb)},
        compiler_params=pltpu.CompilerParams(has_side_effects=EFFECT),
    )(*[_in_hbm(v) for v in list(srcs) + list(lands)], *after)
    return outs[0], outs[1], list(outs[2:2 + n]), list(outs[2 + n:2 + nb]), outs[-1]


def _split_wait(name, started, make_copies, after):
    send_sems, recv_sems, srcs, lands, _ = started
    n, nb = len(srcs), len(srcs) + len(lands)

    def body(*refs):
        for cp in make_copies(refs[:n], refs[n:nb], refs[nb], refs[nb + 1]):
            cp.wait_send()
            cp.wait_recv()

    order = [] if after is None else list(after) if isinstance(after, (list, tuple)) else [after]
    outs = pl.pallas_call(
        body, name=name,
        out_shape=tuple(pltpu.HBM(v.shape, v.dtype) for v in srcs + lands),
        in_specs=[HBM] * nb + [SEM, SEM] + [pl.BlockSpec(memory_space=pl.ANY)] * len(order),
        out_specs=tuple([HBM] * nb),
        input_output_aliases={i: i for i in range(nb)},
        compiler_params=pltpu.CompilerParams(has_side_effects=EFFECT),
    )(*srcs, *lands, send_sems, recv_sems, *order)
    return list(outs[:n]), list(outs[n:])


def _gather_first_copies(xs, lands, send_sems, recv_sems):
    x, y, c, _ = _position()
    copies = []
    for a in range(len(xs)):
        for k, peer in enumerate([(x, y, 1 - c), (1 - x, y, c), (x, 1 - y, c)]):
            copies.append(pltpu.make_async_remote_copy(
                src_ref=xs[a], dst_ref=lands[a].at[4 * x + 2 * y + c],
                send_sem=send_sems.at[3 * a + k], recv_sem=recv_sems.at[3 * a + k], device_id=peer, device_id_type=MESH))
    return copies


def _gather_second_copies(xs, lands, send_sems, recv_sems):
    x, y, c, _ = _position()
    copies = []
    for a in range(len(lands)):
        rows = lands[a].shape[1]
        unit = SUBLANE * (4 // jnp.dtype(lands[a].dtype).itemsize)
        half = rows // 2 // unit * unit or rows
        parts = [((1 - x, y), (x, 1 - y), pl.ds(0, half))]
        if half < rows:
            parts.append(((x, 1 - y), (1 - x, y), pl.ds(half, rows - half)))
        for k, (block, to, rs) in enumerate(parts):
            ref = lands[a].at[4 * block[0] + 2 * block[1] + c, rs]
            copies.append(pltpu.make_async_remote_copy(
                src_ref=ref, dst_ref=ref, send_sem=send_sems.at[2 * a + k], recv_sem=recv_sems.at[2 * a + k],
                device_id=(*to, c), device_id_type=MESH))
    return copies


def _chips_copies(ps, lands, send_sems, recv_sems):
    x, y, c, chips = _position()
    copies = []
    for a in range(len(ps)):
        for j, chip in enumerate(chips):
            copies.append(pltpu.make_async_remote_copy(
                src_ref=ps[a].at[2 * chip[0] + chip[1]], dst_ref=lands[a].at[j],
                send_sem=send_sems.at[3 * a + j], recv_sem=recv_sems.at[3 * a + j], device_id=(*chip, c),
                device_id_type=MESH))
    return copies


def _sibling_copies(gs, lands, send_sems, recv_sems):
    x, y, c, _ = _position()
    copies = []
    for a in range(len(gs)):
        for q in range(4):
            copies.append(pltpu.make_async_remote_copy(
                src_ref=gs[a].at[2 * q + 1 - c], dst_ref=lands[a].at[q],
                send_sem=send_sems.at[4 * a + q], recv_sem=recv_sems.at[4 * a + q],
                device_id=(x, y, 1 - c), device_id_type=MESH))
    return copies


def _everyone_copies(xs, lands, send_sems, recv_sems):
    x, y, c, _ = _position()
    flip = lambda v, bit: 1 - v if bit else v
    copies = []
    for a in range(len(xs)):
        for k in range(1, NDEV):
            copies.append(pltpu.make_async_remote_copy(
                src_ref=xs[a], dst_ref=lands[a].at[4 * x + 2 * y + c],
                send_sem=send_sems.at[7 * a + k - 1], recv_sem=recv_sems.at[7 * a + k - 1],
                device_id=(flip(x, k & 4), flip(y, k & 2), flip(c, k & 1)), device_id_type=MESH))
    return copies


def _row_chunks(rows, dtype):
    unit = SUBLANE * (4 // jnp.dtype(dtype).itemsize)
    units = rows // unit
    if rows % unit or units < 2:
        return [(0, rows)]
    k = min(DMA_CHUNKS, units)
    sizes = [(units // k + (1 if i < units % k else 0)) * unit for i in range(k)]
    return [(sum(sizes[:i]), sz) for i, sz in enumerate(sizes)]


def _gather_forward(name, lands, after=()):
    n = len(lands)
    after = list(after)

    def body(*refs):
        ins, outs = refs[:n], refs[n + len(after):2 * n + len(after)]
        send_sems, recv_sems = refs[2 * n + len(after):]
        x, y, c, chips = _position()
        whole, chunks = [], []
        for a in range(n):
            rows = _row_chunks(ins[a].shape[1], ins[a].dtype)
            for j, chip in enumerate(chips):
                slot = 4 * chip[0] + 2 * chip[1]

                def to_sibling(src, dst):
                    return pltpu.make_async_remote_copy(
                        src_ref=src, dst_ref=dst, send_sem=send_sems.at[a, j], recv_sem=recv_sems.at[a, j],
                        device_id=(x, y, 1 - c), device_id_type=MESH)

                whole.append(to_sibling(ins[a].at[slot + c], outs[a].at[slot + 1 - c]))
                chunks += [to_sibling(ins[a].at[slot + c, pl.ds(r0, nr)], outs[a].at[slot + c, pl.ds(r0, nr)])
                           for r0, nr in rows]
        for cp in chunks:
            cp.start()
        for cp in whole:
            cp.wait()

    return pl.pallas_call(
        body, name=name,
        out_shape=[jax.ShapeDtypeStruct(l.shape, l.dtype) for l in lands],
        in_specs=[HBM] * n + [pl.BlockSpec(memory_space=pl.ANY)] * len(after), out_specs=[HBM] * n,
        input_output_aliases={a: a for a in range(n)},
        scratch_shapes=[pltpu.SemaphoreType.DMA((n, 3)), pltpu.SemaphoreType.DMA((n, 3))],
    )(*lands, *after)


def _sum_sibling(name, g, land, c_arr):
    _, R, C = g.shape
    tr = _pick(R, 512, SUBLANE)

    def body(c_ref, g_ref, l_ref, o_ref):
        o_ref[...] = (g_ref[...].astype(F32) + l_ref[...].astype(F32)).astype(o_ref.dtype)

    return pl.pallas_call(
        body, name=name,
        grid_spec=pltpu.PrefetchScalarGridSpec(
            num_scalar_prefetch=1, grid=(4, R // tr),
            in_specs=[pl.BlockSpec((None, tr, C), lambda q, i, cr: (2 * q + cr[0], i, 0)),
                      pl.BlockSpec((None, tr, C), lambda q, i, cr: (q, i, 0))],
            out_specs=pl.BlockSpec((None, tr, C), lambda q, i, cr: (q, i, 0))),
        out_shape=jax.ShapeDtypeStruct((4, R, C), g.dtype),
        compiler_params=pltpu.CompilerParams(dimension_semantics=("parallel", "parallel")),
    )(c_arr, g, land)


def _adamw(w, g, m, v):
    m = ADAM_B1 * m + (1.0 - ADAM_B1) * g
    v = ADAM_B2 * v + (1.0 - ADAM_B2) * (g * g)
    m_hat = m / (1.0 - ADAM_B1 ** ADAM_STEP)
    v_hat = v / (1.0 - ADAM_B2 ** ADAM_STEP)
    delta = -ADAM_LR * (m_hat / (jnp.sqrt(v_hat) + ADAM_EPS) + ADAM_WD * w)
    return delta, m, v


def _finish_sharded(name, sums, land, q_arr, w, m, v):
    R, C = w.shape
    tr = _pick(R, 512, SUBLANE)
    tc = _pick(C, max(LANE, EW_BLOCK // tr), LANE)

    def body(q_ref, p_ref, l_ref, w_ref, m_ref, v_ref, g_out, d_out, m_out, v_out):
        g = p_ref[...].astype(F32)
        for j in range(3):
            g = g + l_ref[j].astype(F32)
        d, mn, vn = _adamw(w_ref[...], g, m_ref[...], v_ref[...])
        g_out[...] = g
        d_out[...] = d
        m_out[...] = mn
        v_out[...] = vn

    blk = pl.BlockSpec((tr, tc), lambda i, j, qr: (i, j))
    return pl.pallas_call(
        body, name=name,
        grid_spec=pltpu.PrefetchScalarGridSpec(
            num_scalar_prefetch=1, grid=(R // tr, C // tc),
            in_specs=[pl.BlockSpec((None, tr, tc), lambda i, j, qr: (qr[0], i, j)),
                      pl.BlockSpec((3, tr, tc), lambda i, j, qr: (0, i, j)), blk, blk, blk],
            out_specs=[blk] * 4),
        out_shape=[jax.ShapeDtypeStruct((R, C), F32)] * 4,
        compiler_params=pltpu.CompilerParams(dimension_semantics=("parallel", "parallel")),
    )(q_arr, sums, land, w, m, v)


def _finish_replicated(name, gathered, w, m, v):
    _, R, C = gathered.shape
    tr = _pick(R, 256, SUBLANE)

    def fn(gv, wv, mv, vv):
        g = gv[0]
        for d in range(1, NDEV):
            g = g + gv[d]
        dl, mn, vn = _adamw(wv, g, mv, vv)
        return g, dl, mn, vn

    row = ((tr, C), lambda i: (i, 0))
    return _ew(name, (R // tr,),
               [(gathered, (NDEV, tr, C), lambda i: (0, i, 0)), (w,) + row, (m,) + row, (v,) + row],
               [((R, C), F32) + row] * 4, fn)


def _rms_fwd(name, h, g, deps=()):
    S, D = h.shape
    tr = _pick(S, 256, SUBLANE)

    def fn(hv, gv):
        r = lax.rsqrt(jnp.mean(hv * hv, axis=-1, keepdims=True) + EPS)
        return hv * r * gv, r

    return _ew(name, (S // tr,),
               [(h, (tr, D), lambda i: (i, 0)), (g, (1, D), lambda i: (0, 0))],
               [((S, D), BF, (tr, D), lambda i: (i, 0)), ((S, 1), F32, (tr, 1), lambda i: (i, 0))], fn, deps=deps)


def _rms_bwd(name, du, h, r, g, dres, scale):
    S, D = h.shape
    tr = _pick(S, 256, SUBLANE)

    def fn(duv, hv, rv, gv, drv):
        xn = hv * rv
        dxn = duv * gv
        dh = drv + rv * (dxn - xn * jnp.mean(dxn * xn, axis=-1, keepdims=True))
        return dh, scale * dh, jnp.sum(duv * xn, axis=0, keepdims=True)

    row = ((tr, D), lambda i: (i, 0))
    return _ew(name, (S // tr,),
               [(du,) + row, (h,) + row, (r, (tr, 1), lambda i: (i, 0)), (g, (1, D), lambda i: (0, 0)), (dres,) + row],
               [((S, D), F32) + row, ((S, D), BF) + row, ((1, D), F32, (1, D), lambda i: (0, 0))], fn, acc=(2,))


def _loss_head(name, h, g, target):
    S, D = h.shape
    tr = _pick(S, 256, SUBLANE)

    def fn(hv, gv, tv):
        r = lax.rsqrt(jnp.mean(hv * hv, axis=-1, keepdims=True) + EPS)
        xn = hv * r
        diff = xn * gv - tv
        loss = 0.5 * jnp.sum(jnp.mean(diff * diff, axis=-1, keepdims=True))
        dout = diff / D
        dxn = dout * gv
        dh = r * (dxn - xn * jnp.mean(dxn * xn, axis=-1, keepdims=True))
        return (jnp.zeros((1, LANE), F32) + loss, dh, 0.5 * dh, jnp.sum(dout * xn, axis=0, keepdims=True))

    row = ((tr, D), lambda i: (i, 0))
    return _ew(name, (S // tr,),
               [(h,) + row, (g, (1, D), lambda i: (0, 0)), (target,) + row],
               [((1, LANE), F32, (1, LANE), lambda i: (0, 0)), ((S, D), F32) + row, ((S, D), BF) + row,
                ((1, D), F32, (1, D), lambda i: (0, 0))], fn, acc=(0, 3))


def _ffn_fwd(tag, u, h, wg, fetch_wu, fetch_wd):
    S, D = u.shape
    Fs = wg.shape[1]
    tm, tk = _pick(S, M_TILE, SUBLANE), _pick(D, K_TILE, LANE)
    act = ((NDEV, S, Fs), BF, (None, tm, Fs), lambda b, i, j, k: (b, i, 0))
    lhs = (u, (tm, tk), lambda b, i, j, k: (i, k))
    rhs = lambda w: (w, (None, Fs, tk), lambda b, i, j, k: (b, 0, k))
    (gt,) = _mm(tag + "_gate", (NDEV, S // tm, 1, D // tk), [lhs, rhs(wg)], [(0, 1, NT, 0)], [], [act],
                lambda accs, sides: accs, [(tm, Fs)])
    wu = fetch_wu(gt)

    def up_epilogue(accs, sides):
        g = sides[0].astype(F32)
        return accs[0], g * _sigmoid(g) * accs[0]

    up, a = _mm(tag + "_up", (NDEV, S // tm, 1, D // tk), [lhs, rhs(wu)], [(0, 1, NT, 0)],
                [(gt, (None, tm, Fs), lambda b, i, j, k: (b, i, 0))], [act, act], up_epilogue, [(tm, Fs)])
    wd = fetch_wd(a)
    tn = _pick(D, 1024, LANE)
    (hn,) = _mm(
        tag + "_down", (1, S // tm, D // tn, NDEV // K_SHARDS),
        [(a, (K_SHARDS, tm, Fs), lambda b, i, j, k: (k, i, 0)), (wd, (K_SHARDS, Fs, tn), lambda b, i, j, k: (k, 0, j))],
        [(0, 1, NN, 0)], [(h, (tm, tn), lambda b, i, j, k: (i, j))],
        [((S, D), F32, (tm, tn), lambda b, i, j, k: (i, j))],
        lambda accs, sides: [sides[0] + 0.5 * accs[0]], [(tm, tn)])
    return hn, (gt, up, a), wu, wd


def _ffn_bwd(tag, dhs, u, saved, wg, wu, wd, deps=()):
    gt, up, a = saved
    S, D = u.shape
    Fs = wg.shape[1]
    tm, tk = _pick(S, M_TILE, SUBLANE), _pick(D, K_TILE, LANE)
    act_in = lambda arr: (arr, (None, tm, Fs), lambda b, i, j, k: (b, i, 0))
    act_out = ((NDEV, S, Fs), BF, (None, tm, Fs), lambda b, i, j, k: (b, i, 0))

    def act_epilogue(accs, sides):
        da = accs[0]
        gtv, upv = sides[0].astype(F32), sides[1].astype(F32)
        sg = _sigmoid(gtv)
        return da * upv * sg * (1.0 + gtv * (1.0 - sg)), da * gtv * sg

    dgt, dup = _mm(
        tag + "_dact", (NDEV, S // tm, 1, D // tk),
        [(dhs, (tm, tk), lambda b, i, j, k: (i, k)), (wd, (None, Fs, tk), lambda b, i, j, k: (b, 0, k))],
        [(0, 1, NT, 0)], [act_in(gt), act_in(up)], [act_out, act_out], act_epilogue, [(tm, Fs)], deps=deps)

    ts = _pick(S, K_TILE, SUBLANE)
    tn = _pick(D, 1024, LANE)
    wgrad = ((NDEV, Fs, D), BF, (None, Fs, tn), lambda b, i, j, k: (b, 0, j))
    tok = lambda arr: (arr, (None, ts, Fs), lambda b, i, j, k: (b, k, 0))

    def grad_down(deps=()):
        return _mm(
            tag + "_dwd", (NDEV, 1, D // tn, S // ts),
            [tok(a), (dhs, (ts, tn), lambda b, i, j, k: (k, j))],
            [(0, 1, TN, 0)], [], [wgrad], lambda accs, sides: accs, [(Fs, tn)], deps=deps)[0]

    def grad_gate_up(deps=()):
        return _mm(
            tag + "_dwgu", (NDEV, 1, D // tn, S // ts),
            [tok(dgt), tok(dup), (u, (ts, tn), lambda b, i, j, k: (k, j))],
            [(0, 2, TN, 0), (1, 2, TN, 1)], [], [wgrad, wgrad], lambda accs, sides: accs, [(Fs, tn), (Fs, tn)],
            deps=deps)

    def du(deps=()):
        return _mm(
            tag + "_du", (1, S // tm, D // tn, NDEV // K_SHARDS),
            [(dgt, (K_SHARDS, tm, Fs), lambda b, i, j, k: (k, i, 0)), (dup, (K_SHARDS, tm, Fs), lambda b, i, j, k: (k, i, 0)),
             (wg, (K_SHARDS, Fs, tn), lambda b, i, j, k: (k, 0, j)), (wu, (K_SHARDS, Fs, tn), lambda b, i, j, k: (k, 0, j))],
            [(0, 2, NN, 0), (1, 3, NN, 0)], [], [((S, D), F32, (tm, tn), lambda b, i, j, k: (i, j))],
            lambda accs, sides: accs, [(tm, tn)], deps=deps)[0]

    return grad_down, grad_gate_up, du


def _ssm_params(lam_re, lam_im, log_dt, b_re, b_im, c_re, c_im):
    G, N = lam_re.shape
    C = b_re.shape[2]
    lam_re = jnp.minimum(lam_re, -1e-4)
    dt = jnp.exp(log_dt)[:, None]
    mag = jnp.exp(lam_re * dt)
    a_re = mag * jnp.cos(lam_im * dt)
    a_im = mag * jnp.sin(lam_im * dt)
    den = lam_re * lam_re + lam_im * lam_im
    p = a_re - 1.0
    f_re = ((p * lam_re + a_im * lam_im) / den)[:, :, None]
    f_im = ((a_im * lam_re - p * lam_im) / den)[:, :, None]
    bb_re = f_re * b_re - f_im * b_im
    bb_im = f_re * b_im + f_im * b_re
    gpt = LANE // C
    tiles = G // gpt
    eye = jnp.eye(gpt, dtype=F32)

    def bd(bb):
        return jnp.einsum("bgnc,gh->bgchn", bb.reshape(tiles, gpt, N, C), eye).reshape(tiles, gpt * C, gpt * N)

    def cd(cc):
        return jnp.einsum("bgcn,gh->bgnhc", cc.reshape(tiles, gpt, C, N), eye).reshape(tiles, gpt * N, gpt * C)

    rows = G * N // LANE
    return (a_re.reshape(rows, LANE), a_im.reshape(rows, LANE), bd(bb_re), bd(bb_im), cd(c_re), cd(-c_im))


def _tile_states(ref3, b, per):
    return jnp.concatenate([ref3[:, per * b + r, :] for r in range(per)], axis=1).astype(BF)


def _ssm_spread(name, x, m_re, m_im, dims, rows):
    S, W = x.shape
    tiles = m_re.shape[0]
    tch, per = W // tiles, rows // tiles
    tq = _pick(S, 256, SUBLANE)

    def body(x_ref, mre_ref, mim_ref, ore_ref, oim_ref):
        for b in range(tiles):
            xb = x_ref[:, b * tch:(b + 1) * tch]
            for m_ref, o_ref in ((mre_ref, ore_ref), (mim_ref, oim_ref)):
                val = lax.dot_general(xb, m_ref[b], (dims, ((), ())), preferred_element_type=F32)
                for r in range(per):
                    o_ref[:, per * b + r, :] = val[:, r * LANE:(r + 1) * LANE]

    whole = lambda m: pl.BlockSpec(m.shape, lambda i: (0, 0, 0))
    st = pl.BlockSpec((tq, rows, LANE), lambda i: (i, 0, 0))
    return pl.pallas_call(
        body, name=name, grid=(S // tq,),
        in_specs=[pl.BlockSpec((tq, W), lambda i: (i, 0)), whole(m_re), whole(m_im)], out_specs=[st, st],
        out_shape=[jax.ShapeDtypeStruct((S, rows, LANE), F32)] * 2,
        compiler_params=pltpu.CompilerParams(dimension_semantics=("parallel",)),
    )(x, m_re, m_im)


def _ssm_collect(name, z_re3, z_im3, m_re, m_im, dims, side, gain, epilogue, out_dtypes):
    S, rows, _ = z_re3.shape
    tiles = m_re.shape[0]
    W = side.shape[1]
    tch, per = W // tiles, rows // tiles
    tq = _pick(S, 256, SUBLANE)
    n_out = len(out_dtypes)

    def body(zre_ref, zim_ref, mre_ref, mim_ref, side_ref, gain_ref, *out_refs):
        for b in range(tiles):
            cols = slice(b * tch, (b + 1) * tch)
            zre, zim = _tile_states(zre_ref, b, per), _tile_states(zim_ref, b, per)
            acc = lax.dot_general(zre, mre_ref[b], (dims, ((), ())), preferred_element_type=F32)
            acc = acc + lax.dot_general(zim, mim_ref[b], (dims, ((), ())), preferred_element_type=F32)
            for o, v in zip(out_refs[:n_out], epilogue(acc, side_ref[:, cols], gain_ref[:, cols])):
                o[:, cols] = v.astype(o.dtype)
            out_refs[n_out][:, b * per * LANE:(b + 1) * per * LANE] = zre
            out_refs[n_out + 1][:, b * per * LANE:(b + 1) * per * LANE] = zim

    whole = lambda m: pl.BlockSpec(m.shape, lambda i: (0, 0, 0))
    st = pl.BlockSpec((tq, rows, LANE), lambda i: (i, 0, 0))
    ch = pl.BlockSpec((tq, W), lambda i: (i, 0))
    flat = pl.BlockSpec((tq, rows * LANE), lambda i: (i, 0))
    return pl.pallas_call(
        body, name=name, grid=(S // tq,),
        in_specs=[st, st, whole(m_re), whole(m_im), ch, pl.BlockSpec((1, W), lambda i: (0, 0))],
        out_specs=[ch] * n_out + [flat, flat],
        out_shape=[jax.ShapeDtypeStruct((S, W), dt) for dt in out_dtypes]
        + [jax.ShapeDtypeStruct((S, rows * LANE), BF)] * 2,
        compiler_params=pltpu.CompilerParams(dimension_semantics=("parallel",)),
    )(z_re3, z_im3, m_re, m_im, side, gain)


def _scan_fwd(bu_re, bu_im, a_re, a_im):
    S, R, _ = bu_re.shape
    tc = _pick(S, 256, SUBLANE)

    def body(bre, bim, are, aim, sre, sim, carry):
        @pl.when(pl.program_id(0) == 0)
        def _():
            carry[...] = jnp.zeros_like(carry)

        ar, ai = are[...], aim[...]

        def step(t, c):
            pr, pi = c
            nr = ar * pr - ai * pi + bre[t]
            ni = ar * pi + ai * pr + bim[t]
            sre[t] = nr
            sim[t] = ni
            return nr, ni

        pr, pi = lax.fori_loop(0, tc, step, (carry[0], carry[1]), unroll=8)
        carry[0] = pr
        carry[1] = pi

    blk = pl.BlockSpec((tc, R, LANE), lambda i: (i, 0, 0))
    par = pl.BlockSpec((R, LANE), lambda i: (0, 0))
    return pl.pallas_call(
        body, name="ssm_scan_fwd", grid=(S // tc,),
        in_specs=[blk, blk, par, par], out_specs=[blk, blk],
        out_shape=[jax.ShapeDtypeStruct((S, R, LANE), F32)] * 2,
        scratch_shapes=[pltpu.VMEM((2, R, LANE), F32)],
        compiler_params=pltpu.CompilerParams(dimension_semantics=("arbitrary",)),
    )(bu_re, bu_im, a_re, a_im)


def _scan_bwd(ds_re, ds_im, s_re, s_im, a_re, a_im):
    S, R, _ = ds_re.shape
    tc = _pick(S, 256, SUBLANE)
    nc = S // tc

    def body(dre, dim_, sre, sim, are, aim, lre, lim, dar, dai, carry):
        @pl.when(pl.program_id(0) == 0)
        def _():
            carry[...] = jnp.zeros_like(carry)
            dar[...] = jnp.zeros_like(dar)
            dai[...] = jnp.zeros_like(dai)

        ar, ai = are[...], aim[...]

        def step(tt, c):
            t = tc - 1 - tt
            lr, li, gr, gi = c
            sr, si = sre[t], sim[t]
            gr = gr + lr * sr + li * si
            gi = gi + li * sr - lr * si
            nlr = dre[t] + ar * lr + ai * li
            nli = dim_[t] + ar * li - ai * lr
            lre[t] = nlr
            lim[t] = nli
            return nlr, nli, gr, gi

        lr, li, gr, gi = lax.fori_loop(0, tc, step, (carry[0], carry[1], dar[...], dai[...]), unroll=8)
        carry[0] = lr
        carry[1] = li
        dar[...] = gr
        dai[...] = gi

    blk = pl.BlockSpec((tc, R, LANE), lambda i: (nc - 1 - i, 0, 0))
    par = pl.BlockSpec((R, LANE), lambda i: (0, 0))
    return pl.pallas_call(
        body, name="ssm_scan_bwd", grid=(nc,),
        in_specs=[blk, blk, blk, blk, par, par], out_specs=[blk, blk, par, par],
        out_shape=[jax.ShapeDtypeStruct((S, R, LANE), F32)] * 2 + [jax.ShapeDtypeStruct((R, LANE), F32)] * 2,
        scratch_shapes=[pltpu.VMEM((2, R, LANE), F32)],
        compiler_params=pltpu.CompilerParams(dimension_semantics=("arbitrary",)),
    )(ds_re, ds_im, s_re, s_im, a_re, a_im)


def _shift_down(z, k):
    t = lax.broadcasted_iota(I32, z.shape, 0)
    return jnp.where(t >= k, pltpu.roll(z, k, 0), 0.0)


def _shift_up(z, k):
    n = z.shape[0]
    t = lax.broadcasted_iota(I32, z.shape, 0)
    return jnp.where(t < n - k, pltpu.roll(z, n - k, 0), 0.0)


def _conv_fwd(proj, cw, cb):
    _, S, W = proj.shape
    ct = _pick(W, 256, LANE)

    def fn(bg, cg, val, w, b):
        z = cg * val
        conv = b + w[0:1] * _shift_down(z, 2) + w[1:2] * _shift_down(z, 1) + w[2:3] * z
        return bg * conv, conv

    sl = lambda s: (proj, (None, S, ct), lambda j, s=s: (s, 0, j))
    col = ((S, ct), lambda j: (0, j))
    return _ew("conv_fwd", (W // ct,),
               [sl(1), sl(2), sl(3), (cw, (3, ct), lambda j: (0, j)), (cb, (1, ct), lambda j: (0, j))],
               [((S, W), BF) + col, ((S, W), F32) + col], fn)


def _conv_bwd(dyb, proj, conv, cw):
    _, S, W = proj.shape
    ct = _pick(W, 256, LANE)

    def fn(dy, bg, cg, val, cv, w):
        z = cg * val
        z1, z2 = _shift_down(z, 1), _shift_down(z, 2)
        dconv = dy * bg
        dz = w[2:3] * dconv + w[1:2] * _shift_up(dconv, 1) + w[0:1] * _shift_up(dconv, 2)
        dw = jnp.concatenate([jnp.sum(dconv * z2, axis=0, keepdims=True), jnp.sum(dconv * z1, axis=0, keepdims=True),
                              jnp.sum(dconv * z, axis=0, keepdims=True)], axis=0)
        return dy * cv, dz * val, dz * cg, dw, jnp.sum(dconv, axis=0, keepdims=True)

    sl = lambda s: (proj, (None, S, ct), lambda j, s=s: (s, 0, j))
    col = ((S, ct), lambda j: (0, j))
    return _ew("conv_bwd", (W // ct,),
               [(dyb,) + col, sl(1), sl(2), sl(3), (conv,) + col, (cw, (3, ct), lambda j: (0, j))],
               [((S, W), BF) + col, ((S, W), BF) + col, ((S, W), BF) + col,
                ((3, W), F32, (3, ct), lambda j: (0, j)), ((1, W), F32, (1, ct), lambda j: (0, j))], fn)


def _plain(accs, sides):
    return accs


def kernel(x, ffn1_norm, ffn1_w_gate, ffn1_w_up, ffn1_w_down, mix_norm, w_in, ssm_lambda_re, ssm_lambda_im, ssm_log_dt, ssm_b_re, ssm_b_im, ssm_c_re, ssm_c_im, ssm_d, ssm_w_glu, ssm_b_glu, ssm_w_out, conv_w, conv_b, conv_w_out, w_o, ffn2_norm, ffn2_w_gate, ffn2_w_up, ffn2_w_down, final_norm, loss_target, m_ffn1_norm, m_ffn1_w_gate, m_ffn1_w_up, m_ffn1_w_down, m_mix_norm, m_w_in, m_ssm_lambda_re, m_ssm_lambda_im, m_ssm_log_dt, m_ssm_b_re, m_ssm_b_im, m_ssm_c_re, m_ssm_c_im, m_ssm_d, m_ssm_w_glu, m_ssm_b_glu, m_ssm_w_out, m_conv_w, m_conv_b, m_conv_w_out, m_w_o, m_ffn2_norm, m_ffn2_w_gate, m_ffn2_w_up, m_ffn2_w_down, m_final_norm, v_ffn1_norm, v_ffn1_w_gate, v_ffn1_w_up, v_ffn1_w_down, v_mix_norm, v_w_in, v_ssm_lambda_re, v_ssm_lambda_im, v_ssm_log_dt, v_ssm_b_re, v_ssm_b_im, v_ssm_c_re, v_ssm_c_im, v_ssm_d, v_ssm_w_glu, v_ssm_b_glu, v_ssm_w_out, v_conv_w, v_conv_b, v_conv_w_out, v_w_o, v_ffn2_norm, v_ffn2_w_gate, v_ffn2_w_up, v_ffn2_w_down, v_final_norm):
    P = dict(ffn1_norm=ffn1_norm, ffn1_w_gate=ffn1_w_gate, ffn1_w_up=ffn1_w_up, ffn1_w_down=ffn1_w_down, mix_norm=mix_norm, w_in=w_in, ssm_lambda_re=ssm_lambda_re, ssm_lambda_im=ssm_lambda_im, ssm_log_dt=ssm_log_dt, ssm_b_re=ssm_b_re, ssm_b_im=ssm_b_im, ssm_c_re=ssm_c_re, ssm_c_im=ssm_c_im, ssm_d=ssm_d, ssm_w_glu=ssm_w_glu, ssm_b_glu=ssm_b_glu, ssm_w_out=ssm_w_out, conv_w=conv_w, conv_b=conv_b, conv_w_out=conv_w_out, w_o=w_o, ffn2_norm=ffn2_norm, ffn2_w_gate=ffn2_w_gate, ffn2_w_up=ffn2_w_up, ffn2_w_down=ffn2_w_down, final_norm=final_norm)
    M = dict(ffn1_norm=m_ffn1_norm, ffn1_w_gate=m_ffn1_w_gate, ffn1_w_up=m_ffn1_w_up, ffn1_w_down=m_ffn1_w_down, mix_norm=m_mix_norm, w_in=m_w_in, ssm_lambda_re=m_ssm_lambda_re, ssm_lambda_im=m_ssm_lambda_im, ssm_log_dt=m_ssm_log_dt, ssm_b_re=m_ssm_b_re, ssm_b_im=m_ssm_b_im, ssm_c_re=m_ssm_c_re, ssm_c_im=m_ssm_c_im, ssm_d=m_ssm_d, ssm_w_glu=m_ssm_w_glu, ssm_b_glu=m_ssm_b_glu, ssm_w_out=m_ssm_w_out, conv_w=m_conv_w, conv_b=m_conv_b, conv_w_out=m_conv_w_out, w_o=m_w_o, ffn2_norm=m_ffn2_norm, ffn2_w_gate=m_ffn2_w_gate, ffn2_w_up=m_ffn2_w_up, ffn2_w_down=m_ffn2_w_down, final_norm=m_final_norm)
    V = dict(ffn1_norm=v_ffn1_norm, ffn1_w_gate=v_ffn1_w_gate, ffn1_w_up=v_ffn1_w_up, ffn1_w_down=v_ffn1_w_down, mix_norm=v_mix_norm, w_in=v_w_in, ssm_lambda_re=v_ssm_lambda_re, ssm_lambda_im=v_ssm_lambda_im, ssm_log_dt=v_ssm_log_dt, ssm_b_re=v_ssm_b_re, ssm_b_im=v_ssm_b_im, ssm_c_re=v_ssm_c_re, ssm_c_im=v_ssm_c_im, ssm_d=v_ssm_d, ssm_w_glu=v_ssm_w_glu, ssm_b_glu=v_ssm_b_glu, ssm_w_out=v_ssm_w_out, conv_w=v_conv_w, conv_b=v_conv_b, conv_w_out=v_conv_w_out, w_o=v_w_o, ffn2_norm=v_ffn2_norm, ffn2_w_gate=v_ffn2_w_gate, ffn2_w_up=v_ffn2_w_up, ffn2_w_down=v_ffn2_w_down, final_norm=v_final_norm)
    names = list(P)
    sharded = ["ffn1_w_gate", "ffn1_w_up", "ffn1_w_down", "w_in", "ssm_w_glu", "ssm_w_out", "conv_w_out", "w_o",
               "ffn2_w_gate", "ffn2_w_up", "ffn2_w_down"]
    replicated = [n for n in names if n not in sharded and n != "conv_w"]

    S, D = x.shape[1], x.shape[2]
    W = ssm_d.shape[0]
    Dc = D // NDEV
    G, N = ssm_lambda_re.shape
    rows = G * N // LANE
    xh = x.reshape(S, D)
    target = loss_target.reshape(S, D)
    xi, yi, ci = lax.axis_index("x"), lax.axis_index("y"), lax.axis_index("c")
    c_arr = jnp.reshape(ci, (1,)).astype(I32)
    q_arr = jnp.reshape(2 * xi + yi, (1,)).astype(I32)
    row = lambda v: v.reshape(1, -1)

    transposed = ("ffn1_w_gate", "ffn1_w_up", "ffn2_w_gate", "ffn2_w_up")
    local = lambda table, n: table[n].T if n in transposed else table[n]

    groups = [sharded[0:1], sharded[1:2], sharded[2:3], ["w_in", "conv_w"], sharded[4:8],
              sharded[8:9], sharded[9:10], sharded[10:11]]
    first_begun, second_begun, chain = {}, {}, []
    me = 4 * xi + 2 * yi + ci

    def gather_first(gi):
        hold = 0.0 * chain[0][0, 0] if chain else 0.0
        srcs = [conv_w + hold if n == "conv_w" else (local(P, n) + hold).astype(BF) for n in groups[gi]]
        lands = [lax.dynamic_update_slice(lax.empty((NDEV,) + s.shape, s.dtype), s[None], (me,) + (0,) * s.ndim)
                 for s in srcs]
        first_begun[gi] = _split_start("gather_first_start_%d" % gi, srcs, lands, _gather_first_copies, 3, chain[-1:])
        chain.append(first_begun[gi][4])

    def gather_second(gi, after):
        _, lands = _split_wait("gather_first_wait_%d" % gi, first_begun[gi], _gather_first_copies, after)
        second_begun[gi] = _split_start("gather_second_start_%d" % gi, [], lands, _gather_second_copies, 2, chain[-1:])
        chain.append(second_begun[gi][4])

    def gathered(gi, after):
        _, lands = _split_wait("gather_second_wait_%d" % gi, second_begun[gi], _gather_second_copies, after)
        return _gather_forward("gather_forward_%d" % gi, lands, chain[-1:])

    def fetch(gi, seconds, firsts):
        def get(after):
            for g in seconds:
                gather_second(g, after)
            for g in firsts:
                gather_first(g)
            return gathered(gi, after)
        return get

    tm = _pick(S, M_TILE, SUBLANE)
    th = _pick(S, M_TILE // 2, SUBLANE)
    tk = _pick(D, K_TILE, LANE)
    ts = _pick(S, K_TILE, SUBLANE)
    tn = _pick(D, 1024, LANE)

    gather_first(0)
    gather_first(1)
    u1, r1 = _rms_fwd("rms1", xh, row(ffn1_norm), deps=list(chain))
    gather_second(0, u1)
    gather_first(2)
    gather_second(1, u1)
    gather_first(3)
    (wg1,) = gathered(0, u1)
    h1, ffn1_saved, wu1, wd1 = _ffn_fwd("ffn1", u1, xh, wg1, lambda after: fetch(1, [2], [4])(after)[0],
                                        lambda after: fetch(2, [3], [5])(after)[0])
    u2, r2 = _rms_fwd("rms2", h1, row(mix_norm))
    w_in_f, cw_f = fetch(3, [4], [6])(u2)
    cw = jnp.transpose(cw_f, (1, 0, 2)).reshape(3, W)
    (proj,) = _mm(
        "in_proj", (NDEV, S // tm, 1, D // tk),
        [(u2, (tm, tk), lambda b, i, j, k: (i, k)), (w_in_f, (None, tk, W), lambda b, i, j, k: (b, k, 0))],
        [(0, 1, NN, 0)], [], [((NDEV, S, W), F32, (None, tm, W), lambda b, i, j, k: (b, i, 0))], _plain, [(tm, W)])

    ssm_in = (ssm_lambda_re, ssm_lambda_im, ssm_log_dt, ssm_b_re, ssm_b_im, ssm_c_re, ssm_c_im)
    (a_re, a_im, bd_re, bd_im, cd_re, cd_imn), ssm_vjp = jax.vjp(_ssm_params, *ssm_in)
    bd_re_b, bd_im_b, cd_re_b, cd_imn_b = (t.astype(BF) for t in (bd_re, bd_im, cd_re, cd_imn))
    v_f = proj[0]
    v_bf = v_f.astype(BF)
    bu_re3, bu_im3 = _ssm_spread("ssm_bu", v_bf, bd_re_b, bd_im_b, NN, rows)
    s_re3, s_im3 = _scan_fwd(bu_re3, bu_im3, a_re, a_im)

    def y0_epilogue(acc, v_tile, d_tile):
        y0 = acc + d_tile * v_tile
        return y0, _gelu(y0)

    y0, y1, s_re_b, s_im_b = _ssm_collect("ssm_y0", s_re3, s_im3, cd_re_b, cd_imn_b, NN, v_f, row(ssm_d),
                                          y0_epilogue, [F32, BF])

    tw = _pick(W, 512, LANE)
    w_glu_f, w_so, w_co, w_o_f = fetch(4, [5], [7])(y1)
    w_glu_f = w_glu_f.reshape(W, W)
    w_o_f = w_o_f.reshape(D, D)

    def glu_epilogue(accs, sides):
        q = accs[0] + sides[1]
        return q, _gelu(sides[0]) * _sigmoid(q)

    q_pre, y2 = _mm("ssm_glu", (1, S // tm, W // tw, 1),
                    [(y1, (tm, W), lambda b, i, j, k: (i, 0)), (w_glu_f, (W, tw), lambda b, i, j, k: (0, j))],
                    [(0, 1, NN, 0)],
                    [(y0, (tm, tw), lambda b, i, j, k: (i, j)), (row(ssm_b_glu), (1, tw), lambda b, i, j, k: (0, j))],
                    [((S, W), F32, (tm, tw), lambda b, i, j, k: (i, j)), ((S, W), BF, (tm, tw), lambda b, i, j, k: (i, j))],
                    glu_epilogue, [(tm, tw)])

    yb, conv = _conv_fwd(proj, cw, row(conv_b))

    per = W // Dc
    ga_blk = (proj, (None, tm, Dc), lambda b, i, j, k: (4 + b // per, i, b % per))
    gb_blk = (proj, (None, tm, Dc), lambda b, i, j, k: (6 + b // per, i, b % per))
    dc_out = ((S, D), BF, (tm, Dc), lambda b, i, j, k: (i, b))

    def merge_epilogue(accs, sides):
        za, zb = accs
        return _sigmoid(sides[0]) * za + _sigmoid(sides[1]) * zb, za, zb

    merged, z_a, z_b = _mm(
        "mix_merge", (NDEV, S // tm, 1, 1),
        [(y2, (tm, W), lambda b, i, j, k: (i, 0)), (yb, (tm, W), lambda b, i, j, k: (i, 0)),
         (w_so, (None, W, Dc), lambda b, i, j, k: (b, 0, 0)), (w_co, (None, W, Dc), lambda b, i, j, k: (b, 0, 0))],
        [(0, 2, NN, 0), (1, 3, NN, 1)], [ga_blk, gb_blk], [dc_out, dc_out, dc_out], merge_epilogue, [(tm, Dc)] * 2)

    (h2,) = _mm("mix_out", (1, S // tm, D // tn, D // tk),
                [(merged, (tm, tk), lambda b, i, j, k: (i, k)), (w_o_f, (tk, tn), lambda b, i, j, k: (k, j))],
                [(0, 1, NN, 0)], [(h1, (tm, tn), lambda b, i, j, k: (i, j))],
                [((S, D), F32, (tm, tn), lambda b, i, j, k: (i, j))],
                lambda accs, sides: [sides[0] + accs[0]], [(tm, tn)])

    u3, r3 = _rms_fwd("rms3", h2, row(ffn2_norm))
    (wg2,) = fetch(5, [6, 7], [])(u3)
    h3, ffn2_saved, wu2, wd2 = _ffn_fwd("ffn2", u3, h2, wg2, lambda after: gathered(6, after)[0],
                                        lambda after: gathered(7, after)[0])
    loss_vec, dh3, dh3_half, d_final_norm = _loss_head("loss_head", h3, row(final_norm), target)
    loss = lax.psum(loss_vec[0, 0], ("x", "y", "c"))
    loss_done = jnp.zeros((SUBLANE, LANE), F32) + loss

    grads, deltas, new_m, new_v = {}, {}, {}, {}

    def rs_sibling_start(tag, parts):
        lands = [lax.empty((4,) + p.shape[1:], p.dtype) for p in parts]
        return _split_start("rs_sibling_start_" + tag, parts, lands, _sibling_copies, 4)

    def rs_chips_start(tag, sibling_begun, after):
        parts, lands = _split_wait("rs_sibling_wait_" + tag, sibling_begun, _sibling_copies, after)
        sums = [_sum_sibling("rs_sum_%s_%d" % (tag, a), p, land, c_arr) for a, (p, land) in enumerate(zip(parts, lands))]
        lands2 = [lax.empty((3,) + sm.shape[1:], sm.dtype) for sm in sums]
        return _split_start("rs_chips_start_" + tag, sums, lands2, _chips_copies, 3)

    def rs_end(tag, group, begun, after):
        sums, lands2 = _split_wait("rs_chips_wait_" + tag, begun, _chips_copies, after)
        for n, sm, land2 in zip(group, sums, lands2):
            res = _finish_sharded("adamw_" + n, sm, land2, q_arr, local(P, n), local(M, n), local(V, n))
            grads[n], deltas[n], new_m[n], new_v[n] = [t.T if n in transposed else t for t in res]

    f2_dwd, f2_dwgu, f2_du = _ffn_bwd("ffn2", dh3_half, u3, ffn2_saved, wg2, wu2, wd2, deps=[loss_done])
    dwd2 = f2_dwd()
    dwg2, dwu2 = f2_dwgu()
    sib_ffn2 = rs_sibling_start("ffn2", [dwg2, dwu2, dwd2])
    du3 = f2_du(deps=[sib_ffn2[4]])
    dh2, dh2_b, d_ffn2_norm = _rms_bwd("rms3_bwd", du3, h2, r3, row(ffn2_norm), dh3, 1.0)
    rs_ffn2 = rs_chips_start("ffn2", sib_ffn2, dh2)

    dg_out = ((2, S, W), BF, (None, tm, Dc), lambda b, i, j, k: (j // per, i, j % per))
    ga_blk2 = (proj, (None, tm, Dc), lambda b, i, j, k: (4 + j // per, i, j % per))
    gb_blk2 = (proj, (None, tm, Dc), lambda b, i, j, k: (6 + j // per, i, j % per))
    dcj = lambda arr: (arr, (tm, Dc), lambda b, i, j, k: (i, j))
    dcj_out = ((S, D), BF, (tm, Dc), lambda b, i, j, k: (i, j))

    def dmerge_epilogue(accs, sides):
        dm = accs[0]
        sa, sb = _sigmoid(sides[0]), _sigmoid(sides[1])
        za, zb = sides[2].astype(F32), sides[3].astype(F32)
        return dm * sa, dm * sb, dm * za * sa * (1.0 - sa), dm * zb * sb * (1.0 - sb)

    dz_a, dz_b, dga, dgb = _mm(
        "mix_out_dx", (1, S // tm, NDEV, D // tk),
        [(dh2_b, (tm, tk), lambda b, i, j, k: (i, k)), (w_o_f, (Dc, tk), lambda b, i, j, k: (j, k))],
        [(0, 1, NT, 0)], [ga_blk2, gb_blk2, dcj(z_a), dcj(z_b)], [dcj_out, dcj_out, dg_out, dg_out],
        dmerge_epilogue, [(tm, Dc)], deps=[rs_ffn2[4]])

    td = _pick(D, M_TILE, LANE)
    (dw_o,) = _mm("mix_out_dw", (1, D // td, D // tn, S // ts),
                  [(merged, (ts, td), lambda b, i, j, k: (k, i)), (dh2_b, (ts, tn), lambda b, i, j, k: (k, j))],
                  [(0, 1, TN, 0)], [], [((D, D), BF, (td, tn), lambda b, i, j, k: (i, j))], _plain, [(td, tn)])

    wout = ((NDEV, W, Dc), BF, (None, W, Dc), lambda b, i, j, k: (b, 0, 0))
    dw_so, dw_co = _mm(
        "mix_merge_dw", (NDEV, 1, 1, S // ts),
        [(y2, (ts, W), lambda b, i, j, k: (k, 0)), (yb, (ts, W), lambda b, i, j, k: (k, 0)),
         (dz_a, (ts, Dc), lambda b, i, j, k: (k, b)), (dz_b, (ts, Dc), lambda b, i, j, k: (k, b))],
        [(0, 2, TN, 0), (1, 3, TN, 1)], [], [wout, wout], _plain, [(W, Dc)] * 2)

    def dglu_epilogue(accs, sides):
        dy2, dyb = accs
        sq = _sigmoid(sides[1])
        return dy2 * _gelu(sides[0]) * sq * (1.0 - sq), dy2 * sq, dyb

    full_w = lambda arr: (arr, (th, W), lambda b, i, j, k: (i, 0))
    full_w_out = lambda dt: ((S, W), dt, (th, W), lambda b, i, j, k: (i, 0))
    dq, dy1p, dyb = _mm(
        "mix_merge_dx", (1, S // th, 1, NDEV),
        [(dz_a, (th, Dc), lambda b, i, j, k: (i, k)), (dz_b, (th, Dc), lambda b, i, j, k: (i, k)),
         (w_so, (None, W, Dc), lambda b, i, j, k: (k, 0, 0)), (w_co, (None, W, Dc), lambda b, i, j, k: (k, 0, 0))],
        [(0, 2, NT, 0), (1, 3, NT, 1)], [full_w(y0), full_w(q_pre)], [full_w_out(BF), full_w_out(F32), full_w_out(F32)],
        dglu_epilogue, [(th, W)] * 2)

    def dy0_epilogue(accs, sides):
        dy0 = (sides[0] + accs[0]) * _gelu_grad(sides[1])
        return dy0, dy0

    wj = lambda arr: (arr, (tm, tw), lambda b, i, j, k: (i, j))
    dy0, dy0_b = _mm("ssm_glu_dx", (1, S // tm, W // tw, 1),
                     [(dq, (tm, W), lambda b, i, j, k: (i, 0)), (w_glu_f, (tw, W), lambda b, i, j, k: (j, 0))],
                     [(0, 1, NT, 0)], [wj(dy1p), wj(y0)],
                     [((S, W), F32, (tm, tw), lambda b, i, j, k: (i, j)), ((S, W), BF, (tm, tw), lambda b, i, j, k: (i, j))],
                     dy0_epilogue, [(tm, tw)])

    (dw_glu,) = _mm("ssm_glu_dw", (1, W // tw, 1, S // ts),
                    [(y1, (ts, tw), lambda b, i, j, k: (k, i)), (dq, (ts, W), lambda b, i, j, k: (k, 0))],
                    [(0, 1, TN, 0)], [], [((W, W), BF, (tw, W), lambda b, i, j, k: (i, 0))], _plain, [(tw, W)])

    tr = _pick(S, 256, SUBLANE)
    rw = ((tr, W), lambda i: (i, 0))
    vec_w = ((1, W), F32, (1, W), lambda i: (0, 0))
    d_b_glu, d_ssm_d = _ew(
        "ssm_colsums", (S // tr,), [(dq,) + rw, (dy0,) + rw, (proj, (None, tr, W), lambda i: (0, i, 0))],
        [vec_w, vec_w],
        lambda dqv, dyv, vv: (jnp.sum(dqv.astype(F32), axis=0, keepdims=True), jnp.sum(dyv * vv, axis=0, keepdims=True)),
        acc=(0, 1))

    ds_re3, ds_im3 = _ssm_spread("ssm_ds", dy0_b, cd_re_b, cd_imn_b, NT, rows)
    lam_re3, lam_im3, da_re, da_im = _scan_bwd(ds_re3, ds_im3, s_re3, s_im3, a_re, a_im)
    dv, lam_re_b, lam_im_b = _ssm_collect("ssm_dv", lam_re3, lam_im3, bd_re_b, bd_im_b, NT, dy0, row(ssm_d),
                                          lambda acc, dy_tile, d_tile: [acc + dy_tile * d_tile], [BF])
    tiles, tch, tst = bd_re.shape
    tok_ch = lambda arr: (arr, (ts, tch), lambda b, i, j, k: (k, b))
    tok_st = lambda arr: (arr, (ts, tst), lambda b, i, j, k: (k, b))
    bd_out = ((tiles, tch, tst), F32, (None, tch, tst), lambda b, i, j, k: (b, 0, 0))
    cd_out = ((tiles, tst, tch), F32, (None, tst, tch), lambda b, i, j, k: (b, 0, 0))
    dbd_re, dbd_im = _mm("ssm_dbd", (tiles, 1, 1, S // ts), [tok_ch(v_bf), tok_st(lam_re_b), tok_st(lam_im_b)],
                         [(0, 1, TN, 0), (0, 2, TN, 1)], [], [bd_out, bd_out], _plain, [(tch, tst)] * 2)
    dcd_re, dcd_imn = _mm("ssm_dcd", (tiles, 1, 1, S // ts), [tok_st(s_re_b), tok_st(s_im_b), tok_ch(dy0_b)],
                          [(0, 2, TN, 0), (1, 2, TN, 1)], [], [cd_out, cd_out], _plain, [(tst, tch)] * 2)
    d_ssm = ssm_vjp((da_re, da_im, dbd_re, dbd_im, dcd_re, dcd_imn))

    dbg, dcg, dval, d_conv_w_full, d_conv_b = _conv_bwd(dyb, proj, conv, cw)
    dproj = jnp.concatenate([dv[None], dbg[None], dcg[None], dval[None], dga, dgb], axis=0)

    (dw_in,) = _mm("in_proj_dw", (NDEV, D // td, 1, S // ts),
                   [(u2, (ts, td), lambda b, i, j, k: (k, i)), (dproj, (None, ts, W), lambda b, i, j, k: (b, k, 0))],
                   [(0, 1, TN, 0)], [], [((NDEV, D, W), BF, (None, td, W), lambda b, i, j, k: (b, i, 0))],
                   _plain, [(td, W)])
    sib_mixer = rs_sibling_start(
        "mixer", [dw_in, dw_glu.reshape(NDEV, W // NDEV, W), dw_so, dw_co, dw_o.reshape(NDEV, Dc, D)])
    (du2,) = _mm("in_proj_dx", (1, S // tm, D // tn, NDEV // K_SHARDS),
                 [(dproj, (K_SHARDS, tm, W), lambda b, i, j, k: (k, i, 0)),
                  (w_in_f, (K_SHARDS, tn, W), lambda b, i, j, k: (k, j, 0))],
                 [(0, 1, NT, 0)], [], [((S, D), F32, (tm, tn), lambda b, i, j, k: (i, j))], _plain, [(tm, tn)],
                 deps=[sib_mixer[4]])
    dh1, dh1_half, d_mix_norm = _rms_bwd("rms2_bwd", du2, h1, r2, row(mix_norm), dh2, 0.5)
    rs_mixer = rs_chips_start("mixer", sib_mixer, dh1)

    small = dict(mix_norm=d_mix_norm, ffn2_norm=d_ffn2_norm, final_norm=d_final_norm,
                 ssm_lambda_re=d_ssm[0], ssm_lambda_im=d_ssm[1], ssm_log_dt=d_ssm[2], ssm_b_re=d_ssm[3],
                 ssm_b_im=d_ssm[4], ssm_c_re=d_ssm[5], ssm_c_im=d_ssm[6], ssm_d=d_ssm_d, ssm_b_glu=d_b_glu,
                 conv_b=d_conv_b)
    replicated = [n for n in replicated if n != "ffn1_norm"] + ["ffn1_norm"]
    tile = SUBLANE * LANE

    def as_rows(p):
        flat = p.reshape(-1).astype(F32)
        return jnp.pad(flat, (0, -flat.shape[0] % tile)).reshape(-1, LANE)

    def pack(parts):
        return jnp.concatenate([as_rows(p) for p in parts], axis=0)

    cw_zero = jnp.zeros_like(d_conv_w_full)
    early_pk = pack([small[n] for n in replicated[:-1]] + [d_conv_w_full])
    early_land = lax.dynamic_update_slice(jnp.zeros((NDEV,) + early_pk.shape, F32), early_pk[None], (me, 0, 0))
    small_begun = _split_start("gather_small_start", [early_pk], [early_land], _everyone_copies, NDEV - 1)

    f1_dwd, f1_dwgu, f1_du = _ffn_bwd("ffn1", dh1_half, u1, ffn1_saved, wg1, wu1, wd1,
                                      deps=[rs_mixer[4], small_begun[4]])
    du1 = f1_du()
    dx, _, d_ffn1_norm = _rms_bwd("rms1_bwd", du1, xh, r1, row(ffn1_norm), dh1, 1.0)
    dwd1 = f1_dwd(deps=[d_ffn1_norm])
    late = d_ffn1_norm + 0.0 * dwd1[0, :1, :1].astype(F32)
    (late_all,) = _all_gather("gather_ffn1_norm_grad", [late.reshape(-1, LANE)])
    rs_ffn1_down = rs_chips_start("ffn1_down", rs_sibling_start("ffn1_down", [dwd1]), late_all)
    dwg1, dwu1 = f1_dwgu(deps=[rs_ffn1_down[4]])
    rs_ffn1_gate_up = rs_chips_start("ffn1_gate_up", rs_sibling_start("ffn1_gate_up", [dwg1, dwu1]), None)
    rs_end("ffn2", sharded[8:11], rs_ffn2, rs_ffn1_gate_up[4])
    rs_end("mixer", sharded[3:8], rs_mixer, [grads[n] for n in sharded[8:11]])
    _, (early_all,) = _split_wait("gather_small_wait", small_begun, _everyone_copies, [grads[n] for n in sharded[3:8]])
    small_all = jnp.concatenate([early_all, late_all], axis=1)
    g_pk, d_pk, m_pk, v_pk = _finish_replicated(
        "adamw_replicated", small_all, pack([P[n] for n in replicated[:-1]] + [cw_zero, P["ffn1_norm"]]),
        pack([M[n] for n in replicated[:-1]] + [cw_zero, M["ffn1_norm"]]),
        pack([V[n] for n in replicated[:-1]] + [cw_zero + 1.0, V["ffn1_norm"]]))

    def unpack(pk, r0, like):
        nr = -(-like.size // tile) * SUBLANE
        return pk[r0:r0 + nr].reshape(-1)[:like.size].reshape(like.shape), r0 + nr

    r0 = 0
    for n in replicated[:-1] + ["conv_w", "ffn1_norm"]:
        like = d_conv_w_full if n == "conv_w" else P[n]
        for store, pk in ((grads, g_pk), (deltas, d_pk), (new_m, m_pk), (new_v, v_pk)):
            store[n], r1 = unpack(pk, r0, like)
        r0 = r1
    g_cw_full = grads["conv_w"]
    cwl = conv_w.shape[1]
    g_cw = lax.dynamic_slice_in_dim(g_cw_full, me * cwl, cwl, axis=1)
    full3 = ((3, cwl), lambda i: (0, 0))
    grads["conv_w"], deltas["conv_w"], new_m["conv_w"], new_v["conv_w"] = _ew(
        "adamw_conv_w", (1,), [(g_cw,) + full3, (conv_w,) + full3, (m_conv_w,) + full3, (v_conv_w,) + full3],
        [((3, cwl), F32) + full3] * 4, lambda g, w, m, v: (g,) + _adamw(w, g, m, v))
    rs_end("ffn1_down", sharded[2:3], rs_ffn1_down, [g_pk, grads["conv_w"]])
    rs_end("ffn1_gate_up", sharded[0:2], rs_ffn1_gate_up, grads["ffn1_w_down"])

    return (loss, dx.reshape(x.shape), *[grads[n] for n in names], *[deltas[n] for n in names],
            *[new_m[n] for n in names], *[new_v[n] for n in names])
```

```python
import math

import jax
import jax.numpy as jnp
from jax import lax
from jax.experimental import pallas as pl
from jax.experimental.pallas import tpu as pltpu

F32 = jnp.float32
BF = jnp.bfloat16
I32 = jnp.int32
MESH = pl.DeviceIdType.MESH
LANE = 128
SUBLANE = 8
NDEV = 8
EW_BLOCK = 256 * 1024
M_TILE = 1024
K_TILE = 2048
K_SHARDS = 2
DMA_CHUNKS = 4
EPS = 1e-6
ADAM_LR, ADAM_B1, ADAM_B2, ADAM_EPS, ADAM_WD, ADAM_STEP = 0.001, 0.9, 0.999, 1e-08, 0.01, 10
NN = ((1,), (0,))
NT = ((1,), (1,))
TN = ((0,), (0,))
HBM = pl.BlockSpec(memory_space=pltpu.HBM)


def _pick(n, pref, mult):
    t = min(pref, n)
    t -= t % mult
    while t >= mult:
        if n % t == 0:
            return t
        t -= mult
    return n


def _sigmoid(x):
    return 1.0 / (1.0 + jnp.exp(-x))


_GELU_C = math.sqrt(2.0 / math.pi)


def _gelu(x):
    return 0.5 * x * (1.0 + jnp.tanh(_GELU_C * (x + 0.044715 * x * x * x)))


def _gelu_grad(x):
    t = jnp.tanh(_GELU_C * (x + 0.044715 * x * x * x))
    return 0.5 * (1.0 + t) + 0.5 * x * (1.0 - t * t) * _GELU_C * (1.0 + 3.0 * 0.044715 * x * x)


def _dep_specs(deps, rank):
    return [(d, d.shape, lambda *_, nd=d.ndim: (0,) * nd) for d in deps]


def _mm(name, grid, ops, pairs, sides, outs, epilogue, acc_shapes, deps=()):
    nk = grid[-1]
    n_ops, n_sides, n_outs = len(ops), len(sides), len(outs)
    dep_specs = _dep_specs(deps, len(grid))
    n_deps = len(dep_specs)

    def body(*refs):
        op_refs = refs[:n_ops]
        side_refs = refs[n_ops:n_ops + n_sides]
        out_refs = refs[n_ops + n_sides + n_deps:n_ops + n_sides + n_deps + n_outs]
        acc_refs = refs[n_ops + n_sides + n_deps + n_outs:]

        def partials():
            res = [None] * len(acc_shapes)
            for ia, ib, dims, ai in pairs:
                a_ref, b_ref = op_refs[ia], op_refs[ib]
                for s in range(a_ref.shape[0] if len(a_ref.shape) == 3 else 1):
                    a, b = (a_ref[s], b_ref[s]) if len(a_ref.shape) == 3 else (a_ref[...], b_ref[...])
                    p = lax.dot_general(a, b, (dims, ((), ())), preferred_element_type=F32)
                    res[ai] = p if res[ai] is None else res[ai] + p
            return res

        def finish(accs):
            vals = epilogue(accs, [s[...] for s in side_refs])
            for o, v in zip(out_refs, vals):
                o[...] = v.astype(o.dtype)

        if nk == 1:
            finish(partials())
        else:
            k = pl.program_id(len(grid) - 1)

            @pl.when(k == 0)
            def _():
                for a, p in zip(acc_refs, partials()):
                    a[...] = p

            @pl.when(k > 0)
            def _():
                for a, p in zip(acc_refs, partials()):
                    a[...] += p

            @pl.when(k == nk - 1)
            def _():
                finish([a[...] for a in acc_refs])

    return pl.pallas_call(
        body, name=name, grid=grid,
        in_specs=[pl.BlockSpec(b, m) for (_, b, m) in list(ops) + list(sides) + dep_specs],
        out_specs=[pl.BlockSpec(b, m) for (_, _, b, m) in outs],
        out_shape=[jax.ShapeDtypeStruct(s, d) for (s, d, _, _) in outs],
        scratch_shapes=[pltpu.VMEM(s, F32) for s in acc_shapes] if nk > 1 else [],
        compiler_params=pltpu.CompilerParams(
            dimension_semantics=("parallel",) * (len(grid) - 1) + ("arbitrary",)),
    )(*[a for (a, _, _) in list(ops) + list(sides) + dep_specs])


def _ew(name, grid, ins, outs, fn, acc=(), deps=()):
    n_in = len(ins)
    dep_specs = _dep_specs(deps, len(grid))

    def body(*refs):
        vals = fn(*[r[...] for r in refs[:n_in]])
        first = pl.program_id(0) == 0
        for idx, (o, v) in enumerate(zip(refs[n_in + len(dep_specs):], vals)):
            if idx in acc:
                @pl.when(first)
                def _(o=o, v=v):
                    o[...] = v.astype(o.dtype)

                @pl.when(jnp.logical_not(first))
                def _(o=o, v=v):
                    o[...] += v.astype(o.dtype)
            else:
                o[...] = v.astype(o.dtype)

    return pl.pallas_call(
        body, name=name, grid=grid,
        in_specs=[pl.BlockSpec(b, m) for (_, b, m) in list(ins) + dep_specs],
        out_specs=[pl.BlockSpec(b, m) for (_, _, b, m) in outs],
        out_shape=[jax.ShapeDtypeStruct(s, d) for (s, d, _, _) in outs],
        compiler_params=pltpu.CompilerParams(
            dimension_semantics=(("arbitrary",) if acc else ("parallel",)) * len(grid)),
    )(*[a for (a, _, _) in list(ins) + dep_specs])


def _position():
    x, y, c = lax.axis_index("x"), lax.axis_index("y"), lax.axis_index("c")
    chips = [(1 - x, y), (x, 1 - y), (1 - x, 1 - y)]
    return x, y, c, chips


def _all_gather(name, shards):
    n = len(shards)

    def body(*refs):
        xs, outs = refs[:n], refs[n:2 * n]
        send_sems, recv_sems, local_sems = refs[2 * n:]
        x, y, c, chips = _position()
        me, sibling = (x, y, c), (x, y, 1 - c)

        def copy(a, k, block, to, src=None):
            dst = outs[a].at[4 * block[0] + 2 * block[1] + block[2]]
            return pltpu.make_async_remote_copy(
                src_ref=dst if src is None else src, dst_ref=dst,
                send_sem=send_sems.at[a, k], recv_sem=recv_sems.at[a, k],
                device_id=to, device_id_type=MESH)

        mine = [pltpu.make_async_copy(xs[a], outs[a].at[4 * x + 2 * y + c], local_sems.at[a]) for a in range(n)]
        for cp in mine:
            cp.start()
        first = []
        for a in range(n):
            first.append(copy(a, 0, me, sibling, src=xs[a]))
            first += [copy(a, 1 + j, me, (*chip, c), src=xs[a]) for j, chip in enumerate(chips)]
        for cp in first:
            cp.start()
        passed = []
        for a in range(n):
            for j, chip in enumerate(chips):
                copy(a, 1 + j, (*chip, c), me).wait_recv()
                cp = copy(a, 4 + j, (*chip, c), sibling)
                cp.start()
                passed.append(cp)
        for a in range(n):
            copy(a, 0, sibling, me).wait_recv()
            for j, chip in enumerate(chips):
                copy(a, 4 + j, (*chip, 1 - c), me).wait_recv()
        for cp in first + passed:
            cp.wait_send()
        for cp in mine:
            cp.wait()

    return pl.pallas_call(
        body, name=name,
        out_shape=[jax.ShapeDtypeStruct((NDEV,) + s.shape, s.dtype) for s in shards],
        in_specs=[HBM] * n, out_specs=[HBM] * n,
        scratch_shapes=[pltpu.SemaphoreType.DMA((n, 7)), pltpu.SemaphoreType.DMA((n, 7)),
                        pltpu.SemaphoreType.DMA((n,))],
    )(*shards)


SEM = pl.BlockSpec(memory_space=pltpu.SEMAPHORE)
EFFECT = pltpu.SideEffectType.DATAFLOW_SIDE_EFFECTING


def _in_hbm(v):
    return pltpu.with_memory_space_constraint(v, pltpu.HBM)


def _split_start(name, srcs, lands, make_copies, n_per, after=()):
    n, nb = len(srcs), len(srcs) + len(lands)
    n_sems = len(lands) * n_per
    after = list(after)

    def body(*refs):
        send_sems, recv_sems = refs[nb + len(after)], refs[nb + len(after) + 1]
        for cp in make_copies(refs[:n], refs[n:nb], send_sems, recv_sems):
            cp.start()
        refs[-1][...] = jnp.zeros_like(refs[-1])

    outs = pl.pallas_call(
        body, name=name,
        out_shape=(pltpu.SemaphoreType.DMA((n_sems,)), pltpu.SemaphoreType.DMA((n_sems,)),
                   *[pltpu.HBM(v.shape, v.dtype) for v in list(srcs) + list(lands)],
                   jax.ShapeDtypeStruct((SUBLANE, LANE), F32)),
        in_specs=[HBM] * nb + [pl.BlockSpec(memory_space=pl.ANY)] * len(after),
        out_specs=(SEM, SEM, *[HBM] * nb, pl.BlockSpec(memory_space=pltpu.VMEM)),
        input_output_aliases={i: 2 + i for i in range(nb)},
        compiler_params=pltpu.CompilerParams(has_side_effects=EFFECT),
    )(*[_in_hbm(v) for v in list(srcs) + list(lands)], *after)
    return outs[0], outs[1], list(outs[2:2 + n]), list(outs[2 + n:2 + nb]), outs[-1]


def _split_wait(name, started, make_copies, after):
    send_sems, recv_sems, srcs, lands, _ = started
    n, nb = len(srcs), len(srcs) + len(lands)

    def body(*refs):
        for cp in make_copies(refs[:n], refs[n:nb], refs[nb], refs[nb + 1]):
            cp.wait_send()
            cp.wait_recv()

    order = [] if after is None else list(after) if isinstance(after, (list, tuple)) else [after]
    outs = pl.pallas_call(
        body, name=name,
        out_shape=tuple(pltpu.HBM(v.shape, v.dtype) for v in srcs + lands),
        in_specs=[HBM] * nb + [SEM, SEM] + [pl.BlockSpec(memory_space=pl.ANY)] * len(order),
        out_specs=tuple([HBM] * nb),
        input_output_aliases={i: i for i in range(nb)},
        compiler_params=pltpu.CompilerParams(has_side_effects=EFFECT),
    )(*srcs, *lands, send_sems, recv_sems, *order)
    return list(outs[:n]), list(outs[n:])


def _gather_first_copies(xs, lands, send_sems, recv_sems):
    x, y, c, _ = _position()
    copies = []
    for a in range(len(xs)):
        for k, peer in enumerate([(x, y, 1 - c), (1 - x, y, c), (x, 1 - y, c)]):
            copies.append(pltpu.make_async_remote_copy(
                src_ref=xs[a], dst_ref=lands[a].at[4 * x + 2 * y + c],
                send_sem=send_sems.at[3 * a + k], recv_sem=recv_sems.at[3 * a + k], device_id=peer, device_id_type=MESH))
    return copies


def _gather_second_copies(xs, lands, send_sems, recv_sems):
    x, y, c, _ = _position()
    copies = []
    for a in range(len(lands)):
        rows = lands[a].shape[1]
        unit = SUBLANE * (4 // jnp.dtype(lands[a].dtype).itemsize)
        half = rows // 2 // unit * unit or rows
        parts = [((1 - x, y), (x, 1 - y), pl.ds(0, half))]
        if half < rows:
            parts.append(((x, 1 - y), (1 - x, y), pl.ds(half, rows - half)))
        for k, (block, to, rs) in enumerate(parts):
            ref = lands[a].at[4 * block[0] + 2 * block[1] + c, rs]
            copies.append(pltpu.make_async_remote_copy(
                src_ref=ref, dst_ref=ref, send_sem=send_sems.at[2 * a + k], recv_sem=recv_sems.at[2 * a + k],
                device_id=(*to, c), device_id_type=MESH))
    return copies


def _chips_copies(ps, lands, send_sems, recv_sems):
    x, y, c, chips = _position()
    copies = []
    for a in range(len(ps)):
        for j, chip in enumerate(chips):
            copies.append(pltpu.make_async_remote_copy(
                src_ref=ps[a].at[2 * chip[0] + chip[1]], dst_ref=lands[a].at[j],
                send_sem=send_sems.at[3 * a + j], recv_sem=recv_sems.at[3 * a + j], device_id=(*chip, c),
                device_id_type=MESH))
    return copies


def _sibling_copies(gs, lands, send_sems, recv_sems):
    x, y, c, _ = _position()
    copies = []
    for a in range(len(gs)):
        for q in range(4):
            copies.append(pltpu.make_async_remote_copy(
                src_ref=gs[a].at[2 * q + 1 - c], dst_ref=lands[a].at[q],
                send_sem=send_sems.at[4 * a + q], recv_sem=recv_sems.at[4 * a + q],
                device_id=(x, y, 1 - c), device_id_type=MESH))
    return copies


def _everyone_copies(xs, lands, send_sems, recv_sems):
    x, y, c, _ = _position()
    flip = lambda v, bit: 1 - v if bit else v
    copies = []
    for a in range(len(xs)):
        for k in range(1, NDEV):
            copies.append(pltpu.make_async_remote_copy(
                src_ref=xs[a], dst_ref=lands[a].at[4 * x + 2 * y + c],
                send_sem=send_sems.at[7 * a + k - 1], recv_sem=recv_sems.at[7 * a + k - 1],
                device_id=(flip(x, k & 4), flip(y, k & 2), flip(c, k & 1)), device_id_type=MESH))
    return copies


def _row_chunks(rows, dtype):
    unit = SUBLANE * (4 // jnp.dtype(dtype).itemsize)
    units = rows // unit
    if rows % unit or units < 2:
        return [(0, rows)]
    k = min(DMA_CHUNKS, units)
    sizes = [(units // k + (1 if i < units % k else 0)) * unit for i in range(k)]
    return [(sum(sizes[:i]), sz) for i, sz in enumerate(sizes)]


def _gather_forward(name, lands, after=()):
    n = len(lands)
    after = list(after)

    def body(*refs):
        ins, outs = refs[:n], refs[n + len(after):2 * n + len(after)]
        send_sems, recv_sems = refs[2 * n + len(after):]
        x, y, c, chips = _position()
        whole, chunks = [], []
        for a in range(n):
            rows = _row_chunks(ins[a].shape[1], ins[a].dtype)
            for j, chip in enumerate(chips):
                slot = 4 * chip[0] + 2 * chip[1]

                def to_sibling(src, dst):
                    return pltpu.make_async_remote_copy(
                        src_ref=src, dst_ref=dst, send_sem=send_sems.at[a, j], recv_sem=recv_sems.at[a, j],
                        device_id=(x, y, 1 - c), device_id_type=MESH)

                whole.append(to_sibling(ins[a].at[slot + c], outs[a].at[slot + 1 - c]))
                chunks += [to_sibling(ins[a].at[slot + c, pl.ds(r0, nr)], outs[a].at[slot + c, pl.ds(r0, nr)])
                           for r0, nr in rows]
        for cp in chunks:
            cp.start()
        for cp in whole:
            cp.wait()

    return pl.pallas_call(
        body, name=name,
        out_shape=[jax.ShapeDtypeStruct(l.shape, l.dtype) for l in lands],
        in_specs=[HBM] * n + [pl.BlockSpec(memory_space=pl.ANY)] * len(after), out_specs=[HBM] * n,
        input_output_aliases={a: a for a in range(n)},
        scratch_shapes=[pltpu.SemaphoreType.DMA((n, 3)), pltpu.SemaphoreType.DMA((n, 3))],
    )(*lands, *after)


def _sum_sibling(name, g, land, c_arr):
    _, R, C = g.shape
    tr = _pick(R, 512, SUBLANE)

    def body(c_ref, g_ref, l_ref, o_ref):
        o_ref[...] = (g_ref[...].astype(F32) + l_ref[...].astype(F32)).astype(o_ref.dtype)

    return pl.pallas_call(
        body, name=name,
        grid_spec=pltpu.PrefetchScalarGridSpec(
            num_scalar_prefetch=1, grid=(4, R // tr),
            in_specs=[pl.BlockSpec((None, tr, C), lambda q, i, cr: (2 * q + cr[0], i, 0)),
                      pl.BlockSpec((None, tr, C), lambda q, i, cr: (q, i, 0))],
            out_specs=pl.BlockSpec((None, tr, C), lambda q, i, cr: (q, i, 0))),
        out_shape=jax.ShapeDtypeStruct((4, R, C), g.dtype),
        compiler_params=pltpu.CompilerParams(dimension_semantics=("parallel", "parallel")),
    )(c_arr, g, land)


def _adamw(w, g, m, v):
    m = ADAM_B1 * m + (1.0 - ADAM_B1) * g
    v = ADAM_B2 * v + (1.0 - ADAM_B2) * (g * g)
    m_hat = m / (1.0 - ADAM_B1 ** ADAM_STEP)
    v_hat = v / (1.0 - ADAM_B2 ** ADAM_STEP)
    delta = -ADAM_LR * (m_hat / (jnp.sqrt(v_hat) + ADAM_EPS) + ADAM_WD * w)
    return delta, m, v


def _finish_sharded(name, sums, land, q_arr, w, m, v):
    R, C = w.shape
    tr = _pick(R, 512, SUBLANE)
    tc = _pick(C, max(LANE, EW_BLOCK // tr), LANE)

    def body(q_ref, p_ref, l_ref, w_ref, m_ref, v_ref, g_out, d_out, m_out, v_out):
        g = p_ref[...].astype(F32)
        for j in range(3):
            g = g + l_ref[j].astype(F32)
        d, mn, vn = _adamw(w_ref[...], g, m_ref[...], v_ref[...])
        g_out[...] = g
        d_out[...] = d
        m_out[...] = mn
        v_out[...] = vn

    blk = pl.BlockSpec((tr, tc), lambda i, j, qr: (i, j))
    return pl.pallas_call(
        body, name=name,
        grid_spec=pltpu.PrefetchScalarGridSpec(
            num_scalar_prefetch=1, grid=(R // tr, C // tc),
            in_specs=[pl.BlockSpec((None, tr, tc), lambda i, j, qr: (qr[0], i, j)),
                      pl.BlockSpec((3, tr, tc), lambda i, j, qr: (0, i, j)), blk, blk, blk],
            out_specs=[blk] * 4),
        out_shape=[jax.ShapeDtypeStruct((R, C), F32)] * 4,
        compiler_params=pltpu.CompilerParams(dimension_semantics=("parallel", "parallel")),
    )(q_arr, sums, land, w, m, v)


def _finish_replicated(name, gathered, w, m, v):
    _, R, C = gathered.shape
    tr = _pick(R, 256, SUBLANE)

    def fn(gv, wv, mv, vv):
        g = gv[0]
        for d in range(1, NDEV):
            g = g + gv[d]
        dl, mn, vn = _adamw(wv, g, mv, vv)
        return g, dl, mn, vn

    row = ((tr, C), lambda i: (i, 0))
    return _ew(name, (R // tr,),
               [(gathered, (NDEV, tr, C), lambda i: (0, i, 0)), (w,) + row, (m,) + row, (v,) + row],
               [((R, C), F32) + row] * 4, fn)


def _rms_fwd(name, h, g, deps=()):
    S, D = h.shape
    tr = _pick(S, 256, SUBLANE)

    def fn(hv, gv):
        r = lax.rsqrt(jnp.mean(hv * hv, axis=-1, keepdims=True) + EPS)
        return hv * r * gv, r

    return _ew(name, (S // tr,),
               [(h, (tr, D), lambda i: (i, 0)), (g, (1, D), lambda i: (0, 0))],
               [((S, D), BF, (tr, D), lambda i: (i, 0)), ((S, 1), F32, (tr, 1), lambda i: (i, 0))], fn, deps=deps)


def _rms_bwd(name, du, h, r, g, dres, scale):
    S, D = h.shape
    tr = _pick(S, 256, SUBLANE)

    def fn(duv, hv, rv, gv, drv):
        xn = hv * rv
        dxn = duv * gv
        dh = drv + rv * (dxn - xn * jnp.mean(dxn * xn, axis=-1, keepdims=True))
        return dh, scale * dh, jnp.sum(duv * xn, axis=0, keepdims=True)

    row = ((tr, D), lambda i: (i, 0))
    return _ew(name, (S // tr,),
               [(du,) + row, (h,) + row, (r, (tr, 1), lambda i: (i, 0)), (g, (1, D), lambda i: (0, 0)), (dres,) + row],
               [((S, D), F32) + row, ((S, D), BF) + row, ((1, D), F32, (1, D), lambda i: (0, 0))], fn, acc=(2,))


def _loss_head(name, h, g, target):
    S, D = h.shape
    tr = _pick(S, 256, SUBLANE)

    def fn(hv, gv, tv):
        r = lax.rsqrt(jnp.mean(hv * hv, axis=-1, keepdims=True) + EPS)
        xn = hv * r
        diff = xn * gv - tv
        loss = 0.5 * jnp.sum(jnp.mean(diff * diff, axis=-1, keepdims=True))
        dout = diff / D
        dxn = dout * gv
        dh = r * (dxn - xn * jnp.mean(dxn * xn, axis=-1, keepdims=True))
        return (jnp.zeros((1, LANE), F32) + loss, dh, 0.5 * dh, jnp.sum(dout * xn, axis=0, keepdims=True))

    row = ((tr, D), lambda i: (i, 0))
    return _ew(name, (S // tr,),
               [(h,) + row, (g, (1, D), lambda i: (0, 0)), (target,) + row],
               [((1, LANE), F32, (1, LANE), lambda i: (0, 0)), ((S, D), F32) + row, ((S, D), BF) + row,
                ((1, D), F32, (1, D), lambda i: (0, 0))], fn, acc=(0, 3))


def _ffn_fwd(tag, u, h, wg, fetch_wu, fetch_wd):
    S, D = u.shape
    Fs = wg.shape[1]
    tm, tk = _pick(S, M_TILE, SUBLANE), _pick(D, K_TILE, LANE)
    act = ((NDEV, S, Fs), BF, (None, tm, Fs), lambda b, i, j, k: (b, i, 0))
    lhs = (u, (tm, tk), lambda b, i, j, k: (i, k))
    rhs = lambda w: (w, (None, Fs, tk), lambda b, i, j, k: (b, 0, k))
    (gt,) = _mm(tag + "_gate", (NDEV, S // tm, 1, D // tk), [lhs, rhs(wg)], [(0, 1, NT, 0)], [], [act],
                lambda accs, sides: accs, [(tm, Fs)])
    wu = fetch_wu(gt)

    def up_epilogue(accs, sides):
        g = sides[0].astype(F32)
        return accs[0], g * _sigmoid(g) * accs[0]

    up, a = _mm(tag + "_up", (NDEV, S // tm, 1, D // tk), [lhs, rhs(wu)], [(0, 1, NT, 0)],
                [(gt, (None, tm, Fs), lambda b, i, j, k: (b, i, 0))], [act, act], up_epilogue, [(tm, Fs)])
    wd = fetch_wd(a)
    tn = _pick(D, 1024, LANE)
    (hn,) = _mm(
        tag + "_down", (1, S // tm, D // tn, NDEV // K_SHARDS),
        [(a, (K_SHARDS, tm, Fs), lambda b, i, j, k: (k, i, 0)), (wd, (K_SHARDS, Fs, tn), lambda b, i, j, k: (k, 0, j))],
        [(0, 1, NN, 0)], [(h, (tm, tn), lambda b, i, j, k: (i, j))],
        [((S, D), F32, (tm, tn), lambda b, i, j, k: (i, j))],
        lambda accs, sides: [sides[0] + 0.5 * accs[0]], [(tm, tn)])
    return hn, (gt, up, a), wu, wd


def _ffn_bwd(tag, dhs, u, saved, wg, wu, wd, deps=()):
    gt, up, a = saved
    S, D = u.shape
    Fs = wg.shape[1]
    tm, tk = _pick(S, M_TILE, SUBLANE), _pick(D, K_TILE, LANE)
    act_in = lambda arr: (arr, (None, tm, Fs), lambda b, i, j, k: (b, i, 0))
    act_out = ((NDEV, S, Fs), BF, (None, tm, Fs), lambda b, i, j, k: (b, i, 0))

    def act_epilogue(accs, sides):
        da = accs[0]
        gtv, upv = sides[0].astype(F32), sides[1].astype(F32)
        sg = _sigmoid(gtv)
        return da * upv * sg * (1.0 + gtv * (1.0 - sg)), da * gtv * sg

    dgt, dup = _mm(
        tag + "_dact", (NDEV, S // tm, 1, D // tk),
        [(dhs, (tm, tk), lambda b, i, j, k: (i, k)), (wd, (None, Fs, tk), lambda b, i, j, k: (b, 0, k))],
        [(0, 1, NT, 0)], [act_in(gt), act_in(up)], [act_out, act_out], act_epilogue, [(tm, Fs)], deps=deps)

    ts = _pick(S, K_TILE, SUBLANE)
    tn = _pick(D, 1024, LANE)
    wgrad = ((NDEV, Fs, D), BF, (None, Fs, tn), lambda b, i, j, k: (b, 0, j))
    tok = lambda arr: (arr, (None, ts, Fs), lambda b, i, j, k: (b, k, 0))

    def grad_down(deps=()):
        return _mm(
            tag + "_dwd", (NDEV, 1, D // tn, S // ts),
            [tok(a), (dhs, (ts, tn), lambda b, i, j, k: (k, j))],
            [(0, 1, TN, 0)], [], [wgrad], lambda accs, sides: accs, [(Fs, tn)], deps=deps)[0]

    def grad_gate_up(deps=(), only=None):
        acts = [dgt, dup] if only is None else [(dgt, dup)[only]]
        n = len(acts)
        return _mm(
            tag + ("_dwgu" if only is None else ("_dwg", "_dwu")[only]), (NDEV, 1, D // tn, S // ts),
            [tok(t) for t in acts] + [(u, (ts, tn), lambda b, i, j, k: (k, j))],
            [(i, n, TN, i) for i in range(n)], [], [wgrad] * n, lambda accs, sides: accs, [(Fs, tn)] * n, deps=deps)

    def du(deps=()):
        return _mm(
            tag + "_du", (1, S // tm, D // tn, NDEV // K_SHARDS),
            [(dgt, (K_SHARDS, tm, Fs), lambda b, i, j, k: (k, i, 0)), (dup, (K_SHARDS, tm, Fs), lambda b, i, j, k: (k, i, 0)),
             (wg, (K_SHARDS, Fs, tn), lambda b, i, j, k: (k, 0, j)), (wu, (K_SHARDS, Fs, tn), lambda b, i, j, k: (k, 0, j))],
            [(0, 2, NN, 0), (1, 3, NN, 0)], [], [((S, D), F32, (tm, tn), lambda b, i, j, k: (i, j))],
            lambda accs, sides: accs, [(tm, tn)], deps=deps)[0]

    return grad_down, grad_gate_up, du


def _ssm_params(lam_re, lam_im, log_dt, b_re, b_im, c_re, c_im):
    G, N = lam_re.shape
    C = b_re.shape[2]
    lam_re = jnp.minimum(lam_re, -1e-4)
    dt = jnp.exp(log_dt)[:, None]
    mag = jnp.exp(lam_re * dt)
    a_re = mag * jnp.cos(lam_im * dt)
    a_im = mag * jnp.sin(lam_im * dt)
    den = lam_re * lam_re + lam_im * lam_im
    p = a_re - 1.0
    f_re = ((p * lam_re + a_im * lam_im) / den)[:, :, None]
    f_im = ((a_im * lam_re - p * lam_im) / den)[:, :, None]
    bb_re = f_re * b_re - f_im * b_im
    bb_im = f_re * b_im + f_im * b_re
    gpt = LANE // C
    tiles = G // gpt
    eye = jnp.eye(gpt, dtype=F32)

    def bd(bb):
        return jnp.einsum("bgnc,gh->bgchn", bb.reshape(tiles, gpt, N, C), eye).reshape(tiles, gpt * C, gpt * N)

    def cd(cc):
        return jnp.einsum("bgcn,gh->bgnhc", cc.reshape(tiles, gpt, C, N), eye).reshape(tiles, gpt * N, gpt * C)

    rows = G * N // LANE
    return (a_re.reshape(rows, LANE), a_im.reshape(rows, LANE), bd(bb_re), bd(bb_im), cd(c_re), cd(-c_im))


def _tile_states(ref3, b, per):
    return jnp.concatenate([ref3[:, per * b + r, :] for r in range(per)], axis=1).astype(BF)


def _ssm_spread(name, x, m_re, m_im, dims, rows):
    S, W = x.shape
    tiles = m_re.shape[0]
    tch, per = W // tiles, rows // tiles
    tq = _pick(S, 256, SUBLANE)

    def body(x_ref, mre_ref, mim_ref, ore_ref, oim_ref):
        for b in range(tiles):
            xb = x_ref[:, b * tch:(b + 1) * tch]
            for m_ref, o_ref in ((mre_ref, ore_ref), (mim_ref, oim_ref)):
                val = lax.dot_general(xb, m_ref[b], (dims, ((), ())), preferred_element_type=F32)
                for r in range(per):
                    o_ref[:, per * b + r, :] = val[:, r * LANE:(r + 1) * LANE]

    whole = lambda m: pl.BlockSpec(m.shape, lambda i: (0, 0, 0))
    st = pl.BlockSpec((tq, rows, LANE), lambda i: (i, 0, 0))
    return pl.pallas_call(
        body, name=name, grid=(S // tq,),
        in_specs=[pl.BlockSpec((tq, W), lambda i: (i, 0)), whole(m_re), whole(m_im)], out_specs=[st, st],
        out_shape=[jax.ShapeDtypeStruct((S, rows, LANE), F32)] * 2,
        compiler_params=pltpu.CompilerParams(dimension_semantics=("parallel",)),
    )(x, m_re, m_im)


def _ssm_collect(name, z_re3, z_im3, m_re, m_im, dims, side, gain, epilogue, out_dtypes):
    S, rows, _ = z_re3.shape
    tiles = m_re.shape[0]
    W = side.shape[1]
    tch, per = W // tiles, rows // tiles
    tq = _pick(S, 256, SUBLANE)
    n_out = len(out_dtypes)

    def body(zre_ref, zim_ref, mre_ref, mim_ref, side_ref, gain_ref, *out_refs):
        for b in range(tiles):
            cols = slice(b * tch, (b + 1) * tch)
            zre, zim = _tile_states(zre_ref, b, per), _tile_states(zim_ref, b, per)
            acc = lax.dot_general(zre, mre_ref[b], (dims, ((), ())), preferred_element_type=F32)
            acc = acc + lax.dot_general(zim, mim_ref[b], (dims, ((), ())), preferred_element_type=F32)
            for o, v in zip(out_refs[:n_out], epilogue(acc, side_ref[:, cols], gain_ref[:, cols])):
                o[:, cols] = v.astype(o.dtype)
            out_refs[n_out][:, b * per * LANE:(b + 1) * per * LANE] = zre
            out_refs[n_out + 1][:, b * per * LANE:(b + 1) * per * LANE] = zim

    whole = lambda m: pl.BlockSpec(m.shape, lambda i: (0, 0, 0))
    st = pl.BlockSpec((tq, rows, LANE), lambda i: (i, 0, 0))
    ch = pl.BlockSpec((tq, W), lambda i: (i, 0))
    flat = pl.BlockSpec((tq, rows * LANE), lambda i: (i, 0))
    return pl.pallas_call(
        body, name=name, grid=(S // tq,),
        in_specs=[st, st, whole(m_re), whole(m_im), ch, pl.BlockSpec((1, W), lambda i: (0, 0))],
        out_specs=[ch] * n_out + [flat, flat],
        out_shape=[jax.ShapeDtypeStruct((S, W), dt) for dt in out_dtypes]
        + [jax.ShapeDtypeStruct((S, rows * LANE), BF)] * 2,
        compiler_params=pltpu.CompilerParams(dimension_semantics=("parallel",)),
    )(z_re3, z_im3, m_re, m_im, side, gain)


def _scan_fwd(bu_re, bu_im, a_re, a_im):
    S, R, _ = bu_re.shape
    tc = _pick(S, 256, SUBLANE)

    def body(bre, bim, are, aim, sre, sim, carry):
        @pl.when(pl.program_id(0) == 0)
        def _():
            carry[...] = jnp.zeros_like(carry)

        ar, ai = are[...], aim[...]

        def step(t, c):
            pr, pi = c
            nr = ar * pr - ai * pi + bre[t]
            ni = ar * pi + ai * pr + bim[t]
            sre[t] = nr
            sim[t] = ni
            return nr, ni

        pr, pi = lax.fori_loop(0, tc, step, (carry[0], carry[1]), unroll=8)
        carry[0] = pr
        carry[1] = pi

    blk = pl.BlockSpec((tc, R, LANE), lambda i: (i, 0, 0))
    par = pl.BlockSpec((R, LANE), lambda i: (0, 0))
    return pl.pallas_call(
        body, name="ssm_scan_fwd", grid=(S // tc,),
        in_specs=[blk, blk, par, par], out_specs=[blk, blk],
        out_shape=[jax.ShapeDtypeStruct((S, R, LANE), F32)] * 2,
        scratch_shapes=[pltpu.VMEM((2, R, LANE), F32)],
        compiler_params=pltpu.CompilerParams(dimension_semantics=("arbitrary",)),
    )(bu_re, bu_im, a_re, a_im)


def _scan_bwd(ds_re, ds_im, s_re, s_im, a_re, a_im):
    S, R, _ = ds_re.shape
    tc = _pick(S, 256, SUBLANE)
    nc = S // tc

    def body(dre, dim_, sre, sim, are, aim, lre, lim, dar, dai, carry):
        @pl.when(pl.program_id(0) == 0)
        def _():
            carry[...] = jnp.zeros_like(carry)
            dar[...] = jnp.zeros_like(dar)
            dai[...] = jnp.zeros_like(dai)

        ar, ai = are[...], aim[...]

        def step(tt, c):
            t = tc - 1 - tt
            lr, li, gr, gi = c
            sr, si = sre[t], sim[t]
            gr = gr + lr * sr + li * si
            gi = gi + li * sr - lr * si
            nlr = dre[t] + ar * lr + ai * li
            nli = dim_[t] + ar * li - ai * lr
            lre[t] = nlr
            lim[t] = nli
            return nlr, nli, gr, gi

        lr, li, gr, gi = lax.fori_loop(0, tc, step, (carry[0], carry[1], dar[...], dai[...]), unroll=8)
        carry[0] = lr
        carry[1] = li
        dar[...] = gr
        dai[...] = gi

    blk = pl.BlockSpec((tc, R, LANE), lambda i: (nc - 1 - i, 0, 0))
    par = pl.BlockSpec((R, LANE), lambda i: (0, 0))
    return pl.pallas_call(
        body, name="ssm_scan_bwd", grid=(nc,),
        in_specs=[blk, blk, blk, blk, par, par], out_specs=[blk, blk, par, par],
        out_shape=[jax.ShapeDtypeStruct((S, R, LANE), F32)] * 2 + [jax.ShapeDtypeStruct((R, LANE), F32)] * 2,
        scratch_shapes=[pltpu.VMEM((2, R, LANE), F32)],
        compiler_params=pltpu.CompilerParams(dimension_semantics=("arbitrary",)),
    )(ds_re, ds_im, s_re, s_im, a_re, a_im)


def _shift_down(z, k):
    t = lax.broadcasted_iota(I32, z.shape, 0)
    return jnp.where(t >= k, pltpu.roll(z, k, 0), 0.0)


def _shift_up(z, k):
    n = z.shape[0]
    t = lax.broadcasted_iota(I32, z.shape, 0)
    return jnp.where(t < n - k, pltpu.roll(z, n - k, 0), 0.0)


def _conv_fwd(proj, cw, cb):
    _, S, W = proj.shape
    ct = _pick(W, 256, LANE)

    def fn(bg, cg, val, w, b):
        z = cg * val
        conv = b + w[0:1] * _shift_down(z, 2) + w[1:2] * _shift_down(z, 1) + w[2:3] * z
        return bg * conv, conv

    sl = lambda s: (proj, (None, S, ct), lambda j, s=s: (s, 0, j))
    col = ((S, ct), lambda j: (0, j))
    return _ew("conv_fwd", (W // ct,),
               [sl(1), sl(2), sl(3), (cw, (3, ct), lambda j: (0, j)), (cb, (1, ct), lambda j: (0, j))],
               [((S, W), BF) + col, ((S, W), F32) + col], fn)


def _conv_bwd(dyb, proj, conv, cw):
    _, S, W = proj.shape
    ct = _pick(W, 256, LANE)

    def fn(dy, bg, cg, val, cv, w):
        z = cg * val
        z1, z2 = _shift_down(z, 1), _shift_down(z, 2)
        dconv = dy * bg
        dz = w[2:3] * dconv + w[1:2] * _shift_up(dconv, 1) + w[0:1] * _shift_up(dconv, 2)
        dw = jnp.concatenate([jnp.sum(dconv * z2, axis=0, keepdims=True), jnp.sum(dconv * z1, axis=0, keepdims=True),
                              jnp.sum(dconv * z, axis=0, keepdims=True)], axis=0)
        return dy * cv, dz * val, dz * cg, dw, jnp.sum(dconv, axis=0, keepdims=True)

    sl = lambda s: (proj, (None, S, ct), lambda j, s=s: (s, 0, j))
    col = ((S, ct), lambda j: (0, j))
    return _ew("conv_bwd", (W // ct,),
               [(dyb,) + col, sl(1), sl(2), sl(3), (conv,) + col, (cw, (3, ct), lambda j: (0, j))],
               [((S, W), BF) + col, ((S, W), BF) + col, ((S, W), BF) + col,
                ((3, W), F32, (3, ct), lambda j: (0, j)), ((1, W), F32, (1, ct), lambda j: (0, j))], fn)


def _plain(accs, sides):
    return accs


def kernel(x, ffn1_norm, ffn1_w_gate, ffn1_w_up, ffn1_w_down, mix_norm, w_in, ssm_lambda_re, ssm_lambda_im, ssm_log_dt, ssm_b_re, ssm_b_im, ssm_c_re, ssm_c_im, ssm_d, ssm_w_glu, ssm_b_glu, ssm_w_out, conv_w, conv_b, conv_w_out, w_o, ffn2_norm, ffn2_w_gate, ffn2_w_up, ffn2_w_down, final_norm, loss_target, m_ffn1_norm, m_ffn1_w_gate, m_ffn1_w_up, m_ffn1_w_down, m_mix_norm, m_w_in, m_ssm_lambda_re, m_ssm_lambda_im, m_ssm_log_dt, m_ssm_b_re, m_ssm_b_im, m_ssm_c_re, m_ssm_c_im, m_ssm_d, m_ssm_w_glu, m_ssm_b_glu, m_ssm_w_out, m_conv_w, m_conv_b, m_conv_w_out, m_w_o, m_ffn2_norm, m_ffn2_w_gate, m_ffn2_w_up, m_ffn2_w_down, m_final_norm, v_ffn1_norm, v_ffn1_w_gate, v_ffn1_w_up, v_ffn1_w_down, v_mix_norm, v_w_in, v_ssm_lambda_re, v_ssm_lambda_im, v_ssm_log_dt, v_ssm_b_re, v_ssm_b_im, v_ssm_c_re, v_ssm_c_im, v_ssm_d, v_ssm_w_glu, v_ssm_b_glu, v_ssm_w_out, v_conv_w, v_conv_b, v_conv_w_out, v_w_o, v_ffn2_norm, v_ffn2_w_gate, v_ffn2_w_up, v_ffn2_w_down, v_final_norm):
    P = dict(ffn1_norm=ffn1_norm, ffn1_w_gate=ffn1_w_gate, ffn1_w_up=ffn1_w_up, ffn1_w_down=ffn1_w_down, mix_norm=mix_norm, w_in=w_in, ssm_lambda_re=ssm_lambda_re, ssm_lambda_im=ssm_lambda_im, ssm_log_dt=ssm_log_dt, ssm_b_re=ssm_b_re, ssm_b_im=ssm_b_im, ssm_c_re=ssm_c_re, ssm_c_im=ssm_c_im, ssm_d=ssm_d, ssm_w_glu=ssm_w_glu, ssm_b_glu=ssm_b_glu, ssm_w_out=ssm_w_out, conv_w=conv_w, conv_b=conv_b, conv_w_out=conv_w_out, w_o=w_o, ffn2_norm=ffn2_norm, ffn2_w_gate=ffn2_w_gate, ffn2_w_up=ffn2_w_up, ffn2_w_down=ffn2_w_down, final_norm=final_norm)
    M = dict(ffn1_norm=m_ffn1_norm, ffn1_w_gate=m_ffn1_w_gate, ffn1_w_up=m_ffn1_w_up, ffn1_w_down=m_ffn1_w_down, mix_norm=m_mix_norm, w_in=m_w_in, ssm_lambda_re=m_ssm_lambda_re, ssm_lambda_im=m_ssm_lambda_im, ssm_log_dt=m_ssm_log_dt, ssm_b_re=m_ssm_b_re, ssm_b_im=m_ssm_b_im, ssm_c_re=m_ssm_c_re, ssm_c_im=m_ssm_c_im, ssm_d=m_ssm_d, ssm_w_glu=m_ssm_w_glu, ssm_b_glu=m_ssm_b_glu, ssm_w_out=m_ssm_w_out, conv_w=m_conv_w, conv_b=m_conv_b, conv_w_out=m_conv_w_out, w_o=m_w_o, ffn2_norm=m_ffn2_norm, ffn2_w_gate=m_ffn2_w_gate, ffn2_w_up=m_ffn2_w_up, ffn2_w_down=m_ffn2_w_down, final_norm=m_final_norm)
    V = dict(ffn1_norm=v_ffn1_norm, ffn1_w_gate=v_ffn1_w_gate, ffn1_w_up=v_ffn1_w_up, ffn1_w_down=v_ffn1_w_down, mix_norm=v_mix_norm, w_in=v_w_in, ssm_lambda_re=v_ssm_lambda_re, ssm_lambda_im=v_ssm_lambda_im, ssm_log_dt=v_ssm_log_dt, ssm_b_re=v_ssm_b_re, ssm_b_im=v_ssm_b_im, ssm_c_re=v_ssm_c_re, ssm_c_im=v_ssm_c_im, ssm_d=v_ssm_d, ssm_w_glu=v_ssm_w_glu, ssm_b_glu=v_ssm_b_glu, ssm_w_out=v_ssm_w_out, conv_w=v_conv_w, conv_b=v_conv_b, conv_w_out=v_conv_w_out, w_o=v_w_o, ffn2_norm=v_ffn2_norm, ffn2_w_gate=v_ffn2_w_gate, ffn2_w_up=v_ffn2_w_up, ffn2_w_down=v_ffn2_w_down, final_norm=v_final_norm)
    names = list(P)
    sharded = ["ffn1_w_gate", "ffn1_w_up", "ffn1_w_down", "w_in", "ssm_w_glu", "ssm_w_out", "conv_w_out", "w_o",
               "ffn2_w_gate", "ffn2_w_up", "ffn2_w_down"]
    replicated = [n for n in names if n not in sharded and n != "conv_w"]

    S, D = x.shape[1], x.shape[2]
    W = ssm_d.shape[0]
    Dc = D // NDEV
    G, N = ssm_lambda_re.shape
    rows = G * N // LANE
    xh = x.reshape(S, D)
    target = loss_target.reshape(S, D)
    xi, yi, ci = lax.axis_index("x"), lax.axis_index("y"), lax.axis_index("c")
    c_arr = jnp.reshape(ci, (1,)).astype(I32)
    q_arr = jnp.reshape(2 * xi + yi, (1,)).astype(I32)
    row = lambda v: v.reshape(1, -1)

    transposed = ("ffn1_w_gate", "ffn1_w_up", "ffn2_w_gate", "ffn2_w_up")
    local = lambda table, n: table[n].T if n in transposed else table[n]

    groups = [sharded[0:1], sharded[1:2], sharded[2:3], ["w_in", "conv_w"], sharded[4:8],
              sharded[8:9], sharded[9:10], sharded[10:11]]
    first_begun, second_begun, chain = {}, {}, []
    me = 4 * xi + 2 * yi + ci

    def gather_first(gi):
        hold = 0.0 * chain[0][0, 0] if chain else 0.0
        srcs = [conv_w + hold if n == "conv_w" else (local(P, n) + hold).astype(BF) for n in groups[gi]]
        lands = [lax.dynamic_update_slice(lax.empty((NDEV,) + s.shape, s.dtype), s[None], (me,) + (0,) * s.ndim)
                 for s in srcs]
        first_begun[gi] = _split_start("gather_first_start_%d" % gi, srcs, lands, _gather_first_copies, 3, chain[-1:])
        chain.append(first_begun[gi][4])

    def gather_second(gi, after):
        _, lands = _split_wait("gather_first_wait_%d" % gi, first_begun[gi], _gather_first_copies, after)
        second_begun[gi] = _split_start("gather_second_start_%d" % gi, [], lands, _gather_second_copies, 2, chain[-1:])
        chain.append(second_begun[gi][4])

    def gathered(gi, after):
        _, lands = _split_wait("gather_second_wait_%d" % gi, second_begun[gi], _gather_second_copies, after)
        return _gather_forward("gather_forward_%d" % gi, lands, chain[-1:])

    def fetch(gi, seconds, firsts):
        def get(after):
            for g in seconds:
                gather_second(g, after)
            for g in firsts:
                gather_first(g)
            return gathered(gi, after)
        return get

    tm = _pick(S, M_TILE, SUBLANE)
    th = _pick(S, M_TILE // 2, SUBLANE)
    tk = _pick(D, K_TILE, LANE)
    ts = _pick(S, K_TILE, SUBLANE)
    tn = _pick(D, 1024, LANE)

    gather_first(0)
    gather_first(1)
    u1, r1 = _rms_fwd("rms1", xh, row(ffn1_norm), deps=list(chain))
    gather_second(0, u1)
    gather_first(2)
    gather_second(1, u1)
    gather_first(3)
    (wg1,) = gathered(0, u1)
    h1, ffn1_saved, wu1, wd1 = _ffn_fwd("ffn1", u1, xh, wg1, lambda after: fetch(1, [2], [4])(after)[0],
                                        lambda after: fetch(2, [3], [5])(after)[0])
    u2, r2 = _rms_fwd("rms2", h1, row(mix_norm))
    w_in_f, cw_f = fetch(3, [4], [6])(u2)
    cw = jnp.transpose(cw_f, (1, 0, 2)).reshape(3, W)
    (proj,) = _mm(
        "in_proj", (NDEV, S // tm, 1, D // tk),
        [(u2, (tm, tk), lambda b, i, j, k: (i, k)), (w_in_f, (None, tk, W), lambda b, i, j, k: (b, k, 0))],
        [(0, 1, NN, 0)], [], [((NDEV, S, W), F32, (None, tm, W), lambda b, i, j, k: (b, i, 0))], _plain, [(tm, W)])

    ssm_in = (ssm_lambda_re, ssm_lambda_im, ssm_log_dt, ssm_b_re, ssm_b_im, ssm_c_re, ssm_c_im)
    (a_re, a_im, bd_re, bd_im, cd_re, cd_imn), ssm_vjp = jax.vjp(_ssm_params, *ssm_in)
    bd_re_b, bd_im_b, cd_re_b, cd_imn_b = (t.astype(BF) for t in (bd_re, bd_im, cd_re, cd_imn))
    v_f = proj[0]
    v_bf = v_f.astype(BF)
    bu_re3, bu_im3 = _ssm_spread("ssm_bu", v_bf, bd_re_b, bd_im_b, NN, rows)
    s_re3, s_im3 = _scan_fwd(bu_re3, bu_im3, a_re, a_im)

    def y0_epilogue(acc, v_tile, d_tile):
        y0 = acc + d_tile * v_tile
        return y0, _gelu(y0)

    y0, y1, s_re_b, s_im_b = _ssm_collect("ssm_y0", s_re3, s_im3, cd_re_b, cd_imn_b, NN, v_f, row(ssm_d),
                                          y0_epilogue, [F32, BF])

    tw = _pick(W, 512, LANE)
    w_glu_f, w_so, w_co, w_o_f = fetch(4, [5], [7])(y1)
    w_glu_f = w_glu_f.reshape(W, W)
    w_o_f = w_o_f.reshape(D, D)

    def glu_epilogue(accs, sides):
        q = accs[0] + sides[1]
        return q, _gelu(sides[0]) * _sigmoid(q)

    q_pre, y2 = _mm("ssm_glu", (1, S // tm, W // tw, 1),
                    [(y1, (tm, W), lambda b, i, j, k: (i, 0)), (w_glu_f, (W, tw), lambda b, i, j, k: (0, j))],
                    [(0, 1, NN, 0)],
                    [(y0, (tm, tw), lambda b, i, j, k: (i, j)), (row(ssm_b_glu), (1, tw), lambda b, i, j, k: (0, j))],
                    [((S, W), F32, (tm, tw), lambda b, i, j, k: (i, j)), ((S, W), BF, (tm, tw), lambda b, i, j, k: (i, j))],
                    glu_epilogue, [(tm, tw)])

    yb, conv = _conv_fwd(proj, cw, row(conv_b))

    per = W // Dc
    ga_blk = (proj, (None, tm, Dc), lambda b, i, j, k: (4 + b // per, i, b % per))
    gb_blk = (proj, (None, tm, Dc), lambda b, i, j, k: (6 + b // per, i, b % per))
    dc_out = ((S, D), BF, (tm, Dc), lambda b, i, j, k: (i, b))

    def merge_epilogue(accs, sides):
        za, zb = accs
        return _sigmoid(sides[0]) * za + _sigmoid(sides[1]) * zb, za, zb

    merged, z_a, z_b = _mm(
        "mix_merge", (NDEV, S // tm, 1, 1),
        [(y2, (tm, W), lambda b, i, j, k: (i, 0)), (yb, (tm, W), lambda b, i, j, k: (i, 0)),
         (w_so, (None, W, Dc), lambda b, i, j, k: (b, 0, 0)), (w_co, (None, W, Dc), lambda b, i, j, k: (b, 0, 0))],
        [(0, 2, NN, 0), (1, 3, NN, 1)], [ga_blk, gb_blk], [dc_out, dc_out, dc_out], merge_epilogue, [(tm, Dc)] * 2)

    (h2,) = _mm("mix_out", (1, S // tm, D // tn, D // tk),
                [(merged, (tm, tk), lambda b, i, j, k: (i, k)), (w_o_f, (tk, tn), lambda b, i, j, k: (k, j))],
                [(0, 1, NN, 0)], [(h1, (tm, tn), lambda b, i, j, k: (i, j))],
                [((S, D), F32, (tm, tn), lambda b, i, j, k: (i, j))],
                lambda accs, sides: [sides[0] + accs[0]], [(tm, tn)])

    u3, r3 = _rms_fwd("rms3", h2, row(ffn2_norm))
    (wg2,) = fetch(5, [6, 7], [])(u3)
    h3, ffn2_saved, wu2, wd2 = _ffn_fwd("ffn2", u3, h2, wg2, lambda after: gathered(6, after)[0],
                                        lambda after: gathered(7, after)[0])
    loss_vec, dh3, dh3_half, d_final_norm = _loss_head("loss_head", h3, row(final_norm), target)
    loss = lax.psum(loss_vec[0, 0], ("x", "y", "c"))
    loss_done = jnp.zeros((SUBLANE, LANE), F32) + loss

    grads, deltas, new_m, new_v = {}, {}, {}, {}

    def rs_sibling_start(tag, parts):
        lands = [lax.empty((4,) + p.shape[1:], p.dtype) for p in parts]
        return _split_start("rs_sibling_start_" + tag, parts, lands, _sibling_copies, 4)

    def rs_chips_start(tag, sibling_begun, after):
        parts, lands = _split_wait("rs_sibling_wait_" + tag, sibling_begun, _sibling_copies, after)
        sums = [_sum_sibling("rs_sum_%s_%d" % (tag, a), p, land, c_arr) for a, (p, land) in enumerate(zip(parts, lands))]
        lands2 = [lax.empty((3,) + sm.shape[1:], sm.dtype) for sm in sums]
        return _split_start("rs_chips_start_" + tag, sums, lands2, _chips_copies, 3)

    def rs_end(tag, group, begun, after):
        sums, lands2 = _split_wait("rs_chips_wait_" + tag, begun, _chips_copies, after)
        for n, sm, land2 in zip(group, sums, lands2):
            res = _finish_sharded("adamw_" + n, sm, land2, q_arr, local(P, n), local(M, n), local(V, n))
            grads[n], deltas[n], new_m[n], new_v[n] = [t.T if n in transposed else t for t in res]

    f2_dwd, f2_dwgu, f2_du = _ffn_bwd("ffn2", dh3_half, u3, ffn2_saved, wg2, wu2, wd2, deps=[loss_done])
    dwd2 = f2_dwd()
    dwg2, dwu2 = f2_dwgu()
    sib_ffn2 = rs_sibling_start("ffn2", [dwg2, dwu2, dwd2])
    du3 = f2_du(deps=[sib_ffn2[4]])
    dh2, dh2_b, d_ffn2_norm = _rms_bwd("rms3_bwd", du3, h2, r3, row(ffn2_norm), dh3, 1.0)
    rs_ffn2 = rs_chips_start("ffn2", sib_ffn2, dh2)

    dg_out = ((2, S, W), BF, (None, tm, Dc), lambda b, i, j, k: (j // per, i, j % per))
    ga_blk2 = (proj, (None, tm, Dc), lambda b, i, j, k: (4 + j // per, i, j % per))
    gb_blk2 = (proj, (None, tm, Dc), lambda b, i, j, k: (6 + j // per, i, j % per))
    dcj = lambda arr: (arr, (tm, Dc), lambda b, i, j, k: (i, j))
    dcj_out = ((S, D), BF, (tm, Dc), lambda b, i, j, k: (i, j))

    def dmerge_epilogue(accs, sides):
        dm = accs[0]
        sa, sb = _sigmoid(sides[0]), _sigmoid(sides[1])
        za, zb = sides[2].astype(F32), sides[3].astype(F32)
        return dm * sa, dm * sb, dm * za * sa * (1.0 - sa), dm * zb * sb * (1.0 - sb)

    dz_a, dz_b, dga, dgb = _mm(
        "mix_out_dx", (1, S // tm, NDEV, D // tk),
        [(dh2_b, (tm, tk), lambda b, i, j, k: (i, k)), (w_o_f, (Dc, tk), lambda b, i, j, k: (j, k))],
        [(0, 1, NT, 0)], [ga_blk2, gb_blk2, dcj(z_a), dcj(z_b)], [dcj_out, dcj_out, dg_out, dg_out],
        dmerge_epilogue, [(tm, Dc)], deps=[rs_ffn2[4]])

    td = _pick(D, M_TILE, LANE)
    (dw_o,) = _mm("mix_out_dw", (1, D // td, D // tn, S // ts),
                  [(merged, (ts, td), lambda b, i, j, k: (k, i)), (dh2_b, (ts, tn), lambda b, i, j, k: (k, j))],
                  [(0, 1, TN, 0)], [], [((D, D), BF, (td, tn), lambda b, i, j, k: (i, j))], _plain, [(td, tn)])

    wout = ((NDEV, W, Dc), BF, (None, W, Dc), lambda b, i, j, k: (b, 0, 0))
    dw_so, dw_co = _mm(
        "mix_merge_dw", (NDEV, 1, 1, S // ts),
        [(y2, (ts, W), lambda b, i, j, k: (k, 0)), (yb, (ts, W), lambda b, i, j, k: (k, 0)),
         (dz_a, (ts, Dc), lambda b, i, j, k: (k, b)), (dz_b, (ts, Dc), lambda b, i, j, k: (k, b))],
        [(0, 2, TN, 0), (1, 3, TN, 1)], [], [wout, wout], _plain, [(W, Dc)] * 2)

    def dglu_epilogue(accs, sides):
        dy2, dyb = accs
        sq = _sigmoid(sides[1])
        return dy2 * _gelu(sides[0]) * sq * (1.0 - sq), dy2 * sq, dyb

    full_w = lambda arr: (arr, (th, W), lambda b, i, j, k: (i, 0))
    full_w_out = lambda dt: ((S, W), dt, (th, W), lambda b, i, j, k: (i, 0))
    dq, dy1p, dyb = _mm(
        "mix_merge_dx", (1, S // th, 1, NDEV),
        [(dz_a, (th, Dc), lambda b, i, j, k: (i, k)), (dz_b, (th, Dc), lambda b, i, j, k: (i, k)),
         (w_so, (None, W, Dc), lambda b, i, j, k: (k, 0, 0)), (w_co, (None, W, Dc), lambda b, i, j, k: (k, 0, 0))],
        [(0, 2, NT, 0), (1, 3, NT, 1)], [full_w(y0), full_w(q_pre)], [full_w_out(BF), full_w_out(F32), full_w_out(F32)],
        dglu_epilogue, [(th, W)] * 2)

    def dy0_epilogue(accs, sides):
        dy0 = (sides[0] + accs[0]) * _gelu_grad(sides[1])
        return dy0, dy0

    wj = lambda arr: (arr, (tm, tw), lambda b, i, j, k: (i, j))
    dy0, dy0_b = _mm("ssm_glu_dx", (1, S // tm, W // tw, 1),
                     [(dq, (tm, W), lambda b, i, j, k: (i, 0)), (w_glu_f, (tw, W), lambda b, i, j, k: (j, 0))],
                     [(0, 1, NT, 0)], [wj(dy1p), wj(y0)],
                     [((S, W), F32, (tm, tw), lambda b, i, j, k: (i, j)), ((S, W), BF, (tm, tw), lambda b, i, j, k: (i, j))],
                     dy0_epilogue, [(tm, tw)])

    (dw_glu,) = _mm("ssm_glu_dw", (1, W // tw, 1, S // ts),
                    [(y1, (ts, tw), lambda b, i, j, k: (k, i)), (dq, (ts, W), lambda b, i, j, k: (k, 0))],
                    [(0, 1, TN, 0)], [], [((W, W), BF, (tw, W), lambda b, i, j, k: (i, 0))], _plain, [(tw, W)])

    tr = _pick(S, 256, SUBLANE)
    rw = ((tr, W), lambda i: (i, 0))
    vec_w = ((1, W), F32, (1, W), lambda i: (0, 0))
    d_b_glu, d_ssm_d = _ew(
        "ssm_colsums", (S // tr,), [(dq,) + rw, (dy0,) + rw, (proj, (None, tr, W), lambda i: (0, i, 0))],
        [vec_w, vec_w],
        lambda dqv, dyv, vv: (jnp.sum(dqv.astype(F32), axis=0, keepdims=True), jnp.sum(dyv * vv, axis=0, keepdims=True)),
        acc=(0, 1))

    ds_re3, ds_im3 = _ssm_spread("ssm_ds", dy0_b, cd_re_b, cd_imn_b, NT, rows)
    lam_re3, lam_im3, da_re, da_im = _scan_bwd(ds_re3, ds_im3, s_re3, s_im3, a_re, a_im)
    dv, lam_re_b, lam_im_b = _ssm_collect("ssm_dv", lam_re3, lam_im3, bd_re_b, bd_im_b, NT, dy0, row(ssm_d),
                                          lambda acc, dy_tile, d_tile: [acc + dy_tile * d_tile], [BF])
    tiles, tch, tst = bd_re.shape
    tok_ch = lambda arr: (arr, (ts, tch), lambda b, i, j, k: (k, b))
    tok_st = lambda arr: (arr, (ts, tst), lambda b, i, j, k: (k, b))
    bd_out = ((tiles, tch, tst), F32, (None, tch, tst), lambda b, i, j, k: (b, 0, 0))
    cd_out = ((tiles, tst, tch), F32, (None, tst, tch), lambda b, i, j, k: (b, 0, 0))
    dbd_re, dbd_im = _mm("ssm_dbd", (tiles, 1, 1, S // ts), [tok_ch(v_bf), tok_st(lam_re_b), tok_st(lam_im_b)],
                         [(0, 1, TN, 0), (0, 2, TN, 1)], [], [bd_out, bd_out], _plain, [(tch, tst)] * 2)
    dcd_re, dcd_imn = _mm("ssm_dcd", (tiles, 1, 1, S // ts), [tok_st(s_re_b), tok_st(s_im_b), tok_ch(dy0_b)],
                          [(0, 2, TN, 0), (1, 2, TN, 1)], [], [cd_out, cd_out], _plain, [(tst, tch)] * 2)
    d_ssm = ssm_vjp((da_re, da_im, dbd_re, dbd_im, dcd_re, dcd_imn))

    dbg, dcg, dval, d_conv_w_full, d_conv_b = _conv_bwd(dyb, proj, conv, cw)
    dproj = jnp.concatenate([dv[None], dbg[None], dcg[None], dval[None], dga, dgb], axis=0)

    (dw_in,) = _mm("in_proj_dw", (NDEV, D // td, 1, S // ts),
                   [(u2, (ts, td), lambda b, i, j, k: (k, i)), (dproj, (None, ts, W), lambda b, i, j, k: (b, k, 0))],
                   [(0, 1, TN, 0)], [], [((NDEV, D, W), BF, (None, td, W), lambda b, i, j, k: (b, i, 0))],
                   _plain, [(td, W)])
    sib_mixer = rs_sibling_start(
        "mixer", [dw_in, dw_glu.reshape(NDEV, W // NDEV, W), dw_so, dw_co, dw_o.reshape(NDEV, Dc, D)])
    (du2,) = _mm("in_proj_dx", (1, S // tm, D // tn, NDEV // K_SHARDS),
                 [(dproj, (K_SHARDS, tm, W), lambda b, i, j, k: (k, i, 0)),
                  (w_in_f, (K_SHARDS, tn, W), lambda b, i, j, k: (k, j, 0))],
                 [(0, 1, NT, 0)], [], [((S, D), F32, (tm, tn), lambda b, i, j, k: (i, j))], _plain, [(tm, tn)],
                 deps=[sib_mixer[4]])
    dh1, dh1_half, d_mix_norm = _rms_bwd("rms2_bwd", du2, h1, r2, row(mix_norm), dh2, 0.5)
    rs_mixer = rs_chips_start("mixer", sib_mixer, dh1)

    small = dict(mix_norm=d_mix_norm, ffn2_norm=d_ffn2_norm, final_norm=d_final_norm,
                 ssm_lambda_re=d_ssm[0], ssm_lambda_im=d_ssm[1], ssm_log_dt=d_ssm[2], ssm_b_re=d_ssm[3],
                 ssm_b_im=d_ssm[4], ssm_c_re=d_ssm[5], ssm_c_im=d_ssm[6], ssm_d=d_ssm_d, ssm_b_glu=d_b_glu,
                 conv_b=d_conv_b)
    replicated = [n for n in replicated if n != "ffn1_norm"] + ["ffn1_norm"]
    tile = SUBLANE * LANE

    def as_rows(p):
        flat = p.reshape(-1).astype(F32)
        return jnp.pad(flat, (0, -flat.shape[0] % tile)).reshape(-1, LANE)

    def pack(parts):
        return jnp.concatenate([as_rows(p) for p in parts], axis=0)

    cw_zero = jnp.zeros_like(d_conv_w_full)
    early_pk = pack([small[n] for n in replicated[:-1]] + [d_conv_w_full])
    early_land = lax.dynamic_update_slice(jnp.zeros((NDEV,) + early_pk.shape, F32), early_pk[None], (me, 0, 0))
    small_begun = _split_start("gather_small_start", [early_pk], [early_land], _everyone_copies, NDEV - 1)

    f1_dwd, f1_dwgu, f1_du = _ffn_bwd("ffn1", dh1_half, u1, ffn1_saved, wg1, wu1, wd1,
                                      deps=[rs_mixer[4], small_begun[4]])
    du1 = f1_du()
    dx, _, d_ffn1_norm = _rms_bwd("rms1_bwd", du1, xh, r1, row(ffn1_norm), dh1, 1.0)
    dwd1 = f1_dwd(deps=[d_ffn1_norm])
    late = d_ffn1_norm + 0.0 * dwd1[0, :1, :1].astype(F32)
    (late_all,) = _all_gather("gather_ffn1_norm_grad", [late.reshape(-1, LANE)])
    rs_ffn1_down = rs_chips_start("ffn1_down", rs_sibling_start("ffn1_down", [dwd1]), late_all)
    (dwg1,) = f1_dwgu(deps=[rs_ffn1_down[4]], only=0)
    rs_ffn1_gate = rs_chips_start("ffn1_gate", rs_sibling_start("ffn1_gate", [dwg1]), None)
    (dwu1,) = f1_dwgu(deps=[rs_ffn1_gate[4]], only=1)
    rs_ffn1_up = rs_chips_start("ffn1_up", rs_sibling_start("ffn1_up", [dwu1]), None)
    rs_end("ffn2", sharded[8:11], rs_ffn2, rs_ffn1_up[4])
    rs_end("mixer", sharded[3:8], rs_mixer, [grads[n] for n in sharded[8:11]])
    _, (early_all,) = _split_wait("gather_small_wait", small_begun, _everyone_copies, [grads[n] for n in sharded[3:8]])
    small_all = jnp.concatenate([early_all, late_all], axis=1)
    g_pk, d_pk, m_pk, v_pk = _finish_replicated(
        "adamw_replicated", small_all, pack([P[n] for n in replicated[:-1]] + [cw_zero, P["ffn1_norm"]]),
        pack([M[n] for n in replicated[:-1]] + [cw_zero, M["ffn1_norm"]]),
        pack([V[n] for n in replicated[:-1]] + [cw_zero + 1.0, V["ffn1_norm"]]))

    def unpack(pk, r0, like):
        nr = -(-like.size // tile) * SUBLANE
        return pk[r0:r0 + nr].reshape(-1)[:like.size].reshape(like.shape), r0 + nr

    r0 = 0
    for n in replicated[:-1] + ["conv_w", "ffn1_norm"]:
        like = d_conv_w_full if n == "conv_w" else P[n]
        for store, pk in ((grads, g_pk), (deltas, d_pk), (new_m, m_pk), (new_v, v_pk)):
            store[n], r1 = unpack(pk, r0, like)
        r0 = r1
    g_cw_full = grads["conv_w"]
    cwl = conv_w.shape[1]
    g_cw = lax.dynamic_slice_in_dim(g_cw_full, me * cwl, cwl, axis=1)
    full3 = ((3, cwl), lambda i: (0, 0))
    grads["conv_w"], deltas["conv_w"], new_m["conv_w"], new_v["conv_w"] = _ew(
        "adamw_conv_w", (1,), [(g_cw,) + full3, (conv_w,) + full3, (m_conv_w,) + full3, (v_conv_w,) + full3],
        [((3, cwl), F32) + full3] * 4, lambda g, w, m, v: (g,) + _adamw(w, g, m, v))
    rs_end("ffn1_down", sharded[2:3], rs_ffn1_down, [g_pk, grads["conv_w"]])
    rs_end("ffn1_gate", sharded[0:1], rs_ffn1_gate, grads["ffn1_w_down"])
    rs_end("ffn1_up", sharded[1:2], rs_ffn1_up, grads["ffn1_w_gate"])

    return (loss, dx.reshape(x.shape), *[grads[n] for n in names], *[deltas[n] for n in names],
            *[new_m[n] for n in names], *[new_v[n] for n in names])
```

```python
import math

import jax
import jax.numpy as jnp
from jax import lax
from jax.experimental import pallas as pl
from jax.experimental.pallas import tpu as pltpu

F32 = jnp.float32
BF = jnp.bfloat16
I32 = jnp.int32
MESH = pl.DeviceIdType.MESH
LANE = 128
SUBLANE = 8
NDEV = 8
EW_BLOCK = 256 * 1024
M_TILE = 1024
K_TILE = 2048
K_SHARDS = 2
DMA_CHUNKS = 4
EPS = 1e-6
ADAM_LR, ADAM_B1, ADAM_B2, ADAM_EPS, ADAM_WD, ADAM_STEP = 0.001, 0.9, 0.999, 1e-08, 0.01, 10
NN = ((1,), (0,))
NT = ((1,), (1,))
TN = ((0,), (0,))
HBM = pl.BlockSpec(memory_space=pltpu.HBM)


def _pick(n, pref, mult):
    t = min(pref, n)
    t -= t % mult
    while t >= mult:
        if n % t == 0:
            return t
        t -= mult
    return n


def _sigmoid(x):
    return 1.0 / (1.0 + jnp.exp(-x))


_GELU_C = math.sqrt(2.0 / math.pi)


def _gelu(x):
    return 0.5 * x * (1.0 + jnp.tanh(_GELU_C * (x + 0.044715 * x * x * x)))


def _gelu_grad(x):
    t = jnp.tanh(_GELU_C * (x + 0.044715 * x * x * x))
    return 0.5 * (1.0 + t) + 0.5 * x * (1.0 - t * t) * _GELU_C * (1.0 + 3.0 * 0.044715 * x * x)


def _dep_specs(deps, rank):
    return [(d, d.shape, lambda *_, nd=d.ndim: (0,) * nd) for d in deps]


def _mm(name, grid, ops, pairs, sides, outs, epilogue, acc_shapes, deps=()):
    nk = grid[-1]
    n_ops, n_sides, n_outs = len(ops), len(sides), len(outs)
    dep_specs = _dep_specs(deps, len(grid))
    n_deps = len(dep_specs)

    def body(*refs):
        op_refs = refs[:n_ops]
        side_refs = refs[n_ops:n_ops + n_sides]
        out_refs = refs[n_ops + n_sides + n_deps:n_ops + n_sides + n_deps + n_outs]
        acc_refs = refs[n_ops + n_sides + n_deps + n_outs:]

        def partials():
            res = [None] * len(acc_shapes)
            for ia, ib, dims, ai in pairs:
                a_ref, b_ref = op_refs[ia], op_refs[ib]
                for s in range(a_ref.shape[0] if len(a_ref.shape) == 3 else 1):
                    a, b = (a_ref[s], b_ref[s]) if len(a_ref.shape) == 3 else (a_ref[...], b_ref[...])
                    p = lax.dot_general(a, b, (dims, ((), ())), preferred_element_type=F32)
                    res[ai] = p if res[ai] is None else res[ai] + p
            return res

        def finish(accs):
            vals = epilogue(accs, [s[...] for s in side_refs])
            for o, v in zip(out_refs, vals):
                o[...] = v.astype(o.dtype)

        if nk == 1:
            finish(partials())
        else:
            k = pl.program_id(len(grid) - 1)

            @pl.when(k == 0)
            def _():
                for a, p in zip(acc_refs, partials()):
                    a[...] = p

            @pl.when(k > 0)
            def _():
                for a, p in zip(acc_refs, partials()):
                    a[...] += p

            @pl.when(k == nk - 1)
            def _():
                finish([a[...] for a in acc_refs])

    return pl.pallas_call(
        body, name=name, grid=grid,
        in_specs=[pl.BlockSpec(b, m) for (_, b, m) in list(ops) + list(sides) + dep_specs],
        out_specs=[pl.BlockSpec(b, m) for (_, _, b, m) in outs],
        out_shape=[jax.ShapeDtypeStruct(s, d) for (s, d, _, _) in outs],
        scratch_shapes=[pltpu.VMEM(s, F32) for s in acc_shapes] if nk > 1 else [],
        compiler_params=pltpu.CompilerParams(
            dimension_semantics=("parallel",) * (len(grid) - 1) + ("arbitrary",)),
    )(*[a for (a, _, _) in list(ops) + list(sides) + dep_specs])


def _ew(name, grid, ins, outs, fn, acc=(), deps=()):
    n_in = len(ins)
    dep_specs = _dep_specs(deps, len(grid))

    def body(*refs):
        vals = fn(*[r[...] for r in refs[:n_in]])
        first = pl.program_id(0) == 0
        for idx, (o, v) in enumerate(zip(refs[n_in + len(dep_specs):], vals)):
            if idx in acc:
                @pl.when(first)
                def _(o=o, v=v):
                    o[...] = v.astype(o.dtype)

                @pl.when(jnp.logical_not(first))
                def _(o=o, v=v):
                    o[...] += v.astype(o.dtype)
            else:
                o[...] = v.astype(o.dtype)

    return pl.pallas_call(
        body, name=name, grid=grid,
        in_specs=[pl.BlockSpec(b, m) for (_, b, m) in list(ins) + dep_specs],
        out_specs=[pl.BlockSpec(b, m) for (_, _, b, m) in outs],
        out_shape=[jax.ShapeDtypeStruct(s, d) for (s, d, _, _) in outs],
        compiler_params=pltpu.CompilerParams(
            dimension_semantics=(("arbitrary",) if acc else ("parallel",)) * len(grid)),
    )(*[a for (a, _, _) in list(ins) + dep_specs])


def _position():
    x, y, c = lax.axis_index("x"), lax.axis_index("y"), lax.axis_index("c")
    chips = [(1 - x, y), (x, 1 - y), (1 - x, 1 - y)]
    return x, y, c, chips


def _all_gather(name, shards):
    n = len(shards)

    def body(*refs):
        xs, outs = refs[:n], refs[n:2 * n]
        send_sems, recv_sems, local_sems = refs[2 * n:]
        x, y, c, chips = _position()
        me, sibling = (x, y, c), (x, y, 1 - c)

        def copy(a, k, block, to, src=None):
            dst = outs[a].at[4 * block[0] + 2 * block[1] + block[2]]
            return pltpu.make_async_remote_copy(
                src_ref=dst if src is None else src, dst_ref=dst,
                send_sem=send_sems.at[a, k], recv_sem=recv_sems.at[a, k],
                device_id=to, device_id_type=MESH)

        mine = [pltpu.make_async_copy(xs[a], outs[a].at[4 * x + 2 * y + c], local_sems.at[a]) for a in range(n)]
        for cp in mine:
            cp.start()
        first = []
        for a in range(n):
            first.append(copy(a, 0, me, sibling, src=xs[a]))
            first += [copy(a, 1 + j, me, (*chip, c), src=xs[a]) for j, chip in enumerate(chips)]
        for cp in first:
            cp.start()
        passed = []
        for a in range(n):
            for j, chip in enumerate(chips):
                copy(a, 1 + j, (*chip, c), me).wait_recv()
                cp = copy(a, 4 + j, (*chip, c), sibling)
                cp.start()
                passed.append(cp)
        for a in range(n):
            copy(a, 0, sibling, me).wait_recv()
            for j, chip in enumerate(chips):
                copy(a, 4 + j, (*chip, 1 - c), me).wait_recv()
        for cp in first + passed:
            cp.wait_send()
        for cp in mine:
            cp.wait()

    return pl.pallas_call(
        body, name=name,
        out_shape=[jax.ShapeDtypeStruct((NDEV,) + s.shape, s.dtype) for s in shards],
        in_specs=[HBM] * n, out_specs=[HBM] * n,
        scratch_shapes=[pltpu.SemaphoreType.DMA((n, 7)), pltpu.SemaphoreType.DMA((n, 7)),
                        pltpu.SemaphoreType.DMA((n,))],
    )(*shards)


SEM = pl.BlockSpec(memory_space=pltpu.SEMAPHORE)
EFFECT = pltpu.SideEffectType.DATAFLOW_SIDE_EFFECTING


def _in_hbm(v):
    return pltpu.with_memory_space_constraint(v, pltpu.HBM)


def _split_start(name, srcs, lands, make_copies, n_per, after=()):
    n, nb = len(srcs), len(srcs) + len(lands)
    n_sems = len(lands) * n_per
    after = list(after)

    def body(*refs):
        send_sems, recv_sems = refs[nb + len(after)], refs[nb + len(after) + 1]
        for cp in make_copies(refs[:n], refs[n:nb], send_sems, recv_sems):
            cp.start()
        refs[-1][...] = jnp.zeros_like(refs[-1])

    outs = pl.pallas_call(
        body, name=name,
        out_shape=(pltpu.SemaphoreType.DMA((n_sems,)), pltpu.SemaphoreType.DMA((n_sems,)),
                   *[pltpu.HBM(v.shape, v.dtype) for v in list(srcs) + list(lands)],
                   jax.ShapeDtypeStruct((SUBLANE, LANE), F32)),
        in_specs=[HBM] * nb + [pl.BlockSpec(memory_space=pl.ANY)] * len(after),
        out_specs=(SEM, SEM, *[HBM] * nb, pl.BlockSpec(memory_space=pltpu.VMEM)),
        input_output_aliases={i: 2 + i for i in range(nb)},
        compiler_params=pltpu.CompilerParams(has_side_effects=EFFECT),
    )(*[_in_hbm(v) for v in list(srcs) + list(lands)], *after)
    return outs[0], outs[1], list(outs[2:2 + n]), list(outs[2 + n:2 + nb]), outs[-1]


def _split_wait(name, started, make_copies, after):
    send_sems, recv_sems, srcs, lands, _ = started
    n, nb = len(srcs), len(srcs) + len(lands)

    def body(*refs):
        for cp in make_copies(refs[:n], refs[n:nb], refs[nb], refs[nb + 1]):
            cp.wait_send()
            cp.wait_recv()

    order = [] if after is None else list(after) if isinstance(after, (list, tuple)) else [after]
    outs = pl.pallas_call(
        body, name=name,
        out_shape=tuple(pltpu.HBM(v.shape, v.dtype) for v in srcs + lands),
        in_specs=[HBM] * nb + [SEM, SEM] + [pl.BlockSpec(memory_space=pl.ANY)] * len(order),
        out_specs=tuple([HBM] * nb),
        input_output_aliases={i: i for i in range(nb)},
        compiler_params=pltpu.CompilerParams(has_side_effects=EFFECT),
    )(*srcs, *lands, send_sems, recv_sems, *order)
    return list(outs[:n]), list(outs[n:])


def _gather_first_copies(xs, lands, send_sems, recv_sems):
    x, y, c, _ = _position()
    copies = []
    for a in range(len(xs)):
        for k, peer in enumerate([(x, y, 1 - c), (1 - x, y, c), (x, 1 - y, c)]):
            copies.append(pltpu.make_async_remote_copy(
                src_ref=xs[a], dst_ref=lands[a].at[4 * x + 2 * y + c],
                send_sem=send_sems.at[3 * a + k], recv_sem=recv_sems.at[3 * a + k], device_id=peer, device_id_type=MESH))
    return copies


def _gather_second_copies(xs, lands, send_sems, recv_sems):
    x, y, c, _ = _position()
    copies = []
    for a in range(len(lands)):
        rows = lands[a].shape[1]
        unit = SUBLANE * (4 // jnp.dtype(lands[a].dtype).itemsize)
        half = rows // 2 // unit * unit or rows
        parts = [((1 - x, y), (x, 1 - y), pl.ds(0, half))]
        if half < rows:
            parts.append(((x, 1 - y), (1 - x, y), pl.ds(half, rows - half)))
        for k, (block, to, rs) in enumerate(parts):
            ref = lands[a].at[4 * block[0] + 2 * block[1] + c, rs]
            copies.append(pltpu.make_async_remote_copy(
                src_ref=ref, dst_ref=ref, send_sem=send_sems.at[2 * a + k], recv_sem=recv_sems.at[2 * a + k],
                device_id=(*to, c), device_id_type=MESH))
    return copies


def _chips_copies(ps, lands, send_sems, recv_sems):
    x, y, c, chips = _position()
    copies = []
    for a in range(len(ps)):
        for j, chip in enumerate(chips):
            copies.append(pltpu.make_async_remote_copy(
                src_ref=ps[a].at[2 * chip[0] + chip[1]], dst_ref=lands[a].at[j],
                send_sem=send_sems.at[3 * a + j], recv_sem=recv_sems.at[3 * a + j], device_id=(*chip, c),
                device_id_type=MESH))
    return copies


def _sibling_copies(gs, lands, send_sems, recv_sems):
    x, y, c, _ = _position()
    copies = []
    for a in range(len(gs)):
        for q in range(4):
            copies.append(pltpu.make_async_remote_copy(
                src_ref=gs[a].at[2 * q + 1 - c], dst_ref=lands[a].at[q],
                send_sem=send_sems.at[4 * a + q], recv_sem=recv_sems.at[4 * a + q],
                device_id=(x, y, 1 - c), device_id_type=MESH))
    return copies


def _everyone_copies(xs, lands, send_sems, recv_sems):
    x, y, c, _ = _position()
    flip = lambda v, bit: 1 - v if bit else v
    copies = []
    for a in range(len(xs)):
        for k in range(1, NDEV):
            copies.append(pltpu.make_async_remote_copy(
                src_ref=xs[a], dst_ref=lands[a].at[4 * x + 2 * y + c],
                send_sem=send_sems.at[7 * a + k - 1], recv_sem=recv_sems.at[7 * a + k - 1],
                device_id=(flip(x, k & 4), flip(y, k & 2), flip(c, k & 1)), device_id_type=MESH))
    return copies


def _row_chunks(rows, dtype):
    unit = SUBLANE * (4 // jnp.dtype(dtype).itemsize)
    units = rows // unit
    if rows % unit or units < 2:
        return [(0, rows)]
    k = min(DMA_CHUNKS, units)
    sizes = [(units // k + (1 if i < units % k else 0)) * unit for i in range(k)]
    return [(sum(sizes[:i]), sz) for i, sz in enumerate(sizes)]


def _gather_forward(name, lands, after=()):
    n = len(lands)
    after = list(after)

    def body(*refs):
        ins, outs = refs[:n], refs[n + len(after):2 * n + len(after)]
        send_sems, recv_sems = refs[2 * n + len(after):]
        x, y, c, chips = _position()
        whole, chunks = [], []
        for a in range(n):
            rows = _row_chunks(ins[a].shape[1], ins[a].dtype)
            for j, chip in enumerate(chips):
                slot = 4 * chip[0] + 2 * chip[1]

                def to_sibling(src, dst):
                    return pltpu.make_async_remote_copy(
                        src_ref=src, dst_ref=dst, send_sem=send_sems.at[a, j], recv_sem=recv_sems.at[a, j],
                        device_id=(x, y, 1 - c), device_id_type=MESH)

                whole.append(to_sibling(ins[a].at[slot + c], outs[a].at[slot + 1 - c]))
                chunks += [to_sibling(ins[a].at[slot + c, pl.ds(r0, nr)], outs[a].at[slot + c, pl.ds(r0, nr)])
                           for r0, nr in rows]
        for cp in chunks:
            cp.start()
        for cp in whole:
            cp.wait()

    return pl.pallas_call(
        body, name=name,
        out_shape=[jax.ShapeDtypeStruct(l.shape, l.dtype) for l in lands],
        in_specs=[HBM] * n + [pl.BlockSpec(memory_space=pl.ANY)] * len(after), out_specs=[HBM] * n,
        input_output_aliases={a: a for a in range(n)},
        scratch_shapes=[pltpu.SemaphoreType.DMA((n, 3)), pltpu.SemaphoreType.DMA((n, 3))],
    )(*lands, *after)


def _sum_sibling(name, g, land, c_arr):
    _, R, C = g.shape
    tr = _pick(R, 512, SUBLANE)

    def body(c_ref, g_ref, l_ref, o_ref):
        o_ref[...] = (g_ref[...].astype(F32) + l_ref[...].astype(F32)).astype(o_ref.dtype)

    return pl.pallas_call(
        body, name=name,
        grid_spec=pltpu.PrefetchScalarGridSpec(
            num_scalar_prefetch=1, grid=(4, R // tr),
            in_specs=[pl.BlockSpec((None, tr, C), lambda q, i, cr: (2 * q + cr[0], i, 0)),
                      pl.BlockSpec((None, tr, C), lambda q, i, cr: (q, i, 0))],
            out_specs=pl.BlockSpec((None, tr, C), lambda q, i, cr: (q, i, 0))),
        out_shape=jax.ShapeDtypeStruct((4, R, C), g.dtype),
        compiler_params=pltpu.CompilerParams(dimension_semantics=("parallel", "parallel")),
    )(c_arr, g, land)


def _adamw(w, g, m, v):
    m = ADAM_B1 * m + (1.0 - ADAM_B1) * g
    v = ADAM_B2 * v + (1.0 - ADAM_B2) * (g * g)
    m_hat = m / (1.0 - ADAM_B1 ** ADAM_STEP)
    v_hat = v / (1.0 - ADAM_B2 ** ADAM_STEP)
    delta = -ADAM_LR * (m_hat / (jnp.sqrt(v_hat) + ADAM_EPS) + ADAM_WD * w)
    return delta, m, v


def _finish_sharded(name, sums, land, q_arr, w, m, v):
    R, C = w.shape
    tr = _pick(R, 512, SUBLANE)
    tc = _pick(C, max(LANE, EW_BLOCK // tr), LANE)

    def body(q_ref, p_ref, l_ref, w_ref, m_ref, v_ref, g_out, d_out, m_out, v_out):
        g = p_ref[...].astype(F32)
        for j in range(3):
            g = g + l_ref[j].astype(F32)
        d, mn, vn = _adamw(w_ref[...], g, m_ref[...], v_ref[...])
        g_out[...] = g
        d_out[...] = d
        m_out[...] = mn
        v_out[...] = vn

    blk = pl.BlockSpec((tr, tc), lambda i, j, qr: (i, j))
    return pl.pallas_call(
        body, name=name,
        grid_spec=pltpu.PrefetchScalarGridSpec(
            num_scalar_prefetch=1, grid=(R // tr, C // tc),
            in_specs=[pl.BlockSpec((None, tr, tc), lambda i, j, qr: (qr[0], i, j)),
                      pl.BlockSpec((3, tr, tc), lambda i, j, qr: (0, i, j)), blk, blk, blk],
            out_specs=[blk] * 4),
        out_shape=[jax.ShapeDtypeStruct((R, C), F32)] * 4,
        compiler_params=pltpu.CompilerParams(dimension_semantics=("parallel", "parallel")),
    )(q_arr, sums, land, w, m, v)


def _finish_replicated(name, gathered, w, m, v):
    _, R, C = gathered.shape
    tr = _pick(R, 256, SUBLANE)

    def fn(gv, wv, mv, vv):
        g = gv[0]
        for d in range(1, NDEV):
            g = g + gv[d]
        dl, mn, vn = _adamw(wv, g, mv, vv)
        return g, dl, mn, vn

    row = ((tr, C), lambda i: (i, 0))
    return _ew(name, (R // tr,),
               [(gathered, (NDEV, tr, C), lambda i: (0, i, 0)), (w,) + row, (m,) + row, (v,) + row],
               [((R, C), F32) + row] * 4, fn)


def _rms_fwd(name, h, g, deps=()):
    S, D = h.shape
    tr = _pick(S, 256, SUBLANE)

    def fn(hv, gv):
        r = lax.rsqrt(jnp.mean(hv * hv, axis=-1, keepdims=True) + EPS)
        return hv * r * gv, r

    return _ew(name, (S // tr,),
               [(h, (tr, D), lambda i: (i, 0)), (g, (1, D), lambda i: (0, 0))],
               [((S, D), BF, (tr, D), lambda i: (i, 0)), ((S, 1), F32, (tr, 1), lambda i: (i, 0))], fn, deps=deps)


def _rms_bwd(name, du, h, r, g, dres, scale):
    S, D = h.shape
    tr = _pick(S, 256, SUBLANE)

    def fn(duv, hv, rv, gv, drv):
        xn = hv * rv
        dxn = duv * gv
        dh = drv + rv * (dxn - xn * jnp.mean(dxn * xn, axis=-1, keepdims=True))
        return dh, scale * dh, jnp.sum(duv * xn, axis=0, keepdims=True)

    row = ((tr, D), lambda i: (i, 0))
    return _ew(name, (S // tr,),
               [(du,) + row, (h,) + row, (r, (tr, 1), lambda i: (i, 0)), (g, (1, D), lambda i: (0, 0)), (dres,) + row],
               [((S, D), F32) + row, ((S, D), BF) + row, ((1, D), F32, (1, D), lambda i: (0, 0))], fn, acc=(2,))


def _loss_head(name, h, g, target):
    S, D = h.shape
    tr = _pick(S, 256, SUBLANE)

    def fn(hv, gv, tv):
        r = lax.rsqrt(jnp.mean(hv * hv, axis=-1, keepdims=True) + EPS)
        xn = hv * r
        diff = xn * gv - tv
        loss = 0.5 * jnp.sum(jnp.mean(diff * diff, axis=-1, keepdims=True))
        dout = diff / D
        dxn = dout * gv
        dh = r * (dxn - xn * jnp.mean(dxn * xn, axis=-1, keepdims=True))
        return (jnp.zeros((1, LANE), F32) + loss, dh, 0.5 * dh, jnp.sum(dout * xn, axis=0, keepdims=True))

    row = ((tr, D), lambda i: (i, 0))
    return _ew(name, (S // tr,),
               [(h,) + row, (g, (1, D), lambda i: (0, 0)), (target,) + row],
               [((1, LANE), F32, (1, LANE), lambda i: (0, 0)), ((S, D), F32) + row, ((S, D), BF) + row,
                ((1, D), F32, (1, D), lambda i: (0, 0))], fn, acc=(0, 3))


def _ffn_fwd(tag, u, h, wg, fetch_wu, fetch_wd):
    S, D = u.shape
    Fs = wg.shape[1]
    tm, tk = _pick(S, M_TILE, SUBLANE), _pick(D, K_TILE, LANE)
    act = ((NDEV, S, Fs), BF, (None, tm, Fs), lambda b, i, j, k: (b, i, 0))
    lhs = (u, (tm, tk), lambda b, i, j, k: (i, k))
    rhs = lambda w: (w, (None, Fs, tk), lambda b, i, j, k: (b, 0, k))
    (gt,) = _mm(tag + "_gate", (NDEV, S // tm, 1, D // tk), [lhs, rhs(wg)], [(0, 1, NT, 0)], [], [act],
                lambda accs, sides: accs, [(tm, Fs)])
    wu = fetch_wu(gt)

    def up_epilogue(accs, sides):
        g = sides[0].astype(F32)
        return accs[0], g * _sigmoid(g) * accs[0]

    up, a = _mm(tag + "_up", (NDEV, S // tm, 1, D // tk), [lhs, rhs(wu)], [(0, 1, NT, 0)],
                [(gt, (None, tm, Fs), lambda b, i, j, k: (b, i, 0))], [act, act], up_epilogue, [(tm, Fs)])
    wd = fetch_wd(a)
    tn = _pick(D, 1024, LANE)
    (hn,) = _mm(
        tag + "_down", (1, S // tm, D // tn, NDEV // K_SHARDS),
        [(a, (K_SHARDS, tm, Fs), lambda b, i, j, k: (k, i, 0)), (wd, (K_SHARDS, Fs, tn), lambda b, i, j, k: (k, 0, j))],
        [(0, 1, NN, 0)], [(h, (tm, tn), lambda b, i, j, k: (i, j))],
        [((S, D), F32, (tm, tn), lambda b, i, j, k: (i, j))],
        lambda accs, sides: [sides[0] + 0.5 * accs[0]], [(tm, tn)])
    return hn, (gt, up, a), wu, wd


def _ffn_bwd(tag, dhs, u, saved, wg, wu, wd, deps=()):
    gt, up, a = saved
    S, D = u.shape
    Fs = wg.shape[1]
    tm, tk = _pick(S, M_TILE, SUBLANE), _pick(D, K_TILE, LANE)
    act_in = lambda arr: (arr, (None, tm, Fs), lambda b, i, j, k: (b, i, 0))
    act_out = ((NDEV, S, Fs), BF, (None, tm, Fs), lambda b, i, j, k: (b, i, 0))

    def act_epilogue(accs, sides):
        da = accs[0]
        gtv, upv = sides[0].astype(F32), sides[1].astype(F32)
        sg = _sigmoid(gtv)
        return da * upv * sg * (1.0 + gtv * (1.0 - sg)), da * gtv * sg

    dgt, dup = _mm(
        tag + "_dact", (NDEV, S // tm, 1, D // tk),
        [(dhs, (tm, tk), lambda b, i, j, k: (i, k)), (wd, (None, Fs, tk), lambda b, i, j, k: (b, 0, k))],
        [(0, 1, NT, 0)], [act_in(gt), act_in(up)], [act_out, act_out], act_epilogue, [(tm, Fs)], deps=deps)

    ts = _pick(S, K_TILE, SUBLANE)
    tn = _pick(D, 1024, LANE)
    wgrad = ((NDEV, Fs, D), BF, (None, Fs, tn), lambda b, i, j, k: (b, 0, j))
    tok = lambda arr: (arr, (None, ts, Fs), lambda b, i, j, k: (b, k, 0))

    def grad_down(deps=()):
        return _mm(
            tag + "_dwd", (NDEV, 1, D // tn, S // ts),
            [tok(a), (dhs, (ts, tn), lambda b, i, j, k: (k, j))],
            [(0, 1, TN, 0)], [], [wgrad], lambda accs, sides: accs, [(Fs, tn)], deps=deps)[0]

    def grad_gate_up(deps=(), only=None):
        acts = [dgt, dup] if only is None else [(dgt, dup)[only]]
        n = len(acts)
        return _mm(
            tag + ("_dwgu" if only is None else ("_dwg", "_dwu")[only]), (NDEV, 1, D // tn, S // ts),
            [tok(t) for t in acts] + [(u, (ts, tn), lambda b, i, j, k: (k, j))],
            [(i, n, TN, i) for i in range(n)], [], [wgrad] * n, lambda accs, sides: accs, [(Fs, tn)] * n, deps=deps)

    def du(deps=()):
        return _mm(
            tag + "_du", (1, S // tm, D // tn, NDEV // K_SHARDS),
            [(dgt, (K_SHARDS, tm, Fs), lambda b, i, j, k: (k, i, 0)), (dup, (K_SHARDS, tm, Fs), lambda b, i, j, k: (k, i, 0)),
             (wg, (K_SHARDS, Fs, tn), lambda b, i, j, k: (k, 0, j)), (wu, (K_SHARDS, Fs, tn), lambda b, i, j, k: (k, 0, j))],
            [(0, 2, NN, 0), (1, 3, NN, 0)], [], [((S, D), F32, (tm, tn), lambda b, i, j, k: (i, j))],
            lambda accs, sides: accs, [(tm, tn)], deps=deps)[0]

    return grad_down, grad_gate_up, du


def _ssm_params(lam_re, lam_im, log_dt, b_re, b_im, c_re, c_im):
    G, N = lam_re.shape
    C = b_re.shape[2]
    lam_re = jnp.minimum(lam_re, -1e-4)
    dt = jnp.exp(log_dt)[:, None]
    mag = jnp.exp(lam_re * dt)
    a_re = mag * jnp.cos(lam_im * dt)
    a_im = mag * jnp.sin(lam_im * dt)
    den = lam_re * lam_re + lam_im * lam_im
    p = a_re - 1.0
    f_re = ((p * lam_re + a_im * lam_im) / den)[:, :, None]
    f_im = ((a_im * lam_re - p * lam_im) / den)[:, :, None]
    bb_re = f_re * b_re - f_im * b_im
    bb_im = f_re * b_im + f_im * b_re
    gpt = LANE // C
    tiles = G // gpt
    eye = jnp.eye(gpt, dtype=F32)

    def bd(bb):
        return jnp.einsum("bgnc,gh->bgchn", bb.reshape(tiles, gpt, N, C), eye).reshape(tiles, gpt * C, gpt * N)

    def cd(cc):
        return jnp.einsum("bgcn,gh->bgnhc", cc.reshape(tiles, gpt, C, N), eye).reshape(tiles, gpt * N, gpt * C)

    rows = G * N // LANE
    return (a_re.reshape(rows, LANE), a_im.reshape(rows, LANE), bd(bb_re), bd(bb_im), cd(c_re), cd(-c_im))


def _tile_states(ref3, b, per):
    return jnp.concatenate([ref3[:, per * b + r, :] for r in range(per)], axis=1).astype(BF)


def _ssm_spread(name, x, m_re, m_im, dims, rows):
    S, W = x.shape
    tiles = m_re.shape[0]
    tch, per = W // tiles, rows // tiles
    tq = _pick(S, 256, SUBLANE)

    def body(x_ref, mre_ref, mim_ref, ore_ref, oim_ref):
        for b in range(tiles):
            xb = x_ref[:, b * tch:(b + 1) * tch]
            for m_ref, o_ref in ((mre_ref, ore_ref), (mim_ref, oim_ref)):
                val = lax.dot_general(xb, m_ref[b], (dims, ((), ())), preferred_element_type=F32)
                for r in range(per):
                    o_ref[:, per * b + r, :] = val[:, r * LANE:(r + 1) * LANE]

    whole = lambda m: pl.BlockSpec(m.shape, lambda i: (0, 0, 0))
    st = pl.BlockSpec((tq, rows, LANE), lambda i: (i, 0, 0))
    return pl.pallas_call(
        body, name=name, grid=(S // tq,),
        in_specs=[pl.BlockSpec((tq, W), lambda i: (i, 0)), whole(m_re), whole(m_im)], out_specs=[st, st],
        out_shape=[jax.ShapeDtypeStruct((S, rows, LANE), F32)] * 2,
        compiler_params=pltpu.CompilerParams(dimension_semantics=("parallel",)),
    )(x, m_re, m_im)


def _ssm_collect(name, z_re3, z_im3, m_re, m_im, dims, side, gain, epilogue, out_dtypes):
    S, rows, _ = z_re3.shape
    tiles = m_re.shape[0]
    W = side.shape[1]
    tch, per = W // tiles, rows // tiles
    tq = _pick(S, 256, SUBLANE)
    n_out = len(out_dtypes)

    def body(zre_ref, zim_ref, mre_ref, mim_ref, side_ref, gain_ref, *out_refs):
        for b in range(tiles):
            cols = slice(b * tch, (b + 1) * tch)
            zre, zim = _tile_states(zre_ref, b, per), _tile_states(zim_ref, b, per)
            acc = lax.dot_general(zre, mre_ref[b], (dims, ((), ())), preferred_element_type=F32)
            acc = acc + lax.dot_general(zim, mim_ref[b], (dims, ((), ())), preferred_element_type=F32)
            for o, v in zip(out_refs[:n_out], epilogue(acc, side_ref[:, cols], gain_ref[:, cols])):
                o[:, cols] = v.astype(o.dtype)
            out_refs[n_out][:, b * per * LANE:(b + 1) * per * LANE] = zre
            out_refs[n_out + 1][:, b * per * LANE:(b + 1) * per * LANE] = zim

    whole = lambda m: pl.BlockSpec(m.shape, lambda i: (0, 0, 0))
    st = pl.BlockSpec((tq, rows, LANE), lambda i: (i, 0, 0))
    ch = pl.BlockSpec((tq, W), lambda i: (i, 0))
    flat = pl.BlockSpec((tq, rows * LANE), lambda i: (i, 0))
    return pl.pallas_call(
        body, name=name, grid=(S // tq,),
        in_specs=[st, st, whole(m_re), whole(m_im), ch, pl.BlockSpec((1, W), lambda i: (0, 0))],
        out_specs=[ch] * n_out + [flat, flat],
        out_shape=[jax.ShapeDtypeStruct((S, W), dt) for dt in out_dtypes]
        + [jax.ShapeDtypeStruct((S, rows * LANE), BF)] * 2,
        compiler_params=pltpu.CompilerParams(dimension_semantics=("parallel",)),
    )(z_re3, z_im3, m_re, m_im, side, gain)


def _scan_fwd(bu_re, bu_im, a_re, a_im):
    S, R, _ = bu_re.shape
    tc = _pick(S, 256, SUBLANE)

    def body(bre, bim, are, aim, sre, sim, carry):
        @pl.when(pl.program_id(0) == 0)
        def _():
            carry[...] = jnp.zeros_like(carry)

        ar, ai = are[...], aim[...]

        def step(t, c):
            pr, pi = c
            nr = ar * pr - ai * pi + bre[t]
            ni = ar * pi + ai * pr + bim[t]
            sre[t] = nr
            sim[t] = ni
            return nr, ni

        pr, pi = lax.fori_loop(0, tc, step, (carry[0], carry[1]), unroll=8)
        carry[0] = pr
        carry[1] = pi

    blk = pl.BlockSpec((tc, R, LANE), lambda i: (i, 0, 0))
    par = pl.BlockSpec((R, LANE), lambda i: (0, 0))
    return pl.pallas_call(
        body, name="ssm_scan_fwd", grid=(S // tc,),
        in_specs=[blk, blk, par, par], out_specs=[blk, blk],
        out_shape=[jax.ShapeDtypeStruct((S, R, LANE), F32)] * 2,
        scratch_shapes=[pltpu.VMEM((2, R, LANE), F32)],
        compiler_params=pltpu.CompilerParams(dimension_semantics=("arbitrary",)),
    )(bu_re, bu_im, a_re, a_im)


def _scan_bwd(ds_re, ds_im, s_re, s_im, a_re, a_im):
    S, R, _ = ds_re.shape
    tc = _pick(S, 256, SUBLANE)
    nc = S // tc

    def body(dre, dim_, sre, sim, are, aim, lre, lim, dar, dai, carry):
        @pl.when(pl.program_id(0) == 0)
        def _():
            carry[...] = jnp.zeros_like(carry)
            dar[...] = jnp.zeros_like(dar)
            dai[...] = jnp.zeros_like(dai)

        ar, ai = are[...], aim[...]

        def step(tt, c):
            t = tc - 1 - tt
            lr, li, gr, gi = c
            sr, si = sre[t], sim[t]
            gr = gr + lr * sr + li * si
            gi = gi + li * sr - lr * si
            nlr = dre[t] + ar * lr + ai * li
            nli = dim_[t] + ar * li - ai * lr
            lre[t] = nlr
            lim[t] = nli
            return nlr, nli, gr, gi

        lr, li, gr, gi = lax.fori_loop(0, tc, step, (carry[0], carry[1], dar[...], dai[...]), unroll=8)
        carry[0] = lr
        carry[1] = li
        dar[...] = gr
        dai[...] = gi

    blk = pl.BlockSpec((tc, R, LANE), lambda i: (nc - 1 - i, 0, 0))
    par = pl.BlockSpec((R, LANE), lambda i: (0, 0))
    return pl.pallas_call(
        body, name="ssm_scan_bwd", grid=(nc,),
        in_specs=[blk, blk, blk, blk, par, par], out_specs=[blk, blk, par, par],
        out_shape=[jax.ShapeDtypeStruct((S, R, LANE), F32)] * 2 + [jax.ShapeDtypeStruct((R, LANE), F32)] * 2,
        scratch_shapes=[pltpu.VMEM((2, R, LANE), F32)],
        compiler_params=pltpu.CompilerParams(dimension_semantics=("arbitrary",)),
    )(ds_re, ds_im, s_re, s_im, a_re, a_im)


def _shift_down(z, k):
    t = lax.broadcasted_iota(I32, z.shape, 0)
    return jnp.where(t >= k, pltpu.roll(z, k, 0), 0.0)


def _shift_up(z, k):
    n = z.shape[0]
    t = lax.broadcasted_iota(I32, z.shape, 0)
    return jnp.where(t < n - k, pltpu.roll(z, n - k, 0), 0.0)


def _conv_fwd(proj, cw, cb):
    _, S, W = proj.shape
    ct = _pick(W, 256, LANE)

    def fn(bg, cg, val, w, b):
        z = cg * val
        conv = b + w[0:1] * _shift_down(z, 2) + w[1:2] * _shift_down(z, 1) + w[2:3] * z
        return bg * conv, conv

    sl = lambda s: (proj, (None, S, ct), lambda j, s=s: (s, 0, j))
    col = ((S, ct), lambda j: (0, j))
    return _ew("conv_fwd", (W // ct,),
               [sl(1), sl(2), sl(3), (cw, (3, ct), lambda j: (0, j)), (cb, (1, ct), lambda j: (0, j))],
               [((S, W), BF) + col, ((S, W), F32) + col], fn)


def _conv_bwd(dyb, proj, conv, cw):
    _, S, W = proj.shape
    ct = _pick(W, 256, LANE)

    def fn(dy, bg, cg, val, cv, w):
        z = cg * val
        z1, z2 = _shift_down(z, 1), _shift_down(z, 2)
        dconv = dy * bg
        dz = w[2:3] * dconv + w[1:2] * _shift_up(dconv, 1) + w[0:1] * _shift_up(dconv, 2)
        dw = jnp.concatenate([jnp.sum(dconv * z2, axis=0, keepdims=True), jnp.sum(dconv * z1, axis=0, keepdims=True),
                              jnp.sum(dconv * z, axis=0, keepdims=True)], axis=0)
        return dy * cv, dz * val, dz * cg, dw, jnp.sum(dconv, axis=0, keepdims=True)

    sl = lambda s: (proj, (None, S, ct), lambda j, s=s: (s, 0, j))
    col = ((S, ct), lambda j: (0, j))
    return _ew("conv_bwd", (W // ct,),
               [(dyb,) + col, sl(1), sl(2), sl(3), (conv,) + col, (cw, (3, ct), lambda j: (0, j))],
               [((S, W), BF) + col, ((S, W), BF) + col, ((S, W), BF) + col,
                ((3, W), F32, (3, ct), lambda j: (0, j)), ((1, W), F32, (1, ct), lambda j: (0, j))], fn)


def _plain(accs, sides):
    return accs


def kernel(x, ffn1_norm, ffn1_w_gate, ffn1_w_up, ffn1_w_down, mix_norm, w_in, ssm_lambda_re, ssm_lambda_im, ssm_log_dt, ssm_b_re, ssm_b_im, ssm_c_re, ssm_c_im, ssm_d, ssm_w_glu, ssm_b_glu, ssm_w_out, conv_w, conv_b, conv_w_out, w_o, ffn2_norm, ffn2_w_gate, ffn2_w_up, ffn2_w_down, final_norm, loss_target, m_ffn1_norm, m_ffn1_w_gate, m_ffn1_w_up, m_ffn1_w_down, m_mix_norm, m_w_in, m_ssm_lambda_re, m_ssm_lambda_im, m_ssm_log_dt, m_ssm_b_re, m_ssm_b_im, m_ssm_c_re, m_ssm_c_im, m_ssm_d, m_ssm_w_glu, m_ssm_b_glu, m_ssm_w_out, m_conv_w, m_conv_b, m_conv_w_out, m_w_o, m_ffn2_norm, m_ffn2_w_gate, m_ffn2_w_up, m_ffn2_w_down, m_final_norm, v_ffn1_norm, v_ffn1_w_gate, v_ffn1_w_up, v_ffn1_w_down, v_mix_norm, v_w_in, v_ssm_lambda_re, v_ssm_lambda_im, v_ssm_log_dt, v_ssm_b_re, v_ssm_b_im, v_ssm_c_re, v_ssm_c_im, v_ssm_d, v_ssm_w_glu, v_ssm_b_glu, v_ssm_w_out, v_conv_w, v_conv_b, v_conv_w_out, v_w_o, v_ffn2_norm, v_ffn2_w_gate, v_ffn2_w_up, v_ffn2_w_down, v_final_norm):
    P = dict(ffn1_norm=ffn1_norm, ffn1_w_gate=ffn1_w_gate, ffn1_w_up=ffn1_w_up, ffn1_w_down=ffn1_w_down, mix_norm=mix_norm, w_in=w_in, ssm_lambda_re=ssm_lambda_re, ssm_lambda_im=ssm_lambda_im, ssm_log_dt=ssm_log_dt, ssm_b_re=ssm_b_re, ssm_b_im=ssm_b_im, ssm_c_re=ssm_c_re, ssm_c_im=ssm_c_im, ssm_d=ssm_d, ssm_w_glu=ssm_w_glu, ssm_b_glu=ssm_b_glu, ssm_w_out=ssm_w_out, conv_w=conv_w, conv_b=conv_b, conv_w_out=conv_w_out, w_o=w_o, ffn2_norm=ffn2_norm, ffn2_w_gate=ffn2_w_gate, ffn2_w_up=ffn2_w_up, ffn2_w_down=ffn2_w_down, final_norm=final_norm)
    M = dict(ffn1_norm=m_ffn1_norm, ffn1_w_gate=m_ffn1_w_gate, ffn1_w_up=m_ffn1_w_up, ffn1_w_down=m_ffn1_w_down, mix_norm=m_mix_norm, w_in=m_w_in, ssm_lambda_re=m_ssm_lambda_re, ssm_lambda_im=m_ssm_lambda_im, ssm_log_dt=m_ssm_log_dt, ssm_b_re=m_ssm_b_re, ssm_b_im=m_ssm_b_im, ssm_c_re=m_ssm_c_re, ssm_c_im=m_ssm_c_im, ssm_d=m_ssm_d, ssm_w_glu=m_ssm_w_glu, ssm_b_glu=m_ssm_b_glu, ssm_w_out=m_ssm_w_out, conv_w=m_conv_w, conv_b=m_conv_b, conv_w_out=m_conv_w_out, w_o=m_w_o, ffn2_norm=m_ffn2_norm, ffn2_w_gate=m_ffn2_w_gate, ffn2_w_up=m_ffn2_w_up, ffn2_w_down=m_ffn2_w_down, final_norm=m_final_norm)
    V = dict(ffn1_norm=v_ffn1_norm, ffn1_w_gate=v_ffn1_w_gate, ffn1_w_up=v_ffn1_w_up, ffn1_w_down=v_ffn1_w_down, mix_norm=v_mix_norm, w_in=v_w_in, ssm_lambda_re=v_ssm_lambda_re, ssm_lambda_im=v_ssm_lambda_im, ssm_log_dt=v_ssm_log_dt, ssm_b_re=v_ssm_b_re, ssm_b_im=v_ssm_b_im, ssm_c_re=v_ssm_c_re, ssm_c_im=v_ssm_c_im, ssm_d=v_ssm_d, ssm_w_glu=v_ssm_w_glu, ssm_b_glu=v_ssm_b_glu, ssm_w_out=v_ssm_w_out, conv_w=v_conv_w, conv_b=v_conv_b, conv_w_out=v_conv_w_out, w_o=v_w_o, ffn2_norm=v_ffn2_norm, ffn2_w_gate=v_ffn2_w_gate, ffn2_w_up=v_ffn2_w_up, ffn2_w_down=v_ffn2_w_down, final_norm=v_final_norm)
    names = list(P)
    sharded = ["ffn1_w_gate", "ffn1_w_up", "ffn1_w_down", "w_in", "ssm_w_glu", "ssm_w_out", "conv_w_out", "w_o",
               "ffn2_w_gate", "ffn2_w_up", "ffn2_w_down"]
    replicated = [n for n in names if n not in sharded and n != "conv_w"]

    S, D = x.shape[1], x.shape[2]
    W = ssm_d.shape[0]
    Dc = D // NDEV
    G, N = ssm_lambda_re.shape
    rows = G * N // LANE
    xh = x.reshape(S, D)
    target = loss_target.reshape(S, D)
    xi, yi, ci = lax.axis_index("x"), lax.axis_index("y"), lax.axis_index("c")
    c_arr = jnp.reshape(ci, (1,)).astype(I32)
    q_arr = jnp.reshape(2 * xi + yi, (1,)).astype(I32)
    row = lambda v: v.reshape(1, -1)

    transposed = ("ffn1_w_gate", "ffn1_w_up", "ffn2_w_gate", "ffn2_w_up")
    local = lambda table, n: table[n].T if n in transposed else table[n]

    groups = [sharded[0:1], sharded[1:2], sharded[2:3], ["w_in", "conv_w"], sharded[4:8],
              sharded[8:9], sharded[9:10], sharded[10:11]]
    first_begun, second_begun, chain = {}, {}, []
    me = 4 * xi + 2 * yi + ci

    def gather_first(gi):
        hold = 0.0 * chain[0][0, 0] if chain else 0.0
        srcs = [conv_w + hold if n == "conv_w" else (local(P, n) + hold).astype(BF) for n in groups[gi]]
        lands = [lax.dynamic_update_slice(lax.empty((NDEV,) + s.shape, s.dtype), s[None], (me,) + (0,) * s.ndim)
                 for s in srcs]
        first_begun[gi] = _split_start("gather_first_start_%d" % gi, srcs, lands, _gather_first_copies, 3, chain[-1:])
        chain.append(first_begun[gi][4])

    def gather_second(gi, after):
        _, lands = _split_wait("gather_first_wait_%d" % gi, first_begun[gi], _gather_first_copies, after)
        second_begun[gi] = _split_start("gather_second_start_%d" % gi, [], lands, _gather_second_copies, 2, chain[-1:])
        chain.append(second_begun[gi][4])

    def gathered(gi, after):
        _, lands = _split_wait("gather_second_wait_%d" % gi, second_begun[gi], _gather_second_copies, after)
        return _gather_forward("gather_forward_%d" % gi, lands, chain[-1:])

    def fetch(gi, seconds, firsts):
        def get(after):
            for g in seconds:
                gather_second(g, after)
            for g in firsts:
                gather_first(g)
            return gathered(gi, after)
        return get

    tm = _pick(S, M_TILE, SUBLANE)
    th = _pick(S, M_TILE // 2, SUBLANE)
    tk = _pick(D, K_TILE, LANE)
    ts = _pick(S, K_TILE, SUBLANE)
    tn = _pick(D, 1024, LANE)

    gather_first(0)
    gather_first(1)
    u1, r1 = _rms_fwd("rms1", xh, row(ffn1_norm), deps=list(chain))

    hold = 0.0 * chain[0][0, 0]
    replicated = [n for n in replicated if n != "ffn1_norm"] + ["ffn1_norm"]
    tile = SUBLANE * LANE

    def as_rows(p):
        flat = p.reshape(-1).astype(F32)
        return jnp.pad(flat, (0, -flat.shape[0] % tile)).reshape(-1, LANE)

    def pack(parts):
        return jnp.concatenate([as_rows(p) for p in parts], axis=0)

    cw_zero = jnp.zeros((conv_w.shape[0], W), F32)
    packed_state = [pack([t[n] + hold for n in replicated[:-1]] + [cw_zero + fill, t["ffn1_norm"] + hold])
                    for t, fill in ((P, 0.0), (M, 0.0), (V, 1.0))]
    ssm_in = tuple(p + hold for p in (ssm_lambda_re, ssm_lambda_im, ssm_log_dt, ssm_b_re, ssm_b_im, ssm_c_re, ssm_c_im))
    (a_re, a_im, bd_re, bd_im, cd_re, cd_imn), ssm_vjp = jax.vjp(_ssm_params, *ssm_in)
    bd_re_b, bd_im_b, cd_re_b, cd_imn_b = (t.astype(BF) for t in (bd_re, bd_im, cd_re, cd_imn))
    gather_second(0, [u1, a_re, a_im, bd_re_b, bd_im_b, cd_re_b, cd_imn_b] + packed_state)
    gather_first(2)
    gather_second(1, u1)
    gather_first(3)
    (wg1,) = gathered(0, u1)
    h1, ffn1_saved, wu1, wd1 = _ffn_fwd("ffn1", u1, xh, wg1, lambda after: fetch(1, [2], [4])(after)[0],
                                        lambda after: fetch(2, [3], [5])(after)[0])
    u2, r2 = _rms_fwd("rms2", h1, row(mix_norm))
    w_in_f, cw_f = fetch(3, [4], [6])(u2)
    cw = jnp.transpose(cw_f, (1, 0, 2)).reshape(3, W)
    (proj,) = _mm(
        "in_proj", (NDEV, S // tm, 1, D // tk),
        [(u2, (tm, tk), lambda b, i, j, k: (i, k)), (w_in_f, (None, tk, W), lambda b, i, j, k: (b, k, 0))],
        [(0, 1, NN, 0)], [], [((NDEV, S, W), F32, (None, tm, W), lambda b, i, j, k: (b, i, 0))], _plain, [(tm, W)])

    v_f = proj[0]
    v_bf = v_f.astype(BF)
    bu_re3, bu_im3 = _ssm_spread("ssm_bu", v_bf, bd_re_b, bd_im_b, NN, rows)
    s_re3, s_im3 = _scan_fwd(bu_re3, bu_im3, a_re, a_im)

    def y0_epilogue(acc, v_tile, d_tile):
        y0 = acc + d_tile * v_tile
        return y0, _gelu(y0)

    y0, y1, s_re_b, s_im_b = _ssm_collect("ssm_y0", s_re3, s_im3, cd_re_b, cd_imn_b, NN, v_f, row(ssm_d),
                                          y0_epilogue, [F32, BF])

    tw = _pick(W, 512, LANE)
    w_glu_f, w_so, w_co, w_o_f = fetch(4, [5], [7])(y1)
    w_glu_f = w_glu_f.reshape(W, W)
    w_o_f = w_o_f.reshape(D, D)

    def glu_epilogue(accs, sides):
        q = accs[0] + sides[1]
        return q, _gelu(sides[0]) * _sigmoid(q)

    q_pre, y2 = _mm("ssm_glu", (1, S // tm, W // tw, 1),
                    [(y1, (tm, W), lambda b, i, j, k: (i, 0)), (w_glu_f, (W, tw), lambda b, i, j, k: (0, j))],
                    [(0, 1, NN, 0)],
                    [(y0, (tm, tw), lambda b, i, j, k: (i, j)), (row(ssm_b_glu), (1, tw), lambda b, i, j, k: (0, j))],
                    [((S, W), F32, (tm, tw), lambda b, i, j, k: (i, j)), ((S, W), BF, (tm, tw), lambda b, i, j, k: (i, j))],
                    glu_epilogue, [(tm, tw)])

    yb, conv = _conv_fwd(proj, cw, row(conv_b))

    per = W // Dc
    ga_blk = (proj, (None, tm, Dc), lambda b, i, j, k: (4 + b // per, i, b % per))
    gb_blk = (proj, (None, tm, Dc), lambda b, i, j, k: (6 + b // per, i, b % per))
    dc_out = ((S, D), BF, (tm, Dc), lambda b, i, j, k: (i, b))

    def merge_epilogue(accs, sides):
        za, zb = accs
        return _sigmoid(sides[0]) * za + _sigmoid(sides[1]) * zb, za, zb

    merged, z_a, z_b = _mm(
        "mix_merge", (NDEV, S // tm, 1, 1),
        [(y2, (tm, W), lambda b, i, j, k: (i, 0)), (yb, (tm, W), lambda b, i, j, k: (i, 0)),
         (w_so, (None, W, Dc), lambda b, i, j, k: (b, 0, 0)), (w_co, (None, W, Dc), lambda b, i, j, k: (b, 0, 0))],
        [(0, 2, NN, 0), (1, 3, NN, 1)], [ga_blk, gb_blk], [dc_out, dc_out, dc_out], merge_epilogue, [(tm, Dc)] * 2)

    (h2,) = _mm("mix_out", (1, S // tm, D // tn, D // tk),
                [(merged, (tm, tk), lambda b, i, j, k: (i, k)), (w_o_f, (tk, tn), lambda b, i, j, k: (k, j))],
                [(0, 1, NN, 0)], [(h1, (tm, tn), lambda b, i, j, k: (i, j))],
                [((S, D), F32, (tm, tn), lambda b, i, j, k: (i, j))],
                lambda accs, sides: [sides[0] + accs[0]], [(tm, tn)])

    u3, r3 = _rms_fwd("rms3", h2, row(ffn2_norm))
    (wg2,) = fetch(5, [6, 7], [])(u3)
    h3, ffn2_saved, wu2, wd2 = _ffn_fwd("ffn2", u3, h2, wg2, lambda after: gathered(6, after)[0],
                                        lambda after: gathered(7, after)[0])
    loss_vec, dh3, dh3_half, d_final_norm = _loss_head("loss_head", h3, row(final_norm), target)
    loss = lax.psum(loss_vec[0, 0], ("x", "y", "c"))
    loss_done = jnp.zeros((SUBLANE, LANE), F32) + loss

    grads, deltas, new_m, new_v = {}, {}, {}, {}

    def rs_sibling_start(tag, parts):
        lands = [lax.empty((4,) + p.shape[1:], p.dtype) for p in parts]
        return _split_start("rs_sibling_start_" + tag, parts, lands, _sibling_copies, 4)

    def rs_chips_start(tag, sibling_begun, after):
        parts, lands = _split_wait("rs_sibling_wait_" + tag, sibling_begun, _sibling_copies, after)
        sums = [_sum_sibling("rs_sum_%s_%d" % (tag, a), p, land, c_arr) for a, (p, land) in enumerate(zip(parts, lands))]
        lands2 = [lax.empty((3,) + sm.shape[1:], sm.dtype) for sm in sums]
        return _split_start("rs_chips_start_" + tag, sums, lands2, _chips_copies, 3)

    def rs_end(tag, group, begun, after):
        sums, lands2 = _split_wait("rs_chips_wait_" + tag, begun, _chips_copies, after)
        for n, sm, land2 in zip(group, sums, lands2):
            res = _finish_sharded("adamw_" + n, sm, land2, q_arr, local(P, n), local(M, n), local(V, n))
            grads[n], deltas[n], new_m[n], new_v[n] = [t.T if n in transposed else t for t in res]

    f2_dwd, f2_dwgu, f2_du = _ffn_bwd("ffn2", dh3_half, u3, ffn2_saved, wg2, wu2, wd2, deps=[loss_done])
    dwd2 = f2_dwd()
    dwg2, dwu2 = f2_dwgu()
    sib_ffn2 = rs_sibling_start("ffn2", [dwg2, dwu2, dwd2])
    du3 = f2_du(deps=[sib_ffn2[4]])
    dh2, dh2_b, d_ffn2_norm = _rms_bwd("rms3_bwd", du3, h2, r3, row(ffn2_norm), dh3, 1.0)
    rs_ffn2 = rs_chips_start("ffn2", sib_ffn2, dh2)

    dg_out = ((2, S, W), BF, (None, tm, Dc), lambda b, i, j, k: (j // per, i, j % per))
    ga_blk2 = (proj, (None, tm, Dc), lambda b, i, j, k: (4 + j // per, i, j % per))
    gb_blk2 = (proj, (None, tm, Dc), lambda b, i, j, k: (6 + j // per, i, j % per))
    dcj = lambda arr: (arr, (tm, Dc), lambda b, i, j, k: (i, j))
    dcj_out = ((S, D), BF, (tm, Dc), lambda b, i, j, k: (i, j))

    def dmerge_epilogue(accs, sides):
        dm = accs[0]
        sa, sb = _sigmoid(sides[0]), _sigmoid(sides[1])
        za, zb = sides[2].astype(F32), sides[3].astype(F32)
        return dm * sa, dm * sb, dm * za * sa * (1.0 - sa), dm * zb * sb * (1.0 - sb)

    dz_a, dz_b, dga, dgb = _mm(
        "mix_out_dx", (1, S // tm, NDEV, D // tk),
        [(dh2_b, (tm, tk), lambda b, i, j, k: (i, k)), (w_o_f, (Dc, tk), lambda b, i, j, k: (j, k))],
        [(0, 1, NT, 0)], [ga_blk2, gb_blk2, dcj(z_a), dcj(z_b)], [dcj_out, dcj_out, dg_out, dg_out],
        dmerge_epilogue, [(tm, Dc)], deps=[rs_ffn2[4]])

    td = _pick(D, M_TILE, LANE)
    (dw_o,) = _mm("mix_out_dw", (1, D // td, D // tn, S // ts),
                  [(merged, (ts, td), lambda b, i, j, k: (k, i)), (dh2_b, (ts, tn), lambda b, i, j, k: (k, j))],
                  [(0, 1, TN, 0)], [], [((D, D), BF, (td, tn), lambda b, i, j, k: (i, j))], _plain, [(td, tn)])

    wout = ((NDEV, W, Dc), BF, (None, W, Dc), lambda b, i, j, k: (b, 0, 0))
    dw_so, dw_co = _mm(
        "mix_merge_dw", (NDEV, 1, 1, S // ts),
        [(y2, (ts, W), lambda b, i, j, k: (k, 0)), (yb, (ts, W), lambda b, i, j, k: (k, 0)),
         (dz_a, (ts, Dc), lambda b, i, j, k: (k, b)), (dz_b, (ts, Dc), lambda b, i, j, k: (k, b))],
        [(0, 2, TN, 0), (1, 3, TN, 1)], [], [wout, wout], _plain, [(W, Dc)] * 2)

    def dglu_epilogue(accs, sides):
        dy2, dyb = accs
        sq = _sigmoid(sides[1])
        return dy2 * _gelu(sides[0]) * sq * (1.0 - sq), dy2 * sq, dyb

    full_w = lambda arr: (arr, (th, W), lambda b, i, j, k: (i, 0))
    full_w_out = lambda dt: ((S, W), dt, (th, W), lambda b, i, j, k: (i, 0))
    dq, dy1p, dyb = _mm(
        "mix_merge_dx", (1, S // th, 1, NDEV),
        [(dz_a, (th, Dc), lambda b, i, j, k: (i, k)), (dz_b, (th, Dc), lambda b, i, j, k: (i, k)),
         (w_so, (None, W, Dc), lambda b, i, j, k: (k, 0, 0)), (w_co, (None, W, Dc), lambda b, i, j, k: (k, 0, 0))],
        [(0, 2, NT, 0), (1, 3, NT, 1)], [full_w(y0), full_w(q_pre)], [full_w_out(BF), full_w_out(F32), full_w_out(F32)],
        dglu_epilogue, [(th, W)] * 2)

    def dy0_epilogue(accs, sides):
        dy0 = (sides[0] + accs[0]) * _gelu_grad(sides[1])
        return dy0, dy0

    wj = lambda arr: (arr, (tm, tw), lambda b, i, j, k: (i, j))
    dy0, dy0_b = _mm("ssm_glu_dx", (1, S // tm, W // tw, 1),
                     [(dq, (tm, W), lambda b, i, j, k: (i, 0)), (w_glu_f, (tw, W), lambda b, i, j, k: (j, 0))],
                     [(0, 1, NT, 0)], [wj(dy1p), wj(y0)],
                     [((S, W), F32, (tm, tw), lambda b, i, j, k: (i, j)), ((S, W), BF, (tm, tw), lambda b, i, j, k: (i, j))],
                     dy0_epilogue, [(tm, tw)])

    (dw_glu,) = _mm("ssm_glu_dw", (1, W // tw, 1, S // ts),
                    [(y1, (ts, tw), lambda b, i, j, k: (k, i)), (dq, (ts, W), lambda b, i, j, k: (k, 0))],
                    [(0, 1, TN, 0)], [], [((W, W), BF, (tw, W), lambda b, i, j, k: (i, 0))], _plain, [(tw, W)])

    tr = _pick(S, 256, SUBLANE)
    rw = ((tr, W), lambda i: (i, 0))
    vec_w = ((1, W), F32, (1, W), lambda i: (0, 0))
    d_b_glu, d_ssm_d = _ew(
        "ssm_colsums", (S // tr,), [(dq,) + rw, (dy0,) + rw, (proj, (None, tr, W), lambda i: (0, i, 0))],
        [vec_w, vec_w],
        lambda dqv, dyv, vv: (jnp.sum(dqv.astype(F32), axis=0, keepdims=True), jnp.sum(dyv * vv, axis=0, keepdims=True)),
        acc=(0, 1))

    ds_re3, ds_im3 = _ssm_spread("ssm_ds", dy0_b, cd_re_b, cd_imn_b, NT, rows)
    lam_re3, lam_im3, da_re, da_im = _scan_bwd(ds_re3, ds_im3, s_re3, s_im3, a_re, a_im)
    dv, lam_re_b, lam_im_b = _ssm_collect("ssm_dv", lam_re3, lam_im3, bd_re_b, bd_im_b, NT, dy0, row(ssm_d),
                                          lambda acc, dy_tile, d_tile: [acc + dy_tile * d_tile], [BF])
    tiles, tch, tst = bd_re.shape
    tok_ch = lambda arr: (arr, (ts, tch), lambda b, i, j, k: (k, b))
    tok_st = lambda arr: (arr, (ts, tst), lambda b, i, j, k: (k, b))
    bd_out = ((tiles, tch, tst), F32, (None, tch, tst), lambda b, i, j, k: (b, 0, 0))
    cd_out = ((tiles, tst, tch), F32, (None, tst, tch), lambda b, i, j, k: (b, 0, 0))
    dbd_re, dbd_im = _mm("ssm_dbd", (tiles, 1, 1, S // ts), [tok_ch(v_bf), tok_st(lam_re_b), tok_st(lam_im_b)],
                         [(0, 1, TN, 0), (0, 2, TN, 1)], [], [bd_out, bd_out], _plain, [(tch, tst)] * 2)
    dcd_re, dcd_imn = _mm("ssm_dcd", (tiles, 1, 1, S // ts), [tok_st(s_re_b), tok_st(s_im_b), tok_ch(dy0_b)],
                          [(0, 2, TN, 0), (1, 2, TN, 1)], [], [cd_out, cd_out], _plain, [(tst, tch)] * 2)
    d_ssm = ssm_vjp((da_re, da_im, dbd_re, dbd_im, dcd_re, dcd_imn))

    dbg, dcg, dval, d_conv_w_full, d_conv_b = _conv_bwd(dyb, proj, conv, cw)
    dproj = jnp.concatenate([dv[None], dbg[None], dcg[None], dval[None], dga, dgb], axis=0)

    (dw_in,) = _mm("in_proj_dw", (NDEV, D // td, 1, S // ts),
                   [(u2, (ts, td), lambda b, i, j, k: (k, i)), (dproj, (None, ts, W), lambda b, i, j, k: (b, k, 0))],
                   [(0, 1, TN, 0)], [], [((NDEV, D, W), BF, (None, td, W), lambda b, i, j, k: (b, i, 0))],
                   _plain, [(td, W)])
    sib_mixer = rs_sibling_start(
        "mixer", [dw_in, dw_glu.reshape(NDEV, W // NDEV, W), dw_so, dw_co, dw_o.reshape(NDEV, Dc, D)])
    (du2,) = _mm("in_proj_dx", (1, S // tm, D // tn, NDEV // K_SHARDS),
                 [(dproj, (K_SHARDS, tm, W), lambda b, i, j, k: (k, i, 0)),
                  (w_in_f, (K_SHARDS, tn, W), lambda b, i, j, k: (k, j, 0))],
                 [(0, 1, NT, 0)], [], [((S, D), F32, (tm, tn), lambda b, i, j, k: (i, j))], _plain, [(tm, tn)],
                 deps=[sib_mixer[4]])
    dh1, dh1_half, d_mix_norm = _rms_bwd("rms2_bwd", du2, h1, r2, row(mix_norm), dh2, 0.5)
    rs_mixer = rs_chips_start("mixer", sib_mixer, dh1)

    small = dict(mix_norm=d_mix_norm, ffn2_norm=d_ffn2_norm, final_norm=d_final_norm,
                 ssm_lambda_re=d_ssm[0], ssm_lambda_im=d_ssm[1], ssm_log_dt=d_ssm[2], ssm_b_re=d_ssm[3],
                 ssm_b_im=d_ssm[4], ssm_c_re=d_ssm[5], ssm_c_im=d_ssm[6], ssm_d=d_ssm_d, ssm_b_glu=d_b_glu,
                 conv_b=d_conv_b)
    early_pk = pack([small[n] for n in replicated[:-1]] + [d_conv_w_full])
    early_land = lax.dynamic_update_slice(jnp.zeros((NDEV,) + early_pk.shape, F32), early_pk[None], (me, 0, 0))
    small_begun = _split_start("gather_small_start", [early_pk], [early_land], _everyone_copies, NDEV - 1)

    f1_dwd, f1_dwgu, f1_du = _ffn_bwd("ffn1", dh1_half, u1, ffn1_saved, wg1, wu1, wd1,
                                      deps=[rs_mixer[4], small_begun[4]])
    du1 = f1_du()
    dx, _, d_ffn1_norm = _rms_bwd("rms1_bwd", du1, xh, r1, row(ffn1_norm), dh1, 1.0)
    dwd1 = f1_dwd(deps=[d_ffn1_norm])
    late = d_ffn1_norm + 0.0 * dwd1[0, :1, :1].astype(F32)
    (late_all,) = _all_gather("gather_ffn1_norm_grad", [late.reshape(-1, LANE)])
    rs_ffn1_down = rs_chips_start("ffn1_down", rs_sibling_start("ffn1_down", [dwd1]), late_all)
    (dwg1,) = f1_dwgu(deps=[rs_ffn1_down[4]], only=0)
    rs_ffn1_gate = rs_chips_start("ffn1_gate", rs_sibling_start("ffn1_gate", [dwg1]), None)
    (dwu1,) = f1_dwgu(deps=[rs_ffn1_gate[4]], only=1)
    rs_ffn1_up = rs_chips_start("ffn1_up", rs_sibling_start("ffn1_up", [dwu1]), None)
    rs_end("ffn2", sharded[8:11], rs_ffn2, rs_ffn1_up[4])
    rs_end("mixer", sharded[3:8], rs_mixer, [grads[n] for n in sharded[8:11]])
    _, (early_all,) = _split_wait("gather_small_wait", small_begun, _everyone_copies, [grads[n] for n in sharded[3:8]])
    small_all = jnp.concatenate([early_all, late_all], axis=1)
    g_pk, d_pk, m_pk, v_pk = _finish_replicated("adamw_replicated", small_all, *packed_state)

    def unpack(pk, r0, like):
        nr = -(-like.size // tile) * SUBLANE
        return pk[r0:r0 + nr].reshape(-1)[:like.size].reshape(like.shape), r0 + nr

    r0 = 0
    for n in replicated[:-1] + ["conv_w", "ffn1_norm"]:
        like = d_conv_w_full if n == "conv_w" else P[n]
        for store, pk in ((grads, g_pk), (deltas, d_pk), (new_m, m_pk), (new_v, v_pk)):
            store[n], r1 = unpack(pk, r0, like)
        r0 = r1
    g_cw_full = grads["conv_w"]
    cwl = conv_w.shape[1]
    g_cw = lax.dynamic_slice_in_dim(g_cw_full, me * cwl, cwl, axis=1)
    full3 = ((3, cwl), lambda i: (0, 0))
    grads["conv_w"], deltas["conv_w"], new_m["conv_w"], new_v["conv_w"] = _ew(
        "adamw_conv_w", (1,), [(g_cw,) + full3, (conv_w,) + full3, (m_conv_w,) + full3, (v_conv_w,) + full3],
        [((3, cwl), F32) + full3] * 4, lambda g, w, m, v: (g,) + _adamw(w, g, m, v))
    rs_end("ffn1_down", sharded[2:3], rs_ffn1_down, [g_pk, grads["conv_w"]])
    rs_end("ffn1_gate", sharded[0:1], rs_ffn1_gate, grads["ffn1_w_down"])
    rs_end("ffn1_up", sharded[1:2], rs_ffn1_up, grads["ffn1_w_gate"])

    return (loss, dx.reshape(x.shape), *[grads[n] for n in names], *[deltas[n] for n in names],
            *[new_m[n] for n in names], *[new_v[n] for n in names])
```

```python
import math

import jax
import jax.numpy as jnp
from jax import lax
from jax.experimental import pallas as pl
from jax.experimental.pallas import tpu as pltpu

F32 = jnp.float32
BF = jnp.bfloat16
I32 = jnp.int32
MESH = pl.DeviceIdType.MESH
LANE = 128
SUBLANE = 8
NDEV = 8
EW_BLOCK = 256 * 1024
M_TILE = 1024
K_TILE = 2048
K_SHARDS = 2
DMA_CHUNKS = 4
EPS = 1e-6
ADAM_LR, ADAM_B1, ADAM_B2, ADAM_EPS, ADAM_WD, ADAM_STEP = 0.001, 0.9, 0.999, 1e-08, 0.01, 10
NN = ((1,), (0,))
NT = ((1,), (1,))
TN = ((0,), (0,))
HBM = pl.BlockSpec(memory_space=pltpu.HBM)


def _pick(n, pref, mult):
    t = min(pref, n)
    t -= t % mult
    while t >= mult:
        if n % t == 0:
            return t
        t -= mult
    return n


def _sigmoid(x):
    return 1.0 / (1.0 + jnp.exp(-x))


_GELU_C = math.sqrt(2.0 / math.pi)


def _gelu(x):
    return 0.5 * x * (1.0 + jnp.tanh(_GELU_C * (x + 0.044715 * x * x * x)))


def _gelu_grad(x):
    t = jnp.tanh(_GELU_C * (x + 0.044715 * x * x * x))
    return 0.5 * (1.0 + t) + 0.5 * x * (1.0 - t * t) * _GELU_C * (1.0 + 3.0 * 0.044715 * x * x)


def _dep_specs(deps, rank):
    return [(d, d.shape, lambda *_, nd=d.ndim: (0,) * nd) for d in deps]


def _mm(name, grid, ops, pairs, sides, outs, epilogue, acc_shapes, deps=()):
    nk = grid[-1]
    n_ops, n_sides, n_outs = len(ops), len(sides), len(outs)
    dep_specs = _dep_specs(deps, len(grid))
    n_deps = len(dep_specs)

    def body(*refs):
        op_refs = refs[:n_ops]
        side_refs = refs[n_ops:n_ops + n_sides]
        out_refs = refs[n_ops + n_sides + n_deps:n_ops + n_sides + n_deps + n_outs]
        acc_refs = refs[n_ops + n_sides + n_deps + n_outs:]

        def partials():
            res = [None] * len(acc_shapes)
            for ia, ib, dims, ai in pairs:
                a_ref, b_ref = op_refs[ia], op_refs[ib]
                for s in range(a_ref.shape[0] if len(a_ref.shape) == 3 else 1):
                    a, b = (a_ref[s], b_ref[s]) if len(a_ref.shape) == 3 else (a_ref[...], b_ref[...])
                    p = lax.dot_general(a, b, (dims, ((), ())), preferred_element_type=F32)
                    res[ai] = p if res[ai] is None else res[ai] + p
            return res

        def finish(accs):
            vals = epilogue(accs, [s[...] for s in side_refs])
            for o, v in zip(out_refs, vals):
                o[...] = v.astype(o.dtype)

        if nk == 1:
            finish(partials())
        else:
            k = pl.program_id(len(grid) - 1)

            @pl.when(k == 0)
            def _():
                for a, p in zip(acc_refs, partials()):
                    a[...] = p

            @pl.when(k > 0)
            def _():
                for a, p in zip(acc_refs, partials()):
                    a[...] += p

            @pl.when(k == nk - 1)
            def _():
                finish([a[...] for a in acc_refs])

    return pl.pallas_call(
        body, name=name, grid=grid,
        in_specs=[pl.BlockSpec(b, m) for (_, b, m) in list(ops) + list(sides) + dep_specs],
        out_specs=[pl.BlockSpec(b, m) for (_, _, b, m) in outs],
        out_shape=[jax.ShapeDtypeStruct(s, d) for (s, d, _, _) in outs],
        scratch_shapes=[pltpu.VMEM(s, F32) for s in acc_shapes] if nk > 1 else [],
        compiler_params=pltpu.CompilerParams(
            dimension_semantics=("parallel",) * (len(grid) - 1) + ("arbitrary",)),
    )(*[a for (a, _, _) in list(ops) + list(sides) + dep_specs])


def _ew(name, grid, ins, outs, fn, acc=(), deps=()):
    n_in = len(ins)
    dep_specs = _dep_specs(deps, len(grid))

    def body(*refs):
        vals = fn(*[r[...] for r in refs[:n_in]])
        first = pl.program_id(0) == 0
        for idx, (o, v) in enumerate(zip(refs[n_in + len(dep_specs):], vals)):
            if idx in acc:
                @pl.when(first)
                def _(o=o, v=v):
                    o[...] = v.astype(o.dtype)

                @pl.when(jnp.logical_not(first))
                def _(o=o, v=v):
                    o[...] += v.astype(o.dtype)
            else:
                o[...] = v.astype(o.dtype)

    return pl.pallas_call(
        body, name=name, grid=grid,
        in_specs=[pl.BlockSpec(b, m) for (_, b, m) in list(ins) + dep_specs],
        out_specs=[pl.BlockSpec(b, m) for (_, _, b, m) in outs],
        out_shape=[jax.ShapeDtypeStruct(s, d) for (s, d, _, _) in outs],
        compiler_params=pltpu.CompilerParams(
            dimension_semantics=(("arbitrary",) if acc else ("parallel",)) * len(grid)),
    )(*[a for (a, _, _) in list(ins) + dep_specs])


def _position():
    x, y, c = lax.axis_index("x"), lax.axis_index("y"), lax.axis_index("c")
    chips = [(1 - x, y), (x, 1 - y), (1 - x, 1 - y)]
    return x, y, c, chips


SEM = pl.BlockSpec(memory_space=pltpu.SEMAPHORE)
EFFECT = pltpu.SideEffectType.DATAFLOW_SIDE_EFFECTING


def _in_hbm(v):
    return pltpu.with_memory_space_constraint(v, pltpu.HBM)


def _split_start(name, srcs, lands, make_copies, n_per, after=()):
    n, nb = len(srcs), len(srcs) + len(lands)
    n_sems = len(lands) * n_per
    after = list(after)

    def body(*refs):
        send_sems, recv_sems = refs[nb + len(after)], refs[nb + len(after) + 1]
        for cp in make_copies(refs[:n], refs[n:nb], send_sems, recv_sems):
            cp.start()
        refs[-1][...] = jnp.zeros_like(refs[-1])

    outs = pl.pallas_call(
        body, name=name,
        out_shape=(pltpu.SemaphoreType.DMA((n_sems,)), pltpu.SemaphoreType.DMA((n_sems,)),
                   *[pltpu.HBM(v.shape, v.dtype) for v in list(srcs) + list(lands)],
                   jax.ShapeDtypeStruct((SUBLANE, LANE), F32)),
        in_specs=[HBM] * nb + [pl.BlockSpec(memory_space=pl.ANY)] * len(after),
        out_specs=(SEM, SEM, *[HBM] * nb, pl.BlockSpec(memory_space=pltpu.VMEM)),
        input_output_aliases={i: 2 + i for i in range(nb)},
        compiler_params=pltpu.CompilerParams(has_side_effects=EFFECT),
    )(*[_in_hbm(v) for v in list(srcs) + list(lands)], *after)
    return outs[0], outs[1], list(outs[2:2 + n]), list(outs[2 + n:2 + nb]), outs[-1]


def _split_wait(name, started, make_copies, after):
    send_sems, recv_sems, srcs, lands, _ = started
    n, nb = len(srcs), len(srcs) + len(lands)

    def body(*refs):
        for cp in make_copies(refs[:n], refs[n:nb], refs[nb], refs[nb + 1]):
            cp.wait_send()
            cp.wait_recv()

    order = [] if after is None else list(after) if isinstance(after, (list, tuple)) else [after]
    outs = pl.pallas_call(
        body, name=name,
        out_shape=tuple(pltpu.HBM(v.shape, v.dtype) for v in srcs + lands),
        in_specs=[HBM] * nb + [SEM, SEM] + [pl.BlockSpec(memory_space=pl.ANY)] * len(order),
        out_specs=tuple([HBM] * nb),
        input_output_aliases={i: i for i in range(nb)},
        compiler_params=pltpu.CompilerParams(has_side_effects=EFFECT),
    )(*srcs, *lands, send_sems, recv_sems, *order)
    return list(outs[:n]), list(outs[n:])


def _gather_first_copies(xs, lands, send_sems, recv_sems):
    x, y, c, _ = _position()
    copies = []
    for a in range(len(xs)):
        for k, peer in enumerate([(x, y, 1 - c), (1 - x, y, c), (x, 1 - y, c)]):
            copies.append(pltpu.make_async_remote_copy(
                src_ref=xs[a], dst_ref=lands[a].at[4 * x + 2 * y + c],
                send_sem=send_sems.at[3 * a + k], recv_sem=recv_sems.at[3 * a + k], device_id=peer, device_id_type=MESH))
    return copies


def _gather_second_copies(xs, lands, send_sems, recv_sems):
    x, y, c, _ = _position()
    copies = []
    for a in range(len(lands)):
        rows = lands[a].shape[1]
        unit = SUBLANE * (4 // jnp.dtype(lands[a].dtype).itemsize)
        half = rows // 2 // unit * unit or rows
        parts = [((1 - x, y), (x, 1 - y), pl.ds(0, half))]
        if half < rows:
            parts.append(((x, 1 - y), (1 - x, y), pl.ds(half, rows - half)))
        for k, (block, to, rs) in enumerate(parts):
            ref = lands[a].at[4 * block[0] + 2 * block[1] + c, rs]
            copies.append(pltpu.make_async_remote_copy(
                src_ref=ref, dst_ref=ref, send_sem=send_sems.at[2 * a + k], recv_sem=recv_sems.at[2 * a + k],
                device_id=(*to, c), device_id_type=MESH))
    return copies


def _chips_copies(ps, lands, send_sems, recv_sems):
    x, y, c, chips = _position()
    copies = []
    for a in range(len(ps)):
        for j, chip in enumerate(chips):
            copies.append(pltpu.make_async_remote_copy(
                src_ref=ps[a].at[2 * chip[0] + chip[1]], dst_ref=lands[a].at[j],
                send_sem=send_sems.at[3 * a + j], recv_sem=recv_sems.at[3 * a + j], device_id=(*chip, c),
                device_id_type=MESH))
    return copies


def _sibling_copies(gs, lands, send_sems, recv_sems):
    x, y, c, _ = _position()
    copies = []
    for a in range(len(gs)):
        for q in range(4):
            copies.append(pltpu.make_async_remote_copy(
                src_ref=gs[a].at[2 * q + 1 - c], dst_ref=lands[a].at[q],
                send_sem=send_sems.at[4 * a + q], recv_sem=recv_sems.at[4 * a + q],
                device_id=(x, y, 1 - c), device_id_type=MESH))
    return copies


def _everyone_copies(xs, lands, send_sems, recv_sems):
    x, y, c, _ = _position()
    flip = lambda v, bit: 1 - v if bit else v
    copies = []
    for a in range(len(xs)):
        for k in range(1, NDEV):
            copies.append(pltpu.make_async_remote_copy(
                src_ref=xs[a], dst_ref=lands[a].at[4 * x + 2 * y + c],
                send_sem=send_sems.at[7 * a + k - 1], recv_sem=recv_sems.at[7 * a + k - 1],
                device_id=(flip(x, k & 4), flip(y, k & 2), flip(c, k & 1)), device_id_type=MESH))
    return copies


def _row_chunks(rows, dtype):
    unit = SUBLANE * (4 // jnp.dtype(dtype).itemsize)
    units = rows // unit
    if rows % unit or units < 2:
        return [(0, rows)]
    k = min(DMA_CHUNKS, units)
    sizes = [(units // k + (1 if i < units % k else 0)) * unit for i in range(k)]
    return [(sum(sizes[:i]), sz) for i, sz in enumerate(sizes)]


def _gather_forward(name, lands, after=()):
    n = len(lands)
    after = list(after)

    def body(*refs):
        ins, outs = refs[:n], refs[n + len(after):2 * n + len(after)]
        send_sems, recv_sems = refs[2 * n + len(after):]
        x, y, c, chips = _position()
        whole, chunks = [], []
        for a in range(n):
            rows = _row_chunks(ins[a].shape[1], ins[a].dtype)
            for j, chip in enumerate(chips):
                slot = 4 * chip[0] + 2 * chip[1]

                def to_sibling(src, dst):
                    return pltpu.make_async_remote_copy(
                        src_ref=src, dst_ref=dst, send_sem=send_sems.at[a, j], recv_sem=recv_sems.at[a, j],
                        device_id=(x, y, 1 - c), device_id_type=MESH)

                whole.append(to_sibling(ins[a].at[slot + c], outs[a].at[slot + 1 - c]))
                chunks += [to_sibling(ins[a].at[slot + c, pl.ds(r0, nr)], outs[a].at[slot + c, pl.ds(r0, nr)])
                           for r0, nr in rows]
        for cp in chunks:
            cp.start()
        for cp in whole:
            cp.wait()

    return pl.pallas_call(
        body, name=name,
        out_shape=[jax.ShapeDtypeStruct(l.shape, l.dtype) for l in lands],
        in_specs=[HBM] * n + [pl.BlockSpec(memory_space=pl.ANY)] * len(after), out_specs=[HBM] * n,
        input_output_aliases={a: a for a in range(n)},
        scratch_shapes=[pltpu.SemaphoreType.DMA((n, 3)), pltpu.SemaphoreType.DMA((n, 3))],
    )(*lands, *after)


def _sum_sibling(name, g, land, c_arr):
    _, R, C = g.shape
    tr = _pick(R, 512, SUBLANE)

    def body(c_ref, g_ref, l_ref, o_ref):
        o_ref[...] = (g_ref[...].astype(F32) + l_ref[...].astype(F32)).astype(o_ref.dtype)

    return pl.pallas_call(
        body, name=name,
        grid_spec=pltpu.PrefetchScalarGridSpec(
            num_scalar_prefetch=1, grid=(4, R // tr),
            in_specs=[pl.BlockSpec((None, tr, C), lambda q, i, cr: (2 * q + cr[0], i, 0)),
                      pl.BlockSpec((None, tr, C), lambda q, i, cr: (q, i, 0))],
            out_specs=pl.BlockSpec((None, tr, C), lambda q, i, cr: (q, i, 0))),
        out_shape=jax.ShapeDtypeStruct((4, R, C), g.dtype),
        compiler_params=pltpu.CompilerParams(dimension_semantics=("parallel", "parallel")),
    )(c_arr, g, land)


def _adamw(w, g, m, v):
    m = ADAM_B1 * m + (1.0 - ADAM_B1) * g
    v = ADAM_B2 * v + (1.0 - ADAM_B2) * (g * g)
    m_hat = m / (1.0 - ADAM_B1 ** ADAM_STEP)
    v_hat = v / (1.0 - ADAM_B2 ** ADAM_STEP)
    delta = -ADAM_LR * (m_hat / (jnp.sqrt(v_hat) + ADAM_EPS) + ADAM_WD * w)
    return delta, m, v


def _finish_sharded(name, sums, land, q_arr, w, m, v):
    R, C = w.shape
    tr = _pick(R, 512, SUBLANE)
    tc = _pick(C, max(LANE, EW_BLOCK // tr), LANE)

    def body(q_ref, p_ref, l_ref, w_ref, m_ref, v_ref, g_out, d_out, m_out, v_out):
        g = p_ref[...].astype(F32)
        for j in range(3):
            g = g + l_ref[j].astype(F32)
        d, mn, vn = _adamw(w_ref[...], g, m_ref[...], v_ref[...])
        g_out[...] = g
        d_out[...] = d
        m_out[...] = mn
        v_out[...] = vn

    blk = pl.BlockSpec((tr, tc), lambda i, j, qr: (i, j))
    return pl.pallas_call(
        body, name=name,
        grid_spec=pltpu.PrefetchScalarGridSpec(
            num_scalar_prefetch=1, grid=(R // tr, C // tc),
            in_specs=[pl.BlockSpec((None, tr, tc), lambda i, j, qr: (qr[0], i, j)),
                      pl.BlockSpec((3, tr, tc), lambda i, j, qr: (0, i, j)), blk, blk, blk],
            out_specs=[blk] * 4),
        out_shape=[jax.ShapeDtypeStruct((R, C), F32)] * 4,
        compiler_params=pltpu.CompilerParams(dimension_semantics=("parallel", "parallel")),
    )(q_arr, sums, land, w, m, v)


def _finish_replicated(name, gathered, w, m, v):
    _, R, C = gathered.shape
    tr = _pick(R, 256, SUBLANE)

    def fn(gv, wv, mv, vv):
        g = gv[0]
        for d in range(1, NDEV):
            g = g + gv[d]
        dl, mn, vn = _adamw(wv, g, mv, vv)
        return g, dl, mn, vn

    row = ((tr, C), lambda i: (i, 0))
    return _ew(name, (R // tr,),
               [(gathered, (NDEV, tr, C), lambda i: (0, i, 0)), (w,) + row, (m,) + row, (v,) + row],
               [((R, C), F32) + row] * 4, fn)


def _rms_fwd(name, h, g, deps=()):
    S, D = h.shape
    tr = _pick(S, 256, SUBLANE)

    def fn(hv, gv):
        r = lax.rsqrt(jnp.mean(hv * hv, axis=-1, keepdims=True) + EPS)
        return hv * r * gv, r

    return _ew(name, (S // tr,),
               [(h, (tr, D), lambda i: (i, 0)), (g, (1, D), lambda i: (0, 0))],
               [((S, D), BF, (tr, D), lambda i: (i, 0)), ((S, 1), F32, (tr, 1), lambda i: (i, 0))], fn, deps=deps)


def _rms_bwd(name, du, h, r, g, dres, scale):
    S, D = h.shape
    tr = _pick(S, 256, SUBLANE)

    def fn(duv, hv, rv, gv, drv):
        xn = hv * rv
        dxn = duv * gv
        dh = drv + rv * (dxn - xn * jnp.mean(dxn * xn, axis=-1, keepdims=True))
        return dh, scale * dh, jnp.sum(duv * xn, axis=0, keepdims=True)

    row = ((tr, D), lambda i: (i, 0))
    return _ew(name, (S // tr,),
               [(du,) + row, (h,) + row, (r, (tr, 1), lambda i: (i, 0)), (g, (1, D), lambda i: (0, 0)), (dres,) + row],
               [((S, D), F32) + row, ((S, D), BF) + row, ((1, D), F32, (1, D), lambda i: (0, 0))], fn, acc=(2,))


def _loss_head(name, h, g, target):
    S, D = h.shape
    tr = _pick(S, 256, SUBLANE)

    def fn(hv, gv, tv):
        r = lax.rsqrt(jnp.mean(hv * hv, axis=-1, keepdims=True) + EPS)
        xn = hv * r
        diff = xn * gv - tv
        loss = 0.5 * jnp.sum(jnp.mean(diff * diff, axis=-1, keepdims=True))
        dout = diff / D
        dxn = dout * gv
        dh = r * (dxn - xn * jnp.mean(dxn * xn, axis=-1, keepdims=True))
        return (jnp.zeros((1, LANE), F32) + loss, dh, 0.5 * dh, jnp.sum(dout * xn, axis=0, keepdims=True))

    row = ((tr, D), lambda i: (i, 0))
    return _ew(name, (S // tr,),
               [(h,) + row, (g, (1, D), lambda i: (0, 0)), (target,) + row],
               [((1, LANE), F32, (1, LANE), lambda i: (0, 0)), ((S, D), F32) + row, ((S, D), BF) + row,
                ((1, D), F32, (1, D), lambda i: (0, 0))], fn, acc=(0, 3))


def _ffn_fwd(tag, u, h, wg, fetch_wu, fetch_wd):
    S, D = u.shape
    Fs = wg.shape[1]
    tm, tk = _pick(S, M_TILE, SUBLANE), _pick(D, K_TILE, LANE)
    act = ((NDEV, S, Fs), BF, (None, tm, Fs), lambda b, i, j, k: (b, i, 0))
    lhs = (u, (tm, tk), lambda b, i, j, k: (i, k))
    rhs = lambda w: (w, (None, Fs, tk), lambda b, i, j, k: (b, 0, k))
    (gt,) = _mm(tag + "_gate", (NDEV, S // tm, 1, D // tk), [lhs, rhs(wg)], [(0, 1, NT, 0)], [], [act],
                lambda accs, sides: accs, [(tm, Fs)])
    wu = fetch_wu(gt)

    def up_epilogue(accs, sides):
        g = sides[0].astype(F32)
        return accs[0], g * _sigmoid(g) * accs[0]

    up, a = _mm(tag + "_up", (NDEV, S // tm, 1, D // tk), [lhs, rhs(wu)], [(0, 1, NT, 0)],
                [(gt, (None, tm, Fs), lambda b, i, j, k: (b, i, 0))], [act, act], up_epilogue, [(tm, Fs)])
    wd = fetch_wd(a)
    tn = _pick(D, 1024, LANE)
    (hn,) = _mm(
        tag + "_down", (1, S // tm, D // tn, NDEV // K_SHARDS),
        [(a, (K_SHARDS, tm, Fs), lambda b, i, j, k: (k, i, 0)), (wd, (K_SHARDS, Fs, tn), lambda b, i, j, k: (k, 0, j))],
        [(0, 1, NN, 0)], [(h, (tm, tn), lambda b, i, j, k: (i, j))],
        [((S, D), F32, (tm, tn), lambda b, i, j, k: (i, j))],
        lambda accs, sides: [sides[0] + 0.5 * accs[0]], [(tm, tn)])
    return hn, (gt, up, a), wu, wd


def _ffn_bwd(tag, dhs, u, saved, wg, wu, wd, deps=()):
    gt, up, a = saved
    S, D = u.shape
    Fs = wg.shape[1]
    tm, tk = _pick(S, M_TILE, SUBLANE), _pick(D, K_TILE, LANE)
    act_in = lambda arr: (arr, (None, tm, Fs), lambda b, i, j, k: (b, i, 0))
    act_out = ((NDEV, S, Fs), BF, (None, tm, Fs), lambda b, i, j, k: (b, i, 0))

    def act_epilogue(accs, sides):
        da = accs[0]
        gtv, upv = sides[0].astype(F32), sides[1].astype(F32)
        sg = _sigmoid(gtv)
        return da * upv * sg * (1.0 + gtv * (1.0 - sg)), da * gtv * sg

    dgt, dup = _mm(
        tag + "_dact", (NDEV, S // tm, 1, D // tk),
        [(dhs, (tm, tk), lambda b, i, j, k: (i, k)), (wd, (None, Fs, tk), lambda b, i, j, k: (b, 0, k))],
        [(0, 1, NT, 0)], [act_in(gt), act_in(up)], [act_out, act_out], act_epilogue, [(tm, Fs)], deps=deps)

    ts = _pick(S, K_TILE, SUBLANE)
    tn = _pick(D, 1024, LANE)
    wgrad = ((NDEV, Fs, D), BF, (None, Fs, tn), lambda b, i, j, k: (b, 0, j))
    tok = lambda arr: (arr, (None, ts, Fs), lambda b, i, j, k: (b, k, 0))

    def grad_down(deps=()):
        return _mm(
            tag + "_dwd", (NDEV, 1, D // tn, S // ts),
            [tok(a), (dhs, (ts, tn), lambda b, i, j, k: (k, j))],
            [(0, 1, TN, 0)], [], [wgrad], lambda accs, sides: accs, [(Fs, tn)], deps=deps)[0]

    def grad_gate_up(deps=(), only=None):
        acts = [dgt, dup] if only is None else [(dgt, dup)[only]]
        n = len(acts)
        return _mm(
            tag + ("_dwgu" if only is None else ("_dwg", "_dwu")[only]), (NDEV, 1, D // tn, S // ts),
            [tok(t) for t in acts] + [(u, (ts, tn), lambda b, i, j, k: (k, j))],
            [(i, n, TN, i) for i in range(n)], [], [wgrad] * n, lambda accs, sides: accs, [(Fs, tn)] * n, deps=deps)

    def du(deps=()):
        return _mm(
            tag + "_du", (1, S // tm, D // tn, NDEV // K_SHARDS),
            [(dgt, (K_SHARDS, tm, Fs), lambda b, i, j, k: (k, i, 0)), (dup, (K_SHARDS, tm, Fs), lambda b, i, j, k: (k, i, 0)),
             (wg, (K_SHARDS, Fs, tn), lambda b, i, j, k: (k, 0, j)), (wu, (K_SHARDS, Fs, tn), lambda b, i, j, k: (k, 0, j))],
            [(0, 2, NN, 0), (1, 3, NN, 0)], [], [((S, D), F32, (tm, tn), lambda b, i, j, k: (i, j))],
            lambda accs, sides: accs, [(tm, tn)], deps=deps)[0]

    return grad_down, grad_gate_up, du


def _ssm_params(lam_re, lam_im, log_dt, b_re, b_im, c_re, c_im):
    G, N = lam_re.shape
    C = b_re.shape[2]
    lam_re = jnp.minimum(lam_re, -1e-4)
    dt = jnp.exp(log_dt)[:, None]
    mag = jnp.exp(lam_re * dt)
    a_re = mag * jnp.cos(lam_im * dt)
    a_im = mag * jnp.sin(lam_im * dt)
    den = lam_re * lam_re + lam_im * lam_im
    p = a_re - 1.0
    f_re = ((p * lam_re + a_im * lam_im) / den)[:, :, None]
    f_im = ((a_im * lam_re - p * lam_im) / den)[:, :, None]
    bb_re = f_re * b_re - f_im * b_im
    bb_im = f_re * b_im + f_im * b_re
    gpt = LANE // C
    tiles = G // gpt
    eye = jnp.eye(gpt, dtype=F32)

    def bd(bb):
        return jnp.einsum("bgnc,gh->bgchn", bb.reshape(tiles, gpt, N, C), eye).reshape(tiles, gpt * C, gpt * N)

    def cd(cc):
        return jnp.einsum("bgcn,gh->bgnhc", cc.reshape(tiles, gpt, C, N), eye).reshape(tiles, gpt * N, gpt * C)

    rows = G * N // LANE
    return (a_re.reshape(rows, LANE), a_im.reshape(rows, LANE), bd(bb_re), bd(bb_im), cd(c_re), cd(-c_im))


def _tile_states(ref3, b, per):
    return jnp.concatenate([ref3[:, per * b + r, :] for r in range(per)], axis=1).astype(BF)


def _ssm_spread(name, x, m_re, m_im, dims, rows):
    S, W = x.shape
    tiles = m_re.shape[0]
    tch, per = W // tiles, rows // tiles
    tq = _pick(S, 256, SUBLANE)

    def body(x_ref, mre_ref, mim_ref, ore_ref, oim_ref):
        for b in range(tiles):
            xb = x_ref[:, b * tch:(b + 1) * tch]
            for m_ref, o_ref in ((mre_ref, ore_ref), (mim_ref, oim_ref)):
                val = lax.dot_general(xb, m_ref[b], (dims, ((), ())), preferred_element_type=F32)
                for r in range(per):
                    o_ref[:, per * b + r, :] = val[:, r * LANE:(r + 1) * LANE]

    whole = lambda m: pl.BlockSpec(m.shape, lambda i: (0, 0, 0))
    st = pl.BlockSpec((tq, rows, LANE), lambda i: (i, 0, 0))
    return pl.pallas_call(
        body, name=name, grid=(S // tq,),
        in_specs=[pl.BlockSpec((tq, W), lambda i: (i, 0)), whole(m_re), whole(m_im)], out_specs=[st, st],
        out_shape=[jax.ShapeDtypeStruct((S, rows, LANE), F32)] * 2,
        compiler_params=pltpu.CompilerParams(dimension_semantics=("parallel",)),
    )(x, m_re, m_im)


def _ssm_collect(name, z_re3, z_im3, m_re, m_im, dims, side, gain, epilogue, out_dtypes):
    S, rows, _ = z_re3.shape
    tiles = m_re.shape[0]
    W = side.shape[1]
    tch, per = W // tiles, rows // tiles
    tq = _pick(S, 256, SUBLANE)
    n_out = len(out_dtypes)

    def body(zre_ref, zim_ref, mre_ref, mim_ref, side_ref, gain_ref, *out_refs):
        for b in range(tiles):
            cols = slice(b * tch, (b + 1) * tch)
            zre, zim = _tile_states(zre_ref, b, per), _tile_states(zim_ref, b, per)
            acc = lax.dot_general(zre, mre_ref[b], (dims, ((), ())), preferred_element_type=F32)
            acc = acc + lax.dot_general(zim, mim_ref[b], (dims, ((), ())), preferred_element_type=F32)
            for o, v in zip(out_refs[:n_out], epilogue(acc, side_ref[:, cols], gain_ref[:, cols])):
                o[:, cols] = v.astype(o.dtype)
            out_refs[n_out][:, b * per * LANE:(b + 1) * per * LANE] = zre
            out_refs[n_out + 1][:, b * per * LANE:(b + 1) * per * LANE] = zim

    whole = lambda m: pl.BlockSpec(m.shape, lambda i: (0, 0, 0))
    st = pl.BlockSpec((tq, rows, LANE), lambda i: (i, 0, 0))
    ch = pl.BlockSpec((tq, W), lambda i: (i, 0))
    flat = pl.BlockSpec((tq, rows * LANE), lambda i: (i, 0))
    return pl.pallas_call(
        body, name=name, grid=(S // tq,),
        in_specs=[st, st, whole(m_re), whole(m_im), ch, pl.BlockSpec((1, W), lambda i: (0, 0))],
        out_specs=[ch] * n_out + [flat, flat],
        out_shape=[jax.ShapeDtypeStruct((S, W), dt) for dt in out_dtypes]
        + [jax.ShapeDtypeStruct((S, rows * LANE), BF)] * 2,
        compiler_params=pltpu.CompilerParams(dimension_semantics=("parallel",)),
    )(z_re3, z_im3, m_re, m_im, side, gain)


def _scan_fwd(bu_re, bu_im, a_re, a_im):
    S, R, _ = bu_re.shape
    tc = _pick(S, 256, SUBLANE)

    def body(bre, bim, are, aim, sre, sim, carry):
        @pl.when(pl.program_id(0) == 0)
        def _():
            carry[...] = jnp.zeros_like(carry)

        ar, ai = are[...], aim[...]

        def step(t, c):
            pr, pi = c
            nr = ar * pr - ai * pi + bre[t]
            ni = ar * pi + ai * pr + bim[t]
            sre[t] = nr
            sim[t] = ni
            return nr, ni

        pr, pi = lax.fori_loop(0, tc, step, (carry[0], carry[1]), unroll=8)
        carry[0] = pr
        carry[1] = pi

    blk = pl.BlockSpec((tc, R, LANE), lambda i: (i, 0, 0))
    par = pl.BlockSpec((R, LANE), lambda i: (0, 0))
    return pl.pallas_call(
        body, name="ssm_scan_fwd", grid=(S // tc,),
        in_specs=[blk, blk, par, par], out_specs=[blk, blk],
        out_shape=[jax.ShapeDtypeStruct((S, R, LANE), F32)] * 2,
        scratch_shapes=[pltpu.VMEM((2, R, LANE), F32)],
        compiler_params=pltpu.CompilerParams(dimension_semantics=("arbitrary",)),
    )(bu_re, bu_im, a_re, a_im)


def _scan_bwd(ds_re, ds_im, s_re, s_im, a_re, a_im):
    S, R, _ = ds_re.shape
    tc = _pick(S, 256, SUBLANE)
    nc = S // tc

    def body(dre, dim_, sre, sim, are, aim, lre, lim, dar, dai, carry):
        @pl.when(pl.program_id(0) == 0)
        def _():
            carry[...] = jnp.zeros_like(carry)
            dar[...] = jnp.zeros_like(dar)
            dai[...] = jnp.zeros_like(dai)

        ar, ai = are[...], aim[...]

        def step(tt, c):
            t = tc - 1 - tt
            lr, li, gr, gi = c
            sr, si = sre[t], sim[t]
            gr = gr + lr * sr + li * si
            gi = gi + li * sr - lr * si
            nlr = dre[t] + ar * lr + ai * li
            nli = dim_[t] + ar * li - ai * lr
            lre[t] = nlr
            lim[t] = nli
            return nlr, nli, gr, gi

        lr, li, gr, gi = lax.fori_loop(0, tc, step, (carry[0], carry[1], dar[...], dai[...]), unroll=8)
        carry[0] = lr
        carry[1] = li
        dar[...] = gr
        dai[...] = gi

    blk = pl.BlockSpec((tc, R, LANE), lambda i: (nc - 1 - i, 0, 0))
    par = pl.BlockSpec((R, LANE), lambda i: (0, 0))
    return pl.pallas_call(
        body, name="ssm_scan_bwd", grid=(nc,),
        in_specs=[blk, blk, blk, blk, par, par], out_specs=[blk, blk, par, par],
        out_shape=[jax.ShapeDtypeStruct((S, R, LANE), F32)] * 2 + [jax.ShapeDtypeStruct((R, LANE), F32)] * 2,
        scratch_shapes=[pltpu.VMEM((2, R, LANE), F32)],
        compiler_params=pltpu.CompilerParams(dimension_semantics=("arbitrary",)),
    )(ds_re, ds_im, s_re, s_im, a_re, a_im)


def _shift_down(z, k):
    t = lax.broadcasted_iota(I32, z.shape, 0)
    return jnp.where(t >= k, pltpu.roll(z, k, 0), 0.0)


def _shift_up(z, k):
    n = z.shape[0]
    t = lax.broadcasted_iota(I32, z.shape, 0)
    return jnp.where(t < n - k, pltpu.roll(z, n - k, 0), 0.0)


def _conv_fwd(proj, cw, cb):
    _, S, W = proj.shape
    ct = _pick(W, 256, LANE)

    def fn(bg, cg, val, w, b):
        z = cg * val
        conv = b + w[0:1] * _shift_down(z, 2) + w[1:2] * _shift_down(z, 1) + w[2:3] * z
        return bg * conv, conv

    sl = lambda s: (proj, (None, S, ct), lambda j, s=s: (s, 0, j))
    col = ((S, ct), lambda j: (0, j))
    return _ew("conv_fwd", (W // ct,),
               [sl(1), sl(2), sl(3), (cw, (3, ct), lambda j: (0, j)), (cb, (1, ct), lambda j: (0, j))],
               [((S, W), BF) + col, ((S, W), F32) + col], fn)


def _conv_bwd(dyb, proj, conv, cw):
    _, S, W = proj.shape
    ct = _pick(W, 256, LANE)

    def fn(dy, bg, cg, val, cv, w):
        z = cg * val
        z1, z2 = _shift_down(z, 1), _shift_down(z, 2)
        dconv = dy * bg
        dz = w[2:3] * dconv + w[1:2] * _shift_up(dconv, 1) + w[0:1] * _shift_up(dconv, 2)
        dw = jnp.concatenate([jnp.sum(dconv * z2, axis=0, keepdims=True), jnp.sum(dconv * z1, axis=0, keepdims=True),
                              jnp.sum(dconv * z, axis=0, keepdims=True)], axis=0)
        return dy * cv, dz * val, dz * cg, dw, jnp.sum(dconv, axis=0, keepdims=True)

    sl = lambda s: (proj, (None, S, ct), lambda j, s=s: (s, 0, j))
    col = ((S, ct), lambda j: (0, j))
    return _ew("conv_bwd", (W // ct,),
               [(dyb,) + col, sl(1), sl(2), sl(3), (conv,) + col, (cw, (3, ct), lambda j: (0, j))],
               [((S, W), BF) + col, ((S, W), BF) + col, ((S, W), BF) + col,
                ((3, W), F32, (3, ct), lambda j: (0, j)), ((1, W), F32, (1, ct), lambda j: (0, j))], fn)


def _plain(accs, sides):
    return accs


def kernel(x, ffn1_norm, ffn1_w_gate, ffn1_w_up, ffn1_w_down, mix_norm, w_in, ssm_lambda_re, ssm_lambda_im, ssm_log_dt, ssm_b_re, ssm_b_im, ssm_c_re, ssm_c_im, ssm_d, ssm_w_glu, ssm_b_glu, ssm_w_out, conv_w, conv_b, conv_w_out, w_o, ffn2_norm, ffn2_w_gate, ffn2_w_up, ffn2_w_down, final_norm, loss_target, m_ffn1_norm, m_ffn1_w_gate, m_ffn1_w_up, m_ffn1_w_down, m_mix_norm, m_w_in, m_ssm_lambda_re, m_ssm_lambda_im, m_ssm_log_dt, m_ssm_b_re, m_ssm_b_im, m_ssm_c_re, m_ssm_c_im, m_ssm_d, m_ssm_w_glu, m_ssm_b_glu, m_ssm_w_out, m_conv_w, m_conv_b, m_conv_w_out, m_w_o, m_ffn2_norm, m_ffn2_w_gate, m_ffn2_w_up, m_ffn2_w_down, m_final_norm, v_ffn1_norm, v_ffn1_w_gate, v_ffn1_w_up, v_ffn1_w_down, v_mix_norm, v_w_in, v_ssm_lambda_re, v_ssm_lambda_im, v_ssm_log_dt, v_ssm_b_re, v_ssm_b_im, v_ssm_c_re, v_ssm_c_im, v_ssm_d, v_ssm_w_glu, v_ssm_b_glu, v_ssm_w_out, v_conv_w, v_conv_b, v_conv_w_out, v_w_o, v_ffn2_norm, v_ffn2_w_gate, v_ffn2_w_up, v_ffn2_w_down, v_final_norm):
    P = dict(ffn1_norm=ffn1_norm, ffn1_w_gate=ffn1_w_gate, ffn1_w_up=ffn1_w_up, ffn1_w_down=ffn1_w_down, mix_norm=mix_norm, w_in=w_in, ssm_lambda_re=ssm_lambda_re, ssm_lambda_im=ssm_lambda_im, ssm_log_dt=ssm_log_dt, ssm_b_re=ssm_b_re, ssm_b_im=ssm_b_im, ssm_c_re=ssm_c_re, ssm_c_im=ssm_c_im, ssm_d=ssm_d, ssm_w_glu=ssm_w_glu, ssm_b_glu=ssm_b_glu, ssm_w_out=ssm_w_out, conv_w=conv_w, conv_b=conv_b, conv_w_out=conv_w_out, w_o=w_o, ffn2_norm=ffn2_norm, ffn2_w_gate=ffn2_w_gate, ffn2_w_up=ffn2_w_up, ffn2_w_down=ffn2_w_down, final_norm=final_norm)
    M = dict(ffn1_norm=m_ffn1_norm, ffn1_w_gate=m_ffn1_w_gate, ffn1_w_up=m_ffn1_w_up, ffn1_w_down=m_ffn1_w_down, mix_norm=m_mix_norm, w_in=m_w_in, ssm_lambda_re=m_ssm_lambda_re, ssm_lambda_im=m_ssm_lambda_im, ssm_log_dt=m_ssm_log_dt, ssm_b_re=m_ssm_b_re, ssm_b_im=m_ssm_b_im, ssm_c_re=m_ssm_c_re, ssm_c_im=m_ssm_c_im, ssm_d=m_ssm_d, ssm_w_glu=m_ssm_w_glu, ssm_b_glu=m_ssm_b_glu, ssm_w_out=m_ssm_w_out, conv_w=m_conv_w, conv_b=m_conv_b, conv_w_out=m_conv_w_out, w_o=m_w_o, ffn2_norm=m_ffn2_norm, ffn2_w_gate=m_ffn2_w_gate, ffn2_w_up=m_ffn2_w_up, ffn2_w_down=m_ffn2_w_down, final_norm=m_final_norm)
    V = dict(ffn1_norm=v_ffn1_norm, ffn1_w_gate=v_ffn1_w_gate, ffn1_w_up=v_ffn1_w_up, ffn1_w_down=v_ffn1_w_down, mix_norm=v_mix_norm, w_in=v_w_in, ssm_lambda_re=v_ssm_lambda_re, ssm_lambda_im=v_ssm_lambda_im, ssm_log_dt=v_ssm_log_dt, ssm_b_re=v_ssm_b_re, ssm_b_im=v_ssm_b_im, ssm_c_re=v_ssm_c_re, ssm_c_im=v_ssm_c_im, ssm_d=v_ssm_d, ssm_w_glu=v_ssm_w_glu, ssm_b_glu=v_ssm_b_glu, ssm_w_out=v_ssm_w_out, conv_w=v_conv_w, conv_b=v_conv_b, conv_w_out=v_conv_w_out, w_o=v_w_o, ffn2_norm=v_ffn2_norm, ffn2_w_gate=v_ffn2_w_gate, ffn2_w_up=v_ffn2_w_up, ffn2_w_down=v_ffn2_w_down, final_norm=v_final_norm)
    names = list(P)
    sharded = ["ffn1_w_gate", "ffn1_w_up", "ffn1_w_down", "w_in", "ssm_w_glu", "ssm_w_out", "conv_w_out", "w_o",
               "ffn2_w_gate", "ffn2_w_up", "ffn2_w_down"]
    replicated = [n for n in names if n not in sharded and n != "conv_w"]

    S, D = x.shape[1], x.shape[2]
    W = ssm_d.shape[0]
    Dc = D // NDEV
    G, N = ssm_lambda_re.shape
    rows = G * N // LANE
    xh = x.reshape(S, D)
    target = loss_target.reshape(S, D)
    xi, yi, ci = lax.axis_index("x"), lax.axis_index("y"), lax.axis_index("c")
    c_arr = jnp.reshape(ci, (1,)).astype(I32)
    q_arr = jnp.reshape(2 * xi + yi, (1,)).astype(I32)
    row = lambda v: v.reshape(1, -1)

    transposed = ("ffn1_w_gate", "ffn1_w_up", "ffn2_w_gate", "ffn2_w_up")
    local = lambda table, n: table[n].T if n in transposed else table[n]

    groups = [sharded[0:1], sharded[1:2], sharded[2:3], ["w_in", "conv_w"], sharded[4:8],
              sharded[8:9], sharded[9:10], sharded[10:11]]
    first_begun, second_begun, chain = {}, {}, []
    me = 4 * xi + 2 * yi + ci

    def gather_first(gi):
        hold = 0.0 * chain[0][0, 0] if chain else 0.0
        srcs = [conv_w + hold if n == "conv_w" else (local(P, n) + hold).astype(BF) for n in groups[gi]]
        lands = [lax.dynamic_update_slice(lax.empty((NDEV,) + s.shape, s.dtype), s[None], (me,) + (0,) * s.ndim)
                 for s in srcs]
        first_begun[gi] = _split_start("gather_first_start_%d" % gi, srcs, lands, _gather_first_copies, 3, chain[-1:])
        chain.append(first_begun[gi][4])

    def gather_second(gi, after):
        _, lands = _split_wait("gather_first_wait_%d" % gi, first_begun[gi], _gather_first_copies, after)
        second_begun[gi] = _split_start("gather_second_start_%d" % gi, [], lands, _gather_second_copies, 2, chain[-1:])
        chain.append(second_begun[gi][4])

    def gathered(gi, after):
        _, lands = _split_wait("gather_second_wait_%d" % gi, second_begun[gi], _gather_second_copies, after)
        return _gather_forward("gather_forward_%d" % gi, lands, chain[-1:])

    def fetch(gi, seconds, firsts):
        def get(after):
            for g in seconds:
                gather_second(g, after)
            for g in firsts:
                gather_first(g)
            return gathered(gi, after)
        return get

    tm = _pick(S, M_TILE, SUBLANE)
    th = _pick(S, M_TILE // 2, SUBLANE)
    tk = _pick(D, K_TILE, LANE)
    ts = _pick(S, K_TILE, SUBLANE)
    tn = _pick(D, 1024, LANE)

    gather_first(0)
    gather_first(1)
    u1, r1 = _rms_fwd("rms1", xh, row(ffn1_norm), deps=list(chain))

    hold = 0.0 * chain[0][0, 0]
    replicated = [n for n in replicated if n != "ffn1_norm"] + ["ffn1_norm"]
    tile = SUBLANE * LANE

    def as_rows(p):
        flat = p.reshape(-1).astype(F32)
        return jnp.pad(flat, (0, -flat.shape[0] % tile)).reshape(-1, LANE)

    def pack(parts):
        return jnp.concatenate([as_rows(p) for p in parts], axis=0)

    cw_zero = jnp.zeros((conv_w.shape[0], W), F32)
    packed_state = [pack([t[n] + hold for n in replicated[:-1]] + [cw_zero + fill, t["ffn1_norm"] + hold])
                    for t, fill in ((P, 0.0), (M, 0.0), (V, 1.0))]
    ssm_in = tuple(p + hold for p in (ssm_lambda_re, ssm_lambda_im, ssm_log_dt, ssm_b_re, ssm_b_im, ssm_c_re, ssm_c_im))
    (a_re, a_im, bd_re, bd_im, cd_re, cd_imn), ssm_vjp = jax.vjp(_ssm_params, *ssm_in)
    bd_re_b, bd_im_b, cd_re_b, cd_imn_b = (t.astype(BF) for t in (bd_re, bd_im, cd_re, cd_imn))
    gather_second(0, [u1, a_re, a_im, bd_re_b, bd_im_b, cd_re_b, cd_imn_b] + packed_state)
    gather_first(2)
    gather_second(1, u1)
    gather_first(3)
    (wg1,) = gathered(0, u1)
    h1, ffn1_saved, wu1, wd1 = _ffn_fwd("ffn1", u1, xh, wg1, lambda after: fetch(1, [2], [4])(after)[0],
                                        lambda after: fetch(2, [3], [5])(after)[0])
    u2, r2 = _rms_fwd("rms2", h1, row(mix_norm))
    w_in_f, cw_f = fetch(3, [4], [6])(u2)
    cw = jnp.transpose(cw_f, (1, 0, 2)).reshape(3, W)
    (proj,) = _mm(
        "in_proj", (NDEV, S // tm, 1, D // tk),
        [(u2, (tm, tk), lambda b, i, j, k: (i, k)), (w_in_f, (None, tk, W), lambda b, i, j, k: (b, k, 0))],
        [(0, 1, NN, 0)], [], [((NDEV, S, W), F32, (None, tm, W), lambda b, i, j, k: (b, i, 0))], _plain, [(tm, W)])

    v_f = proj[0]
    v_bf = v_f.astype(BF)
    bu_re3, bu_im3 = _ssm_spread("ssm_bu", v_bf, bd_re_b, bd_im_b, NN, rows)
    s_re3, s_im3 = _scan_fwd(bu_re3, bu_im3, a_re, a_im)

    def y0_epilogue(acc, v_tile, d_tile):
        y0 = acc + d_tile * v_tile
        return y0, _gelu(y0)

    y0, y1, s_re_b, s_im_b = _ssm_collect("ssm_y0", s_re3, s_im3, cd_re_b, cd_imn_b, NN, v_f, row(ssm_d),
                                          y0_epilogue, [F32, BF])

    tw = _pick(W, 512, LANE)
    w_glu_f, w_so, w_co, w_o_f = fetch(4, [5], [7])(y1)
    w_glu_f = w_glu_f.reshape(W, W)
    w_o_f = w_o_f.reshape(D, D)

    def glu_epilogue(accs, sides):
        q = accs[0] + sides[1]
        return q, _gelu(sides[0]) * _sigmoid(q)

    q_pre, y2 = _mm("ssm_glu", (1, S // tm, W // tw, 1),
                    [(y1, (tm, W), lambda b, i, j, k: (i, 0)), (w_glu_f, (W, tw), lambda b, i, j, k: (0, j))],
                    [(0, 1, NN, 0)],
                    [(y0, (tm, tw), lambda b, i, j, k: (i, j)), (row(ssm_b_glu), (1, tw), lambda b, i, j, k: (0, j))],
                    [((S, W), F32, (tm, tw), lambda b, i, j, k: (i, j)), ((S, W), BF, (tm, tw), lambda b, i, j, k: (i, j))],
                    glu_epilogue, [(tm, tw)])

    yb, conv = _conv_fwd(proj, cw, row(conv_b))

    per = W // Dc
    ga_blk = (proj, (None, tm, Dc), lambda b, i, j, k: (4 + b // per, i, b % per))
    gb_blk = (proj, (None, tm, Dc), lambda b, i, j, k: (6 + b // per, i, b % per))
    dc_out = ((S, D), BF, (tm, Dc), lambda b, i, j, k: (i, b))

    def merge_epilogue(accs, sides):
        za, zb = accs
        return _sigmoid(sides[0]) * za + _sigmoid(sides[1]) * zb, za, zb

    merged, z_a, z_b = _mm(
        "mix_merge", (NDEV, S // tm, 1, 1),
        [(y2, (tm, W), lambda b, i, j, k: (i, 0)), (yb, (tm, W), lambda b, i, j, k: (i, 0)),
         (w_so, (None, W, Dc), lambda b, i, j, k: (b, 0, 0)), (w_co, (None, W, Dc), lambda b, i, j, k: (b, 0, 0))],
        [(0, 2, NN, 0), (1, 3, NN, 1)], [ga_blk, gb_blk], [dc_out, dc_out, dc_out], merge_epilogue, [(tm, Dc)] * 2)

    (h2,) = _mm("mix_out", (1, S // tm, D // tn, D // tk),
                [(merged, (tm, tk), lambda b, i, j, k: (i, k)), (w_o_f, (tk, tn), lambda b, i, j, k: (k, j))],
                [(0, 1, NN, 0)], [(h1, (tm, tn), lambda b, i, j, k: (i, j))],
                [((S, D), F32, (tm, tn), lambda b, i, j, k: (i, j))],
                lambda accs, sides: [sides[0] + accs[0]], [(tm, tn)])

    u3, r3 = _rms_fwd("rms3", h2, row(ffn2_norm))
    (wg2,) = fetch(5, [6, 7], [])(u3)
    h3, ffn2_saved, wu2, wd2 = _ffn_fwd("ffn2", u3, h2, wg2, lambda after: gathered(6, after)[0],
                                        lambda after: gathered(7, after)[0])
    loss_vec, dh3, dh3_half, d_final_norm = _loss_head("loss_head", h3, row(final_norm), target)
    loss = lax.psum(loss_vec[0, 0], ("x", "y", "c"))
    loss_done = jnp.zeros((SUBLANE, LANE), F32) + loss

    grads, deltas, new_m, new_v = {}, {}, {}, {}

    def rs_sibling_start(tag, parts):
        lands = [lax.empty((4,) + p.shape[1:], p.dtype) for p in parts]
        return _split_start("rs_sibling_start_" + tag, parts, lands, _sibling_copies, 4)

    def rs_chips_start(tag, sibling_begun, after):
        parts, lands = _split_wait("rs_sibling_wait_" + tag, sibling_begun, _sibling_copies, after)
        sums = [_sum_sibling("rs_sum_%s_%d" % (tag, a), p, land, c_arr) for a, (p, land) in enumerate(zip(parts, lands))]
        lands2 = [lax.empty((3,) + sm.shape[1:], sm.dtype) for sm in sums]
        return _split_start("rs_chips_start_" + tag, sums, lands2, _chips_copies, 3)

    def rs_end(tag, group, begun, after):
        sums, lands2 = _split_wait("rs_chips_wait_" + tag, begun, _chips_copies, after)
        for n, sm, land2 in zip(group, sums, lands2):
            res = _finish_sharded("adamw_" + n, sm, land2, q_arr, local(P, n), local(M, n), local(V, n))
            grads[n], deltas[n], new_m[n], new_v[n] = [t.T if n in transposed else t for t in res]

    f2_dwd, f2_dwgu, f2_du = _ffn_bwd("ffn2", dh3_half, u3, ffn2_saved, wg2, wu2, wd2, deps=[loss_done])
    dwd2 = f2_dwd()
    dwg2, dwu2 = f2_dwgu()
    sib_ffn2 = rs_sibling_start("ffn2", [dwg2, dwu2, dwd2])
    du3 = f2_du(deps=[sib_ffn2[4]])
    dh2, dh2_b, d_ffn2_norm = _rms_bwd("rms3_bwd", du3, h2, r3, row(ffn2_norm), dh3, 1.0)
    rs_ffn2 = rs_chips_start("ffn2", sib_ffn2, dh2)

    dg_out = ((2, S, W), BF, (None, tm, Dc), lambda b, i, j, k: (j // per, i, j % per))
    ga_blk2 = (proj, (None, tm, Dc), lambda b, i, j, k: (4 + j // per, i, j % per))
    gb_blk2 = (proj, (None, tm, Dc), lambda b, i, j, k: (6 + j // per, i, j % per))
    dcj = lambda arr: (arr, (tm, Dc), lambda b, i, j, k: (i, j))
    dcj_out = ((S, D), BF, (tm, Dc), lambda b, i, j, k: (i, j))

    def dmerge_epilogue(accs, sides):
        dm = accs[0]
        sa, sb = _sigmoid(sides[0]), _sigmoid(sides[1])
        za, zb = sides[2].astype(F32), sides[3].astype(F32)
        return dm * sa, dm * sb, dm * za * sa * (1.0 - sa), dm * zb * sb * (1.0 - sb)

    dz_a, dz_b, dga, dgb = _mm(
        "mix_out_dx", (1, S // tm, NDEV, D // tk),
        [(dh2_b, (tm, tk), lambda b, i, j, k: (i, k)), (w_o_f, (Dc, tk), lambda b, i, j, k: (j, k))],
        [(0, 1, NT, 0)], [ga_blk2, gb_blk2, dcj(z_a), dcj(z_b)], [dcj_out, dcj_out, dg_out, dg_out],
        dmerge_epilogue, [(tm, Dc)], deps=[rs_ffn2[4]])

    td = _pick(D, M_TILE, LANE)
    (dw_o,) = _mm("mix_out_dw", (1, D // td, D // tn, S // ts),
                  [(merged, (ts, td), lambda b, i, j, k: (k, i)), (dh2_b, (ts, tn), lambda b, i, j, k: (k, j))],
                  [(0, 1, TN, 0)], [], [((D, D), BF, (td, tn), lambda b, i, j, k: (i, j))], _plain, [(td, tn)])

    wout = ((NDEV, W, Dc), BF, (None, W, Dc), lambda b, i, j, k: (b, 0, 0))
    dw_so, dw_co = _mm(
        "mix_merge_dw", (NDEV, 1, 1, S // ts),
        [(y2, (ts, W), lambda b, i, j, k: (k, 0)), (yb, (ts, W), lambda b, i, j, k: (k, 0)),
         (dz_a, (ts, Dc), lambda b, i, j, k: (k, b)), (dz_b, (ts, Dc), lambda b, i, j, k: (k, b))],
        [(0, 2, TN, 0), (1, 3, TN, 1)], [], [wout, wout], _plain, [(W, Dc)] * 2)

    def dglu_epilogue(accs, sides):
        dy2, dyb = accs
        sq = _sigmoid(sides[1])
        return dy2 * _gelu(sides[0]) * sq * (1.0 - sq), dy2 * sq, dyb

    full_w = lambda arr: (arr, (th, W), lambda b, i, j, k: (i, 0))
    full_w_out = lambda dt: ((S, W), dt, (th, W), lambda b, i, j, k: (i, 0))
    dq, dy1p, dyb = _mm(
        "mix_merge_dx", (1, S // th, 1, NDEV),
        [(dz_a, (th, Dc), lambda b, i, j, k: (i, k)), (dz_b, (th, Dc), lambda b, i, j, k: (i, k)),
         (w_so, (None, W, Dc), lambda b, i, j, k: (k, 0, 0)), (w_co, (None, W, Dc), lambda b, i, j, k: (k, 0, 0))],
        [(0, 2, NT, 0), (1, 3, NT, 1)], [full_w(y0), full_w(q_pre)], [full_w_out(BF), full_w_out(F32), full_w_out(F32)],
        dglu_epilogue, [(th, W)] * 2)

    def dy0_epilogue(accs, sides):
        dy0 = (sides[0] + accs[0]) * _gelu_grad(sides[1])
        return dy0, dy0

    wj = lambda arr: (arr, (tm, tw), lambda b, i, j, k: (i, j))
    dy0, dy0_b = _mm("ssm_glu_dx", (1, S // tm, W // tw, 1),
                     [(dq, (tm, W), lambda b, i, j, k: (i, 0)), (w_glu_f, (tw, W), lambda b, i, j, k: (j, 0))],
                     [(0, 1, NT, 0)], [wj(dy1p), wj(y0)],
                     [((S, W), F32, (tm, tw), lambda b, i, j, k: (i, j)), ((S, W), BF, (tm, tw), lambda b, i, j, k: (i, j))],
                     dy0_epilogue, [(tm, tw)])

    (dw_glu,) = _mm("ssm_glu_dw", (1, W // tw, 1, S // ts),
                    [(y1, (ts, tw), lambda b, i, j, k: (k, i)), (dq, (ts, W), lambda b, i, j, k: (k, 0))],
                    [(0, 1, TN, 0)], [], [((W, W), BF, (tw, W), lambda b, i, j, k: (i, 0))], _plain, [(tw, W)])

    tr = _pick(S, 256, SUBLANE)
    rw = ((tr, W), lambda i: (i, 0))
    vec_w = ((1, W), F32, (1, W), lambda i: (0, 0))
    d_b_glu, d_ssm_d = _ew(
        "ssm_colsums", (S // tr,), [(dq,) + rw, (dy0,) + rw, (proj, (None, tr, W), lambda i: (0, i, 0))],
        [vec_w, vec_w],
        lambda dqv, dyv, vv: (jnp.sum(dqv.astype(F32), axis=0, keepdims=True), jnp.sum(dyv * vv, axis=0, keepdims=True)),
        acc=(0, 1))

    ds_re3, ds_im3 = _ssm_spread("ssm_ds", dy0_b, cd_re_b, cd_imn_b, NT, rows)
    lam_re3, lam_im3, da_re, da_im = _scan_bwd(ds_re3, ds_im3, s_re3, s_im3, a_re, a_im)
    dv, lam_re_b, lam_im_b = _ssm_collect("ssm_dv", lam_re3, lam_im3, bd_re_b, bd_im_b, NT, dy0, row(ssm_d),
                                          lambda acc, dy_tile, d_tile: [acc + dy_tile * d_tile], [BF])
    tiles, tch, tst = bd_re.shape
    tok_ch = lambda arr: (arr, (ts, tch), lambda b, i, j, k: (k, b))
    tok_st = lambda arr: (arr, (ts, tst), lambda b, i, j, k: (k, b))
    bd_out = ((tiles, tch, tst), F32, (None, tch, tst), lambda b, i, j, k: (b, 0, 0))
    cd_out = ((tiles, tst, tch), F32, (None, tst, tch), lambda b, i, j, k: (b, 0, 0))
    dbd_re, dbd_im = _mm("ssm_dbd", (tiles, 1, 1, S // ts), [tok_ch(v_bf), tok_st(lam_re_b), tok_st(lam_im_b)],
                         [(0, 1, TN, 0), (0, 2, TN, 1)], [], [bd_out, bd_out], _plain, [(tch, tst)] * 2)
    dcd_re, dcd_imn = _mm("ssm_dcd", (tiles, 1, 1, S // ts), [tok_st(s_re_b), tok_st(s_im_b), tok_ch(dy0_b)],
                          [(0, 2, TN, 0), (1, 2, TN, 1)], [], [cd_out, cd_out], _plain, [(tst, tch)] * 2)
    d_ssm = ssm_vjp((da_re, da_im, dbd_re, dbd_im, dcd_re, dcd_imn))

    dbg, dcg, dval, d_conv_w_full, d_conv_b = _conv_bwd(dyb, proj, conv, cw)
    dproj = jnp.concatenate([dv[None], dbg[None], dcg[None], dval[None], dga, dgb], axis=0)

    (dw_in,) = _mm("in_proj_dw", (NDEV, D // td, 1, S // ts),
                   [(u2, (ts, td), lambda b, i, j, k: (k, i)), (dproj, (None, ts, W), lambda b, i, j, k: (b, k, 0))],
                   [(0, 1, TN, 0)], [], [((NDEV, D, W), BF, (None, td, W), lambda b, i, j, k: (b, i, 0))],
                   _plain, [(td, W)])
    sib_mixer = rs_sibling_start(
        "mixer", [dw_in, dw_glu.reshape(NDEV, W // NDEV, W), dw_so, dw_co, dw_o.reshape(NDEV, Dc, D)])
    (du2,) = _mm("in_proj_dx", (1, S // tm, D // tn, NDEV // K_SHARDS),
                 [(dproj, (K_SHARDS, tm, W), lambda b, i, j, k: (k, i, 0)),
                  (w_in_f, (K_SHARDS, tn, W), lambda b, i, j, k: (k, j, 0))],
                 [(0, 1, NT, 0)], [], [((S, D), F32, (tm, tn), lambda b, i, j, k: (i, j))], _plain, [(tm, tn)],
                 deps=[sib_mixer[4]])
    dh1, dh1_half, d_mix_norm = _rms_bwd("rms2_bwd", du2, h1, r2, row(mix_norm), dh2, 0.5)
    rs_mixer = rs_chips_start("mixer", sib_mixer, dh1)

    small = dict(mix_norm=d_mix_norm, ffn2_norm=d_ffn2_norm, final_norm=d_final_norm,
                 ssm_lambda_re=d_ssm[0], ssm_lambda_im=d_ssm[1], ssm_log_dt=d_ssm[2], ssm_b_re=d_ssm[3],
                 ssm_b_im=d_ssm[4], ssm_c_re=d_ssm[5], ssm_c_im=d_ssm[6], ssm_d=d_ssm_d, ssm_b_glu=d_b_glu,
                 conv_b=d_conv_b)
    early_pk = pack([small[n] for n in replicated[:-1]] + [d_conv_w_full])
    early_land = lax.dynamic_update_slice(jnp.zeros((NDEV,) + early_pk.shape, F32), early_pk[None], (me, 0, 0))
    small_begun = _split_start("gather_small_start", [early_pk], [early_land], _everyone_copies, NDEV - 1)

    f1_dwd, f1_dwgu, f1_du = _ffn_bwd("ffn1", dh1_half, u1, ffn1_saved, wg1, wu1, wd1,
                                      deps=[rs_mixer[4], small_begun[4]])
    du1 = f1_du()
    dx, _, d_ffn1_norm = _rms_bwd("rms1_bwd", du1, xh, r1, row(ffn1_norm), dh1, 1.0)
    dwd1 = f1_dwd(deps=[d_ffn1_norm])
    late = d_ffn1_norm + 0.0 * dwd1[0, :1, :1].astype(F32)
    late_pk = late.reshape(-1, LANE)
    late_land = lax.dynamic_update_slice(jnp.zeros((NDEV,) + late_pk.shape, F32), late_pk[None], (me, 0, 0))
    late_begun = _split_start("gather_late_start", [late_pk], [late_land], _everyone_copies, NDEV - 1)
    rs_ffn1_down = rs_chips_start("ffn1_down", rs_sibling_start("ffn1_down", [dwd1]), late_begun[4])
    (dwg1,) = f1_dwgu(deps=[rs_ffn1_down[4]], only=0)
    rs_ffn1_gate = rs_chips_start("ffn1_gate", rs_sibling_start("ffn1_gate", [dwg1]), None)
    (dwu1,) = f1_dwgu(deps=[rs_ffn1_gate[4]], only=1)
    rs_ffn1_up = rs_chips_start("ffn1_up", rs_sibling_start("ffn1_up", [dwu1]), None)
    rs_end("ffn2", sharded[8:11], rs_ffn2, rs_ffn1_up[4])
    rs_end("mixer", sharded[3:8], rs_mixer, [grads[n] for n in sharded[8:11]])
    _, (early_all,) = _split_wait("gather_small_wait", small_begun, _everyone_copies, [grads[n] for n in sharded[3:8]])
    _, (late_all,) = _split_wait("gather_late_wait", late_begun, _everyone_copies, early_all)
    small_all = jnp.concatenate([early_all, late_all], axis=1)
    g_pk, d_pk, m_pk, v_pk = _finish_replicated("adamw_replicated", small_all, *packed_state)

    def unpack(pk, r0, like):
        nr = -(-like.size // tile) * SUBLANE
        return pk[r0:r0 + nr].reshape(-1)[:like.size].reshape(like.shape), r0 + nr

    r0 = 0
    for n in replicated[:-1] + ["conv_w", "ffn1_norm"]:
        like = d_conv_w_full if n == "conv_w" else P[n]
        for store, pk in ((grads, g_pk), (deltas, d_pk), (new_m, m_pk), (new_v, v_pk)):
            store[n], r1 = unpack(pk, r0, like)
        r0 = r1
    g_cw_full = grads["conv_w"]
    cwl = conv_w.shape[1]
    g_cw = lax.dynamic_slice_in_dim(g_cw_full, me * cwl, cwl, axis=1)
    full3 = ((3, cwl), lambda i: (0, 0))
    grads["conv_w"], deltas["conv_w"], new_m["conv_w"], new_v["conv_w"] = _ew(
        "adamw_conv_w", (1,), [(g_cw,) + full3, (conv_w,) + full3, (m_conv_w,) + full3, (v_conv_w,) + full3],
        [((3, cwl), F32) + full3] * 4, lambda g, w, m, v: (g,) + _adamw(w, g, m, v))
    rs_end("ffn1_down", sharded[2:3], rs_ffn1_down, [g_pk, grads["conv_w"]])
    rs_end("ffn1_gate", sharded[0:1], rs_ffn1_gate, grads["ffn1_w_down"])
    rs_end("ffn1_up", sharded[1:2], rs_ffn1_up, grads["ffn1_w_gate"])

    return (loss, dx.reshape(x.shape), *[grads[n] for n in names], *[deltas[n] for n in names],
            *[new_m[n] for n in names], *[new_v[n] for n in names])
```

```python
import math

import jax
import jax.numpy as jnp
from jax import lax
from jax.experimental import pallas as pl
from jax.experimental.pallas import tpu as pltpu

F32 = jnp.float32
BF = jnp.bfloat16
I32 = jnp.int32
MESH = pl.DeviceIdType.MESH
LANE = 128
SUBLANE = 8
NDEV = 8
EW_BLOCK = 256 * 1024
M_TILE = 1024
K_TILE = 2048
K_SHARDS = 2
DMA_CHUNKS = 4
EPS = 1e-6
ADAM_LR, ADAM_B1, ADAM_B2, ADAM_EPS, ADAM_WD, ADAM_STEP = 0.001, 0.9, 0.999, 1e-08, 0.01, 10
NN = ((1,), (0,))
NT = ((1,), (1,))
TN = ((0,), (0,))
HBM = pl.BlockSpec(memory_space=pltpu.HBM)


def _pick(n, pref, mult):
    t = min(pref, n)
    t -= t % mult
    while t >= mult:
        if n % t == 0:
            return t
        t -= mult
    return n


def _sigmoid(x):
    return 1.0 / (1.0 + jnp.exp(-x))


_GELU_C = math.sqrt(2.0 / math.pi)


def _gelu(x):
    return 0.5 * x * (1.0 + jnp.tanh(_GELU_C * (x + 0.044715 * x * x * x)))


def _gelu_grad(x):
    t = jnp.tanh(_GELU_C * (x + 0.044715 * x * x * x))
    return 0.5 * (1.0 + t) + 0.5 * x * (1.0 - t * t) * _GELU_C * (1.0 + 3.0 * 0.044715 * x * x)


def _dep_specs(deps, rank):
    return [(d, d.shape, lambda *_, nd=d.ndim: (0,) * nd) for d in deps]


def _mm(name, grid, ops, pairs, sides, outs, epilogue, acc_shapes, deps=()):
    nk = grid[-1]
    n_ops, n_sides, n_outs = len(ops), len(sides), len(outs)
    dep_specs = _dep_specs(deps, len(grid))
    n_deps = len(dep_specs)

    def body(*refs):
        op_refs = refs[:n_ops]
        side_refs = refs[n_ops:n_ops + n_sides]
        out_refs = refs[n_ops + n_sides + n_deps:n_ops + n_sides + n_deps + n_outs]
        acc_refs = refs[n_ops + n_sides + n_deps + n_outs:]

        def partials():
            res = [None] * len(acc_shapes)
            for ia, ib, dims, ai in pairs:
                a_ref, b_ref = op_refs[ia], op_refs[ib]
                for s in range(a_ref.shape[0] if len(a_ref.shape) == 3 else 1):
                    a, b = (a_ref[s], b_ref[s]) if len(a_ref.shape) == 3 else (a_ref[...], b_ref[...])
                    p = lax.dot_general(a, b, (dims, ((), ())), preferred_element_type=F32)
                    res[ai] = p if res[ai] is None else res[ai] + p
            return res

        def finish(accs):
            vals = epilogue(accs, [s[...] for s in side_refs])
            for o, v in zip(out_refs, vals):
                o[...] = v.astype(o.dtype)

        if nk == 1:
            finish(partials())
        else:
            k = pl.program_id(len(grid) - 1)

            @pl.when(k == 0)
            def _():
                for a, p in zip(acc_refs, partials()):
                    a[...] = p

            @pl.when(k > 0)
            def _():
                for a, p in zip(acc_refs, partials()):
                    a[...] += p

            @pl.when(k == nk - 1)
            def _():
                finish([a[...] for a in acc_refs])

    return pl.pallas_call(
        body, name=name, grid=grid,
        in_specs=[pl.BlockSpec(b, m) for (_, b, m) in list(ops) + list(sides) + dep_specs],
        out_specs=[pl.BlockSpec(b, m) for (_, _, b, m) in outs],
        out_shape=[jax.ShapeDtypeStruct(s, d) for (s, d, _, _) in outs],
        scratch_shapes=[pltpu.VMEM(s, F32) for s in acc_shapes] if nk > 1 else [],
        compiler_params=pltpu.CompilerParams(
            dimension_semantics=("parallel",) * (len(grid) - 1) + ("arbitrary",)),
    )(*[a for (a, _, _) in list(ops) + list(sides) + dep_specs])


def _ew(name, grid, ins, outs, fn, acc=(), deps=()):
    n_in = len(ins)
    dep_specs = _dep_specs(deps, len(grid))

    def body(*refs):
        vals = fn(*[r[...] for r in refs[:n_in]])
        first = pl.program_id(0) == 0
        for idx, (o, v) in enumerate(zip(refs[n_in + len(dep_specs):], vals)):
            if idx in acc:
                @pl.when(first)
                def _(o=o, v=v):
                    o[...] = v.astype(o.dtype)

                @pl.when(jnp.logical_not(first))
                def _(o=o, v=v):
                    o[...] += v.astype(o.dtype)
            else:
                o[...] = v.astype(o.dtype)

    return pl.pallas_call(
        body, name=name, grid=grid,
        in_specs=[pl.BlockSpec(b, m) for (_, b, m) in list(ins) + dep_specs],
        out_specs=[pl.BlockSpec(b, m) for (_, _, b, m) in outs],
        out_shape=[jax.ShapeDtypeStruct(s, d) for (s, d, _, _) in outs],
        compiler_params=pltpu.CompilerParams(
            dimension_semantics=(("arbitrary",) if acc else ("parallel",)) * len(grid)),
    )(*[a for (a, _, _) in list(ins) + dep_specs])


def _position():
    x, y, c = lax.axis_index("x"), lax.axis_index("y"), lax.axis_index("c")
    chips = [(1 - x, y), (x, 1 - y), (1 - x, 1 - y)]
    return x, y, c, chips


SEM = pl.BlockSpec(memory_space=pltpu.SEMAPHORE)
EFFECT = pltpu.SideEffectType.DATAFLOW_SIDE_EFFECTING


def _in_hbm(v):
    return pltpu.with_memory_space_constraint(v, pltpu.HBM)


def _split_start(name, srcs, lands, make_copies, n_per, after=()):
    n, nb = len(srcs), len(srcs) + len(lands)
    n_sems = len(lands) * n_per
    after = list(after)

    def body(*refs):
        send_sems, recv_sems = refs[nb + len(after)], refs[nb + len(after) + 1]
        for cp in make_copies(refs[:n], refs[n:nb], send_sems, recv_sems):
            cp.start()
        refs[-1][...] = jnp.zeros_like(refs[-1])

    outs = pl.pallas_call(
        body, name=name,
        out_shape=(pltpu.SemaphoreType.DMA((n_sems,)), pltpu.SemaphoreType.DMA((n_sems,)),
                   *[pltpu.HBM(v.shape, v.dtype) for v in list(srcs) + list(lands)],
                   jax.ShapeDtypeStruct((SUBLANE, LANE), F32)),
        in_specs=[HBM] * nb + [pl.BlockSpec(memory_space=pl.ANY)] * len(after),
        out_specs=(SEM, SEM, *[HBM] * nb, pl.BlockSpec(memory_space=pltpu.VMEM)),
        input_output_aliases={i: 2 + i for i in range(nb)},
        compiler_params=pltpu.CompilerParams(has_side_effects=EFFECT),
    )(*[_in_hbm(v) for v in list(srcs) + list(lands)], *after)
    return outs[0], outs[1], list(outs[2:2 + n]), list(outs[2 + n:2 + nb]), outs[-1]


def _split_wait(name, started, make_copies, after):
    send_sems, recv_sems, srcs, lands, _ = started
    n, nb = len(srcs), len(srcs) + len(lands)

    def body(*refs):
        for cp in make_copies(refs[:n], refs[n:nb], refs[nb], refs[nb + 1]):
            cp.wait_send()
            cp.wait_recv()

    order = [] if after is None else list(after) if isinstance(after, (list, tuple)) else [after]
    outs = pl.pallas_call(
        body, name=name,
        out_shape=tuple(pltpu.HBM(v.shape, v.dtype) for v in srcs + lands),
        in_specs=[HBM] * nb + [SEM, SEM] + [pl.BlockSpec(memory_space=pl.ANY)] * len(order),
        out_specs=tuple([HBM] * nb),
        input_output_aliases={i: i for i in range(nb)},
        compiler_params=pltpu.CompilerParams(has_side_effects=EFFECT),
    )(*srcs, *lands, send_sems, recv_sems, *order)
    return list(outs[:n]), list(outs[n:])


def _gather_first_copies(xs, lands, send_sems, recv_sems):
    x, y, c, _ = _position()
    copies = []
    for a in range(len(xs)):
        for k, peer in enumerate([(x, y, 1 - c), (1 - x, y, c), (x, 1 - y, c)]):
            copies.append(pltpu.make_async_remote_copy(
                src_ref=xs[a], dst_ref=lands[a].at[4 * x + 2 * y + c],
                send_sem=send_sems.at[3 * a + k], recv_sem=recv_sems.at[3 * a + k], device_id=peer, device_id_type=MESH))
    return copies


def _gather_second_copies(xs, lands, send_sems, recv_sems):
    x, y, c, _ = _position()
    copies = []
    for a in range(len(lands)):
        rows = lands[a].shape[1]
        unit = SUBLANE * (4 // jnp.dtype(lands[a].dtype).itemsize)
        half = rows // 2 // unit * unit or rows
        parts = [((1 - x, y), (x, 1 - y), pl.ds(0, half))]
        if half < rows:
            parts.append(((x, 1 - y), (1 - x, y), pl.ds(half, rows - half)))
        for k, (block, to, rs) in enumerate(parts):
            ref = lands[a].at[4 * block[0] + 2 * block[1] + c, rs]
            copies.append(pltpu.make_async_remote_copy(
                src_ref=ref, dst_ref=ref, send_sem=send_sems.at[2 * a + k], recv_sem=recv_sems.at[2 * a + k],
                device_id=(*to, c), device_id_type=MESH))
    return copies


def _chips_copies(ps, lands, send_sems, recv_sems):
    x, y, c, chips = _position()
    copies = []
    for a in range(len(ps)):
        for j, chip in enumerate(chips):
            copies.append(pltpu.make_async_remote_copy(
                src_ref=ps[a].at[2 * chip[0] + chip[1]], dst_ref=lands[a].at[j],
                send_sem=send_sems.at[3 * a + j], recv_sem=recv_sems.at[3 * a + j], device_id=(*chip, c),
                device_id_type=MESH))
    return copies


def _sibling_copies(gs, lands, send_sems, recv_sems):
    x, y, c, _ = _position()
    copies = []
    for a in range(len(gs)):
        for q in range(4):
            copies.append(pltpu.make_async_remote_copy(
                src_ref=gs[a].at[2 * q + 1 - c], dst_ref=lands[a].at[q],
                send_sem=send_sems.at[4 * a + q], recv_sem=recv_sems.at[4 * a + q],
                device_id=(x, y, 1 - c), device_id_type=MESH))
    return copies


def _everyone_copies(xs, lands, send_sems, recv_sems):
    x, y, c, _ = _position()
    flip = lambda v, bit: 1 - v if bit else v
    copies = []
    for a in range(len(xs)):
        for k in range(1, NDEV):
            copies.append(pltpu.make_async_remote_copy(
                src_ref=xs[a], dst_ref=lands[a].at[4 * x + 2 * y + c],
                send_sem=send_sems.at[7 * a + k - 1], recv_sem=recv_sems.at[7 * a + k - 1],
                device_id=(flip(x, k & 4), flip(y, k & 2), flip(c, k & 1)), device_id_type=MESH))
    return copies


def _row_chunks(rows, dtype):
    unit = SUBLANE * (4 // jnp.dtype(dtype).itemsize)
    units = rows // unit
    if rows % unit or units < 2:
        return [(0, rows)]
    k = min(DMA_CHUNKS, units)
    sizes = [(units // k + (1 if i < units % k else 0)) * unit for i in range(k)]
    return [(sum(sizes[:i]), sz) for i, sz in enumerate(sizes)]


def _gather_forward(name, lands, after=()):
    n = len(lands)
    after = list(after)

    def body(*refs):
        ins, outs = refs[:n], refs[n + len(after):2 * n + len(after)]
        send_sems, recv_sems = refs[2 * n + len(after):]
        x, y, c, chips = _position()
        whole, chunks = [], []
        for a in range(n):
            rows = _row_chunks(ins[a].shape[1], ins[a].dtype)
            for j, chip in enumerate(chips):
                slot = 4 * chip[0] + 2 * chip[1]

                def to_sibling(src, dst):
                    return pltpu.make_async_remote_copy(
                        src_ref=src, dst_ref=dst, send_sem=send_sems.at[a, j], recv_sem=recv_sems.at[a, j],
                        device_id=(x, y, 1 - c), device_id_type=MESH)

                whole.append(to_sibling(ins[a].at[slot + c], outs[a].at[slot + 1 - c]))
                chunks += [to_sibling(ins[a].at[slot + c, pl.ds(r0, nr)], outs[a].at[slot + c, pl.ds(r0, nr)])
                           for r0, nr in rows]
        for cp in chunks:
            cp.start()
        for cp in whole:
            cp.wait()

    return pl.pallas_call(
        body, name=name,
        out_shape=[jax.ShapeDtypeStruct(l.shape, l.dtype) for l in lands],
        in_specs=[HBM] * n + [pl.BlockSpec(memory_space=pl.ANY)] * len(after), out_specs=[HBM] * n,
        input_output_aliases={a: a for a in range(n)},
        scratch_shapes=[pltpu.SemaphoreType.DMA((n, 3)), pltpu.SemaphoreType.DMA((n, 3))],
    )(*lands, *after)


def _sum_sibling(name, g, land, c_arr):
    _, R, C = g.shape
    tr = _pick(R, 512, SUBLANE)

    def body(c_ref, g_ref, l_ref, o_ref):
        o_ref[...] = (g_ref[...].astype(F32) + l_ref[...].astype(F32)).astype(o_ref.dtype)

    return pl.pallas_call(
        body, name=name,
        grid_spec=pltpu.PrefetchScalarGridSpec(
            num_scalar_prefetch=1, grid=(4, R // tr),
            in_specs=[pl.BlockSpec((None, tr, C), lambda q, i, cr: (2 * q + cr[0], i, 0)),
                      pl.BlockSpec((None, tr, C), lambda q, i, cr: (q, i, 0))],
            out_specs=pl.BlockSpec((None, tr, C), lambda q, i, cr: (q, i, 0))),
        out_shape=jax.ShapeDtypeStruct((4, R, C), g.dtype),
        compiler_params=pltpu.CompilerParams(dimension_semantics=("parallel", "parallel")),
    )(c_arr, g, land)


def _adamw(w, g, m, v):
    m = ADAM_B1 * m + (1.0 - ADAM_B1) * g
    v = ADAM_B2 * v + (1.0 - ADAM_B2) * (g * g)
    m_hat = m / (1.0 - ADAM_B1 ** ADAM_STEP)
    v_hat = v / (1.0 - ADAM_B2 ** ADAM_STEP)
    delta = -ADAM_LR * (m_hat / (jnp.sqrt(v_hat) + ADAM_EPS) + ADAM_WD * w)
    return delta, m, v


def _finish_sharded(name, sums, land, q_arr, w, m, v):
    R, C = w.shape
    tr = _pick(R, 512, SUBLANE)
    tc = _pick(C, max(LANE, EW_BLOCK // tr), LANE)

    def body(q_ref, p_ref, l_ref, w_ref, m_ref, v_ref, g_out, d_out, m_out, v_out):
        g = p_ref[...].astype(F32)
        for j in range(3):
            g = g + l_ref[j].astype(F32)
        d, mn, vn = _adamw(w_ref[...], g, m_ref[...], v_ref[...])
        g_out[...] = g
        d_out[...] = d
        m_out[...] = mn
        v_out[...] = vn

    blk = pl.BlockSpec((tr, tc), lambda i, j, qr: (i, j))
    return pl.pallas_call(
        body, name=name,
        grid_spec=pltpu.PrefetchScalarGridSpec(
            num_scalar_prefetch=1, grid=(R // tr, C // tc),
            in_specs=[pl.BlockSpec((None, tr, tc), lambda i, j, qr: (qr[0], i, j)),
                      pl.BlockSpec((3, tr, tc), lambda i, j, qr: (0, i, j)), blk, blk, blk],
            out_specs=[blk] * 4),
        out_shape=[jax.ShapeDtypeStruct((R, C), F32)] * 4,
        compiler_params=pltpu.CompilerParams(dimension_semantics=("parallel", "parallel")),
    )(q_arr, sums, land, w, m, v)


def _finish_replicated(name, gathered, w, m, v):
    _, R, C = gathered.shape
    tr = _pick(R, 256, SUBLANE)

    def fn(gv, wv, mv, vv):
        g = gv[0]
        for d in range(1, NDEV):
            g = g + gv[d]
        dl, mn, vn = _adamw(wv, g, mv, vv)
        return g, dl, mn, vn

    row = ((tr, C), lambda i: (i, 0))
    return _ew(name, (R // tr,),
               [(gathered, (NDEV, tr, C), lambda i: (0, i, 0)), (w,) + row, (m,) + row, (v,) + row],
               [((R, C), F32) + row] * 4, fn)


def _rms_fwd(name, h, g, deps=()):
    S, D = h.shape
    tr = _pick(S, 256, SUBLANE)

    def fn(hv, gv):
        r = lax.rsqrt(jnp.mean(hv * hv, axis=-1, keepdims=True) + EPS)
        return hv * r * gv, r

    return _ew(name, (S // tr,),
               [(h, (tr, D), lambda i: (i, 0)), (g, (1, D), lambda i: (0, 0))],
               [((S, D), BF, (tr, D), lambda i: (i, 0)), ((S, 1), F32, (tr, 1), lambda i: (i, 0))], fn, deps=deps)


def _rms_bwd(name, du, h, r, g, dres, scale):
    S, D = h.shape
    tr = _pick(S, 256, SUBLANE)

    def fn(duv, hv, rv, gv, drv):
        xn = hv * rv
        dxn = duv * gv
        dh = drv + rv * (dxn - xn * jnp.mean(dxn * xn, axis=-1, keepdims=True))
        return dh, scale * dh, jnp.sum(duv * xn, axis=0, keepdims=True)

    row = ((tr, D), lambda i: (i, 0))
    return _ew(name, (S // tr,),
               [(du,) + row, (h,) + row, (r, (tr, 1), lambda i: (i, 0)), (g, (1, D), lambda i: (0, 0)), (dres,) + row],
               [((S, D), F32) + row, ((S, D), BF) + row, ((1, D), F32, (1, D), lambda i: (0, 0))], fn, acc=(2,))


def _loss_head(name, h, g, target):
    S, D = h.shape
    tr = _pick(S, 256, SUBLANE)

    def fn(hv, gv, tv):
        r = lax.rsqrt(jnp.mean(hv * hv, axis=-1, keepdims=True) + EPS)
        xn = hv * r
        diff = xn * gv - tv
        loss = 0.5 * jnp.sum(jnp.mean(diff * diff, axis=-1, keepdims=True))
        dout = diff / D
        dxn = dout * gv
        dh = r * (dxn - xn * jnp.mean(dxn * xn, axis=-1, keepdims=True))
        return (jnp.zeros((1, LANE), F32) + loss, dh, 0.5 * dh, jnp.sum(dout * xn, axis=0, keepdims=True))

    row = ((tr, D), lambda i: (i, 0))
    return _ew(name, (S // tr,),
               [(h,) + row, (g, (1, D), lambda i: (0, 0)), (target,) + row],
               [((1, LANE), F32, (1, LANE), lambda i: (0, 0)), ((S, D), F32) + row, ((S, D), BF) + row,
                ((1, D), F32, (1, D), lambda i: (0, 0))], fn, acc=(0, 3))


def _ffn_fwd(tag, u, h, wg, fetch_wu, fetch_wd):
    S, D = u.shape
    Fs = wg.shape[1]
    tm, tk = _pick(S, M_TILE, SUBLANE), _pick(D, K_TILE, LANE)
    act = ((NDEV, S, Fs), BF, (None, tm, Fs), lambda b, i, j, k: (b, i, 0))
    lhs = (u, (tm, tk), lambda b, i, j, k: (i, k))
    rhs = lambda w: (w, (None, Fs, tk), lambda b, i, j, k: (b, 0, k))
    (gt,) = _mm(tag + "_gate", (NDEV, S // tm, 1, D // tk), [lhs, rhs(wg)], [(0, 1, NT, 0)], [], [act],
                lambda accs, sides: accs, [(tm, Fs)])
    wu = fetch_wu(gt)

    def up_epilogue(accs, sides):
        g = sides[0].astype(F32)
        return accs[0], g * _sigmoid(g) * accs[0]

    up, a = _mm(tag + "_up", (NDEV, S // tm, 1, D // tk), [lhs, rhs(wu)], [(0, 1, NT, 0)],
                [(gt, (None, tm, Fs), lambda b, i, j, k: (b, i, 0))], [act, act], up_epilogue, [(tm, Fs)])
    wd = fetch_wd(a)
    tn = _pick(D, 1024, LANE)
    (hn,) = _mm(
        tag + "_down", (1, S // tm, D // tn, NDEV // K_SHARDS),
        [(a, (K_SHARDS, tm, Fs), lambda b, i, j, k: (k, i, 0)), (wd, (K_SHARDS, Fs, tn), lambda b, i, j, k: (k, 0, j))],
        [(0, 1, NN, 0)], [(h, (tm, tn), lambda b, i, j, k: (i, j))],
        [((S, D), F32, (tm, tn), lambda b, i, j, k: (i, j))],
        lambda accs, sides: [sides[0] + 0.5 * accs[0]], [(tm, tn)])
    return hn, (gt, up, a), wu, wd


def _ffn_bwd(tag, dhs, u, saved, wg, wu, wd, deps=()):
    gt, up, a = saved
    S, D = u.shape
    Fs = wg.shape[1]
    tm, tk = _pick(S, M_TILE, SUBLANE), _pick(D, K_TILE, LANE)
    act_in = lambda arr: (arr, (None, tm, Fs), lambda b, i, j, k: (b, i, 0))
    act_out = ((NDEV, S, Fs), BF, (None, tm, Fs), lambda b, i, j, k: (b, i, 0))

    def act_epilogue(accs, sides):
        da = accs[0]
        gtv, upv = sides[0].astype(F32), sides[1].astype(F32)
        sg = _sigmoid(gtv)
        return da * upv * sg * (1.0 + gtv * (1.0 - sg)), da * gtv * sg

    dgt, dup = _mm(
        tag + "_dact", (NDEV, S // tm, 1, D // tk),
        [(dhs, (tm, tk), lambda b, i, j, k: (i, k)), (wd, (None, Fs, tk), lambda b, i, j, k: (b, 0, k))],
        [(0, 1, NT, 0)], [act_in(gt), act_in(up)], [act_out, act_out], act_epilogue, [(tm, Fs)], deps=deps)

    ts = _pick(S, K_TILE, SUBLANE)
    tn = _pick(D, 1024, LANE)
    wgrad = ((NDEV, Fs, D), BF, (None, Fs, tn), lambda b, i, j, k: (b, 0, j))
    tok = lambda arr: (arr, (None, ts, Fs), lambda b, i, j, k: (b, k, 0))

    def grad_down(deps=()):
        return _mm(
            tag + "_dwd", (NDEV, 1, D // tn, S // ts),
            [tok(a), (dhs, (ts, tn), lambda b, i, j, k: (k, j))],
            [(0, 1, TN, 0)], [], [wgrad], lambda accs, sides: accs, [(Fs, tn)], deps=deps)[0]

    def grad_gate_up(deps=()):
        return _mm(
            tag + "_dwgu", (NDEV, 1, D // tn, S // ts),
            [tok(dgt), tok(dup), (u, (ts, tn), lambda b, i, j, k: (k, j))],
            [(0, 2, TN, 0), (1, 2, TN, 1)], [], [wgrad, wgrad], lambda accs, sides: accs, [(Fs, tn), (Fs, tn)],
            deps=deps)

    def du(deps=()):
        return _mm(
            tag + "_du", (1, S // tm, D // tn, NDEV // K_SHARDS),
            [(dgt, (K_SHARDS, tm, Fs), lambda b, i, j, k: (k, i, 0)), (dup, (K_SHARDS, tm, Fs), lambda b, i, j, k: (k, i, 0)),
             (wg, (K_SHARDS, Fs, tn), lambda b, i, j, k: (k, 0, j)), (wu, (K_SHARDS, Fs, tn), lambda b, i, j, k: (k, 0, j))],
            [(0, 2, NN, 0), (1, 3, NN, 0)], [], [((S, D), F32, (tm, tn), lambda b, i, j, k: (i, j))],
            lambda accs, sides: accs, [(tm, tn)], deps=deps)[0]

    return grad_down, grad_gate_up, du


def _ssm_params(lam_re, lam_im, log_dt, b_re, b_im, c_re, c_im):
    G, N = lam_re.shape
    C = b_re.shape[2]
    lam_re = jnp.minimum(lam_re, -1e-4)
    dt = jnp.exp(log_dt)[:, None]
    mag = jnp.exp(lam_re * dt)
    a_re = mag * jnp.cos(lam_im * dt)
    a_im = mag * jnp.sin(lam_im * dt)
    den = lam_re * lam_re + lam_im * lam_im
    p = a_re - 1.0
    f_re = ((p * lam_re + a_im * lam_im) / den)[:, :, None]
    f_im = ((a_im * lam_re - p * lam_im) / den)[:, :, None]
    bb_re = f_re * b_re - f_im * b_im
    bb_im = f_re * b_im + f_im * b_re
    gpt = LANE // C
    tiles = G // gpt
    eye = jnp.eye(gpt, dtype=F32)

    def bd(bb):
        return jnp.einsum("bgnc,gh->bgchn", bb.reshape(tiles, gpt, N, C), eye).reshape(tiles, gpt * C, gpt * N)

    def cd(cc):
        return jnp.einsum("bgcn,gh->bgnhc", cc.reshape(tiles, gpt, C, N), eye).reshape(tiles, gpt * N, gpt * C)

    rows = G * N // LANE
    return (a_re.reshape(rows, LANE), a_im.reshape(rows, LANE), bd(bb_re), bd(bb_im), cd(c_re), cd(-c_im))


def _tile_states(ref3, b, per):
    return jnp.concatenate([ref3[:, per * b + r, :] for r in range(per)], axis=1).astype(BF)


def _ssm_spread(name, x, m_re, m_im, dims, rows):
    S, W = x.shape
    tiles = m_re.shape[0]
    tch, per = W // tiles, rows // tiles
    tq = _pick(S, 256, SUBLANE)

    def body(x_ref, mre_ref, mim_ref, ore_ref, oim_ref):
        for b in range(tiles):
            xb = x_ref[:, b * tch:(b + 1) * tch]
            for m_ref, o_ref in ((mre_ref, ore_ref), (mim_ref, oim_ref)):
                val = lax.dot_general(xb, m_ref[b], (dims, ((), ())), preferred_element_type=F32)
                for r in range(per):
                    o_ref[:, per * b + r, :] = val[:, r * LANE:(r + 1) * LANE]

    whole = lambda m: pl.BlockSpec(m.shape, lambda i: (0, 0, 0))
    st = pl.BlockSpec((tq, rows, LANE), lambda i: (i, 0, 0))
    return pl.pallas_call(
        body, name=name, grid=(S // tq,),
        in_specs=[pl.BlockSpec((tq, W), lambda i: (i, 0)), whole(m_re), whole(m_im)], out_specs=[st, st],
        out_shape=[jax.ShapeDtypeStruct((S, rows, LANE), F32)] * 2,
        compiler_params=pltpu.CompilerParams(dimension_semantics=("parallel",)),
    )(x, m_re, m_im)


def _ssm_collect(name, z_re3, z_im3, m_re, m_im, dims, side, gain, epilogue, out_dtypes):
    S, rows, _ = z_re3.shape
    tiles = m_re.shape[0]
    W = side.shape[1]
    tch, per = W // tiles, rows // tiles
    tq = _pick(S, 256, SUBLANE)
    n_out = len(out_dtypes)

    def body(zre_ref, zim_ref, mre_ref, mim_ref, side_ref, gain_ref, *out_refs):
        for b in range(tiles):
            cols = slice(b * tch, (b + 1) * tch)
            zre, zim = _tile_states(zre_ref, b, per), _tile_states(zim_ref, b, per)
            acc = lax.dot_general(zre, mre_ref[b], (dims, ((), ())), preferred_element_type=F32)
            acc = acc + lax.dot_general(zim, mim_ref[b], (dims, ((), ())), preferred_element_type=F32)
            for o, v in zip(out_refs[:n_out], epilogue(acc, side_ref[:, cols], gain_ref[:, cols])):
                o[:, cols] = v.astype(o.dtype)
            out_refs[n_out][:, b * per * LANE:(b + 1) * per * LANE] = zre
            out_refs[n_out + 1][:, b * per * LANE:(b + 1) * per * LANE] = zim

    whole = lambda m: pl.BlockSpec(m.shape, lambda i: (0, 0, 0))
    st = pl.BlockSpec((tq, rows, LANE), lambda i: (i, 0, 0))
    ch = pl.BlockSpec((tq, W), lambda i: (i, 0))
    flat = pl.BlockSpec((tq, rows * LANE), lambda i: (i, 0))
    return pl.pallas_call(
        body, name=name, grid=(S // tq,),
        in_specs=[st, st, whole(m_re), whole(m_im), ch, pl.BlockSpec((1, W), lambda i: (0, 0))],
        out_specs=[ch] * n_out + [flat, flat],
        out_shape=[jax.ShapeDtypeStruct((S, W), dt) for dt in out_dtypes]
        + [jax.ShapeDtypeStruct((S, rows * LANE), BF)] * 2,
        compiler_params=pltpu.CompilerParams(dimension_semantics=("parallel",)),
    )(z_re3, z_im3, m_re, m_im, side, gain)


def _scan_fwd(bu_re, bu_im, a_re, a_im):
    S, R, _ = bu_re.shape
    tc = _pick(S, 256, SUBLANE)

    def body(bre, bim, are, aim, sre, sim, carry):
        @pl.when(pl.program_id(0) == 0)
        def _():
            carry[...] = jnp.zeros_like(carry)

        ar, ai = are[...], aim[...]

        def step(t, c):
            pr, pi = c
            nr = ar * pr - ai * pi + bre[t]
            ni = ar * pi + ai * pr + bim[t]
            sre[t] = nr
            sim[t] = ni
            return nr, ni

        pr, pi = lax.fori_loop(0, tc, step, (carry[0], carry[1]), unroll=8)
        carry[0] = pr
        carry[1] = pi

    blk = pl.BlockSpec((tc, R, LANE), lambda i: (i, 0, 0))
    par = pl.BlockSpec((R, LANE), lambda i: (0, 0))
    return pl.pallas_call(
        body, name="ssm_scan_fwd", grid=(S // tc,),
        in_specs=[blk, blk, par, par], out_specs=[blk, blk],
        out_shape=[jax.ShapeDtypeStruct((S, R, LANE), F32)] * 2,
        scratch_shapes=[pltpu.VMEM((2, R, LANE), F32)],
        compiler_params=pltpu.CompilerParams(dimension_semantics=("arbitrary",)),
    )(bu_re, bu_im, a_re, a_im)


def _scan_bwd(ds_re, ds_im, s_re, s_im, a_re, a_im):
    S, R, _ = ds_re.shape
    tc = _pick(S, 256, SUBLANE)
    nc = S // tc

    def body(dre, dim_, sre, sim, are, aim, lre, lim, dar, dai, carry):
        @pl.when(pl.program_id(0) == 0)
        def _():
            carry[...] = jnp.zeros_like(carry)
            dar[...] = jnp.zeros_like(dar)
            dai[...] = jnp.zeros_like(dai)

        ar, ai = are[...], aim[...]

        def step(tt, c):
            t = tc - 1 - tt
            lr, li, gr, gi = c
            sr, si = sre[t], sim[t]
            gr = gr + lr * sr + li * si
            gi = gi + li * sr - lr * si
            nlr = dre[t] + ar * lr + ai * li
            nli = dim_[t] + ar * li - ai * lr
            lre[t] = nlr
            lim[t] = nli
            return nlr, nli, gr, gi

        lr, li, gr, gi = lax.fori_loop(0, tc, step, (carry[0], carry[1], dar[...], dai[...]), unroll=8)
        carry[0] = lr
        carry[1] = li
        dar[...] = gr
        dai[...] = gi

    blk = pl.BlockSpec((tc, R, LANE), lambda i: (nc - 1 - i, 0, 0))
    par = pl.BlockSpec((R, LANE), lambda i: (0, 0))
    return pl.pallas_call(
        body, name="ssm_scan_bwd", grid=(nc,),
        in_specs=[blk, blk, blk, blk, par, par], out_specs=[blk, blk, par, par],
        out_shape=[jax.ShapeDtypeStruct((S, R, LANE), F32)] * 2 + [jax.ShapeDtypeStruct((R, LANE), F32)] * 2,
        scratch_shapes=[pltpu.VMEM((2, R, LANE), F32)],
        compiler_params=pltpu.CompilerParams(dimension_semantics=("arbitrary",)),
    )(ds_re, ds_im, s_re, s_im, a_re, a_im)


def _shift_down(z, k):
    t = lax.broadcasted_iota(I32, z.shape, 0)
    return jnp.where(t >= k, pltpu.roll(z, k, 0), 0.0)


def _shift_up(z, k):
    n = z.shape[0]
    t = lax.broadcasted_iota(I32, z.shape, 0)
    return jnp.where(t < n - k, pltpu.roll(z, n - k, 0), 0.0)


def _conv_fwd(proj, cw, cb):
    _, S, W = proj.shape
    ct = _pick(W, 256, LANE)

    def fn(bg, cg, val, w, b):
        z = cg * val
        conv = b + w[0:1] * _shift_down(z, 2) + w[1:2] * _shift_down(z, 1) + w[2:3] * z
        return bg * conv, conv

    sl = lambda s: (proj, (None, S, ct), lambda j, s=s: (s, 0, j))
    col = ((S, ct), lambda j: (0, j))
    return _ew("conv_fwd", (W // ct,),
               [sl(1), sl(2), sl(3), (cw, (3, ct), lambda j: (0, j)), (cb, (1, ct), lambda j: (0, j))],
               [((S, W), BF) + col, ((S, W), F32) + col], fn)


def _conv_bwd(dyb, proj, conv, cw):
    _, S, W = proj.shape
    ct = _pick(W, 256, LANE)

    def fn(dy, bg, cg, val, cv, w):
        z = cg * val
        z1, z2 = _shift_down(z, 1), _shift_down(z, 2)
        dconv = dy * bg
        dz = w[2:3] * dconv + w[1:2] * _shift_up(dconv, 1) + w[0:1] * _shift_up(dconv, 2)
        dw = jnp.concatenate([jnp.sum(dconv * z2, axis=0, keepdims=True), jnp.sum(dconv * z1, axis=0, keepdims=True),
                              jnp.sum(dconv * z, axis=0, keepdims=True)], axis=0)
        return dy * cv, dz * val, dz * cg, dw, jnp.sum(dconv, axis=0, keepdims=True)

    sl = lambda s: (proj, (None, S, ct), lambda j, s=s: (s, 0, j))
    col = ((S, ct), lambda j: (0, j))
    return _ew("conv_bwd", (W // ct,),
               [(dyb,) + col, sl(1), sl(2), sl(3), (conv,) + col, (cw, (3, ct), lambda j: (0, j))],
               [((S, W), BF) + col, ((S, W), BF) + col, ((S, W), BF) + col,
                ((3, W), F32, (3, ct), lambda j: (0, j)), ((1, W), F32, (1, ct), lambda j: (0, j))], fn)


def _plain(accs, sides):
    return accs


def kernel(x, ffn1_norm, ffn1_w_gate, ffn1_w_up, ffn1_w_down, mix_norm, w_in, ssm_lambda_re, ssm_lambda_im, ssm_log_dt, ssm_b_re, ssm_b_im, ssm_c_re, ssm_c_im, ssm_d, ssm_w_glu, ssm_b_glu, ssm_w_out, conv_w, conv_b, conv_w_out, w_o, ffn2_norm, ffn2_w_gate, ffn2_w_up, ffn2_w_down, final_norm, loss_target, m_ffn1_norm, m_ffn1_w_gate, m_ffn1_w_up, m_ffn1_w_down, m_mix_norm, m_w_in, m_ssm_lambda_re, m_ssm_lambda_im, m_ssm_log_dt, m_ssm_b_re, m_ssm_b_im, m_ssm_c_re, m_ssm_c_im, m_ssm_d, m_ssm_w_glu, m_ssm_b_glu, m_ssm_w_out, m_conv_w, m_conv_b, m_conv_w_out, m_w_o, m_ffn2_norm, m_ffn2_w_gate, m_ffn2_w_up, m_ffn2_w_down, m_final_norm, v_ffn1_norm, v_ffn1_w_gate, v_ffn1_w_up, v_ffn1_w_down, v_mix_norm, v_w_in, v_ssm_lambda_re, v_ssm_lambda_im, v_ssm_log_dt, v_ssm_b_re, v_ssm_b_im, v_ssm_c_re, v_ssm_c_im, v_ssm_d, v_ssm_w_glu, v_ssm_b_glu, v_ssm_w_out, v_conv_w, v_conv_b, v_conv_w_out, v_w_o, v_ffn2_norm, v_ffn2_w_gate, v_ffn2_w_up, v_ffn2_w_down, v_final_norm):
    P = dict(ffn1_norm=ffn1_norm, ffn1_w_gate=ffn1_w_gate, ffn1_w_up=ffn1_w_up, ffn1_w_down=ffn1_w_down, mix_norm=mix_norm, w_in=w_in, ssm_lambda_re=ssm_lambda_re, ssm_lambda_im=ssm_lambda_im, ssm_log_dt=ssm_log_dt, ssm_b_re=ssm_b_re, ssm_b_im=ssm_b_im, ssm_c_re=ssm_c_re, ssm_c_im=ssm_c_im, ssm_d=ssm_d, ssm_w_glu=ssm_w_glu, ssm_b_glu=ssm_b_glu, ssm_w_out=ssm_w_out, conv_w=conv_w, conv_b=conv_b, conv_w_out=conv_w_out, w_o=w_o, ffn2_norm=ffn2_norm, ffn2_w_gate=ffn2_w_gate, ffn2_w_up=ffn2_w_up, ffn2_w_down=ffn2_w_down, final_norm=final_norm)
    M = dict(ffn1_norm=m_ffn1_norm, ffn1_w_gate=m_ffn1_w_gate, ffn1_w_up=m_ffn1_w_up, ffn1_w_down=m_ffn1_w_down, mix_norm=m_mix_norm, w_in=m_w_in, ssm_lambda_re=m_ssm_lambda_re, ssm_lambda_im=m_ssm_lambda_im, ssm_log_dt=m_ssm_log_dt, ssm_b_re=m_ssm_b_re, ssm_b_im=m_ssm_b_im, ssm_c_re=m_ssm_c_re, ssm_c_im=m_ssm_c_im, ssm_d=m_ssm_d, ssm_w_glu=m_ssm_w_glu, ssm_b_glu=m_ssm_b_glu, ssm_w_out=m_ssm_w_out, conv_w=m_conv_w, conv_b=m_conv_b, conv_w_out=m_conv_w_out, w_o=m_w_o, ffn2_norm=m_ffn2_norm, ffn2_w_gate=m_ffn2_w_gate, ffn2_w_up=m_ffn2_w_up, ffn2_w_down=m_ffn2_w_down, final_norm=m_final_norm)
    V = dict(ffn1_norm=v_ffn1_norm, ffn1_w_gate=v_ffn1_w_gate, ffn1_w_up=v_ffn1_w_up, ffn1_w_down=v_ffn1_w_down, mix_norm=v_mix_norm, w_in=v_w_in, ssm_lambda_re=v_ssm_lambda_re, ssm_lambda_im=v_ssm_lambda_im, ssm_log_dt=v_ssm_log_dt, ssm_b_re=v_ssm_b_re, ssm_b_im=v_ssm_b_im, ssm_c_re=v_ssm_c_re, ssm_c_im=v_ssm_c_im, ssm_d=v_ssm_d, ssm_w_glu=v_ssm_w_glu, ssm_b_glu=v_ssm_b_glu, ssm_w_out=v_ssm_w_out, conv_w=v_conv_w, conv_b=v_conv_b, conv_w_out=v_conv_w_out, w_o=v_w_o, ffn2_norm=v_ffn2_norm, ffn2_w_gate=v_ffn2_w_gate, ffn2_w_up=v_ffn2_w_up, ffn2_w_down=v_ffn2_w_down, final_norm=v_final_norm)
    names = list(P)
    sharded = ["ffn1_w_gate", "ffn1_w_up", "ffn1_w_down", "w_in", "ssm_w_glu", "ssm_w_out", "conv_w_out", "w_o",
               "ffn2_w_gate", "ffn2_w_up", "ffn2_w_down"]
    replicated = [n for n in names if n not in sharded and n != "conv_w"]

    S, D = x.shape[1], x.shape[2]
    W = ssm_d.shape[0]
    Dc = D // NDEV
    G, N = ssm_lambda_re.shape
    rows = G * N // LANE
    xh = x.reshape(S, D)
    target = loss_target.reshape(S, D)
    xi, yi, ci = lax.axis_index("x"), lax.axis_index("y"), lax.axis_index("c")
    c_arr = jnp.reshape(ci, (1,)).astype(I32)
    q_arr = jnp.reshape(2 * xi + yi, (1,)).astype(I32)
    row = lambda v: v.reshape(1, -1)

    transposed = ("ffn1_w_gate", "ffn1_w_up", "ffn2_w_gate", "ffn2_w_up")
    local = lambda table, n: table[n].T if n in transposed else table[n]

    groups = [sharded[0:1], sharded[1:2], sharded[2:3], ["w_in", "conv_w"], sharded[4:8],
              sharded[8:9], sharded[9:10], sharded[10:11]]
    first_begun, second_begun, chain = {}, {}, []
    me = 4 * xi + 2 * yi + ci

    def gather_first(gi):
        hold = 0.0 * chain[0][0, 0] if chain else 0.0
        srcs = [conv_w + hold if n == "conv_w" else (local(P, n) + hold).astype(BF) for n in groups[gi]]
        lands = [lax.dynamic_update_slice(lax.empty((NDEV,) + s.shape, s.dtype), s[None], (me,) + (0,) * s.ndim)
                 for s in srcs]
        first_begun[gi] = _split_start("gather_first_start_%d" % gi, srcs, lands, _gather_first_copies, 3, chain[-1:])
        chain.append(first_begun[gi][4])

    def gather_second(gi, after):
        _, lands = _split_wait("gather_first_wait_%d" % gi, first_begun[gi], _gather_first_copies, after)
        second_begun[gi] = _split_start("gather_second_start_%d" % gi, [], lands, _gather_second_copies, 2, chain[-1:])
        chain.append(second_begun[gi][4])

    def gathered(gi, after):
        _, lands = _split_wait("gather_second_wait_%d" % gi, second_begun[gi], _gather_second_copies, after)
        return _gather_forward("gather_forward_%d" % gi, lands, chain[-1:])

    def fetch(gi, seconds, firsts):
        def get(after):
            for g in seconds:
                gather_second(g, after)
            for g in firsts:
                gather_first(g)
            return gathered(gi, after)
        return get

    tm = _pick(S, M_TILE, SUBLANE)
    th = _pick(S, M_TILE // 2, SUBLANE)
    tk = _pick(D, K_TILE, LANE)
    ts = _pick(S, K_TILE, SUBLANE)
    tn = _pick(D, 1024, LANE)

    gather_first(0)
    gather_first(1)
    u1, r1 = _rms_fwd("rms1", xh, row(ffn1_norm), deps=list(chain))

    hold = 0.0 * chain[0][0, 0]
    replicated = [n for n in replicated if n != "ffn1_norm"] + ["ffn1_norm"]
    tile = SUBLANE * LANE

    def as_rows(p):
        flat = p.reshape(-1).astype(F32)
        return jnp.pad(flat, (0, -flat.shape[0] % tile)).reshape(-1, LANE)

    def pack(parts):
        return jnp.concatenate([as_rows(p) for p in parts], axis=0)

    cw_zero = jnp.zeros((conv_w.shape[0], W), F32)
    packed_state = [pack([t[n] + hold for n in replicated[:-1]] + [cw_zero + fill, t["ffn1_norm"] + hold])
                    for t, fill in ((P, 0.0), (M, 0.0), (V, 1.0))]
    ssm_in = tuple(p + hold for p in (ssm_lambda_re, ssm_lambda_im, ssm_log_dt, ssm_b_re, ssm_b_im, ssm_c_re, ssm_c_im))
    (a_re, a_im, bd_re, bd_im, cd_re, cd_imn), ssm_vjp = jax.vjp(_ssm_params, *ssm_in)
    bd_re_b, bd_im_b, cd_re_b, cd_imn_b = (t.astype(BF) for t in (bd_re, bd_im, cd_re, cd_imn))
    gather_second(0, [u1, a_re, a_im, bd_re_b, bd_im_b, cd_re_b, cd_imn_b] + packed_state)
    gather_first(2)
    gather_second(1, u1)
    gather_first(3)
    (wg1,) = gathered(0, u1)
    h1, ffn1_saved, wu1, wd1 = _ffn_fwd("ffn1", u1, xh, wg1, lambda after: fetch(1, [2], [4])(after)[0],
                                        lambda after: fetch(2, [3], [5])(after)[0])
    u2, r2 = _rms_fwd("rms2", h1, row(mix_norm))
    w_in_f, cw_f = fetch(3, [4], [6])(u2)
    cw = jnp.transpose(cw_f, (1, 0, 2)).reshape(3, W)
    (proj,) = _mm(
        "in_proj", (NDEV, S // tm, 1, D // tk),
        [(u2, (tm, tk), lambda b, i, j, k: (i, k)), (w_in_f, (None, tk, W), lambda b, i, j, k: (b, k, 0))],
        [(0, 1, NN, 0)], [], [((NDEV, S, W), F32, (None, tm, W), lambda b, i, j, k: (b, i, 0))], _plain, [(tm, W)])

    v_f = proj[0]
    v_bf = v_f.astype(BF)
    bu_re3, bu_im3 = _ssm_spread("ssm_bu", v_bf, bd_re_b, bd_im_b, NN, rows)
    s_re3, s_im3 = _scan_fwd(bu_re3, bu_im3, a_re, a_im)

    def y0_epilogue(acc, v_tile, d_tile):
        y0 = acc + d_tile * v_tile
        return y0, _gelu(y0)

    y0, y1, s_re_b, s_im_b = _ssm_collect("ssm_y0", s_re3, s_im3, cd_re_b, cd_imn_b, NN, v_f, row(ssm_d),
                                          y0_epilogue, [F32, BF])

    tw = _pick(W, 512, LANE)
    w_glu_f, w_so, w_co, w_o_f = fetch(4, [5], [7])(y1)
    w_glu_f = w_glu_f.reshape(W, W)
    w_o_f = w_o_f.reshape(D, D)

    def glu_epilogue(accs, sides):
        q = accs[0] + sides[1]
        return q, _gelu(sides[0]) * _sigmoid(q)

    q_pre, y2 = _mm("ssm_glu", (1, S // tm, W // tw, 1),
                    [(y1, (tm, W), lambda b, i, j, k: (i, 0)), (w_glu_f, (W, tw), lambda b, i, j, k: (0, j))],
                    [(0, 1, NN, 0)],
                    [(y0, (tm, tw), lambda b, i, j, k: (i, j)), (row(ssm_b_glu), (1, tw), lambda b, i, j, k: (0, j))],
                    [((S, W), F32, (tm, tw), lambda b, i, j, k: (i, j)), ((S, W), BF, (tm, tw), lambda b, i, j, k: (i, j))],
                    glu_epilogue, [(tm, tw)])

    yb, conv = _conv_fwd(proj, cw, row(conv_b))

    per = W // Dc
    ga_blk = (proj, (None, tm, Dc), lambda b, i, j, k: (4 + b // per, i, b % per))
    gb_blk = (proj, (None, tm, Dc), lambda b, i, j, k: (6 + b // per, i, b % per))
    dc_out = ((S, D), BF, (tm, Dc), lambda b, i, j, k: (i, b))

    def merge_epilogue(accs, sides):
        za, zb = accs
        return _sigmoid(sides[0]) * za + _sigmoid(sides[1]) * zb, za, zb

    merged, z_a, z_b = _mm(
        "mix_merge", (NDEV, S // tm, 1, 1),
        [(y2, (tm, W), lambda b, i, j, k: (i, 0)), (yb, (tm, W), lambda b, i, j, k: (i, 0)),
         (w_so, (None, W, Dc), lambda b, i, j, k: (b, 0, 0)), (w_co, (None, W, Dc), lambda b, i, j, k: (b, 0, 0))],
        [(0, 2, NN, 0), (1, 3, NN, 1)], [ga_blk, gb_blk], [dc_out, dc_out, dc_out], merge_epilogue, [(tm, Dc)] * 2)

    (h2,) = _mm("mix_out", (1, S // tm, D // tn, D // tk),
                [(merged, (tm, tk), lambda b, i, j, k: (i, k)), (w_o_f, (tk, tn), lambda b, i, j, k: (k, j))],
                [(0, 1, NN, 0)], [(h1, (tm, tn), lambda b, i, j, k: (i, j))],
                [((S, D), F32, (tm, tn), lambda b, i, j, k: (i, j))],
                lambda accs, sides: [sides[0] + accs[0]], [(tm, tn)])

    u3, r3 = _rms_fwd("rms3", h2, row(ffn2_norm))
    (wg2,) = fetch(5, [6, 7], [])(u3)
    h3, ffn2_saved, wu2, wd2 = _ffn_fwd("ffn2", u3, h2, wg2, lambda after: gathered(6, after)[0],
                                        lambda after: gathered(7, after)[0])
    loss_vec, dh3, dh3_half, d_final_norm = _loss_head("loss_head", h3, row(final_norm), target)
    loss = lax.psum(loss_vec[0, 0], ("x", "y", "c"))
    loss_done = jnp.zeros((SUBLANE, LANE), F32) + loss

    grads, deltas, new_m, new_v = {}, {}, {}, {}

    def rs_sibling_start(tag, parts):
        lands = [lax.empty((4,) + p.shape[1:], p.dtype) for p in parts]
        return _split_start("rs_sibling_start_" + tag, parts, lands, _sibling_copies, 4)

    def rs_chips_start(tag, sibling_begun, after):
        parts, lands = _split_wait("rs_sibling_wait_" + tag, sibling_begun, _sibling_copies, after)
        sums = [_sum_sibling("rs_sum_%s_%d" % (tag, a), p, land, c_arr) for a, (p, land) in enumerate(zip(parts, lands))]
        lands2 = [lax.empty((3,) + sm.shape[1:], sm.dtype) for sm in sums]
        return _split_start("rs_chips_start_" + tag, sums, lands2, _chips_copies, 3)

    def rs_end(tag, group, begun, after):
        sums, lands2 = _split_wait("rs_chips_wait_" + tag, begun, _chips_copies, after)
        for n, sm, land2 in zip(group, sums, lands2):
            res = _finish_sharded("adamw_" + n, sm, land2, q_arr, local(P, n), local(M, n), local(V, n))
            grads[n], deltas[n], new_m[n], new_v[n] = [t.T if n in transposed else t for t in res]

    f2_dwd, f2_dwgu, f2_du = _ffn_bwd("ffn2", dh3_half, u3, ffn2_saved, wg2, wu2, wd2, deps=[loss_done])
    dwd2 = f2_dwd()
    dwg2, dwu2 = f2_dwgu()
    sib_ffn2 = rs_sibling_start("ffn2", [dwg2, dwu2, dwd2])
    du3 = f2_du(deps=[sib_ffn2[4]])
    dh2, dh2_b, d_ffn2_norm = _rms_bwd("rms3_bwd", du3, h2, r3, row(ffn2_norm), dh3, 1.0)
    rs_ffn2 = rs_chips_start("ffn2", sib_ffn2, dh2)

    dg_out = ((2, S, W), BF, (None, tm, Dc), lambda b, i, j, k: (j // per, i, j % per))
    ga_blk2 = (proj, (None, tm, Dc), lambda b, i, j, k: (4 + j // per, i, j % per))
    gb_blk2 = (proj, (None, tm, Dc), lambda b, i, j, k: (6 + j // per, i, j % per))
    dcj = lambda arr: (arr, (tm, Dc), lambda b, i, j, k: (i, j))
    dcj_out = ((S, D), BF, (tm, Dc), lambda b, i, j, k: (i, j))

    def dmerge_epilogue(accs, sides):
        dm = accs[0]
        sa, sb = _sigmoid(sides[0]), _sigmoid(sides[1])
        za, zb = sides[2].astype(F32), sides[3].astype(F32)
        return dm * sa, dm * sb, dm * za * sa * (1.0 - sa), dm * zb * sb * (1.0 - sb)

    dz_a, dz_b, dga, dgb = _mm(
        "mix_out_dx", (1, S // tm, NDEV, D // tk),
        [(dh2_b, (tm, tk), lambda b, i, j, k: (i, k)), (w_o_f, (Dc, tk), lambda b, i, j, k: (j, k))],
        [(0, 1, NT, 0)], [ga_blk2, gb_blk2, dcj(z_a), dcj(z_b)], [dcj_out, dcj_out, dg_out, dg_out],
        dmerge_epilogue, [(tm, Dc)], deps=[rs_ffn2[4]])

    td = _pick(D, M_TILE, LANE)
    (dw_o,) = _mm("mix_out_dw", (1, D // td, D // tn, S // ts),
                  [(merged, (ts, td), lambda b, i, j, k: (k, i)), (dh2_b, (ts, tn), lambda b, i, j, k: (k, j))],
                  [(0, 1, TN, 0)], [], [((D, D), BF, (td, tn), lambda b, i, j, k: (i, j))], _plain, [(td, tn)])

    wout = ((NDEV, W, Dc), BF, (None, W, Dc), lambda b, i, j, k: (b, 0, 0))
    dw_so, dw_co = _mm(
        "mix_merge_dw", (NDEV, 1, 1, S // ts),
        [(y2, (ts, W), lambda b, i, j, k: (k, 0)), (yb, (ts, W), lambda b, i, j, k: (k, 0)),
         (dz_a, (ts, Dc), lambda b, i, j, k: (k, b)), (dz_b, (ts, Dc), lambda b, i, j, k: (k, b))],
        [(0, 2, TN, 0), (1, 3, TN, 1)], [], [wout, wout], _plain, [(W, Dc)] * 2)

    def dglu_epilogue(accs, sides):
        dy2, dyb = accs
        sq = _sigmoid(sides[1])
        return dy2 * _gelu(sides[0]) * sq * (1.0 - sq), dy2 * sq, dyb

    full_w = lambda arr: (arr, (th, W), lambda b, i, j, k: (i, 0))
    full_w_out = lambda dt: ((S, W), dt, (th, W), lambda b, i, j, k: (i, 0))
    dq, dy1p, dyb = _mm(
        "mix_merge_dx", (1, S // th, 1, NDEV),
        [(dz_a, (th, Dc), lambda b, i, j, k: (i, k)), (dz_b, (th, Dc), lambda b, i, j, k: (i, k)),
         (w_so, (None, W, Dc), lambda b, i, j, k: (k, 0, 0)), (w_co, (None, W, Dc), lambda b, i, j, k: (k, 0, 0))],
        [(0, 2, NT, 0), (1, 3, NT, 1)], [full_w(y0), full_w(q_pre)], [full_w_out(BF), full_w_out(F32), full_w_out(F32)],
        dglu_epilogue, [(th, W)] * 2)

    def dy0_epilogue(accs, sides):
        dy0 = (sides[0] + accs[0]) * _gelu_grad(sides[1])
        return dy0, dy0

    wj = lambda arr: (arr, (tm, tw), lambda b, i, j, k: (i, j))
    dy0, dy0_b = _mm("ssm_glu_dx", (1, S // tm, W // tw, 1),
                     [(dq, (tm, W), lambda b, i, j, k: (i, 0)), (w_glu_f, (tw, W), lambda b, i, j, k: (j, 0))],
                     [(0, 1, NT, 0)], [wj(dy1p), wj(y0)],
                     [((S, W), F32, (tm, tw), lambda b, i, j, k: (i, j)), ((S, W), BF, (tm, tw), lambda b, i, j, k: (i, j))],
                     dy0_epilogue, [(tm, tw)])

    (dw_glu,) = _mm("ssm_glu_dw", (1, W // tw, 1, S // ts),
                    [(y1, (ts, tw), lambda b, i, j, k: (k, i)), (dq, (ts, W), lambda b, i, j, k: (k, 0))],
                    [(0, 1, TN, 0)], [], [((W, W), BF, (tw, W), lambda b, i, j, k: (i, 0))], _plain, [(tw, W)])

    tr = _pick(S, 256, SUBLANE)
    rw = ((tr, W), lambda i: (i, 0))
    vec_w = ((1, W), F32, (1, W), lambda i: (0, 0))
    d_b_glu, d_ssm_d = _ew(
        "ssm_colsums", (S // tr,), [(dq,) + rw, (dy0,) + rw, (proj, (None, tr, W), lambda i: (0, i, 0))],
        [vec_w, vec_w],
        lambda dqv, dyv, vv: (jnp.sum(dqv.astype(F32), axis=0, keepdims=True), jnp.sum(dyv * vv, axis=0, keepdims=True)),
        acc=(0, 1))

    ds_re3, ds_im3 = _ssm_spread("ssm_ds", dy0_b, cd_re_b, cd_imn_b, NT, rows)
    lam_re3, lam_im3, da_re, da_im = _scan_bwd(ds_re3, ds_im3, s_re3, s_im3, a_re, a_im)
    dv, lam_re_b, lam_im_b = _ssm_collect("ssm_dv", lam_re3, lam_im3, bd_re_b, bd_im_b, NT, dy0, row(ssm_d),
                                          lambda acc, dy_tile, d_tile: [acc + dy_tile * d_tile], [BF])
    tiles, tch, tst = bd_re.shape
    tok_ch = lambda arr: (arr, (ts, tch), lambda b, i, j, k: (k, b))
    tok_st = lambda arr: (arr, (ts, tst), lambda b, i, j, k: (k, b))
    bd_out = ((tiles, tch, tst), F32, (None, tch, tst), lambda b, i, j, k: (b, 0, 0))
    cd_out = ((tiles, tst, tch), F32, (None, tst, tch), lambda b, i, j, k: (b, 0, 0))
    dbd_re, dbd_im = _mm("ssm_dbd", (tiles, 1, 1, S // ts), [tok_ch(v_bf), tok_st(lam_re_b), tok_st(lam_im_b)],
                         [(0, 1, TN, 0), (0, 2, TN, 1)], [], [bd_out, bd_out], _plain, [(tch, tst)] * 2)
    dcd_re, dcd_imn = _mm("ssm_dcd", (tiles, 1, 1, S // ts), [tok_st(s_re_b), tok_st(s_im_b), tok_ch(dy0_b)],
                          [(0, 2, TN, 0), (1, 2, TN, 1)], [], [cd_out, cd_out], _plain, [(tst, tch)] * 2)
    d_ssm = ssm_vjp((da_re, da_im, dbd_re, dbd_im, dcd_re, dcd_imn))

    dbg, dcg, dval, d_conv_w_full, d_conv_b = _conv_bwd(dyb, proj, conv, cw)
    dproj = jnp.concatenate([dv[None], dbg[None], dcg[None], dval[None], dga, dgb], axis=0)

    (dw_in,) = _mm("in_proj_dw", (NDEV, D // td, 1, S // ts),
                   [(u2, (ts, td), lambda b, i, j, k: (k, i)), (dproj, (None, ts, W), lambda b, i, j, k: (b, k, 0))],
                   [(0, 1, TN, 0)], [], [((NDEV, D, W), BF, (None, td, W), lambda b, i, j, k: (b, i, 0))],
                   _plain, [(td, W)])
    sib_mixer = rs_sibling_start(
        "mixer", [dw_in, dw_glu.reshape(NDEV, W // NDEV, W), dw_so, dw_co, dw_o.reshape(NDEV, Dc, D)])
    (du2,) = _mm("in_proj_dx", (1, S // tm, D // tn, NDEV // K_SHARDS),
                 [(dproj, (K_SHARDS, tm, W), lambda b, i, j, k: (k, i, 0)),
                  (w_in_f, (K_SHARDS, tn, W), lambda b, i, j, k: (k, j, 0))],
                 [(0, 1, NT, 0)], [], [((S, D), F32, (tm, tn), lambda b, i, j, k: (i, j))], _plain, [(tm, tn)],
                 deps=[sib_mixer[4]])
    dh1, dh1_half, d_mix_norm = _rms_bwd("rms2_bwd", du2, h1, r2, row(mix_norm), dh2, 0.5)
    rs_mixer = rs_chips_start("mixer", sib_mixer, dh1)

    small = dict(mix_norm=d_mix_norm, ffn2_norm=d_ffn2_norm, final_norm=d_final_norm,
                 ssm_lambda_re=d_ssm[0], ssm_lambda_im=d_ssm[1], ssm_log_dt=d_ssm[2], ssm_b_re=d_ssm[3],
                 ssm_b_im=d_ssm[4], ssm_c_re=d_ssm[5], ssm_c_im=d_ssm[6], ssm_d=d_ssm_d, ssm_b_glu=d_b_glu,
                 conv_b=d_conv_b)
    early_pk = pack([small[n] for n in replicated[:-1]] + [d_conv_w_full])
    early_land = lax.dynamic_update_slice(jnp.zeros((NDEV,) + early_pk.shape, F32), early_pk[None], (me, 0, 0))
    small_begun = _split_start("gather_small_start", [early_pk], [early_land], _everyone_copies, NDEV - 1)

    f1_dwd, f1_dwgu, f1_du = _ffn_bwd("ffn1", dh1_half, u1, ffn1_saved, wg1, wu1, wd1,
                                      deps=[rs_mixer[4], small_begun[4]])
    du1 = f1_du()
    dx, _, d_ffn1_norm = _rms_bwd("rms1_bwd", du1, xh, r1, row(ffn1_norm), dh1, 1.0)
    dwd1 = f1_dwd(deps=[d_ffn1_norm])
    late = d_ffn1_norm + 0.0 * dwd1[0, :1, :1].astype(F32)
    late_pk = late.reshape(-1, LANE)
    late_land = lax.dynamic_update_slice(jnp.zeros((NDEV,) + late_pk.shape, F32), late_pk[None], (me, 0, 0))
    late_begun = _split_start("gather_late_start", [late_pk], [late_land], _everyone_copies, NDEV - 1)
    rs_ffn1_down = rs_chips_start("ffn1_down", rs_sibling_start("ffn1_down", [dwd1]), late_begun[4])
    dwg1, dwu1 = f1_dwgu(deps=[rs_ffn1_down[4]])
    rs_ffn1_gate = rs_chips_start("ffn1_gate", rs_sibling_start("ffn1_gate", [dwg1]), None)
    rs_ffn1_up = rs_chips_start("ffn1_up", rs_sibling_start("ffn1_up", [dwu1]), rs_ffn1_gate[4])
    rs_end("ffn2", sharded[8:11], rs_ffn2, rs_ffn1_up[4])
    rs_end("mixer", sharded[3:8], rs_mixer, [grads[n] for n in sharded[8:11]])
    _, (early_all,) = _split_wait("gather_small_wait", small_begun, _everyone_copies, [grads[n] for n in sharded[3:8]])
    _, (late_all,) = _split_wait("gather_late_wait", late_begun, _everyone_copies, early_all)
    small_all = jnp.concatenate([early_all, late_all], axis=1)
    g_pk, d_pk, m_pk, v_pk = _finish_replicated("adamw_replicated", small_all, *packed_state)

    def unpack(pk, r0, like):
        nr = -(-like.size // tile) * SUBLANE
        return pk[r0:r0 + nr].reshape(-1)[:like.size].reshape(like.shape), r0 + nr

    r0 = 0
    for n in replicated[:-1] + ["conv_w", "ffn1_norm"]:
        like = d_conv_w_full if n == "conv_w" else P[n]
        for store, pk in ((grads, g_pk), (deltas, d_pk), (new_m, m_pk), (new_v, v_pk)):
            store[n], r1 = unpack(pk, r0, like)
        r0 = r1
    g_cw_full = grads["conv_w"]
    cwl = conv_w.shape[1]
    g_cw = lax.dynamic_slice_in_dim(g_cw_full, me * cwl, cwl, axis=1)
    full3 = ((3, cwl), lambda i: (0, 0))
    grads["conv_w"], deltas["conv_w"], new_m["conv_w"], new_v["conv_w"] = _ew(
        "adamw_conv_w", (1,), [(g_cw,) + full3, (conv_w,) + full3, (m_conv_w,) + full3, (v_conv_w,) + full3],
        [((3, cwl), F32) + full3] * 4, lambda g, w, m, v: (g,) + _adamw(w, g, m, v))
    rs_end("ffn1_down", sharded[2:3], rs_ffn1_down, [g_pk, grads["conv_w"]])
    rs_end("ffn1_gate", sharded[0:1], rs_ffn1_gate, grads["ffn1_w_down"])
    rs_end("ffn1_up", sharded[1:2], rs_ffn1_up, grads["ffn1_w_gate"])

    return (loss, dx.reshape(x.shape), *[grads[n] for n in names], *[deltas[n] for n in names],
            *[new_m[n] for n in names], *[new_v[n] for n in names])
```

```python
import math

import jax
import jax.numpy as jnp
from jax import lax
from jax.experimental import pallas as pl
from jax.experimental.pallas import tpu as pltpu

F32 = jnp.float32
BF = jnp.bfloat16
I32 = jnp.int32
MESH = pl.DeviceIdType.MESH
LANE = 128
SUBLANE = 8
NDEV = 8
EW_BLOCK = 256 * 1024
M_TILE = 1024
K_TILE = 2048
K_SHARDS = 2
EPS = 1e-6
ADAM_LR, ADAM_B1, ADAM_B2, ADAM_EPS, ADAM_WD, ADAM_STEP = 0.001, 0.9, 0.999, 1e-08, 0.01, 10
NN = ((1,), (0,))
NT = ((1,), (1,))
TN = ((0,), (0,))
HBM = pl.BlockSpec(memory_space=pltpu.HBM)


def _pick(n, pref, mult):
    t = min(pref, n)
    t -= t % mult
    while t >= mult:
        if n % t == 0:
            return t
        t -= mult
    return n


def _sigmoid(x):
    return 1.0 / (1.0 + jnp.exp(-x))


_GELU_C = math.sqrt(2.0 / math.pi)


def _gelu(x):
    return 0.5 * x * (1.0 + jnp.tanh(_GELU_C * (x + 0.044715 * x * x * x)))


def _gelu_grad(x):
    t = jnp.tanh(_GELU_C * (x + 0.044715 * x * x * x))
    return 0.5 * (1.0 + t) + 0.5 * x * (1.0 - t * t) * _GELU_C * (1.0 + 3.0 * 0.044715 * x * x)


def _dep_specs(deps, rank):
    return [(d, d.shape, lambda *_, nd=d.ndim: (0,) * nd) for d in deps]


def _mm(name, grid, ops, pairs, sides, outs, epilogue, acc_shapes, deps=()):
    nk = grid[-1]
    n_ops, n_sides, n_outs = len(ops), len(sides), len(outs)
    dep_specs = _dep_specs(deps, len(grid))
    n_deps = len(dep_specs)

    def body(*refs):
        op_refs = refs[:n_ops]
        side_refs = refs[n_ops:n_ops + n_sides]
        out_refs = refs[n_ops + n_sides + n_deps:n_ops + n_sides + n_deps + n_outs]
        acc_refs = refs[n_ops + n_sides + n_deps + n_outs:]

        def partials():
            res = [None] * len(acc_shapes)
            for ia, ib, dims, ai in pairs:
                a_ref, b_ref = op_refs[ia], op_refs[ib]
                for s in range(a_ref.shape[0] if len(a_ref.shape) == 3 else 1):
                    a, b = (a_ref[s], b_ref[s]) if len(a_ref.shape) == 3 else (a_ref[...], b_ref[...])
                    p = lax.dot_general(a, b, (dims, ((), ())), preferred_element_type=F32)
                    res[ai] = p if res[ai] is None else res[ai] + p
            return res

        def finish(accs):
            vals = epilogue(accs, [s[...] for s in side_refs])
            for o, v in zip(out_refs, vals):
                o[...] = v.astype(o.dtype)

        if nk == 1:
            finish(partials())
        else:
            k = pl.program_id(len(grid) - 1)

            @pl.when(k == 0)
            def _():
                for a, p in zip(acc_refs, partials()):
                    a[...] = p

            @pl.when(k > 0)
            def _():
                for a, p in zip(acc_refs, partials()):
                    a[...] += p

            @pl.when(k == nk - 1)
            def _():
                finish([a[...] for a in acc_refs])

    return pl.pallas_call(
        body, name=name, grid=grid,
        in_specs=[pl.BlockSpec(b, m) for (_, b, m) in list(ops) + list(sides) + dep_specs],
        out_specs=[pl.BlockSpec(b, m) for (_, _, b, m) in outs],
        out_shape=[jax.ShapeDtypeStruct(s, d) for (s, d, _, _) in outs],
        scratch_shapes=[pltpu.VMEM(s, F32) for s in acc_shapes] if nk > 1 else [],
        compiler_params=pltpu.CompilerParams(
            dimension_semantics=("parallel",) * (len(grid) - 1) + ("arbitrary",)),
    )(*[a for (a, _, _) in list(ops) + list(sides) + dep_specs])


def _ew(name, grid, ins, outs, fn, acc=(), deps=()):
    n_in = len(ins)
    dep_specs = _dep_specs(deps, len(grid))

    def body(*refs):
        vals = fn(*[r[...] for r in refs[:n_in]])
        first = pl.program_id(0) == 0
        for idx, (o, v) in enumerate(zip(refs[n_in + len(dep_specs):], vals)):
            if idx in acc:
                @pl.when(first)
                def _(o=o, v=v):
                    o[...] = v.astype(o.dtype)

                @pl.when(jnp.logical_not(first))
                def _(o=o, v=v):
                    o[...] += v.astype(o.dtype)
            else:
                o[...] = v.astype(o.dtype)

    return pl.pallas_call(
        body, name=name, grid=grid,
        in_specs=[pl.BlockSpec(b, m) for (_, b, m) in list(ins) + dep_specs],
        out_specs=[pl.BlockSpec(b, m) for (_, _, b, m) in outs],
        out_shape=[jax.ShapeDtypeStruct(s, d) for (s, d, _, _) in outs],
        compiler_params=pltpu.CompilerParams(
            dimension_semantics=(("arbitrary",) if acc else ("parallel",)) * len(grid)),
    )(*[a for (a, _, _) in list(ins) + dep_specs])


def _position():
    x, y, c = lax.axis_index("x"), lax.axis_index("y"), lax.axis_index("c")
    chips = [(1 - x, y), (x, 1 - y), (1 - x, 1 - y)]
    return x, y, c, chips


SEM = pl.BlockSpec(memory_space=pltpu.SEMAPHORE)
EFFECT = pltpu.SideEffectType.DATAFLOW_SIDE_EFFECTING


def _in_hbm(v):
    return pltpu.with_memory_space_constraint(v, pltpu.HBM)


def _split_start(name, srcs, lands, make_copies, n_per, after=()):
    n, nb = len(srcs), len(srcs) + len(lands)
    n_sems = len(lands) * n_per
    after = list(after)

    def body(*refs):
        send_sems, recv_sems = refs[nb + len(after)], refs[nb + len(after) + 1]
        for cp in make_copies(refs[:n], refs[n:nb], send_sems, recv_sems):
            cp.start()
        refs[-1][...] = jnp.zeros_like(refs[-1])

    outs = pl.pallas_call(
        body, name=name,
        out_shape=(pltpu.SemaphoreType.DMA((n_sems,)), pltpu.SemaphoreType.DMA((n_sems,)),
                   *[pltpu.HBM(v.shape, v.dtype) for v in list(srcs) + list(lands)],
                   jax.ShapeDtypeStruct((SUBLANE, LANE), F32)),
        in_specs=[HBM] * nb + [pl.BlockSpec(memory_space=pl.ANY)] * len(after),
        out_specs=(SEM, SEM, *[HBM] * nb, pl.BlockSpec(memory_space=pltpu.VMEM)),
        input_output_aliases={i: 2 + i for i in range(nb)},
        compiler_params=pltpu.CompilerParams(has_side_effects=EFFECT),
    )(*[_in_hbm(v) for v in list(srcs) + list(lands)], *after)
    return outs[0], outs[1], list(outs[2:2 + n]), list(outs[2 + n:2 + nb]), outs[-1]


def _split_wait(name, started, make_copies, after):
    send_sems, recv_sems, srcs, lands, _ = started
    n, nb = len(srcs), len(srcs) + len(lands)

    def body(*refs):
        for cp in make_copies(refs[:n], refs[n:nb], refs[nb], refs[nb + 1]):
            cp.wait_send()
            cp.wait_recv()

    order = [] if after is None else list(after) if isinstance(after, (list, tuple)) else [after]
    outs = pl.pallas_call(
        body, name=name,
        out_shape=tuple(pltpu.HBM(v.shape, v.dtype) for v in srcs + lands),
        in_specs=[HBM] * nb + [SEM, SEM] + [pl.BlockSpec(memory_space=pl.ANY)] * len(order),
        out_specs=tuple([HBM] * nb),
        input_output_aliases={i: i for i in range(nb)},
        compiler_params=pltpu.CompilerParams(has_side_effects=EFFECT),
    )(*srcs, *lands, send_sems, recv_sems, *order)
    return list(outs[:n]), list(outs[n:])


def _gather_first_copies(xs, lands, send_sems, recv_sems):
    x, y, c, _ = _position()
    copies = []
    for a in range(len(xs)):
        for k, peer in enumerate([(x, y, 1 - c), (1 - x, y, c), (x, 1 - y, c)]):
            copies.append(pltpu.make_async_remote_copy(
                src_ref=xs[a], dst_ref=lands[a].at[4 * x + 2 * y + c],
                send_sem=send_sems.at[3 * a + k], recv_sem=recv_sems.at[3 * a + k], device_id=peer, device_id_type=MESH))
    return copies


def _gather_second_copies(xs, lands, send_sems, recv_sems):
    x, y, c, _ = _position()
    copies = []
    for a in range(len(lands)):
        rows = lands[a].shape[1]
        unit = SUBLANE * (4 // jnp.dtype(lands[a].dtype).itemsize)
        half = rows // 2 // unit * unit or rows
        parts = [((1 - x, y), (x, 1 - y), pl.ds(0, half))]
        if half < rows:
            parts.append(((x, 1 - y), (1 - x, y), pl.ds(half, rows - half)))
        for k, (block, to, rs) in enumerate(parts):
            ref = lands[a].at[4 * block[0] + 2 * block[1] + c, rs]
            copies.append(pltpu.make_async_remote_copy(
                src_ref=ref, dst_ref=ref, send_sem=send_sems.at[2 * a + k], recv_sem=recv_sems.at[2 * a + k],
                device_id=(*to, c), device_id_type=MESH))
    return copies


def _chips_copies(ps, lands, send_sems, recv_sems):
    x, y, c, chips = _position()
    copies = []
    for a in range(len(ps)):
        for j, chip in enumerate(chips):
            copies.append(pltpu.make_async_remote_copy(
                src_ref=ps[a].at[2 * chip[0] + chip[1]], dst_ref=lands[a].at[j],
                send_sem=send_sems.at[3 * a + j], recv_sem=recv_sems.at[3 * a + j], device_id=(*chip, c),
                device_id_type=MESH))
    return copies


def _sibling_copies(gs, lands, send_sems, recv_sems):
    x, y, c, _ = _position()
    copies = []
    for a in range(len(gs)):
        for q in range(4):
            copies.append(pltpu.make_async_remote_copy(
                src_ref=gs[a].at[2 * q + 1 - c], dst_ref=lands[a].at[q],
                send_sem=send_sems.at[4 * a + q], recv_sem=recv_sems.at[4 * a + q],
                device_id=(x, y, 1 - c), device_id_type=MESH))
    return copies


def _everyone_copies(xs, lands, send_sems, recv_sems):
    x, y, c, _ = _position()
    flip = lambda v, bit: 1 - v if bit else v
    copies = []
    for a in range(len(xs)):
        for k in range(1, NDEV):
            copies.append(pltpu.make_async_remote_copy(
                src_ref=xs[a], dst_ref=lands[a].at[4 * x + 2 * y + c],
                send_sem=send_sems.at[7 * a + k - 1], recv_sem=recv_sems.at[7 * a + k - 1],
                device_id=(flip(x, k & 4), flip(y, k & 2), flip(c, k & 1)), device_id_type=MESH))
    return copies


def _gather_forward(name, lands, after=()):
    n = len(lands)
    after = list(after)

    def body(*refs):
        ins, outs = refs[:n], refs[n + len(after):2 * n + len(after)]
        send_sems, recv_sems = refs[2 * n + len(after):]
        x, y, c, chips = _position()
        sends, arrivals = [], []
        for a in range(n):
            for j, chip in enumerate(chips):
                slot = 4 * chip[0] + 2 * chip[1]

                def to_sibling(src, dst):
                    return pltpu.make_async_remote_copy(
                        src_ref=src, dst_ref=dst, send_sem=send_sems.at[a, j], recv_sem=recv_sems.at[a, j],
                        device_id=(x, y, 1 - c), device_id_type=MESH)

                sends.append(to_sibling(ins[a].at[slot + c], outs[a].at[slot + c]))
                arrivals.append(to_sibling(ins[a].at[slot + c], outs[a].at[slot + 1 - c]))
        for cp in sends:
            cp.start()
        for cp in arrivals:
            cp.wait()

    return pl.pallas_call(
        body, name=name,
        out_shape=[jax.ShapeDtypeStruct(l.shape, l.dtype) for l in lands],
        in_specs=[HBM] * n + [pl.BlockSpec(memory_space=pl.ANY)] * len(after), out_specs=[HBM] * n,
        input_output_aliases={a: a for a in range(n)},
        scratch_shapes=[pltpu.SemaphoreType.DMA((n, 3)), pltpu.SemaphoreType.DMA((n, 3))],
    )(*lands, *after)


def _sum_sibling(name, g, land, c_arr):
    _, R, C = g.shape
    tr = _pick(R, 512, SUBLANE)

    def body(c_ref, g_ref, l_ref, o_ref):
        o_ref[...] = (g_ref[...].astype(F32) + l_ref[...].astype(F32)).astype(o_ref.dtype)

    return pl.pallas_call(
        body, name=name,
        grid_spec=pltpu.PrefetchScalarGridSpec(
            num_scalar_prefetch=1, grid=(4, R // tr),
            in_specs=[pl.BlockSpec((None, tr, C), lambda q, i, cr: (2 * q + cr[0], i, 0)),
                      pl.BlockSpec((None, tr, C), lambda q, i, cr: (q, i, 0))],
            out_specs=pl.BlockSpec((None, tr, C), lambda q, i, cr: (q, i, 0))),
        out_shape=jax.ShapeDtypeStruct((4, R, C), g.dtype),
        compiler_params=pltpu.CompilerParams(dimension_semantics=("parallel", "parallel")),
    )(c_arr, g, land)


def _adamw(w, g, m, v):
    m = ADAM_B1 * m + (1.0 - ADAM_B1) * g
    v = ADAM_B2 * v + (1.0 - ADAM_B2) * (g * g)
    m_hat = m / (1.0 - ADAM_B1 ** ADAM_STEP)
    v_hat = v / (1.0 - ADAM_B2 ** ADAM_STEP)
    delta = -ADAM_LR * (m_hat / (jnp.sqrt(v_hat) + ADAM_EPS) + ADAM_WD * w)
    return delta, m, v


def _finish_sharded(name, sums, land, q_arr, w, m, v):
    R, C = w.shape
    tr = _pick(R, 512, SUBLANE)
    tc = _pick(C, max(LANE, EW_BLOCK // tr), LANE)

    def body(q_ref, p_ref, l_ref, w_ref, m_ref, v_ref, g_out, d_out, m_out, v_out):
        g = p_ref[...].astype(F32)
        for j in range(3):
            g = g + l_ref[j].astype(F32)
        d, mn, vn = _adamw(w_ref[...], g, m_ref[...], v_ref[...])
        g_out[...] = g
        d_out[...] = d
        m_out[...] = mn
        v_out[...] = vn

    blk = pl.BlockSpec((tr, tc), lambda i, j, qr: (i, j))
    return pl.pallas_call(
        body, name=name,
        grid_spec=pltpu.PrefetchScalarGridSpec(
            num_scalar_prefetch=1, grid=(R // tr, C // tc),
            in_specs=[pl.BlockSpec((None, tr, tc), lambda i, j, qr: (qr[0], i, j)),
                      pl.BlockSpec((3, tr, tc), lambda i, j, qr: (0, i, j)), blk, blk, blk],
            out_specs=[blk] * 4),
        out_shape=[jax.ShapeDtypeStruct((R, C), F32)] * 4,
        compiler_params=pltpu.CompilerParams(dimension_semantics=("parallel", "parallel")),
    )(q_arr, sums, land, w, m, v)


def _finish_replicated(name, gathered, w, m, v):
    _, R, C = gathered.shape
    tr = _pick(R, 256, SUBLANE)

    def fn(gv, wv, mv, vv):
        g = gv[0]
        for d in range(1, NDEV):
            g = g + gv[d]
        dl, mn, vn = _adamw(wv, g, mv, vv)
        return g, dl, mn, vn

    row = ((tr, C), lambda i: (i, 0))
    return _ew(name, (R // tr,),
               [(gathered, (NDEV, tr, C), lambda i: (0, i, 0)), (w,) + row, (m,) + row, (v,) + row],
               [((R, C), F32) + row] * 4, fn)


def _rms_fwd(name, h, g, deps=()):
    S, D = h.shape
    tr = _pick(S, 256, SUBLANE)

    def fn(hv, gv):
        r = lax.rsqrt(jnp.mean(hv * hv, axis=-1, keepdims=True) + EPS)
        return hv * r * gv, r

    return _ew(name, (S // tr,),
               [(h, (tr, D), lambda i: (i, 0)), (g, (1, D), lambda i: (0, 0))],
               [((S, D), BF, (tr, D), lambda i: (i, 0)), ((S, 1), F32, (tr, 1), lambda i: (i, 0))], fn, deps=deps)


def _rms_bwd(name, du, h, r, g, dres, scale):
    S, D = h.shape
    tr = _pick(S, 256, SUBLANE)

    def fn(duv, hv, rv, gv, drv):
        xn = hv * rv
        dxn = duv * gv
        dh = drv + rv * (dxn - xn * jnp.mean(dxn * xn, axis=-1, keepdims=True))
        return dh, scale * dh, jnp.sum(duv * xn, axis=0, keepdims=True)

    row = ((tr, D), lambda i: (i, 0))
    return _ew(name, (S // tr,),
               [(du,) + row, (h,) + row, (r, (tr, 1), lambda i: (i, 0)), (g, (1, D), lambda i: (0, 0)), (dres,) + row],
               [((S, D), F32) + row, ((S, D), BF) + row, ((1, D), F32, (1, D), lambda i: (0, 0))], fn, acc=(2,))


def _loss_head(name, h, g, target):
    S, D = h.shape
    tr = _pick(S, 256, SUBLANE)

    def fn(hv, gv, tv):
        r = lax.rsqrt(jnp.mean(hv * hv, axis=-1, keepdims=True) + EPS)
        xn = hv * r
        diff = xn * gv - tv
        loss = 0.5 * jnp.sum(jnp.mean(diff * diff, axis=-1, keepdims=True))
        dout = diff / D
        dxn = dout * gv
        dh = r * (dxn - xn * jnp.mean(dxn * xn, axis=-1, keepdims=True))
        return (jnp.zeros((1, LANE), F32) + loss, dh, 0.5 * dh, jnp.sum(dout * xn, axis=0, keepdims=True))

    row = ((tr, D), lambda i: (i, 0))
    return _ew(name, (S // tr,),
               [(h,) + row, (g, (1, D), lambda i: (0, 0)), (target,) + row],
               [((1, LANE), F32, (1, LANE), lambda i: (0, 0)), ((S, D), F32) + row, ((S, D), BF) + row,
                ((1, D), F32, (1, D), lambda i: (0, 0))], fn, acc=(0, 3))


def _ffn_fwd(tag, u, h, wg, fetch_wu, fetch_wd):
    S, D = u.shape
    Fs = wg.shape[1]
    tm, tk = _pick(S, M_TILE, SUBLANE), _pick(D, K_TILE, LANE)
    act = ((NDEV, S, Fs), BF, (None, tm, Fs), lambda b, i, j, k: (b, i, 0))
    lhs = (u, (tm, tk), lambda b, i, j, k: (i, k))
    rhs = lambda w: (w, (None, Fs, tk), lambda b, i, j, k: (b, 0, k))
    (gt,) = _mm(tag + "_gate", (NDEV, S // tm, 1, D // tk), [lhs, rhs(wg)], [(0, 1, NT, 0)], [], [act],
                lambda accs, sides: accs, [(tm, Fs)])
    wu = fetch_wu(gt)

    def up_epilogue(accs, sides):
        g = sides[0].astype(F32)
        return accs[0], g * _sigmoid(g) * accs[0]

    up, a = _mm(tag + "_up", (NDEV, S // tm, 1, D // tk), [lhs, rhs(wu)], [(0, 1, NT, 0)],
                [(gt, (None, tm, Fs), lambda b, i, j, k: (b, i, 0))], [act, act], up_epilogue, [(tm, Fs)])
    wd = fetch_wd(a)
    tn = _pick(D, 1024, LANE)
    (hn,) = _mm(
        tag + "_down", (1, S // tm, D // tn, NDEV // K_SHARDS),
        [(a, (K_SHARDS, tm, Fs), lambda b, i, j, k: (k, i, 0)), (wd, (K_SHARDS, Fs, tn), lambda b, i, j, k: (k, 0, j))],
        [(0, 1, NN, 0)], [(h, (tm, tn), lambda b, i, j, k: (i, j))],
        [((S, D), F32, (tm, tn), lambda b, i, j, k: (i, j))],
        lambda accs, sides: [sides[0] + 0.5 * accs[0]], [(tm, tn)])
    return hn, (gt, up, a), wu, wd


def _ffn_bwd(tag, dhs, u, saved, wg, wu, wd, deps=()):
    gt, up, a = saved
    S, D = u.shape
    Fs = wg.shape[1]
    tm, tk = _pick(S, M_TILE, SUBLANE), _pick(D, K_TILE, LANE)
    act_in = lambda arr: (arr, (None, tm, Fs), lambda b, i, j, k: (b, i, 0))
    act_out = ((NDEV, S, Fs), BF, (None, tm, Fs), lambda b, i, j, k: (b, i, 0))

    def act_epilogue(accs, sides):
        da = accs[0]
        gtv, upv = sides[0].astype(F32), sides[1].astype(F32)
        sg = _sigmoid(gtv)
        return da * upv * sg * (1.0 + gtv * (1.0 - sg)), da * gtv * sg

    dgt, dup = _mm(
        tag + "_dact", (NDEV, S // tm, 1, D // tk),
        [(dhs, (tm, tk), lambda b, i, j, k: (i, k)), (wd, (None, Fs, tk), lambda b, i, j, k: (b, 0, k))],
        [(0, 1, NT, 0)], [act_in(gt), act_in(up)], [act_out, act_out], act_epilogue, [(tm, Fs)], deps=deps)

    ts = _pick(S, K_TILE, SUBLANE)
    tn = _pick(D, 1024, LANE)
    wgrad = ((NDEV, Fs, D), BF, (None, Fs, tn), lambda b, i, j, k: (b, 0, j))
    tok = lambda arr: (arr, (None, ts, Fs), lambda b, i, j, k: (b, k, 0))

    def grad_down(deps=()):
        return _mm(
            tag + "_dwd", (NDEV, 1, D // tn, S // ts),
            [tok(a), (dhs, (ts, tn), lambda b, i, j, k: (k, j))],
            [(0, 1, TN, 0)], [], [wgrad], lambda accs, sides: accs, [(Fs, tn)], deps=deps)[0]

    def grad_gate_up(deps=()):
        return _mm(
            tag + "_dwgu", (NDEV, 1, D // tn, S // ts),
            [tok(dgt), tok(dup), (u, (ts, tn), lambda b, i, j, k: (k, j))],
            [(0, 2, TN, 0), (1, 2, TN, 1)], [], [wgrad, wgrad], lambda accs, sides: accs, [(Fs, tn), (Fs, tn)],
            deps=deps)

    def du(deps=()):
        return _mm(
            tag + "_du", (1, S // tm, D // tn, NDEV // K_SHARDS),
            [(dgt, (K_SHARDS, tm, Fs), lambda b, i, j, k: (k, i, 0)), (dup, (K_SHARDS, tm, Fs), lambda b, i, j, k: (k, i, 0)),
             (wg, (K_SHARDS, Fs, tn), lambda b, i, j, k: (k, 0, j)), (wu, (K_SHARDS, Fs, tn), lambda b, i, j, k: (k, 0, j))],
            [(0, 2, NN, 0), (1, 3, NN, 0)], [], [((S, D), F32, (tm, tn), lambda b, i, j, k: (i, j))],
            lambda accs, sides: accs, [(tm, tn)], deps=deps)[0]

    return grad_down, grad_gate_up, du


def _ssm_params(lam_re, lam_im, log_dt, b_re, b_im, c_re, c_im):
    G, N = lam_re.shape
    C = b_re.shape[2]
    lam_re = jnp.minimum(lam_re, -1e-4)
    dt = jnp.exp(log_dt)[:, None]
    mag = jnp.exp(lam_re * dt)
    a_re = mag * jnp.cos(lam_im * dt)
    a_im = mag * jnp.sin(lam_im * dt)
    den = lam_re * lam_re + lam_im * lam_im
    p = a_re - 1.0
    f_re = ((p * lam_re + a_im * lam_im) / den)[:, :, None]
    f_im = ((a_im * lam_re - p * lam_im) / den)[:, :, None]
    bb_re = f_re * b_re - f_im * b_im
    bb_im = f_re * b_im + f_im * b_re
    gpt = LANE // C
    tiles = G // gpt
    eye = jnp.eye(gpt, dtype=F32)

    def bd(bb):
        return jnp.einsum("bgnc,gh->bgchn", bb.reshape(tiles, gpt, N, C), eye).reshape(tiles, gpt * C, gpt * N)

    def cd(cc):
        return jnp.einsum("bgcn,gh->bgnhc", cc.reshape(tiles, gpt, C, N), eye).reshape(tiles, gpt * N, gpt * C)

    rows = G * N // LANE
    return (a_re.reshape(rows, LANE), a_im.reshape(rows, LANE), bd(bb_re), bd(bb_im), cd(c_re), cd(-c_im))


def _tile_states(ref3, b, per):
    return jnp.concatenate([ref3[:, per * b + r, :] for r in range(per)], axis=1).astype(BF)


def _ssm_spread(name, x, m_re, m_im, dims, rows):
    S, W = x.shape
    tiles = m_re.shape[0]
    tch, per = W // tiles, rows // tiles
    tq = _pick(S, 256, SUBLANE)

    def body(x_ref, mre_ref, mim_ref, ore_ref, oim_ref):
        for b in range(tiles):
            xb = x_ref[:, b * tch:(b + 1) * tch]
            for m_ref, o_ref in ((mre_ref, ore_ref), (mim_ref, oim_ref)):
                val = lax.dot_general(xb, m_ref[b], (dims, ((), ())), preferred_element_type=F32)
                for r in range(per):
                    o_ref[:, per * b + r, :] = val[:, r * LANE:(r + 1) * LANE]

    whole = lambda m: pl.BlockSpec(m.shape, lambda i: (0, 0, 0))
    st = pl.BlockSpec((tq, rows, LANE), lambda i: (i, 0, 0))
    return pl.pallas_call(
        body, name=name, grid=(S // tq,),
        in_specs=[pl.BlockSpec((tq, W), lambda i: (i, 0)), whole(m_re), whole(m_im)], out_specs=[st, st],
        out_shape=[jax.ShapeDtypeStruct((S, rows, LANE), F32)] * 2,
        compiler_params=pltpu.CompilerParams(dimension_semantics=("parallel",)),
    )(x, m_re, m_im)


def _ssm_collect(name, z_re3, z_im3, m_re, m_im, dims, side, gain, epilogue, out_dtypes):
    S, rows, _ = z_re3.shape
    tiles = m_re.shape[0]
    W = side.shape[1]
    tch, per = W // tiles, rows // tiles
    tq = _pick(S, 256, SUBLANE)
    n_out = len(out_dtypes)

    def body(zre_ref, zim_ref, mre_ref, mim_ref, side_ref, gain_ref, *out_refs):
        for b in range(tiles):
            cols = slice(b * tch, (b + 1) * tch)
            zre, zim = _tile_states(zre_ref, b, per), _tile_states(zim_ref, b, per)
            acc = lax.dot_general(zre, mre_ref[b], (dims, ((), ())), preferred_element_type=F32)
            acc = acc + lax.dot_general(zim, mim_ref[b], (dims, ((), ())), preferred_element_type=F32)
            for o, v in zip(out_refs[:n_out], epilogue(acc, side_ref[:, cols], gain_ref[:, cols])):
                o[:, cols] = v.astype(o.dtype)
            out_refs[n_out][:, b * per * LANE:(b + 1) * per * LANE] = zre
            out_refs[n_out + 1][:, b * per * LANE:(b + 1) * per * LANE] = zim

    whole = lambda m: pl.BlockSpec(m.shape, lambda i: (0, 0, 0))
    st = pl.BlockSpec((tq, rows, LANE), lambda i: (i, 0, 0))
    ch = pl.BlockSpec((tq, W), lambda i: (i, 0))
    flat = pl.BlockSpec((tq, rows * LANE), lambda i: (i, 0))
    return pl.pallas_call(
        body, name=name, grid=(S // tq,),
        in_specs=[st, st, whole(m_re), whole(m_im), ch, pl.BlockSpec((1, W), lambda i: (0, 0))],
        out_specs=[ch] * n_out + [flat, flat],
        out_shape=[jax.ShapeDtypeStruct((S, W), dt) for dt in out_dtypes]
        + [jax.ShapeDtypeStruct((S, rows * LANE), BF)] * 2,
        compiler_params=pltpu.CompilerParams(dimension_semantics=("parallel",)),
    )(z_re3, z_im3, m_re, m_im, side, gain)


def _scan_fwd(bu_re, bu_im, a_re, a_im):
    S, R, _ = bu_re.shape
    tc = _pick(S, 256, SUBLANE)

    def body(bre, bim, are, aim, sre, sim, carry):
        @pl.when(pl.program_id(0) == 0)
        def _():
            carry[...] = jnp.zeros_like(carry)

        ar, ai = are[...], aim[...]

        def step(t, c):
            pr, pi = c
            nr = ar * pr - ai * pi + bre[t]
            ni = ar * pi + ai * pr + bim[t]
            sre[t] = nr
            sim[t] = ni
            return nr, ni

        pr, pi = lax.fori_loop(0, tc, step, (carry[0], carry[1]), unroll=8)
        carry[0] = pr
        carry[1] = pi

    blk = pl.BlockSpec((tc, R, LANE), lambda i: (i, 0, 0))
    par = pl.BlockSpec((R, LANE), lambda i: (0, 0))
    return pl.pallas_call(
        body, name="ssm_scan_fwd", grid=(S // tc,),
        in_specs=[blk, blk, par, par], out_specs=[blk, blk],
        out_shape=[jax.ShapeDtypeStruct((S, R, LANE), F32)] * 2,
        scratch_shapes=[pltpu.VMEM((2, R, LANE), F32)],
        compiler_params=pltpu.CompilerParams(dimension_semantics=("arbitrary",)),
    )(bu_re, bu_im, a_re, a_im)


def _scan_bwd(ds_re, ds_im, s_re, s_im, a_re, a_im):
    S, R, _ = ds_re.shape
    tc = _pick(S, 256, SUBLANE)
    nc = S // tc

    def body(dre, dim_, sre, sim, are, aim, lre, lim, dar, dai, carry):
        @pl.when(pl.program_id(0) == 0)
        def _():
            carry[...] = jnp.zeros_like(carry)
            dar[...] = jnp.zeros_like(dar)
            dai[...] = jnp.zeros_like(dai)

        ar, ai = are[...], aim[...]

        def step(tt, c):
            t = tc - 1 - tt
            lr, li, gr, gi = c
            sr, si = sre[t], sim[t]
            gr = gr + lr * sr + li * si
            gi = gi + li * sr - lr * si
            nlr = dre[t] + ar * lr + ai * li
            nli = dim_[t] + ar * li - ai * lr
            lre[t] = nlr
            lim[t] = nli
            return nlr, nli, gr, gi

        lr, li, gr, gi = lax.fori_loop(0, tc, step, (carry[0], carry[1], dar[...], dai[...]), unroll=8)
        carry[0] = lr
        carry[1] = li
        dar[...] = gr
        dai[...] = gi

    blk = pl.BlockSpec((tc, R, LANE), lambda i: (nc - 1 - i, 0, 0))
    par = pl.BlockSpec((R, LANE), lambda i: (0, 0))
    return pl.pallas_call(
        body, name="ssm_scan_bwd", grid=(nc,),
        in_specs=[blk, blk, blk, blk, par, par], out_specs=[blk, blk, par, par],
        out_shape=[jax.ShapeDtypeStruct((S, R, LANE), F32)] * 2 + [jax.ShapeDtypeStruct((R, LANE), F32)] * 2,
        scratch_shapes=[pltpu.VMEM((2, R, LANE), F32)],
        compiler_params=pltpu.CompilerParams(dimension_semantics=("arbitrary",)),
    )(ds_re, ds_im, s_re, s_im, a_re, a_im)


def _shift_down(z, k):
    t = lax.broadcasted_iota(I32, z.shape, 0)
    return jnp.where(t >= k, pltpu.roll(z, k, 0), 0.0)


def _shift_up(z, k):
    n = z.shape[0]
    t = lax.broadcasted_iota(I32, z.shape, 0)
    return jnp.where(t < n - k, pltpu.roll(z, n - k, 0), 0.0)


def _conv_fwd(proj, cw, cb):
    _, S, W = proj.shape
    ct = _pick(W, 256, LANE)

    def fn(bg, cg, val, w, b):
        z = cg * val
        conv = b + w[0:1] * _shift_down(z, 2) + w[1:2] * _shift_down(z, 1) + w[2:3] * z
        return bg * conv, conv

    sl = lambda s: (proj, (None, S, ct), lambda j, s=s: (s, 0, j))
    col = ((S, ct), lambda j: (0, j))
    return _ew("conv_fwd", (W // ct,),
               [sl(1), sl(2), sl(3), (cw, (3, ct), lambda j: (0, j)), (cb, (1, ct), lambda j: (0, j))],
               [((S, W), BF) + col, ((S, W), F32) + col], fn)


def _conv_bwd(dyb, proj, conv, cw):
    _, S, W = proj.shape
    ct = _pick(W, 256, LANE)

    def fn(dy, bg, cg, val, cv, w):
        z = cg * val
        z1, z2 = _shift_down(z, 1), _shift_down(z, 2)
        dconv = dy * bg
        dz = w[2:3] * dconv + w[1:2] * _shift_up(dconv, 1) + w[0:1] * _shift_up(dconv, 2)
        dw = jnp.concatenate([jnp.sum(dconv * z2, axis=0, keepdims=True), jnp.sum(dconv * z1, axis=0, keepdims=True),
                              jnp.sum(dconv * z, axis=0, keepdims=True)], axis=0)
        return dy * cv, dz * val, dz * cg, dw, jnp.sum(dconv, axis=0, keepdims=True)

    sl = lambda s: (proj, (None, S, ct), lambda j, s=s: (s, 0, j))
    col = ((S, ct), lambda j: (0, j))
    return _ew("conv_bwd", (W // ct,),
               [(dyb,) + col, sl(1), sl(2), sl(3), (conv,) + col, (cw, (3, ct), lambda j: (0, j))],
               [((S, W), BF) + col, ((S, W), BF) + col, ((S, W), BF) + col,
                ((3, W), F32, (3, ct), lambda j: (0, j)), ((1, W), F32, (1, ct), lambda j: (0, j))], fn)


def _plain(accs, sides):
    return accs


def kernel(x, ffn1_norm, ffn1_w_gate, ffn1_w_up, ffn1_w_down, mix_norm, w_in, ssm_lambda_re, ssm_lambda_im, ssm_log_dt, ssm_b_re, ssm_b_im, ssm_c_re, ssm_c_im, ssm_d, ssm_w_glu, ssm_b_glu, ssm_w_out, conv_w, conv_b, conv_w_out, w_o, ffn2_norm, ffn2_w_gate, ffn2_w_up, ffn2_w_down, final_norm, loss_target, m_ffn1_norm, m_ffn1_w_gate, m_ffn1_w_up, m_ffn1_w_down, m_mix_norm, m_w_in, m_ssm_lambda_re, m_ssm_lambda_im, m_ssm_log_dt, m_ssm_b_re, m_ssm_b_im, m_ssm_c_re, m_ssm_c_im, m_ssm_d, m_ssm_w_glu, m_ssm_b_glu, m_ssm_w_out, m_conv_w, m_conv_b, m_conv_w_out, m_w_o, m_ffn2_norm, m_ffn2_w_gate, m_ffn2_w_up, m_ffn2_w_down, m_final_norm, v_ffn1_norm, v_ffn1_w_gate, v_ffn1_w_up, v_ffn1_w_down, v_mix_norm, v_w_in, v_ssm_lambda_re, v_ssm_lambda_im, v_ssm_log_dt, v_ssm_b_re, v_ssm_b_im, v_ssm_c_re, v_ssm_c_im, v_ssm_d, v_ssm_w_glu, v_ssm_b_glu, v_ssm_w_out, v_conv_w, v_conv_b, v_conv_w_out, v_w_o, v_ffn2_norm, v_ffn2_w_gate, v_ffn2_w_up, v_ffn2_w_down, v_final_norm):
    P = dict(ffn1_norm=ffn1_norm, ffn1_w_gate=ffn1_w_gate, ffn1_w_up=ffn1_w_up, ffn1_w_down=ffn1_w_down, mix_norm=mix_norm, w_in=w_in, ssm_lambda_re=ssm_lambda_re, ssm_lambda_im=ssm_lambda_im, ssm_log_dt=ssm_log_dt, ssm_b_re=ssm_b_re, ssm_b_im=ssm_b_im, ssm_c_re=ssm_c_re, ssm_c_im=ssm_c_im, ssm_d=ssm_d, ssm_w_glu=ssm_w_glu, ssm_b_glu=ssm_b_glu, ssm_w_out=ssm_w_out, conv_w=conv_w, conv_b=conv_b, conv_w_out=conv_w_out, w_o=w_o, ffn2_norm=ffn2_norm, ffn2_w_gate=ffn2_w_gate, ffn2_w_up=ffn2_w_up, ffn2_w_down=ffn2_w_down, final_norm=final_norm)
    M = dict(ffn1_norm=m_ffn1_norm, ffn1_w_gate=m_ffn1_w_gate, ffn1_w_up=m_ffn1_w_up, ffn1_w_down=m_ffn1_w_down, mix_norm=m_mix_norm, w_in=m_w_in, ssm_lambda_re=m_ssm_lambda_re, ssm_lambda_im=m_ssm_lambda_im, ssm_log_dt=m_ssm_log_dt, ssm_b_re=m_ssm_b_re, ssm_b_im=m_ssm_b_im, ssm_c_re=m_ssm_c_re, ssm_c_im=m_ssm_c_im, ssm_d=m_ssm_d, ssm_w_glu=m_ssm_w_glu, ssm_b_glu=m_ssm_b_glu, ssm_w_out=m_ssm_w_out, conv_w=m_conv_w, conv_b=m_conv_b, conv_w_out=m_conv_w_out, w_o=m_w_o, ffn2_norm=m_ffn2_norm, ffn2_w_gate=m_ffn2_w_gate, ffn2_w_up=m_ffn2_w_up, ffn2_w_down=m_ffn2_w_down, final_norm=m_final_norm)
    V = dict(ffn1_norm=v_ffn1_norm, ffn1_w_gate=v_ffn1_w_gate, ffn1_w_up=v_ffn1_w_up, ffn1_w_down=v_ffn1_w_down, mix_norm=v_mix_norm, w_in=v_w_in, ssm_lambda_re=v_ssm_lambda_re, ssm_lambda_im=v_ssm_lambda_im, ssm_log_dt=v_ssm_log_dt, ssm_b_re=v_ssm_b_re, ssm_b_im=v_ssm_b_im, ssm_c_re=v_ssm_c_re, ssm_c_im=v_ssm_c_im, ssm_d=v_ssm_d, ssm_w_glu=v_ssm_w_glu, ssm_b_glu=v_ssm_b_glu, ssm_w_out=v_ssm_w_out, conv_w=v_conv_w, conv_b=v_conv_b, conv_w_out=v_conv_w_out, w_o=v_w_o, ffn2_norm=v_ffn2_norm, ffn2_w_gate=v_ffn2_w_gate, ffn2_w_up=v_ffn2_w_up, ffn2_w_down=v_ffn2_w_down, final_norm=v_final_norm)
    names = list(P)
    sharded = ["ffn1_w_gate", "ffn1_w_up", "ffn1_w_down", "w_in", "ssm_w_glu", "ssm_w_out", "conv_w_out", "w_o",
               "ffn2_w_gate", "ffn2_w_up", "ffn2_w_down"]
    replicated = [n for n in names if n not in sharded and n != "conv_w"]

    S, D = x.shape[1], x.shape[2]
    W = ssm_d.shape[0]
    Dc = D // NDEV
    G, N = ssm_lambda_re.shape
    rows = G * N // LANE
    xh = x.reshape(S, D)
    target = loss_target.reshape(S, D)
    xi, yi, ci = lax.axis_index("x"), lax.axis_index("y"), lax.axis_index("c")
    c_arr = jnp.reshape(ci, (1,)).astype(I32)
    q_arr = jnp.reshape(2 * xi + yi, (1,)).astype(I32)
    row = lambda v: v.reshape(1, -1)

    transposed = ("ffn1_w_gate", "ffn1_w_up", "ffn2_w_gate", "ffn2_w_up")
    local = lambda table, n: table[n].T if n in transposed else table[n]

    groups = [sharded[0:1], sharded[1:2], sharded[2:3], ["w_in", "conv_w"], sharded[4:8],
              sharded[8:9], sharded[9:10], sharded[10:11]]
    first_begun, second_begun, chain = {}, {}, []
    me = 4 * xi + 2 * yi + ci

    def gather_first(gi):
        hold = 0.0 * chain[0][0, 0] if chain else 0.0
        srcs = [conv_w + hold if n == "conv_w" else (local(P, n) + hold).astype(BF) for n in groups[gi]]
        lands = [lax.dynamic_update_slice(lax.empty((NDEV,) + s.shape, s.dtype), s[None], (me,) + (0,) * s.ndim)
                 for s in srcs]
        first_begun[gi] = _split_start("gather_first_start_%d" % gi, srcs, lands, _gather_first_copies, 3, chain[-1:])
        chain.append(first_begun[gi][4])

    def gather_second(gi, after):
        _, lands = _split_wait("gather_first_wait_%d" % gi, first_begun[gi], _gather_first_copies, after)
        second_begun[gi] = _split_start("gather_second_start_%d" % gi, [], lands, _gather_second_copies, 2, chain[-1:])
        chain.append(second_begun[gi][4])

    def gathered(gi, after):
        _, lands = _split_wait("gather_second_wait_%d" % gi, second_begun[gi], _gather_second_copies, after)
        return _gather_forward("gather_forward_%d" % gi, lands, chain[-1:])

    def fetch(gi, seconds, firsts):
        def get(after):
            for g in seconds:
                gather_second(g, after)
            for g in firsts:
                gather_first(g)
            return gathered(gi, after)
        return get

    tm = _pick(S, M_TILE, SUBLANE)
    th = _pick(S, M_TILE // 2, SUBLANE)
    tk = _pick(D, K_TILE, LANE)
    ts = _pick(S, K_TILE, SUBLANE)
    tn = _pick(D, 1024, LANE)

    gather_first(0)
    gather_first(1)
    u1, r1 = _rms_fwd("rms1", xh, row(ffn1_norm), deps=list(chain))

    hold = 0.0 * chain[0][0, 0]
    replicated = [n for n in replicated if n != "ffn1_norm"] + ["ffn1_norm"]
    tile = SUBLANE * LANE

    def as_rows(p):
        flat = p.reshape(-1).astype(F32)
        return jnp.pad(flat, (0, -flat.shape[0] % tile)).reshape(-1, LANE)

    def pack(parts):
        return jnp.concatenate([as_rows(p) for p in parts], axis=0)

    cw_zero = jnp.zeros((conv_w.shape[0], W), F32)
    packed_state = [pack([t[n] + hold for n in replicated[:-1]] + [cw_zero + fill, t["ffn1_norm"] + hold])
                    for t, fill in ((P, 0.0), (M, 0.0), (V, 1.0))]
    ssm_in = tuple(p + hold for p in (ssm_lambda_re, ssm_lambda_im, ssm_log_dt, ssm_b_re, ssm_b_im, ssm_c_re, ssm_c_im))
    (a_re, a_im, bd_re, bd_im, cd_re, cd_imn), ssm_vjp = jax.vjp(_ssm_params, *ssm_in)
    bd_re_b, bd_im_b, cd_re_b, cd_imn_b = (t.astype(BF) for t in (bd_re, bd_im, cd_re, cd_imn))
    gather_second(0, [u1, a_re, a_im, bd_re_b, bd_im_b, cd_re_b, cd_imn_b] + packed_state)
    gather_first(2)
    gather_second(1, u1)
    gather_first(3)
    (wg1,) = gathered(0, u1)
    h1, ffn1_saved, wu1, wd1 = _ffn_fwd("ffn1", u1, xh, wg1, lambda after: fetch(1, [2], [4])(after)[0],
                                        lambda after: fetch(2, [3], [5])(after)[0])
    u2, r2 = _rms_fwd("rms2", h1, row(mix_norm))
    w_in_f, cw_f = fetch(3, [4], [6])(u2)
    cw = jnp.transpose(cw_f, (1, 0, 2)).reshape(3, W)
    (proj,) = _mm(
        "in_proj", (NDEV, S // tm, 1, D // tk),
        [(u2, (tm, tk), lambda b, i, j, k: (i, k)), (w_in_f, (None, tk, W), lambda b, i, j, k: (b, k, 0))],
        [(0, 1, NN, 0)], [], [((NDEV, S, W), F32, (None, tm, W), lambda b, i, j, k: (b, i, 0))], _plain, [(tm, W)])

    v_f = proj[0]
    v_bf = v_f.astype(BF)
    bu_re3, bu_im3 = _ssm_spread("ssm_bu", v_bf, bd_re_b, bd_im_b, NN, rows)
    s_re3, s_im3 = _scan_fwd(bu_re3, bu_im3, a_re, a_im)

    def y0_epilogue(acc, v_tile, d_tile):
        y0 = acc + d_tile * v_tile
        return y0, _gelu(y0)

    y0, y1, s_re_b, s_im_b = _ssm_collect("ssm_y0", s_re3, s_im3, cd_re_b, cd_imn_b, NN, v_f, row(ssm_d),
                                          y0_epilogue, [F32, BF])

    tw = _pick(W, 512, LANE)
    w_glu_f, w_so, w_co, w_o_f = fetch(4, [5], [7])(y1)
    w_glu_f = w_glu_f.reshape(W, W)
    w_o_f = w_o_f.reshape(D, D)

    def glu_epilogue(accs, sides):
        q = accs[0] + sides[1]
        return q, _gelu(sides[0]) * _sigmoid(q)

    q_pre, y2 = _mm("ssm_glu", (1, S // tm, W // tw, 1),
                    [(y1, (tm, W), lambda b, i, j, k: (i, 0)), (w_glu_f, (W, tw), lambda b, i, j, k: (0, j))],
                    [(0, 1, NN, 0)],
                    [(y0, (tm, tw), lambda b, i, j, k: (i, j)), (row(ssm_b_glu), (1, tw), lambda b, i, j, k: (0, j))],
                    [((S, W), F32, (tm, tw), lambda b, i, j, k: (i, j)), ((S, W), BF, (tm, tw), lambda b, i, j, k: (i, j))],
                    glu_epilogue, [(tm, tw)])

    yb, conv = _conv_fwd(proj, cw, row(conv_b))

    per = W // Dc
    ga_blk = (proj, (None, tm, Dc), lambda b, i, j, k: (4 + b // per, i, b % per))
    gb_blk = (proj, (None, tm, Dc), lambda b, i, j, k: (6 + b // per, i, b % per))
    dc_out = ((S, D), BF, (tm, Dc), lambda b, i, j, k: (i, b))

    def merge_epilogue(accs, sides):
        za, zb = accs
        return _sigmoid(sides[0]) * za + _sigmoid(sides[1]) * zb, za, zb

    merged, z_a, z_b = _mm(
        "mix_merge", (NDEV, S // tm, 1, 1),
        [(y2, (tm, W), lambda b, i, j, k: (i, 0)), (yb, (tm, W), lambda b, i, j, k: (i, 0)),
         (w_so, (None, W, Dc), lambda b, i, j, k: (b, 0, 0)), (w_co, (None, W, Dc), lambda b, i, j, k: (b, 0, 0))],
        [(0, 2, NN, 0), (1, 3, NN, 1)], [ga_blk, gb_blk], [dc_out, dc_out, dc_out], merge_epilogue, [(tm, Dc)] * 2)

    (h2,) = _mm("mix_out", (1, S // tm, D // tn, D // tk),
                [(merged, (tm, tk), lambda b, i, j, k: (i, k)), (w_o_f, (tk, tn), lambda b, i, j, k: (k, j))],
                [(0, 1, NN, 0)], [(h1, (tm, tn), lambda b, i, j, k: (i, j))],
                [((S, D), F32, (tm, tn), lambda b, i, j, k: (i, j))],
                lambda accs, sides: [sides[0] + accs[0]], [(tm, tn)])

    u3, r3 = _rms_fwd("rms3", h2, row(ffn2_norm))
    (wg2,) = fetch(5, [6, 7], [])(u3)
    h3, ffn2_saved, wu2, wd2 = _ffn_fwd("ffn2", u3, h2, wg2, lambda after: gathered(6, after)[0],
                                        lambda after: gathered(7, after)[0])
    loss_vec, dh3, dh3_half, d_final_norm = _loss_head("loss_head", h3, row(final_norm), target)
    loss = lax.psum(loss_vec[0, 0], ("x", "y", "c"))
    loss_done = jnp.zeros((SUBLANE, LANE), F32) + loss

    grads, deltas, new_m, new_v = {}, {}, {}, {}

    def rs_sibling_start(tag, parts):
        lands = [lax.empty((4,) + p.shape[1:], p.dtype) for p in parts]
        return _split_start("rs_sibling_start_" + tag, parts, lands, _sibling_copies, 4)

    def rs_chips_start(tag, sibling_begun, after):
        parts, lands = _split_wait("rs_sibling_wait_" + tag, sibling_begun, _sibling_copies, after)
        sums = [_sum_sibling("rs_sum_%s_%d" % (tag, a), p, land, c_arr) for a, (p, land) in enumerate(zip(parts, lands))]
        lands2 = [lax.empty((3,) + sm.shape[1:], sm.dtype) for sm in sums]
        return _split_start("rs_chips_start_" + tag, sums, lands2, _chips_copies, 3)

    def rs_end(tag, group, begun, after):
        sums, lands2 = _split_wait("rs_chips_wait_" + tag, begun, _chips_copies, after)
        for n, sm, land2 in zip(group, sums, lands2):
            res = _finish_sharded("adamw_" + n, sm, land2, q_arr, local(P, n), local(M, n), local(V, n))
            grads[n], deltas[n], new_m[n], new_v[n] = [t.T if n in transposed else t for t in res]

    f2_dwd, f2_dwgu, f2_du = _ffn_bwd("ffn2", dh3_half, u3, ffn2_saved, wg2, wu2, wd2, deps=[loss_done])
    dwd2 = f2_dwd()
    dwg2, dwu2 = f2_dwgu()
    sib_ffn2 = rs_sibling_start("ffn2", [dwg2, dwu2, dwd2])
    du3 = f2_du(deps=[sib_ffn2[4]])
    dh2, dh2_b, d_ffn2_norm = _rms_bwd("rms3_bwd", du3, h2, r3, row(ffn2_norm), dh3, 1.0)
    rs_ffn2 = rs_chips_start("ffn2", sib_ffn2, dh2)

    dg_out = ((2, S, W), BF, (None, tm, Dc), lambda b, i, j, k: (j // per, i, j % per))
    ga_blk2 = (proj, (None, tm, Dc), lambda b, i, j, k: (4 + j // per, i, j % per))
    gb_blk2 = (proj, (None, tm, Dc), lambda b, i, j, k: (6 + j // per, i, j % per))
    dcj = lambda arr: (arr, (tm, Dc), lambda b, i, j, k: (i, j))
    dcj_out = ((S, D), BF, (tm, Dc), lambda b, i, j, k: (i, j))

    def dmerge_epilogue(accs, sides):
        dm = accs[0]
        sa, sb = _sigmoid(sides[0]), _sigmoid(sides[1])
        za, zb = sides[2].astype(F32), sides[3].astype(F32)
        return dm * sa, dm * sb, dm * za * sa * (1.0 - sa), dm * zb * sb * (1.0 - sb)

    dz_a, dz_b, dga, dgb = _mm(
        "mix_out_dx", (1, S // tm, NDEV, D // tk),
        [(dh2_b, (tm, tk), lambda b, i, j, k: (i, k)), (w_o_f, (Dc, tk), lambda b, i, j, k: (j, k))],
        [(0, 1, NT, 0)], [ga_blk2, gb_blk2, dcj(z_a), dcj(z_b)], [dcj_out, dcj_out, dg_out, dg_out],
        dmerge_epilogue, [(tm, Dc)], deps=[rs_ffn2[4]])

    td = _pick(D, M_TILE, LANE)
    (dw_o,) = _mm("mix_out_dw", (1, D // td, D // tn, S // ts),
                  [(merged, (ts, td), lambda b, i, j, k: (k, i)), (dh2_b, (ts, tn), lambda b, i, j, k: (k, j))],
                  [(0, 1, TN, 0)], [], [((D, D), BF, (td, tn), lambda b, i, j, k: (i, j))], _plain, [(td, tn)])

    wout = ((NDEV, W, Dc), BF, (None, W, Dc), lambda b, i, j, k: (b, 0, 0))
    dw_so, dw_co = _mm(
        "mix_merge_dw", (NDEV, 1, 1, S // ts),
        [(y2, (ts, W), lambda b, i, j, k: (k, 0)), (yb, (ts, W), lambda b, i, j, k: (k, 0)),
         (dz_a, (ts, Dc), lambda b, i, j, k: (k, b)), (dz_b, (ts, Dc), lambda b, i, j, k: (k, b))],
        [(0, 2, TN, 0), (1, 3, TN, 1)], [], [wout, wout], _plain, [(W, Dc)] * 2)

    def dglu_epilogue(accs, sides):
        dy2, dyb = accs
        sq = _sigmoid(sides[1])
        return dy2 * _gelu(sides[0]) * sq * (1.0 - sq), dy2 * sq, dyb

    full_w = lambda arr: (arr, (th, W), lambda b, i, j, k: (i, 0))
    full_w_out = lambda dt: ((S, W), dt, (th, W), lambda b, i, j, k: (i, 0))
    dq, dy1p, dyb = _mm(
        "mix_merge_dx", (1, S // th, 1, NDEV),
        [(dz_a, (th, Dc), lambda b, i, j, k: (i, k)), (dz_b, (th, Dc), lambda b, i, j, k: (i, k)),
         (w_so, (None, W, Dc), lambda b, i, j, k: (k, 0, 0)), (w_co, (None, W, Dc), lambda b, i, j, k: (k, 0, 0))],
        [(0, 2, NT, 0), (1, 3, NT, 1)], [full_w(y0), full_w(q_pre)], [full_w_out(BF), full_w_out(F32), full_w_out(F32)],
        dglu_epilogue, [(th, W)] * 2)

    def dy0_epilogue(accs, sides):
        dy0 = (sides[0] + accs[0]) * _gelu_grad(sides[1])
        return dy0, dy0

    wj = lambda arr: (arr, (tm, tw), lambda b, i, j, k: (i, j))
    dy0, dy0_b = _mm("ssm_glu_dx", (1, S // tm, W // tw, 1),
                     [(dq, (tm, W), lambda b, i, j, k: (i, 0)), (w_glu_f, (tw, W), lambda b, i, j, k: (j, 0))],
                     [(0, 1, NT, 0)], [wj(dy1p), wj(y0)],
                     [((S, W), F32, (tm, tw), lambda b, i, j, k: (i, j)), ((S, W), BF, (tm, tw), lambda b, i, j, k: (i, j))],
                     dy0_epilogue, [(tm, tw)])

    (dw_glu,) = _mm("ssm_glu_dw", (1, W // tw, 1, S // ts),
                    [(y1, (ts, tw), lambda b, i, j, k: (k, i)), (dq, (ts, W), lambda b, i, j, k: (k, 0))],
                    [(0, 1, TN, 0)], [], [((W, W), BF, (tw, W), lambda b, i, j, k: (i, 0))], _plain, [(tw, W)])

    tr = _pick(S, 256, SUBLANE)
    rw = ((tr, W), lambda i: (i, 0))
    vec_w = ((1, W), F32, (1, W), lambda i: (0, 0))
    d_b_glu, d_ssm_d = _ew(
        "ssm_colsums", (S // tr,), [(dq,) + rw, (dy0,) + rw, (proj, (None, tr, W), lambda i: (0, i, 0))],
        [vec_w, vec_w],
        lambda dqv, dyv, vv: (jnp.sum(dqv.astype(F32), axis=0, keepdims=True), jnp.sum(dyv * vv, axis=0, keepdims=True)),
        acc=(0, 1))

    ds_re3, ds_im3 = _ssm_spread("ssm_ds", dy0_b, cd_re_b, cd_imn_b, NT, rows)
    lam_re3, lam_im3, da_re, da_im = _scan_bwd(ds_re3, ds_im3, s_re3, s_im3, a_re, a_im)
    dv, lam_re_b, lam_im_b = _ssm_collect("ssm_dv", lam_re3, lam_im3, bd_re_b, bd_im_b, NT, dy0, row(ssm_d),
                                          lambda acc, dy_tile, d_tile: [acc + dy_tile * d_tile], [BF])
    tiles, tch, tst = bd_re.shape
    tok_ch = lambda arr: (arr, (ts, tch), lambda b, i, j, k: (k, b))
    tok_st = lambda arr: (arr, (ts, tst), lambda b, i, j, k: (k, b))
    bd_out = ((tiles, tch, tst), F32, (None, tch, tst), lambda b, i, j, k: (b, 0, 0))
    cd_out = ((tiles, tst, tch), F32, (None, tst, tch), lambda b, i, j, k: (b, 0, 0))
    dbd_re, dbd_im = _mm("ssm_dbd", (tiles, 1, 1, S // ts), [tok_ch(v_bf), tok_st(lam_re_b), tok_st(lam_im_b)],
                         [(0, 1, TN, 0), (0, 2, TN, 1)], [], [bd_out, bd_out], _plain, [(tch, tst)] * 2)
    dcd_re, dcd_imn = _mm("ssm_dcd", (tiles, 1, 1, S // ts), [tok_st(s_re_b), tok_st(s_im_b), tok_ch(dy0_b)],
                          [(0, 2, TN, 0), (1, 2, TN, 1)], [], [cd_out, cd_out], _plain, [(tst, tch)] * 2)
    d_ssm = ssm_vjp((da_re, da_im, dbd_re, dbd_im, dcd_re, dcd_imn))

    dbg, dcg, dval, d_conv_w_full, d_conv_b = _conv_bwd(dyb, proj, conv, cw)
    dproj = jnp.concatenate([dv[None], dbg[None], dcg[None], dval[None], dga, dgb], axis=0)

    (dw_in,) = _mm("in_proj_dw", (NDEV, D // td, 1, S // ts),
                   [(u2, (ts, td), lambda b, i, j, k: (k, i)), (dproj, (None, ts, W), lambda b, i, j, k: (b, k, 0))],
                   [(0, 1, TN, 0)], [], [((NDEV, D, W), BF, (None, td, W), lambda b, i, j, k: (b, i, 0))],
                   _plain, [(td, W)])
    sib_mixer = rs_sibling_start(
        "mixer", [dw_in, dw_glu.reshape(NDEV, W // NDEV, W), dw_so, dw_co, dw_o.reshape(NDEV, Dc, D)])
    (du2,) = _mm("in_proj_dx", (1, S // tm, D // tn, NDEV // K_SHARDS),
                 [(dproj, (K_SHARDS, tm, W), lambda b, i, j, k: (k, i, 0)),
                  (w_in_f, (K_SHARDS, tn, W), lambda b, i, j, k: (k, j, 0))],
                 [(0, 1, NT, 0)], [], [((S, D), F32, (tm, tn), lambda b, i, j, k: (i, j))], _plain, [(tm, tn)],
                 deps=[sib_mixer[4]])
    dh1, dh1_half, d_mix_norm = _rms_bwd("rms2_bwd", du2, h1, r2, row(mix_norm), dh2, 0.5)
    rs_mixer = rs_chips_start("mixer", sib_mixer, dh1)

    small = dict(mix_norm=d_mix_norm, ffn2_norm=d_ffn2_norm, final_norm=d_final_norm,
                 ssm_lambda_re=d_ssm[0], ssm_lambda_im=d_ssm[1], ssm_log_dt=d_ssm[2], ssm_b_re=d_ssm[3],
                 ssm_b_im=d_ssm[4], ssm_c_re=d_ssm[5], ssm_c_im=d_ssm[6], ssm_d=d_ssm_d, ssm_b_glu=d_b_glu,
                 conv_b=d_conv_b)
    early_pk = pack([small[n] for n in replicated[:-1]] + [d_conv_w_full])
    early_land = lax.dynamic_update_slice(jnp.zeros((NDEV,) + early_pk.shape, F32), early_pk[None], (me, 0, 0))
    small_begun = _split_start("gather_small_start", [early_pk], [early_land], _everyone_copies, NDEV - 1)

    f1_dwd, f1_dwgu, f1_du = _ffn_bwd("ffn1", dh1_half, u1, ffn1_saved, wg1, wu1, wd1,
                                      deps=[rs_mixer[4], small_begun[4]])
    du1 = f1_du()
    dx, _, d_ffn1_norm = _rms_bwd("rms1_bwd", du1, xh, r1, row(ffn1_norm), dh1, 1.0)
    dwd1 = f1_dwd(deps=[d_ffn1_norm])
    late = d_ffn1_norm + 0.0 * dwd1[0, :1, :1].astype(F32)
    late_pk = late.reshape(-1, LANE)
    late_land = lax.dynamic_update_slice(jnp.zeros((NDEV,) + late_pk.shape, F32), late_pk[None], (me, 0, 0))
    late_begun = _split_start("gather_late_start", [late_pk], [late_land], _everyone_copies, NDEV - 1)
    rs_ffn1_down = rs_chips_start("ffn1_down", rs_sibling_start("ffn1_down", [dwd1]), late_begun[4])
    dwg1, dwu1 = f1_dwgu(deps=[rs_ffn1_down[4]])
    rs_ffn1_gate = rs_chips_start("ffn1_gate", rs_sibling_start("ffn1_gate", [dwg1]), None)
    rs_ffn1_up = rs_chips_start("ffn1_up", rs_sibling_start("ffn1_up", [dwu1]), rs_ffn1_gate[4])
    rs_end("ffn2", sharded[8:11], rs_ffn2, rs_ffn1_up[4])
    rs_end("mixer", sharded[3:8], rs_mixer, [grads[n] for n in sharded[8:11]])
    _, (early_all,) = _split_wait("gather_small_wait", small_begun, _everyone_copies, [grads[n] for n in sharded[3:8]])
    _, (late_all,) = _split_wait("gather_late_wait", late_begun, _everyone_copies, early_all)
    small_all = jnp.concatenate([early_all, late_all], axis=1)
    g_pk, d_pk, m_pk, v_pk = _finish_replicated("adamw_replicated", small_all, *packed_state)

    def unpack(pk, r0, like):
        nr = -(-like.size // tile) * SUBLANE
        return pk[r0:r0 + nr].reshape(-1)[:like.size].reshape(like.shape), r0 + nr

    r0 = 0
    for n in replicated[:-1] + ["conv_w", "ffn1_norm"]:
        like = d_conv_w_full if n == "conv_w" else P[n]
        for store, pk in ((grads, g_pk), (deltas, d_pk), (new_m, m_pk), (new_v, v_pk)):
            store[n], r1 = unpack(pk, r0, like)
        r0 = r1
    g_cw_full = grads["conv_w"]
    cwl = conv_w.shape[1]
    g_cw = lax.dynamic_slice_in_dim(g_cw_full, me * cwl, cwl, axis=1)
    full3 = ((3, cwl), lambda i: (0, 0))
    grads["conv_w"], deltas["conv_w"], new_m["conv_w"], new_v["conv_w"] = _ew(
        "adamw_conv_w", (1,), [(g_cw,) + full3, (conv_w,) + full3, (m_conv_w,) + full3, (v_conv_w,) + full3],
        [((3, cwl), F32) + full3] * 4, lambda g, w, m, v: (g,) + _adamw(w, g, m, v))
    rs_end("ffn1_down", sharded[2:3], rs_ffn1_down, [g_pk, grads["conv_w"]])
    rs_end("ffn1_gate", sharded[0:1], rs_ffn1_gate, grads["ffn1_w_down"])
    rs_end("ffn1_up", sharded[1:2], rs_ffn1_up, grads["ffn1_w_gate"])

    return (loss, dx.reshape(x.shape), *[grads[n] for n in names], *[deltas[n] for n in names],
            *[new_m[n] for n in names], *[new_v[n] for n in names])
```

```python
import math

import jax
import jax.numpy as jnp
from jax import lax
from jax.experimental import pallas as pl
from jax.experimental.pallas import tpu as pltpu

F32 = jnp.float32
BF = jnp.bfloat16
I32 = jnp.int32
MESH = pl.DeviceIdType.MESH
LANE = 128
SUBLANE = 8
NDEV = 8
EW_BLOCK = 256 * 1024
M_TILE = 1024
K_TILE = 2048
K_SHARDS = 2
DMA_CHUNKS = 4
EPS = 1e-6
ADAM_LR, ADAM_B1, ADAM_B2, ADAM_EPS, ADAM_WD, ADAM_STEP = 0.001, 0.9, 0.999, 1e-08, 0.01, 10
NN = ((1,), (0,))
NT = ((1,), (1,))
TN = ((0,), (0,))
HBM = pl.BlockSpec(memory_space=pltpu.HBM)


def _pick(n, pref, mult):
    t = min(pref, n)
    t -= t % mult
    while t >= mult:
        if n % t == 0:
            return t
        t -= mult
    return n


def _sigmoid(x):
    return 1.0 / (1.0 + jnp.exp(-x))


_GELU_C = math.sqrt(2.0 / math.pi)


def _gelu(x):
    return 0.5 * x * (1.0 + jnp.tanh(_GELU_C * (x + 0.044715 * x * x * x)))


def _gelu_grad(x):
    t = jnp.tanh(_GELU_C * (x + 0.044715 * x * x * x))
    return 0.5 * (1.0 + t) + 0.5 * x * (1.0 - t * t) * _GELU_C * (1.0 + 3.0 * 0.044715 * x * x)


def _dep_specs(deps, rank):
    return [(d, d.shape, lambda *_, nd=d.ndim: (0,) * nd) for d in deps]


def _mm(name, grid, ops, pairs, sides, outs, epilogue, acc_shapes, deps=()):
    nk = grid[-1]
    n_ops, n_sides, n_outs = len(ops), len(sides), len(outs)
    dep_specs = _dep_specs(deps, len(grid))
    n_deps = len(dep_specs)

    def body(*refs):
        op_refs = refs[:n_ops]
        side_refs = refs[n_ops:n_ops + n_sides]
        out_refs = refs[n_ops + n_sides + n_deps:n_ops + n_sides + n_deps + n_outs]
        acc_refs = refs[n_ops + n_sides + n_deps + n_outs:]

        def partials():
            res = [None] * len(acc_shapes)
            for ia, ib, dims, ai in pairs:
                a_ref, b_ref = op_refs[ia], op_refs[ib]
                for s in range(a_ref.shape[0] if len(a_ref.shape) == 3 else 1):
                    a, b = (a_ref[s], b_ref[s]) if len(a_ref.shape) == 3 else (a_ref[...], b_ref[...])
                    p = lax.dot_general(a, b, (dims, ((), ())), preferred_element_type=F32)
                    res[ai] = p if res[ai] is None else res[ai] + p
            return res

        def finish(accs):
            vals = epilogue(accs, [s[...] for s in side_refs])
            for o, v in zip(out_refs, vals):
                o[...] = v.astype(o.dtype)

        if nk == 1:
            finish(partials())
        else:
            k = pl.program_id(len(grid) - 1)

            @pl.when(k == 0)
            def _():
                for a, p in zip(acc_refs, partials()):
                    a[...] = p

            @pl.when(k > 0)
            def _():
                for a, p in zip(acc_refs, partials()):
                    a[...] += p

            @pl.when(k == nk - 1)
            def _():
                finish([a[...] for a in acc_refs])

    return pl.pallas_call(
        body, name=name, grid=grid,
        in_specs=[pl.BlockSpec(b, m) for (_, b, m) in list(ops) + list(sides) + dep_specs],
        out_specs=[pl.BlockSpec(b, m) for (_, _, b, m) in outs],
        out_shape=[jax.ShapeDtypeStruct(s, d) for (s, d, _, _) in outs],
        scratch_shapes=[pltpu.VMEM(s, F32) for s in acc_shapes] if nk > 1 else [],
        compiler_params=pltpu.CompilerParams(
            dimension_semantics=("parallel",) * (len(grid) - 1) + ("arbitrary",)),
    )(*[a for (a, _, _) in list(ops) + list(sides) + dep_specs])


def _ew(name, grid, ins, outs, fn, acc=(), deps=()):
    n_in = len(ins)
    dep_specs = _dep_specs(deps, len(grid))

    def body(*refs):
        vals = fn(*[r[...] for r in refs[:n_in]])
        first = pl.program_id(0) == 0
        for idx, (o, v) in enumerate(zip(refs[n_in + len(dep_specs):], vals)):
            if idx in acc:
                @pl.when(first)
                def _(o=o, v=v):
                    o[...] = v.astype(o.dtype)

                @pl.when(jnp.logical_not(first))
                def _(o=o, v=v):
                    o[...] += v.astype(o.dtype)
            else:
                o[...] = v.astype(o.dtype)

    return pl.pallas_call(
        body, name=name, grid=grid,
        in_specs=[pl.BlockSpec(b, m) for (_, b, m) in list(ins) + dep_specs],
        out_specs=[pl.BlockSpec(b, m) for (_, _, b, m) in outs],
        out_shape=[jax.ShapeDtypeStruct(s, d) for (s, d, _, _) in outs],
        compiler_params=pltpu.CompilerParams(
            dimension_semantics=(("arbitrary",) if acc else ("parallel",)) * len(grid)),
    )(*[a for (a, _, _) in list(ins) + dep_specs])


def _position():
    x, y, c = lax.axis_index("x"), lax.axis_index("y"), lax.axis_index("c")
    chips = [(1 - x, y), (x, 1 - y), (1 - x, 1 - y)]
    return x, y, c, chips


SEM = pl.BlockSpec(memory_space=pltpu.SEMAPHORE)
EFFECT = pltpu.SideEffectType.DATAFLOW_SIDE_EFFECTING


def _in_hbm(v):
    return pltpu.with_memory_space_constraint(v, pltpu.HBM)


def _split_start(name, srcs, lands, make_copies, n_per, after=()):
    n, nb = len(srcs), len(srcs) + len(lands)
    n_sems = len(lands) * n_per
    after = list(after)

    def body(*refs):
        send_sems, recv_sems = refs[nb + len(after)], refs[nb + len(after) + 1]
        for cp in make_copies(refs[:n], refs[n:nb], send_sems, recv_sems):
            cp.start()
        refs[-1][...] = jnp.zeros_like(refs[-1])

    outs = pl.pallas_call(
        body, name=name,
        out_shape=(pltpu.SemaphoreType.DMA((n_sems,)), pltpu.SemaphoreType.DMA((n_sems,)),
                   *[pltpu.HBM(v.shape, v.dtype) for v in list(srcs) + list(lands)],
                   jax.ShapeDtypeStruct((SUBLANE, LANE), F32)),
        in_specs=[HBM] * nb + [pl.BlockSpec(memory_space=pl.ANY)] * len(after),
        out_specs=(SEM, SEM, *[HBM] * nb, pl.BlockSpec(memory_space=pltpu.VMEM)),
        input_output_aliases={i: 2 + i for i in range(nb)},
        compiler_params=pltpu.CompilerParams(has_side_effects=EFFECT),
    )(*[_in_hbm(v) for v in list(srcs) + list(lands)], *after)
    return outs[0], outs[1], list(outs[2:2 + n]), list(outs[2 + n:2 + nb]), outs[-1]


def _split_wait(name, started, make_copies, after):
    send_sems, recv_sems, srcs, lands, _ = started
    n, nb = len(srcs), len(srcs) + len(lands)

    def body(*refs):
        for cp in make_copies(refs[:n], refs[n:nb], refs[nb], refs[nb + 1]):
            cp.wait_send()
            cp.wait_recv()

    order = [] if after is None else list(after) if isinstance(after, (list, tuple)) else [after]
    outs = pl.pallas_call(
        body, name=name,
        out_shape=tuple(pltpu.HBM(v.shape, v.dtype) for v in srcs + lands),
        in_specs=[HBM] * nb + [SEM, SEM] + [pl.BlockSpec(memory_space=pl.ANY)] * len(order),
        out_specs=tuple([HBM] * nb),
        input_output_aliases={i: i for i in range(nb)},
        compiler_params=pltpu.CompilerParams(has_side_effects=EFFECT),
    )(*srcs, *lands, send_sems, recv_sems, *order)
    return list(outs[:n]), list(outs[n:])


def _gather_first_copies(xs, lands, send_sems, recv_sems):
    x, y, c, _ = _position()
    copies = []
    for a in range(len(xs)):
        for k, peer in enumerate([(x, y, 1 - c), (1 - x, y, c), (x, 1 - y, c)]):
            copies.append(pltpu.make_async_remote_copy(
                src_ref=xs[a], dst_ref=lands[a].at[4 * x + 2 * y + c],
                send_sem=send_sems.at[3 * a + k], recv_sem=recv_sems.at[3 * a + k], device_id=peer, device_id_type=MESH))
    return copies


def _gather_second_copies(xs, lands, send_sems, recv_sems):
    x, y, c, _ = _position()
    copies = []
    for a in range(len(lands)):
        rows = lands[a].shape[1]
        unit = SUBLANE * (4 // jnp.dtype(lands[a].dtype).itemsize)
        half = rows // 2 // unit * unit or rows
        parts = [((1 - x, y), (x, 1 - y), pl.ds(0, half))]
        if half < rows:
            parts.append(((x, 1 - y), (1 - x, y), pl.ds(half, rows - half)))
        for k, (block, to, rs) in enumerate(parts):
            ref = lands[a].at[4 * block[0] + 2 * block[1] + c, rs]
            copies.append(pltpu.make_async_remote_copy(
                src_ref=ref, dst_ref=ref, send_sem=send_sems.at[2 * a + k], recv_sem=recv_sems.at[2 * a + k],
                device_id=(*to, c), device_id_type=MESH))
    return copies


def _chips_copies(ps, lands, send_sems, recv_sems):
    x, y, c, chips = _position()
    copies = []
    for a in range(len(ps)):
        for j, chip in enumerate(chips):
            copies.append(pltpu.make_async_remote_copy(
                src_ref=ps[a].at[2 * chip[0] + chip[1]], dst_ref=lands[a].at[j],
                send_sem=send_sems.at[3 * a + j], recv_sem=recv_sems.at[3 * a + j], device_id=(*chip, c),
                device_id_type=MESH))
    return copies


def _sibling_copies(gs, lands, send_sems, recv_sems):
    x, y, c, _ = _position()
    copies = []
    for a in range(len(gs)):
        for q in range(4):
            copies.append(pltpu.make_async_remote_copy(
                src_ref=gs[a].at[2 * q + 1 - c], dst_ref=lands[a].at[q],
                send_sem=send_sems.at[4 * a + q], recv_sem=recv_sems.at[4 * a + q],
                device_id=(x, y, 1 - c), device_id_type=MESH))
    return copies


def _everyone_copies(xs, lands, send_sems, recv_sems):
    x, y, c, _ = _position()
    flip = lambda v, bit: 1 - v if bit else v
    copies = []
    for a in range(len(xs)):
        for k in range(1, NDEV):
            copies.append(pltpu.make_async_remote_copy(
                src_ref=xs[a], dst_ref=lands[a].at[4 * x + 2 * y + c],
                send_sem=send_sems.at[7 * a + k - 1], recv_sem=recv_sems.at[7 * a + k - 1],
                device_id=(flip(x, k & 4), flip(y, k & 2), flip(c, k & 1)), device_id_type=MESH))
    return copies


def _row_chunks(rows, dtype):
    unit = SUBLANE * (4 // jnp.dtype(dtype).itemsize)
    units = rows // unit
    if rows % unit or units < 2:
        return [(0, rows)]
    k = min(DMA_CHUNKS, units)
    sizes = [(units // k + (1 if i < units % k else 0)) * unit for i in range(k)]
    return [(sum(sizes[:i]), sz) for i, sz in enumerate(sizes)]


def _gather_forward(name, lands, after=()):
    n = len(lands)
    after = list(after)

    def body(*refs):
        ins, outs = refs[:n], refs[n + len(after):2 * n + len(after)]
        send_sems, recv_sems = refs[2 * n + len(after):]
        x, y, c, chips = _position()
        whole, chunks = [], []
        for a in range(n):
            rows = _row_chunks(ins[a].shape[1], ins[a].dtype)
            for j, chip in enumerate(chips):
                slot = 4 * chip[0] + 2 * chip[1]

                def to_sibling(src, dst):
                    return pltpu.make_async_remote_copy(
                        src_ref=src, dst_ref=dst, send_sem=send_sems.at[a, j], recv_sem=recv_sems.at[a, j],
                        device_id=(x, y, 1 - c), device_id_type=MESH)

                whole.append(to_sibling(ins[a].at[slot + c], outs[a].at[slot + 1 - c]))
                chunks += [to_sibling(ins[a].at[slot + c, pl.ds(r0, nr)], outs[a].at[slot + c, pl.ds(r0, nr)])
                           for r0, nr in rows]
        for cp in chunks:
            cp.start()
        for cp in whole:
            cp.wait()

    return pl.pallas_call(
        body, name=name,
        out_shape=[jax.ShapeDtypeStruct(l.shape, l.dtype) for l in lands],
        in_specs=[HBM] * n + [pl.BlockSpec(memory_space=pl.ANY)] * len(after), out_specs=[HBM] * n,
        input_output_aliases={a: a for a in range(n)},
        scratch_shapes=[pltpu.SemaphoreType.DMA((n, 3)), pltpu.SemaphoreType.DMA((n, 3))],
    )(*lands, *after)


def _sum_sibling(name, g, land, c_arr):
    _, R, C = g.shape
    tr = _pick(R, 512, SUBLANE)

    def body(c_ref, g_ref, l_ref, o_ref):
        o_ref[...] = (g_ref[...].astype(F32) + l_ref[...].astype(F32)).astype(o_ref.dtype)

    return pl.pallas_call(
        body, name=name,
        grid_spec=pltpu.PrefetchScalarGridSpec(
            num_scalar_prefetch=1, grid=(4, R // tr),
            in_specs=[pl.BlockSpec((None, tr, C), lambda q, i, cr: (2 * q + cr[0], i, 0)),
                      pl.BlockSpec((None, tr, C), lambda q, i, cr: (q, i, 0))],
            out_specs=pl.BlockSpec((None, tr, C), lambda q, i, cr: (q, i, 0))),
        out_shape=jax.ShapeDtypeStruct((4, R, C), g.dtype),
        compiler_params=pltpu.CompilerParams(dimension_semantics=("parallel", "parallel")),
    )(c_arr, g, land)


def _adamw(w, g, m, v):
    m = ADAM_B1 * m + (1.0 - ADAM_B1) * g
    v = ADAM_B2 * v + (1.0 - ADAM_B2) * (g * g)
    m_hat = m / (1.0 - ADAM_B1 ** ADAM_STEP)
    v_hat = v / (1.0 - ADAM_B2 ** ADAM_STEP)
    delta = -ADAM_LR * (m_hat / (jnp.sqrt(v_hat) + ADAM_EPS) + ADAM_WD * w)
    return delta, m, v


def _finish_sharded(name, sums, land, q_arr, w, m, v):
    R, C = w.shape
    tr = _pick(R, 512, SUBLANE)
    tc = _pick(C, max(LANE, EW_BLOCK // tr), LANE)

    def body(q_ref, p_ref, l_ref, w_ref, m_ref, v_ref, g_out, d_out, m_out, v_out):
        g = p_ref[...].astype(F32)
        for j in range(3):
            g = g + l_ref[j].astype(F32)
        d, mn, vn = _adamw(w_ref[...], g, m_ref[...], v_ref[...])
        g_out[...] = g
        d_out[...] = d
        m_out[...] = mn
        v_out[...] = vn

    blk = pl.BlockSpec((tr, tc), lambda i, j, qr: (i, j))
    return pl.pallas_call(
        body, name=name,
        grid_spec=pltpu.PrefetchScalarGridSpec(
            num_scalar_prefetch=1, grid=(R // tr, C // tc),
            in_specs=[pl.BlockSpec((None, tr, tc), lambda i, j, qr: (qr[0], i, j)),
                      pl.BlockSpec((3, tr, tc), lambda i, j, qr: (0, i, j)), blk, blk, blk],
            out_specs=[blk] * 4),
        out_shape=[jax.ShapeDtypeStruct((R, C), F32)] * 4,
        compiler_params=pltpu.CompilerParams(dimension_semantics=("parallel", "parallel")),
    )(q_arr, sums, land, w, m, v)


def _finish_replicated(name, gathered, w, m, v):
    _, R, C = gathered.shape
    tr = _pick(R, 256, SUBLANE)

    def fn(gv, wv, mv, vv):
        g = gv[0]
        for d in range(1, NDEV):
            g = g + gv[d]
        dl, mn, vn = _adamw(wv, g, mv, vv)
        return g, dl, mn, vn

    row = ((tr, C), lambda i: (i, 0))
    return _ew(name, (R // tr,),
               [(gathered, (NDEV, tr, C), lambda i: (0, i, 0)), (w,) + row, (m,) + row, (v,) + row],
               [((R, C), F32) + row] * 4, fn)


def _rms_fwd(name, h, g, deps=()):
    S, D = h.shape
    tr = _pick(S, 512, SUBLANE)

    def fn(hv, gv):
        r = lax.rsqrt(jnp.mean(hv * hv, axis=-1, keepdims=True) + EPS)
        return hv * r * gv, r

    return _ew(name, (S // tr,),
               [(h, (tr, D), lambda i: (i, 0)), (g, (1, D), lambda i: (0, 0))],
               [((S, D), BF, (tr, D), lambda i: (i, 0)), ((S, 1), F32, (tr, 1), lambda i: (i, 0))], fn, deps=deps)


def _rms_bwd(name, du, h, r, g, dres, scale):
    S, D = h.shape
    tr = _pick(S, 512, SUBLANE)

    def fn(duv, hv, rv, gv, drv):
        xn = hv * rv
        dxn = duv * gv
        dh = drv + rv * (dxn - xn * jnp.mean(dxn * xn, axis=-1, keepdims=True))
        return dh, scale * dh, jnp.sum(duv * xn, axis=0, keepdims=True)

    row = ((tr, D), lambda i: (i, 0))
    return _ew(name, (S // tr,),
               [(du,) + row, (h,) + row, (r, (tr, 1), lambda i: (i, 0)), (g, (1, D), lambda i: (0, 0)), (dres,) + row],
               [((S, D), F32) + row, ((S, D), BF) + row, ((1, D), F32, (1, D), lambda i: (0, 0))], fn, acc=(2,))


def _loss_head(name, h, g, target):
    S, D = h.shape
    tr = _pick(S, 512, SUBLANE)

    def fn(hv, gv, tv):
        r = lax.rsqrt(jnp.mean(hv * hv, axis=-1, keepdims=True) + EPS)
        xn = hv * r
        diff = xn * gv - tv
        loss = 0.5 * jnp.sum(jnp.mean(diff * diff, axis=-1, keepdims=True))
        dout = diff / D
        dxn = dout * gv
        dh = r * (dxn - xn * jnp.mean(dxn * xn, axis=-1, keepdims=True))
        return (jnp.zeros((1, LANE), F32) + loss, dh, 0.5 * dh, jnp.sum(dout * xn, axis=0, keepdims=True))

    row = ((tr, D), lambda i: (i, 0))
    return _ew(name, (S // tr,),
               [(h,) + row, (g, (1, D), lambda i: (0, 0)), (target,) + row],
               [((1, LANE), F32, (1, LANE), lambda i: (0, 0)), ((S, D), F32) + row, ((S, D), BF) + row,
                ((1, D), F32, (1, D), lambda i: (0, 0))], fn, acc=(0, 3))


def _ffn_fwd(tag, u, h, wg, fetch_wu, fetch_wd):
    S, D = u.shape
    Fs = wg.shape[1]
    tm, tk = _pick(S, M_TILE, SUBLANE), _pick(D, K_TILE, LANE)
    act = ((NDEV, S, Fs), BF, (None, tm, Fs), lambda b, i, j, k: (b, i, 0))
    lhs = (u, (tm, tk), lambda b, i, j, k: (i, k))
    rhs = lambda w: (w, (None, Fs, tk), lambda b, i, j, k: (b, 0, k))
    (gt,) = _mm(tag + "_gate", (NDEV, S // tm, 1, D // tk), [lhs, rhs(wg)], [(0, 1, NT, 0)], [], [act],
                lambda accs, sides: accs, [(tm, Fs)])
    wu = fetch_wu(gt)

    def up_epilogue(accs, sides):
        g = sides[0].astype(F32)
        return accs[0], g * _sigmoid(g) * accs[0]

    up, a = _mm(tag + "_up", (NDEV, S // tm, 1, D // tk), [lhs, rhs(wu)], [(0, 1, NT, 0)],
                [(gt, (None, tm, Fs), lambda b, i, j, k: (b, i, 0))], [act, act], up_epilogue, [(tm, Fs)])
    wd = fetch_wd(a)
    tn = _pick(D, 1024, LANE)
    (hn,) = _mm(
        tag + "_down", (1, S // tm, D // tn, NDEV // K_SHARDS),
        [(a, (K_SHARDS, tm, Fs), lambda b, i, j, k: (k, i, 0)), (wd, (K_SHARDS, Fs, tn), lambda b, i, j, k: (k, 0, j))],
        [(0, 1, NN, 0)], [(h, (tm, tn), lambda b, i, j, k: (i, j))],
        [((S, D), F32, (tm, tn), lambda b, i, j, k: (i, j))],
        lambda accs, sides: [sides[0] + 0.5 * accs[0]], [(tm, tn)])
    return hn, (gt, up, a), wu, wd


def _ffn_bwd(tag, dhs, u, saved, wg, wu, wd, deps=()):
    gt, up, a = saved
    S, D = u.shape
    Fs = wg.shape[1]
    tm, tk = _pick(S, M_TILE, SUBLANE), _pick(D, K_TILE, LANE)
    act_in = lambda arr: (arr, (None, tm, Fs), lambda b, i, j, k: (b, i, 0))
    act_out = ((NDEV, S, Fs), BF, (None, tm, Fs), lambda b, i, j, k: (b, i, 0))

    def act_epilogue(accs, sides):
        da = accs[0]
        gtv, upv = sides[0].astype(F32), sides[1].astype(F32)
        sg = _sigmoid(gtv)
        return da * upv * sg * (1.0 + gtv * (1.0 - sg)), da * gtv * sg

    dgt, dup = _mm(
        tag + "_dact", (NDEV, S // tm, 1, D // tk),
        [(dhs, (tm, tk), lambda b, i, j, k: (i, k)), (wd, (None, Fs, tk), lambda b, i, j, k: (b, 0, k))],
        [(0, 1, NT, 0)], [act_in(gt), act_in(up)], [act_out, act_out], act_epilogue, [(tm, Fs)], deps=deps)

    ts = _pick(S, K_TILE, SUBLANE)
    tn = _pick(D, 1024, LANE)
    wgrad = ((NDEV, Fs, D), BF, (None, Fs, tn), lambda b, i, j, k: (b, 0, j))
    tok = lambda arr: (arr, (None, ts, Fs), lambda b, i, j, k: (b, k, 0))

    def grad_down(deps=()):
        return _mm(
            tag + "_dwd", (NDEV, 1, D // tn, S // ts),
            [tok(a), (dhs, (ts, tn), lambda b, i, j, k: (k, j))],
            [(0, 1, TN, 0)], [], [wgrad], lambda accs, sides: accs, [(Fs, tn)], deps=deps)[0]

    def grad_gate_up(deps=()):
        return _mm(
            tag + "_dwgu", (NDEV, 1, D // tn, S // ts),
            [tok(dgt), tok(dup), (u, (ts, tn), lambda b, i, j, k: (k, j))],
            [(0, 2, TN, 0), (1, 2, TN, 1)], [], [wgrad, wgrad], lambda accs, sides: accs, [(Fs, tn), (Fs, tn)],
            deps=deps)

    def du(deps=()):
        return _mm(
            tag + "_du", (1, S // tm, D // tn, NDEV // K_SHARDS),
            [(dgt, (K_SHARDS, tm, Fs), lambda b, i, j, k: (k, i, 0)), (dup, (K_SHARDS, tm, Fs), lambda b, i, j, k: (k, i, 0)),
             (wg, (K_SHARDS, Fs, tn), lambda b, i, j, k: (k, 0, j)), (wu, (K_SHARDS, Fs, tn), lambda b, i, j, k: (k, 0, j))],
            [(0, 2, NN, 0), (1, 3, NN, 0)], [], [((S, D), F32, (tm, tn), lambda b, i, j, k: (i, j))],
            lambda accs, sides: accs, [(tm, tn)], deps=deps)[0]

    return grad_down, grad_gate_up, du


def _ssm_params(lam_re, lam_im, log_dt, b_re, b_im, c_re, c_im):
    G, N = lam_re.shape
    C = b_re.shape[2]
    lam_re = jnp.minimum(lam_re, -1e-4)
    dt = jnp.exp(log_dt)[:, None]
    mag = jnp.exp(lam_re * dt)
    a_re = mag * jnp.cos(lam_im * dt)
    a_im = mag * jnp.sin(lam_im * dt)
    den = lam_re * lam_re + lam_im * lam_im
    p = a_re - 1.0
    f_re = ((p * lam_re + a_im * lam_im) / den)[:, :, None]
    f_im = ((a_im * lam_re - p * lam_im) / den)[:, :, None]
    bb_re = f_re * b_re - f_im * b_im
    bb_im = f_re * b_im + f_im * b_re
    gpt = LANE // C
    tiles = G // gpt
    eye = jnp.eye(gpt, dtype=F32)

    def bd(bb):
        return jnp.einsum("bgnc,gh->bgchn", bb.reshape(tiles, gpt, N, C), eye).reshape(tiles, gpt * C, gpt * N)

    def cd(cc):
        return jnp.einsum("bgcn,gh->bgnhc", cc.reshape(tiles, gpt, C, N), eye).reshape(tiles, gpt * N, gpt * C)

    rows = G * N // LANE
    return (a_re.reshape(rows, LANE), a_im.reshape(rows, LANE), bd(bb_re), bd(bb_im), cd(c_re), cd(-c_im))


def _tile_states(ref3, b, per):
    return jnp.concatenate([ref3[:, per * b + r, :] for r in range(per)], axis=1).astype(BF)


def _ssm_spread(name, x, m_re, m_im, dims, rows):
    S, W = x.shape
    tiles = m_re.shape[0]
    tch, per = W // tiles, rows // tiles
    tq = _pick(S, 256, SUBLANE)

    def body(x_ref, mre_ref, mim_ref, ore_ref, oim_ref):
        for b in range(tiles):
            xb = x_ref[:, b * tch:(b + 1) * tch]
            for m_ref, o_ref in ((mre_ref, ore_ref), (mim_ref, oim_ref)):
                val = lax.dot_general(xb, m_ref[b], (dims, ((), ())), preferred_element_type=F32)
                for r in range(per):
                    o_ref[:, per * b + r, :] = val[:, r * LANE:(r + 1) * LANE]

    whole = lambda m: pl.BlockSpec(m.shape, lambda i: (0, 0, 0))
    st = pl.BlockSpec((tq, rows, LANE), lambda i: (i, 0, 0))
    return pl.pallas_call(
        body, name=name, grid=(S // tq,),
        in_specs=[pl.BlockSpec((tq, W), lambda i: (i, 0)), whole(m_re), whole(m_im)], out_specs=[st, st],
        out_shape=[jax.ShapeDtypeStruct((S, rows, LANE), F32)] * 2,
        compiler_params=pltpu.CompilerParams(dimension_semantics=("parallel",)),
    )(x, m_re, m_im)


def _ssm_collect(name, z_re3, z_im3, m_re, m_im, dims, side, gain, epilogue, out_dtypes):
    S, rows, _ = z_re3.shape
    tiles = m_re.shape[0]
    W = side.shape[1]
    tch, per = W // tiles, rows // tiles
    tq = _pick(S, 256, SUBLANE)
    n_out = len(out_dtypes)

    def body(zre_ref, zim_ref, mre_ref, mim_ref, side_ref, gain_ref, *out_refs):
        for b in range(tiles):
            cols = slice(b * tch, (b + 1) * tch)
            zre, zim = _tile_states(zre_ref, b, per), _tile_states(zim_ref, b, per)
            acc = lax.dot_general(zre, mre_ref[b], (dims, ((), ())), preferred_element_type=F32)
            acc = acc + lax.dot_general(zim, mim_ref[b], (dims, ((), ())), preferred_element_type=F32)
            for o, v in zip(out_refs[:n_out], epilogue(acc, side_ref[:, cols], gain_ref[:, cols])):
                o[:, cols] = v.astype(o.dtype)
            out_refs[n_out][:, b * per * LANE:(b + 1) * per * LANE] = zre
            out_refs[n_out + 1][:, b * per * LANE:(b + 1) * per * LANE] = zim

    whole = lambda m: pl.BlockSpec(m.shape, lambda i: (0, 0, 0))
    st = pl.BlockSpec((tq, rows, LANE), lambda i: (i, 0, 0))
    ch = pl.BlockSpec((tq, W), lambda i: (i, 0))
    flat = pl.BlockSpec((tq, rows * LANE), lambda i: (i, 0))
    return pl.pallas_call(
        body, name=name, grid=(S // tq,),
        in_specs=[st, st, whole(m_re), whole(m_im), ch, pl.BlockSpec((1, W), lambda i: (0, 0))],
        out_specs=[ch] * n_out + [flat, flat],
        out_shape=[jax.ShapeDtypeStruct((S, W), dt) for dt in out_dtypes]
        + [jax.ShapeDtypeStruct((S, rows * LANE), BF)] * 2,
        compiler_params=pltpu.CompilerParams(dimension_semantics=("parallel",)),
    )(z_re3, z_im3, m_re, m_im, side, gain)


def _scan_fwd(bu_re, bu_im, a_re, a_im):
    S, R, _ = bu_re.shape
    tc = _pick(S, 256, SUBLANE)

    def body(bre, bim, are, aim, sre, sim, carry):
        @pl.when(pl.program_id(0) == 0)
        def _():
            carry[...] = jnp.zeros_like(carry)

        ar, ai = are[...], aim[...]

        def step(t, c):
            pr, pi = c
            nr = ar * pr - ai * pi + bre[t]
            ni = ar * pi + ai * pr + bim[t]
            sre[t] = nr
            sim[t] = ni
            return nr, ni

        pr, pi = lax.fori_loop(0, tc, step, (carry[0], carry[1]), unroll=8)
        carry[0] = pr
        carry[1] = pi

    blk = pl.BlockSpec((tc, R, LANE), lambda i: (i, 0, 0))
    par = pl.BlockSpec((R, LANE), lambda i: (0, 0))
    return pl.pallas_call(
        body, name="ssm_scan_fwd", grid=(S // tc,),
        in_specs=[blk, blk, par, par], out_specs=[blk, blk],
        out_shape=[jax.ShapeDtypeStruct((S, R, LANE), F32)] * 2,
        scratch_shapes=[pltpu.VMEM((2, R, LANE), F32)],
        compiler_params=pltpu.CompilerParams(dimension_semantics=("arbitrary",)),
    )(bu_re, bu_im, a_re, a_im)


def _scan_bwd(ds_re, ds_im, s_re, s_im, a_re, a_im):
    S, R, _ = ds_re.shape
    tc = _pick(S, 256, SUBLANE)
    nc = S // tc

    def body(dre, dim_, sre, sim, are, aim, lre, lim, dar, dai, carry):
        @pl.when(pl.program_id(0) == 0)
        def _():
            carry[...] = jnp.zeros_like(carry)
            dar[...] = jnp.zeros_like(dar)
            dai[...] = jnp.zeros_like(dai)

        ar, ai = are[...], aim[...]

        def step(tt, c):
            t = tc - 1 - tt
            lr, li, gr, gi = c
            sr, si = sre[t], sim[t]
            gr = gr + lr * sr + li * si
            gi = gi + li * sr - lr * si
            nlr = dre[t] + ar * lr + ai * li
            nli = dim_[t] + ar * li - ai * lr
            lre[t] = nlr
            lim[t] = nli
            return nlr, nli, gr, gi

        lr, li, gr, gi = lax.fori_loop(0, tc, step, (carry[0], carry[1], dar[...], dai[...]), unroll=8)
        carry[0] = lr
        carry[1] = li
        dar[...] = gr
        dai[...] = gi

    blk = pl.BlockSpec((tc, R, LANE), lambda i: (nc - 1 - i, 0, 0))
    par = pl.BlockSpec((R, LANE), lambda i: (0, 0))
    return pl.pallas_call(
        body, name="ssm_scan_bwd", grid=(nc,),
        in_specs=[blk, blk, blk, blk, par, par], out_specs=[blk, blk, par, par],
        out_shape=[jax.ShapeDtypeStruct((S, R, LANE), F32)] * 2 + [jax.ShapeDtypeStruct((R, LANE), F32)] * 2,
        scratch_shapes=[pltpu.VMEM((2, R, LANE), F32)],
        compiler_params=pltpu.CompilerParams(dimension_semantics=("arbitrary",)),
    )(ds_re, ds_im, s_re, s_im, a_re, a_im)


def _shift_down(z, k):
    t = lax.broadcasted_iota(I32, z.shape, 0)
    return jnp.where(t >= k, pltpu.roll(z, k, 0), 0.0)


def _shift_up(z, k):
    n = z.shape[0]
    t = lax.broadcasted_iota(I32, z.shape, 0)
    return jnp.where(t < n - k, pltpu.roll(z, n - k, 0), 0.0)


def _conv_fwd(proj, cw, cb):
    _, S, W = proj.shape
    ct = _pick(W, 256, LANE)

    def fn(bg, cg, val, w, b):
        z = cg * val
        conv = b + w[0:1] * _shift_down(z, 2) + w[1:2] * _shift_down(z, 1) + w[2:3] * z
        return bg * conv, conv

    sl = lambda s: (proj, (None, S, ct), lambda j, s=s: (s, 0, j))
    col = ((S, ct), lambda j: (0, j))
    return _ew("conv_fwd", (W // ct,),
               [sl(1), sl(2), sl(3), (cw, (3, ct), lambda j: (0, j)), (cb, (1, ct), lambda j: (0, j))],
               [((S, W), BF) + col, ((S, W), F32) + col], fn)


def _conv_bwd(dyb, proj, conv, cw):
    _, S, W = proj.shape
    ct = _pick(W, 256, LANE)

    def fn(dy, bg, cg, val, cv, w):
        z = cg * val
        z1, z2 = _shift_down(z, 1), _shift_down(z, 2)
        dconv = dy * bg
        dz = w[2:3] * dconv + w[1:2] * _shift_up(dconv, 1) + w[0:1] * _shift_up(dconv, 2)
        dw = jnp.concatenate([jnp.sum(dconv * z2, axis=0, keepdims=True), jnp.sum(dconv * z1, axis=0, keepdims=True),
                              jnp.sum(dconv * z, axis=0, keepdims=True)], axis=0)
        return dy * cv, dz * val, dz * cg, dw, jnp.sum(dconv, axis=0, keepdims=True)

    sl = lambda s: (proj, (None, S, ct), lambda j, s=s: (s, 0, j))
    col = ((S, ct), lambda j: (0, j))
    return _ew("conv_bwd", (W // ct,),
               [(dyb,) + col, sl(1), sl(2), sl(3), (conv,) + col, (cw, (3, ct), lambda j: (0, j))],
               [((S, W), BF) + col, ((S, W), BF) + col, ((S, W), BF) + col,
                ((3, W), F32, (3, ct), lambda j: (0, j)), ((1, W), F32, (1, ct), lambda j: (0, j))], fn)


def _plain(accs, sides):
    return accs


def kernel(x, ffn1_norm, ffn1_w_gate, ffn1_w_up, ffn1_w_down, mix_norm, w_in, ssm_lambda_re, ssm_lambda_im, ssm_log_dt, ssm_b_re, ssm_b_im, ssm_c_re, ssm_c_im, ssm_d, ssm_w_glu, ssm_b_glu, ssm_w_out, conv_w, conv_b, conv_w_out, w_o, ffn2_norm, ffn2_w_gate, ffn2_w_up, ffn2_w_down, final_norm, loss_target, m_ffn1_norm, m_ffn1_w_gate, m_ffn1_w_up, m_ffn1_w_down, m_mix_norm, m_w_in, m_ssm_lambda_re, m_ssm_lambda_im, m_ssm_log_dt, m_ssm_b_re, m_ssm_b_im, m_ssm_c_re, m_ssm_c_im, m_ssm_d, m_ssm_w_glu, m_ssm_b_glu, m_ssm_w_out, m_conv_w, m_conv_b, m_conv_w_out, m_w_o, m_ffn2_norm, m_ffn2_w_gate, m_ffn2_w_up, m_ffn2_w_down, m_final_norm, v_ffn1_norm, v_ffn1_w_gate, v_ffn1_w_up, v_ffn1_w_down, v_mix_norm, v_w_in, v_ssm_lambda_re, v_ssm_lambda_im, v_ssm_log_dt, v_ssm_b_re, v_ssm_b_im, v_ssm_c_re, v_ssm_c_im, v_ssm_d, v_ssm_w_glu, v_ssm_b_glu, v_ssm_w_out, v_conv_w, v_conv_b, v_conv_w_out, v_w_o, v_ffn2_norm, v_ffn2_w_gate, v_ffn2_w_up, v_ffn2_w_down, v_final_norm):
    P = dict(ffn1_norm=ffn1_norm, ffn1_w_gate=ffn1_w_gate, ffn1_w_up=ffn1_w_up, ffn1_w_down=ffn1_w_down, mix_norm=mix_norm, w_in=w_in, ssm_lambda_re=ssm_lambda_re, ssm_lambda_im=ssm_lambda_im, ssm_log_dt=ssm_log_dt, ssm_b_re=ssm_b_re, ssm_b_im=ssm_b_im, ssm_c_re=ssm_c_re, ssm_c_im=ssm_c_im, ssm_d=ssm_d, ssm_w_glu=ssm_w_glu, ssm_b_glu=ssm_b_glu, ssm_w_out=ssm_w_out, conv_w=conv_w, conv_b=conv_b, conv_w_out=conv_w_out, w_o=w_o, ffn2_norm=ffn2_norm, ffn2_w_gate=ffn2_w_gate, ffn2_w_up=ffn2_w_up, ffn2_w_down=ffn2_w_down, final_norm=final_norm)
    M = dict(ffn1_norm=m_ffn1_norm, ffn1_w_gate=m_ffn1_w_gate, ffn1_w_up=m_ffn1_w_up, ffn1_w_down=m_ffn1_w_down, mix_norm=m_mix_norm, w_in=m_w_in, ssm_lambda_re=m_ssm_lambda_re, ssm_lambda_im=m_ssm_lambda_im, ssm_log_dt=m_ssm_log_dt, ssm_b_re=m_ssm_b_re, ssm_b_im=m_ssm_b_im, ssm_c_re=m_ssm_c_re, ssm_c_im=m_ssm_c_im, ssm_d=m_ssm_d, ssm_w_glu=m_ssm_w_glu, ssm_b_glu=m_ssm_b_glu, ssm_w_out=m_ssm_w_out, conv_w=m_conv_w, conv_b=m_conv_b, conv_w_out=m_conv_w_out, w_o=m_w_o, ffn2_norm=m_ffn2_norm, ffn2_w_gate=m_ffn2_w_gate, ffn2_w_up=m_ffn2_w_up, ffn2_w_down=m_ffn2_w_down, final_norm=m_final_norm)
    V = dict(ffn1_norm=v_ffn1_norm, ffn1_w_gate=v_ffn1_w_gate, ffn1_w_up=v_ffn1_w_up, ffn1_w_down=v_ffn1_w_down, mix_norm=v_mix_norm, w_in=v_w_in, ssm_lambda_re=v_ssm_lambda_re, ssm_lambda_im=v_ssm_lambda_im, ssm_log_dt=v_ssm_log_dt, ssm_b_re=v_ssm_b_re, ssm_b_im=v_ssm_b_im, ssm_c_re=v_ssm_c_re, ssm_c_im=v_ssm_c_im, ssm_d=v_ssm_d, ssm_w_glu=v_ssm_w_glu, ssm_b_glu=v_ssm_b_glu, ssm_w_out=v_ssm_w_out, conv_w=v_conv_w, conv_b=v_conv_b, conv_w_out=v_conv_w_out, w_o=v_w_o, ffn2_norm=v_ffn2_norm, ffn2_w_gate=v_ffn2_w_gate, ffn2_w_up=v_ffn2_w_up, ffn2_w_down=v_ffn2_w_down, final_norm=v_final_norm)
    names = list(P)
    sharded = ["ffn1_w_gate", "ffn1_w_up", "ffn1_w_down", "w_in", "ssm_w_glu", "ssm_w_out", "conv_w_out", "w_o",
               "ffn2_w_gate", "ffn2_w_up", "ffn2_w_down"]
    replicated = [n for n in names if n not in sharded and n != "conv_w"]

    S, D = x.shape[1], x.shape[2]
    W = ssm_d.shape[0]
    Dc = D // NDEV
    G, N = ssm_lambda_re.shape
    rows = G * N // LANE
    xh = x.reshape(S, D)
    target = loss_target.reshape(S, D)
    xi, yi, ci = lax.axis_index("x"), lax.axis_index("y"), lax.axis_index("c")
    c_arr = jnp.reshape(ci, (1,)).astype(I32)
    q_arr = jnp.reshape(2 * xi + yi, (1,)).astype(I32)
    row = lambda v: v.reshape(1, -1)

    transposed = ("ffn1_w_gate", "ffn1_w_up", "ffn2_w_gate", "ffn2_w_up")
    local = lambda table, n: table[n].T if n in transposed else table[n]

    groups = [sharded[0:1], sharded[1:2], sharded[2:3], ["w_in", "conv_w"], sharded[4:8],
              sharded[8:9], sharded[9:10], sharded[10:11]]
    first_begun, second_begun, chain = {}, {}, []
    me = 4 * xi + 2 * yi + ci

    def gather_first(gi):
        hold = 0.0 * chain[0][0, 0] if chain else 0.0
        srcs = [conv_w + hold if n == "conv_w" else (local(P, n) + hold).astype(BF) for n in groups[gi]]
        lands = [lax.dynamic_update_slice(lax.empty((NDEV,) + s.shape, s.dtype), s[None], (me,) + (0,) * s.ndim)
                 for s in srcs]
        first_begun[gi] = _split_start("gather_first_start_%d" % gi, srcs, lands, _gather_first_copies, 3, chain[-1:])
        chain.append(first_begun[gi][4])

    def gather_second(gi, after):
        _, lands = _split_wait("gather_first_wait_%d" % gi, first_begun[gi], _gather_first_copies, after)
        second_begun[gi] = _split_start("gather_second_start_%d" % gi, [], lands, _gather_second_copies, 2, chain[-1:])
        chain.append(second_begun[gi][4])

    def gathered(gi, after):
        _, lands = _split_wait("gather_second_wait_%d" % gi, second_begun[gi], _gather_second_copies, after)
        return _gather_forward("gather_forward_%d" % gi, lands, chain[-1:])

    def fetch(gi, seconds, firsts):
        def get(after):
            for g in seconds:
                gather_second(g, after)
            for g in firsts:
                gather_first(g)
            return gathered(gi, after)
        return get

    tm = _pick(S, M_TILE, SUBLANE)
    th = _pick(S, M_TILE // 2, SUBLANE)
    tk = _pick(D, K_TILE, LANE)
    ts = _pick(S, K_TILE, SUBLANE)
    tn = _pick(D, 1024, LANE)

    gather_first(0)
    gather_first(1)
    u1, r1 = _rms_fwd("rms1", xh, row(ffn1_norm), deps=list(chain))

    hold = 0.0 * chain[0][0, 0]
    replicated = [n for n in replicated if n != "ffn1_norm"] + ["ffn1_norm"]
    tile = SUBLANE * LANE

    def as_rows(p):
        flat = p.reshape(-1).astype(F32)
        return jnp.pad(flat, (0, -flat.shape[0] % tile)).reshape(-1, LANE)

    def pack(parts):
        return jnp.concatenate([as_rows(p) for p in parts], axis=0)

    cw_zero = jnp.zeros((conv_w.shape[0], W), F32)
    packed_state = [pack([t[n] + hold for n in replicated[:-1]] + [cw_zero + fill, t["ffn1_norm"] + hold])
                    for t, fill in ((P, 0.0), (M, 0.0), (V, 1.0))]
    ssm_in = tuple(p + hold for p in (ssm_lambda_re, ssm_lambda_im, ssm_log_dt, ssm_b_re, ssm_b_im, ssm_c_re, ssm_c_im))
    (a_re, a_im, bd_re, bd_im, cd_re, cd_imn), ssm_vjp = jax.vjp(_ssm_params, *ssm_in)
    bd_re_b, bd_im_b, cd_re_b, cd_imn_b = (t.astype(BF) for t in (bd_re, bd_im, cd_re, cd_imn))
    gather_second(0, [u1, a_re, a_im, bd_re_b, bd_im_b, cd_re_b, cd_imn_b] + packed_state)
    gather_first(2)
    gather_second(1, u1)
    gather_first(3)
    (wg1,) = gathered(0, u1)
    h1, ffn1_saved, wu1, wd1 = _ffn_fwd("ffn1", u1, xh, wg1, lambda after: fetch(1, [2], [4])(after)[0],
                                        lambda after: fetch(2, [3], [5])(after)[0])
    u2, r2 = _rms_fwd("rms2", h1, row(mix_norm))
    w_in_f, cw_f = fetch(3, [4], [6])(u2)
    cw = jnp.transpose(cw_f, (1, 0, 2)).reshape(3, W)
    (proj,) = _mm(
        "in_proj", (NDEV, S // tm, 1, D // tk),
        [(u2, (tm, tk), lambda b, i, j, k: (i, k)), (w_in_f, (None, tk, W), lambda b, i, j, k: (b, k, 0))],
        [(0, 1, NN, 0)], [], [((NDEV, S, W), F32, (None, tm, W), lambda b, i, j, k: (b, i, 0))], _plain, [(tm, W)])

    v_f = proj[0]
    v_bf = v_f.astype(BF)
    bu_re3, bu_im3 = _ssm_spread("ssm_bu", v_bf, bd_re_b, bd_im_b, NN, rows)
    s_re3, s_im3 = _scan_fwd(bu_re3, bu_im3, a_re, a_im)

    def y0_epilogue(acc, v_tile, d_tile):
        y0 = acc + d_tile * v_tile
        return y0, _gelu(y0)

    y0, y1, s_re_b, s_im_b = _ssm_collect("ssm_y0", s_re3, s_im3, cd_re_b, cd_imn_b, NN, v_f, row(ssm_d),
                                          y0_epilogue, [F32, BF])

    tw = _pick(W, 512, LANE)
    w_glu_f, w_so, w_co, w_o_f = fetch(4, [5], [7])(y1)
    w_glu_f = w_glu_f.reshape(W, W)
    w_o_f = w_o_f.reshape(D, D)

    def glu_epilogue(accs, sides):
        q = accs[0] + sides[1]
        return q, _gelu(sides[0]) * _sigmoid(q)

    q_pre, y2 = _mm("ssm_glu", (1, S // tm, W // tw, 1),
                    [(y1, (tm, W), lambda b, i, j, k: (i, 0)), (w_glu_f, (W, tw), lambda b, i, j, k: (0, j))],
                    [(0, 1, NN, 0)],
                    [(y0, (tm, tw), lambda b, i, j, k: (i, j)), (row(ssm_b_glu), (1, tw), lambda b, i, j, k: (0, j))],
                    [((S, W), F32, (tm, tw), lambda b, i, j, k: (i, j)), ((S, W), BF, (tm, tw), lambda b, i, j, k: (i, j))],
                    glu_epilogue, [(tm, tw)])

    yb, conv = _conv_fwd(proj, cw, row(conv_b))

    per = W // Dc
    ga_blk = (proj, (None, tm, Dc), lambda b, i, j, k: (4 + b // per, i, b % per))
    gb_blk = (proj, (None, tm, Dc), lambda b, i, j, k: (6 + b // per, i, b % per))
    dc_out = ((S, D), BF, (tm, Dc), lambda b, i, j, k: (i, b))

    def merge_epilogue(accs, sides):
        za, zb = accs
        return _sigmoid(sides[0]) * za + _sigmoid(sides[1]) * zb, za, zb

    merged, z_a, z_b = _mm(
        "mix_merge", (NDEV, S // tm, 1, 1),
        [(y2, (tm, W), lambda b, i, j, k: (i, 0)), (yb, (tm, W), lambda b, i, j, k: (i, 0)),
         (w_so, (None, W, Dc), lambda b, i, j, k: (b, 0, 0)), (w_co, (None, W, Dc), lambda b, i, j, k: (b, 0, 0))],
        [(0, 2, NN, 0), (1, 3, NN, 1)], [ga_blk, gb_blk], [dc_out, dc_out, dc_out], merge_epilogue, [(tm, Dc)] * 2)

    (h2,) = _mm("mix_out", (1, S // tm, D // tn, D // tk),
                [(merged, (tm, tk), lambda b, i, j, k: (i, k)), (w_o_f, (tk, tn), lambda b, i, j, k: (k, j))],
                [(0, 1, NN, 0)], [(h1, (tm, tn), lambda b, i, j, k: (i, j))],
                [((S, D), F32, (tm, tn), lambda b, i, j, k: (i, j))],
                lambda accs, sides: [sides[0] + accs[0]], [(tm, tn)])

    u3, r3 = _rms_fwd("rms3", h2, row(ffn2_norm))
    (wg2,) = fetch(5, [6, 7], [])(u3)
    h3, ffn2_saved, wu2, wd2 = _ffn_fwd("ffn2", u3, h2, wg2, lambda after: gathered(6, after)[0],
                                        lambda after: gathered(7, after)[0])
    loss_vec, dh3, dh3_half, d_final_norm = _loss_head("loss_head", h3, row(final_norm), target)
    loss = lax.psum(loss_vec[0, 0], ("x", "y", "c"))
    loss_done = jnp.zeros((SUBLANE, LANE), F32) + loss

    grads, deltas, new_m, new_v = {}, {}, {}, {}

    def rs_sibling_start(tag, parts):
        lands = [lax.empty((4,) + p.shape[1:], p.dtype) for p in parts]
        return _split_start("rs_sibling_start_" + tag, parts, lands, _sibling_copies, 4)

    def rs_chips_start(tag, sibling_begun, after):
        parts, lands = _split_wait("rs_sibling_wait_" + tag, sibling_begun, _sibling_copies, after)
        sums = [_sum_sibling("rs_sum_%s_%d" % (tag, a), p, land, c_arr) for a, (p, land) in enumerate(zip(parts, lands))]
        lands2 = [lax.empty((3,) + sm.shape[1:], sm.dtype) for sm in sums]
        return _split_start("rs_chips_start_" + tag, sums, lands2, _chips_copies, 3)

    def rs_end(tag, group, begun, after):
        sums, lands2 = _split_wait("rs_chips_wait_" + tag, begun, _chips_copies, after)
        for n, sm, land2 in zip(group, sums, lands2):
            res = _finish_sharded("adamw_" + n, sm, land2, q_arr, local(P, n), local(M, n), local(V, n))
            grads[n], deltas[n], new_m[n], new_v[n] = [t.T if n in transposed else t for t in res]

    f2_dwd, f2_dwgu, f2_du = _ffn_bwd("ffn2", dh3_half, u3, ffn2_saved, wg2, wu2, wd2, deps=[loss_done])
    dwd2 = f2_dwd()
    dwg2, dwu2 = f2_dwgu()
    sib_ffn2 = rs_sibling_start("ffn2", [dwg2, dwu2, dwd2])
    du3 = f2_du(deps=[sib_ffn2[4]])
    dh2, dh2_b, d_ffn2_norm = _rms_bwd("rms3_bwd", du3, h2, r3, row(ffn2_norm), dh3, 1.0)
    rs_ffn2 = rs_chips_start("ffn2", sib_ffn2, dh2)

    tg = _pick(W, 512, Dc)
    per2 = W // tg
    dg_out = ((2, S, W), BF, (None, tm, tg), lambda b, i, j, k: (j // per2, i, j % per2))
    ga_blk2 = (proj, (None, tm, tg), lambda b, i, j, k: (4 + j // per2, i, j % per2))
    gb_blk2 = (proj, (None, tm, tg), lambda b, i, j, k: (6 + j // per2, i, j % per2))
    dcj = lambda arr: (arr, (tm, tg), lambda b, i, j, k: (i, j))
    dcj_out = ((S, D), BF, (tm, tg), lambda b, i, j, k: (i, j))

    def dmerge_epilogue(accs, sides):
        dm = accs[0]
        sa, sb = _sigmoid(sides[0]), _sigmoid(sides[1])
        za, zb = sides[2].astype(F32), sides[3].astype(F32)
        return dm * sa, dm * sb, dm * za * sa * (1.0 - sa), dm * zb * sb * (1.0 - sb)

    dz_a, dz_b, dga, dgb = _mm(
        "mix_out_dx", (1, S // tm, D // tg, D // tk),
        [(dh2_b, (tm, tk), lambda b, i, j, k: (i, k)), (w_o_f, (tg, tk), lambda b, i, j, k: (j, k))],
        [(0, 1, NT, 0)], [ga_blk2, gb_blk2, dcj(z_a), dcj(z_b)], [dcj_out, dcj_out, dg_out, dg_out],
        dmerge_epilogue, [(tm, tg)], deps=[rs_ffn2[4]])

    td = _pick(D, M_TILE, LANE)
    (dw_o,) = _mm("mix_out_dw", (1, D // td, D // tn, S // ts),
                  [(merged, (ts, td), lambda b, i, j, k: (k, i)), (dh2_b, (ts, tn), lambda b, i, j, k: (k, j))],
                  [(0, 1, TN, 0)], [], [((D, D), BF, (td, tn), lambda b, i, j, k: (i, j))], _plain, [(td, tn)])

    wout = ((NDEV, W, Dc), BF, (None, W, Dc), lambda b, i, j, k: (b, 0, 0))
    dw_so, dw_co = _mm(
        "mix_merge_dw", (NDEV, 1, 1, S // ts),
        [(y2, (ts, W), lambda b, i, j, k: (k, 0)), (yb, (ts, W), lambda b, i, j, k: (k, 0)),
         (dz_a, (ts, Dc), lambda b, i, j, k: (k, b)), (dz_b, (ts, Dc), lambda b, i, j, k: (k, b))],
        [(0, 2, TN, 0), (1, 3, TN, 1)], [], [wout, wout], _plain, [(W, Dc)] * 2)

    def dglu_epilogue(accs, sides):
        dy2, dyb = accs
        sq = _sigmoid(sides[1])
        return dy2 * _gelu(sides[0]) * sq * (1.0 - sq), dy2 * sq, dyb

    full_w = lambda arr: (arr, (th, W), lambda b, i, j, k: (i, 0))
    full_w_out = lambda dt: ((S, W), dt, (th, W), lambda b, i, j, k: (i, 0))
    dq, dy1p, dyb = _mm(
        "mix_merge_dx", (1, S // th, 1, NDEV),
        [(dz_a, (th, Dc), lambda b, i, j, k: (i, k)), (dz_b, (th, Dc), lambda b, i, j, k: (i, k)),
         (w_so, (None, W, Dc), lambda b, i, j, k: (k, 0, 0)), (w_co, (None, W, Dc), lambda b, i, j, k: (k, 0, 0))],
        [(0, 2, NT, 0), (1, 3, NT, 1)], [full_w(y0), full_w(q_pre)], [full_w_out(BF), full_w_out(F32), full_w_out(F32)],
        dglu_epilogue, [(th, W)] * 2)

    def dy0_epilogue(accs, sides):
        dy0 = (sides[0] + accs[0]) * _gelu_grad(sides[1])
        return dy0, dy0

    wj = lambda arr: (arr, (tm, tw), lambda b, i, j, k: (i, j))
    dy0, dy0_b = _mm("ssm_glu_dx", (1, S // tm, W // tw, 1),
                     [(dq, (tm, W), lambda b, i, j, k: (i, 0)), (w_glu_f, (tw, W), lambda b, i, j, k: (j, 0))],
                     [(0, 1, NT, 0)], [wj(dy1p), wj(y0)],
                     [((S, W), F32, (tm, tw), lambda b, i, j, k: (i, j)), ((S, W), BF, (tm, tw), lambda b, i, j, k: (i, j))],
                     dy0_epilogue, [(tm, tw)])

    (dw_glu,) = _mm("ssm_glu_dw", (1, W // tw, 1, S // ts),
                    [(y1, (ts, tw), lambda b, i, j, k: (k, i)), (dq, (ts, W), lambda b, i, j, k: (k, 0))],
                    [(0, 1, TN, 0)], [], [((W, W), BF, (tw, W), lambda b, i, j, k: (i, 0))], _plain, [(tw, W)])

    tr = _pick(S, 256, SUBLANE)
    rw = ((tr, W), lambda i: (i, 0))
    vec_w = ((1, W), F32, (1, W), lambda i: (0, 0))
    d_b_glu, d_ssm_d = _ew(
        "ssm_colsums", (S // tr,), [(dq,) + rw, (dy0,) + rw, (proj, (None, tr, W), lambda i: (0, i, 0))],
        [vec_w, vec_w],
        lambda dqv, dyv, vv: (jnp.sum(dqv.astype(F32), axis=0, keepdims=True), jnp.sum(dyv * vv, axis=0, keepdims=True)),
        acc=(0, 1))

    ds_re3, ds_im3 = _ssm_spread("ssm_ds", dy0_b, cd_re_b, cd_imn_b, NT, rows)
    lam_re3, lam_im3, da_re, da_im = _scan_bwd(ds_re3, ds_im3, s_re3, s_im3, a_re, a_im)
    dv, lam_re_b, lam_im_b = _ssm_collect("ssm_dv", lam_re3, lam_im3, bd_re_b, bd_im_b, NT, dy0, row(ssm_d),
                                          lambda acc, dy_tile, d_tile: [acc + dy_tile * d_tile], [BF])
    tiles, tch, tst = bd_re.shape
    tok_ch = lambda arr: (arr, (ts, tch), lambda b, i, j, k: (k, b))
    tok_st = lambda arr: (arr, (ts, tst), lambda b, i, j, k: (k, b))
    bd_out = ((tiles, tch, tst), F32, (None, tch, tst), lambda b, i, j, k: (b, 0, 0))
    cd_out = ((tiles, tst, tch), F32, (None, tst, tch), lambda b, i, j, k: (b, 0, 0))
    dbd_re, dbd_im = _mm("ssm_dbd", (tiles, 1, 1, S // ts), [tok_ch(v_bf), tok_st(lam_re_b), tok_st(lam_im_b)],
                         [(0, 1, TN, 0), (0, 2, TN, 1)], [], [bd_out, bd_out], _plain, [(tch, tst)] * 2)
    dcd_re, dcd_imn = _mm("ssm_dcd", (tiles, 1, 1, S // ts), [tok_st(s_re_b), tok_st(s_im_b), tok_ch(dy0_b)],
                          [(0, 2, TN, 0), (1, 2, TN, 1)], [], [cd_out, cd_out], _plain, [(tst, tch)] * 2)
    d_ssm = ssm_vjp((da_re, da_im, dbd_re, dbd_im, dcd_re, dcd_imn))

    dbg, dcg, dval, d_conv_w_full, d_conv_b = _conv_bwd(dyb, proj, conv, cw)
    dproj = jnp.concatenate([dv[None], dbg[None], dcg[None], dval[None], dga, dgb], axis=0)

    (dw_in,) = _mm("in_proj_dw", (NDEV, D // td, 1, S // ts),
                   [(u2, (ts, td), lambda b, i, j, k: (k, i)), (dproj, (None, ts, W), lambda b, i, j, k: (b, k, 0))],
                   [(0, 1, TN, 0)], [], [((NDEV, D, W), BF, (None, td, W), lambda b, i, j, k: (b, i, 0))],
                   _plain, [(td, W)])
    sib_mixer = rs_sibling_start(
        "mixer", [dw_in, dw_glu.reshape(NDEV, W // NDEV, W), dw_so, dw_co, dw_o.reshape(NDEV, Dc, D)])
    (du2,) = _mm("in_proj_dx", (1, S // tm, D // tn, NDEV // K_SHARDS),
                 [(dproj, (K_SHARDS, tm, W), lambda b, i, j, k: (k, i, 0)),
                  (w_in_f, (K_SHARDS, tn, W), lambda b, i, j, k: (k, j, 0))],
                 [(0, 1, NT, 0)], [], [((S, D), F32, (tm, tn), lambda b, i, j, k: (i, j))], _plain, [(tm, tn)],
                 deps=[sib_mixer[4]])
    dh1, dh1_half, d_mix_norm = _rms_bwd("rms2_bwd", du2, h1, r2, row(mix_norm), dh2, 0.5)
    rs_mixer = rs_chips_start("mixer", sib_mixer, dh1)

    small = dict(mix_norm=d_mix_norm, ffn2_norm=d_ffn2_norm, final_norm=d_final_norm,
                 ssm_lambda_re=d_ssm[0], ssm_lambda_im=d_ssm[1], ssm_log_dt=d_ssm[2], ssm_b_re=d_ssm[3],
                 ssm_b_im=d_ssm[4], ssm_c_re=d_ssm[5], ssm_c_im=d_ssm[6], ssm_d=d_ssm_d, ssm_b_glu=d_b_glu,
                 conv_b=d_conv_b)
    early_pk = pack([small[n] for n in replicated[:-1]] + [d_conv_w_full])
    early_land = lax.dynamic_update_slice(jnp.zeros((NDEV,) + early_pk.shape, F32), early_pk[None], (me, 0, 0))
    small_begun = _split_start("gather_small_start", [early_pk], [early_land], _everyone_copies, NDEV - 1)

    f1_dwd, f1_dwgu, f1_du = _ffn_bwd("ffn1", dh1_half, u1, ffn1_saved, wg1, wu1, wd1,
                                      deps=[rs_mixer[4], small_begun[4]])
    du1 = f1_du()
    dx, _, d_ffn1_norm = _rms_bwd("rms1_bwd", du1, xh, r1, row(ffn1_norm), dh1, 1.0)
    dwd1 = f1_dwd(deps=[d_ffn1_norm])
    late = d_ffn1_norm + 0.0 * dwd1[0, :1, :1].astype(F32)
    late_pk = late.reshape(-1, LANE)
    late_land = lax.dynamic_update_slice(jnp.zeros((NDEV,) + late_pk.shape, F32), late_pk[None], (me, 0, 0))
    late_begun = _split_start("gather_late_start", [late_pk], [late_land], _everyone_copies, NDEV - 1)
    rs_ffn1_down = rs_chips_start("ffn1_down", rs_sibling_start("ffn1_down", [dwd1]), late_begun[4])
    dwg1, dwu1 = f1_dwgu(deps=[rs_ffn1_down[4]])
    rs_ffn1_gate = rs_chips_start("ffn1_gate", rs_sibling_start("ffn1_gate", [dwg1]), None)
    rs_ffn1_up = rs_chips_start("ffn1_up", rs_sibling_start("ffn1_up", [dwu1]), rs_ffn1_gate[4])
    rs_end("ffn2", sharded[8:11], rs_ffn2, rs_ffn1_up[4])
    rs_end("mixer", sharded[3:8], rs_mixer, [grads[n] for n in sharded[8:11]])
    _, (early_all,) = _split_wait("gather_small_wait", small_begun, _everyone_copies, [grads[n] for n in sharded[3:8]])
    _, (late_all,) = _split_wait("gather_late_wait", late_begun, _everyone_copies, early_all)
    small_all = jnp.concatenate([early_all, late_all], axis=1)
    g_pk, d_pk, m_pk, v_pk = _finish_replicated("adamw_replicated", small_all, *packed_state)

    def unpack(pk, r0, like):
        nr = -(-like.size // tile) * SUBLANE
        return pk[r0:r0 + nr].reshape(-1)[:like.size].reshape(like.shape), r0 + nr

    r0 = 0
    for n in replicated[:-1] + ["conv_w", "ffn1_norm"]:
        like = d_conv_w_full if n == "conv_w" else P[n]
        for store, pk in ((grads, g_pk), (deltas, d_pk), (new_m, m_pk), (new_v, v_pk)):
            store[n], r1 = unpack(pk, r0, like)
        r0 = r1
    g_cw_full = grads["conv_w"]
    cwl = conv_w.shape[1]
    g_cw = lax.dynamic_slice_in_dim(g_cw_full, me * cwl, cwl, axis=1)
    full3 = ((3, cwl), lambda i: (0, 0))
    grads["conv_w"], deltas["conv_w"], new_m["conv_w"], new_v["conv_w"] = _ew(
        "adamw_conv_w", (1,), [(g_cw,) + full3, (conv_w,) + full3, (m_conv_w,) + full3, (v_conv_w,) + full3],
        [((3, cwl), F32) + full3] * 4, lambda g, w, m, v: (g,) + _adamw(w, g, m, v))
    rs_end("ffn1_down", sharded[2:3], rs_ffn1_down, [g_pk, grads["conv_w"]])
    rs_end("ffn1_gate", sharded[0:1], rs_ffn1_gate, grads["ffn1_w_down"])
    rs_end("ffn1_up", sharded[1:2], rs_ffn1_up, grads["ffn1_w_gate"])

    return (loss, dx.reshape(x.shape), *[grads[n] for n in names], *[deltas[n] for n in names],
            *[new_m[n] for n in names], *[new_v[n] for n in names])
```

```python
import math

import jax
import jax.numpy as jnp
from jax import lax
from jax.experimental import pallas as pl
from jax.experimental.pallas import tpu as pltpu

F32 = jnp.float32
BF = jnp.bfloat16
I32 = jnp.int32
MESH = pl.DeviceIdType.MESH
LANE = 128
SUBLANE = 8
NDEV = 8
EW_BLOCK = 256 * 1024
M_TILE = 1024
K_TILE = 2048
K_SHARDS = 2
DMA_CHUNKS = 4
EPS = 1e-6
ADAM_LR, ADAM_B1, ADAM_B2, ADAM_EPS, ADAM_WD, ADAM_STEP = 0.001, 0.9, 0.999, 1e-08, 0.01, 10
NN = ((1,), (0,))
NT = ((1,), (1,))
TN = ((0,), (0,))
HBM = pl.BlockSpec(memory_space=pltpu.HBM)


def _pick(n, pref, mult):
    t = min(pref, n)
    t -= t % mult
    while t >= mult:
        if n % t == 0:
            return t
        t -= mult
    return n


def _sigmoid(x):
    return 1.0 / (1.0 + jnp.exp(-x))


_GELU_C = math.sqrt(2.0 / math.pi)


def _gelu(x):
    return 0.5 * x * (1.0 + jnp.tanh(_GELU_C * (x + 0.044715 * x * x * x)))


def _gelu_grad(x):
    t = jnp.tanh(_GELU_C * (x + 0.044715 * x * x * x))
    return 0.5 * (1.0 + t) + 0.5 * x * (1.0 - t * t) * _GELU_C * (1.0 + 3.0 * 0.044715 * x * x)


def _dep_specs(deps, rank):
    return [(d, d.shape, lambda *_, nd=d.ndim: (0,) * nd) for d in deps]


def _mm(name, grid, ops, pairs, sides, outs, epilogue, acc_shapes, deps=()):
    nk = grid[-1]
    n_ops, n_sides, n_outs = len(ops), len(sides), len(outs)
    dep_specs = _dep_specs(deps, len(grid))
    n_deps = len(dep_specs)

    def body(*refs):
        op_refs = refs[:n_ops]
        side_refs = refs[n_ops:n_ops + n_sides]
        out_refs = refs[n_ops + n_sides + n_deps:n_ops + n_sides + n_deps + n_outs]
        acc_refs = refs[n_ops + n_sides + n_deps + n_outs:]

        def partials():
            res = [None] * len(acc_shapes)
            for ia, ib, dims, ai in pairs:
                a_ref, b_ref = op_refs[ia], op_refs[ib]
                for s in range(a_ref.shape[0] if len(a_ref.shape) == 3 else 1):
                    a, b = (a_ref[s], b_ref[s]) if len(a_ref.shape) == 3 else (a_ref[...], b_ref[...])
                    p = lax.dot_general(a, b, (dims, ((), ())), preferred_element_type=F32)
                    res[ai] = p if res[ai] is None else res[ai] + p
            return res

        def finish(accs):
            vals = epilogue(accs, [s[...] for s in side_refs])
            for o, v in zip(out_refs, vals):
                o[...] = v.astype(o.dtype)

        if nk == 1:
            finish(partials())
        else:
            k = pl.program_id(len(grid) - 1)

            @pl.when(k == 0)
            def _():
                for a, p in zip(acc_refs, partials()):
                    a[...] = p

            @pl.when(k > 0)
            def _():
                for a, p in zip(acc_refs, partials()):
                    a[...] += p

            @pl.when(k == nk - 1)
            def _():
                finish([a[...] for a in acc_refs])

    return pl.pallas_call(
        body, name=name, grid=grid,
        in_specs=[pl.BlockSpec(b, m) for (_, b, m) in list(ops) + list(sides) + dep_specs],
        out_specs=[pl.BlockSpec(b, m) for (_, _, b, m) in outs],
        out_shape=[jax.ShapeDtypeStruct(s, d) for (s, d, _, _) in outs],
        scratch_shapes=[pltpu.VMEM(s, F32) for s in acc_shapes] if nk > 1 else [],
        compiler_params=pltpu.CompilerParams(
            dimension_semantics=("parallel",) * (len(grid) - 1) + ("arbitrary",)),
    )(*[a for (a, _, _) in list(ops) + list(sides) + dep_specs])


def _ew(name, grid, ins, outs, fn, acc=(), deps=()):
    n_in = len(ins)
    dep_specs = _dep_specs(deps, len(grid))

    def body(*refs):
        vals = fn(*[r[...] for r in refs[:n_in]])
        first = pl.program_id(0) == 0
        for idx, (o, v) in enumerate(zip(refs[n_in + len(dep_specs):], vals)):
            if idx in acc:
                @pl.when(first)
                def _(o=o, v=v):
                    o[...] = v.astype(o.dtype)

                @pl.when(jnp.logical_not(first))
                def _(o=o, v=v):
                    o[...] += v.astype(o.dtype)
            else:
                o[...] = v.astype(o.dtype)

    return pl.pallas_call(
        body, name=name, grid=grid,
        in_specs=[pl.BlockSpec(b, m) for (_, b, m) in list(ins) + dep_specs],
        out_specs=[pl.BlockSpec(b, m) for (_, _, b, m) in outs],
        out_shape=[jax.ShapeDtypeStruct(s, d) for (s, d, _, _) in outs],
        compiler_params=pltpu.CompilerParams(
            dimension_semantics=(("arbitrary",) if acc else ("parallel",)) * len(grid)),
    )(*[a for (a, _, _) in list(ins) + dep_specs])


def _position():
    x, y, c = lax.axis_index("x"), lax.axis_index("y"), lax.axis_index("c")
    chips = [(1 - x, y), (x, 1 - y), (1 - x, 1 - y)]
    return x, y, c, chips


SEM = pl.BlockSpec(memory_space=pltpu.SEMAPHORE)
EFFECT = pltpu.SideEffectType.DATAFLOW_SIDE_EFFECTING


def _in_hbm(v):
    return pltpu.with_memory_space_constraint(v, pltpu.HBM)


def _split_start(name, srcs, lands, make_copies, n_per, after=()):
    n, nb = len(srcs), len(srcs) + len(lands)
    n_sems = len(lands) * n_per
    after = list(after)

    def body(*refs):
        send_sems, recv_sems = refs[nb + len(after)], refs[nb + len(after) + 1]
        for cp in make_copies(refs[:n], refs[n:nb], send_sems, recv_sems):
            cp.start()
        refs[-1][...] = jnp.zeros_like(refs[-1])

    outs = pl.pallas_call(
        body, name=name,
        out_shape=(pltpu.SemaphoreType.DMA((n_sems,)), pltpu.SemaphoreType.DMA((n_sems,)),
                   *[pltpu.HBM(v.shape, v.dtype) for v in list(srcs) + list(lands)],
                   jax.ShapeDtypeStruct((SUBLANE, LANE), F32)),
        in_specs=[HBM] * nb + [pl.BlockSpec(memory_space=pl.ANY)] * len(after),
        out_specs=(SEM, SEM, *[HBM] * nb, pl.BlockSpec(memory_space=pltpu.VMEM)),
        input_output_aliases={i: 2 + i for i in range(nb)},
        compiler_params=pltpu.CompilerParams(has_side_effects=EFFECT),
    )(*[_in_hbm(v) for v in list(srcs) + list(lands)], *after)
    return outs[0], outs[1], list(outs[2:2 + n]), list(outs[2 + n:2 + nb]), outs[-1]


def _split_wait(name, started, make_copies, after):
    send_sems, recv_sems, srcs, lands, _ = started
    n, nb = len(srcs), len(srcs) + len(lands)

    def body(*refs):
        for cp in make_copies(refs[:n], refs[n:nb], refs[nb], refs[nb + 1]):
            cp.wait_send()
            cp.wait_recv()

    order = [] if after is None else list(after) if isinstance(after, (list, tuple)) else [after]
    outs = pl.pallas_call(
        body, name=name,
        out_shape=tuple(pltpu.HBM(v.shape, v.dtype) for v in srcs + lands),
        in_specs=[HBM] * nb + [SEM, SEM] + [pl.BlockSpec(memory_space=pl.ANY)] * len(order),
        out_specs=tuple([HBM] * nb),
        input_output_aliases={i: i for i in range(nb)},
        compiler_params=pltpu.CompilerParams(has_side_effects=EFFECT),
    )(*srcs, *lands, send_sems, recv_sems, *order)
    return list(outs[:n]), list(outs[n:])


def _gather_first_copies(xs, lands, send_sems, recv_sems):
    x, y, c, _ = _position()
    copies = []
    for a in range(len(xs)):
        for k, peer in enumerate([(x, y, 1 - c), (1 - x, y, c), (x, 1 - y, c)]):
            copies.append(pltpu.make_async_remote_copy(
                src_ref=xs[a], dst_ref=lands[a].at[4 * x + 2 * y + c],
                send_sem=send_sems.at[3 * a + k], recv_sem=recv_sems.at[3 * a + k], device_id=peer, device_id_type=MESH))
    return copies


def _gather_second_copies(xs, lands, send_sems, recv_sems):
    x, y, c, _ = _position()
    copies = []
    for a in range(len(lands)):
        rows = lands[a].shape[1]
        unit = SUBLANE * (4 // jnp.dtype(lands[a].dtype).itemsize)
        half = rows // 2 // unit * unit or rows
        parts = [((1 - x, y), (x, 1 - y), pl.ds(0, half))]
        if half < rows:
            parts.append(((x, 1 - y), (1 - x, y), pl.ds(half, rows - half)))
        for k, (block, to, rs) in enumerate(parts):
            ref = lands[a].at[4 * block[0] + 2 * block[1] + c, rs]
            copies.append(pltpu.make_async_remote_copy(
                src_ref=ref, dst_ref=ref, send_sem=send_sems.at[2 * a + k], recv_sem=recv_sems.at[2 * a + k],
                device_id=(*to, c), device_id_type=MESH))
    return copies


def _chips_copies(ps, lands, send_sems, recv_sems):
    x, y, c, chips = _position()
    copies = []
    for a in range(len(ps)):
        for j, chip in enumerate(chips):
            copies.append(pltpu.make_async_remote_copy(
                src_ref=ps[a].at[2 * chip[0] + chip[1]], dst_ref=lands[a].at[j],
                send_sem=send_sems.at[3 * a + j], recv_sem=recv_sems.at[3 * a + j], device_id=(*chip, c),
                device_id_type=MESH))
    return copies


def _sibling_copies(gs, lands, send_sems, recv_sems):
    x, y, c, _ = _position()
    copies = []
    for a in range(len(gs)):
        for q in range(4):
            copies.append(pltpu.make_async_remote_copy(
                src_ref=gs[a].at[2 * q + 1 - c], dst_ref=lands[a].at[q],
                send_sem=send_sems.at[4 * a + q], recv_sem=recv_sems.at[4 * a + q],
                device_id=(x, y, 1 - c), device_id_type=MESH))
    return copies


def _everyone_copies(xs, lands, send_sems, recv_sems):
    x, y, c, _ = _position()
    flip = lambda v, bit: 1 - v if bit else v
    copies = []
    for a in range(len(xs)):
        for k in range(1, NDEV):
            copies.append(pltpu.make_async_remote_copy(
                src_ref=xs[a], dst_ref=lands[a].at[4 * x + 2 * y + c],
                send_sem=send_sems.at[7 * a + k - 1], recv_sem=recv_sems.at[7 * a + k - 1],
                device_id=(flip(x, k & 4), flip(y, k & 2), flip(c, k & 1)), device_id_type=MESH))
    return copies


def _row_chunks(rows, dtype):
    unit = SUBLANE * (4 // jnp.dtype(dtype).itemsize)
    units = rows // unit
    if rows % unit or units < 2:
        return [(0, rows)]
    k = min(DMA_CHUNKS, units)
    sizes = [(units // k + (1 if i < units % k else 0)) * unit for i in range(k)]
    return [(sum(sizes[:i]), sz) for i, sz in enumerate(sizes)]


def _gather_forward(name, lands, after=()):
    n = len(lands)
    after = list(after)

    def body(*refs):
        ins, outs = refs[:n], refs[n + len(after):2 * n + len(after)]
        send_sems, recv_sems = refs[2 * n + len(after):]
        x, y, c, chips = _position()
        whole, chunks = [], []
        for a in range(n):
            rows = _row_chunks(ins[a].shape[1], ins[a].dtype)
            for j, chip in enumerate(chips):
                slot = 4 * chip[0] + 2 * chip[1]

                def to_sibling(src, dst):
                    return pltpu.make_async_remote_copy(
                        src_ref=src, dst_ref=dst, send_sem=send_sems.at[a, j], recv_sem=recv_sems.at[a, j],
                        device_id=(x, y, 1 - c), device_id_type=MESH)

                whole.append(to_sibling(ins[a].at[slot + c], outs[a].at[slot + 1 - c]))
                chunks += [to_sibling(ins[a].at[slot + c, pl.ds(r0, nr)], outs[a].at[slot + c, pl.ds(r0, nr)])
                           for r0, nr in rows]
        for cp in chunks:
            cp.start()
        for cp in whole:
            cp.wait()

    return pl.pallas_call(
        body, name=name,
        out_shape=[jax.ShapeDtypeStruct(l.shape, l.dtype) for l in lands],
        in_specs=[HBM] * n + [pl.BlockSpec(memory_space=pl.ANY)] * len(after), out_specs=[HBM] * n,
        input_output_aliases={a: a for a in range(n)},
        scratch_shapes=[pltpu.SemaphoreType.DMA((n, 3)), pltpu.SemaphoreType.DMA((n, 3))],
    )(*lands, *after)


def _sum_sibling(name, g, land, c_arr):
    _, R, C = g.shape
    tr = _pick(R, 512, SUBLANE)

    def body(c_ref, g_ref, l_ref, o_ref):
        o_ref[...] = (g_ref[...].astype(F32) + l_ref[...].astype(F32)).astype(o_ref.dtype)

    return pl.pallas_call(
        body, name=name,
        grid_spec=pltpu.PrefetchScalarGridSpec(
            num_scalar_prefetch=1, grid=(4, R // tr),
            in_specs=[pl.BlockSpec((None, tr, C), lambda q, i, cr: (2 * q + cr[0], i, 0)),
                      pl.BlockSpec((None, tr, C), lambda q, i, cr: (q, i, 0))],
            out_specs=pl.BlockSpec((None, tr, C), lambda q, i, cr: (q, i, 0))),
        out_shape=jax.ShapeDtypeStruct((4, R, C), g.dtype),
        compiler_params=pltpu.CompilerParams(dimension_semantics=("parallel", "parallel")),
    )(c_arr, g, land)


def _adamw(w, g, m, v):
    m = ADAM_B1 * m + (1.0 - ADAM_B1) * g
    v = ADAM_B2 * v + (1.0 - ADAM_B2) * (g * g)
    m_hat = m / (1.0 - ADAM_B1 ** ADAM_STEP)
    v_hat = v / (1.0 - ADAM_B2 ** ADAM_STEP)
    delta = -ADAM_LR * (m_hat / (jnp.sqrt(v_hat) + ADAM_EPS) + ADAM_WD * w)
    return delta, m, v


def _finish_sharded(name, sums, land, q_arr, w, m, v):
    R, C = w.shape
    tr = _pick(R, 512, SUBLANE)
    tc = _pick(C, max(LANE, EW_BLOCK // tr), LANE)

    def body(q_ref, p_ref, l_ref, w_ref, m_ref, v_ref, g_out, d_out, m_out, v_out):
        g = p_ref[...].astype(F32)
        for j in range(3):
            g = g + l_ref[j].astype(F32)
        d, mn, vn = _adamw(w_ref[...], g, m_ref[...], v_ref[...])
        g_out[...] = g
        d_out[...] = d
        m_out[...] = mn
        v_out[...] = vn

    blk = pl.BlockSpec((tr, tc), lambda i, j, qr: (i, j))
    return pl.pallas_call(
        body, name=name,
        grid_spec=pltpu.PrefetchScalarGridSpec(
            num_scalar_prefetch=1, grid=(R // tr, C // tc),
            in_specs=[pl.BlockSpec((None, tr, tc), lambda i, j, qr: (qr[0], i, j)),
                      pl.BlockSpec((3, tr, tc), lambda i, j, qr: (0, i, j)), blk, blk, blk],
            out_specs=[blk] * 4),
        out_shape=[jax.ShapeDtypeStruct((R, C), F32)] * 4,
        compiler_params=pltpu.CompilerParams(dimension_semantics=("parallel", "parallel")),
    )(q_arr, sums, land, w, m, v)


def _finish_replicated(name, gathered, w, m, v):
    _, R, C = gathered.shape
    tr = _pick(R, 256, SUBLANE)

    def fn(gv, wv, mv, vv):
        g = gv[0]
        for d in range(1, NDEV):
            g = g + gv[d]
        dl, mn, vn = _adamw(wv, g, mv, vv)
        return g, dl, mn, vn

    row = ((tr, C), lambda i: (i, 0))
    return _ew(name, (R // tr,),
               [(gathered, (NDEV, tr, C), lambda i: (0, i, 0)), (w,) + row, (m,) + row, (v,) + row],
               [((R, C), F32) + row] * 4, fn)


def _rms_fwd(name, h, g, deps=()):
    S, D = h.shape
    tr = _pick(S, 512, SUBLANE)

    def fn(hv, gv):
        r = lax.rsqrt(jnp.mean(hv * hv, axis=-1, keepdims=True) + EPS)
        return hv * r * gv, r

    return _ew(name, (S // tr,),
               [(h, (tr, D), lambda i: (i, 0)), (g, (1, D), lambda i: (0, 0))],
               [((S, D), BF, (tr, D), lambda i: (i, 0)), ((S, 1), F32, (tr, 1), lambda i: (i, 0))], fn, deps=deps)


def _rms_bwd(name, du, h, r, g, dres, scale):
    S, D = h.shape
    tr = _pick(S, 512, SUBLANE)

    def fn(duv, hv, rv, gv, drv):
        xn = hv * rv
        dxn = duv * gv
        dh = drv + rv * (dxn - xn * jnp.mean(dxn * xn, axis=-1, keepdims=True))
        return dh, scale * dh, jnp.sum(duv * xn, axis=0, keepdims=True)

    row = ((tr, D), lambda i: (i, 0))
    return _ew(name, (S // tr,),
               [(du,) + row, (h,) + row, (r, (tr, 1), lambda i: (i, 0)), (g, (1, D), lambda i: (0, 0)), (dres,) + row],
               [((S, D), F32) + row, ((S, D), BF) + row, ((1, D), F32, (1, D), lambda i: (0, 0))], fn, acc=(2,))


def _loss_head(name, h, g, target):
    S, D = h.shape
    tr = _pick(S, 512, SUBLANE)

    def fn(hv, gv, tv):
        r = lax.rsqrt(jnp.mean(hv * hv, axis=-1, keepdims=True) + EPS)
        xn = hv * r
        diff = xn * gv - tv
        loss = 0.5 * jnp.sum(jnp.mean(diff * diff, axis=-1, keepdims=True))
        dout = diff / D
        dxn = dout * gv
        dh = r * (dxn - xn * jnp.mean(dxn * xn, axis=-1, keepdims=True))
        return (jnp.zeros((1, LANE), F32) + loss, dh, 0.5 * dh, jnp.sum(dout * xn, axis=0, keepdims=True))

    row = ((tr, D), lambda i: (i, 0))
    return _ew(name, (S // tr,),
               [(h,) + row, (g, (1, D), lambda i: (0, 0)), (target,) + row],
               [((1, LANE), F32, (1, LANE), lambda i: (0, 0)), ((S, D), F32) + row, ((S, D), BF) + row,
                ((1, D), F32, (1, D), lambda i: (0, 0))], fn, acc=(0, 3))


def _ffn_fwd(tag, u, h, wg, fetch_wu, fetch_wd):
    S, D = u.shape
    Fs = wg.shape[1]
    tm, tk = _pick(S, M_TILE, SUBLANE), _pick(D, K_TILE, LANE)
    act = ((NDEV, S, Fs), BF, (None, tm, Fs), lambda b, i, j, k: (b, i, 0))
    lhs = (u, (tm, tk), lambda b, i, j, k: (i, k))
    rhs = lambda w: (w, (None, Fs, tk), lambda b, i, j, k: (b, 0, k))
    (gt,) = _mm(tag + "_gate", (NDEV, S // tm, 1, D // tk), [lhs, rhs(wg)], [(0, 1, NT, 0)], [], [act],
                lambda accs, sides: accs, [(tm, Fs)])
    wu = fetch_wu(gt)

    def up_epilogue(accs, sides):
        g, up = sides[0].astype(F32), accs[0]
        sg = _sigmoid(g)
        silu = g * sg
        return up * sg * (1.0 + g * (1.0 - sg)), silu, silu * up

    gt, up, a = _mm(tag + "_up", (NDEV, S // tm, 1, D // tk), [lhs, rhs(wu)], [(0, 1, NT, 0)],
                    [(gt, (None, tm, Fs), lambda b, i, j, k: (b, i, 0))], [act, act, act], up_epilogue, [(tm, Fs)])
    wd = fetch_wd(a)
    tn = _pick(D, 1024, LANE)
    (hn,) = _mm(
        tag + "_down", (1, S // tm, D // tn, NDEV // K_SHARDS),
        [(a, (K_SHARDS, tm, Fs), lambda b, i, j, k: (k, i, 0)), (wd, (K_SHARDS, Fs, tn), lambda b, i, j, k: (k, 0, j))],
        [(0, 1, NN, 0)], [(h, (tm, tn), lambda b, i, j, k: (i, j))],
        [((S, D), F32, (tm, tn), lambda b, i, j, k: (i, j))],
        lambda accs, sides: [sides[0] + 0.5 * accs[0]], [(tm, tn)])
    return hn, (gt, up, a), wu, wd


def _ffn_bwd(tag, dhs, u, saved, wg, wu, wd, deps=()):
    gt, up, a = saved
    S, D = u.shape
    Fs = wg.shape[1]
    tm, tk = _pick(S, M_TILE, SUBLANE), _pick(D, K_TILE, LANE)
    act_in = lambda arr: (arr, (None, tm, Fs), lambda b, i, j, k: (b, i, 0))
    act_out = ((NDEV, S, Fs), BF, (None, tm, Fs), lambda b, i, j, k: (b, i, 0))

    def act_epilogue(accs, sides):
        da = accs[0]
        return da * sides[0].astype(F32), da * sides[1].astype(F32)

    dgt, dup = _mm(
        tag + "_dact", (NDEV, S // tm, 1, D // tk),
        [(dhs, (tm, tk), lambda b, i, j, k: (i, k)), (wd, (None, Fs, tk), lambda b, i, j, k: (b, 0, k))],
        [(0, 1, NT, 0)], [act_in(gt), act_in(up)], [act_out, act_out], act_epilogue, [(tm, Fs)], deps=deps)

    ts = _pick(S, K_TILE, SUBLANE)
    tn = _pick(D, 1024, LANE)
    wgrad = ((NDEV, Fs, D), BF, (None, Fs, tn), lambda b, i, j, k: (b, 0, j))
    tok = lambda arr: (arr, (None, ts, Fs), lambda b, i, j, k: (b, k, 0))

    def grad_down(deps=()):
        return _mm(
            tag + "_dwd", (NDEV, 1, D // tn, S // ts),
            [tok(a), (dhs, (ts, tn), lambda b, i, j, k: (k, j))],
            [(0, 1, TN, 0)], [], [wgrad], lambda accs, sides: accs, [(Fs, tn)], deps=deps)[0]

    def grad_gate_up(deps=()):
        return _mm(
            tag + "_dwgu", (NDEV, 1, D // tn, S // ts),
            [tok(dgt), tok(dup), (u, (ts, tn), lambda b, i, j, k: (k, j))],
            [(0, 2, TN, 0), (1, 2, TN, 1)], [], [wgrad, wgrad], lambda accs, sides: accs, [(Fs, tn), (Fs, tn)],
            deps=deps)

    def du(deps=()):
        return _mm(
            tag + "_du", (1, S // tm, D // tn, NDEV // K_SHARDS),
            [(dgt, (K_SHARDS, tm, Fs), lambda b, i, j, k: (k, i, 0)), (dup, (K_SHARDS, tm, Fs), lambda b, i, j, k: (k, i, 0)),
             (wg, (K_SHARDS, Fs, tn), lambda b, i, j, k: (k, 0, j)), (wu, (K_SHARDS, Fs, tn), lambda b, i, j, k: (k, 0, j))],
            [(0, 2, NN, 0), (1, 3, NN, 0)], [], [((S, D), F32, (tm, tn), lambda b, i, j, k: (i, j))],
            lambda accs, sides: accs, [(tm, tn)], deps=deps)[0]

    return grad_down, grad_gate_up, du


def _ssm_params(lam_re, lam_im, log_dt, b_re, b_im, c_re, c_im):
    G, N = lam_re.shape
    C = b_re.shape[2]
    lam_re = jnp.minimum(lam_re, -1e-4)
    dt = jnp.exp(log_dt)[:, None]
    mag = jnp.exp(lam_re * dt)
    a_re = mag * jnp.cos(lam_im * dt)
    a_im = mag * jnp.sin(lam_im * dt)
    den = lam_re * lam_re + lam_im * lam_im
    p = a_re - 1.0
    f_re = ((p * lam_re + a_im * lam_im) / den)[:, :, None]
    f_im = ((a_im * lam_re - p * lam_im) / den)[:, :, None]
    bb_re = f_re * b_re - f_im * b_im
    bb_im = f_re * b_im + f_im * b_re
    gpt = LANE // C
    tiles = G // gpt
    eye = jnp.eye(gpt, dtype=F32)

    def bd(bb):
        return jnp.einsum("bgnc,gh->bgchn", bb.reshape(tiles, gpt, N, C), eye).reshape(tiles, gpt * C, gpt * N)

    def cd(cc):
        return jnp.einsum("bgcn,gh->bgnhc", cc.reshape(tiles, gpt, C, N), eye).reshape(tiles, gpt * N, gpt * C)

    rows = G * N // LANE
    return (a_re.reshape(rows, LANE), a_im.reshape(rows, LANE), bd(bb_re), bd(bb_im), cd(c_re), cd(-c_im))


def _tile_states(ref3, b, per):
    return jnp.concatenate([ref3[:, per * b + r, :] for r in range(per)], axis=1).astype(BF)


def _ssm_spread(name, x, m_re, m_im, dims, rows):
    S, W = x.shape
    tiles = m_re.shape[0]
    tch, per = W // tiles, rows // tiles
    tq = _pick(S, 256, SUBLANE)

    def body(x_ref, mre_ref, mim_ref, ore_ref, oim_ref):
        for b in range(tiles):
            xb = x_ref[:, b * tch:(b + 1) * tch]
            for m_ref, o_ref in ((mre_ref, ore_ref), (mim_ref, oim_ref)):
                val = lax.dot_general(xb, m_ref[b], (dims, ((), ())), preferred_element_type=F32)
                for r in range(per):
                    o_ref[:, per * b + r, :] = val[:, r * LANE:(r + 1) * LANE]

    whole = lambda m: pl.BlockSpec(m.shape, lambda i: (0, 0, 0))
    st = pl.BlockSpec((tq, rows, LANE), lambda i: (i, 0, 0))
    return pl.pallas_call(
        body, name=name, grid=(S // tq,),
        in_specs=[pl.BlockSpec((tq, W), lambda i: (i, 0)), whole(m_re), whole(m_im)], out_specs=[st, st],
        out_shape=[jax.ShapeDtypeStruct((S, rows, LANE), F32)] * 2,
        compiler_params=pltpu.CompilerParams(dimension_semantics=("parallel",)),
    )(x, m_re, m_im)


def _ssm_collect(name, z_re3, z_im3, m_re, m_im, dims, side, gain, epilogue, out_dtypes):
    S, rows, _ = z_re3.shape
    tiles = m_re.shape[0]
    W = side.shape[1]
    tch, per = W // tiles, rows // tiles
    tq = _pick(S, 256, SUBLANE)
    n_out = len(out_dtypes)

    def body(zre_ref, zim_ref, mre_ref, mim_ref, side_ref, gain_ref, *out_refs):
        for b in range(tiles):
            cols = slice(b * tch, (b + 1) * tch)
            zre, zim = _tile_states(zre_ref, b, per), _tile_states(zim_ref, b, per)
            acc = lax.dot_general(zre, mre_ref[b], (dims, ((), ())), preferred_element_type=F32)
            acc = acc + lax.dot_general(zim, mim_ref[b], (dims, ((), ())), preferred_element_type=F32)
            for o, v in zip(out_refs[:n_out], epilogue(acc, side_ref[:, cols], gain_ref[:, cols])):
                o[:, cols] = v.astype(o.dtype)
            out_refs[n_out][:, b * per * LANE:(b + 1) * per * LANE] = zre
            out_refs[n_out + 1][:, b * per * LANE:(b + 1) * per * LANE] = zim

    whole = lambda m: pl.BlockSpec(m.shape, lambda i: (0, 0, 0))
    st = pl.BlockSpec((tq, rows, LANE), lambda i: (i, 0, 0))
    ch = pl.BlockSpec((tq, W), lambda i: (i, 0))
    flat = pl.BlockSpec((tq, rows * LANE), lambda i: (i, 0))
    return pl.pallas_call(
        body, name=name, grid=(S // tq,),
        in_specs=[st, st, whole(m_re), whole(m_im), ch, pl.BlockSpec((1, W), lambda i: (0, 0))],
        out_specs=[ch] * n_out + [flat, flat],
        out_shape=[jax.ShapeDtypeStruct((S, W), dt) for dt in out_dtypes]
        + [jax.ShapeDtypeStruct((S, rows * LANE), BF)] * 2,
        compiler_params=pltpu.CompilerParams(dimension_semantics=("parallel",)),
    )(z_re3, z_im3, m_re, m_im, side, gain)


def _scan_fwd(bu_re, bu_im, a_re, a_im):
    S, R, _ = bu_re.shape
    tc = _pick(S, 256, SUBLANE)

    def body(bre, bim, are, aim, sre, sim, carry):
        @pl.when(pl.program_id(0) == 0)
        def _():
            carry[...] = jnp.zeros_like(carry)

        ar, ai = are[...], aim[...]

        def step(t, c):
            pr, pi = c
            nr = ar * pr - ai * pi + bre[t]
            ni = ar * pi + ai * pr + bim[t]
            sre[t] = nr
            sim[t] = ni
            return nr, ni

        pr, pi = lax.fori_loop(0, tc, step, (carry[0], carry[1]), unroll=8)
        carry[0] = pr
        carry[1] = pi

    blk = pl.BlockSpec((tc, R, LANE), lambda i: (i, 0, 0))
    par = pl.BlockSpec((R, LANE), lambda i: (0, 0))
    return pl.pallas_call(
        body, name="ssm_scan_fwd", grid=(S // tc,),
        in_specs=[blk, blk, par, par], out_specs=[blk, blk],
        out_shape=[jax.ShapeDtypeStruct((S, R, LANE), F32)] * 2,
        scratch_shapes=[pltpu.VMEM((2, R, LANE), F32)],
        compiler_params=pltpu.CompilerParams(dimension_semantics=("arbitrary",)),
    )(bu_re, bu_im, a_re, a_im)


def _scan_bwd(ds_re, ds_im, s_re, s_im, a_re, a_im):
    S, R, _ = ds_re.shape
    tc = _pick(S, 256, SUBLANE)
    nc = S // tc

    def body(dre, dim_, sre, sim, are, aim, lre, lim, dar, dai, carry):
        @pl.when(pl.program_id(0) == 0)
        def _():
            carry[...] = jnp.zeros_like(carry)
            dar[...] = jnp.zeros_like(dar)
            dai[...] = jnp.zeros_like(dai)

        ar, ai = are[...], aim[...]

        def step(tt, c):
            t = tc - 1 - tt
            lr, li, gr, gi = c
            sr, si = sre[t], sim[t]
            gr = gr + lr * sr + li * si
            gi = gi + li * sr - lr * si
            nlr = dre[t] + ar * lr + ai * li
            nli = dim_[t] + ar * li - ai * lr
            lre[t] = nlr
            lim[t] = nli
            return nlr, nli, gr, gi

        lr, li, gr, gi = lax.fori_loop(0, tc, step, (carry[0], carry[1], dar[...], dai[...]), unroll=8)
        carry[0] = lr
        carry[1] = li
        dar[...] = gr
        dai[...] = gi

    blk = pl.BlockSpec((tc, R, LANE), lambda i: (nc - 1 - i, 0, 0))
    par = pl.BlockSpec((R, LANE), lambda i: (0, 0))
    return pl.pallas_call(
        body, name="ssm_scan_bwd", grid=(nc,),
        in_specs=[blk, blk, blk, blk, par, par], out_specs=[blk, blk, par, par],
        out_shape=[jax.ShapeDtypeStruct((S, R, LANE), F32)] * 2 + [jax.ShapeDtypeStruct((R, LANE), F32)] * 2,
        scratch_shapes=[pltpu.VMEM((2, R, LANE), F32)],
        compiler_params=pltpu.CompilerParams(dimension_semantics=("arbitrary",)),
    )(ds_re, ds_im, s_re, s_im, a_re, a_im)


def _shift_down(z, k):
    t = lax.broadcasted_iota(I32, z.shape, 0)
    return jnp.where(t >= k, pltpu.roll(z, k, 0), 0.0)


def _shift_up(z, k):
    n = z.shape[0]
    t = lax.broadcasted_iota(I32, z.shape, 0)
    return jnp.where(t < n - k, pltpu.roll(z, n - k, 0), 0.0)


def _conv_fwd(proj, cw, cb):
    _, S, W = proj.shape
    ct = _pick(W, 256, LANE)

    def fn(bg, cg, val, w, b):
        z = cg * val
        conv = b + w[0:1] * _shift_down(z, 2) + w[1:2] * _shift_down(z, 1) + w[2:3] * z
        return bg * conv, conv

    sl = lambda s: (proj, (None, S, ct), lambda j, s=s: (s, 0, j))
    col = ((S, ct), lambda j: (0, j))
    return _ew("conv_fwd", (W // ct,),
               [sl(1), sl(2), sl(3), (cw, (3, ct), lambda j: (0, j)), (cb, (1, ct), lambda j: (0, j))],
               [((S, W), BF) + col, ((S, W), F32) + col], fn)


def _conv_bwd(dyb, proj, conv, cw):
    _, S, W = proj.shape
    ct = _pick(W, 256, LANE)

    def fn(dy, bg, cg, val, cv, w):
        z = cg * val
        z1, z2 = _shift_down(z, 1), _shift_down(z, 2)
        dconv = dy * bg
        dz = w[2:3] * dconv + w[1:2] * _shift_up(dconv, 1) + w[0:1] * _shift_up(dconv, 2)
        dw = jnp.concatenate([jnp.sum(dconv * z2, axis=0, keepdims=True), jnp.sum(dconv * z1, axis=0, keepdims=True),
                              jnp.sum(dconv * z, axis=0, keepdims=True)], axis=0)
        return dy * cv, dz * val, dz * cg, dw, jnp.sum(dconv, axis=0, keepdims=True)

    sl = lambda s: (proj, (None, S, ct), lambda j, s=s: (s, 0, j))
    col = ((S, ct), lambda j: (0, j))
    return _ew("conv_bwd", (W // ct,),
               [(dyb,) + col, sl(1), sl(2), sl(3), (conv,) + col, (cw, (3, ct), lambda j: (0, j))],
               [((S, W), BF) + col, ((S, W), BF) + col, ((S, W), BF) + col,
                ((3, W), F32, (3, ct), lambda j: (0, j)), ((1, W), F32, (1, ct), lambda j: (0, j))], fn)


def _plain(accs, sides):
    return accs


def kernel(x, ffn1_norm, ffn1_w_gate, ffn1_w_up, ffn1_w_down, mix_norm, w_in, ssm_lambda_re, ssm_lambda_im, ssm_log_dt, ssm_b_re, ssm_b_im, ssm_c_re, ssm_c_im, ssm_d, ssm_w_glu, ssm_b_glu, ssm_w_out, conv_w, conv_b, conv_w_out, w_o, ffn2_norm, ffn2_w_gate, ffn2_w_up, ffn2_w_down, final_norm, loss_target, m_ffn1_norm, m_ffn1_w_gate, m_ffn1_w_up, m_ffn1_w_down, m_mix_norm, m_w_in, m_ssm_lambda_re, m_ssm_lambda_im, m_ssm_log_dt, m_ssm_b_re, m_ssm_b_im, m_ssm_c_re, m_ssm_c_im, m_ssm_d, m_ssm_w_glu, m_ssm_b_glu, m_ssm_w_out, m_conv_w, m_conv_b, m_conv_w_out, m_w_o, m_ffn2_norm, m_ffn2_w_gate, m_ffn2_w_up, m_ffn2_w_down, m_final_norm, v_ffn1_norm, v_ffn1_w_gate, v_ffn1_w_up, v_ffn1_w_down, v_mix_norm, v_w_in, v_ssm_lambda_re, v_ssm_lambda_im, v_ssm_log_dt, v_ssm_b_re, v_ssm_b_im, v_ssm_c_re, v_ssm_c_im, v_ssm_d, v_ssm_w_glu, v_ssm_b_glu, v_ssm_w_out, v_conv_w, v_conv_b, v_conv_w_out, v_w_o, v_ffn2_norm, v_ffn2_w_gate, v_ffn2_w_up, v_ffn2_w_down, v_final_norm):
    P = dict(ffn1_norm=ffn1_norm, ffn1_w_gate=ffn1_w_gate, ffn1_w_up=ffn1_w_up, ffn1_w_down=ffn1_w_down, mix_norm=mix_norm, w_in=w_in, ssm_lambda_re=ssm_lambda_re, ssm_lambda_im=ssm_lambda_im, ssm_log_dt=ssm_log_dt, ssm_b_re=ssm_b_re, ssm_b_im=ssm_b_im, ssm_c_re=ssm_c_re, ssm_c_im=ssm_c_im, ssm_d=ssm_d, ssm_w_glu=ssm_w_glu, ssm_b_glu=ssm_b_glu, ssm_w_out=ssm_w_out, conv_w=conv_w, conv_b=conv_b, conv_w_out=conv_w_out, w_o=w_o, ffn2_norm=ffn2_norm, ffn2_w_gate=ffn2_w_gate, ffn2_w_up=ffn2_w_up, ffn2_w_down=ffn2_w_down, final_norm=final_norm)
    M = dict(ffn1_norm=m_ffn1_norm, ffn1_w_gate=m_ffn1_w_gate, ffn1_w_up=m_ffn1_w_up, ffn1_w_down=m_ffn1_w_down, mix_norm=m_mix_norm, w_in=m_w_in, ssm_lambda_re=m_ssm_lambda_re, ssm_lambda_im=m_ssm_lambda_im, ssm_log_dt=m_ssm_log_dt, ssm_b_re=m_ssm_b_re, ssm_b_im=m_ssm_b_im, ssm_c_re=m_ssm_c_re, ssm_c_im=m_ssm_c_im, ssm_d=m_ssm_d, ssm_w_glu=m_ssm_w_glu, ssm_b_glu=m_ssm_b_glu, ssm_w_out=m_ssm_w_out, conv_w=m_conv_w, conv_b=m_conv_b, conv_w_out=m_conv_w_out, w_o=m_w_o, ffn2_norm=m_ffn2_norm, ffn2_w_gate=m_ffn2_w_gate, ffn2_w_up=m_ffn2_w_up, ffn2_w_down=m_ffn2_w_down, final_norm=m_final_norm)
    V = dict(ffn1_norm=v_ffn1_norm, ffn1_w_gate=v_ffn1_w_gate, ffn1_w_up=v_ffn1_w_up, ffn1_w_down=v_ffn1_w_down, mix_norm=v_mix_norm, w_in=v_w_in, ssm_lambda_re=v_ssm_lambda_re, ssm_lambda_im=v_ssm_lambda_im, ssm_log_dt=v_ssm_log_dt, ssm_b_re=v_ssm_b_re, ssm_b_im=v_ssm_b_im, ssm_c_re=v_ssm_c_re, ssm_c_im=v_ssm_c_im, ssm_d=v_ssm_d, ssm_w_glu=v_ssm_w_glu, ssm_b_glu=v_ssm_b_glu, ssm_w_out=v_ssm_w_out, conv_w=v_conv_w, conv_b=v_conv_b, conv_w_out=v_conv_w_out, w_o=v_w_o, ffn2_norm=v_ffn2_norm, ffn2_w_gate=v_ffn2_w_gate, ffn2_w_up=v_ffn2_w_up, ffn2_w_down=v_ffn2_w_down, final_norm=v_final_norm)
    names = list(P)
    sharded = ["ffn1_w_gate", "ffn1_w_up", "ffn1_w_down", "w_in", "ssm_w_glu", "ssm_w_out", "conv_w_out", "w_o",
               "ffn2_w_gate", "ffn2_w_up", "ffn2_w_down"]
    replicated = [n for n in names if n not in sharded and n != "conv_w"]

    S, D = x.shape[1], x.shape[2]
    W = ssm_d.shape[0]
    Dc = D // NDEV
    G, N = ssm_lambda_re.shape
    rows = G * N // LANE
    xh = x.reshape(S, D)
    target = loss_target.reshape(S, D)
    xi, yi, ci = lax.axis_index("x"), lax.axis_index("y"), lax.axis_index("c")
    c_arr = jnp.reshape(ci, (1,)).astype(I32)
    q_arr = jnp.reshape(2 * xi + yi, (1,)).astype(I32)
    row = lambda v: v.reshape(1, -1)

    transposed = ("ffn1_w_gate", "ffn1_w_up", "ffn2_w_gate", "ffn2_w_up")
    local = lambda table, n: table[n].T if n in transposed else table[n]

    groups = [sharded[0:1], sharded[1:2], sharded[2:3], ["w_in", "conv_w"], sharded[4:8],
              sharded[8:9], sharded[9:10], sharded[10:11]]
    first_begun, second_begun, chain = {}, {}, []
    me = 4 * xi + 2 * yi + ci

    def gather_first(gi):
        hold = 0.0 * chain[0][0, 0] if chain else 0.0
        srcs = [conv_w + hold if n == "conv_w" else (local(P, n) + hold).astype(BF) for n in groups[gi]]
        lands = [lax.dynamic_update_slice(lax.empty((NDEV,) + s.shape, s.dtype), s[None], (me,) + (0,) * s.ndim)
                 for s in srcs]
        first_begun[gi] = _split_start("gather_first_start_%d" % gi, srcs, lands, _gather_first_copies, 3, chain[-1:])
        chain.append(first_begun[gi][4])

    def gather_second(gi, after):
        _, lands = _split_wait("gather_first_wait_%d" % gi, first_begun[gi], _gather_first_copies, after)
        second_begun[gi] = _split_start("gather_second_start_%d" % gi, [], lands, _gather_second_copies, 2, chain[-1:])
        chain.append(second_begun[gi][4])

    def gathered(gi, after):
        _, lands = _split_wait("gather_second_wait_%d" % gi, second_begun[gi], _gather_second_copies, after)
        return _gather_forward("gather_forward_%d" % gi, lands, chain[-1:])

    def fetch(gi, seconds, firsts):
        def get(after):
            for g in seconds:
                gather_second(g, after)
            for g in firsts:
                gather_first(g)
            return gathered(gi, after)
        return get

    tm = _pick(S, M_TILE, SUBLANE)
    th = _pick(S, M_TILE // 2, SUBLANE)
    tk = _pick(D, K_TILE, LANE)
    ts = _pick(S, K_TILE, SUBLANE)
    tn = _pick(D, 1024, LANE)

    gather_first(0)
    gather_first(1)
    u1, r1 = _rms_fwd("rms1", xh, row(ffn1_norm), deps=list(chain))

    hold = 0.0 * chain[0][0, 0]
    replicated = [n for n in replicated if n != "ffn1_norm"] + ["ffn1_norm"]
    tile = SUBLANE * LANE

    def as_rows(p):
        flat = p.reshape(-1).astype(F32)
        return jnp.pad(flat, (0, -flat.shape[0] % tile)).reshape(-1, LANE)

    def pack(parts):
        return jnp.concatenate([as_rows(p) for p in parts], axis=0)

    cw_zero = jnp.zeros((conv_w.shape[0], W), F32)
    packed_state = [pack([t[n] + hold for n in replicated[:-1]] + [cw_zero + fill, t["ffn1_norm"] + hold])
                    for t, fill in ((P, 0.0), (M, 0.0), (V, 1.0))]
    ssm_in = tuple(p + hold for p in (ssm_lambda_re, ssm_lambda_im, ssm_log_dt, ssm_b_re, ssm_b_im, ssm_c_re, ssm_c_im))
    (a_re, a_im, bd_re, bd_im, cd_re, cd_imn), ssm_vjp = jax.vjp(_ssm_params, *ssm_in)
    bd_re_b, bd_im_b, cd_re_b, cd_imn_b = (t.astype(BF) for t in (bd_re, bd_im, cd_re, cd_imn))
    gather_second(0, [u1, a_re, a_im, bd_re_b, bd_im_b, cd_re_b, cd_imn_b] + packed_state)
    gather_first(2)
    gather_second(1, u1)
    gather_first(3)
    (wg1,) = gathered(0, u1)
    h1, ffn1_saved, wu1, wd1 = _ffn_fwd("ffn1", u1, xh, wg1, lambda after: fetch(1, [2], [4])(after)[0],
                                        lambda after: fetch(2, [3], [5])(after)[0])
    u2, r2 = _rms_fwd("rms2", h1, row(mix_norm))
    w_in_f, cw_f = fetch(3, [4], [6])(u2)
    cw = jnp.transpose(cw_f, (1, 0, 2)).reshape(3, W)
    (proj,) = _mm(
        "in_proj", (NDEV, S // tm, 1, D // tk),
        [(u2, (tm, tk), lambda b, i, j, k: (i, k)), (w_in_f, (None, tk, W), lambda b, i, j, k: (b, k, 0))],
        [(0, 1, NN, 0)], [], [((NDEV, S, W), F32, (None, tm, W), lambda b, i, j, k: (b, i, 0))], _plain, [(tm, W)])

    v_f = proj[0]
    v_bf = v_f.astype(BF)
    bu_re3, bu_im3 = _ssm_spread("ssm_bu", v_bf, bd_re_b, bd_im_b, NN, rows)
    s_re3, s_im3 = _scan_fwd(bu_re3, bu_im3, a_re, a_im)

    def y0_epilogue(acc, v_tile, d_tile):
        y0 = acc + d_tile * v_tile
        return y0, _gelu(y0)

    y0, y1, s_re_b, s_im_b = _ssm_collect("ssm_y0", s_re3, s_im3, cd_re_b, cd_imn_b, NN, v_f, row(ssm_d),
                                          y0_epilogue, [F32, BF])

    tw = _pick(W, 512, LANE)
    w_glu_f, w_so, w_co, w_o_f = fetch(4, [5], [7])(y1)
    w_glu_f = w_glu_f.reshape(W, W)
    w_o_f = w_o_f.reshape(D, D)

    def glu_epilogue(accs, sides):
        q = accs[0] + sides[1]
        return q, _gelu(sides[0]) * _sigmoid(q)

    q_pre, y2 = _mm("ssm_glu", (1, S // tm, W // tw, 1),
                    [(y1, (tm, W), lambda b, i, j, k: (i, 0)), (w_glu_f, (W, tw), lambda b, i, j, k: (0, j))],
                    [(0, 1, NN, 0)],
                    [(y0, (tm, tw), lambda b, i, j, k: (i, j)), (row(ssm_b_glu), (1, tw), lambda b, i, j, k: (0, j))],
                    [((S, W), F32, (tm, tw), lambda b, i, j, k: (i, j)), ((S, W), BF, (tm, tw), lambda b, i, j, k: (i, j))],
                    glu_epilogue, [(tm, tw)])

    yb, conv = _conv_fwd(proj, cw, row(conv_b))

    per = W // Dc
    ga_blk = (proj, (None, tm, Dc), lambda b, i, j, k: (4 + b // per, i, b % per))
    gb_blk = (proj, (None, tm, Dc), lambda b, i, j, k: (6 + b // per, i, b % per))
    dc_out = ((S, D), BF, (tm, Dc), lambda b, i, j, k: (i, b))

    def merge_epilogue(accs, sides):
        za, zb = accs
        return _sigmoid(sides[0]) * za + _sigmoid(sides[1]) * zb, za, zb

    merged, z_a, z_b = _mm(
        "mix_merge", (NDEV, S // tm, 1, 1),
        [(y2, (tm, W), lambda b, i, j, k: (i, 0)), (yb, (tm, W), lambda b, i, j, k: (i, 0)),
         (w_so, (None, W, Dc), lambda b, i, j, k: (b, 0, 0)), (w_co, (None, W, Dc), lambda b, i, j, k: (b, 0, 0))],
        [(0, 2, NN, 0), (1, 3, NN, 1)], [ga_blk, gb_blk], [dc_out, dc_out, dc_out], merge_epilogue, [(tm, Dc)] * 2)

    (h2,) = _mm("mix_out", (1, S // tm, D // tn, D // tk),
                [(merged, (tm, tk), lambda b, i, j, k: (i, k)), (w_o_f, (tk, tn), lambda b, i, j, k: (k, j))],
                [(0, 1, NN, 0)], [(h1, (tm, tn), lambda b, i, j, k: (i, j))],
                [((S, D), F32, (tm, tn), lambda b, i, j, k: (i, j))],
                lambda accs, sides: [sides[0] + accs[0]], [(tm, tn)])

    u3, r3 = _rms_fwd("rms3", h2, row(ffn2_norm))
    (wg2,) = fetch(5, [6, 7], [])(u3)
    h3, ffn2_saved, wu2, wd2 = _ffn_fwd("ffn2", u3, h2, wg2, lambda after: gathered(6, after)[0],
                                        lambda after: gathered(7, after)[0])
    loss_vec, dh3, dh3_half, d_final_norm = _loss_head("loss_head", h3, row(final_norm), target)
    loss = lax.psum(loss_vec[0, 0], ("x", "y", "c"))
    loss_done = jnp.zeros((SUBLANE, LANE), F32) + loss

    grads, deltas, new_m, new_v = {}, {}, {}, {}

    def rs_sibling_start(tag, parts):
        lands = [lax.empty((4,) + p.shape[1:], p.dtype) for p in parts]
        return _split_start("rs_sibling_start_" + tag, parts, lands, _sibling_copies, 4)

    def rs_chips_start(tag, sibling_begun, after):
        parts, lands = _split_wait("rs_sibling_wait_" + tag, sibling_begun, _sibling_copies, after)
        sums = [_sum_sibling("rs_sum_%s_%d" % (tag, a), p, land, c_arr) for a, (p, land) in enumerate(zip(parts, lands))]
        lands2 = [lax.empty((3,) + sm.shape[1:], sm.dtype) for sm in sums]
        return _split_start("rs_chips_start_" + tag, sums, lands2, _chips_copies, 3)

    def rs_end(tag, group, begun, after):
        sums, lands2 = _split_wait("rs_chips_wait_" + tag, begun, _chips_copies, after)
        for n, sm, land2 in zip(group, sums, lands2):
            res = _finish_sharded("adamw_" + n, sm, land2, q_arr, local(P, n), local(M, n), local(V, n))
            grads[n], deltas[n], new_m[n], new_v[n] = [t.T if n in transposed else t for t in res]

    f2_dwd, f2_dwgu, f2_du = _ffn_bwd("ffn2", dh3_half, u3, ffn2_saved, wg2, wu2, wd2, deps=[loss_done])
    dwd2 = f2_dwd()
    dwg2, dwu2 = f2_dwgu()
    sib_ffn2 = rs_sibling_start("ffn2", [dwg2, dwu2, dwd2])
    du3 = f2_du(deps=[sib_ffn2[4]])
    dh2, dh2_b, d_ffn2_norm = _rms_bwd("rms3_bwd", du3, h2, r3, row(ffn2_norm), dh3, 1.0)
    rs_ffn2 = rs_chips_start("ffn2", sib_ffn2, dh2)

    tg = _pick(W, 512, Dc)
    per2 = W // tg
    dg_out = ((2, S, W), BF, (None, tm, tg), lambda b, i, j, k: (j // per2, i, j % per2))
    ga_blk2 = (proj, (None, tm, tg), lambda b, i, j, k: (4 + j // per2, i, j % per2))
    gb_blk2 = (proj, (None, tm, tg), lambda b, i, j, k: (6 + j // per2, i, j % per2))
    dcj = lambda arr: (arr, (tm, tg), lambda b, i, j, k: (i, j))
    dcj_out = ((S, D), BF, (tm, tg), lambda b, i, j, k: (i, j))

    def dmerge_epilogue(accs, sides):
        dm = accs[0]
        sa, sb = _sigmoid(sides[0]), _sigmoid(sides[1])
        za, zb = sides[2].astype(F32), sides[3].astype(F32)
        return dm * sa, dm * sb, dm * za * sa * (1.0 - sa), dm * zb * sb * (1.0 - sb)

    dz_a, dz_b, dga, dgb = _mm(
        "mix_out_dx", (1, S // tm, D // tg, D // tk),
        [(dh2_b, (tm, tk), lambda b, i, j, k: (i, k)), (w_o_f, (tg, tk), lambda b, i, j, k: (j, k))],
        [(0, 1, NT, 0)], [ga_blk2, gb_blk2, dcj(z_a), dcj(z_b)], [dcj_out, dcj_out, dg_out, dg_out],
        dmerge_epilogue, [(tm, tg)], deps=[rs_ffn2[4]])

    td = _pick(D, M_TILE, LANE)
    (dw_o,) = _mm("mix_out_dw", (1, D // td, D // tn, S // ts),
                  [(merged, (ts, td), lambda b, i, j, k: (k, i)), (dh2_b, (ts, tn), lambda b, i, j, k: (k, j))],
                  [(0, 1, TN, 0)], [], [((D, D), BF, (td, tn), lambda b, i, j, k: (i, j))], _plain, [(td, tn)])

    wout = ((NDEV, W, Dc), BF, (None, W, Dc), lambda b, i, j, k: (b, 0, 0))
    dw_so, dw_co = _mm(
        "mix_merge_dw", (NDEV, 1, 1, S // ts),
        [(y2, (ts, W), lambda b, i, j, k: (k, 0)), (yb, (ts, W), lambda b, i, j, k: (k, 0)),
         (dz_a, (ts, Dc), lambda b, i, j, k: (k, b)), (dz_b, (ts, Dc), lambda b, i, j, k: (k, b))],
        [(0, 2, TN, 0), (1, 3, TN, 1)], [], [wout, wout], _plain, [(W, Dc)] * 2)

    def dglu_epilogue(accs, sides):
        dy2, dyb = accs
        sq = _sigmoid(sides[1])
        return dy2 * _gelu(sides[0]) * sq * (1.0 - sq), dy2 * sq, dyb

    full_w = lambda arr: (arr, (th, W), lambda b, i, j, k: (i, 0))
    full_w_out = lambda dt: ((S, W), dt, (th, W), lambda b, i, j, k: (i, 0))
    dq, dy1p, dyb = _mm(
        "mix_merge_dx", (1, S // th, 1, NDEV),
        [(dz_a, (th, Dc), lambda b, i, j, k: (i, k)), (dz_b, (th, Dc), lambda b, i, j, k: (i, k)),
         (w_so, (None, W, Dc), lambda b, i, j, k: (k, 0, 0)), (w_co, (None, W, Dc), lambda b, i, j, k: (k, 0, 0))],
        [(0, 2, NT, 0), (1, 3, NT, 1)], [full_w(y0), full_w(q_pre)], [full_w_out(BF), full_w_out(F32), full_w_out(F32)],
        dglu_epilogue, [(th, W)] * 2)

    def dy0_epilogue(accs, sides):
        dy0 = (sides[0] + accs[0]) * _gelu_grad(sides[1])
        return dy0, dy0

    wj = lambda arr: (arr, (tm, tw), lambda b, i, j, k: (i, j))
    dy0, dy0_b = _mm("ssm_glu_dx", (1, S // tm, W // tw, 1),
                     [(dq, (tm, W), lambda b, i, j, k: (i, 0)), (w_glu_f, (tw, W), lambda b, i, j, k: (j, 0))],
                     [(0, 1, NT, 0)], [wj(dy1p), wj(y0)],
                     [((S, W), F32, (tm, tw), lambda b, i, j, k: (i, j)), ((S, W), BF, (tm, tw), lambda b, i, j, k: (i, j))],
                     dy0_epilogue, [(tm, tw)])

    (dw_glu,) = _mm("ssm_glu_dw", (1, W // tw, 1, S // ts),
                    [(y1, (ts, tw), lambda b, i, j, k: (k, i)), (dq, (ts, W), lambda b, i, j, k: (k, 0))],
                    [(0, 1, TN, 0)], [], [((W, W), BF, (tw, W), lambda b, i, j, k: (i, 0))], _plain, [(tw, W)])

    tr = _pick(S, 256, SUBLANE)
    rw = ((tr, W), lambda i: (i, 0))
    vec_w = ((1, W), F32, (1, W), lambda i: (0, 0))
    d_b_glu, d_ssm_d = _ew(
        "ssm_colsums", (S // tr,), [(dq,) + rw, (dy0,) + rw, (proj, (None, tr, W), lambda i: (0, i, 0))],
        [vec_w, vec_w],
        lambda dqv, dyv, vv: (jnp.sum(dqv.astype(F32), axis=0, keepdims=True), jnp.sum(dyv * vv, axis=0, keepdims=True)),
        acc=(0, 1))

    ds_re3, ds_im3 = _ssm_spread("ssm_ds", dy0_b, cd_re_b, cd_imn_b, NT, rows)
    lam_re3, lam_im3, da_re, da_im = _scan_bwd(ds_re3, ds_im3, s_re3, s_im3, a_re, a_im)
    dv, lam_re_b, lam_im_b = _ssm_collect("ssm_dv", lam_re3, lam_im3, bd_re_b, bd_im_b, NT, dy0, row(ssm_d),
                                          lambda acc, dy_tile, d_tile: [acc + dy_tile * d_tile], [BF])
    tiles, tch, tst = bd_re.shape
    tok_ch = lambda arr: (arr, (ts, tch), lambda b, i, j, k: (k, b))
    tok_st = lambda arr: (arr, (ts, tst), lambda b, i, j, k: (k, b))
    bd_out = ((tiles, tch, tst), F32, (None, tch, tst), lambda b, i, j, k: (b, 0, 0))
    cd_out = ((tiles, tst, tch), F32, (None, tst, tch), lambda b, i, j, k: (b, 0, 0))
    dbd_re, dbd_im = _mm("ssm_dbd", (tiles, 1, 1, S // ts), [tok_ch(v_bf), tok_st(lam_re_b), tok_st(lam_im_b)],
                         [(0, 1, TN, 0), (0, 2, TN, 1)], [], [bd_out, bd_out], _plain, [(tch, tst)] * 2)
    dcd_re, dcd_imn = _mm("ssm_dcd", (tiles, 1, 1, S // ts), [tok_st(s_re_b), tok_st(s_im_b), tok_ch(dy0_b)],
                          [(0, 2, TN, 0), (1, 2, TN, 1)], [], [cd_out, cd_out], _plain, [(tst, tch)] * 2)
    d_ssm = ssm_vjp((da_re, da_im, dbd_re, dbd_im, dcd_re, dcd_imn))

    dbg, dcg, dval, d_conv_w_full, d_conv_b = _conv_bwd(dyb, proj, conv, cw)
    dproj = jnp.concatenate([dv[None], dbg[None], dcg[None], dval[None], dga, dgb], axis=0)

    (dw_in,) = _mm("in_proj_dw", (NDEV, D // td, 1, S // ts),
                   [(u2, (ts, td), lambda b, i, j, k: (k, i)), (dproj, (None, ts, W), lambda b, i, j, k: (b, k, 0))],
                   [(0, 1, TN, 0)], [], [((NDEV, D, W), BF, (None, td, W), lambda b, i, j, k: (b, i, 0))],
                   _plain, [(td, W)])
    sib_mixer = rs_sibling_start(
        "mixer", [dw_in, dw_glu.reshape(NDEV, W // NDEV, W), dw_so, dw_co, dw_o.reshape(NDEV, Dc, D)])
    (du2,) = _mm("in_proj_dx", (1, S // tm, D // tn, NDEV // K_SHARDS),
                 [(dproj, (K_SHARDS, tm, W), lambda b, i, j, k: (k, i, 0)),
                  (w_in_f, (K_SHARDS, tn, W), lambda b, i, j, k: (k, j, 0))],
                 [(0, 1, NT, 0)], [], [((S, D), F32, (tm, tn), lambda b, i, j, k: (i, j))], _plain, [(tm, tn)],
                 deps=[sib_mixer[4]])
    dh1, dh1_half, d_mix_norm = _rms_bwd("rms2_bwd", du2, h1, r2, row(mix_norm), dh2, 0.5)
    rs_mixer = rs_chips_start("mixer", sib_mixer, dh1)

    small = dict(mix_norm=d_mix_norm, ffn2_norm=d_ffn2_norm, final_norm=d_final_norm,
                 ssm_lambda_re=d_ssm[0], ssm_lambda_im=d_ssm[1], ssm_log_dt=d_ssm[2], ssm_b_re=d_ssm[3],
                 ssm_b_im=d_ssm[4], ssm_c_re=d_ssm[5], ssm_c_im=d_ssm[6], ssm_d=d_ssm_d, ssm_b_glu=d_b_glu,
                 conv_b=d_conv_b)
    early_pk = pack([small[n] for n in replicated[:-1]] + [d_conv_w_full])
    early_land = lax.dynamic_update_slice(jnp.zeros((NDEV,) + early_pk.shape, F32), early_pk[None], (me, 0, 0))
    small_begun = _split_start("gather_small_start", [early_pk], [early_land], _everyone_copies, NDEV - 1)

    f1_dwd, f1_dwgu, f1_du = _ffn_bwd("ffn1", dh1_half, u1, ffn1_saved, wg1, wu1, wd1,
                                      deps=[rs_mixer[4], small_begun[4]])
    du1 = f1_du()
    dx, _, d_ffn1_norm = _rms_bwd("rms1_bwd", du1, xh, r1, row(ffn1_norm), dh1, 1.0)
    dwd1 = f1_dwd(deps=[d_ffn1_norm])
    late = d_ffn1_norm + 0.0 * dwd1[0, :1, :1].astype(F32)
    late_pk = late.reshape(-1, LANE)
    late_land = lax.dynamic_update_slice(jnp.zeros((NDEV,) + late_pk.shape, F32), late_pk[None], (me, 0, 0))
    late_begun = _split_start("gather_late_start", [late_pk], [late_land], _everyone_copies, NDEV - 1)
    rs_ffn1_down = rs_chips_start("ffn1_down", rs_sibling_start("ffn1_down", [dwd1]), late_begun[4])
    dwg1, dwu1 = f1_dwgu(deps=[rs_ffn1_down[4]])
    rs_ffn1_gate = rs_chips_start("ffn1_gate", rs_sibling_start("ffn1_gate", [dwg1]), None)
    rs_ffn1_up = rs_chips_start("ffn1_up", rs_sibling_start("ffn1_up", [dwu1]), rs_ffn1_gate[4])
    rs_end("ffn2", sharded[8:11], rs_ffn2, rs_ffn1_up[4])
    rs_end("mixer", sharded[3:8], rs_mixer, [grads[n] for n in sharded[8:11]])
    _, (early_all,) = _split_wait("gather_small_wait", small_begun, _everyone_copies, [grads[n] for n in sharded[3:8]])
    _, (late_all,) = _split_wait("gather_late_wait", late_begun, _everyone_copies, early_all)
    small_all = jnp.concatenate([early_all, late_all], axis=1)
    g_pk, d_pk, m_pk, v_pk = _finish_replicated("adamw_replicated", small_all, *packed_state)

    def unpack(pk, r0, like):
        nr = -(-like.size // tile) * SUBLANE
        return pk[r0:r0 + nr].reshape(-1)[:like.size].reshape(like.shape), r0 + nr

    r0 = 0
    for n in replicated[:-1] + ["conv_w", "ffn1_norm"]:
        like = d_conv_w_full if n == "conv_w" else P[n]
        for store, pk in ((grads, g_pk), (deltas, d_pk), (new_m, m_pk), (new_v, v_pk)):
            store[n], r1 = unpack(pk, r0, like)
        r0 = r1
    g_cw_full = grads["conv_w"]
    cwl = conv_w.shape[1]
    g_cw = lax.dynamic_slice_in_dim(g_cw_full, me * cwl, cwl, axis=1)
    full3 = ((3, cwl), lambda i: (0, 0))
    grads["conv_w"], deltas["conv_w"], new_m["conv_w"], new_v["conv_w"] = _ew(
        "adamw_conv_w", (1,), [(g_cw,) + full3, (conv_w,) + full3, (m_conv_w,) + full3, (v_conv_w,) + full3],
        [((3, cwl), F32) + full3] * 4, lambda g, w, m, v: (g,) + _adamw(w, g, m, v))
    rs_end("ffn1_down", sharded[2:3], rs_ffn1_down, [g_pk, grads["conv_w"]])
    rs_end("ffn1_gate", sharded[0:1], rs_ffn1_gate, grads["ffn1_w_down"])
    rs_end("ffn1_up", sharded[1:2], rs_ffn1_up, grads["ffn1_w_gate"])

    return (loss, dx.reshape(x.shape), *[grads[n] for n in names], *[deltas[n] for n in names],
            *[new_m[n] for n in names], *[new_v[n] for n in names])
```
